```python
import math
import jax, jax.numpy as jnp
from jax import lax
import numpy as np


D_MODEL = 1024
BATCH = 8
SEQ = 4096
DEPTH = 1

HEAD_DIM = 64
ATTN_WIDTH = D_MODEL // 2
N_ATTN_HEADS = ATTN_WIDTH // HEAD_DIM
REC_WIDTH = D_MODEL - ATTN_WIDTH
REC_BLOCKS = 8
REC_BLOCK = REC_WIDTH // REC_BLOCKS
MIX_WIDTH = ATTN_WIDTH + REC_WIDTH
IN_WIDTH = 3 * ATTN_WIDTH + 2 * REC_WIDTH
REC_CONV = 4
LRU_C = 8.0
D_FF = 3 * D_MODEL
FFN_CONV = 3
WINDOW_DILATIONS = ((128, 1), (512, 4), (2048, 16))
BLOCK = 128
ROPE_THETA = 10000.0
EPS = 1e-6
NEG_INF = -1e30

kernel_name = "hybrid_dilated_attn_rglru_convffn"


def rms_norm(x, g):
    xf = x.astype(jnp.float32)
    y = xf * lax.rsqrt(jnp.mean(xf * xf, axis=-1, keepdims=True) + EPS)
    return (y * g.astype(jnp.float32)).astype(x.dtype)


def rotary(x, positions):
    half = HEAD_DIM // 2
    inv_freq = ROPE_THETA ** (-jnp.arange(half, dtype=jnp.float32) / half)
    ang = positions.astype(jnp.float32)[..., None] * inv_freq
    cos = jnp.cos(ang)[:, :, None, :]
    sin = jnp.sin(ang)[:, :, None, :]
    xf = x.astype(jnp.float32)
    x1, x2 = xf[..., :half], xf[..., half:]
    return jnp.concatenate([x1 * cos - x2 * sin, x2 * cos + x1 * sin], axis=-1).astype(x.dtype)


def causal_depthwise_conv(x, w, b):
    k_width = w.shape[0]
    s = x.shape[1]
    xp = jnp.pad(x, ((0, 0), (k_width - 1, 0), (0, 0)))
    y = b
    for k in range(k_width):
        y = y + xp[:, k:k + s, :] * w[k]
    return y


def dilated_window_branch(q, k, v, window, dilation):
    bsz, s, h, d = q.shape
    length = s // dilation
    span = window // dilation
    assert span <= BLOCK
    nb = -(-length // BLOCK)
    lp = nb * BLOCK

    def regroup(t):
        return t.reshape(bsz, length, dilation, h, d).transpose(0, 2, 3, 1, 4)

    qs = jnp.pad(regroup(q), ((0, 0), (0, 0), (0, 0), (0, lp - length), (0, 0)))
    ks = jnp.pad(regroup(k), ((0, 0), (0, 0), (0, 0), (BLOCK, lp - length), (0, 0)))
    vs = jnp.pad(regroup(v), ((0, 0), (0, 0), (0, 0), (BLOCK, lp - length), (0, 0)))
    qb = qs.reshape(bsz, dilation, h, nb, BLOCK, d)
    kb = ks.reshape(bsz, dilation, h, nb + 1, BLOCK, d)
    vb = vs.reshape(bsz, dilation, h, nb + 1, BLOCK, d)
    kwin = jnp.concatenate([kb[:, :, :, :-1], kb[:, :, :, 1:]], axis=4)
    vwin = jnp.concatenate([vb[:, :, :, :-1], vb[:, :, :, 1:]], axis=4)

    scores = jnp.einsum('bchnqd,bchnkd->bchnqk', qb, kwin).astype(jnp.float32)
    qi = jnp.arange(BLOCK)[:, None]
    kj = jnp.arange(2 * BLOCK)[None, :]
    rel = qi - kj + BLOCK
    band = (rel >= 0) & (rel <= span)
    blk = jnp.arange(nb)[:, None, None]
    key_ok = (blk * BLOCK + kj[None] - BLOCK) >= 0
    mask = band[None] & key_ok
    scores = jnp.where(mask, scores, NEG_INF)
    m = jnp.max(scores, axis=-1, keepdims=True)
    p = jnp.exp(scores - m)
    l = jnp.sum(p, axis=-1, keepdims=True)
    o = jnp.einsum('bchnqk,bchnkd->bchnqd', p, vwin.astype(jnp.float32)) / l
    lse = (m + jnp.log(l))[..., 0]

    o = o.reshape(bsz, dilation, h, lp, d)[:, :, :, :length]
    lse = lse.reshape(bsz, dilation, h, lp)[:, :, :, :length]
    o = o.transpose(0, 3, 1, 2, 4).reshape(bsz, s, h, d)
    lse = lse.transpose(0, 3, 1, 2).reshape(bsz, s, h)
    return o, lse


def dilated_attention(q, k, v):
    outs, lses = [], []
    for window, dilation in WINDOW_DILATIONS:
        o, lse = dilated_window_branch(q, k, v, window, dilation)
        outs.append(o)
        lses.append(lse)
    wts = jax.nn.softmax(jnp.stack(lses, axis=0), axis=0)
    return jnp.einsum('gbsh,gbshd->bshd', wts, jnp.stack(outs, axis=0))


def lru_combine(left, right):
    a_l, b_l = left
    a_r, b_r = right
    return a_l * a_r, a_r * b_l + b_r


def rg_lru(xr, w_rg, b_rg, w_ig, b_ig, lru_lambda):
    bsz, s, _ = xr.shape
    xb = xr.reshape(bsz, s, REC_BLOCKS, REC_BLOCK)
    r = jax.nn.sigmoid(jnp.einsum('bsnc,ncd->bsnd', xb, w_rg) + b_rg).reshape(bsz, s, REC_WIDTH)
    i = jax.nn.sigmoid(jnp.einsum('bsnc,ncd->bsnd', xb, w_ig) + b_ig).reshape(bsz, s, REC_WIDTH)
    r = r.astype(jnp.float32)
    i = i.astype(jnp.float32)
    log_a = -LRU_C * r * jax.nn.softplus(-lru_lambda.astype(jnp.float32))
    a = jnp.exp(log_a)
    mult = jnp.sqrt(-jnp.expm1(2.0 * log_a))
    u = mult * (i * xr.astype(jnp.float32))
    _, hseq = lax.associative_scan(lru_combine, (a, u), axis=1)
    return hseq.astype(xr.dtype)


def _fwd_setup_inputs(seed: int = 0) -> dict:
    key = jax.random.key(seed)
    ks = jax.random.split(key, 24)
    f32 = jnp.float32

    def nrm(k, shape, scale):
        return jax.random.normal(k, shape, f32) * scale

    def gain(k, shape):
        return 1.0 + 0.01 * jax.random.normal(k, shape, f32)

    x = jax.random.normal(ks[0], (BATCH, SEQ, D_MODEL), f32)
    positions = jnp.broadcast_to(jnp.arange(SEQ, dtype=jnp.int32)[None, :], (BATCH, SEQ))
    a_c = jax.random.uniform(ks[11], (DEPTH, REC_WIDTH), f32, 0.9, 0.999)
    sig = a_c ** (1.0 / LRU_C)
    lru_lambda = jnp.log(sig) - jnp.log1p(-sig)
    return {
        "x": x,
        "positions": positions,
        "g_mix": gain(ks[1], (DEPTH, D_MODEL)),
        "w_in": nrm(ks[2], (DEPTH, D_MODEL, IN_WIDTH), D_MODEL ** -0.5),
        "q_norm_g": gain(ks[3], (DEPTH, HEAD_DIM)),
        "k_norm_g": gain(ks[4], (DEPTH, HEAD_DIM)),
        "rec_conv_w": nrm(ks[5], (DEPTH, REC_CONV, REC_WIDTH), REC_CONV ** -0.5),
        "rec_conv_b": nrm(ks[6], (DEPTH, REC_WIDTH), 0.01),
        "w_rg": nrm(ks[7], (DEPTH, REC_BLOCKS, REC_BLOCK, REC_BLOCK), REC_BLOCK ** -0.5),
        "b_rg": nrm(ks[8], (DEPTH, REC_BLOCKS, REC_BLOCK), 0.01),
        "w_ig": nrm(ks[9], (DEPTH, REC_BLOCKS, REC_BLOCK, REC_BLOCK), REC_BLOCK ** -0.5),
        "b_ig": nrm(ks[10], (DEPTH, REC_BLOCKS, REC_BLOCK), 0.01),
        "lru_lambda": lru_lambda,
        "g_attn_out": gain(ks[12], (DEPTH, ATTN_WIDTH)),
        "g_rec_out": gain(ks[13], (DEPTH, REC_WIDTH)),
        "w_out": nrm(ks[14], (DEPTH, MIX_WIDTH, D_MODEL), MIX_WIDTH ** -0.5),
        "g_ffn": gain(ks[15], (DEPTH, D_MODEL)),
        "w_up": nrm(ks[16], (DEPTH, D_MODEL, 2 * D_FF), D_MODEL ** -0.5),
        "ffn_conv_w": nrm(ks[17], (DEPTH, FFN_CONV, 2 * D_FF), FFN_CONV ** -0.5),
        "ffn_conv_b": nrm(ks[18], (DEPTH, 2 * D_FF), 0.01),
        "w_down": nrm(ks[19], (DEPTH, D_FF, D_MODEL), D_FF ** -0.5),
    }


def _fwd_reference(x, positions, g_mix, w_in, q_norm_g, k_norm_g, rec_conv_w, rec_conv_b,
              w_rg, b_rg, w_ig, b_ig, lru_lambda, g_attn_out, g_rec_out, w_out,
              g_ffn, w_up, ffn_conv_w, ffn_conv_b, w_down):
    bsz, s, _ = x.shape
    for layer in range(DEPTH):
        h = rms_norm(x, g_mix[layer])
        proj = h @ w_in[layer]
        q, k, v, xr, gr = jnp.split(
            proj, [ATTN_WIDTH, 2 * ATTN_WIDTH, 3 * ATTN_WIDTH, 3 * ATTN_WIDTH + REC_WIDTH], axis=-1)
        q = q.reshape(bsz, s, N_ATTN_HEADS, HEAD_DIM)
        k = k.reshape(bsz, s, N_ATTN_HEADS, HEAD_DIM)
        v = v.reshape(bsz, s, N_ATTN_HEADS, HEAD_DIM)
        q = rotary(rms_norm(q, q_norm_g[layer]), positions) * (HEAD_DIM ** -0.5)
        k = rotary(rms_norm(k, k_norm_g[layer]), positions)
        attn = dilated_attention(q, k, v).astype(x.dtype).reshape(bsz, s, ATTN_WIDTH)
        attn = rms_norm(attn, g_attn_out[layer])

        xr = causal_depthwise_conv(xr, rec_conv_w[layer], rec_conv_b[layer])
        rec = rg_lru(xr, w_rg[layer], b_rg[layer], w_ig[layer], b_ig[layer], lru_lambda[layer])
        rec = rms_norm(rec * jax.nn.gelu(gr), g_rec_out[layer])

        x = x + jnp.concatenate([attn, rec], axis=-1) @ w_out[layer]

        h = rms_norm(x, g_ffn[layer])
        u = causal_depthwise_conv(h @ w_up[layer], ffn_conv_w[layer], ffn_conv_b[layer])
        gate, up = jnp.split(u, 2, axis=-1)
        x = x + (jax.nn.gelu(gate) * up) @ w_down[layer]
    return x


import jax as _jax
import jax.numpy as _jnp

TWIN_FORMAT = 'train_step'
FWD_PARAMS = ['x', 'positions', 'g_mix', 'w_in', 'q_norm_g', 'k_norm_g', 'rec_conv_w', 'rec_conv_b', 'w_rg', 'b_rg', 'w_ig', 'b_ig', 'lru_lambda', 'g_attn_out', 'g_rec_out', 'w_out', 'g_ffn', 'w_up', 'ffn_conv_w', 'ffn_conv_b', 'w_down']
TWIN_WEIGHTS = ['g_mix', 'w_in', 'q_norm_g', 'k_norm_g', 'rec_conv_w', 'rec_conv_b', 'w_rg', 'b_rg', 'w_ig', 'b_ig', 'lru_lambda', 'g_attn_out', 'g_rec_out', 'w_out', 'g_ffn', 'w_up', 'ffn_conv_w', 'ffn_conv_b', 'w_down']
TWIN_DIFF_INPUT = 'x'
TWIN_INPUTS = ['x', 'positions', 'g_mix', 'w_in', 'q_norm_g', 'k_norm_g', 'rec_conv_w', 'rec_conv_b', 'w_rg', 'b_rg', 'w_ig', 'b_ig', 'lru_lambda', 'g_attn_out', 'g_rec_out', 'w_out', 'g_ffn', 'w_up', 'ffn_conv_w', 'ffn_conv_b', 'w_down', 'loss_target', 'm_g_mix', 'm_w_in', 'm_q_norm_g', 'm_k_norm_g', 'm_rec_conv_w', 'm_rec_conv_b', 'm_w_rg', 'm_b_rg', 'm_w_ig', 'm_b_ig', 'm_lru_lambda', 'm_g_attn_out', 'm_g_rec_out', 'm_w_out', 'm_g_ffn', 'm_w_up', 'm_ffn_conv_w', 'm_ffn_conv_b', 'm_w_down', 'v_g_mix', 'v_w_in', 'v_q_norm_g', 'v_k_norm_g', 'v_rec_conv_w', 'v_rec_conv_b', 'v_w_rg', 'v_b_rg', 'v_w_ig', 'v_b_ig', 'v_lru_lambda', 'v_g_attn_out', 'v_g_rec_out', 'v_w_out', 'v_g_ffn', 'v_w_up', 'v_ffn_conv_w', 'v_ffn_conv_b', 'v_w_down']
TWIN_OUTPUTS = ['loss', 'grad_x', 'grad_g_mix', 'grad_w_in', 'grad_q_norm_g', 'grad_k_norm_g', 'grad_rec_conv_w', 'grad_rec_conv_b', 'grad_w_rg', 'grad_b_rg', 'grad_w_ig', 'grad_b_ig', 'grad_lru_lambda', 'grad_g_attn_out', 'grad_g_rec_out', 'grad_w_out', 'grad_g_ffn', 'grad_w_up', 'grad_ffn_conv_w', 'grad_ffn_conv_b', 'grad_w_down', 'delta_g_mix', 'delta_w_in', 'delta_q_norm_g', 'delta_k_norm_g', 'delta_rec_conv_w', 'delta_rec_conv_b', 'delta_w_rg', 'delta_b_rg', 'delta_w_ig', 'delta_b_ig', 'delta_lru_lambda', 'delta_g_attn_out', 'delta_g_rec_out', 'delta_w_out', 'delta_g_ffn', 'delta_w_up', 'delta_ffn_conv_w', 'delta_ffn_conv_b', 'delta_w_down', 'new_m_g_mix', 'new_m_w_in', 'new_m_q_norm_g', 'new_m_k_norm_g', 'new_m_rec_conv_w', 'new_m_rec_conv_b', 'new_m_w_rg', 'new_m_b_rg', 'new_m_w_ig', 'new_m_b_ig', 'new_m_lru_lambda', 'new_m_g_attn_out', 'new_m_g_rec_out', 'new_m_w_out', 'new_m_g_ffn', 'new_m_w_up', 'new_m_ffn_conv_w', 'new_m_ffn_conv_b', 'new_m_w_down', 'new_v_g_mix', 'new_v_w_in', 'new_v_q_norm_g', 'new_v_k_norm_g', 'new_v_rec_conv_w', 'new_v_rec_conv_b', 'new_v_w_rg', 'new_v_b_rg', 'new_v_w_ig', 'new_v_b_ig', 'new_v_lru_lambda', 'new_v_g_attn_out', 'new_v_g_rec_out', 'new_v_w_out', 'new_v_g_ffn', 'new_v_w_up', 'new_v_ffn_conv_w', 'new_v_ffn_conv_b', 'new_v_w_down']
TWIN_LEAF_KINDS = {'loss': 'loss', 'grad_x': 'grad_x', 'grad_g_mix': 'grad_w', 'grad_w_in': 'grad_w', 'grad_q_norm_g': 'grad_w', 'grad_k_norm_g': 'grad_w', 'grad_rec_conv_w': 'grad_w', 'grad_rec_conv_b': 'grad_w', 'grad_w_rg': 'grad_w', 'grad_b_rg': 'grad_w', 'grad_w_ig': 'grad_w', 'grad_b_ig': 'grad_w', 'grad_lru_lambda': 'grad_w', 'grad_g_attn_out': 'grad_w', 'grad_g_rec_out': 'grad_w', 'grad_w_out': 'grad_w', 'grad_g_ffn': 'grad_w', 'grad_w_up': 'grad_w', 'grad_ffn_conv_w': 'grad_w', 'grad_ffn_conv_b': 'grad_w', 'grad_w_down': 'grad_w', 'delta_g_mix': 'delta_w', 'delta_w_in': 'delta_w', 'delta_q_norm_g': 'delta_w', 'delta_k_norm_g': 'delta_w', 'delta_rec_conv_w': 'delta_w', 'delta_rec_conv_b': 'delta_w', 'delta_w_rg': 'delta_w', 'delta_b_rg': 'delta_w', 'delta_w_ig': 'delta_w', 'delta_b_ig': 'delta_w', 'delta_lru_lambda': 'delta_w', 'delta_g_attn_out': 'delta_w', 'delta_g_rec_out': 'delta_w', 'delta_w_out': 'delta_w', 'delta_g_ffn': 'delta_w', 'delta_w_up': 'delta_w', 'delta_ffn_conv_w': 'delta_w', 'delta_ffn_conv_b': 'delta_w', 'delta_w_down': 'delta_w', 'new_m_g_mix': 'new_m', 'new_m_w_in': 'new_m', 'new_m_q_norm_g': 'new_m', 'new_m_k_norm_g': 'new_m', 'new_m_rec_conv_w': 'new_m', 'new_m_rec_conv_b': 'new_m', 'new_m_w_rg': 'new_m', 'new_m_b_rg': 'new_m', 'new_m_w_ig': 'new_m', 'new_m_b_ig': 'new_m', 'new_m_lru_lambda': 'new_m', 'new_m_g_attn_out': 'new_m', 'new_m_g_rec_out': 'new_m', 'new_m_w_out': 'new_m', 'new_m_g_ffn': 'new_m', 'new_m_w_up': 'new_m', 'new_m_ffn_conv_w': 'new_m', 'new_m_ffn_conv_b': 'new_m', 'new_m_w_down': 'new_m', 'new_v_g_mix': 'new_v', 'new_v_w_in': 'new_v', 'new_v_q_norm_g': 'new_v', 'new_v_k_norm_g': 'new_v', 'new_v_rec_conv_w': 'new_v', 'new_v_rec_conv_b': 'new_v', 'new_v_w_rg': 'new_v', 'new_v_b_rg': 'new_v', 'new_v_w_ig': 'new_v', 'new_v_b_ig': 'new_v', 'new_v_lru_lambda': 'new_v', 'new_v_g_attn_out': 'new_v', 'new_v_g_rec_out': 'new_v', 'new_v_w_out': 'new_v', 'new_v_g_ffn': 'new_v', 'new_v_w_up': 'new_v', 'new_v_ffn_conv_w': 'new_v', 'new_v_ffn_conv_b': 'new_v', 'new_v_w_down': 'new_v'}


def _forward(args):
    return _fwd_reference(*[args[k] for k in FWD_PARAMS])


def _output_shape():
    def fwd():
        inp = _fwd_setup_inputs(0)
        return _fwd_reference(*[inp[k] for k in FWD_PARAMS])
    out = _jax.eval_shape(fwd)
    return out.shape, out.dtype

N_MICROBATCH = 1
ADAM_LR = 0.001
ADAM_B1 = 0.9
ADAM_B2 = 0.999
ADAM_EPS = 1e-08
ADAM_WD = 0.01
ADAM_STEP = 10
PER_EXAMPLE_BATCH_AXIS = {'x': 0, 'positions': 0, 'loss_target': 0}
SHARED_INPUTS = []
_WEIGHT_DTYPES = {'g_mix': _jnp.float32, 'w_in': _jnp.float32, 'q_norm_g': _jnp.float32, 'k_norm_g': _jnp.float32, 'rec_conv_w': _jnp.float32, 'rec_conv_b': _jnp.float32, 'w_rg': _jnp.float32, 'b_rg': _jnp.float32, 'w_ig': _jnp.float32, 'b_ig': _jnp.float32, 'lru_lambda': _jnp.float32, 'g_attn_out': _jnp.float32, 'g_rec_out': _jnp.float32, 'w_out': _jnp.float32, 'g_ffn': _jnp.float32, 'w_up': _jnp.float32, 'ffn_conv_w': _jnp.float32, 'ffn_conv_b': _jnp.float32, 'w_down': _jnp.float32}
MOMENT_SCALE = {'g_mix': 1.026504e+00, 'w_in': 6.555289e-01, 'q_norm_g': 1.307805e+00, 'k_norm_g': 1.379489e+00, 'rec_conv_w': 2.566679e+00, 'rec_conv_b': 3.216991e+01, 'w_rg': 1.191739e+00, 'b_rg': 6.493428e-01, 'w_ig': 2.164206e+00, 'b_ig': 6.849333e-01, 'lru_lambda': 1.009858e+00, 'g_attn_out': 3.203683e+01, 'g_rec_out': 5.149291e+01, 'w_out': 2.283817e+00, 'g_ffn': 2.873694e+01, 'w_up': 6.719015e-01, 'ffn_conv_w': 3.796383e+00, 'ffn_conv_b': 3.592460e+00, 'w_down': 5.249727e-01}


def _to_microbatches(a, axis):
    t = _jnp.moveaxis(a, axis, 0)
    t = t.reshape((N_MICROBATCH, t.shape[0] // N_MICROBATCH) + t.shape[1:])
    return _jnp.moveaxis(t, 1, axis + 1)


def setup_inputs(seed: int = 0) -> dict:
    inp = _fwd_setup_inputs(seed)
    key = _jax.random.fold_in(_jax.random.key(seed), 7919)
    shape, _ = _output_shape()
    out = dict(inp)
    out["loss_target"] = _jax.random.normal(_jax.random.fold_in(key, 0), shape, _jnp.float32)
    for i, name in enumerate(TWIN_WEIGHTS):
        w = inp[name].astype(_jnp.float32)
        if MOMENT_SCALE is None:
            s = _jnp.sqrt(_jnp.mean(_jnp.square(w)) + 1e-30)
        else:
            s = MOMENT_SCALE[name]
        km, kv = _jax.random.split(_jax.random.fold_in(key, i + 1))
        out[name] = w
        out["m_" + name] = s * _jax.random.normal(km, w.shape, _jnp.float32)
        out["v_" + name] = (s * s) * _jax.random.uniform(kv, w.shape, _jnp.float32, 0.5, 1.5)
    if N_MICROBATCH > 1:
        for name, axis in PER_EXAMPLE_BATCH_AXIS.items():
            out[name] = _to_microbatches(out[name], axis)
    return {'x': out['x'], 'positions': out['positions'], 'g_mix': out['g_mix'], 'w_in': out['w_in'], 'q_norm_g': out['q_norm_g'], 'k_norm_g': out['k_norm_g'], 'rec_conv_w': out['rec_conv_w'], 'rec_conv_b': out['rec_conv_b'], 'w_rg': out['w_rg'], 'b_rg': out['b_rg'], 'w_ig': out['w_ig'], 'b_ig': out['b_ig'], 'lru_lambda': out['lru_lambda'], 'g_attn_out': out['g_attn_out'], 'g_rec_out': out['g_rec_out'], 'w_out': out['w_out'], 'g_ffn': out['g_ffn'], 'w_up': out['w_up'], 'ffn_conv_w': out['ffn_conv_w'], 'ffn_conv_b': out['ffn_conv_b'], 'w_down': out['w_down'], 'loss_target': out['loss_target'], 'm_g_mix': out['m_g_mix'], 'm_w_in': out['m_w_in'], 'm_q_norm_g': out['m_q_norm_g'], 'm_k_norm_g': out['m_k_norm_g'], 'm_rec_conv_w': out['m_rec_conv_w'], 'm_rec_conv_b': out['m_rec_conv_b'], 'm_w_rg': out['m_w_rg'], 'm_b_rg': out['m_b_rg'], 'm_w_ig': out['m_w_ig'], 'm_b_ig': out['m_b_ig'], 'm_lru_lambda': out['m_lru_lambda'], 'm_g_attn_out': out['m_g_attn_out'], 'm_g_rec_out': out['m_g_rec_out'], 'm_w_out': out['m_w_out'], 'm_g_ffn': out['m_g_ffn'], 'm_w_up': out['m_w_up'], 'm_ffn_conv_w': out['m_ffn_conv_w'], 'm_ffn_conv_b': out['m_ffn_conv_b'], 'm_w_down': out['m_w_down'], 'v_g_mix': out['v_g_mix'], 'v_w_in': out['v_w_in'], 'v_q_norm_g': out['v_q_norm_g'], 'v_k_norm_g': out['v_k_norm_g'], 'v_rec_conv_w': out['v_rec_conv_w'], 'v_rec_conv_b': out['v_rec_conv_b'], 'v_w_rg': out['v_w_rg'], 'v_b_rg': out['v_b_rg'], 'v_w_ig': out['v_w_ig'], 'v_b_ig': out['v_b_ig'], 'v_lru_lambda': out['v_lru_lambda'], 'v_g_attn_out': out['v_g_attn_out'], 'v_g_rec_out': out['v_g_rec_out'], 'v_w_out': out['v_w_out'], 'v_g_ffn': out['v_g_ffn'], 'v_w_up': out['v_w_up'], 'v_ffn_conv_w': out['v_ffn_conv_w'], 'v_ffn_conv_b': out['v_ffn_conv_b'], 'v_w_down': out['v_w_down']}


def _loss(weights, diff, rest, loss_target):
    with _jax.named_scope("forward"):
        args = {**rest, TWIN_DIFF_INPUT: diff, **{k: w.astype(_WEIGHT_DTYPES[k]) for k, w in weights.items()}}
        y = _forward(args)
    with _jax.named_scope("loss_head"):
        err = _jnp.square(y.astype(_jnp.float32) - loss_target)
        return 0.5 * _jnp.sum(_jnp.mean(err, axis=-1)) if err.ndim else 0.5 * err


def _adamw(w, g, m, v):
    m = ADAM_B1 * m + (1.0 - ADAM_B1) * g
    v = ADAM_B2 * v + (1.0 - ADAM_B2) * _jnp.square(g)
    m_hat = m / (1.0 - ADAM_B1 ** ADAM_STEP)
    v_hat = v / (1.0 - ADAM_B2 ** ADAM_STEP)
    delta = -ADAM_LR * (m_hat / (_jnp.sqrt(v_hat) + ADAM_EPS) + ADAM_WD * w)
    return delta, m, v


def reference(x, positions, g_mix, w_in, q_norm_g, k_norm_g, rec_conv_w, rec_conv_b, w_rg, b_rg, w_ig, b_ig, lru_lambda, g_attn_out, g_rec_out, w_out, g_ffn, w_up, ffn_conv_w, ffn_conv_b, w_down, loss_target, m_g_mix, m_w_in, m_q_norm_g, m_k_norm_g, m_rec_conv_w, m_rec_conv_b, m_w_rg, m_b_rg, m_w_ig, m_b_ig, m_lru_lambda, m_g_attn_out, m_g_rec_out, m_w_out, m_g_ffn, m_w_up, m_ffn_conv_w, m_ffn_conv_b, m_w_down, v_g_mix, v_w_in, v_q_norm_g, v_k_norm_g, v_rec_conv_w, v_rec_conv_b, v_w_rg, v_b_rg, v_w_ig, v_b_ig, v_lru_lambda, v_g_attn_out, v_g_rec_out, v_w_out, v_g_ffn, v_w_up, v_ffn_conv_w, v_ffn_conv_b, v_w_down):
    given = dict(x=x, positions=positions, g_mix=g_mix, w_in=w_in, q_norm_g=q_norm_g, k_norm_g=k_norm_g, rec_conv_w=rec_conv_w, rec_conv_b=rec_conv_b, w_rg=w_rg, b_rg=b_rg, w_ig=w_ig, b_ig=b_ig, lru_lambda=lru_lambda, g_attn_out=g_attn_out, g_rec_out=g_rec_out, w_out=w_out, g_ffn=g_ffn, w_up=w_up, ffn_conv_w=ffn_conv_w, ffn_conv_b=ffn_conv_b, w_down=w_down, loss_target=loss_target, m_g_mix=m_g_mix, m_w_in=m_w_in, m_q_norm_g=m_q_norm_g, m_k_norm_g=m_k_norm_g, m_rec_conv_w=m_rec_conv_w, m_rec_conv_b=m_rec_conv_b, m_w_rg=m_w_rg, m_b_rg=m_b_rg, m_w_ig=m_w_ig, m_b_ig=m_b_ig, m_lru_lambda=m_lru_lambda, m_g_attn_out=m_g_attn_out, m_g_rec_out=m_g_rec_out, m_w_out=m_w_out, m_g_ffn=m_g_ffn, m_w_up=m_w_up, m_ffn_conv_w=m_ffn_conv_w, m_ffn_conv_b=m_ffn_conv_b, m_w_down=m_w_down, v_g_mix=v_g_mix, v_w_in=v_w_in, v_q_norm_g=v_q_norm_g, v_k_norm_g=v_k_norm_g, v_rec_conv_w=v_rec_conv_w, v_rec_conv_b=v_rec_conv_b, v_w_rg=v_w_rg, v_b_rg=v_b_rg, v_w_ig=v_w_ig, v_b_ig=v_b_ig, v_lru_lambda=v_lru_lambda, v_g_attn_out=v_g_attn_out, v_g_rec_out=v_g_rec_out, v_w_out=v_w_out, v_g_ffn=v_g_ffn, v_w_up=v_w_up, v_ffn_conv_w=v_ffn_conv_w, v_ffn_conv_b=v_ffn_conv_b, v_w_down=v_w_down)
    weights = {n: given[n] for n in TWIN_WEIGHTS}
    shared = {n: given[n] for n in SHARED_INPUTS}
    per_example = {n: given[n] for n in ['x', 'positions']}
    grad_fn = _jax.value_and_grad(_loss, argnums=(0, 1))

    def one_microbatch(ex, loss_target):
        ex = dict(ex)
        diff = ex.pop(TWIN_DIFF_INPUT)
        return grad_fn(weights, diff, {**shared, **ex}, loss_target)

    if N_MICROBATCH == 1:
        loss, (grad_w, grad_x) = one_microbatch(per_example, given["loss_target"])
    else:
        def body(carry, xs):
            loss_sum, grad_sum = carry
            l_k, (gw_k, gx_k) = one_microbatch(xs[0], xs[1])
            with _jax.named_scope("update"):
                return (loss_sum + l_k, _jax.tree.map(_jnp.add, grad_sum, gw_k)), gx_k

        init = (_jnp.zeros((), _jnp.float32), _jax.tree.map(_jnp.zeros_like, weights))
        (loss, grad_w), grad_x = _jax.lax.scan(body, init, (per_example, given["loss_target"]))
    with _jax.named_scope("update"):
        delta_w, new_m, new_v = {}, {}, {}
        for n in TWIN_WEIGHTS:
            delta_w[n], new_m[n], new_v[n] = _adamw(weights[n], grad_w[n], given["m_" + n], given["v_" + n])
    return (loss, grad_x, *[grad_w[n] for n in TWIN_WEIGHTS], *[delta_w[n] for n in TWIN_WEIGHTS],
            *[new_m[n] for n in TWIN_WEIGHTS], *[new_v[n] for n in TWIN_WEIGHTS])
```

```python
import functools
import math

import numpy as np
import jax
import jax.numpy as jnp
from jax import lax
from jax.experimental import pallas as pl
from jax.experimental.pallas import tpu as pltpu

F32 = jnp.float32
BF16 = jnp.bfloat16

D_MODEL = 1024
HEAD_DIM = 64
ATTN_W = 512
REC_W = 512
N_HEADS = 8
D_FF = 3072
IN_W = 2560
REC_CONV = 4
FFN_CONV = 3
LRU_C = 8.0
ROPE_THETA = 10000.0
EPS = 1e-6
NEG_INF = -1e30
QBLK = 128
DILATIONS = (1, 4, 16)
N_DEV = 8
SHARD_ROWS = 1600
SMALL_ROWS = 96
ADAM_LR, ADAM_B1, ADAM_B2, ADAM_EPS, ADAM_WD, ADAM_STEP = 0.001, 0.9, 0.999, 1e-08, 0.01, 10
MESH = pl.DeviceIdType.MESH
ANY = pl.BlockSpec(memory_space=pl.ANY)


def _call(body, *, name, **kw):
    return pl.pallas_call(body, name=name, **kw)


def _params(*sem):
    return pltpu.CompilerParams(dimension_semantics=sem, vmem_limit_bytes=56 * 1024 * 1024)


def _gelu(x):
    c = math.sqrt(2.0 / math.pi)
    return 0.5 * x * (1.0 + jnp.tanh(c * (x + 0.044715 * (x * x * x))))


def _gelu_and_grad(x):
    c = math.sqrt(2.0 / math.pi)
    t = jnp.tanh(c * (x + 0.044715 * (x * x * x)))
    g = 0.5 * x * (1.0 + t)
    dg = 0.5 * (1.0 + t) + 0.5 * x * (1.0 - t * t) * (c * (1.0 + 3.0 * 0.044715 * (x * x)))
    return g, dg


def _sigmoid(x):
    return 1.0 / (1.0 + jnp.exp(-x))


def _softplus_neg(lam):
    y = jnp.exp(-jnp.abs(lam))
    u = 1.0 + y
    log1p = jnp.where(u == 1.0, y, jnp.log(u) * y / jnp.where(u == 1.0, 1.0, u - 1.0))
    return jnp.maximum(-lam, 0.0) + log1p


def _dot(a, b, dims=(((1,), (0,)), ((), ()))):
    return lax.dot_general(a, b, dims, preferred_element_type=F32)


_NT = (((1,), (1,)), ((), ()))
_TN = (((0,), (0,)), ((), ()))


def _group_mean(v, bd):
    hi = v.astype(BF16)
    lo = (v - hi.astype(F32)).astype(BF16)
    return _dot(hi, bd) + _dot(lo, bd)


def _shift_down(x, halo, s):
    rolled = pltpu.roll(x, s, 0)
    hr = pltpu.roll(halo, s, 0)
    row = lax.broadcasted_iota(jnp.int32, hr.shape, 0)
    first = jnp.where(row < s, hr, rolled[:8])
    return jnp.concatenate([first, rolled[8:]], axis=0)


def _shift_up(x, halo, s):
    n = x.shape[0]
    rolled = pltpu.roll(x, n - s, 0)
    hr = pltpu.roll(halo, 8 - s, 0)
    row = lax.broadcasted_iota(jnp.int32, hr.shape, 0)
    last = jnp.where(row >= 8 - s, hr, rolled[n - 8:])
    return jnp.concatenate([rolled[:n - 8], last], axis=0)


def _scan_fwd(a, u):
    n = a.shape[0]
    row = lax.broadcasted_iota(jnp.int32, a.shape, 0)
    s = 1
    while s < n:
        a_s = jnp.where(row < s, 1.0, pltpu.roll(a, s, 0))
        u_s = jnp.where(row < s, 0.0, pltpu.roll(u, s, 0))
        u = u + a * u_s
        a = a * a_s
        s *= 2
    return a, u


def _scan_bwd(b, v):
    n = b.shape[0]
    row = lax.broadcasted_iota(jnp.int32, b.shape, 0)
    s = 1
    while s < n:
        b_s = jnp.where(row >= n - s, 1.0, pltpu.roll(b, n - s, 0))
        v_s = jnp.where(row >= n - s, 0.0, pltpu.roll(v, n - s, 0))
        v = v + b * v_s
        b = b * b_s
        s *= 2
    return b, v


def _rot_half(y):
    n = y.shape[1]
    lane = lax.broadcasted_iota(jnp.int32, y.shape, 1) & (HEAD_DIM - 1)
    return jnp.where(lane < HEAD_DIM // 2, -pltpu.roll(y, n - HEAD_DIM // 2, 1), pltpu.roll(y, HEAD_DIM // 2, 1))


def _all_gather(shard, name):
    m, n = shard.shape

    def body(x_ref, out_ref, send_sems, recv_sems, local_sem):
        x, y, c = lax.axis_index("x"), lax.axis_index("y"), lax.axis_index("c")
        me, sibling = (x, y, c), (x, y, 1 - c)
        chips = [(1 - x, y), (x, 1 - y), (1 - x, 1 - y)]

        def rows(px, py, pc):
            return out_ref.at[pl.ds((4 * px + 2 * py + pc) * m, m), :]

        def copy(k, block, to, src=None):
            return pltpu.make_async_remote_copy(
                src_ref=rows(*block) if src is None else src, dst_ref=rows(*block),
                send_sem=send_sems.at[k], recv_sem=recv_sems.at[k], device_id=to, device_id_type=MESH)

        mine = pltpu.make_async_copy(x_ref, rows(*me), local_sem)
        mine.start()
        first = [copy(0, me, sibling, src=x_ref)]
        first += [copy(1 + j, me, (*chip, c), src=x_ref) for j, chip in enumerate(chips)]
        for cp in first:
            cp.start()
        passed = [copy(4 + j, (*chip, c), sibling) for j, chip in enumerate(chips)]
        for j, chip in enumerate(chips):
            copy(1 + j, (*chip, c), me).wait_recv()
            passed[j].start()
        copy(0, sibling, me).wait_recv()
        for j, chip in enumerate(chips):
            copy(4 + j, (*chip, 1 - c), me).wait_recv()
        for cp in first + passed:
            cp.wait_send()
        mine.wait()

    return _call(
        body, name=name, out_shape=jax.ShapeDtypeStruct((N_DEV * m, n), shard.dtype),
        in_specs=[ANY], out_specs=ANY,
        scratch_shapes=[pltpu.SemaphoreType.DMA((7,)), pltpu.SemaphoreType.DMA((7,)), pltpu.SemaphoreType.DMA],
    )(shard)


def _exchange_sibling(g, name):
    _, r, n = g.shape

    def body(g_ref, out_ref, send_sems, recv_sems):
        x, y, c = lax.axis_index("x"), lax.axis_index("y"), lax.axis_index("c")
        copies = [
            pltpu.make_async_remote_copy(
                src_ref=g_ref.at[2 * k + (1 - c)], dst_ref=out_ref.at[k],
                send_sem=send_sems.at[k], recv_sem=recv_sems.at[k], device_id=(x, y, 1 - c), device_id_type=MESH)
            for k in range(4)]
        for cp in copies:
            cp.start()
        for cp in copies:
            cp.wait()

    return _call(
        body, name=name, out_shape=jax.ShapeDtypeStruct((4, r, n), g.dtype), in_specs=[ANY], out_specs=ANY,
        scratch_shapes=[pltpu.SemaphoreType.DMA((4,)), pltpu.SemaphoreType.DMA((4,))],
    )(g)


def _exchange_chips(p, name):
    _, r, n = p.shape

    def body(p_ref, out_ref, send_sems, recv_sems):
        x, y, c = lax.axis_index("x"), lax.axis_index("y"), lax.axis_index("c")
        chips = [(1 - x, y), (x, 1 - y), (1 - x, 1 - y)]
        copies = [
            pltpu.make_async_remote_copy(
                src_ref=p_ref.at[2 * cx + cy], dst_ref=out_ref.at[k],
                send_sem=send_sems.at[k], recv_sem=recv_sems.at[k], device_id=(cx, cy, c), device_id_type=MESH)
            for k, (cx, cy) in enumerate(chips)]
        for cp in copies:
            cp.start()
        for cp in copies:
            cp.wait()

    return _call(
        body, name=name, out_shape=jax.ShapeDtypeStruct((3, r, n), p.dtype), in_specs=[ANY], out_specs=ANY,
        scratch_shapes=[pltpu.SemaphoreType.DMA((3,)), pltpu.SemaphoreType.DMA((3,))],
    )(p)


def _mm(a, b, mode, out_dtype, name, add=None, tm=1024, tn=1024, tk=1024):
    if mode == "nn":
        (M, K), (_, N) = a.shape, b.shape
    elif mode == "nt":
        (M, K), (N, _) = a.shape, b.shape
    else:
        (K, M), (_, N) = a.shape, b.shape
    tm, tn, tk = min(tm, M), min(tn, N), min(tk, K)
    assert M % tm == 0 and N % tn == 0 and K % tk == 0, (name, M, N, K)
    nk = K // tk
    if mode == "nn":
        a_spec, b_spec, dims = pl.BlockSpec((tm, tk), lambda i, j, k: (i, k)), pl.BlockSpec((tk, tn), lambda i, j, k: (k, j)), (((1,), (0,)), ((), ()))
    elif mode == "nt":
        a_spec, b_spec, dims = pl.BlockSpec((tm, tk), lambda i, j, k: (i, k)), pl.BlockSpec((tn, tk), lambda i, j, k: (j, k)), _NT
    else:
        a_spec, b_spec, dims = pl.BlockSpec((tk, tm), lambda i, j, k: (k, i)), pl.BlockSpec((tk, tn), lambda i, j, k: (k, j)), _TN
    o_spec = pl.BlockSpec((tm, tn), lambda i, j, k: (i, j))
    has_add = add is not None

    def body(*refs):
        if has_add:
            a_ref, b_ref, add_ref, o_ref, acc = refs
        else:
            a_ref, b_ref, o_ref, acc = refs
        k = pl.program_id(2)

        @pl.when(k == 0)
        def _():
            acc[...] = jnp.zeros_like(acc)

        acc[...] += _dot(a_ref[...], b_ref[...], dims)

        @pl.when(k == nk - 1)
        def _():
            r = acc[...]
            if has_add:
                r = r + add_ref[...]
            o_ref[...] = r.astype(out_dtype)

    ins = [a, b] + ([add] if has_add else [])
    specs = [a_spec, b_spec] + ([o_spec] if has_add else [])
    return _call(
        body, name=name, grid=(M // tm, N // tn, nk), in_specs=specs, out_specs=o_spec,
        out_shape=jax.ShapeDtypeStruct((M, N), out_dtype), scratch_shapes=[pltpu.VMEM((tm, tn), F32)],
        compiler_params=_params("parallel", "parallel", "arbitrary"),
    )(*ins)


def _rmsnorm(x, g, name, tm=512):
    T, D = x.shape

    def body(x_ref, g_ref, o_ref):
        xv = x_ref[...]
        r = lax.rsqrt(jnp.mean(xv * xv, axis=-1, keepdims=True) + EPS)
        o_ref[...] = (xv * r * g_ref[...]).astype(BF16)

    return _call(
        body, name=name, grid=(T // tm,),
        in_specs=[pl.BlockSpec((tm, D), lambda i: (i, 0)), pl.BlockSpec((1, D), lambda i: (0, 0))],
        out_specs=pl.BlockSpec((tm, D), lambda i: (i, 0)), out_shape=jax.ShapeDtypeStruct((T, D), BF16),
        compiler_params=_params("parallel"),
    )(x, g)


def _rmsnorm_bwd(x, dh, resid, g, name, tm=512):
    T, D = x.shape

    def body(x_ref, dh_ref, res_ref, g_ref, dx_ref, dxb_ref, dg_ref):
        @pl.when(pl.program_id(0) == 0)
        def _():
            dg_ref[...] = jnp.zeros_like(dg_ref)

        xv, dhv = x_ref[...], dh_ref[...]
        r = lax.rsqrt(jnp.mean(xv * xv, axis=-1, keepdims=True) + EPS)
        gd = dhv * g_ref[...]
        m = jnp.mean(gd * xv, axis=-1, keepdims=True)
        dx = res_ref[...] + r * gd - xv * (r * r * r) * m
        dx_ref[...] = dx
        dxb_ref[...] = dx.astype(BF16)
        dg_ref[...] += jnp.sum(dhv * xv * r, axis=0, keepdims=True)

    row = pl.BlockSpec((tm, D), lambda i: (i, 0))
    vec = pl.BlockSpec((1, D), lambda i: (0, 0))
    return _call(
        body, name=name, grid=(T // tm,), in_specs=[row, row, row, vec], out_specs=[row, row, vec],
        out_shape=[jax.ShapeDtypeStruct((T, D), F32), jax.ShapeDtypeStruct((T, D), BF16), jax.ShapeDtypeStruct((1, D), F32)],
        compiler_params=_params("arbitrary"),
    )(x, dh, resid, g)


def _loss_grad(y, target, name, tm=512):
    T, D = y.shape

    def body(y_ref, t_ref, dy_ref, dyb_ref, l_ref):
        e = y_ref[...] - t_ref[...]
        dy = e * (1.0 / D)
        dy_ref[...] = dy
        dyb_ref[...] = dy.astype(BF16)
        l_ref[...] = jnp.sum(e * e, axis=0, keepdims=True)[None]

    row = pl.BlockSpec((tm, D), lambda i: (i, 0))
    return _call(
        body, name=name, grid=(T // tm,), in_specs=[row, row],
        out_specs=[row, row, pl.BlockSpec((1, 1, D), lambda i: (i, 0, 0))],
        out_shape=[jax.ShapeDtypeStruct((T, D), F32), jax.ShapeDtypeStruct((T, D), BF16),
                   jax.ShapeDtypeStruct((T // tm, 1, D), F32)],
        compiler_params=_params("parallel"),
    )(y, target)


def _qk_prep(proj, pos, invf, qg, kg, bd, name, tm=512):
    T = proj.shape[0]

    def body(q_ref, k_ref, v_ref, pos_ref, invf_ref, qg_ref, kg_ref, bd_ref, qo_ref, ko_ref, vo_ref):
        ang = pos_ref[...].astype(F32) * invf_ref[...]
        cos, sin = jnp.cos(ang), jnp.sin(ang)

        def prep(xv, gv, scale):
            r = lax.rsqrt(_group_mean(xv * xv, bd_ref[...]) + EPS)
            yv = xv * r * gv
            return (yv * cos + _rot_half(yv) * sin) * scale

        qo_ref[...] = prep(q_ref[...], qg_ref[...], HEAD_DIM ** -0.5).astype(BF16)
        ko_ref[...] = prep(k_ref[...], kg_ref[...], 1.0).astype(BF16)
        vo_ref[...] = v_ref[...].astype(BF16)

    col = lambda j: pl.BlockSpec((tm, ATTN_W), lambda i, j=j: (i, j))
    vec = pl.BlockSpec((1, ATTN_W), lambda i: (0, 0))
    out = pl.BlockSpec((tm, ATTN_W), lambda i: (i, 0))
    return _call(
        body, name=name, grid=(T // tm,),
        in_specs=[col(0), col(1), col(2), pl.BlockSpec((tm, 1), lambda i: (i, 0)), vec, vec, vec,
                  pl.BlockSpec((ATTN_W, ATTN_W), lambda i: (0, 0))],
        out_specs=[out, out, out], out_shape=[jax.ShapeDtypeStruct((T, ATTN_W), BF16)] * 3,
        compiler_params=_params("parallel"),
    )(proj, proj, proj, pos, invf, qg, kg, bd)


def _qk_prep_bwd(proj, dqs, dks, dvs, pos, invf, qg, kg, bd, name, tm=512):
    T = proj.shape[0]

    def body(q_ref, k_ref, dq1, dq2, dq3, dk1, dk2, dk3, dv1, dv2, dv3, pos_ref, invf_ref, qg_ref, kg_ref, bd_ref,
             dq_ref, dk_ref, dv_ref, gq_ref, gk_ref):
        @pl.when(pl.program_id(0) == 0)
        def _():
            gq_ref[...] = jnp.zeros_like(gq_ref)
            gk_ref[...] = jnp.zeros_like(gk_ref)

        ang = pos_ref[...].astype(F32) * invf_ref[...]
        cos, sin = jnp.cos(ang), jnp.sin(ang)

        def back(xv, gv, dz, scale):
            dz = dz * scale
            dy = dz * cos - _rot_half(dz * sin)
            r = lax.rsqrt(_group_mean(xv * xv, bd_ref[...]) + EPS)
            gd = dy * gv
            m = _group_mean(gd * xv, bd_ref[...])
            dx = r * gd - xv * (r * r * r) * m
            return dx, jnp.sum(dy * xv * r, axis=0, keepdims=True)

        dx, gs = back(q_ref[...], qg_ref[...], dq1[...] + dq2[...] + dq3[...], HEAD_DIM ** -0.5)
        dq_ref[...] = dx.astype(BF16)
        gq_ref[...] += gs
        dx, gs = back(k_ref[...], kg_ref[...], dk1[...] + dk2[...] + dk3[...], 1.0)
        dk_ref[...] = dx.astype(BF16)
        gk_ref[...] += gs
        dv_ref[...] = (dv1[...] + dv2[...] + dv3[...]).astype(BF16)

    col = lambda j: pl.BlockSpec((tm, ATTN_W), lambda i, j=j: (i, j))
    row = pl.BlockSpec((tm, ATTN_W), lambda i: (i, 0))
    vec = pl.BlockSpec((1, ATTN_W), lambda i: (0, 0))
    return _call(
        body, name=name, grid=(T // tm,),
        in_specs=[col(0), col(1)] + [row] * 9 + [pl.BlockSpec((tm, 1), lambda i: (i, 0)), vec, vec, vec,
                                                  pl.BlockSpec((ATTN_W, ATTN_W), lambda i: (0, 0))],
        out_specs=[row, row, row, vec, vec],
        out_shape=[jax.ShapeDtypeStruct((T, ATTN_W), BF16)] * 3 + [jax.ShapeDtypeStruct((1, ATTN_W), F32)] * 2,
        compiler_params=_params("arbitrary"),
    )(proj, proj, *dqs, *dks, *dvs, pos, invf, qg, kg, bd)


def _attn_fwd(q, k, v, dil, name):
    T = q.shape[0]
    L = T // dil
    nb = L // QBLK
    view = lambda t: t.reshape(L, dil * ATTN_W)

    def body(q_ref, k_ref, v_ref, o_ref, lse_ref):
        qi = lax.broadcasted_iota(jnp.int32, (QBLK, QBLK), 0)
        kj = lax.broadcasted_iota(jnp.int32, (QBLK, QBLK), 1)
        head0 = lax.broadcasted_iota(jnp.int32, (QBLK, 2 * HEAD_DIM), 1) < HEAD_DIM

        def step(n, carry):
            r0 = pl.multiple_of(n * QBLK, QBLK)
            rp = pl.multiple_of(jnp.maximum(n - 1, 0) * QBLK, QBLK)
            qv = q_ref[pl.ds(r0, QBLK), :]
            k0, v0 = k_ref[pl.ds(rp, QBLK), :], v_ref[pl.ds(rp, QBLK), :]
            k1, v1 = k_ref[pl.ds(r0, QBLK), :], v_ref[pl.ds(r0, QBLK), :]
            ok0 = (kj >= qi) & (n > 0)
            ok1 = kj <= qi
            outs, lses = [], []
            for h in range(2):
                qh = jnp.where(head0 if h == 0 else ~head0, qv, jnp.zeros_like(qv))
                s0 = jnp.where(ok0, _dot(qh, k0, _NT), NEG_INF)
                s1 = jnp.where(ok1, _dot(qh, k1, _NT), NEG_INF)
                m = jnp.maximum(jnp.max(s0, axis=-1, keepdims=True), jnp.max(s1, axis=-1, keepdims=True))
                p0, p1 = jnp.exp(s0 - m), jnp.exp(s1 - m)
                l = jnp.sum(p0, axis=-1, keepdims=True) + jnp.sum(p1, axis=-1, keepdims=True)
                acc = _dot(p0.astype(BF16), v0) + _dot(p1.astype(BF16), v1)
                outs.append(acc / l)
                lses.append(m + jnp.log(l))
            o_ref[pl.ds(r0, QBLK), :] = jnp.where(head0, outs[0], outs[1])
            lse_ref[pl.ds(r0, QBLK), :] = jnp.where(head0, lses[0], lses[1])
            return carry

        lax.fori_loop(0, nb, step, 0)

    blk = pl.BlockSpec((L, 2 * HEAD_DIM), lambda c, hp: (0, c * 4 + hp))
    o, lse = _call(
        body, name=name, grid=(dil, 4), in_specs=[blk, blk, blk], out_specs=[blk, blk],
        out_shape=[jax.ShapeDtypeStruct((L, dil * ATTN_W), F32)] * 2,
        compiler_params=_params("parallel", "parallel"),
    )(view(q), view(k), view(v))
    return o.reshape(T, ATTN_W), lse.reshape(T, ATTN_W)


def _attn_bwd(q, k, v, do, lse, delta, dil, name):
    T = q.shape[0]
    L = T // dil
    nb = L // QBLK
    view = lambda t: t.reshape(L, dil * ATTN_W)

    def body(q_ref, k_ref, v_ref, do_ref, lse_ref, dl_ref, dq_ref, dk_ref, dv_ref):
        qi = lax.broadcasted_iota(jnp.int32, (QBLK, QBLK), 0)
        kj = lax.broadcasted_iota(jnp.int32, (QBLK, QBLK), 1)
        head0 = lax.broadcasted_iota(jnp.int32, (QBLK, 2 * HEAD_DIM), 1) < HEAD_DIM
        dk_ref[...] = jnp.zeros_like(dk_ref)
        dv_ref[...] = jnp.zeros_like(dv_ref)

        def step(n, carry):
            r0 = pl.multiple_of(n * QBLK, QBLK)
            rp = pl.multiple_of(jnp.maximum(n - 1, 0) * QBLK, QBLK)
            qv, dov = q_ref[pl.ds(r0, QBLK), :], do_ref[pl.ds(r0, QBLK), :]
            lsev, dlv = lse_ref[pl.ds(r0, QBLK), :], dl_ref[pl.ds(r0, QBLK), :]
            k0, v0 = k_ref[pl.ds(rp, QBLK), :], v_ref[pl.ds(rp, QBLK), :]
            k1, v1 = k_ref[pl.ds(r0, QBLK), :], v_ref[pl.ds(r0, QBLK), :]
            ok0 = (kj >= qi) & (n > 0)
            ok1 = kj <= qi
            dqs = []
            dk0 = dk1 = dv0 = dv1 = None
            for h in range(2):
                sel = head0 if h == 0 else ~head0
                qh = jnp.where(sel, qv, jnp.zeros_like(qv))
                doh = jnp.where(sel, dov, jnp.zeros_like(dov))
                lh = lsev[:, h * HEAD_DIM:h * HEAD_DIM + 1]
                dh = dlv[:, h * HEAD_DIM:h * HEAD_DIM + 1]
                p0 = jnp.where(ok0, jnp.exp(_dot(qh, k0, _NT) - lh), 0.0)
                p1 = jnp.where(ok1, jnp.exp(_dot(qh, k1, _NT) - lh), 0.0)
                ds0 = (p0 * (_dot(doh, v0, _NT) - dh)).astype(BF16)
                ds1 = (p1 * (_dot(doh, v1, _NT) - dh)).astype(BF16)
                dqs.append(_dot(ds0, k0) + _dot(ds1, k1))
                c0, c1 = _dot(ds0, qh, _TN), _dot(ds1, qh, _TN)
                e0, e1 = _dot(p0.astype(BF16), doh, _TN), _dot(p1.astype(BF16), doh, _TN)
                dk0, dk1 = (c0, c1) if h == 0 else (dk0 + c0, dk1 + c1)
                dv0, dv1 = (e0, e1) if h == 0 else (dv0 + e0, dv1 + e1)
            dq_ref[pl.ds(r0, QBLK), :] = jnp.where(head0, dqs[0], dqs[1])
            dk_ref[pl.ds(rp, QBLK), :] += dk0
            dv_ref[pl.ds(rp, QBLK), :] += dv0
            dk_ref[pl.ds(r0, QBLK), :] += dk1
            dv_ref[pl.ds(r0, QBLK), :] += dv1
            return carry

        lax.fori_loop(0, nb, step, 0)

    blk = pl.BlockSpec((L, 2 * HEAD_DIM), lambda c, hp: (0, c * 4 + hp))
    outs = _call(
        body, name=name, grid=(dil, 4), in_specs=[blk] * 6, out_specs=[blk] * 3,
        out_shape=[jax.ShapeDtypeStruct((L, dil * ATTN_W), F32)] * 3,
        compiler_params=_params("parallel", "parallel"),
    )(view(q), view(k), view(v), view(do), view(lse), view(delta))
    return [t.reshape(T, ATTN_W) for t in outs]


def _attn_merge(os_, lses, g, name, tm=512):
    T = os_[0].shape[0]

    def body(o1, o2, o3, l1, l2, l3, g_ref, a_ref, lse_ref, an_ref):
        a1, a2, a3 = l1[...], l2[...], l3[...]
        m = jnp.maximum(jnp.maximum(a1, a2), a3)
        e1, e2, e3 = jnp.exp(a1 - m), jnp.exp(a2 - m), jnp.exp(a3 - m)
        z = e1 + e2 + e3
        attn = (e1 * o1[...] + e2 * o2[...] + e3 * o3[...]) / z
        a_ref[...] = attn
        lse_ref[...] = m + jnp.log(z)
        r = lax.rsqrt(jnp.mean(attn * attn, axis=-1, keepdims=True) + EPS)
        an_ref[...] = (attn * r * g_ref[...]).astype(BF16)

    row = pl.BlockSpec((tm, ATTN_W), lambda i: (i, 0))
    vec = pl.BlockSpec((1, ATTN_W), lambda i: (0, 0))
    return _call(
        body, name=name, grid=(T // tm,), in_specs=[row] * 6 + [vec], out_specs=[row, row, row],
        out_shape=[jax.ShapeDtypeStruct((T, ATTN_W), F32)] * 2 + [jax.ShapeDtypeStruct((T, ATTN_W), BF16)],
        compiler_params=_params("parallel"),
    )(*os_, *lses, g)


def _attn_norm_bwd(dmix, attn, g, bd, name, tm=512):
    T = attn.shape[0]

    def body(d_ref, a_ref, g_ref, bd_ref, do_ref, dl_ref, dg_ref):
        @pl.when(pl.program_id(0) == 0)
        def _():
            dg_ref[...] = jnp.zeros_like(dg_ref)

        dy, av = d_ref[...], a_ref[...]
        r = lax.rsqrt(jnp.mean(av * av, axis=-1, keepdims=True) + EPS)
        gd = dy * g_ref[...]
        m = jnp.mean(gd * av, axis=-1, keepdims=True)
        da = r * gd - av * (r * r * r) * m
        do_ref[...] = da.astype(BF16)
        dl_ref[...] = _group_mean(da * av, bd_ref[...]) * float(HEAD_DIM)
        dg_ref[...] += jnp.sum(dy * av * r, axis=0, keepdims=True)

    row = pl.BlockSpec((tm, ATTN_W), lambda i: (i, 0))
    vec = pl.BlockSpec((1, ATTN_W), lambda i: (0, 0))
    return _call(
        body, name=name, grid=(T // tm,),
        in_specs=[pl.BlockSpec((tm, ATTN_W), lambda i: (i, 0)), row, vec, pl.BlockSpec((ATTN_W, ATTN_W), lambda i: (0, 0))],
        out_specs=[row, row, vec],
        out_shape=[jax.ShapeDtypeStruct((T, ATTN_W), BF16), jax.ShapeDtypeStruct((T, ATTN_W), F32),
                   jax.ShapeDtypeStruct((1, ATTN_W), F32)],
        compiler_params=_params("arbitrary"),
    )(dmix, attn, g, bd)


def _rec_gates(xc, wrg_ref, wig_ref, brg_ref, big_ref, lam_ref):
    xb = xc.astype(BF16)
    r = _sigmoid(_dot(xb, wrg_ref[...]) + brg_ref[...])
    ig = _sigmoid(_dot(xb, wig_ref[...]) + big_ref[...])
    sp = _softplus_neg(lam_ref[...])
    log_a = -LRU_C * r * sp
    a = jnp.exp(log_a)
    th = jnp.tanh(log_a)
    mult = jnp.sqrt(-2.0 * th / (1.0 - th))
    return xb, r, ig, sp, a, mult


def _rec_fwd(proj, cw, cb, wrg, wig, brg, big, lam, g, name, tm=256):
    T = proj.shape[0]
    hb = tm // 8

    def body(xr_ref, halo_ref, gr_ref, cw_ref, cb_ref, wrg_ref, wig_ref, brg_ref, big_ref, lam_ref, g_ref,
             xc_ref, h_ref, out_ref, carry):
        i = pl.program_id(0)

        @pl.when(i == 0)
        def _():
            carry[...] = jnp.zeros_like(carry)

        xr = xr_ref[...]
        halo = jnp.where(i > 0, halo_ref[...], 0.0)
        xc = cb_ref[...] + cw_ref[3:4, :] * xr
        for s in range(1, REC_CONV):
            xc = xc + cw_ref[3 - s:4 - s, :] * _shift_down(xr, halo, s)
        xc_ref[...] = xc
        _, _, ig, _, a, mult = _rec_gates(xc, wrg_ref, wig_ref, brg_ref, big_ref, lam_ref)
        pa, hl = _scan_fwd(a, mult * (ig * xc))
        h = hl + pa * carry[0:1, :]
        h_ref[...] = h
        carry[0:1, :] = h_ref[pl.ds(tm - 1, 1), :]
        hg = h * _gelu(gr_ref[...])
        r = lax.rsqrt(jnp.mean(hg * hg, axis=-1, keepdims=True) + EPS)
        out_ref[...] = (hg * r * g_ref[...]).astype(BF16)

    vec = pl.BlockSpec((1, REC_W), lambda i: (0, 0))
    row = pl.BlockSpec((tm, REC_W), lambda i: (i, 0))
    mat = pl.BlockSpec((REC_W, REC_W), lambda i: (0, 0))
    return _call(
        body, name=name, grid=(T // tm,),
        in_specs=[pl.BlockSpec((tm, REC_W), lambda i: (i, 3)),
                  pl.BlockSpec((8, REC_W), lambda i: (jnp.maximum(i * hb - 1, 0), 3)),
                  pl.BlockSpec((tm, REC_W), lambda i: (i, 4)),
                  pl.BlockSpec((8, REC_W), lambda i: (0, 0)), vec, mat, mat, vec, vec, vec, vec],
        out_specs=[row, row, row],
        out_shape=[jax.ShapeDtypeStruct((T, REC_W), F32)] * 2 + [jax.ShapeDtypeStruct((T, REC_W), BF16)],
        scratch_shapes=[pltpu.VMEM((8, REC_W), F32)],
        compiler_params=_params("arbitrary"),
    )(proj, proj, proj, cw, cb, wrg, wig, brg, big, lam, g)


def _rec_bwd(dmix, proj, xc, h, cw, cb, wrg, wig, brg, big, lam, g, name, tm=256):
    T = proj.shape[0]
    nt = T // tm
    hb = tm // 8

    def body(d_ref, xr_ref, xhalo_ref, gr_ref, xc_ref, h_ref, hhalo_ref, cw_ref, cb_ref, wrg_ref, wig_ref, brg_ref,
             big_ref, lam_ref, g_ref,
             dxr_ref, dgr_ref, gcw_ref, gcb_ref, gwrg_ref, gwig_ref, gbrg_ref, gbig_ref, glam_ref, gg_ref,
             g_carry, a_first, dxc_next, gsp):
        i = pl.program_id(0)
        first_tile = i == nt - 1

        @pl.when(i == 0)
        def _():
            for ref in (gcw_ref, gcb_ref, gwrg_ref, gwig_ref, gbrg_ref, gbig_ref, glam_ref, gg_ref,
                        g_carry, a_first, dxc_next, gsp):
                ref[...] = jnp.zeros_like(ref)

        xr, xc, hv = xr_ref[...], xc_ref[...], h_ref[...]
        xhalo = jnp.where(first_tile, 0.0, xhalo_ref[...])
        hhalo = jnp.where(first_tile, 0.0, hhalo_ref[...])
        xb, r, ig, sp, a, mult = _rec_gates(xc, wrg_ref, wig_ref, brg_ref, big_ref, lam_ref)
        h_prev = _shift_down(hv, hhalo, 1)
        ge, dge = _gelu_and_grad(gr_ref[...])
        hg = hv * ge
        rr = lax.rsqrt(jnp.mean(hg * hg, axis=-1, keepdims=True) + EPS)
        dy = d_ref[...]
        gd = dy * g_ref[...]
        dhg = rr * gd - hg * (rr * rr * rr) * jnp.mean(gd * hg, axis=-1, keepdims=True)
        gg_ref[...] += jnp.sum(dy * hg * rr, axis=0, keepdims=True)
        dgr_ref[...] = (dhg * hv * dge).astype(BF16)
        dh = dhg * ge
        b = _shift_up(a, jnp.broadcast_to(a_first[0:1, :], (8, REC_W)), 1)
        pb, gl = _scan_bwd(b, dh)
        gs = gl + pb * g_carry[0:1, :]
        g_carry[0:1, :] = gs[0:1, :]
        a_first[0:1, :] = a[0:1, :]
        da = gs * h_prev
        dmult = gs * (ig * xc)
        di = gs * (mult * xc)
        dxc = gs * (mult * ig)
        dlog_a = da * a - dmult * (a * a) / mult
        gsp[...] += jnp.sum(dlog_a * (-LRU_C * r), axis=0, keepdims=True)
        dzr = (dlog_a * (-LRU_C * sp)) * (r * (1.0 - r))
        dzi = di * (ig * (1.0 - ig))
        dzr_b, dzi_b = dzr.astype(BF16), dzi.astype(BF16)
        dxc = dxc + _dot(dzr_b, wrg_ref[...], _NT) + _dot(dzi_b, wig_ref[...], _NT)
        gwrg_ref[...] += _dot(xb, dzr_b, _TN)
        gwig_ref[...] += _dot(xb, dzi_b, _TN)
        gbrg_ref[...] += jnp.sum(dzr, axis=0, keepdims=True)
        gbig_ref[...] += jnp.sum(dzi, axis=0, keepdims=True)
        nxt = dxc_next[...]
        dxr = cw_ref[3:4, :] * dxc
        gcw_ref[3:4, :] += jnp.sum(dxc * xr, axis=0, keepdims=True)
        for s in range(1, REC_CONV):
            dxr = dxr + cw_ref[3 - s:4 - s, :] * _shift_up(dxc, nxt, s)
            gcw_ref[3 - s:4 - s, :] += jnp.sum(dxc * _shift_down(xr, xhalo, s), axis=0, keepdims=True)
        gcb_ref[...] += jnp.sum(dxc, axis=0, keepdims=True)
        dxc_next[...] = dxc[:8]
        dxr_ref[...] = dxr.astype(BF16)

        @pl.when(first_tile)
        def _():
            glam_ref[...] = gsp[...] * (-_sigmoid(-lam_ref[...]))

    rev = lambda i: nt - 1 - i
    vec = pl.BlockSpec((1, REC_W), lambda i: (0, 0))
    row = pl.BlockSpec((tm, REC_W), lambda i: (rev(i), 0))
    mat = pl.BlockSpec((REC_W, REC_W), lambda i: (0, 0))
    cwb = pl.BlockSpec((8, REC_W), lambda i: (0, 0))
    halo = lambda c: pl.BlockSpec((8, REC_W), lambda i, c=c: (jnp.maximum(rev(i) * hb - 1, 0), c))
    return _call(
        body, name=name, grid=(nt,),
        in_specs=[pl.BlockSpec((tm, REC_W), lambda i: (rev(i), 1)),
                  pl.BlockSpec((tm, REC_W), lambda i: (rev(i), 3)), halo(3),
                  pl.BlockSpec((tm, REC_W), lambda i: (rev(i), 4)),
                  row, row, halo(0), cwb, vec, mat, mat, vec, vec, vec, vec],
        out_specs=[row, row, cwb, vec, mat, mat, vec, vec, vec, vec],
        out_shape=[jax.ShapeDtypeStruct((T, REC_W), BF16)] * 2
        + [jax.ShapeDtypeStruct((8, REC_W), F32), jax.ShapeDtypeStruct((1, REC_W), F32)]
        + [jax.ShapeDtypeStruct((REC_W, REC_W), F32)] * 2 + [jax.ShapeDtypeStruct((1, REC_W), F32)] * 4,
        scratch_shapes=[pltpu.VMEM((8, REC_W), F32)] * 3 + [pltpu.VMEM((1, REC_W), F32)],
        compiler_params=_params("arbitrary"),
    )(dmix, proj, proj, proj, xc, h, h, cw, cb, wrg, wig, brg, big, lam, g)


def _ffn_conv(x_ext, cw_ref, cb_ref):
    n = x_ext.shape[0]
    return (cb_ref[...] + cw_ref[2:3, :] * x_ext + cw_ref[1:2, :] * pltpu.roll(x_ext, 1, 0)
            + cw_ref[0:1, :] * pltpu.roll(x_ext, 2, 0))


def _ffn_act(pg, pu, cwg, cwu, cbg, cbu, name, tm=512, tc=768):
    T, F = pg.shape
    hb = tm // 8

    def body(g_ref, gh_ref, u_ref, uh_ref, cwg_ref, cwu_ref, cbg_ref, cbu_ref, o_ref):
        first = pl.program_id(0) == 0
        ge = jnp.concatenate([jnp.where(first, 0.0, gh_ref[...]), g_ref[...]], axis=0)
        ue = jnp.concatenate([jnp.where(first, 0.0, uh_ref[...]), u_ref[...]], axis=0)
        act = _gelu(_ffn_conv(ge, cwg_ref, cbg_ref)) * _ffn_conv(ue, cwu_ref, cbu_ref)
        o_ref[...] = act[8:].astype(BF16)

    tile = pl.BlockSpec((tm, tc), lambda i, j: (i, j))
    halo = pl.BlockSpec((8, tc), lambda i, j: (jnp.maximum(i * hb - 1, 0), j))
    cw = pl.BlockSpec((8, tc), lambda i, j: (0, j))
    cb = pl.BlockSpec((1, tc), lambda i, j: (0, j))
    return _call(
        body, name=name, grid=(T // tm, F // tc), in_specs=[tile, halo, tile, halo, cw, cw, cb, cb], out_specs=tile,
        out_shape=jax.ShapeDtypeStruct((T, F), BF16), compiler_params=_params("parallel", "parallel"),
    )(pg, pg, pu, pu, cwg, cwu, cbg, cbu)


def _ffn_act_bwd(pg, pu, dact, cwg, cwu, cbg, cbu, name, tm=512, tc=768):
    T, F = pg.shape
    nt = T // tm
    hb = tm // 8

    def body(g_ref, gp_ref, gn_ref, u_ref, up_ref, un_ref, d_ref, dn_ref, cwg_ref, cwu_ref, cbg_ref, cbu_ref,
             dg_ref, du_ref, gcwg_ref, gcwu_ref, gcbg_ref, gcbu_ref):
        i = pl.program_id(1)
        first, last = i == 0, i == nt - 1

        @pl.when(first)
        def _():
            for ref in (gcwg_ref, gcwu_ref, gcbg_ref, gcbu_ref):
                ref[...] = jnp.zeros_like(ref)

        ext = lambda p, t, n: jnp.concatenate([jnp.where(first, 0.0, p[...]), t[...], jnp.where(last, 0.0, n[...])], axis=0)
        ge, ue = ext(gp_ref, g_ref, gn_ref), ext(up_ref, u_ref, un_ref)
        de = jnp.concatenate([jnp.zeros((8, tc), F32), d_ref[...], jnp.where(last, 0.0, dn_ref[...])], axis=0)
        gel, dgel = _gelu_and_grad(_ffn_conv(ge, cwg_ref, cbg_ref))
        d_gate = de * _ffn_conv(ue, cwu_ref, cbu_ref) * dgel
        d_up = de * gel
        n = tm + 16
        for dcv, xe, cw_ref, dx_ref, gcw_ref, gcb_ref in ((d_gate, ge, cwg_ref, dg_ref, gcwg_ref, gcbg_ref),
                                                            (d_up, ue, cwu_ref, du_ref, gcwu_ref, gcbu_ref)):
            dx = cw_ref[2:3, :] * dcv + cw_ref[1:2, :] * pltpu.roll(dcv, n - 1, 0) + cw_ref[0:1, :] * pltpu.roll(dcv, n - 2, 0)
            dx_ref[...] = dx[8:tm + 8].astype(BF16)
            dt = dcv[8:tm + 8]
            gcw_ref[2:3, :] += jnp.sum(dt * xe[8:tm + 8], axis=0, keepdims=True)
            gcw_ref[1:2, :] += jnp.sum(dt * pltpu.roll(xe, 1, 0)[8:tm + 8], axis=0, keepdims=True)
            gcw_ref[0:1, :] += jnp.sum(dt * pltpu.roll(xe, 2, 0)[8:tm + 8], axis=0, keepdims=True)
            gcb_ref[...] += jnp.sum(dt, axis=0, keepdims=True)

    tile = pl.BlockSpec((tm, tc), lambda j, i: (i, j))
    prev = pl.BlockSpec((8, tc), lambda j, i: (jnp.maximum(i * hb - 1, 0), j))
    nxt = pl.BlockSpec((8, tc), lambda j, i: (jnp.minimum((i + 1) * hb, nt * hb - 1), j))
    cw = pl.BlockSpec((8, tc), lambda j, i: (0, j))
    cb = pl.BlockSpec((1, tc), lambda j, i: (0, j))
    return _call(
        body, name=name, grid=(F // tc, nt),
        in_specs=[tile, prev, nxt, tile, prev, nxt, tile, nxt, cw, cw, cb, cb],
        out_specs=[tile, tile, cw, cw, cb, cb],
        out_shape=[jax.ShapeDtypeStruct((T, F), BF16)] * 2 + [jax.ShapeDtypeStruct((8, F), F32)] * 2
        + [jax.ShapeDtypeStruct((1, F), F32)] * 2,
        compiler_params=_params("parallel", "arbitrary"),
    )(pg, pg, pg, pu, pu, pu, dact, dact, cwg, cwu, cbg, cbu)


def _add_pairs(g, r1, name, tr=400):
    _, R, n = g.shape

    def body(g_ref, r_ref, o_ref):
        o_ref[...] = (g_ref[...].astype(F32) + r_ref[...].astype(F32)).astype(BF16)

    blk = pl.BlockSpec((None, tr, n), lambda k, i: (k, i, 0))
    return _call(
        body, name=name, grid=(4, R // tr), in_specs=[blk, blk],
        out_specs=pl.BlockSpec((None, tr, n), lambda k, i: (k, i, 0)),
        out_shape=jax.ShapeDtypeStruct((4, R, n), BF16), compiler_params=_params("parallel", "parallel"),
    )(g, r1)


def _adam_update(w, g, m, v):
    m2 = ADAM_B1 * m + (1.0 - ADAM_B1) * g
    v2 = ADAM_B2 * v + (1.0 - ADAM_B2) * (g * g)
    m_hat = m2 / (1.0 - ADAM_B1 ** ADAM_STEP)
    v_hat = v2 / (1.0 - ADAM_B2 ** ADAM_STEP)
    delta = -ADAM_LR * (m_hat / (jnp.sqrt(v_hat) + ADAM_EPS) + ADAM_WD * w)
    return delta, m2, v2


def _adam_sharded(p, r2, w, m, v, name, tr=400):
    R, n = w.shape

    def body(p_ref, r_ref, w_ref, m_ref, v_ref, g_ref, d_ref, m2_ref, v2_ref):
        g = p_ref[...].astype(F32) + r_ref[0].astype(F32) + r_ref[1].astype(F32) + r_ref[2].astype(F32)
        g_ref[...] = g
        d_ref[...], m2_ref[...], v2_ref[...] = _adam_update(w_ref[...], g, m_ref[...], v_ref[...])

    row = pl.BlockSpec((tr, n), lambda i: (i, 0))
    return _call(
        body, name=name, grid=(R // tr,),
        in_specs=[row, pl.BlockSpec((3, tr, n), lambda i: (0, i, 0)), row, row, row],
        out_specs=[row] * 4, out_shape=[jax.ShapeDtypeStruct((R, n), F32)] * 4, compiler_params=_params("parallel"),
    )(p, r2, w, m, v)


def _sum_devices(allg, name):
    r, n = allg.shape[0] // N_DEV, allg.shape[1]

    def body(a_ref, o_ref):
        acc = a_ref[0:r, :]
        for k in range(1, N_DEV):
            acc = acc + a_ref[k * r:(k + 1) * r, :]
        o_ref[...] = acc

    return _call(body, name=name, out_shape=jax.ShapeDtypeStruct((r, n), F32))(allg)


def _adam_small(w, g, m, v, name):
    def body(w_ref, g_ref, m_ref, v_ref, d_ref, m2_ref, v2_ref):
        d_ref[...], m2_ref[...], v2_ref[...] = _adam_update(w_ref[...], g_ref[...], m_ref[...], v_ref[...])

    return _call(body, name=name, out_shape=[jax.ShapeDtypeStruct(w.shape, F32)] * 3)(w, g, m, v)


def _pack_shard(w_in, w_out, w_up, w_down):
    return jnp.concatenate([w_in.reshape(320, 1024), w_out.reshape(128, 1024), w_up.reshape(768, 1024),
                            w_down.reshape(384, 1024)], axis=0)


def _unpack_shard(p):
    return (p[0:320].reshape(1, 1024, 320), p[320:448].reshape(1, 128, 1024), p[448:1216].reshape(1, 1024, 768),
            p[1216:1600].reshape(1, 384, 1024))


def _unpack_gathered(a):
    w_in = a[:, 0:320].reshape(8, 1024, 320).transpose(1, 0, 2).reshape(1024, IN_W)
    w_out = a[:, 320:448].reshape(1024, 1024)
    w_up = a[:, 448:1216].reshape(8, 1024, 768).transpose(1, 0, 2).reshape(1024, 2 * D_FF)
    w_down = a[:, 1216:1600].reshape(D_FF, 1024)
    return w_in, w_out, w_up, w_down


def _pack_full(g_in, g_out, g_up, g_down):
    return jnp.concatenate([
        g_in.reshape(1024, 8, 320).transpose(1, 0, 2).reshape(8, 320, 1024),
        g_out.reshape(8, 128, 1024),
        g_up.reshape(1024, 8, 768).transpose(1, 0, 2).reshape(8, 768, 1024),
        g_down.reshape(8, 384, 1024)], axis=1)


_SMALL = (("g_mix", 1024), ("q_norm_g", 64), ("k_norm_g", 64), ("rec_conv_b", 512), ("w_rg", 32768), ("b_rg", 512),
          ("w_ig", 32768), ("b_ig", 512), ("lru_lambda", 512), ("g_attn_out", 512), ("g_rec_out", 512),
          ("g_ffn", 1024), ("ffn_conv_b", 6144))
_SMALL_SHAPES = {"g_mix": (1, 1024), "q_norm_g": (1, 64), "k_norm_g": (1, 64), "rec_conv_b": (1, 512),
                 "w_rg": (1, 8, 64, 64), "b_rg": (1, 8, 64), "w_ig": (1, 8, 64, 64), "b_ig": (1, 8, 64),
                 "lru_lambda": (1, 512), "g_attn_out": (1, 512), "g_rec_out": (1, 512), "g_ffn": (1, 1024),
                 "ffn_conv_b": (1, 6144)}
_N_REPL = sum(n for _, n in _SMALL)
_N_SMALL = _N_REPL + 4 * 64 + 3 * 768
_SMALL_PAD_ROWS = 80


def _pack_small(d, rec_cw, ffn_cw):
    flat = jnp.concatenate([d[k].reshape(-1) for k, _ in _SMALL] + [rec_cw.reshape(-1), ffn_cw.reshape(-1)])
    return jnp.pad(flat, (0, _SMALL_PAD_ROWS * 1024 - _N_SMALL)).reshape(_SMALL_PAD_ROWS, 1024)


def _unpack_small(p):
    flat = p.reshape(-1)
    out, o = {}, 0
    for k, n in _SMALL:
        out[k] = flat[o:o + n].reshape(_SMALL_SHAPES[k])
        o += n
    out["rec_conv_w"] = flat[o:o + 256].reshape(1, 4, 64)
    out["ffn_conv_w"] = flat[o + 256:o + 256 + 2304].reshape(1, 3, 768)
    return out


def _block_diag(w):
    eye = jnp.eye(8, dtype=w.dtype)
    return (w[:, :, None, :] * eye[:, None, :, None]).reshape(512, 512)


def kernel(x, positions, g_mix, w_in, q_norm_g, k_norm_g, rec_conv_w, rec_conv_b, w_rg, b_rg, w_ig, b_ig, lru_lambda, g_attn_out, g_rec_out, w_out, g_ffn, w_up, ffn_conv_w, ffn_conv_b, w_down, loss_target, m_g_mix, m_w_in, m_q_norm_g, m_k_norm_g, m_rec_conv_w, m_rec_conv_b, m_w_rg, m_b_rg, m_w_ig, m_b_ig, m_lru_lambda, m_g_attn_out, m_g_rec_out, m_w_out, m_g_ffn, m_w_up, m_ffn_conv_w, m_ffn_conv_b, m_w_down, v_g_mix, v_w_in, v_q_norm_g, v_k_norm_g, v_rec_conv_w, v_rec_conv_b, v_w_rg, v_b_rg, v_w_ig, v_b_ig, v_lru_lambda, v_g_attn_out, v_g_rec_out, v_w_out, v_g_ffn, v_w_up, v_ffn_conv_w, v_ffn_conv_b, v_w_down):
    T = x.shape[1]
    dev = 4 * lax.axis_index("x") + 2 * lax.axis_index("y") + lax.axis_index("c")
    xs = x.reshape(T, D_MODEL)
    tgt = loss_target.reshape(T, D_MODEL)
    pos = positions.reshape(T, 1)

    wsh = _pack_shard(w_in, w_out, w_up, w_down)
    wall = _all_gather(wsh.astype(BF16), "ag_weights").reshape(N_DEV, SHARD_ROWS, 1024)
    W_in, W_out, W_up, W_down = _unpack_gathered(wall)
    W_up_g, W_up_u = W_up[:, :D_FF], W_up[:, D_FF:]
    taps = jnp.concatenate([rec_conv_w.reshape(-1), ffn_conv_w.reshape(-1), jnp.zeros((4096 - 2560,), F32)]).reshape(8, 512)
    taps_all = _all_gather(taps, "ag_taps").reshape(N_DEV, 4096)
    rcw = taps_all[:, :256].reshape(8, 4, 64).transpose(1, 0, 2).reshape(4, REC_W)
    fcw = taps_all[:, 256:2560].reshape(8, 3, 768).transpose(1, 0, 2).reshape(3, 2 * D_FF)
    rcw8 = jnp.pad(rcw, ((0, 4), (0, 0)))
    fcw8 = jnp.pad(fcw, ((0, 5), (0, 0)))
    fcwg, fcwu = fcw8[:, :D_FF], fcw8[:, D_FF:]
    fcb = ffn_conv_b.reshape(1, 2 * D_FF)
    fcbg, fcbu = fcb[:, :D_FF], fcb[:, D_FF:]

    half = HEAD_DIM // 2
    inv_freq = ROPE_THETA ** (-jnp.arange(half, dtype=F32) / half)
    invf = jnp.tile(inv_freq, 2 * N_HEADS).reshape(1, ATTN_W)
    bd = jnp.asarray(np.kron(np.eye(N_HEADS), np.full((HEAD_DIM, HEAD_DIM), 1.0 / HEAD_DIM)), BF16)
    qg = jnp.tile(q_norm_g.reshape(HEAD_DIM), N_HEADS).reshape(1, ATTN_W)
    kg = jnp.tile(k_norm_g.reshape(HEAD_DIM), N_HEADS).reshape(1, ATTN_W)
    wrg_bd = _block_diag(w_rg[0]).astype(BF16)
    wig_bd = _block_diag(w_ig[0]).astype(BF16)
    brg, big = b_rg.reshape(1, REC_W), b_ig.reshape(1, REC_W)

    h1 = _rmsnorm(xs, g_mix, "norm_mix")
    proj = _mm(h1, W_in, "nn", F32, "in_proj", tn=1280)
    qh, kh, vh = _qk_prep(proj, pos, invf, qg, kg, bd, "qk_prep")
    branch = [_attn_fwd(qh, kh, vh, d, "attn_fwd_d%d" % d) for d in DILATIONS]
    attn, lse, attn_n = _attn_merge([b[0] for b in branch], [b[1] for b in branch], g_attn_out, "attn_merge")
    xc, hstate, rec_n = _rec_fwd(proj, rcw8, rec_conv_b, wrg_bd, wig_bd, brg, big, lru_lambda, g_rec_out, "rec_fwd")
    mix = jnp.concatenate([attn_n, rec_n], axis=1)
    x2 = _mm(mix, W_out, "nn", F32, "out_proj", add=xs)

    h2 = _rmsnorm(x2, g_ffn, "norm_ffn")
    pg = _mm(h2, W_up_g, "nn", F32, "up_proj_gate")
    pu = _mm(h2, W_up_u, "nn", F32, "up_proj_up")
    act = _ffn_act(pg, pu, fcwg, fcwu, fcbg, fcbu, "ffn_act")
    y = _mm(act, W_down, "nn", F32, "down_proj", add=x2)
    dy, dyb, lparts = _loss_grad(y, tgt, "loss_grad")
    loss = lax.psum(0.5 / D_MODEL * jnp.sum(lparts), ("x", "y", "c"))

    dact = _mm(dyb, W_down, "nt", F32, "d_act")
    g_down = _mm(act, dyb, "tn", BF16, "g_w_down")
    dpg, dpu, g_fcwg, g_fcwu, g_fcbg, g_fcbu = _ffn_act_bwd(pg, pu, dact, fcwg, fcwu, fcbg, fcbu, "ffn_act_bwd")
    g_up = jnp.concatenate([_mm(h2, dpg, "tn", BF16, "g_w_up_gate"), _mm(h2, dpu, "tn", BF16, "g_w_up_up")], axis=1)
    dh2 = _mm(dpg, W_up_g, "nt", F32, "d_h2_gate")
    dh2 = _mm(dpu, W_up_u, "nt", F32, "d_h2_up", add=dh2)
    dx2, dx2b, g_gffn = _rmsnorm_bwd(x2, dh2, dy, g_ffn, "norm_ffn_bwd")

    dmix = _mm(dx2b, W_out, "nt", F32, "d_mix")
    g_out = _mm(mix, dx2b, "tn", BF16, "g_w_out")
    do, delta, g_gattn = _attn_norm_bwd(dmix, attn, g_attn_out, bd, "attn_norm_bwd")
    grads = [_attn_bwd(qh, kh, vh, do, lse, delta, d, "attn_bwd_d%d" % d) for d in DILATIONS]
    dq, dk, dv, g_qg, g_kg = _qk_prep_bwd(proj, [g[0] for g in grads], [g[1] for g in grads], [g[2] for g in grads],
                                          pos, invf, qg, kg, bd, "qk_prep_bwd")
    (dxr, dgr, g_rcw, g_rcb, g_wrg, g_wig, g_brg, g_big, g_lam, g_grec) = _rec_bwd(
        dmix, proj, xc, hstate, rcw8, rec_conv_b, wrg_bd, wig_bd, brg, big, lru_lambda, g_rec_out, "rec_bwd")
    dproj = jnp.concatenate([dq, dk, dv, dxr, dgr], axis=1)
    g_in = _mm(h1, dproj, "tn", BF16, "g_w_in", tn=1280)
    dh1 = _mm(dproj, W_in, "nt", F32, "d_h1", tk=1280)
    grad_x, _, g_gmix = _rmsnorm_bwd(xs, dh1, dx2, g_mix, "norm_mix_bwd")

    gfull = _pack_full(g_in, g_out, g_up, g_down)
    r1 = _exchange_sibling(gfull, "rs_sibling")
    mine = lax.dynamic_index_in_dim(gfull.reshape(4, 2, SHARD_ROWS, 1024), lax.axis_index("c"), 1, keepdims=False)
    part = _add_pairs(mine, r1, "rs_add_pairs")
    r2 = _exchange_chips(part, "rs_chips")
    msh = _pack_shard(m_w_in, m_w_out, m_w_up, m_w_down)
    vsh = _pack_shard(v_w_in, v_w_out, v_w_up, v_w_down)
    own = lax.dynamic_index_in_dim(part, 2 * lax.axis_index("x") + lax.axis_index("y"), 0, keepdims=False)
    gsh, dsh, m2sh, v2sh = _adam_sharded(own, r2, wsh, msh, vsh, "adam_sharded")
    big_out = {k: dict(zip(("w_in", "w_out", "w_up", "w_down"), _unpack_shard(a)))
               for k, a in (("grad", gsh), ("delta", dsh), ("new_m", m2sh), ("new_v", v2sh))}

    blocks = lambda g: jnp.stack([g[64 * n:64 * n + 64, 64 * n:64 * n + 64] for n in range(8)])
    small_g = {
        "g_mix": g_gmix, "q_norm_g": g_qg.reshape(N_HEADS, HEAD_DIM).sum(0), "k_norm_g": g_kg.reshape(N_HEADS, HEAD_DIM).sum(0),
        "rec_conv_b": g_rcb, "w_rg": blocks(g_wrg), "b_rg": g_brg, "w_ig": blocks(g_wig), "b_ig": g_big,
        "lru_lambda": g_lam, "g_attn_out": g_gattn, "g_rec_out": g_grec, "g_ffn": g_gffn,
        "ffn_conv_b": jnp.concatenate([g_fcbg, g_fcbu], axis=1)}
    g_fcw = jnp.concatenate([g_fcwg[:3], g_fcwu[:3]], axis=1)
    flat = jnp.concatenate([small_g[k].reshape(-1) for k, _ in _SMALL] + [g_rcw[:4].reshape(-1), g_fcw.reshape(-1)])
    flat = jnp.pad(flat, (0, SMALL_ROWS * 1024 - flat.shape[0])).reshape(SMALL_ROWS, 1024)
    tot = _sum_devices(_all_gather(flat, "ag_small_grads"), "sum_small_grads").reshape(-1)
    g_small, o = {}, 0
    for k, n in _SMALL:
        g_small[k] = tot[o:o + n]
        o += n
    g_rcw_mine = lax.dynamic_slice(tot[o:o + 2048].reshape(4, REC_W), (0, 64 * dev), (4, 64))
    g_fcw_mine = lax.dynamic_slice(tot[o + 2048:o + 2048 + 18432].reshape(3, 2 * D_FF), (0, 768 * dev), (3, 768))
    given = dict(g_mix=g_mix, q_norm_g=q_norm_g, k_norm_g=k_norm_g, rec_conv_b=rec_conv_b, w_rg=w_rg, b_rg=b_rg, w_ig=w_ig,
                 b_ig=b_ig, lru_lambda=lru_lambda, g_attn_out=g_attn_out, g_rec_out=g_rec_out, g_ffn=g_ffn, ffn_conv_b=ffn_conv_b)
    given_m = dict(g_mix=m_g_mix, q_norm_g=m_q_norm_g, k_norm_g=m_k_norm_g, rec_conv_b=m_rec_conv_b, w_rg=m_w_rg, b_rg=m_b_rg,
                   w_ig=m_w_ig, b_ig=m_b_ig, lru_lambda=m_lru_lambda, g_attn_out=m_g_attn_out, g_rec_out=m_g_rec_out,
                   g_ffn=m_g_ffn, ffn_conv_b=m_ffn_conv_b)
    given_v = dict(g_mix=v_g_mix, q_norm_g=v_q_norm_g, k_norm_g=v_k_norm_g, rec_conv_b=v_rec_conv_b, w_rg=v_w_rg, b_rg=v_b_rg,
                   w_ig=v_w_ig, b_ig=v_b_ig, lru_lambda=v_lru_lambda, g_attn_out=v_g_attn_out, g_rec_out=v_g_rec_out,
                   g_ffn=v_g_ffn, ffn_conv_b=v_ffn_conv_b)
    ws = _pack_small(given, rec_conv_w, ffn_conv_w)
    gs = _pack_small(g_small, g_rcw_mine, g_fcw_mine)
    ms = _pack_small(given_m, m_rec_conv_w, m_ffn_conv_w)
    vs = _pack_small(given_v, v_rec_conv_w, v_ffn_conv_w)
    ds, m2s, v2s = _adam_small(ws, gs, ms, vs, "adam_small")
    small_out = {"grad": _unpack_small(gs), "delta": _unpack_small(ds), "new_m": _unpack_small(m2s), "new_v": _unpack_small(v2s)}

    order = ("g_mix", "w_in", "q_norm_g", "k_norm_g", "rec_conv_w", "rec_conv_b", "w_rg", "b_rg", "w_ig", "b_ig",
             "lru_lambda", "g_attn_out", "g_rec_out", "w_out", "g_ffn", "w_up", "ffn_conv_w", "ffn_conv_b", "w_down")
    outs = [loss, grad_x.reshape(1, T, D_MODEL)]
    for kind in ("grad", "delta", "new_m", "new_v"):
        for name in order:
            outs.append(big_out[kind][name] if name in big_out[kind] else small_out[kind][name])
    return tuple(outs)
```

```python
import math

import numpy as np
import jax
import jax.numpy as jnp
from jax import lax
from jax.experimental import pallas as pl
from jax.experimental.pallas import tpu as pltpu

F32 = jnp.float32
BF16 = jnp.bfloat16

D_MODEL = 1024
HEAD_DIM = 64
ATTN_W = 512
REC_W = 512
N_HEADS = 8
D_FF = 3072
IN_W = 2560
REC_CONV = 4
FFN_CONV = 3
LRU_C = 8.0
ROPE_THETA = 10000.0
EPS = 1e-6
NEG_INF = -1e30
QBLK = 128
DILATIONS = (1, 4, 16)
N_DEV = 8
SMALL_ROWS = 96
ADAM_LR, ADAM_B1, ADAM_B2, ADAM_EPS, ADAM_WD, ADAM_STEP = 0.001, 0.9, 0.999, 1e-08, 0.01, 10
MESH = pl.DeviceIdType.MESH
ANY = pl.BlockSpec(memory_space=pl.ANY)


def _call(body, *, name, **kw):
    return pl.pallas_call(body, name=name, **kw)


def _params(*sem):
    return pltpu.CompilerParams(dimension_semantics=sem, vmem_limit_bytes=56 * 1024 * 1024)


def _gelu(x):
    c = math.sqrt(2.0 / math.pi)
    return 0.5 * x * (1.0 + jnp.tanh(c * (x + 0.044715 * (x * x * x))))


def _gelu_and_grad(x):
    c = math.sqrt(2.0 / math.pi)
    t = jnp.tanh(c * (x + 0.044715 * (x * x * x)))
    g = 0.5 * x * (1.0 + t)
    dg = 0.5 * (1.0 + t) + 0.5 * x * (1.0 - t * t) * (c * (1.0 + 3.0 * 0.044715 * (x * x)))
    return g, dg


def _sigmoid(x):
    return 1.0 / (1.0 + jnp.exp(-x))


def _softplus_neg(lam):
    y = jnp.exp(-jnp.abs(lam))
    u = 1.0 + y
    log1p = jnp.where(u == 1.0, y, jnp.log(u) * y / jnp.where(u == 1.0, 1.0, u - 1.0))
    return jnp.maximum(-lam, 0.0) + log1p


_NN = (((1,), (0,)), ((), ()))
_NT = (((1,), (1,)), ((), ()))
_TN = (((0,), (0,)), ((), ()))


def _dot(a, b, dims=_NN):
    return lax.dot_general(a, b, dims, preferred_element_type=F32)


def _group_mean(v, bd):
    hi = v.astype(BF16)
    lo = (v - hi.astype(F32)).astype(BF16)
    return _dot(hi, bd) + _dot(lo, bd)


def _shift_down(x, halo, s):
    rolled = pltpu.roll(x, s, 0)
    hr = pltpu.roll(halo, s, 0)
    row = lax.broadcasted_iota(jnp.int32, hr.shape, 0)
    first = jnp.where(row < s, hr, rolled[:8])
    return jnp.concatenate([first, rolled[8:]], axis=0)


def _shift_up(x, halo, s):
    n = x.shape[0]
    rolled = pltpu.roll(x, n - s, 0)
    hr = pltpu.roll(halo, 8 - s, 0)
    row = lax.broadcasted_iota(jnp.int32, hr.shape, 0)
    last = jnp.where(row >= 8 - s, hr, rolled[n - 8:])
    return jnp.concatenate([rolled[:n - 8], last], axis=0)


def _scan_fwd(a, u):
    n = a.shape[0]
    row = lax.broadcasted_iota(jnp.int32, a.shape, 0)
    s = 1
    while s < n:
        a_s = jnp.where(row < s, 1.0, pltpu.roll(a, s, 0))
        u_s = jnp.where(row < s, 0.0, pltpu.roll(u, s, 0))
        u = u + a * u_s
        a = a * a_s
        s *= 2
    return a, u


def _scan_bwd(b, v):
    n = b.shape[0]
    row = lax.broadcasted_iota(jnp.int32, b.shape, 0)
    s = 1
    while s < n:
        b_s = jnp.where(row >= n - s, 1.0, pltpu.roll(b, n - s, 0))
        v_s = jnp.where(row >= n - s, 0.0, pltpu.roll(v, n - s, 0))
        v = v + b * v_s
        b = b * b_s
        s *= 2
    return b, v


def _rot_half(y):
    n = y.shape[1]
    lane = lax.broadcasted_iota(jnp.int32, y.shape, 1) & (HEAD_DIM - 1)
    return jnp.where(lane < HEAD_DIM // 2, -pltpu.roll(y, n - HEAD_DIM // 2, 1), pltpu.roll(y, HEAD_DIM // 2, 1))


def _row_tile(r, cap=256):
    return max(t for t in range(16, cap + 1, 16) if r % t == 0)


def _all_gather(shards, name):
    na = len(shards)
    ms = [s.shape[0] for s in shards]

    def body(*refs):
        x_refs, out_refs = refs[:na], refs[na:2 * na]
        send_sems, recv_sems, local_sems = refs[2 * na:]
        x, y, c = lax.axis_index("x"), lax.axis_index("y"), lax.axis_index("c")
        me, sibling = (x, y, c), (x, y, 1 - c)
        chips = [(1 - x, y), (x, 1 - y), (1 - x, 1 - y)]

        def rows(a, px, py, pc):
            return out_refs[a].at[pl.ds((4 * px + 2 * py + pc) * ms[a], ms[a]), :]

        def copy(a, k, block, to, src=None):
            return pltpu.make_async_remote_copy(
                src_ref=rows(a, *block) if src is None else src, dst_ref=rows(a, *block),
                send_sem=send_sems.at[7 * a + k], recv_sem=recv_sems.at[7 * a + k], device_id=to, device_id_type=MESH)

        mine = [pltpu.make_async_copy(x_refs[a], rows(a, *me), local_sems.at[a]) for a in range(na)]
        first = []
        for a in range(na):
            mine[a].start()
            first.append(copy(a, 0, me, sibling, src=x_refs[a]))
            first += [copy(a, 1 + j, me, (*chip, c), src=x_refs[a]) for j, chip in enumerate(chips)]
        for cp in first:
            cp.start()
        passed = []
        for a in range(na):
            for j, chip in enumerate(chips):
                copy(a, 1 + j, (*chip, c), me).wait_recv()
                fw = copy(a, 4 + j, (*chip, c), sibling)
                fw.start()
                passed.append(fw)
        for a in range(na):
            copy(a, 0, sibling, me).wait_recv()
            for j, chip in enumerate(chips):
                copy(a, 4 + j, (*chip, 1 - c), me).wait_recv()
        for cp in first + passed:
            cp.wait_send()
        for cp in mine:
            cp.wait()

    return _call(
        body, name=name, out_shape=[jax.ShapeDtypeStruct((N_DEV * s.shape[0], s.shape[1]), s.dtype) for s in shards],
        in_specs=[ANY] * na, out_specs=[ANY] * na,
        scratch_shapes=[pltpu.SemaphoreType.DMA((7 * na,)), pltpu.SemaphoreType.DMA((7 * na,)),
                        pltpu.SemaphoreType.DMA((na,))],
    )(*shards)


def _exchange_sibling(gs, name):
    na = len(gs)

    def body(*refs):
        g_refs, out_refs = refs[:na], refs[na:2 * na]
        send_sems, recv_sems = refs[2 * na:]
        x, y, c = lax.axis_index("x"), lax.axis_index("y"), lax.axis_index("c")
        copies = [
            pltpu.make_async_remote_copy(
                src_ref=g_refs[a].at[2 * k + (1 - c)], dst_ref=out_refs[a].at[k],
                send_sem=send_sems.at[4 * a + k], recv_sem=recv_sems.at[4 * a + k],
                device_id=(x, y, 1 - c), device_id_type=MESH)
            for a in range(na) for k in range(4)]
        for cp in copies:
            cp.start()
        for cp in copies:
            cp.wait()

    return _call(
        body, name=name, out_shape=[jax.ShapeDtypeStruct((4,) + g.shape[1:], g.dtype) for g in gs],
        in_specs=[ANY] * na, out_specs=[ANY] * na,
        scratch_shapes=[pltpu.SemaphoreType.DMA((4 * na,)), pltpu.SemaphoreType.DMA((4 * na,))],
    )(*gs)


def _exchange_chips(ps, name):
    na = len(ps)

    def body(*refs):
        p_refs, out_refs = refs[:na], refs[na:2 * na]
        send_sems, recv_sems = refs[2 * na:]
        x, y, c = lax.axis_index("x"), lax.axis_index("y"), lax.axis_index("c")
        chips = [(1 - x, y), (x, 1 - y), (1 - x, 1 - y)]
        copies = [
            pltpu.make_async_remote_copy(
                src_ref=p_refs[a].at[2 * cx + cy], dst_ref=out_refs[a].at[k],
                send_sem=send_sems.at[3 * a + k], recv_sem=recv_sems.at[3 * a + k],
                device_id=(cx, cy, c), device_id_type=MESH)
            for a in range(na) for k, (cx, cy) in enumerate(chips)]
        for cp in copies:
            cp.start()
        for cp in copies:
            cp.wait()

    return _call(
        body, name=name, out_shape=[jax.ShapeDtypeStruct((3,) + p.shape[1:], p.dtype) for p in ps],
        in_specs=[ANY] * na, out_specs=[ANY] * na,
        scratch_shapes=[pltpu.SemaphoreType.DMA((3 * na,)), pltpu.SemaphoreType.DMA((3 * na,))],
    )(*ps)


def _mm(a, b, mode, out_dtype, name, add=None, tm=1024, tn=1024, tk=1024, b_noff=0, b_koff=0,
        n=None, k=None, into=None, o_rows=None, o_moff=0):
    if mode == "tn":
        K, M = a.shape
    else:
        M, K = a.shape
    N = n if n is not None else (b.shape[0] if mode == "nt" else b.shape[1])
    if k is not None:
        assert k == K
    tm, tn, tk = min(tm, M), min(tn, N), min(tk, K)
    assert M % tm == 0 and N % tn == 0 and K % tk == 0, (name, M, N, K)
    nk = K // tk
    if mode == "nn":
        a_spec = pl.BlockSpec((tm, tk), lambda i, j, kk: (i, kk))
        b_spec, dims = pl.BlockSpec((tk, tn), lambda i, j, kk: (kk + b_koff, j + b_noff)), _NN
    elif mode == "nt":
        a_spec = pl.BlockSpec((tm, tk), lambda i, j, kk: (i, kk))
        b_spec, dims = pl.BlockSpec((tn, tk), lambda i, j, kk: (j + b_noff, kk + b_koff)), _NT
    else:
        a_spec = pl.BlockSpec((tk, tm), lambda i, j, kk: (kk, i))
        b_spec, dims = pl.BlockSpec((tk, tn), lambda i, j, kk: (kk + b_koff, j + b_noff)), _TN
    o_spec = pl.BlockSpec((tm, tn), lambda i, j, kk: (i + o_moff, j))
    has_add, has_into = add is not None, into is not None

    def body(*refs):
        a_ref, b_ref = refs[0], refs[1]
        add_ref = refs[2] if has_add else None
        o_ref, acc = refs[-2], refs[-1]
        kk = pl.program_id(2)

        @pl.when(kk == 0)
        def _():
            acc[...] = jnp.zeros_like(acc)

        acc[...] += _dot(a_ref[...], b_ref[...], dims)

        @pl.when(kk == nk - 1)
        def _():
            r = acc[...]
            if has_add:
                r = r + add_ref[...]
            o_ref[...] = r.astype(out_dtype)

    ins = [a, b] + ([add] if has_add else []) + ([into] if has_into else [])
    specs = [a_spec, b_spec] + ([pl.BlockSpec((tm, tn), lambda i, j, kk: (i, j))] if has_add else []) + ([ANY] if has_into else [])
    rows = into.shape[0] if has_into else (o_rows if o_rows is not None else M)
    return _call(
        body, name=name, grid=(M // tm, N // tn, nk), in_specs=specs, out_specs=o_spec,
        out_shape=jax.ShapeDtypeStruct((rows, N), out_dtype), scratch_shapes=[pltpu.VMEM((tm, tn), F32)],
        input_output_aliases={len(ins) - 1: 0} if has_into else {},
        compiler_params=_params("parallel", "parallel", "arbitrary"),
    )(*ins)


def _rmsnorm(x, g, name, tm=512):
    T, D = x.shape

    def body(x_ref, g_ref, o_ref):
        xv = x_ref[...]
        r = lax.rsqrt(jnp.mean(xv * xv, axis=-1, keepdims=True) + EPS)
        o_ref[...] = (xv * r * g_ref[...]).astype(BF16)

    return _call(
        body, name=name, grid=(T // tm,),
        in_specs=[pl.BlockSpec((tm, D), lambda i: (i, 0)), pl.BlockSpec((1, D), lambda i: (0, 0))],
        out_specs=pl.BlockSpec((tm, D), lambda i: (i, 0)), out_shape=jax.ShapeDtypeStruct((T, D), BF16),
        compiler_params=_params("parallel"),
    )(x, g)


def _rmsnorm_bwd(x, dh, resid, g, name, tm=512):
    T, D = x.shape

    def body(x_ref, dh_ref, res_ref, g_ref, dx_ref, dxb_ref, dg_ref):
        @pl.when(pl.program_id(0) == 0)
        def _():
            dg_ref[...] = jnp.zeros_like(dg_ref)

        xv, dhv = x_ref[...], dh_ref[...]
        r = lax.rsqrt(jnp.mean(xv * xv, axis=-1, keepdims=True) + EPS)
        gd = dhv * g_ref[...]
        m = jnp.mean(gd * xv, axis=-1, keepdims=True)
        dx = res_ref[...] + r * gd - xv * (r * r * r) * m
        dx_ref[...] = dx
        dxb_ref[...] = dx.astype(BF16)
        dg_ref[...] += jnp.sum(dhv * xv * r, axis=0, keepdims=True)

    row = pl.BlockSpec((tm, D), lambda i: (i, 0))
    vec = pl.BlockSpec((1, D), lambda i: (0, 0))
    return _call(
        body, name=name, grid=(T // tm,), in_specs=[row, row, row, vec], out_specs=[row, row, vec],
        out_shape=[jax.ShapeDtypeStruct((T, D), F32), jax.ShapeDtypeStruct((T, D), BF16), jax.ShapeDtypeStruct((1, D), F32)],
        compiler_params=_params("arbitrary"),
    )(x, dh, resid, g)


def _loss_grad(y, target, name, tm=512):
    T, D = y.shape

    def body(y_ref, t_ref, dy_ref, dyb_ref, l_ref):
        e = y_ref[...] - t_ref[...]
        dy = e * (1.0 / D)
        dy_ref[...] = dy
        dyb_ref[...] = dy.astype(BF16)
        l_ref[...] = jnp.sum(e * e, axis=0, keepdims=True)[None]

    row = pl.BlockSpec((tm, D), lambda i: (i, 0))
    return _call(
        body, name=name, grid=(T // tm,), in_specs=[row, row],
        out_specs=[row, row, pl.BlockSpec((1, 1, D), lambda i: (i, 0, 0))],
        out_shape=[jax.ShapeDtypeStruct((T, D), F32), jax.ShapeDtypeStruct((T, D), BF16),
                   jax.ShapeDtypeStruct((T // tm, 1, D), F32)],
        compiler_params=_params("parallel"),
    )(y, target)


def _qk_prep(proj, pos, invf, qg, kg, bd, name, tm=512):
    T = proj.shape[0]

    def body(q_ref, k_ref, pos_ref, invf_ref, qg_ref, kg_ref, bd_ref, qo_ref, ko_ref):
        ang = pos_ref[...].astype(F32) * invf_ref[...]
        cos, sin = jnp.cos(ang), jnp.sin(ang)

        def prep(xv, gv, scale):
            r = lax.rsqrt(_group_mean(xv * xv, bd_ref[...]) + EPS)
            yv = xv * r * gv
            return ((yv * cos + _rot_half(yv) * sin) * scale).astype(BF16).astype(F32)

        qo_ref[...] = prep(q_ref[...], qg_ref[...], HEAD_DIM ** -0.5)
        ko_ref[...] = prep(k_ref[...], kg_ref[...], 1.0)

    col = lambda j: pl.BlockSpec((tm, ATTN_W), lambda i, j=j: (i, j))
    vec = pl.BlockSpec((1, ATTN_W), lambda i: (0, 0))
    out = pl.BlockSpec((tm, ATTN_W), lambda i: (i, 0))
    return _call(
        body, name=name, grid=(T // tm,),
        in_specs=[col(0), col(1), pl.BlockSpec((tm, 1), lambda i: (i, 0)), vec, vec, vec,
                  pl.BlockSpec((ATTN_W, ATTN_W), lambda i: (0, 0))],
        out_specs=[out, out], out_shape=[jax.ShapeDtypeStruct((T, ATTN_W), F32)] * 2,
        compiler_params=_params("parallel"),
    )(proj, proj, pos, invf, qg, kg, bd)


def _qk_prep_bwd(proj, dqh, dkh, dv, pos, invf, qg, kg, bd, name, tm=512):
    T = proj.shape[0]

    def body(q_ref, k_ref, dq_ref, dk_ref, dv_ref, pos_ref, invf_ref, qg_ref, kg_ref, bd_ref, o_ref, gq_ref, gk_ref):
        @pl.when(pl.program_id(0) == 0)
        def _():
            gq_ref[...] = jnp.zeros_like(gq_ref)
            gk_ref[...] = jnp.zeros_like(gk_ref)

        ang = pos_ref[...].astype(F32) * invf_ref[...]
        cos, sin = jnp.cos(ang), jnp.sin(ang)

        def back(xv, gv, dz, scale):
            dz = dz * scale
            dy = dz * cos - _rot_half(dz * sin)
            r = lax.rsqrt(_group_mean(xv * xv, bd_ref[...]) + EPS)
            gd = dy * gv
            m = _group_mean(gd * xv, bd_ref[...])
            dx = r * gd - xv * (r * r * r) * m
            return dx, jnp.sum(dy * xv * r, axis=0, keepdims=True)

        dxq, gs = back(q_ref[...], qg_ref[...], dq_ref[...], HEAD_DIM ** -0.5)
        gq_ref[...] += gs
        dxk, gs = back(k_ref[...], kg_ref[...], dk_ref[...], 1.0)
        gk_ref[...] += gs
        o_ref[...] = jnp.concatenate([dxq.astype(BF16), dxk.astype(BF16), dv_ref[...].astype(BF16)], axis=1)

    col = lambda j: pl.BlockSpec((tm, ATTN_W), lambda i, j=j: (i, j))
    row = pl.BlockSpec((tm, ATTN_W), lambda i: (i, 0))
    vec = pl.BlockSpec((1, ATTN_W), lambda i: (0, 0))
    return _call(
        body, name=name, grid=(T // tm,),
        in_specs=[col(0), col(1), row, row, row, pl.BlockSpec((tm, 1), lambda i: (i, 0)), vec, vec, vec,
                  pl.BlockSpec((ATTN_W, ATTN_W), lambda i: (0, 0))],
        out_specs=[pl.BlockSpec((tm, 3 * ATTN_W), lambda i: (i, 0)), vec, vec],
        out_shape=[jax.ShapeDtypeStruct((T, 3 * ATTN_W), BF16)] + [jax.ShapeDtypeStruct((1, ATTN_W), F32)] * 2,
        compiler_params=_params("arbitrary"),
    )(proj, proj, dqh, dkh, dv, pos, invf, qg, kg, bd)


def _ld(ref, start, size, dil):
    return ref[pl.ds(start, size), :] if dil == 1 else ref[pl.ds(start, size, stride=dil), :]


def _st(ref, start, size, dil, val):
    if dil == 1:
        ref[pl.ds(start, size), :] = val
    else:
        ref[pl.ds(start, size, stride=dil), :] = val


def _attn_geometry(T, dil):
    nb = T // dil // QBLK
    kw = 2 * QBLK if nb >= 2 else QBLK
    return nb, kw


def _attn_block(it, dil, kw):
    c, n = it & (dil - 1), lax.shift_right_logical(it, dil.bit_length() - 1)
    sq = n * (QBLK * dil) + c
    sk = jnp.maximum(n - (kw // QBLK - 1), 0) * (QBLK * dil) + c
    qi = lax.broadcasted_iota(jnp.int32, (QBLK, kw), 0)
    kj = lax.broadcasted_iota(jnp.int32, (QBLK, kw), 1)
    rel = jnp.where(n > 0, kw - QBLK, 0) + qi - kj
    return sq, sk, (rel >= 0) & (rel <= QBLK)


def _attn_fwd(qf, kf, proj, name):
    T = qf.shape[0]

    def body(q_ref, k_ref, v_ref, o_ref, lse_ref):
        head0 = lax.broadcasted_iota(jnp.int32, (QBLK, 2 * HEAD_DIM), 1) < HEAD_DIM
        for bi, dil in enumerate(DILATIONS):
            nb, kw = _attn_geometry(T, dil)

            def step(it, carry, bi=bi, dil=dil, kw=kw):
                sq, sk, ok = _attn_block(it, dil, kw)
                qv = _ld(q_ref, sq, QBLK, dil).astype(BF16)
                kv = _ld(k_ref, sk, kw, dil).astype(BF16)
                vv = _ld(v_ref, sk, kw, dil).astype(BF16)
                outs, lses = [], []
                for h in range(2):
                    qh = jnp.where(head0 if h == 0 else ~head0, qv, jnp.zeros_like(qv))
                    s = jnp.where(ok, _dot(qh, kv, _NT), NEG_INF)
                    m = jnp.max(s, axis=-1, keepdims=True)
                    p = jnp.exp(s - m)
                    l = jnp.sum(p, axis=-1, keepdims=True)
                    outs.append(_dot(p.astype(BF16), vv) / l)
                    lses.append(m + jnp.log(l))
                o_new = jnp.where(head0, outs[0], outs[1])
                l_new = jnp.where(head0, lses[0], lses[1])
                if bi > 0:
                    o_old, l_old = _ld(o_ref, sq, QBLK, dil), _ld(lse_ref, sq, QBLK, dil)
                    mx = jnp.maximum(l_old, l_new)
                    e0, e1 = jnp.exp(l_old - mx), jnp.exp(l_new - mx)
                    z = e0 + e1
                    o_new = (e0 * o_old + e1 * o_new) / z
                    l_new = mx + jnp.log(z)
                _st(o_ref, sq, QBLK, dil, o_new)
                _st(lse_ref, sq, QBLK, dil, l_new)
                return carry

            lax.fori_loop(0, nb * dil, step, 0)

    blk = lambda off: pl.BlockSpec((T, 2 * HEAD_DIM), lambda hp, off=off: (0, off + hp))
    return _call(
        body, name=name, grid=(4,), in_specs=[blk(0), blk(0), blk(8)], out_specs=[blk(0), blk(0)],
        out_shape=[jax.ShapeDtypeStruct((T, ATTN_W), F32)] * 2, compiler_params=_params("parallel"),
    )(qf, kf, proj)


def _attn_bwd(qf, kf, proj, do, lse, delta, name):
    T = qf.shape[0]

    def body(q_ref, k_ref, v_ref, do_ref, lse_ref, dl_ref, dq_ref, dk_ref, dv_ref):
        head0 = lax.broadcasted_iota(jnp.int32, (QBLK, 2 * HEAD_DIM), 1) < HEAD_DIM
        for ref in (dq_ref, dk_ref, dv_ref):
            ref[...] = jnp.zeros_like(ref)
        for dil in DILATIONS:
            nb, kw = _attn_geometry(T, dil)

            def step(it, carry, dil=dil, kw=kw):
                sq, sk, ok = _attn_block(it, dil, kw)
                qv = _ld(q_ref, sq, QBLK, dil).astype(BF16)
                dov = _ld(do_ref, sq, QBLK, dil).astype(BF16)
                lsev, dlv = _ld(lse_ref, sq, QBLK, dil), _ld(dl_ref, sq, QBLK, dil)
                kv = _ld(k_ref, sk, kw, dil).astype(BF16)
                vv = _ld(v_ref, sk, kw, dil).astype(BF16)
                dqs = []
                dk = dv = None
                for h in range(2):
                    sel = head0 if h == 0 else ~head0
                    qh = jnp.where(sel, qv, jnp.zeros_like(qv))
                    doh = jnp.where(sel, dov, jnp.zeros_like(dov))
                    lh = lsev[:, h * HEAD_DIM:h * HEAD_DIM + 1]
                    dh = dlv[:, h * HEAD_DIM:h * HEAD_DIM + 1]
                    p = jnp.where(ok, jnp.exp(_dot(qh, kv, _NT) - lh), 0.0)
                    ds = (p * (_dot(doh, vv, _NT) - dh)).astype(BF16)
                    dqs.append(_dot(ds, kv))
                    ck, cv = _dot(ds, qh, _TN), _dot(p.astype(BF16), doh, _TN)
                    dk, dv = (ck, cv) if h == 0 else (dk + ck, dv + cv)
                _st(dq_ref, sq, QBLK, dil, _ld(dq_ref, sq, QBLK, dil) + jnp.where(head0, dqs[0], dqs[1]))
                _st(dk_ref, sk, kw, dil, _ld(dk_ref, sk, kw, dil) + dk)
                _st(dv_ref, sk, kw, dil, _ld(dv_ref, sk, kw, dil) + dv)
                return carry

            lax.fori_loop(0, nb * dil, step, 0)

    blk = lambda off: pl.BlockSpec((T, 2 * HEAD_DIM), lambda hp, off=off: (0, off + hp))
    return _call(
        body, name=name, grid=(4,), in_specs=[blk(0), blk(0), blk(8), blk(0), blk(0), blk(0)], out_specs=[blk(0)] * 3,
        out_shape=[jax.ShapeDtypeStruct((T, ATTN_W), F32)] * 3, compiler_params=_params("parallel"),
    )(qf, kf, proj, do, lse, delta)


def _attn_norm(attn, g, name, tm=512):
    T = attn.shape[0]

    def body(a_ref, g_ref, o_ref):
        av = a_ref[...]
        r = lax.rsqrt(jnp.mean(av * av, axis=-1, keepdims=True) + EPS)
        o_ref[...] = (av * r * g_ref[...]).astype(BF16)

    row = pl.BlockSpec((tm, ATTN_W), lambda i: (i, 0))
    return _call(
        body, name=name, grid=(T // tm,), in_specs=[row, pl.BlockSpec((1, ATTN_W), lambda i: (0, 0))], out_specs=row,
        out_shape=jax.ShapeDtypeStruct((T, 2 * ATTN_W), BF16), compiler_params=_params("parallel"),
    )(attn, g)


def _attn_norm_bwd(dmix, attn, g, bd, name, tm=512):
    T = attn.shape[0]

    def body(d_ref, a_ref, g_ref, bd_ref, do_ref, dl_ref, dg_ref):
        @pl.when(pl.program_id(0) == 0)
        def _():
            dg_ref[...] = jnp.zeros_like(dg_ref)

        dy, av = d_ref[...], a_ref[...]
        r = lax.rsqrt(jnp.mean(av * av, axis=-1, keepdims=True) + EPS)
        gd = dy * g_ref[...]
        m = jnp.mean(gd * av, axis=-1, keepdims=True)
        da = r * gd - av * (r * r * r) * m
        do_ref[...] = da
        dl_ref[...] = _group_mean(da * av, bd_ref[...]) * float(HEAD_DIM)
        dg_ref[...] += jnp.sum(dy * av * r, axis=0, keepdims=True)

    row = pl.BlockSpec((tm, ATTN_W), lambda i: (i, 0))
    vec = pl.BlockSpec((1, ATTN_W), lambda i: (0, 0))
    return _call(
        body, name=name, grid=(T // tm,),
        in_specs=[row, row, vec, pl.BlockSpec((ATTN_W, ATTN_W), lambda i: (0, 0))], out_specs=[row, row, vec],
        out_shape=[jax.ShapeDtypeStruct((T, ATTN_W), F32)] * 2 + [jax.ShapeDtypeStruct((1, ATTN_W), F32)],
        compiler_params=_params("arbitrary"),
    )(dmix, attn, g, bd)


def _rec_gates(xc, wrg_ref, wig_ref, brg_ref, big_ref, lam_ref):
    xb = xc.astype(BF16)
    r = _sigmoid(_dot(xb, wrg_ref[...]) + brg_ref[...])
    ig = _sigmoid(_dot(xb, wig_ref[...]) + big_ref[...])
    sp = _softplus_neg(lam_ref[...])
    log_a = -LRU_C * r * sp
    a = jnp.exp(log_a)
    th = jnp.tanh(log_a)
    mult = jnp.sqrt(-2.0 * th / (1.0 - th))
    return xb, r, ig, sp, a, mult


def _rec_fwd(proj, mix, cw, cb, wrg, wig, brg, big, lam, g, name, tm=256):
    T = proj.shape[0]
    hb = tm // 8

    def body(xr_ref, halo_ref, gr_ref, cw_ref, cb_ref, wrg_ref, wig_ref, brg_ref, big_ref, lam_ref, g_ref, mix_ref,
             xc_ref, h_ref, out_ref, carry):
        i = pl.program_id(0)

        @pl.when(i == 0)
        def _():
            carry[...] = jnp.zeros_like(carry)

        xr = xr_ref[...]
        halo = jnp.where(i > 0, halo_ref[...], 0.0)
        xc = cb_ref[...] + cw_ref[3:4, :] * xr
        for s in range(1, REC_CONV):
            xc = xc + cw_ref[3 - s:4 - s, :] * _shift_down(xr, halo, s)
        xc_ref[...] = xc
        _, _, ig, _, a, mult = _rec_gates(xc, wrg_ref, wig_ref, brg_ref, big_ref, lam_ref)
        pa, hl = _scan_fwd(a, mult * (ig * xc))
        h = hl + pa * carry[0:1, :]
        h_ref[...] = h
        carry[0:1, :] = h_ref[pl.ds(tm - 1, 1), :]
        hg = h * _gelu(gr_ref[...])
        r = lax.rsqrt(jnp.mean(hg * hg, axis=-1, keepdims=True) + EPS)
        out_ref[...] = (hg * r * g_ref[...]).astype(BF16)

    vec = pl.BlockSpec((1, REC_W), lambda i: (0, 0))
    row = pl.BlockSpec((tm, REC_W), lambda i: (i, 0))
    mat = pl.BlockSpec((REC_W, REC_W), lambda i: (0, 0))
    return _call(
        body, name=name, grid=(T // tm,),
        in_specs=[pl.BlockSpec((tm, REC_W), lambda i: (i, 3)),
                  pl.BlockSpec((8, REC_W), lambda i: (jnp.maximum(i * hb - 1, 0), 3)),
                  pl.BlockSpec((tm, REC_W), lambda i: (i, 4)),
                  pl.BlockSpec((8, REC_W), lambda i: (0, 0)), vec, mat, mat, vec, vec, vec, vec, ANY],
        out_specs=[row, row, pl.BlockSpec((tm, REC_W), lambda i: (i, 1))],
        out_shape=[jax.ShapeDtypeStruct((T, REC_W), F32)] * 2 + [jax.ShapeDtypeStruct(mix.shape, BF16)],
        scratch_shapes=[pltpu.VMEM((8, REC_W), F32)], input_output_aliases={11: 2},
        compiler_params=_params("arbitrary"),
    )(proj, proj, proj, cw, cb, wrg, wig, brg, big, lam, g, mix)


def _rec_bwd(dmix, proj, xc, h, cw, cb, wrg, wig, brg, big, lam, g, name, tm=256):
    T = proj.shape[0]
    nt = T // tm
    hb = tm // 8

    def body(d_ref, xr_ref, xhalo_ref, gr_ref, xc_ref, h_ref, hhalo_ref, cw_ref, cb_ref, wrg_ref, wig_ref, brg_ref,
             big_ref, lam_ref, g_ref,
             drec_ref, gcw_ref, gcb_ref, gwrg_ref, gwig_ref, gbrg_ref, gbig_ref, glam_ref, gg_ref,
             g_carry, a_first, dxc_next, gsp):
        i = pl.program_id(0)
        first_tile = i == nt - 1

        @pl.when(i == 0)
        def _():
            for ref in (gcw_ref, gcb_ref, gwrg_ref, gwig_ref, gbrg_ref, gbig_ref, glam_ref, gg_ref,
                        g_carry, a_first, dxc_next, gsp):
                ref[...] = jnp.zeros_like(ref)

        xr, xc, hv = xr_ref[...], xc_ref[...], h_ref[...]
        xhalo = jnp.where(first_tile, 0.0, xhalo_ref[...])
        hhalo = jnp.where(first_tile, 0.0, hhalo_ref[...])
        xb, r, ig, sp, a, mult = _rec_gates(xc, wrg_ref, wig_ref, brg_ref, big_ref, lam_ref)
        h_prev = _shift_down(hv, hhalo, 1)
        ge, dge = _gelu_and_grad(gr_ref[...])
        hg = hv * ge
        rr = lax.rsqrt(jnp.mean(hg * hg, axis=-1, keepdims=True) + EPS)
        dy = d_ref[...]
        gd = dy * g_ref[...]
        dhg = rr * gd - hg * (rr * rr * rr) * jnp.mean(gd * hg, axis=-1, keepdims=True)
        gg_ref[...] += jnp.sum(dy * hg * rr, axis=0, keepdims=True)
        dgr = (dhg * hv * dge).astype(BF16)
        dh = dhg * ge
        b = _shift_up(a, jnp.broadcast_to(a_first[0:1, :], (8, REC_W)), 1)
        pb, gl = _scan_bwd(b, dh)
        gs = gl + pb * g_carry[0:1, :]
        g_carry[0:1, :] = gs[0:1, :]
        a_first[0:1, :] = a[0:1, :]
        da = gs * h_prev
        dmult = gs * (ig * xc)
        di = gs * (mult * xc)
        dxc = gs * (mult * ig)
        dlog_a = da * a - dmult * (a * a) / mult
        gsp[...] += jnp.sum(dlog_a * (-LRU_C * r), axis=0, keepdims=True)
        dzr = (dlog_a * (-LRU_C * sp)) * (r * (1.0 - r))
        dzi = di * (ig * (1.0 - ig))
        dzr_b, dzi_b = dzr.astype(BF16), dzi.astype(BF16)
        dxc = dxc + _dot(dzr_b, wrg_ref[...], _NT) + _dot(dzi_b, wig_ref[...], _NT)
        gwrg_ref[...] += _dot(xb, dzr_b, _TN)
        gwig_ref[...] += _dot(xb, dzi_b, _TN)
        gbrg_ref[...] += jnp.sum(dzr, axis=0, keepdims=True)
        gbig_ref[...] += jnp.sum(dzi, axis=0, keepdims=True)
        nxt = dxc_next[...]
        dxr = cw_ref[3:4, :] * dxc
        gcw_ref[3:4, :] += jnp.sum(dxc * xr, axis=0, keepdims=True)
        for s in range(1, REC_CONV):
            dxr = dxr + cw_ref[3 - s:4 - s, :] * _shift_up(dxc, nxt, s)
            gcw_ref[3 - s:4 - s, :] += jnp.sum(dxc * _shift_down(xr, xhalo, s), axis=0, keepdims=True)
        gcb_ref[...] += jnp.sum(dxc, axis=0, keepdims=True)
        dxc_next[...] = dxc[:8]
        drec_ref[...] = jnp.concatenate([dxr.astype(BF16), dgr], axis=1)

        @pl.when(first_tile)
        def _():
            glam_ref[...] = gsp[...] * (-_sigmoid(-lam_ref[...]))

    rev = lambda i: nt - 1 - i
    vec = pl.BlockSpec((1, REC_W), lambda i: (0, 0))
    row = pl.BlockSpec((tm, REC_W), lambda i: (rev(i), 0))
    mat = pl.BlockSpec((REC_W, REC_W), lambda i: (0, 0))
    cwb = pl.BlockSpec((8, REC_W), lambda i: (0, 0))
    halo = lambda c: pl.BlockSpec((8, REC_W), lambda i, c=c: (jnp.maximum(rev(i) * hb - 1, 0), c))
    return _call(
        body, name=name, grid=(nt,),
        in_specs=[pl.BlockSpec((tm, REC_W), lambda i: (rev(i), 1)),
                  pl.BlockSpec((tm, REC_W), lambda i: (rev(i), 3)), halo(3),
                  pl.BlockSpec((tm, REC_W), lambda i: (rev(i), 4)),
                  row, row, halo(0), cwb, vec, mat, mat, vec, vec, vec, vec],
        out_specs=[pl.BlockSpec((tm, 2 * REC_W), lambda i: (rev(i), 0)), cwb, vec, mat, mat, vec, vec, vec, vec],
        out_shape=[jax.ShapeDtypeStruct((T, 2 * REC_W), BF16)]
        + [jax.ShapeDtypeStruct((8, REC_W), F32), jax.ShapeDtypeStruct((1, REC_W), F32)]
        + [jax.ShapeDtypeStruct((REC_W, REC_W), F32)] * 2 + [jax.ShapeDtypeStruct((1, REC_W), F32)] * 4,
        scratch_shapes=[pltpu.VMEM((8, REC_W), F32)] * 3 + [pltpu.VMEM((1, REC_W), F32)],
        compiler_params=_params("arbitrary"),
    )(dmix, proj, proj, proj, xc, h, h, cw, cb, wrg, wig, brg, big, lam, g)


def _ffn_conv(x_ext, cw_ref, cb_ref):
    return (cb_ref[...] + cw_ref[2:3, :] * x_ext + cw_ref[1:2, :] * pltpu.roll(x_ext, 1, 0)
            + cw_ref[0:1, :] * pltpu.roll(x_ext, 2, 0))


def _ffn_act(pg, pu, cw, cb, name, tm=512, tc=768):
    T, F = pg.shape
    hb = tm // 8
    nc = F // tc

    def body(g_ref, gh_ref, u_ref, uh_ref, cwg_ref, cwu_ref, cbg_ref, cbu_ref, o_ref):
        first = pl.program_id(0) == 0
        ge = jnp.concatenate([jnp.where(first, 0.0, gh_ref[...]), g_ref[...]], axis=0)
        ue = jnp.concatenate([jnp.where(first, 0.0, uh_ref[...]), u_ref[...]], axis=0)
        act = _gelu(_ffn_conv(ge, cwg_ref, cbg_ref)) * _ffn_conv(ue, cwu_ref, cbu_ref)
        o_ref[...] = act[8:].astype(BF16)

    tile = pl.BlockSpec((tm, tc), lambda i, j: (i, j))
    halo = pl.BlockSpec((8, tc), lambda i, j: (jnp.maximum(i * hb - 1, 0), j))
    cws = lambda off: pl.BlockSpec((8, tc), lambda i, j, off=off: (0, j + off))
    cbs = lambda off: pl.BlockSpec((1, tc), lambda i, j, off=off: (0, j + off))
    return _call(
        body, name=name, grid=(T // tm, nc),
        in_specs=[tile, halo, tile, halo, cws(0), cws(nc), cbs(0), cbs(nc)], out_specs=tile,
        out_shape=jax.ShapeDtypeStruct((T, F), BF16), compiler_params=_params("parallel", "parallel"),
    )(pg, pg, pu, pu, cw, cw, cb, cb)


def _ffn_act_bwd(pg, pu, dact, cw, cb, name, tm=512, tc=768):
    T, F = pg.shape
    nt = T // tm
    hb = tm // 8
    nc = F // tc

    def body(g_ref, gp_ref, gn_ref, u_ref, up_ref, un_ref, d_ref, dn_ref, cwg_ref, cwu_ref, cbg_ref, cbu_ref,
             dg_ref, du_ref, gcwg_ref, gcwu_ref, gcbg_ref, gcbu_ref):
        i = pl.program_id(1)
        first, last = i == 0, i == nt - 1

        @pl.when(first)
        def _():
            for ref in (gcwg_ref, gcwu_ref, gcbg_ref, gcbu_ref):
                ref[...] = jnp.zeros_like(ref)

        ext = lambda p, t, n: jnp.concatenate([jnp.where(first, 0.0, p[...]), t[...], jnp.where(last, 0.0, n[...])], axis=0)
        ge, ue = ext(gp_ref, g_ref, gn_ref), ext(up_ref, u_ref, un_ref)
        de = jnp.concatenate([jnp.zeros((8, tc), F32), d_ref[...], jnp.where(last, 0.0, dn_ref[...])], axis=0)
        gel, dgel = _gelu_and_grad(_ffn_conv(ge, cwg_ref, cbg_ref))
        d_gate = de * _ffn_conv(ue, cwu_ref, cbu_ref) * dgel
        d_up = de * gel
        n = tm + 16
        for dcv, xe, cw_ref, dx_ref, gcw_ref, gcb_ref in ((d_gate, ge, cwg_ref, dg_ref, gcwg_ref, gcbg_ref),
                                                            (d_up, ue, cwu_ref, du_ref, gcwu_ref, gcbu_ref)):
            dx = cw_ref[2:3, :] * dcv + cw_ref[1:2, :] * pltpu.roll(dcv, n - 1, 0) + cw_ref[0:1, :] * pltpu.roll(dcv, n - 2, 0)
            dx_ref[...] = dx[8:tm + 8].astype(BF16)
            dt = dcv[8:tm + 8]
            gcw_ref[2:3, :] += jnp.sum(dt * xe[8:tm + 8], axis=0, keepdims=True)
            gcw_ref[1:2, :] += jnp.sum(dt * pltpu.roll(xe, 1, 0)[8:tm + 8], axis=0, keepdims=True)
            gcw_ref[0:1, :] += jnp.sum(dt * pltpu.roll(xe, 2, 0)[8:tm + 8], axis=0, keepdims=True)
            gcb_ref[...] += jnp.sum(dt, axis=0, keepdims=True)

    tile = pl.BlockSpec((tm, tc), lambda j, i: (i, j))
    prev = pl.BlockSpec((8, tc), lambda j, i: (jnp.maximum(i * hb - 1, 0), j))
    nxt = pl.BlockSpec((8, tc), lambda j, i: (jnp.minimum((i + 1) * hb, nt * hb - 1), j))
    cws = lambda off: pl.BlockSpec((8, tc), lambda j, i, off=off: (0, j + off))
    cbs = lambda off: pl.BlockSpec((1, tc), lambda j, i, off=off: (0, j + off))
    return _call(
        body, name=name, grid=(nc, nt),
        in_specs=[tile, prev, nxt, tile, prev, nxt, tile, nxt, cws(0), cws(nc), cbs(0), cbs(nc)],
        out_specs=[tile, tile, cws(0), cws(0), cbs(0), cbs(0)],
        out_shape=[jax.ShapeDtypeStruct((T, F), BF16)] * 2 + [jax.ShapeDtypeStruct((8, F), F32)] * 2
        + [jax.ShapeDtypeStruct((1, F), F32)] * 2,
        compiler_params=_params("parallel", "arbitrary"),
    )(pg, pg, pg, pu, pu, pu, dact, dact, cw, cw, cb, cb)


def _add_pairs(g, r1, core, name):
    _, r, n = g.shape

    def body(c_ref, g_ref, r_ref, o_ref):
        o_ref[...] = (g_ref[...].astype(F32) + r_ref[...].astype(F32)).astype(BF16)

    spec = pltpu.PrefetchScalarGridSpec(
        num_scalar_prefetch=1, grid=(4,),
        in_specs=[pl.BlockSpec((None, r, n), lambda k, c_ref: (2 * k + c_ref[0], 0, 0)),
                  pl.BlockSpec((None, r, n), lambda k, c_ref: (k, 0, 0))],
        out_specs=pl.BlockSpec((None, r, n), lambda k, c_ref: (k, 0, 0)))
    return _call(body, name=name, grid_spec=spec, out_shape=jax.ShapeDtypeStruct((4, r, n), BF16),
                 compiler_params=_params("parallel"))(core, g, r1)


def _adam_update(w, g, m, v):
    m2 = ADAM_B1 * m + (1.0 - ADAM_B1) * g
    v2 = ADAM_B2 * v + (1.0 - ADAM_B2) * (g * g)
    m_hat = m2 / (1.0 - ADAM_B1 ** ADAM_STEP)
    v_hat = v2 / (1.0 - ADAM_B2 ** ADAM_STEP)
    delta = -ADAM_LR * (m_hat / (jnp.sqrt(v_hat) + ADAM_EPS) + ADAM_WD * w)
    return delta, m2, v2


def _adam_sharded(p, r2, chip, w, m, v, name):
    r, n = w.shape
    tr = _row_tile(r)

    def body(c_ref, p_ref, r_ref, w_ref, m_ref, v_ref, g_ref, d_ref, m2_ref, v2_ref):
        g = p_ref[...].astype(F32) + r_ref[0].astype(F32) + r_ref[1].astype(F32) + r_ref[2].astype(F32)
        g_ref[...] = g
        d_ref[...], m2_ref[...], v2_ref[...] = _adam_update(w_ref[...], g, m_ref[...], v_ref[...])

    row = pl.BlockSpec((tr, n), lambda i, c_ref: (i, 0))
    spec = pltpu.PrefetchScalarGridSpec(
        num_scalar_prefetch=1, grid=(r // tr,),
        in_specs=[pl.BlockSpec((None, tr, n), lambda i, c_ref: (c_ref[0], i, 0)),
                  pl.BlockSpec((3, tr, n), lambda i, c_ref: (0, i, 0)), row, row, row],
        out_specs=[row] * 4)
    return _call(body, name=name, grid_spec=spec, out_shape=[jax.ShapeDtypeStruct((r, n), F32)] * 4,
                 compiler_params=_params("parallel"))(chip, p, r2, w, m, v)


def _sum_devices(allg, name):
    r, n = allg.shape[0] // N_DEV, allg.shape[1]

    def body(a_ref, o_ref):
        acc = a_ref[0:r, :]
        for k in range(1, N_DEV):
            acc = acc + a_ref[k * r:(k + 1) * r, :]
        o_ref[...] = acc

    return _call(body, name=name, out_shape=jax.ShapeDtypeStruct((r, n), F32))(allg)


def _adam_small(w, g, m, v, name):
    def body(w_ref, g_ref, m_ref, v_ref, d_ref, m2_ref, v2_ref):
        d_ref[...], m2_ref[...], v2_ref[...] = _adam_update(w_ref[...], g_ref[...], m_ref[...], v_ref[...])

    return _call(body, name=name, out_shape=[jax.ShapeDtypeStruct(w.shape, F32)] * 3)(w, g, m, v)


_SMALL = (("g_mix", 1024), ("q_norm_g", 64), ("k_norm_g", 64), ("rec_conv_b", 512), ("w_rg", 32768), ("b_rg", 512),
          ("w_ig", 32768), ("b_ig", 512), ("lru_lambda", 512), ("g_attn_out", 512), ("g_rec_out", 512),
          ("g_ffn", 1024), ("ffn_conv_b", 6144))
_SMALL_SHAPES = {"g_mix": (1, 1024), "q_norm_g": (1, 64), "k_norm_g": (1, 64), "rec_conv_b": (1, 512),
                 "w_rg": (1, 8, 64, 64), "b_rg": (1, 8, 64), "w_ig": (1, 8, 64, 64), "b_ig": (1, 8, 64),
                 "lru_lambda": (1, 512), "g_attn_out": (1, 512), "g_rec_out": (1, 512), "g_ffn": (1, 1024),
                 "ffn_conv_b": (1, 6144)}
_N_REPL = sum(n for _, n in _SMALL)
_N_SMALL = _N_REPL + 4 * 64 + 3 * 768
_SMALL_PAD_ROWS = 80


def _pack_small(d, rec_cw, ffn_cw):
    flat = jnp.concatenate([d[k].reshape(-1) for k, _ in _SMALL] + [rec_cw.reshape(-1), ffn_cw.reshape(-1)])
    return jnp.pad(flat, (0, _SMALL_PAD_ROWS * 1024 - _N_SMALL)).reshape(_SMALL_PAD_ROWS, 1024)


def _unpack_small(p):
    flat = p.reshape(-1)
    out, o = {}, 0
    for k, n in _SMALL:
        out[k] = flat[o:o + n].reshape(_SMALL_SHAPES[k])
        o += n
    out["rec_conv_w"] = flat[o:o + 256].reshape(1, 4, 64)
    out["ffn_conv_w"] = flat[o + 256:o + 256 + 2304].reshape(1, 3, 768)
    return out


def _block_diag(w):
    eye = jnp.eye(8, dtype=w.dtype)
    return (w[:, :, None, :] * eye[:, None, :, None]).reshape(512, 512)


def kernel(x, positions, g_mix, w_in, q_norm_g, k_norm_g, rec_conv_w, rec_conv_b, w_rg, b_rg, w_ig, b_ig, lru_lambda, g_attn_out, g_rec_out, w_out, g_ffn, w_up, ffn_conv_w, ffn_conv_b, w_down, loss_target, m_g_mix, m_w_in, m_q_norm_g, m_k_norm_g, m_rec_conv_w, m_rec_conv_b, m_w_rg, m_b_rg, m_w_ig, m_b_ig, m_lru_lambda, m_g_attn_out, m_g_rec_out, m_w_out, m_g_ffn, m_w_up, m_ffn_conv_w, m_ffn_conv_b, m_w_down, v_g_mix, v_w_in, v_q_norm_g, v_k_norm_g, v_rec_conv_w, v_rec_conv_b, v_w_rg, v_b_rg, v_w_ig, v_b_ig, v_lru_lambda, v_g_attn_out, v_g_rec_out, v_w_out, v_g_ffn, v_w_up, v_ffn_conv_w, v_ffn_conv_b, v_w_down):
    T = x.shape[1]
    ix, iy, ic = lax.axis_index("x"), lax.axis_index("y"), lax.axis_index("c")
    dev = 4 * ix + 2 * iy + ic
    core = jnp.reshape(ic, (1,)).astype(jnp.int32)
    chip = jnp.reshape(2 * ix + iy, (1,)).astype(jnp.int32)
    xs = x.reshape(T, D_MODEL)
    tgt = loss_target.reshape(T, D_MODEL)
    pos = positions.reshape(T, 1)

    tr = lambda a: a[0].T
    shards = {"w_in": (tr(w_in), tr(m_w_in), tr(v_w_in)), "w_out": (w_out[0], m_w_out[0], v_w_out[0]),
              "w_up": (tr(w_up), tr(m_w_up), tr(v_w_up)), "w_down": (w_down[0], m_w_down[0], v_w_down[0])}
    taps = jnp.concatenate([rec_conv_w.reshape(-1), ffn_conv_w.reshape(-1), jnp.zeros((4096 - 2560,), F32)]).reshape(8, 512)
    W_inT, taps_all, W_out, W_upT, W_down = _all_gather(
        [shards["w_in"][0].astype(BF16), taps, shards["w_out"][0].astype(BF16), shards["w_up"][0].astype(BF16),
         shards["w_down"][0].astype(BF16)], "ag_weights")
    taps_all = taps_all.reshape(N_DEV, 4096)
    rcw = taps_all[:, :256].reshape(8, 4, 64).transpose(1, 0, 2).reshape(4, REC_W)
    fcw = taps_all[:, 256:2560].reshape(8, 3, 768).transpose(1, 0, 2).reshape(3, 2 * D_FF)
    rcw8 = jnp.pad(rcw, ((0, 4), (0, 0)))
    fcw8 = jnp.pad(fcw, ((0, 5), (0, 0)))
    fcb = ffn_conv_b.reshape(1, 2 * D_FF)

    half = HEAD_DIM // 2
    inv_freq = ROPE_THETA ** (-jnp.arange(half, dtype=F32) / half)
    invf = jnp.tile(inv_freq, 2 * N_HEADS).reshape(1, ATTN_W)
    bd = jnp.asarray(np.kron(np.eye(N_HEADS), np.full((HEAD_DIM, HEAD_DIM), 1.0 / HEAD_DIM)), BF16)
    qg = jnp.tile(q_norm_g.reshape(HEAD_DIM), N_HEADS).reshape(1, ATTN_W)
    kg = jnp.tile(k_norm_g.reshape(HEAD_DIM), N_HEADS).reshape(1, ATTN_W)
    wrg_bd = _block_diag(w_rg[0]).astype(BF16)
    wig_bd = _block_diag(w_ig[0]).astype(BF16)
    brg, big = b_rg.reshape(1, REC_W), b_ig.reshape(1, REC_W)

    h1 = _rmsnorm(xs, g_mix, "norm_mix")
    proj = _mm(h1, W_inT, "nt", F32, "in_proj", tn=1280)
    qf, kf = _qk_prep(proj, pos, invf, qg, kg, bd, "qk_prep")
    attn, lse = _attn_fwd(qf, kf, proj, "attn_fwd")
    mix = _attn_norm(attn, g_attn_out, "attn_norm")
    xc, hstate, mix = _rec_fwd(proj, mix, rcw8, rec_conv_b, wrg_bd, wig_bd, brg, big, lru_lambda, g_rec_out, "rec_fwd")
    x2 = _mm(mix, W_out, "nn", F32, "out_proj", add=xs)

    h2 = _rmsnorm(x2, g_ffn, "norm_ffn")
    pg = _mm(h2, W_upT, "nt", F32, "up_proj_gate", n=D_FF)
    pu = _mm(h2, W_upT, "nt", F32, "up_proj_up", n=D_FF, b_noff=D_FF // 1024)
    act = _ffn_act(pg, pu, fcw8, fcb, "ffn_act")
    y = _mm(act, W_down, "nn", F32, "down_proj", add=x2)
    dy, dyb, lparts = _loss_grad(y, tgt, "loss_grad")
    loss = lax.psum(0.5 / D_MODEL * jnp.sum(lparts), ("x", "y", "c"))

    dact = _mm(dyb, W_down, "nt", F32, "d_act")
    g_down = _mm(act, dyb, "tn", BF16, "g_w_down")
    dpg, dpu, g_fcwg, g_fcwu, g_fcbg, g_fcbu = _ffn_act_bwd(pg, pu, dact, fcw8, fcb, "ffn_act_bwd")
    g_upT = _mm(dpg, h2, "tn", BF16, "g_w_up_gate", o_rows=2 * D_FF)
    g_upT = _mm(dpu, h2, "tn", BF16, "g_w_up_up", into=g_upT, o_moff=D_FF // 1024)
    dh2 = _mm(dpg, W_upT, "nn", F32, "d_h2_gate", k=D_FF)
    dh2 = _mm(dpu, W_upT, "nn", F32, "d_h2_up", k=D_FF, b_koff=D_FF // 1024, add=dh2)
    dx2, dx2b, g_gffn = _rmsnorm_bwd(x2, dh2, dy, g_ffn, "norm_ffn_bwd")

    dmix = _mm(dx2b, W_out, "nt", F32, "d_mix")
    g_out = _mm(mix, dx2b, "tn", BF16, "g_w_out")
    do, delta, g_gattn = _attn_norm_bwd(dmix, attn, g_attn_out, bd, "attn_norm_bwd")
    dqh, dkh, dv = _attn_bwd(qf, kf, proj, do, lse, delta, "attn_bwd")
    dqkv, g_qg, g_kg = _qk_prep_bwd(proj, dqh, dkh, dv, pos, invf, qg, kg, bd, "qk_prep_bwd")
    (drec, g_rcw, g_rcb, g_wrg, g_wig, g_brg, g_big, g_lam, g_grec) = _rec_bwd(
        dmix, proj, xc, hstate, rcw8, rec_conv_b, wrg_bd, wig_bd, brg, big, lru_lambda, g_rec_out, "rec_bwd")
    g_inT = _mm(dqkv, h1, "tn", BF16, "g_w_in_qkv", tm=512, o_rows=IN_W)
    g_inT = _mm(drec, h1, "tn", BF16, "g_w_in_rec", tm=512, into=g_inT, o_moff=3 * ATTN_W // 512)
    dh1 = _mm(dqkv, W_inT, "nn", F32, "d_h1_qkv", tk=512, k=3 * ATTN_W)
    dh1 = _mm(drec, W_inT, "nn", F32, "d_h1_rec", tk=512, k=2 * REC_W, b_koff=3 * ATTN_W // 512, add=dh1)
    grad_x, _, g_gmix = _rmsnorm_bwd(xs, dh1, dx2, g_mix, "norm_mix_bwd")

    names = ("w_in", "w_out", "w_up", "w_down")
    gfull = [g.reshape(N_DEV, g.shape[0] // N_DEV, 1024) for g in (g_inT, g_out, g_upT, g_down)]
    r1 = _exchange_sibling(gfull, "rs_sibling")
    part = [_add_pairs(g, r, core, "rs_add_pairs_" + nm) for g, r, nm in zip(gfull, r1, names)]
    r2 = _exchange_chips(part, "rs_chips")
    big_out = {"grad": {}, "delta": {}, "new_m": {}, "new_v": {}}
    for nm, p, r in zip(names, part, r2):
        w_, m_, v_ = shards[nm]
        res = _adam_sharded(p, r, chip, w_, m_, v_, "adam_" + nm)
        for kind, a in zip(("grad", "delta", "new_m", "new_v"), res):
            big_out[kind][nm] = a.T[None] if nm in ("w_in", "w_up") else a[None]

    blocks = lambda g: jnp.stack([g[64 * n:64 * n + 64, 64 * n:64 * n + 64] for n in range(8)])
    small_g = {
        "g_mix": g_gmix, "q_norm_g": g_qg.reshape(N_HEADS, HEAD_DIM).sum(0), "k_norm_g": g_kg.reshape(N_HEADS, HEAD_DIM).sum(0),
        "rec_conv_b": g_rcb, "w_rg": blocks(g_wrg), "b_rg": g_brg, "w_ig": blocks(g_wig), "b_ig": g_big,
        "lru_lambda": g_lam, "g_attn_out": g_gattn, "g_rec_out": g_grec, "g_ffn": g_gffn,
        "ffn_conv_b": jnp.concatenate([g_fcbg, g_fcbu], axis=1)}
    g_fcw = jnp.concatenate([g_fcwg[:3], g_fcwu[:3]], axis=1)
    flat = jnp.concatenate([small_g[k].reshape(-1) for k, _ in _SMALL] + [g_rcw[:4].reshape(-1), g_fcw.reshape(-1)])
    flat = jnp.pad(flat, (0, SMALL_ROWS * 1024 - flat.shape[0])).reshape(SMALL_ROWS, 1024)
    tot = _sum_devices(_all_gather([flat], "ag_small_grads")[0], "sum_small_grads").reshape(-1)
    g_small, o = {}, 0
    for k, n in _SMALL:
        g_small[k] = tot[o:o + n]
        o += n
    g_rcw_mine = lax.dynamic_slice(tot[o:o + 2048].reshape(4, REC_W), (0, 64 * dev), (4, 64))
    g_fcw_mine = lax.dynamic_slice(tot[o + 2048:o + 2048 + 18432].reshape(3, 2 * D_FF), (0, 768 * dev), (3, 768))
    given = dict(g_mix=g_mix, q_norm_g=q_norm_g, k_norm_g=k_norm_g, rec_conv_b=rec_conv_b, w_rg=w_rg, b_rg=b_rg, w_ig=w_ig,
                 b_ig=b_ig, lru_lambda=lru_lambda, g_attn_out=g_attn_out, g_rec_out=g_rec_out, g_ffn=g_ffn, ffn_conv_b=ffn_conv_b)
    given_m = dict(g_mix=m_g_mix, q_norm_g=m_q_norm_g, k_norm_g=m_k_norm_g, rec_conv_b=m_rec_conv_b, w_rg=m_w_rg, b_rg=m_b_rg,
                   w_ig=m_w_ig, b_ig=m_b_ig, lru_lambda=m_lru_lambda, g_attn_out=m_g_attn_out, g_rec_out=m_g_rec_out,
                   g_ffn=m_g_ffn, ffn_conv_b=m_ffn_conv_b)
    given_v = dict(g_mix=v_g_mix, q_norm_g=v_q_norm_g, k_norm_g=v_k_norm_g, rec_conv_b=v_rec_conv_b, w_rg=v_w_rg, b_rg=v_b_rg,
                   w_ig=v_w_ig, b_ig=v_b_ig, lru_lambda=v_lru_lambda, g_attn_out=v_g_attn_out, g_rec_out=v_g_rec_out,
                   g_ffn=v_g_ffn, ffn_conv_b=v_ffn_conv_b)
    ws = _pack_small(given, rec_conv_w, ffn_conv_w)
    gs = _pack_small(g_small, g_rcw_mine, g_fcw_mine)
    ms = _pack_small(given_m, m_rec_conv_w, m_ffn_conv_w)
    vs = _pack_small(given_v, v_rec_conv_w, v_ffn_conv_w)
    ds, m2s, v2s = _adam_small(ws, gs, ms, vs, "adam_small")
    small_out = {"grad": _unpack_small(gs), "delta": _unpack_small(ds), "new_m": _unpack_small(m2s), "new_v": _unpack_small(v2s)}

    order = ("g_mix", "w_in", "q_norm_g", "k_norm_g", "rec_conv_w", "rec_conv_b", "w_rg", "b_rg", "w_ig", "b_ig",
             "lru_lambda", "g_attn_out", "g_rec_out", "w_out", "g_ffn", "w_up", "ffn_conv_w", "ffn_conv_b", "w_down")
    outs = [loss, grad_x.reshape(1, T, D_MODEL)]
    for kind in ("grad", "delta", "new_m", "new_v"):
        for name in order:
            outs.append(big_out[kind][name] if name in big_out[kind] else small_out[kind][name])
    return tuple(outs)
```

```python
import math

import numpy as np
import jax
import jax.numpy as jnp
from jax import lax
from jax.experimental import pallas as pl
from jax.experimental.pallas import tpu as pltpu

F32 = jnp.float32
BF16 = jnp.bfloat16

D_MODEL = 1024
HEAD_DIM = 64
ATTN_W = 512
REC_W = 512
N_HEADS = 8
D_FF = 3072
IN_W = 2560
REC_CONV = 4
FFN_CONV = 3
LRU_C = 8.0
ROPE_THETA = 10000.0
EPS = 1e-6
NEG_INF = -1e30
QBLK = 128
DILATIONS = (1, 4, 16)
N_DEV = 8
SMALL_ROWS = 96
ADAM_LR, ADAM_B1, ADAM_B2, ADAM_EPS, ADAM_WD, ADAM_STEP = 0.001, 0.9, 0.999, 1e-08, 0.01, 10
MESH = pl.DeviceIdType.MESH
ANY = pl.BlockSpec(memory_space=pl.ANY)


def _call(body, *, name, **kw):
    return pl.pallas_call(body, name=name, **kw)


def _params(*sem):
    return pltpu.CompilerParams(dimension_semantics=sem, vmem_limit_bytes=56 * 1024 * 1024)


def _gelu(x):
    c = math.sqrt(2.0 / math.pi)
    return 0.5 * x * (1.0 + jnp.tanh(c * (x + 0.044715 * (x * x * x))))


def _gelu_and_grad(x):
    c = math.sqrt(2.0 / math.pi)
    t = jnp.tanh(c * (x + 0.044715 * (x * x * x)))
    g = 0.5 * x * (1.0 + t)
    dg = 0.5 * (1.0 + t) + 0.5 * x * (1.0 - t * t) * (c * (1.0 + 3.0 * 0.044715 * (x * x)))
    return g, dg


def _sigmoid(x):
    return 1.0 / (1.0 + jnp.exp(-x))


def _softplus_neg(lam):
    y = jnp.exp(-jnp.abs(lam))
    u = 1.0 + y
    log1p = jnp.where(u == 1.0, y, jnp.log(u) * y / jnp.where(u == 1.0, 1.0, u - 1.0))
    return jnp.maximum(-lam, 0.0) + log1p


_NN = (((1,), (0,)), ((), ()))
_NT = (((1,), (1,)), ((), ()))
_TN = (((0,), (0,)), ((), ()))


def _dot(a, b, dims=_NN):
    return lax.dot_general(a, b, dims, preferred_element_type=F32)


def _group_mean(v, bd):
    hi = v.astype(BF16)
    lo = (v - hi.astype(F32)).astype(BF16)
    return _dot(hi, bd) + _dot(lo, bd)


def _shift_down(x, halo, s):
    rolled = pltpu.roll(x, s, 0)
    hr = pltpu.roll(halo, s, 0)
    row = lax.broadcasted_iota(jnp.int32, hr.shape, 0)
    first = jnp.where(row < s, hr, rolled[:8])
    return jnp.concatenate([first, rolled[8:]], axis=0)


def _shift_up(x, halo, s):
    n = x.shape[0]
    rolled = pltpu.roll(x, n - s, 0)
    hr = pltpu.roll(halo, 8 - s, 0)
    row = lax.broadcasted_iota(jnp.int32, hr.shape, 0)
    last = jnp.where(row >= 8 - s, hr, rolled[n - 8:])
    return jnp.concatenate([rolled[:n - 8], last], axis=0)


def _scan_fwd(a, u):
    n = a.shape[0]
    row = lax.broadcasted_iota(jnp.int32, a.shape, 0)
    s = 1
    while s < n:
        a_s = jnp.where(row < s, 1.0, pltpu.roll(a, s, 0))
        u_s = jnp.where(row < s, 0.0, pltpu.roll(u, s, 0))
        u = u + a * u_s
        a = a * a_s
        s *= 2
    return a, u


def _scan_bwd(b, v):
    n = b.shape[0]
    row = lax.broadcasted_iota(jnp.int32, b.shape, 0)
    s = 1
    while s < n:
        b_s = jnp.where(row >= n - s, 1.0, pltpu.roll(b, n - s, 0))
        v_s = jnp.where(row >= n - s, 0.0, pltpu.roll(v, n - s, 0))
        v = v + b * v_s
        b = b * b_s
        s *= 2
    return b, v


def _rot_half(y):
    n = y.shape[1]
    lane = lax.broadcasted_iota(jnp.int32, y.shape, 1) & (HEAD_DIM - 1)
    return jnp.where(lane < HEAD_DIM // 2, -pltpu.roll(y, n - HEAD_DIM // 2, 1), pltpu.roll(y, HEAD_DIM // 2, 1))


def _row_tile(r, cap=256):
    return max(t for t in range(16, cap + 1, 16) if r % t == 0)


def _all_gather(shards, name):
    na = len(shards)
    ms = [s.shape[0] for s in shards]

    def body(*refs):
        x_refs, out_refs = refs[:na], refs[na:2 * na]
        send_sems, recv_sems, local_sems = refs[2 * na:]
        x, y, c = lax.axis_index("x"), lax.axis_index("y"), lax.axis_index("c")
        me, sibling = (x, y, c), (x, y, 1 - c)
        chips = [(1 - x, y), (x, 1 - y), (1 - x, 1 - y)]

        def rows(a, px, py, pc):
            return out_refs[a].at[pl.ds((4 * px + 2 * py + pc) * ms[a], ms[a]), :]

        def copy(a, k, block, to, src=None):
            return pltpu.make_async_remote_copy(
                src_ref=rows(a, *block) if src is None else src, dst_ref=rows(a, *block),
                send_sem=send_sems.at[7 * a + k], recv_sem=recv_sems.at[7 * a + k], device_id=to, device_id_type=MESH)

        mine = [pltpu.make_async_copy(x_refs[a], rows(a, *me), local_sems.at[a]) for a in range(na)]
        first = []
        for a in range(na):
            mine[a].start()
            first.append(copy(a, 0, me, sibling, src=x_refs[a]))
            first += [copy(a, 1 + j, me, (*chip, c), src=x_refs[a]) for j, chip in enumerate(chips)]
        for cp in first:
            cp.start()
        passed = []
        for a in range(na):
            for j, chip in enumerate(chips):
                copy(a, 1 + j, (*chip, c), me).wait_recv()
                fw = copy(a, 4 + j, (*chip, c), sibling)
                fw.start()
                passed.append(fw)
        for a in range(na):
            copy(a, 0, sibling, me).wait_recv()
            for j, chip in enumerate(chips):
                copy(a, 4 + j, (*chip, 1 - c), me).wait_recv()
        for cp in first + passed:
            cp.wait_send()
        for cp in mine:
            cp.wait()

    return _call(
        body, name=name, out_shape=[jax.ShapeDtypeStruct((N_DEV * s.shape[0], s.shape[1]), s.dtype) for s in shards],
        in_specs=[ANY] * na, out_specs=[ANY] * na,
        scratch_shapes=[pltpu.SemaphoreType.DMA((7 * na,)), pltpu.SemaphoreType.DMA((7 * na,)),
                        pltpu.SemaphoreType.DMA((na,))],
    )(*shards)


def _exchange_sibling(gs, name):
    na = len(gs)

    def body(*refs):
        g_refs, out_refs = refs[:na], refs[na:2 * na]
        send_sems, recv_sems = refs[2 * na:]
        x, y, c = lax.axis_index("x"), lax.axis_index("y"), lax.axis_index("c")
        copies = [
            pltpu.make_async_remote_copy(
                src_ref=g_refs[a].at[2 * k + (1 - c)], dst_ref=out_refs[a].at[k],
                send_sem=send_sems.at[4 * a + k], recv_sem=recv_sems.at[4 * a + k],
                device_id=(x, y, 1 - c), device_id_type=MESH)
            for a in range(na) for k in range(4)]
        for cp in copies:
            cp.start()
        for cp in copies:
            cp.wait()

    return _call(
        body, name=name, out_shape=[jax.ShapeDtypeStruct((4,) + g.shape[1:], g.dtype) for g in gs],
        in_specs=[ANY] * na, out_specs=[ANY] * na,
        scratch_shapes=[pltpu.SemaphoreType.DMA((4 * na,)), pltpu.SemaphoreType.DMA((4 * na,))],
    )(*gs)


def _exchange_chips(ps, name):
    na = len(ps)

    def body(*refs):
        p_refs, out_refs = refs[:na], refs[na:2 * na]
        send_sems, recv_sems = refs[2 * na:]
        x, y, c = lax.axis_index("x"), lax.axis_index("y"), lax.axis_index("c")
        chips = [(1 - x, y), (x, 1 - y), (1 - x, 1 - y)]
        copies = [
            pltpu.make_async_remote_copy(
                src_ref=p_refs[a].at[2 * cx + cy], dst_ref=out_refs[a].at[k],
                send_sem=send_sems.at[3 * a + k], recv_sem=recv_sems.at[3 * a + k],
                device_id=(cx, cy, c), device_id_type=MESH)
            for a in range(na) for k, (cx, cy) in enumerate(chips)]
        for cp in copies:
            cp.start()
        for cp in copies:
            cp.wait()

    return _call(
        body, name=name, out_shape=[jax.ShapeDtypeStruct((3,) + p.shape[1:], p.dtype) for p in ps],
        in_specs=[ANY] * na, out_specs=[ANY] * na,
        scratch_shapes=[pltpu.SemaphoreType.DMA((3 * na,)), pltpu.SemaphoreType.DMA((3 * na,))],
    )(*ps)


def _mm(a, b, mode, out_dtype, name, add=None, tm=1024, tn=1024, tk=1024, b_noff=0, b_koff=0,
        n=None, k=None, into=None, o_rows=None, o_moff=0):
    if mode == "tn":
        K, M = a.shape
    else:
        M, K = a.shape
    N = n if n is not None else (b.shape[0] if mode == "nt" else b.shape[1])
    if k is not None:
        assert k == K
    tm, tn, tk = min(tm, M), min(tn, N), min(tk, K)
    assert M % tm == 0 and N % tn == 0 and K % tk == 0, (name, M, N, K)
    nk = K // tk
    if mode == "nn":
        a_spec = pl.BlockSpec((tm, tk), lambda i, j, kk: (i, kk))
        b_spec, dims = pl.BlockSpec((tk, tn), lambda i, j, kk: (kk + b_koff, j + b_noff)), _NN
    elif mode == "nt":
        a_spec = pl.BlockSpec((tm, tk), lambda i, j, kk: (i, kk))
        b_spec, dims = pl.BlockSpec((tn, tk), lambda i, j, kk: (j + b_noff, kk + b_koff)), _NT
    else:
        a_spec = pl.BlockSpec((tk, tm), lambda i, j, kk: (kk, i))
        b_spec, dims = pl.BlockSpec((tk, tn), lambda i, j, kk: (kk + b_koff, j + b_noff)), _TN
    o_spec = pl.BlockSpec((tm, tn), lambda i, j, kk: (i + o_moff, j))
    has_add, has_into = add is not None, into is not None

    def body(*refs):
        a_ref, b_ref = refs[0], refs[1]
        add_ref = refs[2] if has_add else None
        o_ref, acc = refs[-2], refs[-1]
        kk = pl.program_id(2)

        @pl.when(kk == 0)
        def _():
            acc[...] = jnp.zeros_like(acc)

        acc[...] += _dot(a_ref[...], b_ref[...], dims)

        @pl.when(kk == nk - 1)
        def _():
            r = acc[...]
            if has_add:
                r = r + add_ref[...]
            o_ref[...] = r.astype(out_dtype)

    ins = [a, b] + ([add] if has_add else []) + ([into] if has_into else [])
    specs = [a_spec, b_spec] + ([pl.BlockSpec((tm, tn), lambda i, j, kk: (i, j))] if has_add else []) + ([ANY] if has_into else [])
    rows = into.shape[0] if has_into else (o_rows if o_rows is not None else M)
    return _call(
        body, name=name, grid=(M // tm, N // tn, nk), in_specs=specs, out_specs=o_spec,
        out_shape=jax.ShapeDtypeStruct((rows, N), out_dtype), scratch_shapes=[pltpu.VMEM((tm, tn), F32)],
        input_output_aliases={len(ins) - 1: 0} if has_into else {},
        compiler_params=_params("parallel", "parallel", "arbitrary"),
    )(*ins)


def _rmsnorm(x, g, name, tm=512):
    T, D = x.shape

    def body(x_ref, g_ref, o_ref):
        xv = x_ref[...]
        r = lax.rsqrt(jnp.mean(xv * xv, axis=-1, keepdims=True) + EPS)
        o_ref[...] = (xv * r * g_ref[...]).astype(BF16)

    return _call(
        body, name=name, grid=(T // tm,),
        in_specs=[pl.BlockSpec((tm, D), lambda i: (i, 0)), pl.BlockSpec((1, D), lambda i: (0, 0))],
        out_specs=pl.BlockSpec((tm, D), lambda i: (i, 0)), out_shape=jax.ShapeDtypeStruct((T, D), BF16),
        compiler_params=_params("parallel"),
    )(x, g)


def _rmsnorm_bwd(x, dh, resid, g, name, tm=512):
    T, D = x.shape

    def body(x_ref, dh_ref, res_ref, g_ref, dx_ref, dxb_ref, dg_ref):
        @pl.when(pl.program_id(0) == 0)
        def _():
            dg_ref[...] = jnp.zeros_like(dg_ref)

        xv, dhv = x_ref[...], dh_ref[...]
        r = lax.rsqrt(jnp.mean(xv * xv, axis=-1, keepdims=True) + EPS)
        gd = dhv * g_ref[...]
        m = jnp.mean(gd * xv, axis=-1, keepdims=True)
        dx = res_ref[...] + r * gd - xv * (r * r * r) * m
        dx_ref[...] = dx
        dxb_ref[...] = dx.astype(BF16)
        dg_ref[...] += jnp.sum(dhv * xv * r, axis=0, keepdims=True)

    row = pl.BlockSpec((tm, D), lambda i: (i, 0))
    vec = pl.BlockSpec((1, D), lambda i: (0, 0))
    return _call(
        body, name=name, grid=(T // tm,), in_specs=[row, row, row, vec], out_specs=[row, row, vec],
        out_shape=[jax.ShapeDtypeStruct((T, D), F32), jax.ShapeDtypeStruct((T, D), BF16), jax.ShapeDtypeStruct((1, D), F32)],
        compiler_params=_params("arbitrary"),
    )(x, dh, resid, g)


def _loss_grad(y, target, name, tm=512):
    T, D = y.shape

    def body(y_ref, t_ref, dy_ref, dyb_ref, l_ref):
        e = y_ref[...] - t_ref[...]
        dy = e * (1.0 / D)
        dy_ref[...] = dy
        dyb_ref[...] = dy.astype(BF16)
        l_ref[...] = jnp.sum(e * e, axis=0, keepdims=True)[None]

    row = pl.BlockSpec((tm, D), lambda i: (i, 0))
    return _call(
        body, name=name, grid=(T // tm,), in_specs=[row, row],
        out_specs=[row, row, pl.BlockSpec((1, 1, D), lambda i: (i, 0, 0))],
        out_shape=[jax.ShapeDtypeStruct((T, D), F32), jax.ShapeDtypeStruct((T, D), BF16),
                   jax.ShapeDtypeStruct((T // tm, 1, D), F32)],
        compiler_params=_params("parallel"),
    )(y, target)


def _qk_prep(proj, pos, invf, qg, kg, bd, name, tm=512):
    T = proj.shape[0]

    def body(q_ref, k_ref, pos_ref, invf_ref, qg_ref, kg_ref, bd_ref, qo_ref, ko_ref):
        ang = pos_ref[...].astype(F32) * invf_ref[...]
        cos, sin = jnp.cos(ang), jnp.sin(ang)

        def prep(xv, gv, scale):
            r = lax.rsqrt(_group_mean(xv * xv, bd_ref[...]) + EPS)
            yv = xv * r * gv
            return ((yv * cos + _rot_half(yv) * sin) * scale).astype(BF16).astype(F32)

        qo_ref[...] = prep(q_ref[...], qg_ref[...], HEAD_DIM ** -0.5)
        ko_ref[...] = prep(k_ref[...], kg_ref[...], 1.0)

    col = lambda j: pl.BlockSpec((tm, ATTN_W), lambda i, j=j: (i, j))
    vec = pl.BlockSpec((1, ATTN_W), lambda i: (0, 0))
    out = pl.BlockSpec((tm, ATTN_W), lambda i: (i, 0))
    return _call(
        body, name=name, grid=(T // tm,),
        in_specs=[col(0), col(1), pl.BlockSpec((tm, 1), lambda i: (i, 0)), vec, vec, vec,
                  pl.BlockSpec((ATTN_W, ATTN_W), lambda i: (0, 0))],
        out_specs=[out, out], out_shape=[jax.ShapeDtypeStruct((T, ATTN_W), F32)] * 2,
        compiler_params=_params("parallel"),
    )(proj, proj, pos, invf, qg, kg, bd)


def _qk_prep_bwd(proj, dqh, dkh, dv, pos, invf, qg, kg, bd, name, tm=512):
    T = proj.shape[0]

    def body(q_ref, k_ref, dq_ref, dk_ref, dv_ref, pos_ref, invf_ref, qg_ref, kg_ref, bd_ref, o_ref, gq_ref, gk_ref):
        @pl.when(pl.program_id(0) == 0)
        def _():
            gq_ref[...] = jnp.zeros_like(gq_ref)
            gk_ref[...] = jnp.zeros_like(gk_ref)

        ang = pos_ref[...].astype(F32) * invf_ref[...]
        cos, sin = jnp.cos(ang), jnp.sin(ang)

        def back(xv, gv, dz, scale):
            dz = dz * scale
            dy = dz * cos - _rot_half(dz * sin)
            r = lax.rsqrt(_group_mean(xv * xv, bd_ref[...]) + EPS)
            gd = dy * gv
            m = _group_mean(gd * xv, bd_ref[...])
            dx = r * gd - xv * (r * r * r) * m
            return dx, jnp.sum(dy * xv * r, axis=0, keepdims=True)

        dxq, gs = back(q_ref[...], qg_ref[...], dq_ref[...], HEAD_DIM ** -0.5)
        gq_ref[...] += gs
        dxk, gs = back(k_ref[...], kg_ref[...], dk_ref[...], 1.0)
        gk_ref[...] += gs
        o_ref[...] = jnp.concatenate([dxq.astype(BF16), dxk.astype(BF16), dv_ref[...].astype(BF16)], axis=1)

    col = lambda j: pl.BlockSpec((tm, ATTN_W), lambda i, j=j: (i, j))
    row = pl.BlockSpec((tm, ATTN_W), lambda i: (i, 0))
    vec = pl.BlockSpec((1, ATTN_W), lambda i: (0, 0))
    return _call(
        body, name=name, grid=(T // tm,),
        in_specs=[col(0), col(1), row, row, row, pl.BlockSpec((tm, 1), lambda i: (i, 0)), vec, vec, vec,
                  pl.BlockSpec((ATTN_W, ATTN_W), lambda i: (0, 0))],
        out_specs=[pl.BlockSpec((tm, 3 * ATTN_W), lambda i: (i, 0)), vec, vec],
        out_shape=[jax.ShapeDtypeStruct((T, 3 * ATTN_W), BF16)] + [jax.ShapeDtypeStruct((1, ATTN_W), F32)] * 2,
        compiler_params=_params("arbitrary"),
    )(proj, proj, dqh, dkh, dv, pos, invf, qg, kg, bd)


def _ld(ref, start, size, dil):
    return ref[pl.ds(start, size), :] if dil == 1 else ref[pl.ds(start, size, stride=dil), :]


def _st(ref, start, size, dil, val):
    if dil == 1:
        ref[pl.ds(start, size), :] = val
    else:
        ref[pl.ds(start, size, stride=dil), :] = val


def _attn_geometry(T, dil):
    nb = T // dil // QBLK
    kw = 2 * QBLK if nb >= 2 else QBLK
    return nb, kw


ATTN_UNROLL = 4


def _attn_unit(j, u, dil, nit):
    return ATTN_UNROLL * j + u if dil >= ATTN_UNROLL else j + u * (nit // ATTN_UNROLL)


def _attn_block(it, dil, kw):
    c, n = it & (dil - 1), lax.shift_right_logical(it, dil.bit_length() - 1)
    sq = n * (QBLK * dil) + c
    sk = jnp.maximum(n - (kw // QBLK - 1), 0) * (QBLK * dil) + c
    qi = lax.broadcasted_iota(jnp.int32, (2 * QBLK, kw), 0) & (QBLK - 1)
    kj = lax.broadcasted_iota(jnp.int32, (2 * QBLK, kw), 1)
    rel = jnp.where(n > 0, kw - QBLK, 0) + qi - kj
    return sq, sk, (rel >= 0) & (rel <= QBLK)


def _stack_heads(xv, head0):
    z = jnp.zeros_like(xv)
    return jnp.concatenate([jnp.where(head0, xv, z), jnp.where(head0, z, xv)], axis=0)


def _unstack_heads(x2, head0):
    return jnp.where(head0, x2[:QBLK], x2[QBLK:])


def _attn_fwd(qf, kf, proj, name):
    T = qf.shape[0]

    def body(q_ref, k_ref, v_ref, o_ref, lse_ref):
        head0 = lax.broadcasted_iota(jnp.int32, (QBLK, 2 * HEAD_DIM), 1) < HEAD_DIM
        for bi, dil in enumerate(DILATIONS):
            nb, kw = _attn_geometry(T, dil)

            nit = nb * dil

            def step(j, carry, bi=bi, dil=dil, kw=kw, nit=nit):
                units = []
                for u in range(ATTN_UNROLL):
                    sq, sk, ok = _attn_block(_attn_unit(j, u, dil, nit), dil, kw)
                    old = (_ld(o_ref, sq, QBLK, dil), _ld(lse_ref, sq, QBLK, dil)) if bi > 0 else None
                    units.append((sq, ok, _ld(q_ref, sq, QBLK, dil).astype(BF16), _ld(k_ref, sk, kw, dil).astype(BF16),
                                  _ld(v_ref, sk, kw, dil).astype(BF16), old))
                results = []
                for sq, ok, qv, kv, vv, old in units:
                    s = jnp.where(ok, _dot(_stack_heads(qv, head0), kv, _NT), NEG_INF)
                    m = jnp.max(s, axis=-1, keepdims=True)
                    p = jnp.exp(s - m).astype(BF16)
                    acc = _dot(p, jnp.concatenate([vv, jnp.ones_like(vv)], axis=1))
                    l = acc[:, 2 * HEAD_DIM:]
                    o_new = _unstack_heads(acc[:, :2 * HEAD_DIM] / l, head0)
                    l_new = _unstack_heads(m + jnp.log(l), head0)
                    if bi > 0:
                        o_old, l_old = old
                        mx = jnp.maximum(l_old, l_new)
                        e0, e1 = jnp.exp(l_old - mx), jnp.exp(l_new - mx)
                        z = e0 + e1
                        o_new = (e0 * o_old + e1 * o_new) / z
                        l_new = mx + jnp.log(z)
                    results.append((sq, o_new, l_new))
                for sq, o_new, l_new in results:
                    _st(o_ref, sq, QBLK, dil, o_new)
                    _st(lse_ref, sq, QBLK, dil, l_new)
                return carry

            lax.fori_loop(0, nit // ATTN_UNROLL, step, 0)

    blk = lambda off: pl.BlockSpec((T, 2 * HEAD_DIM), lambda hp, off=off: (0, off + hp))
    return _call(
        body, name=name, grid=(4,), in_specs=[blk(0), blk(0), blk(8)], out_specs=[blk(0), blk(0)],
        out_shape=[jax.ShapeDtypeStruct((T, ATTN_W), F32)] * 2, compiler_params=_params("parallel"),
    )(qf, kf, proj)


def _attn_bwd(qf, kf, proj, do, lse, delta, name):
    T = qf.shape[0]

    def body(q_ref, k_ref, v_ref, do_ref, lse_ref, dl_ref, dq_ref, dk_ref, dv_ref):
        head0 = lax.broadcasted_iota(jnp.int32, (QBLK, 2 * HEAD_DIM), 1) < HEAD_DIM
        for ref in (dq_ref, dk_ref, dv_ref):
            ref[...] = jnp.zeros_like(ref)
        for dil in DILATIONS:
            nb, kw = _attn_geometry(T, dil)

            nit = nb * dil

            def step(j, carry, dil=dil, kw=kw, nit=nit):
                units = []
                for u in range(ATTN_UNROLL):
                    sq, sk, ok = _attn_block(_attn_unit(j, u, dil, nit), dil, kw)
                    lsev, dlv = _ld(lse_ref, sq, QBLK, dil), _ld(dl_ref, sq, QBLK, dil)
                    units.append((sq, sk, ok, _ld(q_ref, sq, QBLK, dil).astype(BF16), _ld(do_ref, sq, QBLK, dil).astype(BF16),
                                  jnp.concatenate([lsev[:, 0:1], lsev[:, HEAD_DIM:HEAD_DIM + 1]], axis=0),
                                  jnp.concatenate([dlv[:, 0:1], dlv[:, HEAD_DIM:HEAD_DIM + 1]], axis=0),
                                  _ld(k_ref, sk, kw, dil).astype(BF16), _ld(v_ref, sk, kw, dil).astype(BF16),
                                  _ld(dq_ref, sq, QBLK, dil), _ld(dk_ref, sk, kw, dil), _ld(dv_ref, sk, kw, dil)))
                results = []
                for sq, sk, ok, qv, dov, lse2, dl2, kv, vv, dq0, dk0, dv0 in units:
                    q2, do2 = _stack_heads(qv, head0), _stack_heads(dov, head0)
                    p = jnp.where(ok, jnp.exp(_dot(q2, kv, _NT) - lse2), 0.0)
                    ds = (p * (_dot(do2, vv, _NT) - dl2)).astype(BF16)
                    results.append((sq, sk, dq0 + _unstack_heads(_dot(ds, kv), head0),
                                    dk0 + _dot(ds, q2, _TN), dv0 + _dot(p.astype(BF16), do2, _TN)))
                for sq, sk, dq, dk, dv in results:
                    _st(dq_ref, sq, QBLK, dil, dq)
                    _st(dk_ref, sk, kw, dil, dk)
                    _st(dv_ref, sk, kw, dil, dv)
                return carry

            lax.fori_loop(0, nit // ATTN_UNROLL, step, 0)

    blk = lambda off: pl.BlockSpec((T, 2 * HEAD_DIM), lambda hp, off=off: (0, off + hp))
    return _call(
        body, name=name, grid=(4,), in_specs=[blk(0), blk(0), blk(8), blk(0), blk(0), blk(0)], out_specs=[blk(0)] * 3,
        out_shape=[jax.ShapeDtypeStruct((T, ATTN_W), F32)] * 3, compiler_params=_params("parallel"),
    )(qf, kf, proj, do, lse, delta)


def _attn_norm(attn, g, name, tm=512):
    T = attn.shape[0]

    def body(a_ref, g_ref, o_ref):
        av = a_ref[...]
        r = lax.rsqrt(jnp.mean(av * av, axis=-1, keepdims=True) + EPS)
        o_ref[...] = (av * r * g_ref[...]).astype(BF16)

    row = pl.BlockSpec((tm, ATTN_W), lambda i: (i, 0))
    return _call(
        body, name=name, grid=(T // tm,), in_specs=[row, pl.BlockSpec((1, ATTN_W), lambda i: (0, 0))], out_specs=row,
        out_shape=jax.ShapeDtypeStruct((T, 2 * ATTN_W), BF16), compiler_params=_params("parallel"),
    )(attn, g)


def _attn_norm_bwd(dmix, attn, g, bd, name, tm=512):
    T = attn.shape[0]

    def body(d_ref, a_ref, g_ref, bd_ref, do_ref, dl_ref, dg_ref):
        @pl.when(pl.program_id(0) == 0)
        def _():
            dg_ref[...] = jnp.zeros_like(dg_ref)

        dy, av = d_ref[...], a_ref[...]
        r = lax.rsqrt(jnp.mean(av * av, axis=-1, keepdims=True) + EPS)
        gd = dy * g_ref[...]
        m = jnp.mean(gd * av, axis=-1, keepdims=True)
        da = r * gd - av * (r * r * r) * m
        do_ref[...] = da
        dl_ref[...] = _group_mean(da * av, bd_ref[...]) * float(HEAD_DIM)
        dg_ref[...] += jnp.sum(dy * av * r, axis=0, keepdims=True)

    row = pl.BlockSpec((tm, ATTN_W), lambda i: (i, 0))
    vec = pl.BlockSpec((1, ATTN_W), lambda i: (0, 0))
    return _call(
        body, name=name, grid=(T // tm,),
        in_specs=[row, row, vec, pl.BlockSpec((ATTN_W, ATTN_W), lambda i: (0, 0))], out_specs=[row, row, vec],
        out_shape=[jax.ShapeDtypeStruct((T, ATTN_W), F32)] * 2 + [jax.ShapeDtypeStruct((1, ATTN_W), F32)],
        compiler_params=_params("arbitrary"),
    )(dmix, attn, g, bd)


def _rec_gates(xc, wrg_ref, wig_ref, brg_ref, big_ref, lam_ref):
    xb = xc.astype(BF16)
    r = _sigmoid(_dot(xb, wrg_ref[...]) + brg_ref[...])
    ig = _sigmoid(_dot(xb, wig_ref[...]) + big_ref[...])
    sp = _softplus_neg(lam_ref[...])
    log_a = -LRU_C * r * sp
    a = jnp.exp(log_a)
    th = jnp.tanh(log_a)
    mult = jnp.sqrt(-2.0 * th / (1.0 - th))
    return xb, r, ig, sp, a, mult


def _rec_fwd(proj, mix, cw, cb, wrg, wig, brg, big, lam, g, name, tm=256):
    T = proj.shape[0]
    hb = tm // 8

    def body(xr_ref, halo_ref, gr_ref, cw_ref, cb_ref, wrg_ref, wig_ref, brg_ref, big_ref, lam_ref, g_ref, mix_ref,
             xc_ref, h_ref, out_ref, carry):
        i = pl.program_id(0)

        @pl.when(i == 0)
        def _():
            carry[...] = jnp.zeros_like(carry)

        xr = xr_ref[...]
        halo = jnp.where(i > 0, halo_ref[...], 0.0)
        xc = cb_ref[...] + cw_ref[3:4, :] * xr
        for s in range(1, REC_CONV):
            xc = xc + cw_ref[3 - s:4 - s, :] * _shift_down(xr, halo, s)
        xc_ref[...] = xc
        _, _, ig, _, a, mult = _rec_gates(xc, wrg_ref, wig_ref, brg_ref, big_ref, lam_ref)
        pa, hl = _scan_fwd(a, mult * (ig * xc))
        h = hl + pa * carry[0:1, :]
        h_ref[...] = h
        carry[0:1, :] = h_ref[pl.ds(tm - 1, 1), :]
        hg = h * _gelu(gr_ref[...])
        r = lax.rsqrt(jnp.mean(hg * hg, axis=-1, keepdims=True) + EPS)
        out_ref[...] = (hg * r * g_ref[...]).astype(BF16)

    vec = pl.BlockSpec((1, REC_W), lambda i: (0, 0))
    row = pl.BlockSpec((tm, REC_W), lambda i: (i, 0))
    mat = pl.BlockSpec((REC_W, REC_W), lambda i: (0, 0))
    return _call(
        body, name=name, grid=(T // tm,),
        in_specs=[pl.BlockSpec((tm, REC_W), lambda i: (i, 3)),
                  pl.BlockSpec((8, REC_W), lambda i: (jnp.maximum(i * hb - 1, 0), 3)),
                  pl.BlockSpec((tm, REC_W), lambda i: (i, 4)),
                  pl.BlockSpec((8, REC_W), lambda i: (0, 0)), vec, mat, mat, vec, vec, vec, vec, ANY],
        out_specs=[row, row, pl.BlockSpec((tm, REC_W), lambda i: (i, 1))],
        out_shape=[jax.ShapeDtypeStruct((T, REC_W), F32)] * 2 + [jax.ShapeDtypeStruct(mix.shape, BF16)],
        scratch_shapes=[pltpu.VMEM((8, REC_W), F32)], input_output_aliases={11: 2},
        compiler_params=_params("arbitrary"),
    )(proj, proj, proj, cw, cb, wrg, wig, brg, big, lam, g, mix)


def _rec_bwd(dmix, proj, xc, h, cw, cb, wrg, wig, brg, big, lam, g, name, tm=256):
    T = proj.shape[0]
    nt = T // tm
    hb = tm // 8

    def body(d_ref, xr_ref, xhalo_ref, gr_ref, xc_ref, h_ref, hhalo_ref, cw_ref, cb_ref, wrg_ref, wig_ref, brg_ref,
             big_ref, lam_ref, g_ref,
             drec_ref, gcw_ref, gcb_ref, gwrg_ref, gwig_ref, gbrg_ref, gbig_ref, glam_ref, gg_ref,
             g_carry, a_first, dxc_next, gsp):
        i = pl.program_id(0)
        first_tile = i == nt - 1

        @pl.when(i == 0)
        def _():
            for ref in (gcw_ref, gcb_ref, gwrg_ref, gwig_ref, gbrg_ref, gbig_ref, glam_ref, gg_ref,
                        g_carry, a_first, dxc_next, gsp):
                ref[...] = jnp.zeros_like(ref)

        xr, xc, hv = xr_ref[...], xc_ref[...], h_ref[...]
        xhalo = jnp.where(first_tile, 0.0, xhalo_ref[...])
        hhalo = jnp.where(first_tile, 0.0, hhalo_ref[...])
        xb, r, ig, sp, a, mult = _rec_gates(xc, wrg_ref, wig_ref, brg_ref, big_ref, lam_ref)
        h_prev = _shift_down(hv, hhalo, 1)
        ge, dge = _gelu_and_grad(gr_ref[...])
        hg = hv * ge
        rr = lax.rsqrt(jnp.mean(hg * hg, axis=-1, keepdims=True) + EPS)
        dy = d_ref[...]
        gd = dy * g_ref[...]
        dhg = rr * gd - hg * (rr * rr * rr) * jnp.mean(gd * hg, axis=-1, keepdims=True)
        gg_ref[...] += jnp.sum(dy * hg * rr, axis=0, keepdims=True)
        dgr = (dhg * hv * dge).astype(BF16)
        dh = dhg * ge
        b = _shift_up(a, jnp.broadcast_to(a_first[0:1, :], (8, REC_W)), 1)
        pb, gl = _scan_bwd(b, dh)
        gs = gl + pb * g_carry[0:1, :]
        g_carry[0:1, :] = gs[0:1, :]
        a_first[0:1, :] = a[0:1, :]
        da = gs * h_prev
        dmult = gs * (ig * xc)
        di = gs * (mult * xc)
        dxc = gs * (mult * ig)
        dlog_a = da * a - dmult * (a * a) / mult
        gsp[...] += jnp.sum(dlog_a * (-LRU_C * r), axis=0, keepdims=True)
        dzr = (dlog_a * (-LRU_C * sp)) * (r * (1.0 - r))
        dzi = di * (ig * (1.0 - ig))
        dzr_b, dzi_b = dzr.astype(BF16), dzi.astype(BF16)
        dxc = dxc + _dot(dzr_b, wrg_ref[...], _NT) + _dot(dzi_b, wig_ref[...], _NT)
        gwrg_ref[...] += _dot(xb, dzr_b, _TN)
        gwig_ref[...] += _dot(xb, dzi_b, _TN)
        gbrg_ref[...] += jnp.sum(dzr, axis=0, keepdims=True)
        gbig_ref[...] += jnp.sum(dzi, axis=0, keepdims=True)
        nxt = dxc_next[...]
        dxr = cw_ref[3:4, :] * dxc
        gcw_ref[3:4, :] += jnp.sum(dxc * xr, axis=0, keepdims=True)
        for s in range(1, REC_CONV):
            dxr = dxr + cw_ref[3 - s:4 - s, :] * _shift_up(dxc, nxt, s)
            gcw_ref[3 - s:4 - s, :] += jnp.sum(dxc * _shift_down(xr, xhalo, s), axis=0, keepdims=True)
        gcb_ref[...] += jnp.sum(dxc, axis=0, keepdims=True)
        dxc_next[...] = dxc[:8]
        drec_ref[...] = jnp.concatenate([dxr.astype(BF16), dgr], axis=1)

        @pl.when(first_tile)
        def _():
            glam_ref[...] = gsp[...] * (-_sigmoid(-lam_ref[...]))

    rev = lambda i: nt - 1 - i
    vec = pl.BlockSpec((1, REC_W), lambda i: (0, 0))
    row = pl.BlockSpec((tm, REC_W), lambda i: (rev(i), 0))
    mat = pl.BlockSpec((REC_W, REC_W), lambda i: (0, 0))
    cwb = pl.BlockSpec((8, REC_W), lambda i: (0, 0))
    halo = lambda c: pl.BlockSpec((8, REC_W), lambda i, c=c: (jnp.maximum(rev(i) * hb - 1, 0), c))
    return _call(
        body, name=name, grid=(nt,),
        in_specs=[pl.BlockSpec((tm, REC_W), lambda i: (rev(i), 1)),
                  pl.BlockSpec((tm, REC_W), lambda i: (rev(i), 3)), halo(3),
                  pl.BlockSpec((tm, REC_W), lambda i: (rev(i), 4)),
                  row, row, halo(0), cwb, vec, mat, mat, vec, vec, vec, vec],
        out_specs=[pl.BlockSpec((tm, 2 * REC_W), lambda i: (rev(i), 0)), cwb, vec, mat, mat, vec, vec, vec, vec],
        out_shape=[jax.ShapeDtypeStruct((T, 2 * REC_W), BF16)]
        + [jax.ShapeDtypeStruct((8, REC_W), F32), jax.ShapeDtypeStruct((1, REC_W), F32)]
        + [jax.ShapeDtypeStruct((REC_W, REC_W), F32)] * 2 + [jax.ShapeDtypeStruct((1, REC_W), F32)] * 4,
        scratch_shapes=[pltpu.VMEM((8, REC_W), F32)] * 3 + [pltpu.VMEM((1, REC_W), F32)],
        compiler_params=_params("arbitrary"),
    )(dmix, proj, proj, proj, xc, h, h, cw, cb, wrg, wig, brg, big, lam, g)


def _ffn_conv(x_ext, cw_ref, cb_ref):
    return (cb_ref[...] + cw_ref[2:3, :] * x_ext + cw_ref[1:2, :] * pltpu.roll(x_ext, 1, 0)
            + cw_ref[0:1, :] * pltpu.roll(x_ext, 2, 0))


def _ffn_act(pg, pu, cw, cb, name, tm=512, tc=768):
    T, F = pg.shape
    hb = tm // 8
    nc = F // tc

    def body(g_ref, gh_ref, u_ref, uh_ref, cwg_ref, cwu_ref, cbg_ref, cbu_ref, o_ref):
        first = pl.program_id(0) == 0
        ge = jnp.concatenate([jnp.where(first, 0.0, gh_ref[...]), g_ref[...]], axis=0)
        ue = jnp.concatenate([jnp.where(first, 0.0, uh_ref[...]), u_ref[...]], axis=0)
        act = _gelu(_ffn_conv(ge, cwg_ref, cbg_ref)) * _ffn_conv(ue, cwu_ref, cbu_ref)
        o_ref[...] = act[8:].astype(BF16)

    tile = pl.BlockSpec((tm, tc), lambda i, j: (i, j))
    halo = pl.BlockSpec((8, tc), lambda i, j: (jnp.maximum(i * hb - 1, 0), j))
    cws = lambda off: pl.BlockSpec((8, tc), lambda i, j, off=off: (0, j + off))
    cbs = lambda off: pl.BlockSpec((1, tc), lambda i, j, off=off: (0, j + off))
    return _call(
        body, name=name, grid=(T // tm, nc),
        in_specs=[tile, halo, tile, halo, cws(0), cws(nc), cbs(0), cbs(nc)], out_specs=tile,
        out_shape=jax.ShapeDtypeStruct((T, F), BF16), compiler_params=_params("parallel", "parallel"),
    )(pg, pg, pu, pu, cw, cw, cb, cb)


def _ffn_act_bwd(pg, pu, dact, cw, cb, name, tm=512, tc=768):
    T, F = pg.shape
    nt = T // tm
    hb = tm // 8
    nc = F // tc

    def body(g_ref, gp_ref, gn_ref, u_ref, up_ref, un_ref, d_ref, dn_ref, cwg_ref, cwu_ref, cbg_ref, cbu_ref,
             dg_ref, du_ref, gcwg_ref, gcwu_ref, gcbg_ref, gcbu_ref):
        i = pl.program_id(1)
        first, last = i == 0, i == nt - 1

        @pl.when(first)
        def _():
            for ref in (gcwg_ref, gcwu_ref, gcbg_ref, gcbu_ref):
                ref[...] = jnp.zeros_like(ref)

        ext = lambda p, t, n: jnp.concatenate([jnp.where(first, 0.0, p[...]), t[...], jnp.where(last, 0.0, n[...])], axis=0)
        ge, ue = ext(gp_ref, g_ref, gn_ref), ext(up_ref, u_ref, un_ref)
        de = jnp.concatenate([jnp.zeros((8, tc), F32), d_ref[...], jnp.where(last, 0.0, dn_ref[...])], axis=0)
        gel, dgel = _gelu_and_grad(_ffn_conv(ge, cwg_ref, cbg_ref))
        d_gate = de * _ffn_conv(ue, cwu_ref, cbu_ref) * dgel
        d_up = de * gel
        n = tm + 16
        for dcv, xe, cw_ref, dx_ref, gcw_ref, gcb_ref in ((d_gate, ge, cwg_ref, dg_ref, gcwg_ref, gcbg_ref),
                                                            (d_up, ue, cwu_ref, du_ref, gcwu_ref, gcbu_ref)):
            dx = cw_ref[2:3, :] * dcv + cw_ref[1:2, :] * pltpu.roll(dcv, n - 1, 0) + cw_ref[0:1, :] * pltpu.roll(dcv, n - 2, 0)
            dx_ref[...] = dx[8:tm + 8].astype(BF16)
            dt = dcv[8:tm + 8]
            gcw_ref[2:3, :] += jnp.sum(dt * xe[8:tm + 8], axis=0, keepdims=True)
            gcw_ref[1:2, :] += jnp.sum(dt * pltpu.roll(xe, 1, 0)[8:tm + 8], axis=0, keepdims=True)
            gcw_ref[0:1, :] += jnp.sum(dt * pltpu.roll(xe, 2, 0)[8:tm + 8], axis=0, keepdims=True)
            gcb_ref[...] += jnp.sum(dt, axis=0, keepdims=True)

    tile = pl.BlockSpec((tm, tc), lambda j, i: (i, j))
    prev = pl.BlockSpec((8, tc), lambda j, i: (jnp.maximum(i * hb - 1, 0), j))
    nxt = pl.BlockSpec((8, tc), lambda j, i: (jnp.minimum((i + 1) * hb, nt * hb - 1), j))
    cws = lambda off: pl.BlockSpec((8, tc), lambda j, i, off=off: (0, j + off))
    cbs = lambda off: pl.BlockSpec((1, tc), lambda j, i, off=off: (0, j + off))
    return _call(
        body, name=name, grid=(nc, nt),
        in_specs=[tile, prev, nxt, tile, prev, nxt, tile, nxt, cws(0), cws(nc), cbs(0), cbs(nc)],
        out_specs=[tile, tile, cws(0), cws(0), cbs(0), cbs(0)],
        out_shape=[jax.ShapeDtypeStruct((T, F), BF16)] * 2 + [jax.ShapeDtypeStruct((8, F), F32)] * 2
        + [jax.ShapeDtypeStruct((1, F), F32)] * 2,
        compiler_params=_params("parallel", "arbitrary"),
    )(pg, pg, pg, pu, pu, pu, dact, dact, cw, cw, cb, cb)


def _add_pairs(g, r1, core, name):
    _, r, n = g.shape

    def body(c_ref, g_ref, r_ref, o_ref):
        o_ref[...] = (g_ref[...].astype(F32) + r_ref[...].astype(F32)).astype(BF16)

    spec = pltpu.PrefetchScalarGridSpec(
        num_scalar_prefetch=1, grid=(4,),
        in_specs=[pl.BlockSpec((None, r, n), lambda k, c_ref: (2 * k + c_ref[0], 0, 0)),
                  pl.BlockSpec((None, r, n), lambda k, c_ref: (k, 0, 0))],
        out_specs=pl.BlockSpec((None, r, n), lambda k, c_ref: (k, 0, 0)))
    return _call(body, name=name, grid_spec=spec, out_shape=jax.ShapeDtypeStruct((4, r, n), BF16),
                 compiler_params=_params("parallel"))(core, g, r1)


def _adam_update(w, g, m, v):
    m2 = ADAM_B1 * m + (1.0 - ADAM_B1) * g
    v2 = ADAM_B2 * v + (1.0 - ADAM_B2) * (g * g)
    m_hat = m2 / (1.0 - ADAM_B1 ** ADAM_STEP)
    v_hat = v2 / (1.0 - ADAM_B2 ** ADAM_STEP)
    delta = -ADAM_LR * (m_hat / (jnp.sqrt(v_hat) + ADAM_EPS) + ADAM_WD * w)
    return delta, m2, v2


def _adam_sharded(p, r2, chip, w, m, v, name):
    r, n = w.shape
    tr = _row_tile(r)

    def body(c_ref, p_ref, r_ref, w_ref, m_ref, v_ref, g_ref, d_ref, m2_ref, v2_ref):
        g = p_ref[...].astype(F32) + r_ref[0].astype(F32) + r_ref[1].astype(F32) + r_ref[2].astype(F32)
        g_ref[...] = g
        d_ref[...], m2_ref[...], v2_ref[...] = _adam_update(w_ref[...], g, m_ref[...], v_ref[...])

    row = pl.BlockSpec((tr, n), lambda i, c_ref: (i, 0))
    spec = pltpu.PrefetchScalarGridSpec(
        num_scalar_prefetch=1, grid=(r // tr,),
        in_specs=[pl.BlockSpec((None, tr, n), lambda i, c_ref: (c_ref[0], i, 0)),
                  pl.BlockSpec((3, tr, n), lambda i, c_ref: (0, i, 0)), row, row, row],
        out_specs=[row] * 4)
    return _call(body, name=name, grid_spec=spec, out_shape=[jax.ShapeDtypeStruct((r, n), F32)] * 4,
                 compiler_params=_params("parallel"))(chip, p, r2, w, m, v)


def _sum_devices(allg, name):
    r, n = allg.shape[0] // N_DEV, allg.shape[1]

    def body(a_ref, o_ref):
        acc = a_ref[0:r, :]
        for k in range(1, N_DEV):
            acc = acc + a_ref[k * r:(k + 1) * r, :]
        o_ref[...] = acc

    return _call(body, name=name, out_shape=jax.ShapeDtypeStruct((r, n), F32))(allg)


def _adam_small(w, g, m, v, name):
    def body(w_ref, g_ref, m_ref, v_ref, d_ref, m2_ref, v2_ref):
        d_ref[...], m2_ref[...], v2_ref[...] = _adam_update(w_ref[...], g_ref[...], m_ref[...], v_ref[...])

    return _call(body, name=name, out_shape=[jax.ShapeDtypeStruct(w.shape, F32)] * 3)(w, g, m, v)


_SMALL = (("g_mix", 1024), ("q_norm_g", 64), ("k_norm_g", 64), ("rec_conv_b", 512), ("w_rg", 32768), ("b_rg", 512),
          ("w_ig", 32768), ("b_ig", 512), ("lru_lambda", 512), ("g_attn_out", 512), ("g_rec_out", 512),
          ("g_ffn", 1024), ("ffn_conv_b", 6144))
_SMALL_SHAPES = {"g_mix": (1, 1024), "q_norm_g": (1, 64), "k_norm_g": (1, 64), "rec_conv_b": (1, 512),
                 "w_rg": (1, 8, 64, 64), "b_rg": (1, 8, 64), "w_ig": (1, 8, 64, 64), "b_ig": (1, 8, 64),
                 "lru_lambda": (1, 512), "g_attn_out": (1, 512), "g_rec_out": (1, 512), "g_ffn": (1, 1024),
                 "ffn_conv_b": (1, 6144)}
_N_REPL = sum(n for _, n in _SMALL)
_N_SMALL = _N_REPL + 4 * 64 + 3 * 768
_SMALL_PAD_ROWS = 80


def _pack_small(d, rec_cw, ffn_cw):
    flat = jnp.concatenate([d[k].reshape(-1) for k, _ in _SMALL] + [rec_cw.reshape(-1), ffn_cw.reshape(-1)])
    return jnp.pad(flat, (0, _SMALL_PAD_ROWS * 1024 - _N_SMALL)).reshape(_SMALL_PAD_ROWS, 1024)


def _unpack_small(p):
    flat = p.reshape(-1)
    out, o = {}, 0
    for k, n in _SMALL:
        out[k] = flat[o:o + n].reshape(_SMALL_SHAPES[k])
        o += n
    out["rec_conv_w"] = flat[o:o + 256].reshape(1, 4, 64)
    out["ffn_conv_w"] = flat[o + 256:o + 256 + 2304].reshape(1, 3, 768)
    return out


def _block_diag(w):
    eye = jnp.eye(8, dtype=w.dtype)
    return (w[:, :, None, :] * eye[:, None, :, None]).reshape(512, 512)


def kernel(x, positions, g_mix, w_in, q_norm_g, k_norm_g, rec_conv_w, rec_conv_b, w_rg, b_rg, w_ig, b_ig, lru_lambda, g_attn_out, g_rec_out, w_out, g_ffn, w_up, ffn_conv_w, ffn_conv_b, w_down, loss_target, m_g_mix, m_w_in, m_q_norm_g, m_k_norm_g, m_rec_conv_w, m_rec_conv_b, m_w_rg, m_b_rg, m_w_ig, m_b_ig, m_lru_lambda, m_g_attn_out, m_g_rec_out, m_w_out, m_g_ffn, m_w_up, m_ffn_conv_w, m_ffn_conv_b, m_w_down, v_g_mix, v_w_in, v_q_norm_g, v_k_norm_g, v_rec_conv_w, v_rec_conv_b, v_w_rg, v_b_rg, v_w_ig, v_b_ig, v_lru_lambda, v_g_attn_out, v_g_rec_out, v_w_out, v_g_ffn, v_w_up, v_ffn_conv_w, v_ffn_conv_b, v_w_down):
    T = x.shape[1]
    ix, iy, ic = lax.axis_index("x"), lax.axis_index("y"), lax.axis_index("c")
    dev = 4 * ix + 2 * iy + ic
    core = jnp.reshape(ic, (1,)).astype(jnp.int32)
    chip = jnp.reshape(2 * ix + iy, (1,)).astype(jnp.int32)
    xs = x.reshape(T, D_MODEL)
    tgt = loss_target.reshape(T, D_MODEL)
    pos = positions.reshape(T, 1)

    tr = lambda a: a[0].T
    shards = {"w_in": (tr(w_in), tr(m_w_in), tr(v_w_in)), "w_out": (w_out[0], m_w_out[0], v_w_out[0]),
              "w_up": (tr(w_up), tr(m_w_up), tr(v_w_up)), "w_down": (w_down[0], m_w_down[0], v_w_down[0])}
    taps = jnp.concatenate([rec_conv_w.reshape(-1), ffn_conv_w.reshape(-1), jnp.zeros((4096 - 2560,), F32)]).reshape(8, 512)
    W_inT, taps_all, W_out, W_upT, W_down = _all_gather(
        [shards["w_in"][0].astype(BF16), taps, shards["w_out"][0].astype(BF16), shards["w_up"][0].astype(BF16),
         shards["w_down"][0].astype(BF16)], "ag_weights")
    taps_all = taps_all.reshape(N_DEV, 4096)
    rcw = taps_all[:, :256].reshape(8, 4, 64).transpose(1, 0, 2).reshape(4, REC_W)
    fcw = taps_all[:, 256:2560].reshape(8, 3, 768).transpose(1, 0, 2).reshape(3, 2 * D_FF)
    rcw8 = jnp.pad(rcw, ((0, 4), (0, 0)))
    fcw8 = jnp.pad(fcw, ((0, 5), (0, 0)))
    fcb = ffn_conv_b.reshape(1, 2 * D_FF)

    half = HEAD_DIM // 2
    inv_freq = ROPE_THETA ** (-jnp.arange(half, dtype=F32) / half)
    invf = jnp.tile(inv_freq, 2 * N_HEADS).reshape(1, ATTN_W)
    bd = jnp.asarray(np.kron(np.eye(N_HEADS), np.full((HEAD_DIM, HEAD_DIM), 1.0 / HEAD_DIM)), BF16)
    qg = jnp.tile(q_norm_g.reshape(HEAD_DIM), N_HEADS).reshape(1, ATTN_W)
    kg = jnp.tile(k_norm_g.reshape(HEAD_DIM), N_HEADS).reshape(1, ATTN_W)
    wrg_bd = _block_diag(w_rg[0]).astype(BF16)
    wig_bd = _block_diag(w_ig[0]).astype(BF16)
    brg, big = b_rg.reshape(1, REC_W), b_ig.reshape(1, REC_W)

    h1 = _rmsnorm(xs, g_mix, "norm_mix")
    proj = _mm(h1, W_inT, "nt", F32, "in_proj", tn=1280)
    qf, kf = _qk_prep(proj, pos, invf, qg, kg, bd, "qk_prep")
    attn, lse = _attn_fwd(qf, kf, proj, "attn_fwd")
    mix = _attn_norm(attn, g_attn_out, "attn_norm")
    xc, hstate, mix = _rec_fwd(proj, mix, rcw8, rec_conv_b, wrg_bd, wig_bd, brg, big, lru_lambda, g_rec_out, "rec_fwd")
    x2 = _mm(mix, W_out, "nn", F32, "out_proj", add=xs)

    h2 = _rmsnorm(x2, g_ffn, "norm_ffn")
    pg = _mm(h2, W_upT, "nt", F32, "up_proj_gate", n=D_FF)
    pu = _mm(h2, W_upT, "nt", F32, "up_proj_up", n=D_FF, b_noff=D_FF // 1024)
    act = _ffn_act(pg, pu, fcw8, fcb, "ffn_act")
    y = _mm(act, W_down, "nn", F32, "down_proj", add=x2)
    dy, dyb, lparts = _loss_grad(y, tgt, "loss_grad")
    loss = lax.psum(0.5 / D_MODEL * jnp.sum(lparts), ("x", "y", "c"))

    dact = _mm(dyb, W_down, "nt", F32, "d_act")
    g_down = _mm(act, dyb, "tn", BF16, "g_w_down")
    dpg, dpu, g_fcwg, g_fcwu, g_fcbg, g_fcbu = _ffn_act_bwd(pg, pu, dact, fcw8, fcb, "ffn_act_bwd")
    g_upT = _mm(dpg, h2, "tn", BF16, "g_w_up_gate", o_rows=2 * D_FF)
    g_upT = _mm(dpu, h2, "tn", BF16, "g_w_up_up", into=g_upT, o_moff=D_FF // 1024)
    dh2 = _mm(dpg, W_upT, "nn", F32, "d_h2_gate", k=D_FF)
    dh2 = _mm(dpu, W_upT, "nn", F32, "d_h2_up", k=D_FF, b_koff=D_FF // 1024, add=dh2)
    dx2, dx2b, g_gffn = _rmsnorm_bwd(x2, dh2, dy, g_ffn, "norm_ffn_bwd")

    dmix = _mm(dx2b, W_out, "nt", F32, "d_mix")
    g_out = _mm(mix, dx2b, "tn", BF16, "g_w_out")
    do, delta, g_gattn = _attn_norm_bwd(dmix, attn, g_attn_out, bd, "attn_norm_bwd")
    dqh, dkh, dv = _attn_bwd(qf, kf, proj, do, lse, delta, "attn_bwd")
    dqkv, g_qg, g_kg = _qk_prep_bwd(proj, dqh, dkh, dv, pos, invf, qg, kg, bd, "qk_prep_bwd")
    (drec, g_rcw, g_rcb, g_wrg, g_wig, g_brg, g_big, g_lam, g_grec) = _rec_bwd(
        dmix, proj, xc, hstate, rcw8, rec_conv_b, wrg_bd, wig_bd, brg, big, lru_lambda, g_rec_out, "rec_bwd")
    g_inT = _mm(dqkv, h1, "tn", BF16, "g_w_in_qkv", tm=512, o_rows=IN_W)
    g_inT = _mm(drec, h1, "tn", BF16, "g_w_in_rec", tm=512, into=g_inT, o_moff=3 * ATTN_W // 512)
    dh1 = _mm(dqkv, W_inT, "nn", F32, "d_h1_qkv", tk=512, k=3 * ATTN_W)
    dh1 = _mm(drec, W_inT, "nn", F32, "d_h1_rec", tk=512, k=2 * REC_W, b_koff=3 * ATTN_W // 512, add=dh1)
    grad_x, _, g_gmix = _rmsnorm_bwd(xs, dh1, dx2, g_mix, "norm_mix_bwd")

    names = ("w_in", "w_out", "w_up", "w_down")
    gfull = [g.reshape(N_DEV, g.shape[0] // N_DEV, 1024) for g in (g_inT, g_out, g_upT, g_down)]
    r1 = _exchange_sibling(gfull, "rs_sibling")
    part = [_add_pairs(g, r, core, "rs_add_pairs_" + nm) for g, r, nm in zip(gfull, r1, names)]
    r2 = _exchange_chips(part, "rs_chips")
    big_out = {"grad": {}, "delta": {}, "new_m": {}, "new_v": {}}
    for nm, p, r in zip(names, part, r2):
        w_, m_, v_ = shards[nm]
        res = _adam_sharded(p, r, chip, w_, m_, v_, "adam_" + nm)
        for kind, a in zip(("grad", "delta", "new_m", "new_v"), res):
            big_out[kind][nm] = a.T[None] if nm in ("w_in", "w_up") else a[None]

    blocks = lambda g: jnp.stack([g[64 * n:64 * n + 64, 64 * n:64 * n + 64] for n in range(8)])
    small_g = {
        "g_mix": g_gmix, "q_norm_g": g_qg.reshape(N_HEADS, HEAD_DIM).sum(0), "k_norm_g": g_kg.reshape(N_HEADS, HEAD_DIM).sum(0),
        "rec_conv_b": g_rcb, "w_rg": blocks(g_wrg), "b_rg": g_brg, "w_ig": blocks(g_wig), "b_ig": g_big,
        "lru_lambda": g_lam, "g_attn_out": g_gattn, "g_rec_out": g_grec, "g_ffn": g_gffn,
        "ffn_conv_b": jnp.concatenate([g_fcbg, g_fcbu], axis=1)}
    g_fcw = jnp.concatenate([g_fcwg[:3], g_fcwu[:3]], axis=1)
    flat = jnp.concatenate([small_g[k].reshape(-1) for k, _ in _SMALL] + [g_rcw[:4].reshape(-1), g_fcw.reshape(-1)])
    flat = jnp.pad(flat, (0, SMALL_ROWS * 1024 - flat.shape[0])).reshape(SMALL_ROWS, 1024)
    tot = _sum_devices(_all_gather([flat], "ag_small_grads")[0], "sum_small_grads").reshape(-1)
    g_small, o = {}, 0
    for k, n in _SMALL:
        g_small[k] = tot[o:o + n]
        o += n
    g_rcw_mine = lax.dynamic_slice(tot[o:o + 2048].reshape(4, REC_W), (0, 64 * dev), (4, 64))
    g_fcw_mine = lax.dynamic_slice(tot[o + 2048:o + 2048 + 18432].reshape(3, 2 * D_FF), (0, 768 * dev), (3, 768))
    given = dict(g_mix=g_mix, q_norm_g=q_norm_g, k_norm_g=k_norm_g, rec_conv_b=rec_conv_b, w_rg=w_rg, b_rg=b_rg, w_ig=w_ig,
                 b_ig=b_ig, lru_lambda=lru_lambda, g_attn_out=g_attn_out, g_rec_out=g_rec_out, g_ffn=g_ffn, ffn_conv_b=ffn_conv_b)
    given_m = dict(g_mix=m_g_mix, q_norm_g=m_q_norm_g, k_norm_g=m_k_norm_g, rec_conv_b=m_rec_conv_b, w_rg=m_w_rg, b_rg=m_b_rg,
                   w_ig=m_w_ig, b_ig=m_b_ig, lru_lambda=m_lru_lambda, g_attn_out=m_g_attn_out, g_rec_out=m_g_rec_out,
                   g_ffn=m_g_ffn, ffn_conv_b=m_ffn_conv_b)
    given_v = dict(g_mix=v_g_mix, q_norm_g=v_q_norm_g, k_norm_g=v_k_norm_g, rec_conv_b=v_rec_conv_b, w_rg=v_w_rg, b_rg=v_b_rg,
                   w_ig=v_w_ig, b_ig=v_b_ig, lru_lambda=v_lru_lambda, g_attn_out=v_g_attn_out, g_rec_out=v_g_rec_out,
                   g_ffn=v_g_ffn, ffn_conv_b=v_ffn_conv_b)
    ws = _pack_small(given, rec_conv_w, ffn_conv_w)
    gs = _pack_small(g_small, g_rcw_mine, g_fcw_mine)
    ms = _pack_small(given_m, m_rec_conv_w, m_ffn_conv_w)
    vs = _pack_small(given_v, v_rec_conv_w, v_ffn_conv_w)
    ds, m2s, v2s = _adam_small(ws, gs, ms, vs, "adam_small")
    small_out = {"grad": _unpack_small(gs), "delta": _unpack_small(ds), "new_m": _unpack_small(m2s), "new_v": _unpack_small(v2s)}

    order = ("g_mix", "w_in", "q_norm_g", "k_norm_g", "rec_conv_w", "rec_conv_b", "w_rg", "b_rg", "w_ig", "b_ig",
             "lru_lambda", "g_attn_out", "g_rec_out", "w_out", "g_ffn", "w_up", "ffn_conv_w", "ffn_conv_b", "w_down")
    outs = [loss, grad_x.reshape(1, T, D_MODEL)]
    for kind in ("grad", "delta", "new_m", "new_v"):
        for name in order:
            outs.append(big_out[kind][name] if name in big_out[kind] else small_out[kind][name])
    return tuple(outs)
```

```python
import math

import numpy as np
import jax
import jax.numpy as jnp
from jax import lax
from jax.experimental import pallas as pl
from jax.experimental.pallas import tpu as pltpu

F32 = jnp.float32
BF16 = jnp.bfloat16

D_MODEL = 1024
HEAD_DIM = 64
ATTN_W = 512
REC_W = 512
N_HEADS = 8
D_FF = 3072
IN_W = 2560
REC_CONV = 4
FFN_CONV = 3
LRU_C = 8.0
ROPE_THETA = 10000.0
EPS = 1e-6
NEG_INF = -1e30
QBLK = 128
DILATIONS = (1, 4, 16)
N_DEV = 8
SMALL_ROWS = 96
ADAM_LR, ADAM_B1, ADAM_B2, ADAM_EPS, ADAM_WD, ADAM_STEP = 0.001, 0.9, 0.999, 1e-08, 0.01, 10
MESH = pl.DeviceIdType.MESH
ANY = pl.BlockSpec(memory_space=pl.ANY)


def _call(body, *, name, **kw):
    return pl.pallas_call(body, name=name, **kw)


def _params(*sem):
    return pltpu.CompilerParams(dimension_semantics=sem, vmem_limit_bytes=56 * 1024 * 1024)


def _gelu(x):
    c = math.sqrt(2.0 / math.pi)
    return 0.5 * x * (1.0 + jnp.tanh(c * (x + 0.044715 * (x * x * x))))


def _gelu_and_grad(x):
    c = math.sqrt(2.0 / math.pi)
    t = jnp.tanh(c * (x + 0.044715 * (x * x * x)))
    g = 0.5 * x * (1.0 + t)
    dg = 0.5 * (1.0 + t) + 0.5 * x * (1.0 - t * t) * (c * (1.0 + 3.0 * 0.044715 * (x * x)))
    return g, dg


def _sigmoid(x):
    return 1.0 / (1.0 + jnp.exp(-x))


def _softplus_neg(lam):
    y = jnp.exp(-jnp.abs(lam))
    u = 1.0 + y
    log1p = jnp.where(u == 1.0, y, jnp.log(u) * y / jnp.where(u == 1.0, 1.0, u - 1.0))
    return jnp.maximum(-lam, 0.0) + log1p


_NN = (((1,), (0,)), ((), ()))
_NT = (((1,), (1,)), ((), ()))
_TN = (((0,), (0,)), ((), ()))


def _dot(a, b, dims=_NN):
    return lax.dot_general(a, b, dims, preferred_element_type=F32)


def _group_mean(v, bd):
    hi = v.astype(BF16)
    lo = (v - hi.astype(F32)).astype(BF16)
    return _dot(hi, bd) + _dot(lo, bd)


def _shift_down(x, halo, s):
    rolled = pltpu.roll(x, s, 0)
    hr = pltpu.roll(halo, s, 0)
    row = lax.broadcasted_iota(jnp.int32, hr.shape, 0)
    first = jnp.where(row < s, hr, rolled[:8])
    return jnp.concatenate([first, rolled[8:]], axis=0)


def _shift_up(x, halo, s):
    n = x.shape[0]
    rolled = pltpu.roll(x, n - s, 0)
    hr = pltpu.roll(halo, 8 - s, 0)
    row = lax.broadcasted_iota(jnp.int32, hr.shape, 0)
    last = jnp.where(row >= 8 - s, hr, rolled[n - 8:])
    return jnp.concatenate([rolled[:n - 8], last], axis=0)


def _scan_fwd(a, u):
    n = a.shape[0]
    row = lax.broadcasted_iota(jnp.int32, a.shape, 0)
    s = 1
    while s < n:
        a_s = jnp.where(row < s, 1.0, pltpu.roll(a, s, 0))
        u_s = jnp.where(row < s, 0.0, pltpu.roll(u, s, 0))
        u = u + a * u_s
        a = a * a_s
        s *= 2
    return a, u


def _scan_bwd(b, v):
    n = b.shape[0]
    row = lax.broadcasted_iota(jnp.int32, b.shape, 0)
    s = 1
    while s < n:
        b_s = jnp.where(row >= n - s, 1.0, pltpu.roll(b, n - s, 0))
        v_s = jnp.where(row >= n - s, 0.0, pltpu.roll(v, n - s, 0))
        v = v + b * v_s
        b = b * b_s
        s *= 2
    return b, v


def _rot_half(y):
    n = y.shape[1]
    lane = lax.broadcasted_iota(jnp.int32, y.shape, 1) & (HEAD_DIM - 1)
    return jnp.where(lane < HEAD_DIM // 2, -pltpu.roll(y, n - HEAD_DIM // 2, 1), pltpu.roll(y, HEAD_DIM // 2, 1))


def _row_tile(r, cap=256):
    return max(t for t in range(16, cap + 1, 16) if r % t == 0)


def _all_gather(shards, name):
    na = len(shards)
    ms = [s.shape[0] for s in shards]

    def body(*refs):
        x_refs, out_refs = refs[:na], refs[na:2 * na]
        send_sems, recv_sems, local_sems = refs[2 * na:]
        x, y, c = lax.axis_index("x"), lax.axis_index("y"), lax.axis_index("c")
        me, sibling = (x, y, c), (x, y, 1 - c)
        chips = [(1 - x, y), (x, 1 - y), (1 - x, 1 - y)]

        def rows(a, px, py, pc):
            return out_refs[a].at[pl.ds((4 * px + 2 * py + pc) * ms[a], ms[a]), :]

        def copy(a, k, block, to, src=None):
            return pltpu.make_async_remote_copy(
                src_ref=rows(a, *block) if src is None else src, dst_ref=rows(a, *block),
                send_sem=send_sems.at[7 * a + k], recv_sem=recv_sems.at[7 * a + k], device_id=to, device_id_type=MESH)

        mine = [pltpu.make_async_copy(x_refs[a], rows(a, *me), local_sems.at[a]) for a in range(na)]
        first = []
        for a in range(na):
            mine[a].start()
            first.append(copy(a, 0, me, sibling, src=x_refs[a]))
            first += [copy(a, 1 + j, me, (*chip, c), src=x_refs[a]) for j, chip in enumerate(chips)]
        for cp in first:
            cp.start()
        passed = []
        for a in range(na):
            for j, chip in enumerate(chips):
                copy(a, 1 + j, (*chip, c), me).wait_recv()
                fw = copy(a, 4 + j, (*chip, c), sibling)
                fw.start()
                passed.append(fw)
        for a in range(na):
            copy(a, 0, sibling, me).wait_recv()
            for j, chip in enumerate(chips):
                copy(a, 4 + j, (*chip, 1 - c), me).wait_recv()
        for cp in first + passed:
            cp.wait_send()
        for cp in mine:
            cp.wait()

    return _call(
        body, name=name, out_shape=[jax.ShapeDtypeStruct((N_DEV * s.shape[0], s.shape[1]), s.dtype) for s in shards],
        in_specs=[ANY] * na, out_specs=[ANY] * na,
        scratch_shapes=[pltpu.SemaphoreType.DMA((7 * na,)), pltpu.SemaphoreType.DMA((7 * na,)),
                        pltpu.SemaphoreType.DMA((na,))],
    )(*shards)


def _exchange_sibling(gs, name):
    na = len(gs)

    def body(*refs):
        g_refs, out_refs = refs[:na], refs[na:2 * na]
        send_sems, recv_sems = refs[2 * na:]
        x, y, c = lax.axis_index("x"), lax.axis_index("y"), lax.axis_index("c")
        copies = [
            pltpu.make_async_remote_copy(
                src_ref=g_refs[a].at[2 * k + (1 - c)], dst_ref=out_refs[a].at[k],
                send_sem=send_sems.at[4 * a + k], recv_sem=recv_sems.at[4 * a + k],
                device_id=(x, y, 1 - c), device_id_type=MESH)
            for a in range(na) for k in range(4)]
        for cp in copies:
            cp.start()
        for cp in copies:
            cp.wait()

    return _call(
        body, name=name, out_shape=[jax.ShapeDtypeStruct((4,) + g.shape[1:], g.dtype) for g in gs],
        in_specs=[ANY] * na, out_specs=[ANY] * na,
        scratch_shapes=[pltpu.SemaphoreType.DMA((4 * na,)), pltpu.SemaphoreType.DMA((4 * na,))],
    )(*gs)


def _exchange_chips(ps, name):
    na = len(ps)

    def body(*refs):
        p_refs, out_refs = refs[:na], refs[na:2 * na]
        send_sems, recv_sems = refs[2 * na:]
        x, y, c = lax.axis_index("x"), lax.axis_index("y"), lax.axis_index("c")
        chips = [(1 - x, y), (x, 1 - y), (1 - x, 1 - y)]
        copies = [
            pltpu.make_async_remote_copy(
                src_ref=p_refs[a].at[2 * cx + cy], dst_ref=out_refs[a].at[k],
                send_sem=send_sems.at[3 * a + k], recv_sem=recv_sems.at[3 * a + k],
                device_id=(cx, cy, c), device_id_type=MESH)
            for a in range(na) for k, (cx, cy) in enumerate(chips)]
        for cp in copies:
            cp.start()
        for cp in copies:
            cp.wait()

    return _call(
        body, name=name, out_shape=[jax.ShapeDtypeStruct((3,) + p.shape[1:], p.dtype) for p in ps],
        in_specs=[ANY] * na, out_specs=[ANY] * na,
        scratch_shapes=[pltpu.SemaphoreType.DMA((3 * na,)), pltpu.SemaphoreType.DMA((3 * na,))],
    )(*ps)


HBM = pl.BlockSpec(memory_space=pltpu.HBM)
SEM = pl.BlockSpec(memory_space=pltpu.SEMAPHORE)
EFFECT = pltpu.SideEffectType.DATAFLOW_SIDE_EFFECTING
N_PEERS = N_DEV - 1


def _peer(k):
    x, y, c = lax.axis_index("x"), lax.axis_index("y"), lax.axis_index("c")
    b = k + 1
    flip = lambda v, bit: 1 - v if bit else v
    return flip(x, b & 4), flip(y, b & 2), flip(c, b & 1)


def _in_hbm(a):
    return pltpu.with_memory_space_constraint(a, pltpu.HBM)


def _split_copy_descr(na, kind, src_refs, land_refs, send_sems, recv_sems):
    x, y, c = lax.axis_index("x"), lax.axis_index("y"), lax.axis_index("c")
    me = 4 * x + 2 * y + c
    copies = []
    for a in range(na):
        for k in range(N_PEERS):
            px, py, pc = _peer(k)
            if kind == "gather":
                m = src_refs[a].shape[0]
                src, dst = src_refs[a], land_refs[a].at[pl.ds(me * m, m), :]
            else:
                src, dst = src_refs[a].at[4 * px + 2 * py + pc], land_refs[a].at[k]
            copies.append(pltpu.make_async_remote_copy(
                src_ref=src, dst_ref=dst, send_sem=send_sems.at[N_PEERS * a + k], recv_sem=recv_sems.at[N_PEERS * a + k],
                device_id=(px, py, pc), device_id_type=MESH))
    return copies


def _exchange_start(srcs, land_shapes, kind, after, name):
    na = len(srcs)

    def body(*refs):
        src_refs, land_refs = refs[:na], refs[na:2 * na]
        send_sems, recv_sems = refs[2 * na + 1], refs[2 * na + 2]
        token = refs[-1]
        for cp in _split_copy_descr(na, kind, src_refs, land_refs, send_sems, recv_sems):
            cp.start()
        token[...] = jnp.zeros_like(token)

    lands = [_in_hbm(lax.empty(s, srcs[0].dtype)) for s in land_shapes]
    sem = pltpu.SemaphoreType.DMA((N_PEERS * na,))
    outs = _call(
        body, name=name,
        out_shape=[sem, sem] + [pltpu.HBM(s.shape, s.dtype) for s in srcs] + [pltpu.HBM(s, srcs[0].dtype) for s in land_shapes]
        + [jax.ShapeDtypeStruct((8, 128), F32)],
        in_specs=[HBM] * (2 * na) + [ANY], out_specs=[SEM, SEM] + [HBM] * (2 * na) + [pl.BlockSpec(memory_space=pltpu.VMEM)],
        input_output_aliases={i: 2 + i for i in range(2 * na)},
        compiler_params=pltpu.CompilerParams(has_side_effects=EFFECT),
    )(*[_in_hbm(s) for s in srcs], *lands, after)
    return outs[0], outs[1], outs[2:2 + na], outs[2 + na:2 + 2 * na], outs[-1]


def _exchange_wait(send_sems, recv_sems, srcs, lands, kind, after, name):
    na = len(srcs)

    def body(*refs):
        src_refs, land_refs = refs[:na], refs[na:2 * na]
        s_sems, r_sems = refs[2 * na], refs[2 * na + 1]
        for cp in _split_copy_descr(na, kind, src_refs, land_refs, s_sems, r_sems):
            cp.wait_send()
            cp.wait_recv()

    outs = _call(
        body, name=name, out_shape=[pltpu.HBM(s.shape, s.dtype) for s in srcs] + [pltpu.HBM(l.shape, l.dtype) for l in lands],
        in_specs=[HBM] * (2 * na) + [SEM, SEM, ANY], out_specs=[HBM] * (2 * na),
        input_output_aliases={i: i for i in range(2 * na)},
        compiler_params=pltpu.CompilerParams(has_side_effects=EFFECT),
    )(*srcs, *lands, send_sems, recv_sems, after)
    return outs[:na], outs[na:]


def _mm(a, b, mode, out_dtype, name, add=None, tm=1024, tn=1024, tk=1024, b_noff=0, b_koff=0,
        n=None, k=None, into=None, o_rows=None, o_moff=0):
    if mode == "tn":
        K, M = a.shape
    else:
        M, K = a.shape
    N = n if n is not None else (b.shape[0] if mode == "nt" else b.shape[1])
    if k is not None:
        assert k == K
    tm, tn, tk = min(tm, M), min(tn, N), min(tk, K)
    assert M % tm == 0 and N % tn == 0 and K % tk == 0, (name, M, N, K)
    nk = K // tk
    if mode == "nn":
        a_spec = pl.BlockSpec((tm, tk), lambda i, j, kk: (i, kk))
        b_spec, dims = pl.BlockSpec((tk, tn), lambda i, j, kk: (kk + b_koff, j + b_noff)), _NN
    elif mode == "nt":
        a_spec = pl.BlockSpec((tm, tk), lambda i, j, kk: (i, kk))
        b_spec, dims = pl.BlockSpec((tn, tk), lambda i, j, kk: (j + b_noff, kk + b_koff)), _NT
    else:
        a_spec = pl.BlockSpec((tk, tm), lambda i, j, kk: (kk, i))
        b_spec, dims = pl.BlockSpec((tk, tn), lambda i, j, kk: (kk + b_koff, j + b_noff)), _TN
    o_spec = pl.BlockSpec((tm, tn), lambda i, j, kk: (i + o_moff, j))
    has_add, has_into = add is not None, into is not None

    def body(*refs):
        a_ref, b_ref = refs[0], refs[1]
        add_ref = refs[2] if has_add else None
        o_ref, acc = refs[-2], refs[-1]
        kk = pl.program_id(2)

        @pl.when(kk == 0)
        def _():
            acc[...] = jnp.zeros_like(acc)

        acc[...] += _dot(a_ref[...], b_ref[...], dims)

        @pl.when(kk == nk - 1)
        def _():
            r = acc[...]
            if has_add:
                r = r + add_ref[...]
            o_ref[...] = r.astype(out_dtype)

    ins = [a, b] + ([add] if has_add else []) + ([into] if has_into else [])
    specs = [a_spec, b_spec] + ([pl.BlockSpec((tm, tn), lambda i, j, kk: (i, j))] if has_add else []) + ([ANY] if has_into else [])
    rows = into.shape[0] if has_into else (o_rows if o_rows is not None else M)
    return _call(
        body, name=name, grid=(M // tm, N // tn, nk), in_specs=specs, out_specs=o_spec,
        out_shape=jax.ShapeDtypeStruct((rows, N), out_dtype), scratch_shapes=[pltpu.VMEM((tm, tn), F32)],
        input_output_aliases={len(ins) - 1: 0} if has_into else {},
        compiler_params=_params("parallel", "parallel", "arbitrary"),
    )(*ins)


def _rmsnorm(x, g, name, tm=512):
    T, D = x.shape

    def body(x_ref, g_ref, o_ref):
        xv = x_ref[...]
        r = lax.rsqrt(jnp.mean(xv * xv, axis=-1, keepdims=True) + EPS)
        o_ref[...] = (xv * r * g_ref[...]).astype(BF16)

    return _call(
        body, name=name, grid=(T // tm,),
        in_specs=[pl.BlockSpec((tm, D), lambda i: (i, 0)), pl.BlockSpec((1, D), lambda i: (0, 0))],
        out_specs=pl.BlockSpec((tm, D), lambda i: (i, 0)), out_shape=jax.ShapeDtypeStruct((T, D), BF16),
        compiler_params=_params("parallel"),
    )(x, g)


def _rmsnorm_bwd(x, dh, resid, g, name, tm=512):
    T, D = x.shape

    def body(x_ref, dh_ref, res_ref, g_ref, dx_ref, dxb_ref, dg_ref):
        @pl.when(pl.program_id(0) == 0)
        def _():
            dg_ref[...] = jnp.zeros_like(dg_ref)

        xv, dhv = x_ref[...], dh_ref[...]
        r = lax.rsqrt(jnp.mean(xv * xv, axis=-1, keepdims=True) + EPS)
        gd = dhv * g_ref[...]
        m = jnp.mean(gd * xv, axis=-1, keepdims=True)
        dx = res_ref[...] + r * gd - xv * (r * r * r) * m
        dx_ref[...] = dx
        dxb_ref[...] = dx.astype(BF16)
        dg_ref[...] += jnp.sum(dhv * xv * r, axis=0, keepdims=True)

    row = pl.BlockSpec((tm, D), lambda i: (i, 0))
    vec = pl.BlockSpec((1, D), lambda i: (0, 0))
    return _call(
        body, name=name, grid=(T // tm,), in_specs=[row, row, row, vec], out_specs=[row, row, vec],
        out_shape=[jax.ShapeDtypeStruct((T, D), F32), jax.ShapeDtypeStruct((T, D), BF16), jax.ShapeDtypeStruct((1, D), F32)],
        compiler_params=_params("arbitrary"),
    )(x, dh, resid, g)


def _loss_grad(y, target, name, tm=512):
    T, D = y.shape

    def body(y_ref, t_ref, dy_ref, dyb_ref, l_ref):
        e = y_ref[...] - t_ref[...]
        dy = e * (1.0 / D)
        dy_ref[...] = dy
        dyb_ref[...] = dy.astype(BF16)
        l_ref[...] = jnp.sum(e * e, axis=0, keepdims=True)[None]

    row = pl.BlockSpec((tm, D), lambda i: (i, 0))
    return _call(
        body, name=name, grid=(T // tm,), in_specs=[row, row],
        out_specs=[row, row, pl.BlockSpec((1, 1, D), lambda i: (i, 0, 0))],
        out_shape=[jax.ShapeDtypeStruct((T, D), F32), jax.ShapeDtypeStruct((T, D), BF16),
                   jax.ShapeDtypeStruct((T // tm, 1, D), F32)],
        compiler_params=_params("parallel"),
    )(y, target)


def _qk_prep(proj, pos, invf, qg, kg, bd, name, tm=512):
    T = proj.shape[0]

    def body(q_ref, k_ref, pos_ref, invf_ref, qg_ref, kg_ref, bd_ref, qo_ref, ko_ref):
        ang = pos_ref[...].astype(F32) * invf_ref[...]
        cos, sin = jnp.cos(ang), jnp.sin(ang)

        def prep(xv, gv, scale):
            r = lax.rsqrt(_group_mean(xv * xv, bd_ref[...]) + EPS)
            yv = xv * r * gv
            return ((yv * cos + _rot_half(yv) * sin) * scale).astype(BF16).astype(F32)

        qo_ref[...] = prep(q_ref[...], qg_ref[...], HEAD_DIM ** -0.5)
        ko_ref[...] = prep(k_ref[...], kg_ref[...], 1.0)

    col = lambda j: pl.BlockSpec((tm, ATTN_W), lambda i, j=j: (i, j))
    vec = pl.BlockSpec((1, ATTN_W), lambda i: (0, 0))
    out = pl.BlockSpec((tm, ATTN_W), lambda i: (i, 0))
    return _call(
        body, name=name, grid=(T // tm,),
        in_specs=[col(0), col(1), pl.BlockSpec((tm, 1), lambda i: (i, 0)), vec, vec, vec,
                  pl.BlockSpec((ATTN_W, ATTN_W), lambda i: (0, 0))],
        out_specs=[out, out], out_shape=[jax.ShapeDtypeStruct((T, ATTN_W), F32)] * 2,
        compiler_params=_params("parallel"),
    )(proj, proj, pos, invf, qg, kg, bd)


def _qk_prep_bwd(proj, dqh, dkh, dv, pos, invf, qg, kg, bd, name, tm=512):
    T = proj.shape[0]

    def body(q_ref, k_ref, dq_ref, dk_ref, dv_ref, pos_ref, invf_ref, qg_ref, kg_ref, bd_ref, o_ref, gq_ref, gk_ref):
        @pl.when(pl.program_id(0) == 0)
        def _():
            gq_ref[...] = jnp.zeros_like(gq_ref)
            gk_ref[...] = jnp.zeros_like(gk_ref)

        ang = pos_ref[...].astype(F32) * invf_ref[...]
        cos, sin = jnp.cos(ang), jnp.sin(ang)

        def back(xv, gv, dz, scale):
            dz = dz * scale
            dy = dz * cos - _rot_half(dz * sin)
            r = lax.rsqrt(_group_mean(xv * xv, bd_ref[...]) + EPS)
            gd = dy * gv
            m = _group_mean(gd * xv, bd_ref[...])
            dx = r * gd - xv * (r * r * r) * m
            return dx, jnp.sum(dy * xv * r, axis=0, keepdims=True)

        dxq, gs = back(q_ref[...], qg_ref[...], dq_ref[...], HEAD_DIM ** -0.5)
        gq_ref[...] += gs
        dxk, gs = back(k_ref[...], kg_ref[...], dk_ref[...], 1.0)
        gk_ref[...] += gs
        o_ref[...] = jnp.concatenate([dxq.astype(BF16), dxk.astype(BF16), dv_ref[...].astype(BF16)], axis=1)

    col = lambda j: pl.BlockSpec((tm, ATTN_W), lambda i, j=j: (i, j))
    row = pl.BlockSpec((tm, ATTN_W), lambda i: (i, 0))
    vec = pl.BlockSpec((1, ATTN_W), lambda i: (0, 0))
    return _call(
        body, name=name, grid=(T // tm,),
        in_specs=[col(0), col(1), row, row, row, pl.BlockSpec((tm, 1), lambda i: (i, 0)), vec, vec, vec,
                  pl.BlockSpec((ATTN_W, ATTN_W), lambda i: (0, 0))],
        out_specs=[pl.BlockSpec((tm, 3 * ATTN_W), lambda i: (i, 0)), vec, vec],
        out_shape=[jax.ShapeDtypeStruct((T, 3 * ATTN_W), BF16)] + [jax.ShapeDtypeStruct((1, ATTN_W), F32)] * 2,
        compiler_params=_params("arbitrary"),
    )(proj, proj, dqh, dkh, dv, pos, invf, qg, kg, bd)


def _ld(ref, start, size, dil):
    return ref[pl.ds(start, size), :] if dil == 1 else ref[pl.ds(start, size, stride=dil), :]


def _st(ref, start, size, dil, val):
    if dil == 1:
        ref[pl.ds(start, size), :] = val
    else:
        ref[pl.ds(start, size, stride=dil), :] = val


def _attn_geometry(T, dil):
    nb = T // dil // QBLK
    kw = 2 * QBLK if nb >= 2 else QBLK
    return nb, kw


ATTN_UNROLL = 4


def _attn_unit(j, u, dil, nit):
    return ATTN_UNROLL * j + u if dil >= ATTN_UNROLL else j + u * (nit // ATTN_UNROLL)


def _attn_block(it, dil, kw):
    c, n = it & (dil - 1), lax.shift_right_logical(it, dil.bit_length() - 1)
    sq = n * (QBLK * dil) + c
    sk = jnp.maximum(n - (kw // QBLK - 1), 0) * (QBLK * dil) + c
    qi = lax.broadcasted_iota(jnp.int32, (2 * QBLK, kw), 0) & (QBLK - 1)
    kj = lax.broadcasted_iota(jnp.int32, (2 * QBLK, kw), 1)
    rel = jnp.where(n > 0, kw - QBLK, 0) + qi - kj
    return sq, sk, (rel >= 0) & (rel <= QBLK)


def _stack_heads(xv, head0):
    z = jnp.zeros_like(xv)
    return jnp.concatenate([jnp.where(head0, xv, z), jnp.where(head0, z, xv)], axis=0)


def _unstack_heads(x2, head0):
    return jnp.where(head0, x2[:QBLK], x2[QBLK:])


def _attn_fwd(qf, kf, proj, name):
    T = qf.shape[0]

    def body(q_ref, k_ref, v_ref, o_ref, lse_ref):
        head0 = lax.broadcasted_iota(jnp.int32, (QBLK, 2 * HEAD_DIM), 1) < HEAD_DIM
        for bi, dil in enumerate(DILATIONS):
            nb, kw = _attn_geometry(T, dil)

            nit = nb * dil

            def step(j, carry, bi=bi, dil=dil, kw=kw, nit=nit):
                units = []
                for u in range(ATTN_UNROLL):
                    sq, sk, ok = _attn_block(_attn_unit(j, u, dil, nit), dil, kw)
                    old = (_ld(o_ref, sq, QBLK, dil), _ld(lse_ref, sq, QBLK, dil)) if bi > 0 else None
                    units.append((sq, ok, _ld(q_ref, sq, QBLK, dil).astype(BF16), _ld(k_ref, sk, kw, dil).astype(BF16),
                                  _ld(v_ref, sk, kw, dil).astype(BF16), old))
                results = []
                for sq, ok, qv, kv, vv, old in units:
                    s = jnp.where(ok, _dot(_stack_heads(qv, head0), kv, _NT), NEG_INF)
                    m = jnp.max(s, axis=-1, keepdims=True)
                    p = jnp.exp(s - m).astype(BF16)
                    acc = _dot(p, jnp.concatenate([vv, jnp.ones_like(vv)], axis=1))
                    l = acc[:, 2 * HEAD_DIM:]
                    o_new = _unstack_heads(acc[:, :2 * HEAD_DIM] / l, head0)
                    l_new = _unstack_heads(m + jnp.log(l), head0)
                    if bi > 0:
                        o_old, l_old = old
                        mx = jnp.maximum(l_old, l_new)
                        e0, e1 = jnp.exp(l_old - mx), jnp.exp(l_new - mx)
                        z = e0 + e1
                        o_new = (e0 * o_old + e1 * o_new) / z
                        l_new = mx + jnp.log(z)
                    results.append((sq, o_new, l_new))
                for sq, o_new, l_new in results:
                    _st(o_ref, sq, QBLK, dil, o_new)
                    _st(lse_ref, sq, QBLK, dil, l_new)
                return carry

            lax.fori_loop(0, nit // ATTN_UNROLL, step, 0)

    blk = lambda off: pl.BlockSpec((T, 2 * HEAD_DIM), lambda hp, off=off: (0, off + hp))
    return _call(
        body, name=name, grid=(4,), in_specs=[blk(0), blk(0), blk(8)], out_specs=[blk(0), blk(0)],
        out_shape=[jax.ShapeDtypeStruct((T, ATTN_W), F32)] * 2, compiler_params=_params("parallel"),
    )(qf, kf, proj)


def _attn_bwd(qf, kf, proj, do, lse, delta, name):
    T = qf.shape[0]

    def body(q_ref, k_ref, v_ref, do_ref, lse_ref, dl_ref, dq_ref, dk_ref, dv_ref):
        head0 = lax.broadcasted_iota(jnp.int32, (QBLK, 2 * HEAD_DIM), 1) < HEAD_DIM
        for ref in (dq_ref, dk_ref, dv_ref):
            ref[...] = jnp.zeros_like(ref)
        for dil in DILATIONS:
            nb, kw = _attn_geometry(T, dil)

            nit = nb * dil

            def step(j, carry, dil=dil, kw=kw, nit=nit):
                units = []
                for u in range(ATTN_UNROLL):
                    sq, sk, ok = _attn_block(_attn_unit(j, u, dil, nit), dil, kw)
                    lsev, dlv = _ld(lse_ref, sq, QBLK, dil), _ld(dl_ref, sq, QBLK, dil)
                    units.append((sq, sk, ok, _ld(q_ref, sq, QBLK, dil).astype(BF16), _ld(do_ref, sq, QBLK, dil).astype(BF16),
                                  jnp.concatenate([lsev[:, 0:1], lsev[:, HEAD_DIM:HEAD_DIM + 1]], axis=0),
                                  jnp.concatenate([dlv[:, 0:1], dlv[:, HEAD_DIM:HEAD_DIM + 1]], axis=0),
                                  _ld(k_ref, sk, kw, dil).astype(BF16), _ld(v_ref, sk, kw, dil).astype(BF16),
                                  _ld(dq_ref, sq, QBLK, dil), _ld(dk_ref, sk, kw, dil), _ld(dv_ref, sk, kw, dil)))
                results = []
                for sq, sk, ok, qv, dov, lse2, dl2, kv, vv, dq0, dk0, dv0 in units:
                    q2, do2 = _stack_heads(qv, head0), _stack_heads(dov, head0)
                    p = jnp.where(ok, jnp.exp(_dot(q2, kv, _NT) - lse2), 0.0)
                    ds = (p * (_dot(do2, vv, _NT) - dl2)).astype(BF16)
                    results.append((sq, sk, dq0 + _unstack_heads(_dot(ds, kv), head0),
                                    dk0 + _dot(ds, q2, _TN), dv0 + _dot(p.astype(BF16), do2, _TN)))
                for sq, sk, dq, dk, dv in results:
                    _st(dq_ref, sq, QBLK, dil, dq)
                    _st(dk_ref, sk, kw, dil, dk)
                    _st(dv_ref, sk, kw, dil, dv)
                return carry

            lax.fori_loop(0, nit // ATTN_UNROLL, step, 0)

    blk = lambda off: pl.BlockSpec((T, 2 * HEAD_DIM), lambda hp, off=off: (0, off + hp))
    return _call(
        body, name=name, grid=(4,), in_specs=[blk(0), blk(0), blk(8), blk(0), blk(0), blk(0)], out_specs=[blk(0)] * 3,
        out_shape=[jax.ShapeDtypeStruct((T, ATTN_W), F32)] * 3, compiler_params=_params("parallel"),
    )(qf, kf, proj, do, lse, delta)


def _attn_norm(attn, g, name, tm=512):
    T = attn.shape[0]

    def body(a_ref, g_ref, o_ref):
        av = a_ref[...]
        r = lax.rsqrt(jnp.mean(av * av, axis=-1, keepdims=True) + EPS)
        o_ref[...] = (av * r * g_ref[...]).astype(BF16)

    row = pl.BlockSpec((tm, ATTN_W), lambda i: (i, 0))
    return _call(
        body, name=name, grid=(T // tm,), in_specs=[row, pl.BlockSpec((1, ATTN_W), lambda i: (0, 0))], out_specs=row,
        out_shape=jax.ShapeDtypeStruct((T, 2 * ATTN_W), BF16), compiler_params=_params("parallel"),
    )(attn, g)


def _attn_norm_bwd(dmix, attn, g, bd, name, tm=512):
    T = attn.shape[0]

    def body(d_ref, a_ref, g_ref, bd_ref, do_ref, dl_ref, dg_ref):
        @pl.when(pl.program_id(0) == 0)
        def _():
            dg_ref[...] = jnp.zeros_like(dg_ref)

        dy, av = d_ref[...], a_ref[...]
        r = lax.rsqrt(jnp.mean(av * av, axis=-1, keepdims=True) + EPS)
        gd = dy * g_ref[...]
        m = jnp.mean(gd * av, axis=-1, keepdims=True)
        da = r * gd - av * (r * r * r) * m
        do_ref[...] = da
        dl_ref[...] = _group_mean(da * av, bd_ref[...]) * float(HEAD_DIM)
        dg_ref[...] += jnp.sum(dy * av * r, axis=0, keepdims=True)

    row = pl.BlockSpec((tm, ATTN_W), lambda i: (i, 0))
    vec = pl.BlockSpec((1, ATTN_W), lambda i: (0, 0))
    return _call(
        body, name=name, grid=(T // tm,),
        in_specs=[row, row, vec, pl.BlockSpec((ATTN_W, ATTN_W), lambda i: (0, 0))], out_specs=[row, row, vec],
        out_shape=[jax.ShapeDtypeStruct((T, ATTN_W), F32)] * 2 + [jax.ShapeDtypeStruct((1, ATTN_W), F32)],
        compiler_params=_params("arbitrary"),
    )(dmix, attn, g, bd)


def _rec_gates(xc, wrg_ref, wig_ref, brg_ref, big_ref, lam_ref):
    xb = xc.astype(BF16)
    r = _sigmoid(_dot(xb, wrg_ref[...]) + brg_ref[...])
    ig = _sigmoid(_dot(xb, wig_ref[...]) + big_ref[...])
    sp = _softplus_neg(lam_ref[...])
    log_a = -LRU_C * r * sp
    a = jnp.exp(log_a)
    th = jnp.tanh(log_a)
    mult = jnp.sqrt(-2.0 * th / (1.0 - th))
    return xb, r, ig, sp, a, mult


def _rec_fwd(proj, mix, cw, cb, wrg, wig, brg, big, lam, g, name, tm=256):
    T = proj.shape[0]
    hb = tm // 8

    def body(xr_ref, halo_ref, gr_ref, cw_ref, cb_ref, wrg_ref, wig_ref, brg_ref, big_ref, lam_ref, g_ref, mix_ref,
             xc_ref, h_ref, out_ref, carry):
        i = pl.program_id(0)

        @pl.when(i == 0)
        def _():
            carry[...] = jnp.zeros_like(carry)

        xr = xr_ref[...]
        halo = jnp.where(i > 0, halo_ref[...], 0.0)
        xc = cb_ref[...] + cw_ref[3:4, :] * xr
        for s in range(1, REC_CONV):
            xc = xc + cw_ref[3 - s:4 - s, :] * _shift_down(xr, halo, s)
        xc_ref[...] = xc
        _, _, ig, _, a, mult = _rec_gates(xc, wrg_ref, wig_ref, brg_ref, big_ref, lam_ref)
        pa, hl = _scan_fwd(a, mult * (ig * xc))
        h = hl + pa * carry[0:1, :]
        h_ref[...] = h
        carry[0:1, :] = h_ref[pl.ds(tm - 1, 1), :]
        hg = h * _gelu(gr_ref[...])
        r = lax.rsqrt(jnp.mean(hg * hg, axis=-1, keepdims=True) + EPS)
        out_ref[...] = (hg * r * g_ref[...]).astype(BF16)

    vec = pl.BlockSpec((1, REC_W), lambda i: (0, 0))
    row = pl.BlockSpec((tm, REC_W), lambda i: (i, 0))
    mat = pl.BlockSpec((REC_W, REC_W), lambda i: (0, 0))
    return _call(
        body, name=name, grid=(T // tm,),
        in_specs=[pl.BlockSpec((tm, REC_W), lambda i: (i, 3)),
                  pl.BlockSpec((8, REC_W), lambda i: (jnp.maximum(i * hb - 1, 0), 3)),
                  pl.BlockSpec((tm, REC_W), lambda i: (i, 4)),
                  pl.BlockSpec((8, REC_W), lambda i: (0, 0)), vec, mat, mat, vec, vec, vec, vec, ANY],
        out_specs=[row, row, pl.BlockSpec((tm, REC_W), lambda i: (i, 1))],
        out_shape=[jax.ShapeDtypeStruct((T, REC_W), F32)] * 2 + [jax.ShapeDtypeStruct(mix.shape, BF16)],
        scratch_shapes=[pltpu.VMEM((8, REC_W), F32)], input_output_aliases={11: 2},
        compiler_params=_params("arbitrary"),
    )(proj, proj, proj, cw, cb, wrg, wig, brg, big, lam, g, mix)


def _rec_bwd(dmix, proj, xc, h, cw, cb, wrg, wig, brg, big, lam, g, name, tm=256):
    T = proj.shape[0]
    nt = T // tm
    hb = tm // 8

    def body(d_ref, xr_ref, xhalo_ref, gr_ref, xc_ref, h_ref, hhalo_ref, cw_ref, cb_ref, wrg_ref, wig_ref, brg_ref,
             big_ref, lam_ref, g_ref,
             drec_ref, gcw_ref, gcb_ref, gwrg_ref, gwig_ref, gbrg_ref, gbig_ref, glam_ref, gg_ref,
             g_carry, a_first, dxc_next, gsp):
        i = pl.program_id(0)
        first_tile = i == nt - 1

        @pl.when(i == 0)
        def _():
            for ref in (gcw_ref, gcb_ref, gwrg_ref, gwig_ref, gbrg_ref, gbig_ref, glam_ref, gg_ref,
                        g_carry, a_first, dxc_next, gsp):
                ref[...] = jnp.zeros_like(ref)

        xr, xc, hv = xr_ref[...], xc_ref[...], h_ref[...]
        xhalo = jnp.where(first_tile, 0.0, xhalo_ref[...])
        hhalo = jnp.where(first_tile, 0.0, hhalo_ref[...])
        xb, r, ig, sp, a, mult = _rec_gates(xc, wrg_ref, wig_ref, brg_ref, big_ref, lam_ref)
        h_prev = _shift_down(hv, hhalo, 1)
        ge, dge = _gelu_and_grad(gr_ref[...])
        hg = hv * ge
        rr = lax.rsqrt(jnp.mean(hg * hg, axis=-1, keepdims=True) + EPS)
        dy = d_ref[...]
        gd = dy * g_ref[...]
        dhg = rr * gd - hg * (rr * rr * rr) * jnp.mean(gd * hg, axis=-1, keepdims=True)
        gg_ref[...] += jnp.sum(dy * hg * rr, axis=0, keepdims=True)
        dgr = (dhg * hv * dge).astype(BF16)
        dh = dhg * ge
        b = _shift_up(a, jnp.broadcast_to(a_first[0:1, :], (8, REC_W)), 1)
        pb, gl = _scan_bwd(b, dh)
        gs = gl + pb * g_carry[0:1, :]
        g_carry[0:1, :] = gs[0:1, :]
        a_first[0:1, :] = a[0:1, :]
        da = gs * h_prev
        dmult = gs * (ig * xc)
        di = gs * (mult * xc)
        dxc = gs * (mult * ig)
        dlog_a = da * a - dmult * (a * a) / mult
        gsp[...] += jnp.sum(dlog_a * (-LRU_C * r), axis=0, keepdims=True)
        dzr = (dlog_a * (-LRU_C * sp)) * (r * (1.0 - r))
        dzi = di * (ig * (1.0 - ig))
        dzr_b, dzi_b = dzr.astype(BF16), dzi.astype(BF16)
        dxc = dxc + _dot(dzr_b, wrg_ref[...], _NT) + _dot(dzi_b, wig_ref[...], _NT)
        gwrg_ref[...] += _dot(xb, dzr_b, _TN)
        gwig_ref[...] += _dot(xb, dzi_b, _TN)
        gbrg_ref[...] += jnp.sum(dzr, axis=0, keepdims=True)
        gbig_ref[...] += jnp.sum(dzi, axis=0, keepdims=True)
        nxt = dxc_next[...]
        dxr = cw_ref[3:4, :] * dxc
        gcw_ref[3:4, :] += jnp.sum(dxc * xr, axis=0, keepdims=True)
        for s in range(1, REC_CONV):
            dxr = dxr + cw_ref[3 - s:4 - s, :] * _shift_up(dxc, nxt, s)
            gcw_ref[3 - s:4 - s, :] += jnp.sum(dxc * _shift_down(xr, xhalo, s), axis=0, keepdims=True)
        gcb_ref[...] += jnp.sum(dxc, axis=0, keepdims=True)
        dxc_next[...] = dxc[:8]
        drec_ref[...] = jnp.concatenate([dxr.astype(BF16), dgr], axis=1)

        @pl.when(first_tile)
        def _():
            glam_ref[...] = gsp[...] * (-_sigmoid(-lam_ref[...]))

    rev = lambda i: nt - 1 - i
    vec = pl.BlockSpec((1, REC_W), lambda i: (0, 0))
    row = pl.BlockSpec((tm, REC_W), lambda i: (rev(i), 0))
    mat = pl.BlockSpec((REC_W, REC_W), lambda i: (0, 0))
    cwb = pl.BlockSpec((8, REC_W), lambda i: (0, 0))
    halo = lambda c: pl.BlockSpec((8, REC_W), lambda i, c=c: (jnp.maximum(rev(i) * hb - 1, 0), c))
    return _call(
        body, name=name, grid=(nt,),
        in_specs=[pl.BlockSpec((tm, REC_W), lambda i: (rev(i), 1)),
                  pl.BlockSpec((tm, REC_W), lambda i: (rev(i), 3)), halo(3),
                  pl.BlockSpec((tm, REC_W), lambda i: (rev(i), 4)),
                  row, row, halo(0), cwb, vec, mat, mat, vec, vec, vec, vec],
        out_specs=[pl.BlockSpec((tm, 2 * REC_W), lambda i: (rev(i), 0)), cwb, vec, mat, mat, vec, vec, vec, vec],
        out_shape=[jax.ShapeDtypeStruct((T, 2 * REC_W), BF16)]
        + [jax.ShapeDtypeStruct((8, REC_W), F32), jax.ShapeDtypeStruct((1, REC_W), F32)]
        + [jax.ShapeDtypeStruct((REC_W, REC_W), F32)] * 2 + [jax.ShapeDtypeStruct((1, REC_W), F32)] * 4,
        scratch_shapes=[pltpu.VMEM((8, REC_W), F32)] * 3 + [pltpu.VMEM((1, REC_W), F32)],
        compiler_params=_params("arbitrary"),
    )(dmix, proj, proj, proj, xc, h, h, cw, cb, wrg, wig, brg, big, lam, g)


def _ffn_conv(x_ext, cw_ref, cb_ref):
    return (cb_ref[...] + cw_ref[2:3, :] * x_ext + cw_ref[1:2, :] * pltpu.roll(x_ext, 1, 0)
            + cw_ref[0:1, :] * pltpu.roll(x_ext, 2, 0))


def _ffn_act(pg, pu, cw, cb, name, tm=512, tc=768):
    T, F = pg.shape
    hb = tm // 8
    nc = F // tc

    def body(g_ref, gh_ref, u_ref, uh_ref, cwg_ref, cwu_ref, cbg_ref, cbu_ref, o_ref):
        first = pl.program_id(0) == 0
        ge = jnp.concatenate([jnp.where(first, 0.0, gh_ref[...]), g_ref[...]], axis=0)
        ue = jnp.concatenate([jnp.where(first, 0.0, uh_ref[...]), u_ref[...]], axis=0)
        act = _gelu(_ffn_conv(ge, cwg_ref, cbg_ref)) * _ffn_conv(ue, cwu_ref, cbu_ref)
        o_ref[...] = act[8:].astype(BF16)

    tile = pl.BlockSpec((tm, tc), lambda i, j: (i, j))
    halo = pl.BlockSpec((8, tc), lambda i, j: (jnp.maximum(i * hb - 1, 0), j))
    cws = lambda off: pl.BlockSpec((8, tc), lambda i, j, off=off: (0, j + off))
    cbs = lambda off: pl.BlockSpec((1, tc), lambda i, j, off=off: (0, j + off))
    return _call(
        body, name=name, grid=(T // tm, nc),
        in_specs=[tile, halo, tile, halo, cws(0), cws(nc), cbs(0), cbs(nc)], out_specs=tile,
        out_shape=jax.ShapeDtypeStruct((T, F), BF16), compiler_params=_params("parallel", "parallel"),
    )(pg, pg, pu, pu, cw, cw, cb, cb)


def _ffn_act_bwd(pg, pu, dact, cw, cb, name, tm=512, tc=768):
    T, F = pg.shape
    nt = T // tm
    hb = tm // 8
    nc = F // tc

    def body(g_ref, gp_ref, gn_ref, u_ref, up_ref, un_ref, d_ref, dn_ref, cwg_ref, cwu_ref, cbg_ref, cbu_ref,
             dg_ref, du_ref, gcwg_ref, gcwu_ref, gcbg_ref, gcbu_ref):
        i = pl.program_id(1)
        first, last = i == 0, i == nt - 1

        @pl.when(first)
        def _():
            for ref in (gcwg_ref, gcwu_ref, gcbg_ref, gcbu_ref):
                ref[...] = jnp.zeros_like(ref)

        ext = lambda p, t, n: jnp.concatenate([jnp.where(first, 0.0, p[...]), t[...], jnp.where(last, 0.0, n[...])], axis=0)
        ge, ue = ext(gp_ref, g_ref, gn_ref), ext(up_ref, u_ref, un_ref)
        de = jnp.concatenate([jnp.zeros((8, tc), F32), d_ref[...], jnp.where(last, 0.0, dn_ref[...])], axis=0)
        gel, dgel = _gelu_and_grad(_ffn_conv(ge, cwg_ref, cbg_ref))
        d_gate = de * _ffn_conv(ue, cwu_ref, cbu_ref) * dgel
        d_up = de * gel
        n = tm + 16
        for dcv, xe, cw_ref, dx_ref, gcw_ref, gcb_ref in ((d_gate, ge, cwg_ref, dg_ref, gcwg_ref, gcbg_ref),
                                                            (d_up, ue, cwu_ref, du_ref, gcwu_ref, gcbu_ref)):
            dx = cw_ref[2:3, :] * dcv + cw_ref[1:2, :] * pltpu.roll(dcv, n - 1, 0) + cw_ref[0:1, :] * pltpu.roll(dcv, n - 2, 0)
            dx_ref[...] = dx[8:tm + 8].astype(BF16)
            dt = dcv[8:tm + 8]
            gcw_ref[2:3, :] += jnp.sum(dt * xe[8:tm + 8], axis=0, keepdims=True)
            gcw_ref[1:2, :] += jnp.sum(dt * pltpu.roll(xe, 1, 0)[8:tm + 8], axis=0, keepdims=True)
            gcw_ref[0:1, :] += jnp.sum(dt * pltpu.roll(xe, 2, 0)[8:tm + 8], axis=0, keepdims=True)
            gcb_ref[...] += jnp.sum(dt, axis=0, keepdims=True)

    tile = pl.BlockSpec((tm, tc), lambda j, i: (i, j))
    prev = pl.BlockSpec((8, tc), lambda j, i: (jnp.maximum(i * hb - 1, 0), j))
    nxt = pl.BlockSpec((8, tc), lambda j, i: (jnp.minimum((i + 1) * hb, nt * hb - 1), j))
    cws = lambda off: pl.BlockSpec((8, tc), lambda j, i, off=off: (0, j + off))
    cbs = lambda off: pl.BlockSpec((1, tc), lambda j, i, off=off: (0, j + off))
    return _call(
        body, name=name, grid=(nc, nt),
        in_specs=[tile, prev, nxt, tile, prev, nxt, tile, nxt, cws(0), cws(nc), cbs(0), cbs(nc)],
        out_specs=[tile, tile, cws(0), cws(0), cbs(0), cbs(0)],
        out_shape=[jax.ShapeDtypeStruct((T, F), BF16)] * 2 + [jax.ShapeDtypeStruct((8, F), F32)] * 2
        + [jax.ShapeDtypeStruct((1, F), F32)] * 2,
        compiler_params=_params("parallel", "arbitrary"),
    )(pg, pg, pg, pu, pu, pu, dact, dact, cw, cw, cb, cb)


def _add_pairs(g, r1, core, name):
    _, r, n = g.shape

    def body(c_ref, g_ref, r_ref, o_ref):
        o_ref[...] = (g_ref[...].astype(F32) + r_ref[...].astype(F32)).astype(BF16)

    spec = pltpu.PrefetchScalarGridSpec(
        num_scalar_prefetch=1, grid=(4,),
        in_specs=[pl.BlockSpec((None, r, n), lambda k, c_ref: (2 * k + c_ref[0], 0, 0)),
                  pl.BlockSpec((None, r, n), lambda k, c_ref: (k, 0, 0))],
        out_specs=pl.BlockSpec((None, r, n), lambda k, c_ref: (k, 0, 0)))
    return _call(body, name=name, grid_spec=spec, out_shape=jax.ShapeDtypeStruct((4, r, n), BF16),
                 compiler_params=_params("parallel"))(core, g, r1)


def _adam_update(w, g, m, v):
    m2 = ADAM_B1 * m + (1.0 - ADAM_B1) * g
    v2 = ADAM_B2 * v + (1.0 - ADAM_B2) * (g * g)
    m_hat = m2 / (1.0 - ADAM_B1 ** ADAM_STEP)
    v_hat = v2 / (1.0 - ADAM_B2 ** ADAM_STEP)
    delta = -ADAM_LR * (m_hat / (jnp.sqrt(v_hat) + ADAM_EPS) + ADAM_WD * w)
    return delta, m2, v2


def _adam_sharded(p, r2, idx, w, m, v, name):
    r, n = w.shape
    tr = _row_tile(r)
    nrecv = r2.shape[0]

    def body(c_ref, p_ref, r_ref, w_ref, m_ref, v_ref, g_ref, d_ref, m2_ref, v2_ref):
        g = p_ref[...].astype(F32)
        for k in range(nrecv):
            g = g + r_ref[k].astype(F32)
        g_ref[...] = g
        d_ref[...], m2_ref[...], v2_ref[...] = _adam_update(w_ref[...], g, m_ref[...], v_ref[...])

    row = pl.BlockSpec((tr, n), lambda i, c_ref: (i, 0))
    spec = pltpu.PrefetchScalarGridSpec(
        num_scalar_prefetch=1, grid=(r // tr,),
        in_specs=[pl.BlockSpec((None, tr, n), lambda i, c_ref: (c_ref[0], i, 0)),
                  pl.BlockSpec((nrecv, tr, n), lambda i, c_ref: (0, i, 0)), row, row, row],
        out_specs=[row] * 4)
    return _call(body, name=name, grid_spec=spec, out_shape=[jax.ShapeDtypeStruct((r, n), F32)] * 4,
                 compiler_params=_params("parallel"))(idx, p, r2, w, m, v)


def _sum_devices(allg, name):
    r, n = allg.shape[0] // N_DEV, allg.shape[1]

    def body(a_ref, o_ref):
        acc = a_ref[0:r, :]
        for k in range(1, N_DEV):
            acc = acc + a_ref[k * r:(k + 1) * r, :]
        o_ref[...] = acc

    return _call(body, name=name, out_shape=jax.ShapeDtypeStruct((r, n), F32))(allg)


def _adam_small(w, g, m, v, name):
    def body(w_ref, g_ref, m_ref, v_ref, d_ref, m2_ref, v2_ref):
        d_ref[...], m2_ref[...], v2_ref[...] = _adam_update(w_ref[...], g_ref[...], m_ref[...], v_ref[...])

    return _call(body, name=name, out_shape=[jax.ShapeDtypeStruct(w.shape, F32)] * 3)(w, g, m, v)


_SMALL = (("g_mix", 1024), ("q_norm_g", 64), ("k_norm_g", 64), ("rec_conv_b", 512), ("w_rg", 32768), ("b_rg", 512),
          ("w_ig", 32768), ("b_ig", 512), ("lru_lambda", 512), ("g_attn_out", 512), ("g_rec_out", 512),
          ("g_ffn", 1024), ("ffn_conv_b", 6144))
_SMALL_SHAPES = {"g_mix": (1, 1024), "q_norm_g": (1, 64), "k_norm_g": (1, 64), "rec_conv_b": (1, 512),
                 "w_rg": (1, 8, 64, 64), "b_rg": (1, 8, 64), "w_ig": (1, 8, 64, 64), "b_ig": (1, 8, 64),
                 "lru_lambda": (1, 512), "g_attn_out": (1, 512), "g_rec_out": (1, 512), "g_ffn": (1, 1024),
                 "ffn_conv_b": (1, 6144)}
_N_REPL = sum(n for _, n in _SMALL)
_N_SMALL = _N_REPL + 4 * 64 + 3 * 768
_SMALL_PAD_ROWS = 80


def _pack_small(d, rec_cw, ffn_cw):
    flat = jnp.concatenate([d[k].reshape(-1) for k, _ in _SMALL] + [rec_cw.reshape(-1), ffn_cw.reshape(-1)])
    return jnp.pad(flat, (0, _SMALL_PAD_ROWS * 1024 - _N_SMALL)).reshape(_SMALL_PAD_ROWS, 1024)


def _unpack_small(p):
    flat = p.reshape(-1)
    out, o = {}, 0
    for k, n in _SMALL:
        out[k] = flat[o:o + n].reshape(_SMALL_SHAPES[k])
        o += n
    out["rec_conv_w"] = flat[o:o + 256].reshape(1, 4, 64)
    out["ffn_conv_w"] = flat[o + 256:o + 256 + 2304].reshape(1, 3, 768)
    return out


def _block_diag(w):
    eye = jnp.eye(8, dtype=w.dtype)
    return (w[:, :, None, :] * eye[:, None, :, None]).reshape(512, 512)


def kernel(x, positions, g_mix, w_in, q_norm_g, k_norm_g, rec_conv_w, rec_conv_b, w_rg, b_rg, w_ig, b_ig, lru_lambda, g_attn_out, g_rec_out, w_out, g_ffn, w_up, ffn_conv_w, ffn_conv_b, w_down, loss_target, m_g_mix, m_w_in, m_q_norm_g, m_k_norm_g, m_rec_conv_w, m_rec_conv_b, m_w_rg, m_b_rg, m_w_ig, m_b_ig, m_lru_lambda, m_g_attn_out, m_g_rec_out, m_w_out, m_g_ffn, m_w_up, m_ffn_conv_w, m_ffn_conv_b, m_w_down, v_g_mix, v_w_in, v_q_norm_g, v_k_norm_g, v_rec_conv_w, v_rec_conv_b, v_w_rg, v_b_rg, v_w_ig, v_b_ig, v_lru_lambda, v_g_attn_out, v_g_rec_out, v_w_out, v_g_ffn, v_w_up, v_ffn_conv_w, v_ffn_conv_b, v_w_down):
    T = x.shape[1]
    ix, iy, ic = lax.axis_index("x"), lax.axis_index("y"), lax.axis_index("c")
    dev = 4 * ix + 2 * iy + ic
    core = jnp.reshape(ic, (1,)).astype(jnp.int32)
    chip = jnp.reshape(2 * ix + iy, (1,)).astype(jnp.int32)
    xs = x.reshape(T, D_MODEL)
    tgt = loss_target.reshape(T, D_MODEL)
    pos = positions.reshape(T, 1)

    tr = lambda a: a[0].T
    shards = {"w_in": (tr(w_in), tr(m_w_in), tr(v_w_in)), "w_out": (w_out[0], m_w_out[0], v_w_out[0]),
              "w_up": (tr(w_up), tr(m_w_up), tr(v_w_up)), "w_down": (w_down[0], m_w_down[0], v_w_down[0])}
    taps = jnp.concatenate([rec_conv_w.reshape(-1), ffn_conv_w.reshape(-1), jnp.zeros((4096 - 2560,), F32)]).reshape(8, 512)
    W_inT, taps_all = _all_gather([shards["w_in"][0].astype(BF16), taps], "ag_w_in")
    late = [shards[nm][0].astype(BF16) for nm in ("w_out", "w_up", "w_down")]
    ag_send, ag_recv, late_thru, land_thru, ag_token = _exchange_start(
        late, [(N_DEV * s.shape[0], 1024) for s in late], "gather", taps_all, "ag_late_start")
    taps_all = taps_all.reshape(N_DEV, 4096)
    rcw = taps_all[:, :256].reshape(8, 4, 64).transpose(1, 0, 2).reshape(4, REC_W)
    fcw = taps_all[:, 256:2560].reshape(8, 3, 768).transpose(1, 0, 2).reshape(3, 2 * D_FF)
    rcw8 = jnp.pad(rcw, ((0, 4), (0, 0)))
    fcw8 = jnp.pad(fcw, ((0, 5), (0, 0)))
    fcb = ffn_conv_b.reshape(1, 2 * D_FF)

    half = HEAD_DIM // 2
    inv_freq = ROPE_THETA ** (-jnp.arange(half, dtype=F32) / half)
    invf = jnp.tile(inv_freq, 2 * N_HEADS).reshape(1, ATTN_W)
    bd = jnp.asarray(np.kron(np.eye(N_HEADS), np.full((HEAD_DIM, HEAD_DIM), 1.0 / HEAD_DIM)), BF16)
    qg = jnp.tile(q_norm_g.reshape(HEAD_DIM), N_HEADS).reshape(1, ATTN_W)
    kg = jnp.tile(k_norm_g.reshape(HEAD_DIM), N_HEADS).reshape(1, ATTN_W)
    wrg_bd = _block_diag(w_rg[0]).astype(BF16)
    wig_bd = _block_diag(w_ig[0]).astype(BF16)
    brg, big = b_rg.reshape(1, REC_W), b_ig.reshape(1, REC_W)

    h1 = _rmsnorm(xs, g_mix + ag_token[0, 0], "norm_mix")
    proj = _mm(h1, W_inT, "nt", F32, "in_proj", tn=1280)
    qf, kf = _qk_prep(proj, pos, invf, qg, kg, bd, "qk_prep")
    attn, lse = _attn_fwd(qf, kf, proj, "attn_fwd")
    mix = _attn_norm(attn, g_attn_out, "attn_norm")
    xc, hstate, mix = _rec_fwd(proj, mix, rcw8, rec_conv_b, wrg_bd, wig_bd, brg, big, lru_lambda, g_rec_out, "rec_fwd")
    late_thru, land_thru = _exchange_wait(ag_send, ag_recv, late_thru, land_thru, "gather", hstate, "ag_late_wait")
    W_out, W_upT, W_down = [lax.dynamic_update_slice(l, s, (dev * s.shape[0], 0)) for l, s in zip(land_thru, late_thru)]
    x2 = _mm(mix, W_out, "nn", F32, "out_proj", add=xs)

    h2 = _rmsnorm(x2, g_ffn, "norm_ffn")
    pg = _mm(h2, W_upT, "nt", F32, "up_proj_gate", n=D_FF)
    pu = _mm(h2, W_upT, "nt", F32, "up_proj_up", n=D_FF, b_noff=D_FF // 1024)
    act = _ffn_act(pg, pu, fcw8, fcb, "ffn_act")
    y = _mm(act, W_down, "nn", F32, "down_proj", add=x2)
    dy, dyb, lparts = _loss_grad(y, tgt, "loss_grad")
    loss = lax.psum(0.5 / D_MODEL * jnp.sum(lparts), ("x", "y", "c"))

    dact = _mm(dyb, W_down, "nt", F32, "d_act")
    g_down = _mm(act, dyb, "tn", BF16, "g_w_down")
    dpg, dpu, g_fcwg, g_fcwu, g_fcbg, g_fcbu = _ffn_act_bwd(pg, pu, dact, fcw8, fcb, "ffn_act_bwd")
    g_upT = _mm(dpg, h2, "tn", BF16, "g_w_up_gate", o_rows=2 * D_FF)
    g_upT = _mm(dpu, h2, "tn", BF16, "g_w_up_up", into=g_upT, o_moff=D_FF // 1024)
    dh2 = _mm(dpg, W_upT, "nn", F32, "d_h2_gate", k=D_FF)
    dh2 = _mm(dpu, W_upT, "nn", F32, "d_h2_up", k=D_FF, b_koff=D_FF // 1024, add=dh2)
    ffn_g = [g_upT.reshape(N_DEV, 2 * D_FF // N_DEV, 1024), g_down.reshape(N_DEV, D_FF // N_DEV, 1024)]
    rs_send, rs_recv, ffn_g, ffn_land, rs_token = _exchange_start(
        ffn_g, [(N_PEERS,) + g.shape[1:] for g in ffn_g], "scatter", dh2, "rs_ffn_start")
    dx2, dx2b, g_gffn = _rmsnorm_bwd(x2, dh2, dy, g_ffn + rs_token[0, 0], "norm_ffn_bwd")

    dmix = _mm(dx2b, W_out, "nt", F32, "d_mix")
    g_out = _mm(mix, dx2b, "tn", BF16, "g_w_out")
    do, delta, g_gattn = _attn_norm_bwd(dmix, attn, g_attn_out, bd, "attn_norm_bwd")
    dqh, dkh, dv = _attn_bwd(qf, kf, proj, do, lse, delta, "attn_bwd")
    dqkv, g_qg, g_kg = _qk_prep_bwd(proj, dqh, dkh, dv, pos, invf, qg, kg, bd, "qk_prep_bwd")
    (drec, g_rcw, g_rcb, g_wrg, g_wig, g_brg, g_big, g_lam, g_grec) = _rec_bwd(
        dmix, proj, xc, hstate, rcw8, rec_conv_b, wrg_bd, wig_bd, brg, big, lru_lambda, g_rec_out, "rec_bwd")
    g_inT = _mm(dqkv, h1, "tn", BF16, "g_w_in_qkv", tm=512, o_rows=IN_W)
    g_inT = _mm(drec, h1, "tn", BF16, "g_w_in_rec", tm=512, into=g_inT, o_moff=3 * ATTN_W // 512)
    dh1 = _mm(dqkv, W_inT, "nn", F32, "d_h1_qkv", tk=512, k=3 * ATTN_W)
    dh1 = _mm(drec, W_inT, "nn", F32, "d_h1_rec", tk=512, k=2 * REC_W, b_koff=3 * ATTN_W // 512, add=dh1)
    grad_x, _, g_gmix = _rmsnorm_bwd(xs, dh1, dx2, g_mix, "norm_mix_bwd")

    names = ("w_in", "w_out")
    gfull = [g.reshape(N_DEV, g.shape[0] // N_DEV, 1024) for g in (g_inT, g_out)]
    r1 = _exchange_sibling(gfull, "rs_sibling")
    part = [_add_pairs(g, r, core, "rs_add_pairs_" + nm) for g, r, nm in zip(gfull, r1, names)]
    r2 = _exchange_chips(part, "rs_chips")
    ffn_g, ffn_land = _exchange_wait(rs_send, rs_recv, ffn_g, ffn_land, "scatter", r2[0], "rs_ffn_wait")
    devi = jnp.reshape(dev, (1,)).astype(jnp.int32)
    big_out = {"grad": {}, "delta": {}, "new_m": {}, "new_v": {}}
    for nm, p, r, idx in list(zip(names, part, r2, (chip, chip))) + list(zip(("w_up", "w_down"), ffn_g, ffn_land, (devi, devi))):
        w_, m_, v_ = shards[nm]
        res = _adam_sharded(p, r, idx, w_, m_, v_, "adam_" + nm)
        for kind, a in zip(("grad", "delta", "new_m", "new_v"), res):
            big_out[kind][nm] = a.T[None] if nm in ("w_in", "w_up") else a[None]

    blocks = lambda g: jnp.stack([g[64 * n:64 * n + 64, 64 * n:64 * n + 64] for n in range(8)])
    small_g = {
        "g_mix": g_gmix, "q_norm_g": g_qg.reshape(N_HEADS, HEAD_DIM).sum(0), "k_norm_g": g_kg.reshape(N_HEADS, HEAD_DIM).sum(0),
        "rec_conv_b": g_rcb, "w_rg": blocks(g_wrg), "b_rg": g_brg, "w_ig": blocks(g_wig), "b_ig": g_big,
        "lru_lambda": g_lam, "g_attn_out": g_gattn, "g_rec_out": g_grec, "g_ffn": g_gffn,
        "ffn_conv_b": jnp.concatenate([g_fcbg, g_fcbu], axis=1)}
    g_fcw = jnp.concatenate([g_fcwg[:3], g_fcwu[:3]], axis=1)
    flat = jnp.concatenate([small_g[k].reshape(-1) for k, _ in _SMALL] + [g_rcw[:4].reshape(-1), g_fcw.reshape(-1)])
    flat = jnp.pad(flat, (0, SMALL_ROWS * 1024 - flat.shape[0])).reshape(SMALL_ROWS, 1024)
    tot = _sum_devices(_all_gather([flat], "ag_small_grads")[0], "sum_small_grads").reshape(-1)
    g_small, o = {}, 0
    for k, n in _SMALL:
        g_small[k] = tot[o:o + n]
        o += n
    g_rcw_mine = lax.dynamic_slice(tot[o:o + 2048].reshape(4, REC_W), (0, 64 * dev), (4, 64))
    g_fcw_mine = lax.dynamic_slice(tot[o + 2048:o + 2048 + 18432].reshape(3, 2 * D_FF), (0, 768 * dev), (3, 768))
    given = dict(g_mix=g_mix, q_norm_g=q_norm_g, k_norm_g=k_norm_g, rec_conv_b=rec_conv_b, w_rg=w_rg, b_rg=b_rg, w_ig=w_ig,
                 b_ig=b_ig, lru_lambda=lru_lambda, g_attn_out=g_attn_out, g_rec_out=g_rec_out, g_ffn=g_ffn, ffn_conv_b=ffn_conv_b)
    given_m = dict(g_mix=m_g_mix, q_norm_g=m_q_norm_g, k_norm_g=m_k_norm_g, rec_conv_b=m_rec_conv_b, w_rg=m_w_rg, b_rg=m_b_rg,
                   w_ig=m_w_ig, b_ig=m_b_ig, lru_lambda=m_lru_lambda, g_attn_out=m_g_attn_out, g_rec_out=m_g_rec_out,
                   g_ffn=m_g_ffn, ffn_conv_b=m_ffn_conv_b)
    given_v = dict(g_mix=v_g_mix, q_norm_g=v_q_norm_g, k_norm_g=v_k_norm_g, rec_conv_b=v_rec_conv_b, w_rg=v_w_rg, b_rg=v_b_rg,
                   w_ig=v_w_ig, b_ig=v_b_ig, lru_lambda=v_lru_lambda, g_attn_out=v_g_attn_out, g_rec_out=v_g_rec_out,
                   g_ffn=v_g_ffn, ffn_conv_b=v_ffn_conv_b)
    ws = _pack_small(given, rec_conv_w, ffn_conv_w)
    gs = _pack_small(g_small, g_rcw_mine, g_fcw_mine)
    ms = _pack_small(given_m, m_rec_conv_w, m_ffn_conv_w)
    vs = _pack_small(given_v, v_rec_conv_w, v_ffn_conv_w)
    ds, m2s, v2s = _adam_small(ws, gs, ms, vs, "adam_small")
    small_out = {"grad": _unpack_small(gs), "delta": _unpack_small(ds), "new_m": _unpack_small(m2s), "new_v": _unpack_small(v2s)}

    order = ("g_mix", "w_in", "q_norm_g", "k_norm_g", "rec_conv_w", "rec_conv_b", "w_rg", "b_rg", "w_ig", "b_ig",
             "lru_lambda", "g_attn_out", "g_rec_out", "w_out", "g_ffn", "w_up", "ffn_conv_w", "ffn_conv_b", "w_down")
    outs = [loss, grad_x.reshape(1, T, D_MODEL)]
    for kind in ("grad", "delta", "new_m", "new_v"):
        for name in order:
            outs.append(big_out[kind][name] if name in big_out[kind] else small_out[kind][name])
    return tuple(outs)
```

```python
import math

import numpy as np
import jax
import jax.numpy as jnp
from jax import lax
from jax.experimental import pallas as pl
from jax.experimental.pallas import tpu as pltpu

F32 = jnp.float32
BF16 = jnp.bfloat16

D_MODEL = 1024
HEAD_DIM = 64
ATTN_W = 512
REC_W = 512
N_HEADS = 8
D_FF = 3072
IN_W = 2560
REC_CONV = 4
FFN_CONV = 3
LRU_C = 8.0
ROPE_THETA = 10000.0
EPS = 1e-6
NEG_INF = -1e30
QBLK = 128
DILATIONS = (1, 4, 16)
N_DEV = 8
SMALL_ROWS = 96
ADAM_LR, ADAM_B1, ADAM_B2, ADAM_EPS, ADAM_WD, ADAM_STEP = 0.001, 0.9, 0.999, 1e-08, 0.01, 10
MESH = pl.DeviceIdType.MESH
ANY = pl.BlockSpec(memory_space=pl.ANY)


def _call(body, *, name, **kw):
    return pl.pallas_call(body, name=name, **kw)


def _params(*sem):
    return pltpu.CompilerParams(dimension_semantics=sem, vmem_limit_bytes=56 * 1024 * 1024)


def _gelu(x):
    c = math.sqrt(2.0 / math.pi)
    return 0.5 * x * (1.0 + jnp.tanh(c * (x + 0.044715 * (x * x * x))))


def _gelu_and_grad(x):
    c = math.sqrt(2.0 / math.pi)
    t = jnp.tanh(c * (x + 0.044715 * (x * x * x)))
    g = 0.5 * x * (1.0 + t)
    dg = 0.5 * (1.0 + t) + 0.5 * x * (1.0 - t * t) * (c * (1.0 + 3.0 * 0.044715 * (x * x)))
    return g, dg


def _sigmoid(x):
    return 1.0 / (1.0 + jnp.exp(-x))


def _softplus_neg(lam):
    y = jnp.exp(-jnp.abs(lam))
    u = 1.0 + y
    log1p = jnp.where(u == 1.0, y, jnp.log(u) * y / jnp.where(u == 1.0, 1.0, u - 1.0))
    return jnp.maximum(-lam, 0.0) + log1p


_NN = (((1,), (0,)), ((), ()))
_NT = (((1,), (1,)), ((), ()))
_TN = (((0,), (0,)), ((), ()))


def _dot(a, b, dims=_NN):
    return lax.dot_general(a, b, dims, preferred_element_type=F32)


def _group_mean(v, bd):
    hi = v.astype(BF16)
    lo = (v - hi.astype(F32)).astype(BF16)
    w = bd.shape[0]
    return jnp.concatenate([_dot(hi[:, c:c + w], bd) + _dot(lo[:, c:c + w], bd) for c in range(0, v.shape[1], w)], axis=1)


def _rope_tables(pos_ref, invf_ref):
    ang = pos_ref[...].astype(F32) * invf_ref[:, :2 * HEAD_DIM]
    reps = invf_ref.shape[1] // (2 * HEAD_DIM)
    return jnp.tile(jnp.cos(ang), (1, reps)), jnp.tile(jnp.sin(ang), (1, reps))


def _shift_down(x, halo, s):
    rolled = pltpu.roll(x, s, 0)
    hr = pltpu.roll(halo, s, 0)
    row = lax.broadcasted_iota(jnp.int32, hr.shape, 0)
    first = jnp.where(row < s, hr, rolled[:8])
    return jnp.concatenate([first, rolled[8:]], axis=0)


def _shift_up(x, halo, s):
    n = x.shape[0]
    rolled = pltpu.roll(x, n - s, 0)
    hr = pltpu.roll(halo, 8 - s, 0)
    row = lax.broadcasted_iota(jnp.int32, hr.shape, 0)
    last = jnp.where(row >= 8 - s, hr, rolled[n - 8:])
    return jnp.concatenate([rolled[:n - 8], last], axis=0)


def _scan_fwd(a, u):
    n = a.shape[0]
    row = lax.broadcasted_iota(jnp.int32, a.shape, 0)
    s = 1
    while s < n:
        a_s = jnp.where(row < s, 1.0, pltpu.roll(a, s, 0))
        u_s = jnp.where(row < s, 0.0, pltpu.roll(u, s, 0))
        u = u + a * u_s
        a = a * a_s
        s *= 2
    return a, u


def _scan_bwd(b, v):
    n = b.shape[0]
    row = lax.broadcasted_iota(jnp.int32, b.shape, 0)
    s = 1
    while s < n:
        b_s = jnp.where(row >= n - s, 1.0, pltpu.roll(b, n - s, 0))
        v_s = jnp.where(row >= n - s, 0.0, pltpu.roll(v, n - s, 0))
        v = v + b * v_s
        b = b * b_s
        s *= 2
    return b, v


def _rot_half(y):
    n = y.shape[1]
    lane = lax.broadcasted_iota(jnp.int32, y.shape, 1) & (HEAD_DIM - 1)
    return jnp.where(lane < HEAD_DIM // 2, -pltpu.roll(y, n - HEAD_DIM // 2, 1), pltpu.roll(y, HEAD_DIM // 2, 1))


def _row_tile(r, cap=256):
    return max(t for t in range(16, cap + 1, 16) if r % t == 0)


def _all_gather(shards, name):
    na = len(shards)
    ms = [s.shape[0] for s in shards]

    def body(*refs):
        x_refs, out_refs = refs[:na], refs[na:2 * na]
        send_sems, recv_sems, local_sems = refs[2 * na:]
        x, y, c = lax.axis_index("x"), lax.axis_index("y"), lax.axis_index("c")
        me, sibling = (x, y, c), (x, y, 1 - c)
        chips = [(1 - x, y), (x, 1 - y), (1 - x, 1 - y)]

        def rows(a, px, py, pc):
            return out_refs[a].at[pl.ds((4 * px + 2 * py + pc) * ms[a], ms[a]), :]

        def copy(a, k, block, to, src=None):
            return pltpu.make_async_remote_copy(
                src_ref=rows(a, *block) if src is None else src, dst_ref=rows(a, *block),
                send_sem=send_sems.at[7 * a + k], recv_sem=recv_sems.at[7 * a + k], device_id=to, device_id_type=MESH)

        mine = [pltpu.make_async_copy(x_refs[a], rows(a, *me), local_sems.at[a]) for a in range(na)]
        first = []
        for a in range(na):
            mine[a].start()
            first.append(copy(a, 0, me, sibling, src=x_refs[a]))
            first += [copy(a, 1 + j, me, (*chip, c), src=x_refs[a]) for j, chip in enumerate(chips)]
        for cp in first:
            cp.start()
        passed = []
        for a in range(na):
            for j, chip in enumerate(chips):
                copy(a, 1 + j, (*chip, c), me).wait_recv()
                fw = copy(a, 4 + j, (*chip, c), sibling)
                fw.start()
                passed.append(fw)
        for a in range(na):
            copy(a, 0, sibling, me).wait_recv()
            for j, chip in enumerate(chips):
                copy(a, 4 + j, (*chip, 1 - c), me).wait_recv()
        for cp in first + passed:
            cp.wait_send()
        for cp in mine:
            cp.wait()

    return _call(
        body, name=name, out_shape=[jax.ShapeDtypeStruct((N_DEV * s.shape[0], s.shape[1]), s.dtype) for s in shards],
        in_specs=[ANY] * na, out_specs=[ANY] * na,
        scratch_shapes=[pltpu.SemaphoreType.DMA((7 * na,)), pltpu.SemaphoreType.DMA((7 * na,)),
                        pltpu.SemaphoreType.DMA((na,))],
    )(*shards)


def _exchange_sibling(gs, name):
    na = len(gs)

    def body(*refs):
        g_refs, out_refs = refs[:na], refs[na:2 * na]
        send_sems, recv_sems = refs[2 * na:]
        x, y, c = lax.axis_index("x"), lax.axis_index("y"), lax.axis_index("c")
        copies = [
            pltpu.make_async_remote_copy(
                src_ref=g_refs[a].at[2 * k + (1 - c)], dst_ref=out_refs[a].at[k],
                send_sem=send_sems.at[4 * a + k], recv_sem=recv_sems.at[4 * a + k],
                device_id=(x, y, 1 - c), device_id_type=MESH)
            for a in range(na) for k in range(4)]
        for cp in copies:
            cp.start()
        for cp in copies:
            cp.wait()

    return _call(
        body, name=name, out_shape=[jax.ShapeDtypeStruct((4,) + g.shape[1:], g.dtype) for g in gs],
        in_specs=[ANY] * na, out_specs=[ANY] * na,
        scratch_shapes=[pltpu.SemaphoreType.DMA((4 * na,)), pltpu.SemaphoreType.DMA((4 * na,))],
    )(*gs)


def _exchange_chips(ps, name):
    na = len(ps)

    def body(*refs):
        p_refs, out_refs = refs[:na], refs[na:2 * na]
        send_sems, recv_sems = refs[2 * na:]
        x, y, c = lax.axis_index("x"), lax.axis_index("y"), lax.axis_index("c")
        chips = [(1 - x, y), (x, 1 - y), (1 - x, 1 - y)]
        copies = [
            pltpu.make_async_remote_copy(
                src_ref=p_refs[a].at[2 * cx + cy], dst_ref=out_refs[a].at[k],
                send_sem=send_sems.at[3 * a + k], recv_sem=recv_sems.at[3 * a + k],
                device_id=(cx, cy, c), device_id_type=MESH)
            for a in range(na) for k, (cx, cy) in enumerate(chips)]
        for cp in copies:
            cp.start()
        for cp in copies:
            cp.wait()

    return _call(
        body, name=name, out_shape=[jax.ShapeDtypeStruct((3,) + p.shape[1:], p.dtype) for p in ps],
        in_specs=[ANY] * na, out_specs=[ANY] * na,
        scratch_shapes=[pltpu.SemaphoreType.DMA((3 * na,)), pltpu.SemaphoreType.DMA((3 * na,))],
    )(*ps)


HBM = pl.BlockSpec(memory_space=pltpu.HBM)
SEM = pl.BlockSpec(memory_space=pltpu.SEMAPHORE)
EFFECT = pltpu.SideEffectType.DATAFLOW_SIDE_EFFECTING
N_PEERS = N_DEV - 1


def _peer(k):
    x, y, c = lax.axis_index("x"), lax.axis_index("y"), lax.axis_index("c")
    b = k + 1
    flip = lambda v, bit: 1 - v if bit else v
    return flip(x, b & 4), flip(y, b & 2), flip(c, b & 1)


def _in_hbm(a):
    return pltpu.with_memory_space_constraint(a, pltpu.HBM)


def _split_copy_descr(na, kind, src_refs, land_refs, send_sems, recv_sems):
    x, y, c = lax.axis_index("x"), lax.axis_index("y"), lax.axis_index("c")
    me = 4 * x + 2 * y + c
    copies = []
    for a in range(na):
        for k in range(N_PEERS):
            px, py, pc = _peer(k)
            if kind == "gather":
                m = src_refs[a].shape[0]
                src, dst = src_refs[a], land_refs[a].at[pl.ds(me * m, m), :]
            else:
                src, dst = src_refs[a].at[4 * px + 2 * py + pc], land_refs[a].at[k]
            copies.append(pltpu.make_async_remote_copy(
                src_ref=src, dst_ref=dst, send_sem=send_sems.at[N_PEERS * a + k], recv_sem=recv_sems.at[N_PEERS * a + k],
                device_id=(px, py, pc), device_id_type=MESH))
    return copies


def _exchange_start(srcs, land_shapes, kind, after, name):
    na = len(srcs)

    def body(*refs):
        src_refs, land_refs = refs[:na], refs[na:2 * na]
        send_sems, recv_sems = refs[2 * na + 1], refs[2 * na + 2]
        token = refs[-1]
        for cp in _split_copy_descr(na, kind, src_refs, land_refs, send_sems, recv_sems):
            cp.start()
        token[...] = jnp.zeros_like(token)

    lands = [_in_hbm(lax.empty(s, srcs[0].dtype)) for s in land_shapes]
    sem = pltpu.SemaphoreType.DMA((N_PEERS * na,))
    outs = _call(
        body, name=name,
        out_shape=[sem, sem] + [pltpu.HBM(s.shape, s.dtype) for s in srcs] + [pltpu.HBM(s, srcs[0].dtype) for s in land_shapes]
        + [jax.ShapeDtypeStruct((8, 128), F32)],
        in_specs=[HBM] * (2 * na) + [ANY], out_specs=[SEM, SEM] + [HBM] * (2 * na) + [pl.BlockSpec(memory_space=pltpu.VMEM)],
        input_output_aliases={i: 2 + i for i in range(2 * na)},
        compiler_params=pltpu.CompilerParams(has_side_effects=EFFECT),
    )(*[_in_hbm(s) for s in srcs], *lands, after)
    return outs[0], outs[1], outs[2:2 + na], outs[2 + na:2 + 2 * na], outs[-1]


def _exchange_wait(send_sems, recv_sems, srcs, lands, kind, after, name):
    na = len(srcs)

    def body(*refs):
        src_refs, land_refs = refs[:na], refs[na:2 * na]
        s_sems, r_sems = refs[2 * na], refs[2 * na + 1]
        for cp in _split_copy_descr(na, kind, src_refs, land_refs, s_sems, r_sems):
            cp.wait_send()
            cp.wait_recv()

    outs = _call(
        body, name=name, out_shape=[pltpu.HBM(s.shape, s.dtype) for s in srcs] + [pltpu.HBM(l.shape, l.dtype) for l in lands],
        in_specs=[HBM] * (2 * na) + [SEM, SEM, ANY], out_specs=[HBM] * (2 * na),
        input_output_aliases={i: i for i in range(2 * na)},
        compiler_params=pltpu.CompilerParams(has_side_effects=EFFECT),
    )(*srcs, *lands, send_sems, recv_sems, after)
    return outs[:na], outs[na:]


def _mm(a, b, mode, out_dtype, name, add=None, tm=1024, tn=1024, tk=1024, b_noff=0, b_koff=0,
        n=None, k=None, into=None, o_rows=None, o_moff=0, loss_target=None):
    if mode == "tn":
        K, M = a.shape
    else:
        M, K = a.shape
    N = n if n is not None else (b.shape[0] if mode == "nt" else b.shape[1])
    if k is not None:
        assert k == K
    tm, tn, tk = min(tm, M), min(tn, N), min(tk, K)
    assert M % tm == 0 and N % tn == 0 and K % tk == 0, (name, M, N, K)
    nk = K // tk
    if mode == "nn":
        a_spec = pl.BlockSpec((tm, tk), lambda i, j, kk: (i, kk))
        b_spec, dims = pl.BlockSpec((tk, tn), lambda i, j, kk: (kk + b_koff, j + b_noff)), _NN
    elif mode == "nt":
        a_spec = pl.BlockSpec((tm, tk), lambda i, j, kk: (i, kk))
        b_spec, dims = pl.BlockSpec((tn, tk), lambda i, j, kk: (j + b_noff, kk + b_koff)), _NT
    else:
        a_spec = pl.BlockSpec((tk, tm), lambda i, j, kk: (kk, i))
        b_spec, dims = pl.BlockSpec((tk, tn), lambda i, j, kk: (kk + b_koff, j + b_noff)), _TN
    o_spec = pl.BlockSpec((tm, tn), lambda i, j, kk: (i + o_moff, j))
    has_add, has_into, has_loss = add is not None, into is not None, loss_target is not None
    assert not has_loss or (has_add and tn == N and not has_into)
    n_in = 2 + has_add + has_loss + has_into

    def body(*refs):
        a_ref, b_ref = refs[0], refs[1]
        add_ref = refs[2] if has_add else None
        outs = refs[n_in:]

        def finish(r):
            if has_add:
                r = r + add_ref[...]
            if has_loss:
                e = r - refs[3][...]
                dy = e * (1.0 / N)
                outs[0][...] = dy
                outs[1][...] = dy.astype(BF16)
                outs[2][...] = jnp.sum(e * e, axis=0, keepdims=True)[None]
            else:
                outs[0][...] = r.astype(out_dtype)

        if nk == 1:
            finish(_dot(a_ref[...], b_ref[...], dims))
        else:
            acc = refs[-1]
            kk = pl.program_id(2)

            @pl.when(kk == 0)
            def _():
                acc[...] = _dot(a_ref[...], b_ref[...], dims)

            @pl.when((kk > 0) & (kk < nk - 1))
            def _():
                acc[...] += _dot(a_ref[...], b_ref[...], dims)

            @pl.when(kk == nk - 1)
            def _():
                finish(acc[...] + _dot(a_ref[...], b_ref[...], dims))

    tile = pl.BlockSpec((tm, tn), lambda i, j, kk: (i, j))
    ins = [a, b] + ([add] if has_add else []) + ([loss_target] if has_loss else []) + ([into] if has_into else [])
    specs = [a_spec, b_spec] + [tile] * (has_add + has_loss) + ([ANY] if has_into else [])
    rows = into.shape[0] if has_into else (o_rows if o_rows is not None else M)
    if has_loss:
        out_specs = [tile, tile, pl.BlockSpec((1, 1, N), lambda i, j, kk: (i, 0, 0))]
        out_shape = [jax.ShapeDtypeStruct((M, N), F32), jax.ShapeDtypeStruct((M, N), BF16), jax.ShapeDtypeStruct((M // tm, 1, N), F32)]
    else:
        out_specs, out_shape = o_spec, jax.ShapeDtypeStruct((rows, N), out_dtype)
    return _call(
        body, name=name, grid=(M // tm, N // tn, nk), in_specs=specs, out_specs=out_specs, out_shape=out_shape,
        scratch_shapes=[pltpu.VMEM((tm, tn), F32)] if nk > 1 else [],
        input_output_aliases={len(ins) - 1: 0} if has_into else {},
        compiler_params=_params("parallel", "parallel", "arbitrary"),
    )(*ins)


def _rmsnorm(x, g, name, tm=512):
    T, D = x.shape

    def body(x_ref, g_ref, o_ref):
        xv = x_ref[...]
        r = lax.rsqrt(jnp.mean(xv * xv, axis=-1, keepdims=True) + EPS)
        o_ref[...] = (xv * r * g_ref[...]).astype(BF16)

    return _call(
        body, name=name, grid=(T // tm,),
        in_specs=[pl.BlockSpec((tm, D), lambda i: (i, 0)), pl.BlockSpec((1, D), lambda i: (0, 0))],
        out_specs=pl.BlockSpec((tm, D), lambda i: (i, 0)), out_shape=jax.ShapeDtypeStruct((T, D), BF16),
        compiler_params=_params("parallel"),
    )(x, g)


def _rmsnorm_bwd(x, dh, resid, g, name, tm=512):
    T, D = x.shape

    def body(x_ref, dh_ref, res_ref, g_ref, dx_ref, dxb_ref, dg_ref):
        @pl.when(pl.program_id(0) == 0)
        def _():
            dg_ref[...] = jnp.zeros_like(dg_ref)

        xv, dhv = x_ref[...], dh_ref[...]
        r = lax.rsqrt(jnp.mean(xv * xv, axis=-1, keepdims=True) + EPS)
        gd = dhv * g_ref[...]
        m = jnp.mean(gd * xv, axis=-1, keepdims=True)
        dx = res_ref[...] + r * gd - xv * (r * r * r) * m
        dx_ref[...] = dx
        dxb_ref[...] = dx.astype(BF16)
        dg_ref[...] += jnp.sum(dhv * xv * r, axis=0, keepdims=True)

    row = pl.BlockSpec((tm, D), lambda i: (i, 0))
    vec = pl.BlockSpec((1, D), lambda i: (0, 0))
    return _call(
        body, name=name, grid=(T // tm,), in_specs=[row, row, row, vec], out_specs=[row, row, vec],
        out_shape=[jax.ShapeDtypeStruct((T, D), F32), jax.ShapeDtypeStruct((T, D), BF16), jax.ShapeDtypeStruct((1, D), F32)],
        compiler_params=_params("arbitrary"),
    )(x, dh, resid, g)


def _loss_grad(y, target, name, tm=512):
    T, D = y.shape

    def body(y_ref, t_ref, dy_ref, dyb_ref, l_ref):
        e = y_ref[...] - t_ref[...]
        dy = e * (1.0 / D)
        dy_ref[...] = dy
        dyb_ref[...] = dy.astype(BF16)
        l_ref[...] = jnp.sum(e * e, axis=0, keepdims=True)[None]

    row = pl.BlockSpec((tm, D), lambda i: (i, 0))
    return _call(
        body, name=name, grid=(T // tm,), in_specs=[row, row],
        out_specs=[row, row, pl.BlockSpec((1, 1, D), lambda i: (i, 0, 0))],
        out_shape=[jax.ShapeDtypeStruct((T, D), F32), jax.ShapeDtypeStruct((T, D), BF16),
                   jax.ShapeDtypeStruct((T // tm, 1, D), F32)],
        compiler_params=_params("parallel"),
    )(y, target)


def _qk_prep(proj, pos, invf, qg, kg, bd, name, tm=512):
    T = proj.shape[0]

    def body(q_ref, k_ref, pos_ref, invf_ref, qg_ref, kg_ref, bd_ref, qo_ref, ko_ref):
        cos, sin = _rope_tables(pos_ref, invf_ref)

        def prep(xv, gv, scale):
            r = lax.rsqrt(_group_mean(xv * xv, bd_ref[...]) + EPS)
            yv = xv * r * gv
            return ((yv * cos + _rot_half(yv) * sin) * scale).astype(BF16).astype(F32)

        qo_ref[...] = prep(q_ref[...], qg_ref[...], HEAD_DIM ** -0.5)
        ko_ref[...] = prep(k_ref[...], kg_ref[...], 1.0)

    col = lambda j: pl.BlockSpec((tm, ATTN_W), lambda i, j=j: (i, j))
    vec = pl.BlockSpec((1, ATTN_W), lambda i: (0, 0))
    out = pl.BlockSpec((tm, ATTN_W), lambda i: (i, 0))
    return _call(
        body, name=name, grid=(T // tm,),
        in_specs=[col(0), col(1), pl.BlockSpec((tm, 1), lambda i: (i, 0)), vec, vec, vec,
                  pl.BlockSpec((2 * HEAD_DIM, 2 * HEAD_DIM), lambda i: (0, 0))],
        out_specs=[out, out], out_shape=[jax.ShapeDtypeStruct((T, ATTN_W), F32)] * 2,
        compiler_params=_params("parallel"),
    )(proj, proj, pos, invf, qg, kg, bd)


def _qk_prep_bwd(proj, dqh, dkh, dv, pos, invf, qg, kg, bd, name, tm=512):
    T = proj.shape[0]

    def body(q_ref, k_ref, dq_ref, dk_ref, dv_ref, pos_ref, invf_ref, qg_ref, kg_ref, bd_ref, o_ref, gq_ref, gk_ref):
        @pl.when(pl.program_id(0) == 0)
        def _():
            gq_ref[...] = jnp.zeros_like(gq_ref)
            gk_ref[...] = jnp.zeros_like(gk_ref)

        cos, sin = _rope_tables(pos_ref, invf_ref)

        def back(xv, gv, dz, scale):
            dz = dz * scale
            dy = dz * cos - _rot_half(dz * sin)
            r = lax.rsqrt(_group_mean(xv * xv, bd_ref[...]) + EPS)
            gd = dy * gv
            m = _group_mean(gd * xv, bd_ref[...])
            dx = r * gd - xv * (r * r * r) * m
            return dx, jnp.sum(dy * xv * r, axis=0, keepdims=True)

        dxq, gs = back(q_ref[...], qg_ref[...], dq_ref[...], HEAD_DIM ** -0.5)
        gq_ref[...] += gs
        dxk, gs = back(k_ref[...], kg_ref[...], dk_ref[...], 1.0)
        gk_ref[...] += gs
        o_ref[...] = jnp.concatenate([dxq.astype(BF16), dxk.astype(BF16), dv_ref[...].astype(BF16)], axis=1)

    col = lambda j: pl.BlockSpec((tm, ATTN_W), lambda i, j=j: (i, j))
    row = pl.BlockSpec((tm, ATTN_W), lambda i: (i, 0))
    vec = pl.BlockSpec((1, ATTN_W), lambda i: (0, 0))
    return _call(
        body, name=name, grid=(T // tm,),
        in_specs=[col(0), col(1), row, row, row, pl.BlockSpec((tm, 1), lambda i: (i, 0)), vec, vec, vec,
                  pl.BlockSpec((2 * HEAD_DIM, 2 * HEAD_DIM), lambda i: (0, 0))],
        out_specs=[pl.BlockSpec((tm, 3 * ATTN_W), lambda i: (i, 0)), vec, vec],
        out_shape=[jax.ShapeDtypeStruct((T, 3 * ATTN_W), BF16)] + [jax.ShapeDtypeStruct((1, ATTN_W), F32)] * 2,
        compiler_params=_params("arbitrary"),
    )(proj, proj, dqh, dkh, dv, pos, invf, qg, kg, bd)


def _ld(ref, start, size, dil):
    return ref[pl.ds(start, size), :] if dil == 1 else ref[pl.ds(start, size, stride=dil), :]


def _st(ref, start, size, dil, val):
    if dil == 1:
        ref[pl.ds(start, size), :] = val
    else:
        ref[pl.ds(start, size, stride=dil), :] = val


def _attn_geometry(T, dil):
    nb = T // dil // QBLK
    kw = 2 * QBLK if nb >= 2 else QBLK
    return nb, kw


ATTN_UNROLL = 4


def _attn_unit(j, u, dil, nit):
    return ATTN_UNROLL * j + u if dil >= ATTN_UNROLL else j + u * (nit // ATTN_UNROLL)


def _attn_block(it, dil, kw):
    c, n = it & (dil - 1), lax.shift_right_logical(it, dil.bit_length() - 1)
    sq = n * (QBLK * dil) + c
    sk = jnp.maximum(n - (kw // QBLK - 1), 0) * (QBLK * dil) + c
    qi = lax.broadcasted_iota(jnp.int32, (2 * QBLK, kw), 0) & (QBLK - 1)
    kj = lax.broadcasted_iota(jnp.int32, (2 * QBLK, kw), 1)
    rel = jnp.where(n > 0, kw - QBLK, 0) + qi - kj
    return sq, sk, (rel >= 0) & (rel <= QBLK)


def _stack_heads(xv, head0):
    z = jnp.zeros_like(xv)
    return jnp.concatenate([jnp.where(head0, xv, z), jnp.where(head0, z, xv)], axis=0)


def _unstack_heads(x2, head0):
    return jnp.where(head0, x2[:QBLK], x2[QBLK:])


def _attn_fwd(qf, kf, proj, name):
    T = qf.shape[0]

    def body(q_ref, k_ref, v_ref, o_ref, lse_ref):
        head0 = lax.broadcasted_iota(jnp.int32, (QBLK, 2 * HEAD_DIM), 1) < HEAD_DIM
        for bi, dil in enumerate(DILATIONS):
            nb, kw = _attn_geometry(T, dil)

            nit = nb * dil

            def step(j, carry, bi=bi, dil=dil, kw=kw, nit=nit):
                units = []
                for u in range(ATTN_UNROLL):
                    sq, sk, ok = _attn_block(_attn_unit(j, u, dil, nit), dil, kw)
                    old = (_ld(o_ref, sq, QBLK, dil), _ld(lse_ref, sq, QBLK, dil)) if bi > 0 else None
                    units.append((sq, ok, _ld(q_ref, sq, QBLK, dil).astype(BF16), _ld(k_ref, sk, kw, dil).astype(BF16),
                                  _ld(v_ref, sk, kw, dil).astype(BF16), old))
                results = []
                for sq, ok, qv, kv, vv, old in units:
                    s = jnp.where(ok, _dot(_stack_heads(qv, head0), kv, _NT), NEG_INF)
                    m = jnp.max(s, axis=-1, keepdims=True)
                    p = jnp.exp(s - m).astype(BF16)
                    acc = _dot(p, jnp.concatenate([vv, jnp.ones_like(vv)], axis=1))
                    l = acc[:, 2 * HEAD_DIM:]
                    o_new = _unstack_heads(acc[:, :2 * HEAD_DIM] / l, head0)
                    l_new = _unstack_heads(m + jnp.log(l), head0)
                    if bi > 0:
                        o_old, l_old = old
                        mx = jnp.maximum(l_old, l_new)
                        e0, e1 = jnp.exp(l_old - mx), jnp.exp(l_new - mx)
                        z = e0 + e1
                        o_new = (e0 * o_old + e1 * o_new) / z
                        l_new = mx + jnp.log(z)
                    results.append((sq, o_new, l_new))
                for sq, o_new, l_new in results:
                    _st(o_ref, sq, QBLK, dil, o_new)
                    _st(lse_ref, sq, QBLK, dil, l_new)
                return carry

            lax.fori_loop(0, nit // ATTN_UNROLL, step, 0)

    blk = lambda off: pl.BlockSpec((T, 2 * HEAD_DIM), lambda hp, off=off: (0, off + hp))
    return _call(
        body, name=name, grid=(4,), in_specs=[blk(0), blk(0), blk(8)], out_specs=[blk(0), blk(0)],
        out_shape=[jax.ShapeDtypeStruct((T, ATTN_W), F32)] * 2, compiler_params=_params("parallel"),
    )(qf, kf, proj)


def _attn_bwd(qf, kf, proj, do, lse, delta, name):
    T = qf.shape[0]

    def body(q_ref, k_ref, v_ref, do_ref, lse_ref, dl_ref, dq_ref, dk_ref, dv_ref):
        head0 = lax.broadcasted_iota(jnp.int32, (QBLK, 2 * HEAD_DIM), 1) < HEAD_DIM
        for ref in (dq_ref, dk_ref, dv_ref):
            ref[...] = jnp.zeros_like(ref)
        for dil in DILATIONS:
            nb, kw = _attn_geometry(T, dil)

            nit = nb * dil

            def step(j, carry, dil=dil, kw=kw, nit=nit):
                units = []
                for u in range(ATTN_UNROLL):
                    sq, sk, ok = _attn_block(_attn_unit(j, u, dil, nit), dil, kw)
                    lsev, dlv = _ld(lse_ref, sq, QBLK, dil), _ld(dl_ref, sq, QBLK, dil)
                    units.append((sq, sk, ok, _ld(q_ref, sq, QBLK, dil).astype(BF16), _ld(do_ref, sq, QBLK, dil).astype(BF16),
                                  jnp.concatenate([lsev[:, 0:1], lsev[:, HEAD_DIM:HEAD_DIM + 1]], axis=0),
                                  jnp.concatenate([dlv[:, 0:1], dlv[:, HEAD_DIM:HEAD_DIM + 1]], axis=0),
                                  _ld(k_ref, sk, kw, dil).astype(BF16), _ld(v_ref, sk, kw, dil).astype(BF16),
                                  _ld(dq_ref, sq, QBLK, dil), _ld(dk_ref, sk, kw, dil), _ld(dv_ref, sk, kw, dil)))
                results = []
                for sq, sk, ok, qv, dov, lse2, dl2, kv, vv, dq0, dk0, dv0 in units:
                    q2, do2 = _stack_heads(qv, head0), _stack_heads(dov, head0)
                    p = jnp.where(ok, jnp.exp(_dot(q2, kv, _NT) - lse2), 0.0)
                    ds = (p * (_dot(do2, vv, _NT) - dl2)).astype(BF16)
                    results.append((sq, sk, dq0 + _unstack_heads(_dot(ds, kv), head0),
                                    dk0 + _dot(ds, q2, _TN), dv0 + _dot(p.astype(BF16), do2, _TN)))
                for sq, sk, dq, dk, dv in results:
                    _st(dq_ref, sq, QBLK, dil, dq)
                    _st(dk_ref, sk, kw, dil, dk)
                    _st(dv_ref, sk, kw, dil, dv)
                return carry

            lax.fori_loop(0, nit // ATTN_UNROLL, step, 0)

    blk = lambda off: pl.BlockSpec((T, 2 * HEAD_DIM), lambda hp, off=off: (0, off + hp))
    return _call(
        body, name=name, grid=(4,), in_specs=[blk(0), blk(0), blk(8), blk(0), blk(0), blk(0)], out_specs=[blk(0)] * 3,
        out_shape=[jax.ShapeDtypeStruct((T, ATTN_W), F32)] * 3, compiler_params=_params("parallel"),
    )(qf, kf, proj, do, lse, delta)


def _attn_norm(attn, g, name, tm=512):
    T = attn.shape[0]

    def body(a_ref, g_ref, o_ref):
        av = a_ref[...]
        r = lax.rsqrt(jnp.mean(av * av, axis=-1, keepdims=True) + EPS)
        o_ref[...] = (av * r * g_ref[...]).astype(BF16)

    row = pl.BlockSpec((tm, ATTN_W), lambda i: (i, 0))
    return _call(
        body, name=name, grid=(T // tm,), in_specs=[row, pl.BlockSpec((1, ATTN_W), lambda i: (0, 0))], out_specs=row,
        out_shape=jax.ShapeDtypeStruct((T, 2 * ATTN_W), BF16), compiler_params=_params("parallel"),
    )(attn, g)


def _attn_norm_bwd(dmix, attn, g, bd, name, tm=512):
    T = attn.shape[0]

    def body(d_ref, a_ref, g_ref, bd_ref, do_ref, dl_ref, dg_ref):
        @pl.when(pl.program_id(0) == 0)
        def _():
            dg_ref[...] = jnp.zeros_like(dg_ref)

        dy, av = d_ref[...], a_ref[...]
        r = lax.rsqrt(jnp.mean(av * av, axis=-1, keepdims=True) + EPS)
        gd = dy * g_ref[...]
        m = jnp.mean(gd * av, axis=-1, keepdims=True)
        da = r * gd - av * (r * r * r) * m
        do_ref[...] = da
        dl_ref[...] = _group_mean(da * av, bd_ref[...]) * float(HEAD_DIM)
        dg_ref[...] += jnp.sum(dy * av * r, axis=0, keepdims=True)

    row = pl.BlockSpec((tm, ATTN_W), lambda i: (i, 0))
    vec = pl.BlockSpec((1, ATTN_W), lambda i: (0, 0))
    return _call(
        body, name=name, grid=(T // tm,),
        in_specs=[row, row, vec, pl.BlockSpec((2 * HEAD_DIM, 2 * HEAD_DIM), lambda i: (0, 0))], out_specs=[row, row, vec],
        out_shape=[jax.ShapeDtypeStruct((T, ATTN_W), F32)] * 2 + [jax.ShapeDtypeStruct((1, ATTN_W), F32)],
        compiler_params=_params("arbitrary"),
    )(dmix, attn, g, bd)


def _rec_gates(xc, wrg_ref, wig_ref, brg_ref, big_ref, lam_ref):
    xb = xc.astype(BF16)
    r = _sigmoid(_dot(xb, wrg_ref[...]) + brg_ref[...])
    ig = _sigmoid(_dot(xb, wig_ref[...]) + big_ref[...])
    sp = _softplus_neg(lam_ref[...])
    log_a = -LRU_C * r * sp
    a = jnp.exp(log_a)
    th = jnp.tanh(log_a)
    mult = jnp.sqrt(-2.0 * th / (1.0 - th))
    return xb, r, ig, sp, a, mult


def _rec_fwd(proj, mix, cw, cb, wrg, wig, brg, big, lam, g, name, tm=256):
    T = proj.shape[0]
    hb = tm // 8

    def body(xr_ref, halo_ref, gr_ref, cw_ref, cb_ref, wrg_ref, wig_ref, brg_ref, big_ref, lam_ref, g_ref, mix_ref,
             xc_ref, h_ref, out_ref, carry):
        i = pl.program_id(0)

        @pl.when(i == 0)
        def _():
            carry[...] = jnp.zeros_like(carry)

        xr = xr_ref[...]
        halo = jnp.where(i > 0, halo_ref[...], 0.0)
        xc = cb_ref[...] + cw_ref[3:4, :] * xr
        for s in range(1, REC_CONV):
            xc = xc + cw_ref[3 - s:4 - s, :] * _shift_down(xr, halo, s)
        xc_ref[...] = xc
        _, _, ig, _, a, mult = _rec_gates(xc, wrg_ref, wig_ref, brg_ref, big_ref, lam_ref)
        pa, hl = _scan_fwd(a, mult * (ig * xc))
        h = hl + pa * carry[0:1, :]
        h_ref[...] = h
        carry[0:1, :] = h_ref[pl.ds(tm - 1, 1), :]
        hg = h * _gelu(gr_ref[...])
        r = lax.rsqrt(jnp.mean(hg * hg, axis=-1, keepdims=True) + EPS)
        out_ref[...] = (hg * r * g_ref[...]).astype(BF16)

    vec = pl.BlockSpec((1, REC_W), lambda i: (0, 0))
    row = pl.BlockSpec((tm, REC_W), lambda i: (i, 0))
    mat = pl.BlockSpec((REC_W, REC_W), lambda i: (0, 0))
    return _call(
        body, name=name, grid=(T // tm,),
        in_specs=[pl.BlockSpec((tm, REC_W), lambda i: (i, 3)),
                  pl.BlockSpec((8, REC_W), lambda i: (jnp.maximum(i * hb - 1, 0), 3)),
                  pl.BlockSpec((tm, REC_W), lambda i: (i, 4)),
                  pl.BlockSpec((8, REC_W), lambda i: (0, 0)), vec, mat, mat, vec, vec, vec, vec, ANY],
        out_specs=[row, row, pl.BlockSpec((tm, REC_W), lambda i: (i, 1))],
        out_shape=[jax.ShapeDtypeStruct((T, REC_W), F32)] * 2 + [jax.ShapeDtypeStruct(mix.shape, BF16)],
        scratch_shapes=[pltpu.VMEM((8, REC_W), F32)], input_output_aliases={11: 2},
        compiler_params=_params("arbitrary"),
    )(proj, proj, proj, cw, cb, wrg, wig, brg, big, lam, g, mix)


def _rec_bwd(dmix, proj, xc, h, cw, cb, wrg, wig, brg, big, lam, g, name, tm=256):
    T = proj.shape[0]
    nt = T // tm
    hb = tm // 8

    def body(d_ref, xr_ref, xhalo_ref, gr_ref, xc_ref, h_ref, hhalo_ref, cw_ref, cb_ref, wrg_ref, wig_ref, brg_ref,
             big_ref, lam_ref, g_ref,
             drec_ref, gcw_ref, gcb_ref, gwrg_ref, gwig_ref, gbrg_ref, gbig_ref, glam_ref, gg_ref,
             g_carry, a_first, dxc_next, gsp):
        i = pl.program_id(0)
        first_tile = i == nt - 1

        @pl.when(i == 0)
        def _():
            for ref in (gcw_ref, gcb_ref, gwrg_ref, gwig_ref, gbrg_ref, gbig_ref, glam_ref, gg_ref,
                        g_carry, a_first, dxc_next, gsp):
                ref[...] = jnp.zeros_like(ref)

        xr, xc, hv = xr_ref[...], xc_ref[...], h_ref[...]
        xhalo = jnp.where(first_tile, 0.0, xhalo_ref[...])
        hhalo = jnp.where(first_tile, 0.0, hhalo_ref[...])
        xb, r, ig, sp, a, mult = _rec_gates(xc, wrg_ref, wig_ref, brg_ref, big_ref, lam_ref)
        h_prev = _shift_down(hv, hhalo, 1)
        ge, dge = _gelu_and_grad(gr_ref[...])
        hg = hv * ge
        rr = lax.rsqrt(jnp.mean(hg * hg, axis=-1, keepdims=True) + EPS)
        dy = d_ref[...]
        gd = dy * g_ref[...]
        dhg = rr * gd - hg * (rr * rr * rr) * jnp.mean(gd * hg, axis=-1, keepdims=True)
        gg_ref[...] += jnp.sum(dy * hg * rr, axis=0, keepdims=True)
        dgr = (dhg * hv * dge).astype(BF16)
        dh = dhg * ge
        b = _shift_up(a, jnp.broadcast_to(a_first[0:1, :], (8, REC_W)), 1)
        pb, gl = _scan_bwd(b, dh)
        gs = gl + pb * g_carry[0:1, :]
        g_carry[0:1, :] = gs[0:1, :]
        a_first[0:1, :] = a[0:1, :]
        da = gs * h_prev
        dmult = gs * (ig * xc)
        di = gs * (mult * xc)
        dxc = gs * (mult * ig)
        dlog_a = da * a - dmult * (a * a) / mult
        gsp[...] += jnp.sum(dlog_a * (-LRU_C * r), axis=0, keepdims=True)
        dzr = (dlog_a * (-LRU_C * sp)) * (r * (1.0 - r))
        dzi = di * (ig * (1.0 - ig))
        dzr_b, dzi_b = dzr.astype(BF16), dzi.astype(BF16)
        dxc = dxc + _dot(dzr_b, wrg_ref[...], _NT) + _dot(dzi_b, wig_ref[...], _NT)
        gwrg_ref[...] += _dot(xb, dzr_b, _TN)
        gwig_ref[...] += _dot(xb, dzi_b, _TN)
        gbrg_ref[...] += jnp.sum(dzr, axis=0, keepdims=True)
        gbig_ref[...] += jnp.sum(dzi, axis=0, keepdims=True)
        nxt = dxc_next[...]
        dxr = cw_ref[3:4, :] * dxc
        gcw_ref[3:4, :] += jnp.sum(dxc * xr, axis=0, keepdims=True)
        for s in range(1, REC_CONV):
            dxr = dxr + cw_ref[3 - s:4 - s, :] * _shift_up(dxc, nxt, s)
            gcw_ref[3 - s:4 - s, :] += jnp.sum(dxc * _shift_down(xr, xhalo, s), axis=0, keepdims=True)
        gcb_ref[...] += jnp.sum(dxc, axis=0, keepdims=True)
        dxc_next[...] = dxc[:8]
        drec_ref[...] = jnp.concatenate([dxr.astype(BF16), dgr], axis=1)

        @pl.when(first_tile)
        def _():
            glam_ref[...] = gsp[...] * (-_sigmoid(-lam_ref[...]))

    rev = lambda i: nt - 1 - i
    vec = pl.BlockSpec((1, REC_W), lambda i: (0, 0))
    row = pl.BlockSpec((tm, REC_W), lambda i: (rev(i), 0))
    mat = pl.BlockSpec((REC_W, REC_W), lambda i: (0, 0))
    cwb = pl.BlockSpec((8, REC_W), lambda i: (0, 0))
    halo = lambda c: pl.BlockSpec((8, REC_W), lambda i, c=c: (jnp.maximum(rev(i) * hb - 1, 0), c))
    return _call(
        body, name=name, grid=(nt,),
        in_specs=[pl.BlockSpec((tm, REC_W), lambda i: (rev(i), 1)),
                  pl.BlockSpec((tm, REC_W), lambda i: (rev(i), 3)), halo(3),
                  pl.BlockSpec((tm, REC_W), lambda i: (rev(i), 4)),
                  row, row, halo(0), cwb, vec, mat, mat, vec, vec, vec, vec],
        out_specs=[pl.BlockSpec((tm, 2 * REC_W), lambda i: (rev(i), 0)), cwb, vec, mat, mat, vec, vec, vec, vec],
        out_shape=[jax.ShapeDtypeStruct((T, 2 * REC_W), BF16)]
        + [jax.ShapeDtypeStruct((8, REC_W), F32), jax.ShapeDtypeStruct((1, REC_W), F32)]
        + [jax.ShapeDtypeStruct((REC_W, REC_W), F32)] * 2 + [jax.ShapeDtypeStruct((1, REC_W), F32)] * 4,
        scratch_shapes=[pltpu.VMEM((8, REC_W), F32)] * 3 + [pltpu.VMEM((1, REC_W), F32)],
        compiler_params=_params("arbitrary"),
    )(dmix, proj, proj, proj, xc, h, h, cw, cb, wrg, wig, brg, big, lam, g)


def _ffn_conv(x_ext, cw_ref, cb_ref):
    return (cb_ref[...] + cw_ref[2:3, :] * x_ext + cw_ref[1:2, :] * pltpu.roll(x_ext, 1, 0)
            + cw_ref[0:1, :] * pltpu.roll(x_ext, 2, 0))


def _ffn_act(pg, pu, cw, cb, name, tm=512, tc=768):
    T, F = pg.shape
    hb = tm // 8
    nc = F // tc

    def body(g_ref, gh_ref, u_ref, uh_ref, cwg_ref, cwu_ref, cbg_ref, cbu_ref, o_ref):
        first = pl.program_id(0) == 0
        ge = jnp.concatenate([jnp.where(first, 0.0, gh_ref[...]), g_ref[...]], axis=0)
        ue = jnp.concatenate([jnp.where(first, 0.0, uh_ref[...]), u_ref[...]], axis=0)
        act = _gelu(_ffn_conv(ge, cwg_ref, cbg_ref)) * _ffn_conv(ue, cwu_ref, cbu_ref)
        o_ref[...] = act[8:].astype(BF16)

    tile = pl.BlockSpec((tm, tc), lambda i, j: (i, j))
    halo = pl.BlockSpec((8, tc), lambda i, j: (jnp.maximum(i * hb - 1, 0), j))
    cws = lambda off: pl.BlockSpec((8, tc), lambda i, j, off=off: (0, j + off))
    cbs = lambda off: pl.BlockSpec((1, tc), lambda i, j, off=off: (0, j + off))
    return _call(
        body, name=name, grid=(T // tm, nc),
        in_specs=[tile, halo, tile, halo, cws(0), cws(nc), cbs(0), cbs(nc)], out_specs=tile,
        out_shape=jax.ShapeDtypeStruct((T, F), BF16), compiler_params=_params("parallel", "parallel"),
    )(pg, pg, pu, pu, cw, cw, cb, cb)


def _ffn_act_bwd(pg, pu, dact, cw, cb, name, tm=512, tc=768):
    T, F = pg.shape
    nt = T // tm
    hb = tm // 8
    nc = F // tc

    def body(g_ref, gp_ref, gn_ref, u_ref, up_ref, un_ref, d_ref, dn_ref, cwg_ref, cwu_ref, cbg_ref, cbu_ref,
             dg_ref, du_ref, gcwg_ref, gcwu_ref, gcbg_ref, gcbu_ref):
        i = pl.program_id(1)
        first, last = i == 0, i == nt - 1

        @pl.when(first)
        def _():
            for ref in (gcwg_ref, gcwu_ref, gcbg_ref, gcbu_ref):
                ref[...] = jnp.zeros_like(ref)

        ext = lambda p, t, n: jnp.concatenate([jnp.where(first, 0.0, p[...]), t[...], jnp.where(last, 0.0, n[...])], axis=0)
        ge, ue = ext(gp_ref, g_ref, gn_ref), ext(up_ref, u_ref, un_ref)
        de = jnp.concatenate([jnp.zeros((8, tc), F32), d_ref[...], jnp.where(last, 0.0, dn_ref[...])], axis=0)
        gel, dgel = _gelu_and_grad(_ffn_conv(ge, cwg_ref, cbg_ref))
        d_gate = de * _ffn_conv(ue, cwu_ref, cbu_ref) * dgel
        d_up = de * gel
        n = tm + 16
        for dcv, xe, cw_ref, dx_ref, gcw_ref, gcb_ref in ((d_gate, ge, cwg_ref, dg_ref, gcwg_ref, gcbg_ref),
                                                            (d_up, ue, cwu_ref, du_ref, gcwu_ref, gcbu_ref)):
            dx = cw_ref[2:3, :] * dcv + cw_ref[1:2, :] * pltpu.roll(dcv, n - 1, 0) + cw_ref[0:1, :] * pltpu.roll(dcv, n - 2, 0)
            dx_ref[...] = dx[8:tm + 8].astype(BF16)
            dt = dcv[8:tm + 8]
            gcw_ref[2:3, :] += jnp.sum(dt * xe[8:tm + 8], axis=0, keepdims=True)
            gcw_ref[1:2, :] += jnp.sum(dt * pltpu.roll(xe, 1, 0)[8:tm + 8], axis=0, keepdims=True)
            gcw_ref[0:1, :] += jnp.sum(dt * pltpu.roll(xe, 2, 0)[8:tm + 8], axis=0, keepdims=True)
            gcb_ref[...] += jnp.sum(dt, axis=0, keepdims=True)

    tile = pl.BlockSpec((tm, tc), lambda j, i: (i, j))
    prev = pl.BlockSpec((8, tc), lambda j, i: (jnp.maximum(i * hb - 1, 0), j))
    nxt = pl.BlockSpec((8, tc), lambda j, i: (jnp.minimum((i + 1) * hb, nt * hb - 1), j))
    cws = lambda off: pl.BlockSpec((8, tc), lambda j, i, off=off: (0, j + off))
    cbs = lambda off: pl.BlockSpec((1, tc), lambda j, i, off=off: (0, j + off))
    return _call(
        body, name=name, grid=(nc, nt),
        in_specs=[tile, prev, nxt, tile, prev, nxt, tile, nxt, cws(0), cws(nc), cbs(0), cbs(nc)],
        out_specs=[tile, tile, cws(0), cws(0), cbs(0), cbs(0)],
        out_shape=[jax.ShapeDtypeStruct((T, F), BF16)] * 2 + [jax.ShapeDtypeStruct((8, F), F32)] * 2
        + [jax.ShapeDtypeStruct((1, F), F32)] * 2,
        compiler_params=_params("parallel", "arbitrary"),
    )(pg, pg, pg, pu, pu, pu, dact, dact, cw, cw, cb, cb)


def _add_pairs(g, r1, core, name):
    _, r, n = g.shape

    def body(c_ref, g_ref, r_ref, o_ref):
        o_ref[...] = (g_ref[...].astype(F32) + r_ref[...].astype(F32)).astype(BF16)

    spec = pltpu.PrefetchScalarGridSpec(
        num_scalar_prefetch=1, grid=(4,),
        in_specs=[pl.BlockSpec((None, r, n), lambda k, c_ref: (2 * k + c_ref[0], 0, 0)),
                  pl.BlockSpec((None, r, n), lambda k, c_ref: (k, 0, 0))],
        out_specs=pl.BlockSpec((None, r, n), lambda k, c_ref: (k, 0, 0)))
    return _call(body, name=name, grid_spec=spec, out_shape=jax.ShapeDtypeStruct((4, r, n), BF16),
                 compiler_params=_params("parallel"))(core, g, r1)


def _adam_update(w, g, m, v):
    m2 = ADAM_B1 * m + (1.0 - ADAM_B1) * g
    v2 = ADAM_B2 * v + (1.0 - ADAM_B2) * (g * g)
    m_hat = m2 / (1.0 - ADAM_B1 ** ADAM_STEP)
    v_hat = v2 / (1.0 - ADAM_B2 ** ADAM_STEP)
    delta = -ADAM_LR * (m_hat / (jnp.sqrt(v_hat) + ADAM_EPS) + ADAM_WD * w)
    return delta, m2, v2


def _adam_sharded(p, r2, idx, w, m, v, name):
    r, n = w.shape
    tr = _row_tile(r)
    nrecv = r2.shape[0]

    def body(c_ref, p_ref, r_ref, w_ref, m_ref, v_ref, g_ref, d_ref, m2_ref, v2_ref):
        g = p_ref[...].astype(F32)
        for k in range(nrecv):
            g = g + r_ref[k].astype(F32)
        g_ref[...] = g
        d_ref[...], m2_ref[...], v2_ref[...] = _adam_update(w_ref[...], g, m_ref[...], v_ref[...])

    row = pl.BlockSpec((tr, n), lambda i, c_ref: (i, 0))
    spec = pltpu.PrefetchScalarGridSpec(
        num_scalar_prefetch=1, grid=(r // tr,),
        in_specs=[pl.BlockSpec((None, tr, n), lambda i, c_ref: (c_ref[0], i, 0)),
                  pl.BlockSpec((nrecv, tr, n), lambda i, c_ref: (0, i, 0)), row, row, row],
        out_specs=[row] * 4)
    return _call(body, name=name, grid_spec=spec, out_shape=[jax.ShapeDtypeStruct((r, n), F32)] * 4,
                 compiler_params=_params("parallel"))(idx, p, r2, w, m, v)


def _sum_devices(allg, name):
    r, n = allg.shape[0] // N_DEV, allg.shape[1]

    def body(a_ref, o_ref):
        acc = a_ref[0:r, :]
        for k in range(1, N_DEV):
            acc = acc + a_ref[k * r:(k + 1) * r, :]
        o_ref[...] = acc

    return _call(body, name=name, out_shape=jax.ShapeDtypeStruct((r, n), F32))(allg)


def _adam_small(w, g, m, v, name):
    def body(w_ref, g_ref, m_ref, v_ref, d_ref, m2_ref, v2_ref):
        d_ref[...], m2_ref[...], v2_ref[...] = _adam_update(w_ref[...], g_ref[...], m_ref[...], v_ref[...])

    return _call(body, name=name, out_shape=[jax.ShapeDtypeStruct(w.shape, F32)] * 3)(w, g, m, v)


_SMALL = (("g_mix", 1024), ("q_norm_g", 64), ("k_norm_g", 64), ("rec_conv_b", 512), ("w_rg", 32768), ("b_rg", 512),
          ("w_ig", 32768), ("b_ig", 512), ("lru_lambda", 512), ("g_attn_out", 512), ("g_rec_out", 512),
          ("g_ffn", 1024), ("ffn_conv_b", 6144))
_SMALL_SHAPES = {"g_mix": (1, 1024), "q_norm_g": (1, 64), "k_norm_g": (1, 64), "rec_conv_b": (1, 512),
                 "w_rg": (1, 8, 64, 64), "b_rg": (1, 8, 64), "w_ig": (1, 8, 64, 64), "b_ig": (1, 8, 64),
                 "lru_lambda": (1, 512), "g_attn_out": (1, 512), "g_rec_out": (1, 512), "g_ffn": (1, 1024),
                 "ffn_conv_b": (1, 6144)}
_N_REPL = sum(n for _, n in _SMALL)
_N_SMALL = _N_REPL + 4 * 64 + 3 * 768
_SMALL_PAD_ROWS = 80


def _pack_small(d, rec_cw, ffn_cw):
    flat = jnp.concatenate([d[k].reshape(-1) for k, _ in _SMALL] + [rec_cw.reshape(-1), ffn_cw.reshape(-1)])
    return jnp.pad(flat, (0, _SMALL_PAD_ROWS * 1024 - _N_SMALL)).reshape(_SMALL_PAD_ROWS, 1024)


def _unpack_small(p):
    flat = p.reshape(-1)
    out, o = {}, 0
    for k, n in _SMALL:
        out[k] = flat[o:o + n].reshape(_SMALL_SHAPES[k])
        o += n
    out["rec_conv_w"] = flat[o:o + 256].reshape(1, 4, 64)
    out["ffn_conv_w"] = flat[o + 256:o + 256 + 2304].reshape(1, 3, 768)
    return out


def _block_diag(w):
    eye = jnp.eye(8, dtype=w.dtype)
    return (w[:, :, None, :] * eye[:, None, :, None]).reshape(512, 512)


def kernel(x, positions, g_mix, w_in, q_norm_g, k_norm_g, rec_conv_w, rec_conv_b, w_rg, b_rg, w_ig, b_ig, lru_lambda, g_attn_out, g_rec_out, w_out, g_ffn, w_up, ffn_conv_w, ffn_conv_b, w_down, loss_target, m_g_mix, m_w_in, m_q_norm_g, m_k_norm_g, m_rec_conv_w, m_rec_conv_b, m_w_rg, m_b_rg, m_w_ig, m_b_ig, m_lru_lambda, m_g_attn_out, m_g_rec_out, m_w_out, m_g_ffn, m_w_up, m_ffn_conv_w, m_ffn_conv_b, m_w_down, v_g_mix, v_w_in, v_q_norm_g, v_k_norm_g, v_rec_conv_w, v_rec_conv_b, v_w_rg, v_b_rg, v_w_ig, v_b_ig, v_lru_lambda, v_g_attn_out, v_g_rec_out, v_w_out, v_g_ffn, v_w_up, v_ffn_conv_w, v_ffn_conv_b, v_w_down):
    T = x.shape[1]
    ix, iy, ic = lax.axis_index("x"), lax.axis_index("y"), lax.axis_index("c")
    dev = 4 * ix + 2 * iy + ic
    core = jnp.reshape(ic, (1,)).astype(jnp.int32)
    chip = jnp.reshape(2 * ix + iy, (1,)).astype(jnp.int32)
    xs = x.reshape(T, D_MODEL)
    tgt = loss_target.reshape(T, D_MODEL)
    pos = positions.reshape(T, 1)

    tr = lambda a: a[0].T
    shards = {"w_in": (tr(w_in), tr(m_w_in), tr(v_w_in)), "w_out": (w_out[0], m_w_out[0], v_w_out[0]),
              "w_up": (tr(w_up), tr(m_w_up), tr(v_w_up)), "w_down": (w_down[0], m_w_down[0], v_w_down[0])}
    taps = jnp.concatenate([rec_conv_w.reshape(-1), ffn_conv_w.reshape(-1), jnp.zeros((4096 - 2560,), F32)]).reshape(8, 512)
    W_inT, taps_all = _all_gather([shards["w_in"][0].astype(BF16), taps], "ag_w_in")
    late = [shards[nm][0].astype(BF16) for nm in ("w_out", "w_up", "w_down")]
    ag_send, ag_recv, late_thru, land_thru, ag_token = _exchange_start(
        late, [(N_DEV * s.shape[0], 1024) for s in late], "gather", taps_all, "ag_late_start")
    taps_all = taps_all.reshape(N_DEV, 4096)
    rcw = taps_all[:, :256].reshape(8, 4, 64).transpose(1, 0, 2).reshape(4, REC_W)
    fcw = taps_all[:, 256:2560].reshape(8, 3, 768).transpose(1, 0, 2).reshape(3, 2 * D_FF)
    rcw8 = jnp.pad(rcw, ((0, 4), (0, 0)))
    fcw8 = jnp.pad(fcw, ((0, 5), (0, 0)))
    fcb = ffn_conv_b.reshape(1, 2 * D_FF)

    half = HEAD_DIM // 2
    inv_freq = ROPE_THETA ** (-jnp.arange(half, dtype=F32) / half)
    invf = jnp.tile(inv_freq, 2 * N_HEADS).reshape(1, ATTN_W)
    bd = jnp.asarray(np.kron(np.eye(2), np.full((HEAD_DIM, HEAD_DIM), 1.0 / HEAD_DIM)), BF16)
    qg = jnp.tile(q_norm_g.reshape(HEAD_DIM), N_HEADS).reshape(1, ATTN_W)
    kg = jnp.tile(k_norm_g.reshape(HEAD_DIM), N_HEADS).reshape(1, ATTN_W)
    wrg_bd = _block_diag(w_rg[0]).astype(BF16)
    wig_bd = _block_diag(w_ig[0]).astype(BF16)
    brg, big = b_rg.reshape(1, REC_W), b_ig.reshape(1, REC_W)

    h1 = _rmsnorm(xs, g_mix + ag_token[0, 0], "norm_mix")
    proj = _mm(h1, W_inT, "nt", F32, "in_proj", tn=1280)
    qf, kf = _qk_prep(proj, pos, invf, qg, kg, bd, "qk_prep")
    attn, lse = _attn_fwd(qf, kf, proj, "attn_fwd")
    mix = _attn_norm(attn, g_attn_out, "attn_norm")
    xc, hstate, mix = _rec_fwd(proj, mix, rcw8, rec_conv_b, wrg_bd, wig_bd, brg, big, lru_lambda, g_rec_out, "rec_fwd")
    late_thru, land_thru = _exchange_wait(ag_send, ag_recv, late_thru, land_thru, "gather", hstate, "ag_late_wait")
    W_out, W_upT, W_down = [lax.dynamic_update_slice(l, s, (dev * s.shape[0], 0)) for l, s in zip(land_thru, late_thru)]
    x2 = _mm(mix, W_out, "nn", F32, "out_proj", add=xs)

    h2 = _rmsnorm(x2, g_ffn, "norm_ffn")
    pg = _mm(h2, W_upT, "nt", F32, "up_proj_gate", n=D_FF, tn=1536)
    pu = _mm(h2, W_upT, "nt", F32, "up_proj_up", n=D_FF, tn=1536, b_noff=D_FF // 1536)
    act = _ffn_act(pg, pu, fcw8, fcb, "ffn_act")
    dy, dyb, lparts = _mm(act, W_down, "nn", F32, "down_proj_loss", add=x2, loss_target=tgt)
    loss_mine = 0.5 / D_MODEL * jnp.sum(lparts)

    dact = _mm(dyb, W_down, "nt", F32, "d_act", tn=1536)
    g_down = _mm(act, dyb, "tn", BF16, "g_w_down", tk=2048)
    dpg, dpu, g_fcwg, g_fcwu, g_fcbg, g_fcbu = _ffn_act_bwd(pg, pu, dact, fcw8, fcb, "ffn_act_bwd")
    g_upT = _mm(dpg, h2, "tn", BF16, "g_w_up_gate", tk=2048, o_rows=2 * D_FF)
    g_upT = _mm(dpu, h2, "tn", BF16, "g_w_up_up", tk=2048, into=g_upT, o_moff=D_FF // 1024)
    dh2 = _mm(dpg, W_upT, "nn", F32, "d_h2_gate", k=D_FF)
    dh2 = _mm(dpu, W_upT, "nn", F32, "d_h2_up", k=D_FF, b_koff=D_FF // 1024, add=dh2)
    ffn_g = [g_upT.reshape(N_DEV, 2 * D_FF // N_DEV, 1024), g_down.reshape(N_DEV, D_FF // N_DEV, 1024)]
    rs_send, rs_recv, ffn_g, ffn_land, rs_token = _exchange_start(
        ffn_g, [(N_PEERS,) + g.shape[1:] for g in ffn_g], "scatter", dh2, "rs_ffn_start")
    dx2, dx2b, g_gffn = _rmsnorm_bwd(x2, dh2, dy, g_ffn + rs_token[0, 0], "norm_ffn_bwd")

    dmix = _mm(dx2b, W_out, "nt", F32, "d_mix")
    g_out = _mm(mix, dx2b, "tn", BF16, "g_w_out", tk=2048).reshape(N_DEV, D_MODEL // N_DEV, 1024)
    out_send, out_recv, (g_out,), out_land, out_token = _exchange_start(
        [g_out], [(N_PEERS,) + g_out.shape[1:]], "scatter", dmix, "rs_out_start")
    do, delta, g_gattn = _attn_norm_bwd(dmix, attn, g_attn_out + out_token[0, 0], bd, "attn_norm_bwd")
    dqh, dkh, dv = _attn_bwd(qf, kf, proj, do, lse, delta, "attn_bwd")
    dqkv, g_qg, g_kg = _qk_prep_bwd(proj, dqh, dkh, dv, pos, invf, qg, kg, bd, "qk_prep_bwd")
    (drec, g_rcw, g_rcb, g_wrg, g_wig, g_brg, g_big, g_lam, g_grec) = _rec_bwd(
        dmix, proj, xc, hstate, rcw8, rec_conv_b, wrg_bd, wig_bd, brg, big, lru_lambda, g_rec_out, "rec_bwd")
    g_inT = _mm(dqkv, h1, "tn", BF16, "g_w_in_qkv", tm=512, o_rows=IN_W)
    g_inT = _mm(drec, h1, "tn", BF16, "g_w_in_rec", tm=512, into=g_inT, o_moff=3 * ATTN_W // 512)
    g_inT = g_inT.reshape(N_DEV, IN_W // N_DEV, 1024)
    in_send, in_recv, (g_inT,), in_land, in_token = _exchange_start(
        [g_inT], [(N_PEERS,) + g_inT.shape[1:]], "scatter", drec, "rs_in_start")
    dh1 = _mm(dqkv, W_inT, "nn", F32, "d_h1_qkv", tk=512, k=3 * ATTN_W)
    dh1 = _mm(drec, W_inT, "nn", F32, "d_h1_rec", tk=512, k=2 * REC_W, b_koff=3 * ATTN_W // 512, add=dh1)
    grad_x, _, g_gmix = _rmsnorm_bwd(xs, dh1, dx2, g_mix + in_token[0, 0], "norm_mix_bwd")

    blocks = lambda g: jnp.stack([g[64 * n:64 * n + 64, 64 * n:64 * n + 64] for n in range(8)])
    small_g = {
        "g_mix": g_gmix, "q_norm_g": g_qg.reshape(N_HEADS, HEAD_DIM).sum(0), "k_norm_g": g_kg.reshape(N_HEADS, HEAD_DIM).sum(0),
        "rec_conv_b": g_rcb, "w_rg": blocks(g_wrg), "b_rg": g_brg, "w_ig": blocks(g_wig), "b_ig": g_big,
        "lru_lambda": g_lam, "g_attn_out": g_gattn, "g_rec_out": g_grec, "g_ffn": g_gffn,
        "ffn_conv_b": jnp.concatenate([g_fcbg, g_fcbu], axis=1)}
    g_fcw = jnp.concatenate([g_fcwg[:3], g_fcwu[:3]], axis=1)
    flat = jnp.concatenate([small_g[k].reshape(-1) for k, _ in _SMALL]
                           + [g_rcw[:4].reshape(-1), g_fcw.reshape(-1), loss_mine.reshape(1)])
    flat = jnp.pad(flat, (0, SMALL_ROWS * 1024 - flat.shape[0])).reshape(SMALL_ROWS, 1024)
    tot = _sum_devices(_all_gather([flat], "ag_small_grads")[0], "sum_small_grads").reshape(-1)
    g_small, o = {}, 0
    for k, n in _SMALL:
        g_small[k] = tot[o:o + n]
        o += n
    g_rcw_mine = lax.dynamic_slice(tot[o:o + 2048].reshape(4, REC_W), (0, 64 * dev), (4, 64))
    g_fcw_mine = lax.dynamic_slice(tot[o + 2048:o + 2048 + 18432].reshape(3, 2 * D_FF), (0, 768 * dev), (3, 768))
    loss = tot[o + 2048 + 18432]

    devi = jnp.reshape(dev, (1,)).astype(jnp.int32)
    ffn_g, ffn_land = _exchange_wait(rs_send, rs_recv, ffn_g, ffn_land, "scatter", tot, "rs_ffn_wait")
    (g_out,), out_land = _exchange_wait(out_send, out_recv, [g_out], out_land, "scatter", tot, "rs_out_wait")
    (g_inT,), in_land = _exchange_wait(in_send, in_recv, [g_inT], in_land, "scatter", tot, "rs_in_wait")
    big_out = {"grad": {}, "delta": {}, "new_m": {}, "new_v": {}}
    for nm, p, r in (("w_up", ffn_g[0], ffn_land[0]), ("w_down", ffn_g[1], ffn_land[1]), ("w_out", g_out, out_land[0]),
                     ("w_in", g_inT, in_land[0])):
        w_, m_, v_ = shards[nm]
        res = _adam_sharded(p, r, devi, w_, m_, v_, "adam_" + nm)
        for kind, a in zip(("grad", "delta", "new_m", "new_v"), res):
            big_out[kind][nm] = a.T[None] if nm in ("w_in", "w_up") else a[None]
    given = dict(g_mix=g_mix, q_norm_g=q_norm_g, k_norm_g=k_norm_g, rec_conv_b=rec_conv_b, w_rg=w_rg, b_rg=b_rg, w_ig=w_ig,
                 b_ig=b_ig, lru_lambda=lru_lambda, g_attn_out=g_attn_out, g_rec_out=g_rec_out, g_ffn=g_ffn, ffn_conv_b=ffn_conv_b)
    given_m = dict(g_mix=m_g_mix, q_norm_g=m_q_norm_g, k_norm_g=m_k_norm_g, rec_conv_b=m_rec_conv_b, w_rg=m_w_rg, b_rg=m_b_rg,
                   w_ig=m_w_ig, b_ig=m_b_ig, lru_lambda=m_lru_lambda, g_attn_out=m_g_attn_out, g_rec_out=m_g_rec_out,
                   g_ffn=m_g_ffn, ffn_conv_b=m_ffn_conv_b)
    given_v = dict(g_mix=v_g_mix, q_norm_g=v_q_norm_g, k_norm_g=v_k_norm_g, rec_conv_b=v_rec_conv_b, w_rg=v_w_rg, b_rg=v_b_rg,
                   w_ig=v_w_ig, b_ig=v_b_ig, lru_lambda=v_lru_lambda, g_attn_out=v_g_attn_out, g_rec_out=v_g_rec_out,
                   g_ffn=v_g_ffn, ffn_conv_b=v_ffn_conv_b)
    ws = _pack_small(given, rec_conv_w, ffn_conv_w)
    gs = _pack_small(g_small, g_rcw_mine, g_fcw_mine)
    ms = _pack_small(given_m, m_rec_conv_w, m_ffn_conv_w)
    vs = _pack_small(given_v, v_rec_conv_w, v_ffn_conv_w)
    ds, m2s, v2s = _adam_small(ws, gs, ms, vs, "adam_small")
    small_out = {"grad": _unpack_small(gs), "delta": _unpack_small(ds), "new_m": _unpack_small(m2s), "new_v": _unpack_small(v2s)}

    order = ("g_mix", "w_in", "q_norm_g", "k_norm_g", "rec_conv_w", "rec_conv_b", "w_rg", "b_rg", "w_ig", "b_ig",
             "lru_lambda", "g_attn_out", "g_rec_out", "w_out", "g_ffn", "w_up", "ffn_conv_w", "ffn_conv_b", "w_down")
    outs = [loss, grad_x.reshape(1, T, D_MODEL)]
    for kind in ("grad", "delta", "new_m", "new_v"):
        for name in order:
            outs.append(big_out[kind][name] if name in big_out[kind] else small_out[kind][name])
    return tuple(outs)
```

```python
import math

import numpy as np
import jax
import jax.numpy as jnp
from jax import lax
from jax.experimental import pallas as pl
from jax.experimental.pallas import tpu as pltpu

F32 = jnp.float32
BF16 = jnp.bfloat16

D_MODEL = 1024
HEAD_DIM = 64
ATTN_W = 512
REC_W = 512
N_HEADS = 8
D_FF = 3072
IN_W = 2560
REC_CONV = 4
FFN_CONV = 3
LRU_C = 8.0
ROPE_THETA = 10000.0
EPS = 1e-6
NEG_INF = -1e30
QBLK = 128
DILATIONS = (1, 4, 16)
N_DEV = 8
SMALL_ROWS = 96
ADAM_LR, ADAM_B1, ADAM_B2, ADAM_EPS, ADAM_WD, ADAM_STEP = 0.001, 0.9, 0.999, 1e-08, 0.01, 10
MESH = pl.DeviceIdType.MESH
ANY = pl.BlockSpec(memory_space=pl.ANY)


def _call(body, *, name, **kw):
    return pl.pallas_call(body, name=name, **kw)


def _params(*sem):
    return pltpu.CompilerParams(dimension_semantics=sem, vmem_limit_bytes=56 * 1024 * 1024)


def _gelu(x):
    c = math.sqrt(2.0 / math.pi)
    return 0.5 * x * (1.0 + jnp.tanh(c * (x + 0.044715 * (x * x * x))))


def _gelu_and_grad(x):
    c = math.sqrt(2.0 / math.pi)
    t = jnp.tanh(c * (x + 0.044715 * (x * x * x)))
    g = 0.5 * x * (1.0 + t)
    dg = 0.5 * (1.0 + t) + 0.5 * x * (1.0 - t * t) * (c * (1.0 + 3.0 * 0.044715 * (x * x)))
    return g, dg


def _sigmoid(x):
    return 1.0 / (1.0 + jnp.exp(-x))


def _softplus_neg(lam):
    y = jnp.exp(-jnp.abs(lam))
    u = 1.0 + y
    log1p = jnp.where(u == 1.0, y, jnp.log(u) * y / jnp.where(u == 1.0, 1.0, u - 1.0))
    return jnp.maximum(-lam, 0.0) + log1p


_NN = (((1,), (0,)), ((), ()))
_NT = (((1,), (1,)), ((), ()))
_TN = (((0,), (0,)), ((), ()))


def _dot(a, b, dims=_NN):
    return lax.dot_general(a, b, dims, preferred_element_type=F32)


def _group_mean(v, bd):
    hi = v.astype(BF16)
    lo = (v - hi.astype(F32)).astype(BF16)
    w = bd.shape[0]
    return jnp.concatenate([_dot(hi[:, c:c + w], bd) + _dot(lo[:, c:c + w], bd) for c in range(0, v.shape[1], w)], axis=1)


def _rope_tables(pos_ref, invf_ref):
    ang = pos_ref[...].astype(F32) * invf_ref[:, :2 * HEAD_DIM]
    reps = invf_ref.shape[1] // (2 * HEAD_DIM)
    return jnp.tile(jnp.cos(ang), (1, reps)), jnp.tile(jnp.sin(ang), (1, reps))


def _shift_down(x, halo, s):
    rolled = pltpu.roll(x, s, 0)
    hr = pltpu.roll(halo, s, 0)
    row = lax.broadcasted_iota(jnp.int32, hr.shape, 0)
    first = jnp.where(row < s, hr, rolled[:8])
    return jnp.concatenate([first, rolled[8:]], axis=0)


def _shift_up(x, halo, s):
    n = x.shape[0]
    rolled = pltpu.roll(x, n - s, 0)
    hr = pltpu.roll(halo, 8 - s, 0)
    row = lax.broadcasted_iota(jnp.int32, hr.shape, 0)
    last = jnp.where(row >= 8 - s, hr, rolled[n - 8:])
    return jnp.concatenate([rolled[:n - 8], last], axis=0)


def _scan_fwd(a, u):
    n = a.shape[0]
    row = lax.broadcasted_iota(jnp.int32, a.shape, 0)
    s = 1
    while s < n:
        a_s = jnp.where(row < s, 1.0, pltpu.roll(a, s, 0))
        u_s = jnp.where(row < s, 0.0, pltpu.roll(u, s, 0))
        u = u + a * u_s
        a = a * a_s
        s *= 2
    return a, u


def _scan_bwd(b, v):
    n = b.shape[0]
    row = lax.broadcasted_iota(jnp.int32, b.shape, 0)
    s = 1
    while s < n:
        b_s = jnp.where(row >= n - s, 1.0, pltpu.roll(b, n - s, 0))
        v_s = jnp.where(row >= n - s, 0.0, pltpu.roll(v, n - s, 0))
        v = v + b * v_s
        b = b * b_s
        s *= 2
    return b, v


def _rot_half(y):
    n = y.shape[1]
    lane = lax.broadcasted_iota(jnp.int32, y.shape, 1) & (HEAD_DIM - 1)
    return jnp.where(lane < HEAD_DIM // 2, -pltpu.roll(y, n - HEAD_DIM // 2, 1), pltpu.roll(y, HEAD_DIM // 2, 1))


def _row_tile(r, cap=256):
    return max(t for t in range(16, cap + 1, 16) if r % t == 0)


def _all_gather(shards, name):
    na = len(shards)
    ms = [s.shape[0] for s in shards]

    def body(*refs):
        x_refs, out_refs = refs[:na], refs[na:2 * na]
        send_sems, recv_sems, local_sems = refs[2 * na:]
        x, y, c = lax.axis_index("x"), lax.axis_index("y"), lax.axis_index("c")
        me, sibling = (x, y, c), (x, y, 1 - c)
        chips = [(1 - x, y), (x, 1 - y), (1 - x, 1 - y)]

        def rows(a, px, py, pc):
            return out_refs[a].at[pl.ds((4 * px + 2 * py + pc) * ms[a], ms[a]), :]

        def copy(a, k, block, to, src=None):
            return pltpu.make_async_remote_copy(
                src_ref=rows(a, *block) if src is None else src, dst_ref=rows(a, *block),
                send_sem=send_sems.at[7 * a + k], recv_sem=recv_sems.at[7 * a + k], device_id=to, device_id_type=MESH)

        mine = [pltpu.make_async_copy(x_refs[a], rows(a, *me), local_sems.at[a]) for a in range(na)]
        first = []
        for a in range(na):
            mine[a].start()
            first.append(copy(a, 0, me, sibling, src=x_refs[a]))
            first += [copy(a, 1 + j, me, (*chip, c), src=x_refs[a]) for j, chip in enumerate(chips)]
        for cp in first:
            cp.start()
        passed = []
        for a in range(na):
            for j, chip in enumerate(chips):
                copy(a, 1 + j, (*chip, c), me).wait_recv()
                fw = copy(a, 4 + j, (*chip, c), sibling)
                fw.start()
                passed.append(fw)
        for a in range(na):
            copy(a, 0, sibling, me).wait_recv()
            for j, chip in enumerate(chips):
                copy(a, 4 + j, (*chip, 1 - c), me).wait_recv()
        for cp in first + passed:
            cp.wait_send()
        for cp in mine:
            cp.wait()

    return _call(
        body, name=name, out_shape=[jax.ShapeDtypeStruct((N_DEV * s.shape[0], s.shape[1]), s.dtype) for s in shards],
        in_specs=[ANY] * na, out_specs=[ANY] * na,
        scratch_shapes=[pltpu.SemaphoreType.DMA((7 * na,)), pltpu.SemaphoreType.DMA((7 * na,)),
                        pltpu.SemaphoreType.DMA((na,))],
    )(*shards)


HBM = pl.BlockSpec(memory_space=pltpu.HBM)
SEM = pl.BlockSpec(memory_space=pltpu.SEMAPHORE)
EFFECT = pltpu.SideEffectType.DATAFLOW_SIDE_EFFECTING
N_PEERS = N_DEV - 1


def _peer(k):
    x, y, c = lax.axis_index("x"), lax.axis_index("y"), lax.axis_index("c")
    b = k + 1
    flip = lambda v, bit: 1 - v if bit else v
    return flip(x, b & 4), flip(y, b & 2), flip(c, b & 1)


def _in_hbm(a):
    return pltpu.with_memory_space_constraint(a, pltpu.HBM)


def _split_copy_descr(na, kind, src_refs, land_refs, send_sems, recv_sems):
    x, y, c = lax.axis_index("x"), lax.axis_index("y"), lax.axis_index("c")
    me = 4 * x + 2 * y + c
    copies = []
    for a in range(na):
        for k in range(N_PEERS):
            px, py, pc = _peer(k)
            if kind == "gather":
                m = src_refs[a].shape[0]
                src, dst = src_refs[a], land_refs[a].at[pl.ds(me * m, m), :]
            else:
                src, dst = src_refs[a].at[4 * px + 2 * py + pc], land_refs[a].at[k]
            copies.append(pltpu.make_async_remote_copy(
                src_ref=src, dst_ref=dst, send_sem=send_sems.at[N_PEERS * a + k], recv_sem=recv_sems.at[N_PEERS * a + k],
                device_id=(px, py, pc), device_id_type=MESH))
    return copies


def _landing(shape, dtype, own=None, at=None):
    buf = lax.empty(shape, dtype)
    return buf if own is None else lax.dynamic_update_slice(buf, own, (at, 0))


def _exchange_start(srcs, lands, kind, after, name):
    na = len(srcs)
    land_shapes = [l.shape for l in lands]

    def body(*refs):
        src_refs, land_refs = refs[:na], refs[na:2 * na]
        send_sems, recv_sems = refs[2 * na + 1], refs[2 * na + 2]
        token = refs[-1]
        for cp in _split_copy_descr(na, kind, src_refs, land_refs, send_sems, recv_sems):
            cp.start()
        token[...] = jnp.zeros_like(token)

    lands = [_in_hbm(l) for l in lands]
    sem = pltpu.SemaphoreType.DMA((N_PEERS * na,))
    outs = _call(
        body, name=name,
        out_shape=[sem, sem] + [pltpu.HBM(s.shape, s.dtype) for s in srcs] + [pltpu.HBM(s, srcs[0].dtype) for s in land_shapes]
        + [jax.ShapeDtypeStruct((8, 128), F32)],
        in_specs=[HBM] * (2 * na) + [ANY], out_specs=[SEM, SEM] + [HBM] * (2 * na) + [pl.BlockSpec(memory_space=pltpu.VMEM)],
        input_output_aliases={i: 2 + i for i in range(2 * na)},
        compiler_params=pltpu.CompilerParams(has_side_effects=EFFECT),
    )(*[_in_hbm(s) for s in srcs], *lands, after)
    return outs[0], outs[1], outs[2:2 + na], outs[2 + na:2 + 2 * na], outs[-1]


def _exchange_wait(send_sems, recv_sems, srcs, lands, kind, after, name):
    na = len(srcs)

    def body(*refs):
        src_refs, land_refs = refs[:na], refs[na:2 * na]
        s_sems, r_sems = refs[2 * na], refs[2 * na + 1]
        for cp in _split_copy_descr(na, kind, src_refs, land_refs, s_sems, r_sems):
            cp.wait_send()
            cp.wait_recv()

    outs = _call(
        body, name=name, out_shape=[pltpu.HBM(s.shape, s.dtype) for s in srcs] + [pltpu.HBM(l.shape, l.dtype) for l in lands],
        in_specs=[HBM] * (2 * na) + [SEM, SEM, ANY], out_specs=[HBM] * (2 * na),
        input_output_aliases={i: i for i in range(2 * na)},
        compiler_params=pltpu.CompilerParams(has_side_effects=EFFECT),
    )(*srcs, *lands, send_sems, recv_sems, after)
    return outs[:na], outs[na:]


def _mm(a, b, mode, out_dtype, name, add=None, tm=1024, tn=1024, tk=1024, b_noff=0, b_koff=0,
        n=None, k=None, into=None, o_rows=None, o_moff=0, loss_target=None):
    if mode == "tn":
        K, M = a.shape
    else:
        M, K = a.shape
    N = n if n is not None else (b.shape[0] if mode == "nt" else b.shape[1])
    if k is not None:
        assert k == K
    tm, tn, tk = min(tm, M), min(tn, N), min(tk, K)
    assert M % tm == 0 and N % tn == 0 and K % tk == 0, (name, M, N, K)
    nk = K // tk
    if mode == "nn":
        a_spec = pl.BlockSpec((tm, tk), lambda i, j, kk: (i, kk))
        b_spec, dims = pl.BlockSpec((tk, tn), lambda i, j, kk: (kk + b_koff, j + b_noff)), _NN
    elif mode == "nt":
        a_spec = pl.BlockSpec((tm, tk), lambda i, j, kk: (i, kk))
        b_spec, dims = pl.BlockSpec((tn, tk), lambda i, j, kk: (j + b_noff, kk + b_koff)), _NT
    else:
        a_spec = pl.BlockSpec((tk, tm), lambda i, j, kk: (kk, i))
        b_spec, dims = pl.BlockSpec((tk, tn), lambda i, j, kk: (kk + b_koff, j + b_noff)), _TN
    o_spec = pl.BlockSpec((tm, tn), lambda i, j, kk: (i + o_moff, j))
    has_add, has_into, has_loss = add is not None, into is not None, loss_target is not None
    assert not has_loss or (has_add and tn == N and not has_into)
    n_in = 2 + has_add + has_loss + has_into

    def body(*refs):
        a_ref, b_ref = refs[0], refs[1]
        add_ref = refs[2] if has_add else None
        outs = refs[n_in:]

        def finish(r):
            if has_add:
                r = r + add_ref[...]
            if has_loss:
                e = r - refs[3][...]
                dy = e * (1.0 / N)
                outs[0][...] = dy
                outs[1][...] = dy.astype(BF16)
                outs[2][...] = jnp.sum(e * e, axis=0, keepdims=True)[None]
            else:
                outs[0][...] = r.astype(out_dtype)

        if nk == 1:
            finish(_dot(a_ref[...], b_ref[...], dims))
        else:
            acc = refs[-1]
            kk = pl.program_id(2)

            @pl.when(kk == 0)
            def _():
                acc[...] = _dot(a_ref[...], b_ref[...], dims)

            @pl.when((kk > 0) & (kk < nk - 1))
            def _():
                acc[...] += _dot(a_ref[...], b_ref[...], dims)

            @pl.when(kk == nk - 1)
            def _():
                finish(acc[...] + _dot(a_ref[...], b_ref[...], dims))

    tile = pl.BlockSpec((tm, tn), lambda i, j, kk: (i, j))
    ins = [a, b] + ([add] if has_add else []) + ([loss_target] if has_loss else []) + ([into] if has_into else [])
    specs = [a_spec, b_spec] + [tile] * (has_add + has_loss) + ([ANY] if has_into else [])
    rows = into.shape[0] if has_into else (o_rows if o_rows is not None else M)
    if has_loss:
        out_specs = [tile, tile, pl.BlockSpec((1, 1, N), lambda i, j, kk: (i, 0, 0))]
        out_shape = [jax.ShapeDtypeStruct((M, N), F32), jax.ShapeDtypeStruct((M, N), BF16), jax.ShapeDtypeStruct((M // tm, 1, N), F32)]
    else:
        out_specs, out_shape = o_spec, jax.ShapeDtypeStruct((rows, N), out_dtype)
    return _call(
        body, name=name, grid=(M // tm, N // tn, nk), in_specs=specs, out_specs=out_specs, out_shape=out_shape,
        scratch_shapes=[pltpu.VMEM((tm, tn), F32)] if nk > 1 else [],
        input_output_aliases={len(ins) - 1: 0} if has_into else {},
        compiler_params=_params("parallel", "parallel", "arbitrary"),
    )(*ins)


def _rmsnorm(x, g, name, tm=512):
    T, D = x.shape

    def body(x_ref, g_ref, o_ref):
        xv = x_ref[...]
        r = lax.rsqrt(jnp.mean(xv * xv, axis=-1, keepdims=True) + EPS)
        o_ref[...] = (xv * r * g_ref[...]).astype(BF16)

    return _call(
        body, name=name, grid=(T // tm,),
        in_specs=[pl.BlockSpec((tm, D), lambda i: (i, 0)), pl.BlockSpec((1, D), lambda i: (0, 0))],
        out_specs=pl.BlockSpec((tm, D), lambda i: (i, 0)), out_shape=jax.ShapeDtypeStruct((T, D), BF16),
        compiler_params=_params("parallel"),
    )(x, g)


def _rmsnorm_bwd(x, dh, resid, g, name, tm=512):
    T, D = x.shape

    def body(x_ref, dh_ref, res_ref, g_ref, dx_ref, dxb_ref, dg_ref):
        @pl.when(pl.program_id(0) == 0)
        def _():
            dg_ref[...] = jnp.zeros_like(dg_ref)

        xv, dhv = x_ref[...], dh_ref[...]
        r = lax.rsqrt(jnp.mean(xv * xv, axis=-1, keepdims=True) + EPS)
        gd = dhv * g_ref[...]
        m = jnp.mean(gd * xv, axis=-1, keepdims=True)
        dx = res_ref[...] + r * gd - xv * (r * r * r) * m
        dx_ref[...] = dx
        dxb_ref[...] = dx.astype(BF16)
        dg_ref[...] += jnp.sum(dhv * xv * r, axis=0, keepdims=True)

    row = pl.BlockSpec((tm, D), lambda i: (i, 0))
    vec = pl.BlockSpec((1, D), lambda i: (0, 0))
    return _call(
        body, name=name, grid=(T // tm,), in_specs=[row, row, row, vec], out_specs=[row, row, vec],
        out_shape=[jax.ShapeDtypeStruct((T, D), F32), jax.ShapeDtypeStruct((T, D), BF16), jax.ShapeDtypeStruct((1, D), F32)],
        compiler_params=_params("arbitrary"),
    )(x, dh, resid, g)


def _qk_prep(proj, pos, invf, qg, kg, bd, name, tm=512):
    T = proj.shape[0]

    def body(q_ref, k_ref, pos_ref, invf_ref, qg_ref, kg_ref, bd_ref, qo_ref, ko_ref):
        cos, sin = _rope_tables(pos_ref, invf_ref)

        def prep(xv, gv, scale):
            r = lax.rsqrt(_group_mean(xv * xv, bd_ref[...]) + EPS)
            yv = xv * r * gv
            return ((yv * cos + _rot_half(yv) * sin) * scale).astype(BF16).astype(F32)

        qo_ref[...] = prep(q_ref[...], qg_ref[...], HEAD_DIM ** -0.5)
        ko_ref[...] = prep(k_ref[...], kg_ref[...], 1.0)

    col = lambda j: pl.BlockSpec((tm, ATTN_W), lambda i, j=j: (i, j))
    vec = pl.BlockSpec((1, ATTN_W), lambda i: (0, 0))
    out = pl.BlockSpec((tm, ATTN_W), lambda i: (i, 0))
    return _call(
        body, name=name, grid=(T // tm,),
        in_specs=[col(0), col(1), pl.BlockSpec((tm, 1), lambda i: (i, 0)), vec, vec, vec,
                  pl.BlockSpec((2 * HEAD_DIM, 2 * HEAD_DIM), lambda i: (0, 0))],
        out_specs=[out, out], out_shape=[jax.ShapeDtypeStruct((T, ATTN_W), F32)] * 2,
        compiler_params=_params("parallel"),
    )(proj, proj, pos, invf, qg, kg, bd)


def _qk_prep_bwd(proj, dqh, dkh, dv, pos, invf, qg, kg, bd, name, tm=512):
    T = proj.shape[0]

    def body(q_ref, k_ref, dq_ref, dk_ref, dv_ref, pos_ref, invf_ref, qg_ref, kg_ref, bd_ref, o_ref, gq_ref, gk_ref):
        @pl.when(pl.program_id(0) == 0)
        def _():
            gq_ref[...] = jnp.zeros_like(gq_ref)
            gk_ref[...] = jnp.zeros_like(gk_ref)

        cos, sin = _rope_tables(pos_ref, invf_ref)

        def back(xv, gv, dz, scale):
            dz = dz * scale
            dy = dz * cos - _rot_half(dz * sin)
            r = lax.rsqrt(_group_mean(xv * xv, bd_ref[...]) + EPS)
            gd = dy * gv
            m = _group_mean(gd * xv, bd_ref[...])
            dx = r * gd - xv * (r * r * r) * m
            return dx, jnp.sum(dy * xv * r, axis=0, keepdims=True)

        dxq, gs = back(q_ref[...], qg_ref[...], dq_ref[...], HEAD_DIM ** -0.5)
        gq_ref[...] += gs
        dxk, gs = back(k_ref[...], kg_ref[...], dk_ref[...], 1.0)
        gk_ref[...] += gs
        o_ref[...] = jnp.concatenate([dxq.astype(BF16), dxk.astype(BF16), dv_ref[...].astype(BF16)], axis=1)

    col = lambda j: pl.BlockSpec((tm, ATTN_W), lambda i, j=j: (i, j))
    row = pl.BlockSpec((tm, ATTN_W), lambda i: (i, 0))
    vec = pl.BlockSpec((1, ATTN_W), lambda i: (0, 0))
    return _call(
        body, name=name, grid=(T // tm,),
        in_specs=[col(0), col(1), row, row, row, pl.BlockSpec((tm, 1), lambda i: (i, 0)), vec, vec, vec,
                  pl.BlockSpec((2 * HEAD_DIM, 2 * HEAD_DIM), lambda i: (0, 0))],
        out_specs=[pl.BlockSpec((tm, 3 * ATTN_W), lambda i: (i, 0)), vec, vec],
        out_shape=[jax.ShapeDtypeStruct((T, 3 * ATTN_W), BF16)] + [jax.ShapeDtypeStruct((1, ATTN_W), F32)] * 2,
        compiler_params=_params("arbitrary"),
    )(proj, proj, dqh, dkh, dv, pos, invf, qg, kg, bd)


def _ld(ref, start, size, dil):
    return ref[pl.ds(start, size), :] if dil == 1 else ref[pl.ds(start, size, stride=dil), :]


def _st(ref, start, size, dil, val):
    if dil == 1:
        ref[pl.ds(start, size), :] = val
    else:
        ref[pl.ds(start, size, stride=dil), :] = val


def _attn_geometry(T, dil):
    nb = T // dil // QBLK
    kw = 2 * QBLK if nb >= 2 else QBLK
    return nb, kw


ATTN_UNROLL = 4


def _attn_unit(j, u, dil, nit):
    return ATTN_UNROLL * j + u if dil >= ATTN_UNROLL else j + u * (nit // ATTN_UNROLL)


def _attn_block(it, dil, kw):
    c, n = it & (dil - 1), lax.shift_right_logical(it, dil.bit_length() - 1)
    sq = n * (QBLK * dil) + c
    sk = jnp.maximum(n - (kw // QBLK - 1), 0) * (QBLK * dil) + c
    qi = lax.broadcasted_iota(jnp.int32, (2 * QBLK, kw), 0) & (QBLK - 1)
    kj = lax.broadcasted_iota(jnp.int32, (2 * QBLK, kw), 1)
    rel = jnp.where(n > 0, kw - QBLK, 0) + qi - kj
    return sq, sk, (rel >= 0) & (rel <= QBLK)


def _stack_heads(xv, head0):
    z = jnp.zeros_like(xv)
    return jnp.concatenate([jnp.where(head0, xv, z), jnp.where(head0, z, xv)], axis=0)


def _unstack_heads(x2, head0):
    return jnp.where(head0, x2[:QBLK], x2[QBLK:])


def _attn_fwd(qf, kf, proj, name):
    T = qf.shape[0]

    def body(q_ref, k_ref, v_ref, o_ref, lse_ref):
        head0 = lax.broadcasted_iota(jnp.int32, (QBLK, 2 * HEAD_DIM), 1) < HEAD_DIM
        for bi, dil in enumerate(DILATIONS):
            nb, kw = _attn_geometry(T, dil)

            nit = nb * dil

            def step(j, carry, bi=bi, dil=dil, kw=kw, nit=nit):
                units = []
                for u in range(ATTN_UNROLL):
                    sq, sk, ok = _attn_block(_attn_unit(j, u, dil, nit), dil, kw)
                    old = (_ld(o_ref, sq, QBLK, dil), _ld(lse_ref, sq, QBLK, dil)) if bi > 0 else None
                    units.append((sq, ok, _ld(q_ref, sq, QBLK, dil).astype(BF16), _ld(k_ref, sk, kw, dil).astype(BF16),
                                  _ld(v_ref, sk, kw, dil).astype(BF16), old))
                results = []
                for sq, ok, qv, kv, vv, old in units:
                    s = jnp.where(ok, _dot(_stack_heads(qv, head0), kv, _NT), NEG_INF)
                    m = jnp.max(s, axis=-1, keepdims=True)
                    p = jnp.exp(s - m).astype(BF16)
                    acc = _dot(p, jnp.concatenate([vv, jnp.ones_like(vv)], axis=1))
                    l = acc[:, 2 * HEAD_DIM:]
                    o_new = _unstack_heads(acc[:, :2 * HEAD_DIM] / l, head0)
                    l_new = _unstack_heads(m + jnp.log(l), head0)
                    if bi > 0:
                        o_old, l_old = old
                        mx = jnp.maximum(l_old, l_new)
                        e0, e1 = jnp.exp(l_old - mx), jnp.exp(l_new - mx)
                        z = e0 + e1
                        o_new = (e0 * o_old + e1 * o_new) / z
                        l_new = mx + jnp.log(z)
                    results.append((sq, o_new, l_new))
                for sq, o_new, l_new in results:
                    _st(o_ref, sq, QBLK, dil, o_new)
                    _st(lse_ref, sq, QBLK, dil, l_new)
                return carry

            lax.fori_loop(0, nit // ATTN_UNROLL, step, 0)

    blk = lambda off: pl.BlockSpec((T, 2 * HEAD_DIM), lambda hp, off=off: (0, off + hp))
    return _call(
        body, name=name, grid=(4,), in_specs=[blk(0), blk(0), blk(8)], out_specs=[blk(0), blk(0)],
        out_shape=[jax.ShapeDtypeStruct((T, ATTN_W), F32)] * 2, compiler_params=_params("parallel"),
    )(qf, kf, proj)


def _attn_bwd(qf, kf, proj, do, lse, delta, name):
    T = qf.shape[0]

    def body(q_ref, k_ref, v_ref, do_ref, lse_ref, dl_ref, dq_ref, dk_ref, dv_ref):
        head0 = lax.broadcasted_iota(jnp.int32, (QBLK, 2 * HEAD_DIM), 1) < HEAD_DIM
        for ref in (dq_ref, dk_ref, dv_ref):
            ref[...] = jnp.zeros_like(ref)
        for dil in DILATIONS:
            nb, kw = _attn_geometry(T, dil)

            nit = nb * dil

            def step(j, carry, dil=dil, kw=kw, nit=nit):
                units = []
                for u in range(ATTN_UNROLL):
                    sq, sk, ok = _attn_block(_attn_unit(j, u, dil, nit), dil, kw)
                    lsev, dlv = _ld(lse_ref, sq, QBLK, dil), _ld(dl_ref, sq, QBLK, dil)
                    units.append((sq, sk, ok, _ld(q_ref, sq, QBLK, dil).astype(BF16), _ld(do_ref, sq, QBLK, dil).astype(BF16),
                                  jnp.concatenate([lsev[:, 0:1], lsev[:, HEAD_DIM:HEAD_DIM + 1]], axis=0),
                                  jnp.concatenate([dlv[:, 0:1], dlv[:, HEAD_DIM:HEAD_DIM + 1]], axis=0),
                                  _ld(k_ref, sk, kw, dil).astype(BF16), _ld(v_ref, sk, kw, dil).astype(BF16),
                                  _ld(dq_ref, sq, QBLK, dil), _ld(dk_ref, sk, kw, dil), _ld(dv_ref, sk, kw, dil)))
                results = []
                for sq, sk, ok, qv, dov, lse2, dl2, kv, vv, dq0, dk0, dv0 in units:
                    q2, do2 = _stack_heads(qv, head0), _stack_heads(dov, head0)
                    p = jnp.where(ok, jnp.exp(_dot(q2, kv, _NT) - lse2), 0.0)
                    ds = (p * (_dot(do2, vv, _NT) - dl2)).astype(BF16)
                    results.append((sq, sk, dq0 + _unstack_heads(_dot(ds, kv), head0),
                                    dk0 + _dot(ds, q2, _TN), dv0 + _dot(p.astype(BF16), do2, _TN)))
                for sq, sk, dq, dk, dv in results:
                    _st(dq_ref, sq, QBLK, dil, dq)
                    _st(dk_ref, sk, kw, dil, dk)
                    _st(dv_ref, sk, kw, dil, dv)
                return carry

            lax.fori_loop(0, nit // ATTN_UNROLL, step, 0)

    blk = lambda off: pl.BlockSpec((T, 2 * HEAD_DIM), lambda hp, off=off: (0, off + hp))
    return _call(
        body, name=name, grid=(4,), in_specs=[blk(0), blk(0), blk(8), blk(0), blk(0), blk(0)], out_specs=[blk(0)] * 3,
        out_shape=[jax.ShapeDtypeStruct((T, ATTN_W), F32)] * 3, compiler_params=_params("parallel"),
    )(qf, kf, proj, do, lse, delta)


def _attn_norm(attn, g, name, tm=512):
    T = attn.shape[0]

    def body(a_ref, g_ref, o_ref):
        av = a_ref[...]
        r = lax.rsqrt(jnp.mean(av * av, axis=-1, keepdims=True) + EPS)
        o_ref[...] = (av * r * g_ref[...]).astype(BF16)

    row = pl.BlockSpec((tm, ATTN_W), lambda i: (i, 0))
    return _call(
        body, name=name, grid=(T // tm,), in_specs=[row, pl.BlockSpec((1, ATTN_W), lambda i: (0, 0))], out_specs=row,
        out_shape=jax.ShapeDtypeStruct((T, 2 * ATTN_W), BF16), compiler_params=_params("parallel"),
    )(attn, g)


def _attn_norm_bwd(dmix, attn, g, bd, name, tm=512):
    T = attn.shape[0]

    def body(d_ref, a_ref, g_ref, bd_ref, do_ref, dl_ref, dg_ref):
        @pl.when(pl.program_id(0) == 0)
        def _():
            dg_ref[...] = jnp.zeros_like(dg_ref)

        dy, av = d_ref[...], a_ref[...]
        r = lax.rsqrt(jnp.mean(av * av, axis=-1, keepdims=True) + EPS)
        gd = dy * g_ref[...]
        m = jnp.mean(gd * av, axis=-1, keepdims=True)
        da = r * gd - av * (r * r * r) * m
        do_ref[...] = da
        dl_ref[...] = _group_mean(da * av, bd_ref[...]) * float(HEAD_DIM)
        dg_ref[...] += jnp.sum(dy * av * r, axis=0, keepdims=True)

    row = pl.BlockSpec((tm, ATTN_W), lambda i: (i, 0))
    vec = pl.BlockSpec((1, ATTN_W), lambda i: (0, 0))
    return _call(
        body, name=name, grid=(T // tm,),
        in_specs=[row, row, vec, pl.BlockSpec((2 * HEAD_DIM, 2 * HEAD_DIM), lambda i: (0, 0))], out_specs=[row, row, vec],
        out_shape=[jax.ShapeDtypeStruct((T, ATTN_W), F32)] * 2 + [jax.ShapeDtypeStruct((1, ATTN_W), F32)],
        compiler_params=_params("arbitrary"),
    )(dmix, attn, g, bd)


def _rec_gates(xc, wrg_ref, wig_ref, brg_ref, big_ref, lam_ref):
    xb = xc.astype(BF16)
    r = _sigmoid(_dot(xb, wrg_ref[...]) + brg_ref[...])
    ig = _sigmoid(_dot(xb, wig_ref[...]) + big_ref[...])
    sp = _softplus_neg(lam_ref[...])
    log_a = -LRU_C * r * sp
    a = jnp.exp(log_a)
    th = jnp.tanh(log_a)
    mult = jnp.sqrt(-2.0 * th / (1.0 - th))
    return xb, r, ig, sp, a, mult


def _rec_fwd(proj, mix, cw, cb, wrg, wig, brg, big, lam, g, name, tm=256):
    T = proj.shape[0]
    hb = tm // 8

    def body(xr_ref, halo_ref, gr_ref, cw_ref, cb_ref, wrg_ref, wig_ref, brg_ref, big_ref, lam_ref, g_ref, mix_ref,
             xc_ref, h_ref, out_ref, carry):
        i = pl.program_id(0)

        @pl.when(i == 0)
        def _():
            carry[...] = jnp.zeros_like(carry)

        xr = xr_ref[...]
        halo = jnp.where(i > 0, halo_ref[...], 0.0)
        xc = cb_ref[...] + cw_ref[3:4, :] * xr
        for s in range(1, REC_CONV):
            xc = xc + cw_ref[3 - s:4 - s, :] * _shift_down(xr, halo, s)
        xc_ref[...] = xc
        _, _, ig, _, a, mult = _rec_gates(xc, wrg_ref, wig_ref, brg_ref, big_ref, lam_ref)
        pa, hl = _scan_fwd(a, mult * (ig * xc))
        h = hl + pa * carry[0:1, :]
        h_ref[...] = h
        carry[0:1, :] = h_ref[pl.ds(tm - 1, 1), :]
        hg = h * _gelu(gr_ref[...])
        r = lax.rsqrt(jnp.mean(hg * hg, axis=-1, keepdims=True) + EPS)
        out_ref[...] = (hg * r * g_ref[...]).astype(BF16)

    vec = pl.BlockSpec((1, REC_W), lambda i: (0, 0))
    row = pl.BlockSpec((tm, REC_W), lambda i: (i, 0))
    mat = pl.BlockSpec((REC_W, REC_W), lambda i: (0, 0))
    return _call(
        body, name=name, grid=(T // tm,),
        in_specs=[pl.BlockSpec((tm, REC_W), lambda i: (i, 3)),
                  pl.BlockSpec((8, REC_W), lambda i: (jnp.maximum(i * hb - 1, 0), 3)),
                  pl.BlockSpec((tm, REC_W), lambda i: (i, 4)),
                  pl.BlockSpec((8, REC_W), lambda i: (0, 0)), vec, mat, mat, vec, vec, vec, vec, ANY],
        out_specs=[row, row, pl.BlockSpec((tm, REC_W), lambda i: (i, 1))],
        out_shape=[jax.ShapeDtypeStruct((T, REC_W), F32)] * 2 + [jax.ShapeDtypeStruct(mix.shape, BF16)],
        scratch_shapes=[pltpu.VMEM((8, REC_W), F32)], input_output_aliases={11: 2},
        compiler_params=_params("arbitrary"),
    )(proj, proj, proj, cw, cb, wrg, wig, brg, big, lam, g, mix)


def _rec_bwd(dmix, proj, xc, h, cw, cb, wrg, wig, brg, big, lam, g, name, tm=256):
    T = proj.shape[0]
    nt = T // tm
    hb = tm // 8

    def body(d_ref, xr_ref, xhalo_ref, gr_ref, xc_ref, h_ref, hhalo_ref, cw_ref, cb_ref, wrg_ref, wig_ref, brg_ref,
             big_ref, lam_ref, g_ref,
             drec_ref, gcw_ref, gcb_ref, gwrg_ref, gwig_ref, gbrg_ref, gbig_ref, glam_ref, gg_ref,
             g_carry, a_first, dxc_next, gsp):
        i = pl.program_id(0)
        first_tile = i == nt - 1

        @pl.when(i == 0)
        def _():
            for ref in (gcw_ref, gcb_ref, gwrg_ref, gwig_ref, gbrg_ref, gbig_ref, glam_ref, gg_ref,
                        g_carry, a_first, dxc_next, gsp):
                ref[...] = jnp.zeros_like(ref)

        xr, xc, hv = xr_ref[...], xc_ref[...], h_ref[...]
        xhalo = jnp.where(first_tile, 0.0, xhalo_ref[...])
        hhalo = jnp.where(first_tile, 0.0, hhalo_ref[...])
        xb, r, ig, sp, a, mult = _rec_gates(xc, wrg_ref, wig_ref, brg_ref, big_ref, lam_ref)
        h_prev = _shift_down(hv, hhalo, 1)
        ge, dge = _gelu_and_grad(gr_ref[...])
        hg = hv * ge
        rr = lax.rsqrt(jnp.mean(hg * hg, axis=-1, keepdims=True) + EPS)
        dy = d_ref[...]
        gd = dy * g_ref[...]
        dhg = rr * gd - hg * (rr * rr * rr) * jnp.mean(gd * hg, axis=-1, keepdims=True)
        gg_ref[...] += jnp.sum(dy * hg * rr, axis=0, keepdims=True)
        dgr = (dhg * hv * dge).astype(BF16)
        dh = dhg * ge
        b = _shift_up(a, jnp.broadcast_to(a_first[0:1, :], (8, REC_W)), 1)
        pb, gl = _scan_bwd(b, dh)
        gs = gl + pb * g_carry[0:1, :]
        g_carry[0:1, :] = gs[0:1, :]
        a_first[0:1, :] = a[0:1, :]
        da = gs * h_prev
        dmult = gs * (ig * xc)
        di = gs * (mult * xc)
        dxc = gs * (mult * ig)
        dlog_a = da * a - dmult * (a * a) / mult
        gsp[...] += jnp.sum(dlog_a * (-LRU_C * r), axis=0, keepdims=True)
        dzr = (dlog_a * (-LRU_C * sp)) * (r * (1.0 - r))
        dzi = di * (ig * (1.0 - ig))
        dzr_b, dzi_b = dzr.astype(BF16), dzi.astype(BF16)
        dxc = dxc + _dot(dzr_b, wrg_ref[...], _NT) + _dot(dzi_b, wig_ref[...], _NT)
        gwrg_ref[...] += _dot(xb, dzr_b, _TN)
        gwig_ref[...] += _dot(xb, dzi_b, _TN)
        gbrg_ref[...] += jnp.sum(dzr, axis=0, keepdims=True)
        gbig_ref[...] += jnp.sum(dzi, axis=0, keepdims=True)
        nxt = dxc_next[...]
        dxr = cw_ref[3:4, :] * dxc
        gcw_ref[3:4, :] += jnp.sum(dxc * xr, axis=0, keepdims=True)
        for s in range(1, REC_CONV):
            dxr = dxr + cw_ref[3 - s:4 - s, :] * _shift_up(dxc, nxt, s)
            gcw_ref[3 - s:4 - s, :] += jnp.sum(dxc * _shift_down(xr, xhalo, s), axis=0, keepdims=True)
        gcb_ref[...] += jnp.sum(dxc, axis=0, keepdims=True)
        dxc_next[...] = dxc[:8]
        drec_ref[...] = jnp.concatenate([dxr.astype(BF16), dgr], axis=1)

        @pl.when(first_tile)
        def _():
            glam_ref[...] = gsp[...] * (-_sigmoid(-lam_ref[...]))

    rev = lambda i: nt - 1 - i
    vec = pl.BlockSpec((1, REC_W), lambda i: (0, 0))
    row = pl.BlockSpec((tm, REC_W), lambda i: (rev(i), 0))
    mat = pl.BlockSpec((REC_W, REC_W), lambda i: (0, 0))
    cwb = pl.BlockSpec((8, REC_W), lambda i: (0, 0))
    halo = lambda c: pl.BlockSpec((8, REC_W), lambda i, c=c: (jnp.maximum(rev(i) * hb - 1, 0), c))
    return _call(
        body, name=name, grid=(nt,),
        in_specs=[pl.BlockSpec((tm, REC_W), lambda i: (rev(i), 1)),
                  pl.BlockSpec((tm, REC_W), lambda i: (rev(i), 3)), halo(3),
                  pl.BlockSpec((tm, REC_W), lambda i: (rev(i), 4)),
                  row, row, halo(0), cwb, vec, mat, mat, vec, vec, vec, vec],
        out_specs=[pl.BlockSpec((tm, 2 * REC_W), lambda i: (rev(i), 0)), cwb, vec, mat, mat, vec, vec, vec, vec],
        out_shape=[jax.ShapeDtypeStruct((T, 2 * REC_W), BF16)]
        + [jax.ShapeDtypeStruct((8, REC_W), F32), jax.ShapeDtypeStruct((1, REC_W), F32)]
        + [jax.ShapeDtypeStruct((REC_W, REC_W), F32)] * 2 + [jax.ShapeDtypeStruct((1, REC_W), F32)] * 4,
        scratch_shapes=[pltpu.VMEM((8, REC_W), F32)] * 3 + [pltpu.VMEM((1, REC_W), F32)],
        compiler_params=_params("arbitrary"),
    )(dmix, proj, proj, proj, xc, h, h, cw, cb, wrg, wig, brg, big, lam, g)


def _ffn_conv(x_ext, cw_ref, cb_ref):
    return (cb_ref[...] + cw_ref[2:3, :] * x_ext + cw_ref[1:2, :] * pltpu.roll(x_ext, 1, 0)
            + cw_ref[0:1, :] * pltpu.roll(x_ext, 2, 0))


def _up_proj_act(h2, w_upT, cw, cb, name, tm=1024, tc=768):
    T = h2.shape[0]
    nc = D_FF // tc

    def body(h_ref, wg_ref, wu_ref, cwg_ref, cwu_ref, cbg_ref, cbu_ref, act_ref, pg_ref, pu_ref, hist_g, hist_u):
        i, j = pl.program_id(0), pl.program_id(1)
        hv = h_ref[...]
        pg, pu = _dot(hv, wg_ref[...], _NT), _dot(hv, wu_ref[...], _NT)
        ge = jnp.concatenate([jnp.where(i > 0, hist_g[j], 0.0), pg], axis=0)
        ue = jnp.concatenate([jnp.where(i > 0, hist_u[j], 0.0), pu], axis=0)
        act = _gelu(_ffn_conv(ge, cwg_ref, cbg_ref)) * _ffn_conv(ue, cwu_ref, cbu_ref)
        act_ref[...] = act[8:].astype(BF16)
        pg_ref[...] = pg.astype(BF16)
        pu_ref[...] = pu.astype(BF16)
        hist_g[j] = pg[tm - 8:]
        hist_u[j] = pu[tm - 8:]

    tile = pl.BlockSpec((tm, tc), lambda i, j: (i, j))
    wsp = lambda off: pl.BlockSpec((tc, D_MODEL), lambda i, j, off=off: (j + off, 0))
    cws = lambda off: pl.BlockSpec((8, tc), lambda i, j, off=off: (0, j + off))
    cbs = lambda off: pl.BlockSpec((1, tc), lambda i, j, off=off: (0, j + off))
    return _call(
        body, name=name, grid=(T // tm, nc),
        in_specs=[pl.BlockSpec((tm, D_MODEL), lambda i, j: (i, 0)), wsp(0), wsp(nc), cws(0), cws(nc), cbs(0), cbs(nc)],
        out_specs=[tile, tile, tile], out_shape=[jax.ShapeDtypeStruct((T, D_FF), BF16)] * 3,
        scratch_shapes=[pltpu.VMEM((nc, 8, tc), F32)] * 2, compiler_params=_params("arbitrary", "arbitrary"),
    )(h2, w_upT, w_upT, cw, cw, cb, cb)


def _ffn_act_bwd(pg, pu, dact, cw, cb, name, tm=512, tc=768):
    T, F = pg.shape
    nt = T // tm
    hb = tm // 8
    hb16 = tm // 16
    nc = F // tc

    def body(g_ref, gp_ref, gn_ref, u_ref, up_ref, un_ref, d_ref, dn_ref, cwg_ref, cwu_ref, cbg_ref, cbu_ref,
             dg_ref, du_ref, gcwg_ref, gcwu_ref, gcbg_ref, gcbu_ref):
        i = pl.program_id(1)
        first, last = i == 0, i == nt - 1

        @pl.when(first)
        def _():
            for ref in (gcwg_ref, gcwu_ref, gcbg_ref, gcbu_ref):
                ref[...] = jnp.zeros_like(ref)

        ext = lambda p, t, n: jnp.concatenate([jnp.where(first, 0.0, p[...].astype(F32)[8:]), t[...].astype(F32),
                                               jnp.where(last, 0.0, n[...].astype(F32)[:8])], axis=0)
        ge, ue = ext(gp_ref, g_ref, gn_ref), ext(up_ref, u_ref, un_ref)
        de = jnp.concatenate([jnp.zeros((8, tc), F32), d_ref[...], jnp.where(last, 0.0, dn_ref[...])], axis=0)
        gel, dgel = _gelu_and_grad(_ffn_conv(ge, cwg_ref, cbg_ref))
        d_gate = de * _ffn_conv(ue, cwu_ref, cbu_ref) * dgel
        d_up = de * gel
        n = tm + 16
        for dcv, xe, cw_ref, dx_ref, gcw_ref, gcb_ref in ((d_gate, ge, cwg_ref, dg_ref, gcwg_ref, gcbg_ref),
                                                            (d_up, ue, cwu_ref, du_ref, gcwu_ref, gcbu_ref)):
            dx = cw_ref[2:3, :] * dcv + cw_ref[1:2, :] * pltpu.roll(dcv, n - 1, 0) + cw_ref[0:1, :] * pltpu.roll(dcv, n - 2, 0)
            dx_ref[...] = dx[8:tm + 8].astype(BF16)
            dt = dcv[8:tm + 8]
            gcw_ref[2:3, :] += jnp.sum(dt * xe[8:tm + 8], axis=0, keepdims=True)
            gcw_ref[1:2, :] += jnp.sum(dt * pltpu.roll(xe, 1, 0)[8:tm + 8], axis=0, keepdims=True)
            gcw_ref[0:1, :] += jnp.sum(dt * pltpu.roll(xe, 2, 0)[8:tm + 8], axis=0, keepdims=True)
            gcb_ref[...] += jnp.sum(dt, axis=0, keepdims=True)

    tile = pl.BlockSpec((tm, tc), lambda j, i: (i, j))
    prev = pl.BlockSpec((16, tc), lambda j, i: (jnp.maximum(i * hb16 - 1, 0), j))
    nxt16 = pl.BlockSpec((16, tc), lambda j, i: (jnp.minimum((i + 1) * hb16, nt * hb16 - 1), j))
    nxt = pl.BlockSpec((8, tc), lambda j, i: (jnp.minimum((i + 1) * hb, nt * hb - 1), j))
    cws = lambda off: pl.BlockSpec((8, tc), lambda j, i, off=off: (0, j + off))
    cbs = lambda off: pl.BlockSpec((1, tc), lambda j, i, off=off: (0, j + off))
    return _call(
        body, name=name, grid=(nc, nt),
        in_specs=[tile, prev, nxt16, tile, prev, nxt16, tile, nxt, cws(0), cws(nc), cbs(0), cbs(nc)],
        out_specs=[tile, tile, cws(0), cws(0), cbs(0), cbs(0)],
        out_shape=[jax.ShapeDtypeStruct((T, F), BF16)] * 2 + [jax.ShapeDtypeStruct((8, F), F32)] * 2
        + [jax.ShapeDtypeStruct((1, F), F32)] * 2,
        compiler_params=_params("parallel", "arbitrary"),
    )(pg, pg, pg, pu, pu, pu, dact, dact, cw, cw, cb, cb)


def _adam_update(w, g, m, v):
    m2 = ADAM_B1 * m + (1.0 - ADAM_B1) * g
    v2 = ADAM_B2 * v + (1.0 - ADAM_B2) * (g * g)
    m_hat = m2 / (1.0 - ADAM_B1 ** ADAM_STEP)
    v_hat = v2 / (1.0 - ADAM_B2 ** ADAM_STEP)
    delta = -ADAM_LR * (m_hat / (jnp.sqrt(v_hat) + ADAM_EPS) + ADAM_WD * w)
    return delta, m2, v2


def _adam_sharded(p, r2, idx, w, m, v, name, transposed=False):
    r, n = p.shape[1:]
    nrecv = r2.shape[0]
    tr = (256 if r % 256 == 0 else r) if transposed else _row_tile(r)

    def body(c_ref, p_ref, r_ref, w_ref, m_ref, v_ref, g_ref, d_ref, m2_ref, v2_ref):
        g = p_ref[...].astype(F32)
        for k in range(nrecv):
            g = g + r_ref[k].astype(F32)
        if transposed:
            g = g.T
        g_ref[...] = g
        d_ref[...], m2_ref[...], v2_ref[...] = _adam_update(w_ref[...], g, m_ref[...], v_ref[...])

    blk = pl.BlockSpec((n, tr), lambda i, c_ref: (0, i)) if transposed else pl.BlockSpec((tr, n), lambda i, c_ref: (i, 0))
    spec = pltpu.PrefetchScalarGridSpec(
        num_scalar_prefetch=1, grid=(r // tr,),
        in_specs=[pl.BlockSpec((None, tr, n), lambda i, c_ref: (c_ref[0], i, 0)),
                  pl.BlockSpec((nrecv, tr, n), lambda i, c_ref: (0, i, 0)), blk, blk, blk],
        out_specs=[blk] * 4)
    return _call(body, name=name, grid_spec=spec, out_shape=[jax.ShapeDtypeStruct(w.shape, F32)] * 4,
                 compiler_params=_params("parallel"))(idx, p, r2, w, m, v)


def _sum_devices(allg, name):
    r, n = allg.shape[0] // N_DEV, allg.shape[1]

    def body(a_ref, o_ref):
        acc = a_ref[0:r, :]
        for k in range(1, N_DEV):
            acc = acc + a_ref[k * r:(k + 1) * r, :]
        o_ref[...] = acc

    return _call(body, name=name, out_shape=jax.ShapeDtypeStruct((r, n), F32))(allg)


def _adam_small(ws, gs, ms, vs, name):
    n = len(ws)

    def body(*refs):
        for i in range(n):
            d, m2, v2 = _adam_update(refs[i][...], refs[n + i][...], refs[2 * n + i][...], refs[3 * n + i][...])
            refs[4 * n + i][...] = d
            refs[5 * n + i][...] = m2
            refs[6 * n + i][...] = v2

    outs = _call(body, name=name, out_shape=[jax.ShapeDtypeStruct(w.shape, F32) for w in ws] * 3)(*ws, *gs, *ms, *vs)
    return outs[:n], outs[n:2 * n], outs[2 * n:]


_SMALL = (("g_mix", 1024), ("q_norm_g", 64), ("k_norm_g", 64), ("rec_conv_b", 512), ("w_rg", 32768), ("b_rg", 512),
          ("w_ig", 32768), ("b_ig", 512), ("lru_lambda", 512), ("g_attn_out", 512), ("g_rec_out", 512),
          ("g_ffn", 1024), ("ffn_conv_b", 6144))
_SMALL_SHAPES = {"g_mix": (1, 1024), "q_norm_g": (1, 64), "k_norm_g": (1, 64), "rec_conv_b": (1, 512),
                 "w_rg": (1, 8, 64, 64), "b_rg": (1, 8, 64), "w_ig": (1, 8, 64, 64), "b_ig": (1, 8, 64),
                 "lru_lambda": (1, 512), "g_attn_out": (1, 512), "g_rec_out": (1, 512), "g_ffn": (1, 1024),
                 "ffn_conv_b": (1, 6144)}


def _block_diag(w):
    eye = jnp.eye(8, dtype=w.dtype)
    return (w[:, :, None, :] * eye[:, None, :, None]).reshape(512, 512)


def kernel(x, positions, g_mix, w_in, q_norm_g, k_norm_g, rec_conv_w, rec_conv_b, w_rg, b_rg, w_ig, b_ig, lru_lambda, g_attn_out, g_rec_out, w_out, g_ffn, w_up, ffn_conv_w, ffn_conv_b, w_down, loss_target, m_g_mix, m_w_in, m_q_norm_g, m_k_norm_g, m_rec_conv_w, m_rec_conv_b, m_w_rg, m_b_rg, m_w_ig, m_b_ig, m_lru_lambda, m_g_attn_out, m_g_rec_out, m_w_out, m_g_ffn, m_w_up, m_ffn_conv_w, m_ffn_conv_b, m_w_down, v_g_mix, v_w_in, v_q_norm_g, v_k_norm_g, v_rec_conv_w, v_rec_conv_b, v_w_rg, v_b_rg, v_w_ig, v_b_ig, v_lru_lambda, v_g_attn_out, v_g_rec_out, v_w_out, v_g_ffn, v_w_up, v_ffn_conv_w, v_ffn_conv_b, v_w_down):
    T = x.shape[1]
    ix, iy, ic = lax.axis_index("x"), lax.axis_index("y"), lax.axis_index("c")
    dev = 4 * ix + 2 * iy + ic
    xs = x.reshape(T, D_MODEL)
    tgt = loss_target.reshape(T, D_MODEL)
    pos = positions.reshape(T, 1)

    shards = {"w_in": (w_in[0], m_w_in[0], v_w_in[0]), "w_out": (w_out[0], m_w_out[0], v_w_out[0]),
              "w_up": (w_up[0], m_w_up[0], v_w_up[0]), "w_down": (w_down[0], m_w_down[0], v_w_down[0])}
    taps = jnp.concatenate([rec_conv_w.reshape(-1), ffn_conv_w.reshape(-1), jnp.zeros((4096 - 2560,), F32)]).reshape(8, 512)
    W_inT, taps_all = _all_gather([w_in[0].T.astype(BF16), taps], "ag_w_in")
    late = [w_out[0].astype(BF16), w_up[0].T.astype(BF16), w_down[0].astype(BF16)]
    ag_send, ag_recv, late_thru, land_thru, ag_token = _exchange_start(
        late, [_landing((N_DEV * s.shape[0], 1024), BF16, s, dev * s.shape[0]) for s in late], "gather", taps_all,
        "ag_late_start")
    taps_all = taps_all.reshape(N_DEV, 4096)
    rcw = taps_all[:, :256].reshape(8, 4, 64).transpose(1, 0, 2).reshape(4, REC_W)
    fcw = taps_all[:, 256:2560].reshape(8, 3, 768).transpose(1, 0, 2).reshape(3, 2 * D_FF)
    rcw8 = jnp.pad(rcw, ((0, 4), (0, 0)))
    fcw8 = jnp.pad(fcw, ((0, 5), (0, 0)))
    fcb = ffn_conv_b.reshape(1, 2 * D_FF)

    half = HEAD_DIM // 2
    inv_freq = ROPE_THETA ** (-jnp.arange(half, dtype=F32) / half)
    invf = jnp.tile(inv_freq, 2 * N_HEADS).reshape(1, ATTN_W)
    bd = jnp.asarray(np.kron(np.eye(2), np.full((HEAD_DIM, HEAD_DIM), 1.0 / HEAD_DIM)), BF16)
    qg = jnp.tile(q_norm_g.reshape(HEAD_DIM), N_HEADS).reshape(1, ATTN_W)
    kg = jnp.tile(k_norm_g.reshape(HEAD_DIM), N_HEADS).reshape(1, ATTN_W)
    wrg_bd = _block_diag(w_rg[0]).astype(BF16)
    wig_bd = _block_diag(w_ig[0]).astype(BF16)
    brg, big = b_rg.reshape(1, REC_W), b_ig.reshape(1, REC_W)

    h1 = _rmsnorm(xs, g_mix + ag_token[0, 0], "norm_mix")
    proj = _mm(h1, W_inT, "nt", F32, "in_proj", tn=1280)
    qf, kf = _qk_prep(proj, pos, invf, qg, kg, bd, "qk_prep")
    attn, lse = _attn_fwd(qf, kf, proj, "attn_fwd")
    mix = _attn_norm(attn, g_attn_out, "attn_norm")
    xc, hstate, mix = _rec_fwd(proj, mix, rcw8, rec_conv_b, wrg_bd, wig_bd, brg, big, lru_lambda, g_rec_out, "rec_fwd")
    _, (W_out, W_upT, W_down) = _exchange_wait(ag_send, ag_recv, late_thru, land_thru, "gather", hstate, "ag_late_wait")
    x2 = _mm(mix, W_out, "nn", F32, "out_proj", add=xs)

    h2 = _rmsnorm(x2, g_ffn, "norm_ffn")
    act, pg, pu = _up_proj_act(h2, W_upT, fcw8, fcb, "up_proj_act")
    dy, dyb, lparts = _mm(act, W_down, "nn", F32, "down_proj_loss", add=x2, loss_target=tgt)
    loss_mine = 0.5 / D_MODEL * jnp.sum(lparts)

    dact = _mm(dyb, W_down, "nt", F32, "d_act", tn=1536)
    g_down = _mm(act, dyb, "tn", BF16, "g_w_down", tk=2048)
    dpg, dpu, g_fcwg, g_fcwu, g_fcbg, g_fcbu = _ffn_act_bwd(pg, pu, dact, fcw8, fcb, "ffn_act_bwd")
    g_upT = _mm(dpg, h2, "tn", BF16, "g_w_up_gate", tk=2048, o_rows=2 * D_FF)
    g_upT = _mm(dpu, h2, "tn", BF16, "g_w_up_up", tk=2048, into=g_upT, o_moff=D_FF // 1024)
    dh2 = _mm(dpg, W_upT, "nn", F32, "d_h2_gate", k=D_FF)
    dh2 = _mm(dpu, W_upT, "nn", F32, "d_h2_up", k=D_FF, b_koff=D_FF // 1024, add=dh2)
    ffn_g = [g_upT.reshape(N_DEV, 2 * D_FF // N_DEV, 1024), g_down.reshape(N_DEV, D_FF // N_DEV, 1024)]
    rs_send, rs_recv, ffn_g, ffn_land, rs_token = _exchange_start(
        ffn_g, [_landing((N_PEERS,) + g.shape[1:], BF16) for g in ffn_g], "scatter", dh2, "rs_ffn_start")
    dx2, dx2b, g_gffn = _rmsnorm_bwd(x2, dh2, dy, g_ffn + rs_token[0, 0], "norm_ffn_bwd")

    dmix = _mm(dx2b, W_out, "nt", F32, "d_mix")
    g_out = _mm(mix, dx2b, "tn", BF16, "g_w_out", tk=2048).reshape(N_DEV, D_MODEL // N_DEV, 1024)
    out_send, out_recv, (g_out,), out_land, out_token = _exchange_start(
        [g_out], [_landing((N_PEERS,) + g_out.shape[1:], BF16)], "scatter", dmix, "rs_out_start")
    do, delta, g_gattn = _attn_norm_bwd(dmix, attn, g_attn_out + out_token[0, 0], bd, "attn_norm_bwd")
    dqh, dkh, dv = _attn_bwd(qf, kf, proj, do, lse, delta, "attn_bwd")
    dqkv, g_qg, g_kg = _qk_prep_bwd(proj, dqh, dkh, dv, pos, invf, qg, kg, bd, "qk_prep_bwd")
    (drec, g_rcw, g_rcb, g_wrg, g_wig, g_brg, g_big, g_lam, g_grec) = _rec_bwd(
        dmix, proj, xc, hstate, rcw8, rec_conv_b, wrg_bd, wig_bd, brg, big, lru_lambda, g_rec_out, "rec_bwd")
    g_inT = _mm(dqkv, h1, "tn", BF16, "g_w_in_qkv", tm=512, o_rows=IN_W)
    g_inT = _mm(drec, h1, "tn", BF16, "g_w_in_rec", tm=512, into=g_inT, o_moff=3 * ATTN_W // 512)
    g_inT = g_inT.reshape(N_DEV, IN_W // N_DEV, 1024)
    in_send, in_recv, (g_inT,), in_land, in_token = _exchange_start(
        [g_inT], [_landing((N_PEERS,) + g_inT.shape[1:], BF16)], "scatter", drec, "rs_in_start")
    dh1 = _mm(dqkv, W_inT, "nn", F32, "d_h1_qkv", tk=512, k=3 * ATTN_W)
    dh1 = _mm(drec, W_inT, "nn", F32, "d_h1_rec", tk=512, k=2 * REC_W, b_koff=3 * ATTN_W // 512, add=dh1)
    grad_x, _, g_gmix = _rmsnorm_bwd(xs, dh1, dx2, g_mix + in_token[0, 0], "norm_mix_bwd")

    blocks = lambda g: jnp.stack([g[64 * n:64 * n + 64, 64 * n:64 * n + 64] for n in range(8)])
    small_g = {
        "g_mix": g_gmix, "q_norm_g": g_qg.reshape(N_HEADS, HEAD_DIM).sum(0), "k_norm_g": g_kg.reshape(N_HEADS, HEAD_DIM).sum(0),
        "rec_conv_b": g_rcb, "w_rg": blocks(g_wrg), "b_rg": g_brg, "w_ig": blocks(g_wig), "b_ig": g_big,
        "lru_lambda": g_lam, "g_attn_out": g_gattn, "g_rec_out": g_grec, "g_ffn": g_gffn,
        "ffn_conv_b": jnp.concatenate([g_fcbg, g_fcbu], axis=1)}
    g_fcw = jnp.concatenate([g_fcwg[:3], g_fcwu[:3]], axis=1)
    flat = jnp.concatenate([small_g[k].reshape(-1) for k, _ in _SMALL]
                           + [g_rcw[:4].reshape(-1), g_fcw.reshape(-1), loss_mine.reshape(1)])
    flat = jnp.pad(flat, (0, SMALL_ROWS * 1024 - flat.shape[0])).reshape(SMALL_ROWS, 1024)
    tot = _sum_devices(_all_gather([flat], "ag_small_grads")[0], "sum_small_grads").reshape(-1)
    g_small, o = {}, 0
    for k, n in _SMALL:
        g_small[k] = tot[o:o + n].reshape(_SMALL_SHAPES[k])
        o += n
    g_small["rec_conv_w"] = lax.dynamic_slice(tot[o:o + 2048].reshape(1, 4, REC_W), (0, 0, 64 * dev), (1, 4, 64))
    g_small["ffn_conv_w"] = lax.dynamic_slice(tot[o + 2048:o + 2048 + 18432].reshape(1, 3, 2 * D_FF), (0, 0, 768 * dev), (1, 3, 768))
    loss = tot[o + 2048 + 18432]

    devi = jnp.reshape(dev, (1,)).astype(jnp.int32)
    ffn_g, ffn_land = _exchange_wait(rs_send, rs_recv, ffn_g, ffn_land, "scatter", tot, "rs_ffn_wait")
    (g_out,), out_land = _exchange_wait(out_send, out_recv, [g_out], out_land, "scatter", tot, "rs_out_wait")
    (g_inT,), in_land = _exchange_wait(in_send, in_recv, [g_inT], in_land, "scatter", tot, "rs_in_wait")
    big_out = {"grad": {}, "delta": {}, "new_m": {}, "new_v": {}}
    for nm, p, r in (("w_up", ffn_g[0], ffn_land[0]), ("w_down", ffn_g[1], ffn_land[1]), ("w_out", g_out, out_land[0]),
                     ("w_in", g_inT, in_land[0])):
        w_, m_, v_ = shards[nm]
        res = _adam_sharded(p, r, devi, w_, m_, v_, "adam_" + nm, transposed=nm in ("w_in", "w_up"))
        for kind, a in zip(("grad", "delta", "new_m", "new_v"), res):
            big_out[kind][nm] = a[None]
    given = dict(rec_conv_w=rec_conv_w, ffn_conv_w=ffn_conv_w,g_mix=g_mix, q_norm_g=q_norm_g, k_norm_g=k_norm_g, rec_conv_b=rec_conv_b, w_rg=w_rg, b_rg=b_rg, w_ig=w_ig,
                 b_ig=b_ig, lru_lambda=lru_lambda, g_attn_out=g_attn_out, g_rec_out=g_rec_out, g_ffn=g_ffn, ffn_conv_b=ffn_conv_b)
    given_m = dict(rec_conv_w=m_rec_conv_w, ffn_conv_w=m_ffn_conv_w, g_mix=m_g_mix, q_norm_g=m_q_norm_g, k_norm_g=m_k_norm_g, rec_conv_b=m_rec_conv_b, w_rg=m_w_rg, b_rg=m_b_rg,
                   w_ig=m_w_ig, b_ig=m_b_ig, lru_lambda=m_lru_lambda, g_attn_out=m_g_attn_out, g_rec_out=m_g_rec_out,
                   g_ffn=m_g_ffn, ffn_conv_b=m_ffn_conv_b)
    given_v = dict(rec_conv_w=v_rec_conv_w, ffn_conv_w=v_ffn_conv_w, g_mix=v_g_mix, q_norm_g=v_q_norm_g, k_norm_g=v_k_norm_g, rec_conv_b=v_rec_conv_b, w_rg=v_w_rg, b_rg=v_b_rg,
                   w_ig=v_w_ig, b_ig=v_b_ig, lru_lambda=v_lru_lambda, g_attn_out=v_g_attn_out, g_rec_out=v_g_rec_out,
                   g_ffn=v_g_ffn, ffn_conv_b=v_ffn_conv_b)
    small = sorted(given)
    ds, m2s, v2s = _adam_small([given[k] for k in small], [g_small[k] for k in small], [given_m[k] for k in small],
                               [given_v[k] for k in small], "adam_small")
    small_out = {"grad": g_small, "delta": dict(zip(small, ds)), "new_m": dict(zip(small, m2s)), "new_v": dict(zip(small, v2s))}

    order = ("g_mix", "w_in", "q_norm_g", "k_norm_g", "rec_conv_w", "rec_conv_b", "w_rg", "b_rg", "w_ig", "b_ig",
             "lru_lambda", "g_attn_out", "g_rec_out", "w_out", "g_ffn", "w_up", "ffn_conv_w", "ffn_conv_b", "w_down")
    outs = [loss, grad_x.reshape(1, T, D_MODEL)]
    for kind in ("grad", "delta", "new_m", "new_v"):
        for name in order:
            outs.append(big_out[kind][name] if name in big_out[kind] else small_out[kind][name])
    return tuple(outs)
```

```python
import math

import numpy as np
import jax
import jax.numpy as jnp
from jax import lax
from jax.experimental import pallas as pl
from jax.experimental.pallas import tpu as pltpu

F32 = jnp.float32
BF16 = jnp.bfloat16

D_MODEL = 1024
HEAD_DIM = 64
ATTN_W = 512
REC_W = 512
N_HEADS = 8
D_FF = 3072
IN_W = 2560
REC_CONV = 4
FFN_CONV = 3
LRU_C = 8.0
ROPE_THETA = 10000.0
EPS = 1e-6
NEG_INF = -1e30
QBLK = 128
DILATIONS = (1, 4, 16)
N_DEV = 8
SMALL_ROWS = 96
ADAM_LR, ADAM_B1, ADAM_B2, ADAM_EPS, ADAM_WD, ADAM_STEP = 0.001, 0.9, 0.999, 1e-08, 0.01, 10
MESH = pl.DeviceIdType.MESH
ANY = pl.BlockSpec(memory_space=pl.ANY)


def _call(body, *, name, **kw):
    return pl.pallas_call(body, name=name, **kw)


def _params(*sem):
    return pltpu.CompilerParams(dimension_semantics=sem, vmem_limit_bytes=56 * 1024 * 1024)


def _gelu(x):
    c = math.sqrt(2.0 / math.pi)
    return 0.5 * x * (1.0 + jnp.tanh(c * (x + 0.044715 * (x * x * x))))


def _gelu_and_grad(x):
    c = math.sqrt(2.0 / math.pi)
    t = jnp.tanh(c * (x + 0.044715 * (x * x * x)))
    g = 0.5 * x * (1.0 + t)
    dg = 0.5 * (1.0 + t) + 0.5 * x * (1.0 - t * t) * (c * (1.0 + 3.0 * 0.044715 * (x * x)))
    return g, dg


def _sigmoid(x):
    return 1.0 / (1.0 + jnp.exp(-x))


def _softplus_neg(lam):
    y = jnp.exp(-jnp.abs(lam))
    u = 1.0 + y
    log1p = jnp.where(u == 1.0, y, jnp.log(u) * y / jnp.where(u == 1.0, 1.0, u - 1.0))
    return jnp.maximum(-lam, 0.0) + log1p


_NN = (((1,), (0,)), ((), ()))
_NT = (((1,), (1,)), ((), ()))
_TN = (((0,), (0,)), ((), ()))


def _dot(a, b, dims=_NN):
    return lax.dot_general(a, b, dims, preferred_element_type=F32)


def _group_mean(v, bd):
    hi = v.astype(BF16)
    lo = (v - hi.astype(F32)).astype(BF16)
    w = bd.shape[0]
    return jnp.concatenate([_dot(hi[:, c:c + w], bd) + _dot(lo[:, c:c + w], bd) for c in range(0, v.shape[1], w)], axis=1)


def _rope_tables(pos_ref, invf_ref):
    ang = pos_ref[...].astype(F32) * invf_ref[:, :2 * HEAD_DIM]
    reps = invf_ref.shape[1] // (2 * HEAD_DIM)
    return jnp.tile(jnp.cos(ang), (1, reps)), jnp.tile(jnp.sin(ang), (1, reps))


def _shift_down(x, halo, s):
    rolled = pltpu.roll(x, s, 0)
    hr = pltpu.roll(halo, s, 0)
    row = lax.broadcasted_iota(jnp.int32, hr.shape, 0)
    first = jnp.where(row < s, hr, rolled[:8])
    return jnp.concatenate([first, rolled[8:]], axis=0)


def _shift_up(x, halo, s):
    n = x.shape[0]
    rolled = pltpu.roll(x, n - s, 0)
    hr = pltpu.roll(halo, 8 - s, 0)
    row = lax.broadcasted_iota(jnp.int32, hr.shape, 0)
    last = jnp.where(row >= 8 - s, hr, rolled[n - 8:])
    return jnp.concatenate([rolled[:n - 8], last], axis=0)


def _scan_fwd(a, u):
    n = a.shape[0]
    row = lax.broadcasted_iota(jnp.int32, a.shape, 0)
    s = 1
    while s < n:
        a_s = jnp.where(row < s, 1.0, pltpu.roll(a, s, 0))
        u_s = jnp.where(row < s, 0.0, pltpu.roll(u, s, 0))
        u = u + a * u_s
        a = a * a_s
        s *= 2
    return a, u


def _scan_bwd(b, v):
    n = b.shape[0]
    row = lax.broadcasted_iota(jnp.int32, b.shape, 0)
    s = 1
    while s < n:
        b_s = jnp.where(row >= n - s, 1.0, pltpu.roll(b, n - s, 0))
        v_s = jnp.where(row >= n - s, 0.0, pltpu.roll(v, n - s, 0))
        v = v + b * v_s
        b = b * b_s
        s *= 2
    return b, v


def _rot_half(y):
    n = y.shape[1]
    lane = lax.broadcasted_iota(jnp.int32, y.shape, 1) & (HEAD_DIM - 1)
    return jnp.where(lane < HEAD_DIM // 2, -pltpu.roll(y, n - HEAD_DIM // 2, 1), pltpu.roll(y, HEAD_DIM // 2, 1))


def _row_tile(r, cap=256):
    return max(t for t in range(16, cap + 1, 16) if r % t == 0)


def _all_gather(shards, name):
    na = len(shards)
    ms = [s.shape[0] for s in shards]

    def body(*refs):
        x_refs, out_refs = refs[:na], refs[na:2 * na]
        send_sems, recv_sems, local_sems = refs[2 * na:]
        x, y, c = lax.axis_index("x"), lax.axis_index("y"), lax.axis_index("c")
        me, sibling = (x, y, c), (x, y, 1 - c)
        chips = [(1 - x, y), (x, 1 - y), (1 - x, 1 - y)]

        def rows(a, px, py, pc):
            return out_refs[a].at[pl.ds((4 * px + 2 * py + pc) * ms[a], ms[a]), :]

        def copy(a, k, block, to, src=None):
            return pltpu.make_async_remote_copy(
                src_ref=rows(a, *block) if src is None else src, dst_ref=rows(a, *block),
                send_sem=send_sems.at[7 * a + k], recv_sem=recv_sems.at[7 * a + k], device_id=to, device_id_type=MESH)

        mine = [pltpu.make_async_copy(x_refs[a], rows(a, *me), local_sems.at[a]) for a in range(na)]
        first = []
        for a in range(na):
            mine[a].start()
            first.append(copy(a, 0, me, sibling, src=x_refs[a]))
            first += [copy(a, 1 + j, me, (*chip, c), src=x_refs[a]) for j, chip in enumerate(chips)]
        for cp in first:
            cp.start()
        passed = []
        for a in range(na):
            for j, chip in enumerate(chips):
                copy(a, 1 + j, (*chip, c), me).wait_recv()
                fw = copy(a, 4 + j, (*chip, c), sibling)
                fw.start()
                passed.append(fw)
        for a in range(na):
            copy(a, 0, sibling, me).wait_recv()
            for j, chip in enumerate(chips):
                copy(a, 4 + j, (*chip, 1 - c), me).wait_recv()
        for cp in first + passed:
            cp.wait_send()
        for cp in mine:
            cp.wait()

    return _call(
        body, name=name, out_shape=[jax.ShapeDtypeStruct((N_DEV * s.shape[0], s.shape[1]), s.dtype) for s in shards],
        in_specs=[ANY] * na, out_specs=[ANY] * na,
        scratch_shapes=[pltpu.SemaphoreType.DMA((7 * na,)), pltpu.SemaphoreType.DMA((7 * na,)),
                        pltpu.SemaphoreType.DMA((na,))],
    )(*shards)


HBM = pl.BlockSpec(memory_space=pltpu.HBM)
SEM = pl.BlockSpec(memory_space=pltpu.SEMAPHORE)
EFFECT = pltpu.SideEffectType.DATAFLOW_SIDE_EFFECTING
N_PEERS = N_DEV - 1


def _peer(k):
    x, y, c = lax.axis_index("x"), lax.axis_index("y"), lax.axis_index("c")
    b = k + 1
    flip = lambda v, bit: 1 - v if bit else v
    return flip(x, b & 4), flip(y, b & 2), flip(c, b & 1)


def _in_hbm(a):
    return pltpu.with_memory_space_constraint(a, pltpu.HBM)


def _split_copy_descr(na, kind, src_refs, land_refs, send_sems, recv_sems):
    x, y, c = lax.axis_index("x"), lax.axis_index("y"), lax.axis_index("c")
    me = 4 * x + 2 * y + c
    copies = []
    for a in range(na):
        for k in range(N_PEERS):
            px, py, pc = _peer(k)
            if kind == "gather":
                m = src_refs[a].shape[0]
                src, dst = src_refs[a], land_refs[a].at[pl.ds(me * m, m), :]
            else:
                src, dst = src_refs[a].at[4 * px + 2 * py + pc], land_refs[a].at[k]
            copies.append(pltpu.make_async_remote_copy(
                src_ref=src, dst_ref=dst, send_sem=send_sems.at[N_PEERS * a + k], recv_sem=recv_sems.at[N_PEERS * a + k],
                device_id=(px, py, pc), device_id_type=MESH))
    return copies


def _landing(shape, dtype, own=None, at=None):
    buf = lax.empty(shape, dtype)
    return buf if own is None else lax.dynamic_update_slice(buf, own, (at, 0))


def _exchange_start(srcs, lands, kind, after, name):
    na = len(srcs)
    land_shapes = [l.shape for l in lands]

    def body(*refs):
        src_refs, land_refs = refs[:na], refs[na:2 * na]
        send_sems, recv_sems = refs[2 * na + 1], refs[2 * na + 2]
        token = refs[-1]
        for cp in _split_copy_descr(na, kind, src_refs, land_refs, send_sems, recv_sems):
            cp.start()
        token[...] = jnp.zeros_like(token)

    lands = [_in_hbm(l) for l in lands]
    sem = pltpu.SemaphoreType.DMA((N_PEERS * na,))
    outs = _call(
        body, name=name,
        out_shape=[sem, sem] + [pltpu.HBM(s.shape, s.dtype) for s in srcs] + [pltpu.HBM(s, srcs[0].dtype) for s in land_shapes]
        + [jax.ShapeDtypeStruct((8, 128), F32)],
        in_specs=[HBM] * (2 * na) + [ANY], out_specs=[SEM, SEM] + [HBM] * (2 * na) + [pl.BlockSpec(memory_space=pltpu.VMEM)],
        input_output_aliases={i: 2 + i for i in range(2 * na)},
        compiler_params=pltpu.CompilerParams(has_side_effects=EFFECT),
    )(*[_in_hbm(s) for s in srcs], *lands, after)
    return outs[0], outs[1], outs[2:2 + na], outs[2 + na:2 + 2 * na], outs[-1]


def _exchange_wait(send_sems, recv_sems, srcs, lands, kind, after, name):
    na = len(srcs)

    def body(*refs):
        src_refs, land_refs = refs[:na], refs[na:2 * na]
        s_sems, r_sems = refs[2 * na], refs[2 * na + 1]
        for cp in _split_copy_descr(na, kind, src_refs, land_refs, s_sems, r_sems):
            cp.wait_send()
            cp.wait_recv()

    outs = _call(
        body, name=name, out_shape=[pltpu.HBM(s.shape, s.dtype) for s in srcs] + [pltpu.HBM(l.shape, l.dtype) for l in lands],
        in_specs=[HBM] * (2 * na) + [SEM, SEM, ANY], out_specs=[HBM] * (2 * na),
        input_output_aliases={i: i for i in range(2 * na)},
        compiler_params=pltpu.CompilerParams(has_side_effects=EFFECT),
    )(*srcs, *lands, send_sems, recv_sems, after)
    return outs[:na], outs[na:]


def _mm(a, b, mode, out_dtype, name, add=None, tm=1024, tn=1024, tk=1024, b_noff=0, b_koff=0,
        n=None, k=None, into=None, o_rows=None, o_moff=0, loss_target=None):
    if mode == "tn":
        K, M = a.shape
    else:
        M, K = a.shape
    N = n if n is not None else (b.shape[0] if mode == "nt" else b.shape[1])
    if k is not None:
        assert k == K
    tm, tn, tk = min(tm, M), min(tn, N), min(tk, K)
    assert M % tm == 0 and N % tn == 0 and K % tk == 0, (name, M, N, K)
    nk = K // tk
    if mode == "nn":
        a_spec = pl.BlockSpec((tm, tk), lambda i, j, kk: (i, kk))
        b_spec, dims = pl.BlockSpec((tk, tn), lambda i, j, kk: (kk + b_koff, j + b_noff)), _NN
    elif mode == "nt":
        a_spec = pl.BlockSpec((tm, tk), lambda i, j, kk: (i, kk))
        b_spec, dims = pl.BlockSpec((tn, tk), lambda i, j, kk: (j + b_noff, kk + b_koff)), _NT
    else:
        a_spec = pl.BlockSpec((tk, tm), lambda i, j, kk: (kk, i))
        b_spec, dims = pl.BlockSpec((tk, tn), lambda i, j, kk: (kk + b_koff, j + b_noff)), _TN
    o_spec = pl.BlockSpec((tm, tn), lambda i, j, kk: (i + o_moff, j))
    has_add, has_into, has_loss = add is not None, into is not None, loss_target is not None
    assert not has_loss or (has_add and tn == N and not has_into)
    n_in = 2 + has_add + has_loss + has_into

    def body(*refs):
        a_ref, b_ref = refs[0], refs[1]
        add_ref = refs[2] if has_add else None
        outs = refs[n_in:]

        def finish(r):
            if has_add:
                r = r + add_ref[...]
            if has_loss:
                e = r - refs[3][...]
                dy = e * (1.0 / N)
                outs[0][...] = dy
                outs[1][...] = dy.astype(BF16)
                outs[2][...] = jnp.sum(e * e, axis=0, keepdims=True)[None]
            else:
                outs[0][...] = r.astype(out_dtype)

        if nk == 1:
            finish(_dot(a_ref[...], b_ref[...], dims))
        else:
            acc = refs[-1]
            kk = pl.program_id(2)

            @pl.when(kk == 0)
            def _():
                acc[...] = _dot(a_ref[...], b_ref[...], dims)

            @pl.when((kk > 0) & (kk < nk - 1))
            def _():
                acc[...] += _dot(a_ref[...], b_ref[...], dims)

            @pl.when(kk == nk - 1)
            def _():
                finish(acc[...] + _dot(a_ref[...], b_ref[...], dims))

    tile = pl.BlockSpec((tm, tn), lambda i, j, kk: (i, j))
    ins = [a, b] + ([add] if has_add else []) + ([loss_target] if has_loss else []) + ([into] if has_into else [])
    specs = [a_spec, b_spec] + [tile] * (has_add + has_loss) + ([ANY] if has_into else [])
    rows = into.shape[0] if has_into else (o_rows if o_rows is not None else M)
    if has_loss:
        out_specs = [tile, tile, pl.BlockSpec((1, 1, N), lambda i, j, kk: (i, 0, 0))]
        out_shape = [jax.ShapeDtypeStruct((M, N), F32), jax.ShapeDtypeStruct((M, N), BF16), jax.ShapeDtypeStruct((M // tm, 1, N), F32)]
    else:
        out_specs, out_shape = o_spec, jax.ShapeDtypeStruct((rows, N), out_dtype)
    return _call(
        body, name=name, grid=(M // tm, N // tn, nk), in_specs=specs, out_specs=out_specs, out_shape=out_shape,
        scratch_shapes=[pltpu.VMEM((tm, tn), F32)] if nk > 1 else [],
        input_output_aliases={len(ins) - 1: 0} if has_into else {},
        compiler_params=_params("parallel", "parallel", "arbitrary"),
    )(*ins)


def _rmsnorm(x, g, name, tm=512):
    T, D = x.shape

    def body(x_ref, g_ref, o_ref):
        xv = x_ref[...]
        r = lax.rsqrt(jnp.mean(xv * xv, axis=-1, keepdims=True) + EPS)
        o_ref[...] = (xv * r * g_ref[...]).astype(BF16)

    return _call(
        body, name=name, grid=(T // tm,),
        in_specs=[pl.BlockSpec((tm, D), lambda i: (i, 0)), pl.BlockSpec((1, D), lambda i: (0, 0))],
        out_specs=pl.BlockSpec((tm, D), lambda i: (i, 0)), out_shape=jax.ShapeDtypeStruct((T, D), BF16),
        compiler_params=_params("parallel"),
    )(x, g)


def _rmsnorm_bwd(x, dh, resid, g, name, tm=512):
    T, D = x.shape

    def body(x_ref, dh_ref, res_ref, g_ref, dx_ref, dxb_ref, dg_ref):
        @pl.when(pl.program_id(0) == 0)
        def _():
            dg_ref[...] = jnp.zeros_like(dg_ref)

        xv, dhv = x_ref[...], dh_ref[...]
        r = lax.rsqrt(jnp.mean(xv * xv, axis=-1, keepdims=True) + EPS)
        gd = dhv * g_ref[...]
        m = jnp.mean(gd * xv, axis=-1, keepdims=True)
        dx = res_ref[...] + r * gd - xv * (r * r * r) * m
        dx_ref[...] = dx
        dxb_ref[...] = dx.astype(BF16)
        dg_ref[...] += jnp.sum(dhv * xv * r, axis=0, keepdims=True)

    row = pl.BlockSpec((tm, D), lambda i: (i, 0))
    vec = pl.BlockSpec((1, D), lambda i: (0, 0))
    return _call(
        body, name=name, grid=(T // tm,), in_specs=[row, row, row, vec], out_specs=[row, row, vec],
        out_shape=[jax.ShapeDtypeStruct((T, D), F32), jax.ShapeDtypeStruct((T, D), BF16), jax.ShapeDtypeStruct((1, D), F32)],
        compiler_params=_params("arbitrary"),
    )(x, dh, resid, g)


def _qk_prep(proj, pos, invf, qg, kg, bd, name, tm=512):
    T = proj.shape[0]

    def body(q_ref, k_ref, pos_ref, invf_ref, qg_ref, kg_ref, bd_ref, qo_ref, ko_ref):
        cos, sin = _rope_tables(pos_ref, invf_ref)

        def prep(xv, gv, scale):
            r = lax.rsqrt(_group_mean(xv * xv, bd_ref[...]) + EPS)
            yv = xv * r * gv
            return ((yv * cos + _rot_half(yv) * sin) * scale).astype(BF16).astype(F32)

        qo_ref[...] = prep(q_ref[...], qg_ref[...], HEAD_DIM ** -0.5)
        ko_ref[...] = prep(k_ref[...], kg_ref[...], 1.0)

    col = lambda j: pl.BlockSpec((tm, ATTN_W), lambda i, j=j: (i, j))
    vec = pl.BlockSpec((1, ATTN_W), lambda i: (0, 0))
    out = pl.BlockSpec((tm, ATTN_W), lambda i: (i, 0))
    return _call(
        body, name=name, grid=(T // tm,),
        in_specs=[col(0), col(1), pl.BlockSpec((tm, 1), lambda i: (i, 0)), vec, vec, vec,
                  pl.BlockSpec((2 * HEAD_DIM, 2 * HEAD_DIM), lambda i: (0, 0))],
        out_specs=[out, out], out_shape=[jax.ShapeDtypeStruct((T, ATTN_W), F32)] * 2,
        compiler_params=_params("parallel"),
    )(proj, proj, pos, invf, qg, kg, bd)


def _qk_prep_bwd(proj, dqh, dkh, dv, pos, invf, qg, kg, bd, name, tm=512):
    T = proj.shape[0]

    def body(q_ref, k_ref, dq_ref, dk_ref, dv_ref, pos_ref, invf_ref, qg_ref, kg_ref, bd_ref, o_ref, gq_ref, gk_ref):
        @pl.when(pl.program_id(0) == 0)
        def _():
            gq_ref[...] = jnp.zeros_like(gq_ref)
            gk_ref[...] = jnp.zeros_like(gk_ref)

        cos, sin = _rope_tables(pos_ref, invf_ref)

        def back(xv, gv, dz, scale):
            dz = dz * scale
            dy = dz * cos - _rot_half(dz * sin)
            r = lax.rsqrt(_group_mean(xv * xv, bd_ref[...]) + EPS)
            gd = dy * gv
            m = _group_mean(gd * xv, bd_ref[...])
            dx = r * gd - xv * (r * r * r) * m
            return dx, jnp.sum(dy * xv * r, axis=0, keepdims=True)

        dxq, gs = back(q_ref[...], qg_ref[...], dq_ref[...], HEAD_DIM ** -0.5)
        gq_ref[...] += gs
        dxk, gs = back(k_ref[...], kg_ref[...], dk_ref[...], 1.0)
        gk_ref[...] += gs
        o_ref[...] = jnp.concatenate([dxq.astype(BF16), dxk.astype(BF16), dv_ref[...].astype(BF16)], axis=1)

    col = lambda j: pl.BlockSpec((tm, ATTN_W), lambda i, j=j: (i, j))
    row = pl.BlockSpec((tm, ATTN_W), lambda i: (i, 0))
    vec = pl.BlockSpec((1, ATTN_W), lambda i: (0, 0))
    return _call(
        body, name=name, grid=(T // tm,),
        in_specs=[col(0), col(1), row, row, row, pl.BlockSpec((tm, 1), lambda i: (i, 0)), vec, vec, vec,
                  pl.BlockSpec((2 * HEAD_DIM, 2 * HEAD_DIM), lambda i: (0, 0))],
        out_specs=[pl.BlockSpec((tm, 3 * ATTN_W), lambda i: (i, 0)), vec, vec],
        out_shape=[jax.ShapeDtypeStruct((T, 3 * ATTN_W), BF16)] + [jax.ShapeDtypeStruct((1, ATTN_W), F32)] * 2,
        compiler_params=_params("arbitrary"),
    )(proj, proj, dqh, dkh, dv, pos, invf, qg, kg, bd)


def _ld(ref, start, size, dil):
    return ref[pl.ds(start, size), :] if dil == 1 else ref[pl.ds(start, size, stride=dil), :]


def _st(ref, start, size, dil, val):
    if dil == 1:
        ref[pl.ds(start, size), :] = val
    else:
        ref[pl.ds(start, size, stride=dil), :] = val


def _attn_geometry(T, dil):
    nb = T // dil // QBLK
    kw = 2 * QBLK if nb >= 2 else QBLK
    return nb, kw


ATTN_UNROLL = 4


def _attn_unit(j, u, dil, nit):
    return ATTN_UNROLL * j + u if dil >= ATTN_UNROLL else j + u * (nit // ATTN_UNROLL)


def _attn_block(it, dil, kw):
    c, n = it & (dil - 1), lax.shift_right_logical(it, dil.bit_length() - 1)
    sq = n * (QBLK * dil) + c
    sk = jnp.maximum(n - (kw // QBLK - 1), 0) * (QBLK * dil) + c
    qi = lax.broadcasted_iota(jnp.int32, (2 * QBLK, kw), 0) & (QBLK - 1)
    kj = lax.broadcasted_iota(jnp.int32, (2 * QBLK, kw), 1)
    rel = jnp.where(n > 0, kw - QBLK, 0) + qi - kj
    return sq, sk, (rel >= 0) & (rel <= QBLK)


def _stack_heads(xv, head0):
    z = jnp.zeros_like(xv)
    return jnp.concatenate([jnp.where(head0, xv, z), jnp.where(head0, z, xv)], axis=0)


def _unstack_heads(x2, head0):
    return jnp.where(head0, x2[:QBLK], x2[QBLK:])


def _attn_fwd(qf, kf, proj, name):
    T = qf.shape[0]

    def body(q_ref, k_ref, v_ref, o_ref, lse_ref):
        head0 = lax.broadcasted_iota(jnp.int32, (QBLK, 2 * HEAD_DIM), 1) < HEAD_DIM
        for bi, dil in enumerate(DILATIONS):
            nb, kw = _attn_geometry(T, dil)

            nit = nb * dil

            def step(j, carry, bi=bi, dil=dil, kw=kw, nit=nit):
                units = []
                for u in range(ATTN_UNROLL):
                    sq, sk, ok = _attn_block(_attn_unit(j, u, dil, nit), dil, kw)
                    old = (_ld(o_ref, sq, QBLK, dil), _ld(lse_ref, sq, QBLK, dil)) if bi > 0 else None
                    units.append((sq, ok, _ld(q_ref, sq, QBLK, dil).astype(BF16), _ld(k_ref, sk, kw, dil).astype(BF16),
                                  _ld(v_ref, sk, kw, dil).astype(BF16), old))
                results = []
                for sq, ok, qv, kv, vv, old in units:
                    s = jnp.where(ok, _dot(_stack_heads(qv, head0), kv, _NT), NEG_INF)
                    m = jnp.max(s, axis=-1, keepdims=True)
                    p = jnp.exp(s - m).astype(BF16)
                    acc = _dot(p, jnp.concatenate([vv, jnp.ones_like(vv)], axis=1))
                    l = acc[:, 2 * HEAD_DIM:]
                    o_new = _unstack_heads(acc[:, :2 * HEAD_DIM] / l, head0)
                    l_new = _unstack_heads(m + jnp.log(l), head0)
                    if bi > 0:
                        o_old, l_old = old
                        mx = jnp.maximum(l_old, l_new)
                        e0, e1 = jnp.exp(l_old - mx), jnp.exp(l_new - mx)
                        z = e0 + e1
                        o_new = (e0 * o_old + e1 * o_new) / z
                        l_new = mx + jnp.log(z)
                    results.append((sq, o_new, l_new))
                for sq, o_new, l_new in results:
                    _st(o_ref, sq, QBLK, dil, o_new)
                    _st(lse_ref, sq, QBLK, dil, l_new)
                return carry

            lax.fori_loop(0, nit // ATTN_UNROLL, step, 0)

    blk = lambda off: pl.BlockSpec((T, 2 * HEAD_DIM), lambda hp, off=off: (0, off + hp))
    return _call(
        body, name=name, grid=(4,), in_specs=[blk(0), blk(0), blk(8)], out_specs=[blk(0), blk(0)],
        out_shape=[jax.ShapeDtypeStruct((T, ATTN_W), F32)] * 2, compiler_params=_params("parallel"),
    )(qf, kf, proj)


def _attn_bwd(qf, kf, proj, do, lse, delta, name):
    T = qf.shape[0]

    def body(q_ref, k_ref, v_ref, do_ref, lse_ref, dl_ref, dq_ref, dk_ref, dv_ref):
        head0 = lax.broadcasted_iota(jnp.int32, (QBLK, 2 * HEAD_DIM), 1) < HEAD_DIM
        for ref in (dq_ref, dk_ref, dv_ref):
            ref[...] = jnp.zeros_like(ref)
        for dil in DILATIONS:
            nb, kw = _attn_geometry(T, dil)

            nit = nb * dil

            def step(j, carry, dil=dil, kw=kw, nit=nit):
                units = []
                for u in range(ATTN_UNROLL):
                    sq, sk, ok = _attn_block(_attn_unit(j, u, dil, nit), dil, kw)
                    lsev, dlv = _ld(lse_ref, sq, QBLK, dil), _ld(dl_ref, sq, QBLK, dil)
                    units.append((sq, sk, ok, _ld(q_ref, sq, QBLK, dil).astype(BF16), _ld(do_ref, sq, QBLK, dil).astype(BF16),
                                  jnp.concatenate([lsev[:, 0:1], lsev[:, HEAD_DIM:HEAD_DIM + 1]], axis=0),
                                  jnp.concatenate([dlv[:, 0:1], dlv[:, HEAD_DIM:HEAD_DIM + 1]], axis=0),
                                  _ld(k_ref, sk, kw, dil).astype(BF16), _ld(v_ref, sk, kw, dil).astype(BF16),
                                  _ld(dq_ref, sq, QBLK, dil), _ld(dk_ref, sk, kw, dil), _ld(dv_ref, sk, kw, dil)))
                results = []
                for sq, sk, ok, qv, dov, lse2, dl2, kv, vv, dq0, dk0, dv0 in units:
                    q2, do2 = _stack_heads(qv, head0), _stack_heads(dov, head0)
                    p = jnp.where(ok, jnp.exp(_dot(q2, kv, _NT) - lse2), 0.0)
                    ds = (p * (_dot(do2, vv, _NT) - dl2)).astype(BF16)
                    results.append((sq, sk, dq0 + _unstack_heads(_dot(ds, kv), head0),
                                    dk0 + _dot(ds, q2, _TN), dv0 + _dot(p.astype(BF16), do2, _TN)))
                for sq, sk, dq, dk, dv in results:
                    _st(dq_ref, sq, QBLK, dil, dq)
                    _st(dk_ref, sk, kw, dil, dk)
                    _st(dv_ref, sk, kw, dil, dv)
                return carry

            lax.fori_loop(0, nit // ATTN_UNROLL, step, 0)

    blk = lambda off: pl.BlockSpec((T, 2 * HEAD_DIM), lambda hp, off=off: (0, off + hp))
    return _call(
        body, name=name, grid=(4,), in_specs=[blk(0), blk(0), blk(8), blk(0), blk(0), blk(0)], out_specs=[blk(0)] * 3,
        out_shape=[jax.ShapeDtypeStruct((T, ATTN_W), F32)] * 3, compiler_params=_params("parallel"),
    )(qf, kf, proj, do, lse, delta)


def _attn_norm(attn, g, name, tm=512):
    T = attn.shape[0]

    def body(a_ref, g_ref, o_ref):
        av = a_ref[...]
        r = lax.rsqrt(jnp.mean(av * av, axis=-1, keepdims=True) + EPS)
        o_ref[...] = (av * r * g_ref[...]).astype(BF16)

    row = pl.BlockSpec((tm, ATTN_W), lambda i: (i, 0))
    return _call(
        body, name=name, grid=(T // tm,), in_specs=[row, pl.BlockSpec((1, ATTN_W), lambda i: (0, 0))], out_specs=row,
        out_shape=jax.ShapeDtypeStruct((T, 2 * ATTN_W), BF16), compiler_params=_params("parallel"),
    )(attn, g)


def _attn_norm_bwd(dmix, attn, g, bd, name, tm=512):
    T = attn.shape[0]

    def body(d_ref, a_ref, g_ref, bd_ref, do_ref, dl_ref, dg_ref):
        @pl.when(pl.program_id(0) == 0)
        def _():
            dg_ref[...] = jnp.zeros_like(dg_ref)

        dy, av = d_ref[...], a_ref[...]
        r = lax.rsqrt(jnp.mean(av * av, axis=-1, keepdims=True) + EPS)
        gd = dy * g_ref[...]
        m = jnp.mean(gd * av, axis=-1, keepdims=True)
        da = r * gd - av * (r * r * r) * m
        do_ref[...] = da
        dl_ref[...] = _group_mean(da * av, bd_ref[...]) * float(HEAD_DIM)
        dg_ref[...] += jnp.sum(dy * av * r, axis=0, keepdims=True)

    row = pl.BlockSpec((tm, ATTN_W), lambda i: (i, 0))
    vec = pl.BlockSpec((1, ATTN_W), lambda i: (0, 0))
    return _call(
        body, name=name, grid=(T // tm,),
        in_specs=[row, row, vec, pl.BlockSpec((2 * HEAD_DIM, 2 * HEAD_DIM), lambda i: (0, 0))], out_specs=[row, row, vec],
        out_shape=[jax.ShapeDtypeStruct((T, ATTN_W), F32)] * 2 + [jax.ShapeDtypeStruct((1, ATTN_W), F32)],
        compiler_params=_params("arbitrary"),
    )(dmix, attn, g, bd)


def _rec_gates(xc, wrg_ref, wig_ref, brg_ref, big_ref, lam_ref):
    xb = xc.astype(BF16)
    r = _sigmoid(_dot(xb, wrg_ref[...]) + brg_ref[...])
    ig = _sigmoid(_dot(xb, wig_ref[...]) + big_ref[...])
    sp = _softplus_neg(lam_ref[...])
    log_a = -LRU_C * r * sp
    a = jnp.exp(log_a)
    th = jnp.tanh(log_a)
    mult = jnp.sqrt(-2.0 * th / (1.0 - th))
    return xb, r, ig, sp, a, mult


def _rec_fwd(proj, mix, cw, cb, wrg, wig, brg, big, lam, g, name, tm=256):
    T = proj.shape[0]
    hb = tm // 8

    def body(xr_ref, halo_ref, gr_ref, cw_ref, cb_ref, wrg_ref, wig_ref, brg_ref, big_ref, lam_ref, g_ref, mix_ref,
             xc_ref, h_ref, out_ref, carry):
        i = pl.program_id(0)

        @pl.when(i == 0)
        def _():
            carry[...] = jnp.zeros_like(carry)

        xr = xr_ref[...]
        halo = jnp.where(i > 0, halo_ref[...], 0.0)
        xc = cb_ref[...] + cw_ref[3:4, :] * xr
        for s in range(1, REC_CONV):
            xc = xc + cw_ref[3 - s:4 - s, :] * _shift_down(xr, halo, s)
        xc_ref[...] = xc
        _, _, ig, _, a, mult = _rec_gates(xc, wrg_ref, wig_ref, brg_ref, big_ref, lam_ref)
        pa, hl = _scan_fwd(a, mult * (ig * xc))
        h = hl + pa * carry[0:1, :]
        h_ref[...] = h
        carry[0:1, :] = h_ref[pl.ds(tm - 1, 1), :]
        hg = h * _gelu(gr_ref[...])
        r = lax.rsqrt(jnp.mean(hg * hg, axis=-1, keepdims=True) + EPS)
        out_ref[...] = (hg * r * g_ref[...]).astype(BF16)

    vec = pl.BlockSpec((1, REC_W), lambda i: (0, 0))
    row = pl.BlockSpec((tm, REC_W), lambda i: (i, 0))
    mat = pl.BlockSpec((REC_W, REC_W), lambda i: (0, 0))
    return _call(
        body, name=name, grid=(T // tm,),
        in_specs=[pl.BlockSpec((tm, REC_W), lambda i: (i, 3)),
                  pl.BlockSpec((8, REC_W), lambda i: (jnp.maximum(i * hb - 1, 0), 3)),
                  pl.BlockSpec((tm, REC_W), lambda i: (i, 4)),
                  pl.BlockSpec((8, REC_W), lambda i: (0, 0)), vec, mat, mat, vec, vec, vec, vec, ANY],
        out_specs=[row, row, pl.BlockSpec((tm, REC_W), lambda i: (i, 1))],
        out_shape=[jax.ShapeDtypeStruct((T, REC_W), F32)] * 2 + [jax.ShapeDtypeStruct(mix.shape, BF16)],
        scratch_shapes=[pltpu.VMEM((8, REC_W), F32)], input_output_aliases={11: 2},
        compiler_params=_params("arbitrary"),
    )(proj, proj, proj, cw, cb, wrg, wig, brg, big, lam, g, mix)


def _rec_bwd(dmix, proj, xc, h, cw, cb, wrg, wig, brg, big, lam, g, name, tm=256):
    T = proj.shape[0]
    nt = T // tm
    hb = tm // 8

    def body(d_ref, xr_ref, xhalo_ref, gr_ref, xc_ref, h_ref, hhalo_ref, cw_ref, cb_ref, wrg_ref, wig_ref, brg_ref,
             big_ref, lam_ref, g_ref,
             drec_ref, gcw_ref, gcb_ref, gwrg_ref, gwig_ref, gbrg_ref, gbig_ref, glam_ref, gg_ref,
             g_carry, a_first, dxc_next, gsp):
        i = pl.program_id(0)
        first_tile = i == nt - 1

        @pl.when(i == 0)
        def _():
            for ref in (gcw_ref, gcb_ref, gwrg_ref, gwig_ref, gbrg_ref, gbig_ref, glam_ref, gg_ref,
                        g_carry, a_first, dxc_next, gsp):
                ref[...] = jnp.zeros_like(ref)

        xr, xc, hv = xr_ref[...], xc_ref[...], h_ref[...]
        xhalo = jnp.where(first_tile, 0.0, xhalo_ref[...])
        hhalo = jnp.where(first_tile, 0.0, hhalo_ref[...])
        xb, r, ig, sp, a, mult = _rec_gates(xc, wrg_ref, wig_ref, brg_ref, big_ref, lam_ref)
        h_prev = _shift_down(hv, hhalo, 1)
        ge, dge = _gelu_and_grad(gr_ref[...])
        hg = hv * ge
        rr = lax.rsqrt(jnp.mean(hg * hg, axis=-1, keepdims=True) + EPS)
        dy = d_ref[...]
        gd = dy * g_ref[...]
        dhg = rr * gd - hg * (rr * rr * rr) * jnp.mean(gd * hg, axis=-1, keepdims=True)
        gg_ref[...] += jnp.sum(dy * hg * rr, axis=0, keepdims=True)
        dgr = (dhg * hv * dge).astype(BF16)
        dh = dhg * ge
        b = _shift_up(a, jnp.broadcast_to(a_first[0:1, :], (8, REC_W)), 1)
        pb, gl = _scan_bwd(b, dh)
        gs = gl + pb * g_carry[0:1, :]
        g_carry[0:1, :] = gs[0:1, :]
        a_first[0:1, :] = a[0:1, :]
        da = gs * h_prev
        dmult = gs * (ig * xc)
        di = gs * (mult * xc)
        dxc = gs * (mult * ig)
        dlog_a = da * a - dmult * (a * a) / mult
        gsp[...] += jnp.sum(dlog_a * (-LRU_C * r), axis=0, keepdims=True)
        dzr = (dlog_a * (-LRU_C * sp)) * (r * (1.0 - r))
        dzi = di * (ig * (1.0 - ig))
        dzr_b, dzi_b = dzr.astype(BF16), dzi.astype(BF16)
        dxc = dxc + _dot(dzr_b, wrg_ref[...], _NT) + _dot(dzi_b, wig_ref[...], _NT)
        gwrg_ref[...] += _dot(xb, dzr_b, _TN)
        gwig_ref[...] += _dot(xb, dzi_b, _TN)
        gbrg_ref[...] += jnp.sum(dzr, axis=0, keepdims=True)
        gbig_ref[...] += jnp.sum(dzi, axis=0, keepdims=True)
        nxt = dxc_next[...]
        dxr = cw_ref[3:4, :] * dxc
        gcw_ref[3:4, :] += jnp.sum(dxc * xr, axis=0, keepdims=True)
        for s in range(1, REC_CONV):
            dxr = dxr + cw_ref[3 - s:4 - s, :] * _shift_up(dxc, nxt, s)
            gcw_ref[3 - s:4 - s, :] += jnp.sum(dxc * _shift_down(xr, xhalo, s), axis=0, keepdims=True)
        gcb_ref[...] += jnp.sum(dxc, axis=0, keepdims=True)
        dxc_next[...] = dxc[:8]
        drec_ref[...] = jnp.concatenate([dxr.astype(BF16), dgr], axis=1)

        @pl.when(first_tile)
        def _():
            glam_ref[...] = gsp[...] * (-_sigmoid(-lam_ref[...]))

    rev = lambda i: nt - 1 - i
    vec = pl.BlockSpec((1, REC_W), lambda i: (0, 0))
    row = pl.BlockSpec((tm, REC_W), lambda i: (rev(i), 0))
    mat = pl.BlockSpec((REC_W, REC_W), lambda i: (0, 0))
    cwb = pl.BlockSpec((8, REC_W), lambda i: (0, 0))
    halo = lambda c: pl.BlockSpec((8, REC_W), lambda i, c=c: (jnp.maximum(rev(i) * hb - 1, 0), c))
    return _call(
        body, name=name, grid=(nt,),
        in_specs=[pl.BlockSpec((tm, REC_W), lambda i: (rev(i), 1)),
                  pl.BlockSpec((tm, REC_W), lambda i: (rev(i), 3)), halo(3),
                  pl.BlockSpec((tm, REC_W), lambda i: (rev(i), 4)),
                  row, row, halo(0), cwb, vec, mat, mat, vec, vec, vec, vec],
        out_specs=[pl.BlockSpec((tm, 2 * REC_W), lambda i: (rev(i), 0)), cwb, vec, mat, mat, vec, vec, vec, vec],
        out_shape=[jax.ShapeDtypeStruct((T, 2 * REC_W), BF16)]
        + [jax.ShapeDtypeStruct((8, REC_W), F32), jax.ShapeDtypeStruct((1, REC_W), F32)]
        + [jax.ShapeDtypeStruct((REC_W, REC_W), F32)] * 2 + [jax.ShapeDtypeStruct((1, REC_W), F32)] * 4,
        scratch_shapes=[pltpu.VMEM((8, REC_W), F32)] * 3 + [pltpu.VMEM((1, REC_W), F32)],
        compiler_params=_params("arbitrary"),
    )(dmix, proj, proj, proj, xc, h, h, cw, cb, wrg, wig, brg, big, lam, g)


def _ffn_conv(x_ext, cw_ref, cb_ref):
    return (cb_ref[...] + cw_ref[2:3, :] * x_ext + cw_ref[1:2, :] * pltpu.roll(x_ext, 1, 0)
            + cw_ref[0:1, :] * pltpu.roll(x_ext, 2, 0))


def _up_proj_act(h2, w_upT, cw, cb, name, tm=1024, tc=768):
    T = h2.shape[0]
    nc = D_FF // tc

    def body(h_ref, wg_ref, wu_ref, cwg_ref, cwu_ref, cbg_ref, cbu_ref, act_ref, da_ref, db_ref, pg_ref, pu_ref,
             hist_g, hist_u):
        i, j = pl.program_id(0), pl.program_id(1)
        hv = h_ref[...]
        pg, pu = _dot(hv, wg_ref[...], _NT), _dot(hv, wu_ref[...], _NT)
        ge = jnp.concatenate([jnp.where(i > 0, hist_g[j], 0.0), pg], axis=0)
        ue = jnp.concatenate([jnp.where(i > 0, hist_u[j], 0.0), pu], axis=0)
        gel, dgel = _gelu_and_grad(_ffn_conv(ge, cwg_ref, cbg_ref)[8:])
        uu = _ffn_conv(ue, cwu_ref, cbu_ref)[8:]
        act_ref[...] = (gel * uu).astype(BF16)
        da_ref[...] = (uu * dgel).astype(BF16)
        db_ref[...] = gel.astype(BF16)
        pg_ref[...] = pg.astype(BF16)
        pu_ref[...] = pu.astype(BF16)
        hist_g[j] = pg[tm - 8:]
        hist_u[j] = pu[tm - 8:]

    tile = pl.BlockSpec((tm, tc), lambda i, j: (i, j))
    wsp = lambda off: pl.BlockSpec((tc, D_MODEL), lambda i, j, off=off: (j + off, 0))
    cws = lambda off: pl.BlockSpec((8, tc), lambda i, j, off=off: (0, j + off))
    cbs = lambda off: pl.BlockSpec((1, tc), lambda i, j, off=off: (0, j + off))
    return _call(
        body, name=name, grid=(T // tm, nc),
        in_specs=[pl.BlockSpec((tm, D_MODEL), lambda i, j: (i, 0)), wsp(0), wsp(nc), cws(0), cws(nc), cbs(0), cbs(nc)],
        out_specs=[tile] * 5, out_shape=[jax.ShapeDtypeStruct((T, D_FF), BF16)] * 5,
        scratch_shapes=[pltpu.VMEM((nc, 8, tc), F32)] * 2, compiler_params=_params("arbitrary", "arbitrary"),
    )(h2, w_upT, w_upT, cw, cw, cb, cb)


def _ffn_bwd(dyb, w_down, da, db, pg, pu, cw, name, tm=512, tc=768):
    T, F = pg.shape
    nt = T // tm
    hb16 = tm // 16
    nc = F // tc
    n = tm + 8

    def body(dy_ref, dyn_ref, wd_ref, a_ref, an_ref, b_ref, bn_ref, g_ref, u_ref, cwg_ref, cwu_ref,
             dg_ref, du_ref, gcwg_ref, gcwu_ref, gcbg_ref, gcbu_ref):
        i = pl.program_id(1)
        last = i == nt - 1

        @pl.when(i == 0)
        def _():
            for ref in (gcwg_ref, gcwu_ref, gcbg_ref, gcbu_ref):
                ref[...] = jnp.zeros_like(ref)

        wd = wd_ref[...]
        dact_next = jnp.where(last, 0.0, _dot(dyn_ref[...], wd, _NT)[:8])
        de = jnp.concatenate([_dot(dy_ref[...], wd, _NT), dact_next], axis=0)
        ext = lambda t, nx: jnp.concatenate([t[...].astype(F32), nx[...].astype(F32)[:8]], axis=0)
        for dcv, x_ref, cw_ref, dx_ref, gcw_ref, gcb_ref in ((de * ext(a_ref, an_ref), g_ref, cwg_ref, dg_ref, gcwg_ref, gcbg_ref),
                                                               (de * ext(b_ref, bn_ref), u_ref, cwu_ref, du_ref, gcwu_ref, gcbu_ref)):
            s1, s2 = pltpu.roll(dcv, n - 1, 0), pltpu.roll(dcv, n - 2, 0)
            dx_ref[...] = (cw_ref[2:3, :] * dcv + cw_ref[1:2, :] * s1 + cw_ref[0:1, :] * s2)[:tm].astype(BF16)
            xv = x_ref[...].astype(F32)
            gcw_ref[2:3, :] += jnp.sum(xv * dcv[:tm], axis=0, keepdims=True)
            gcw_ref[1:2, :] += jnp.sum(xv * s1[:tm], axis=0, keepdims=True)
            gcw_ref[0:1, :] += jnp.sum(xv * s2[:tm], axis=0, keepdims=True)
            gcb_ref[...] += jnp.sum(dcv[:tm], axis=0, keepdims=True)

    tile = pl.BlockSpec((tm, tc), lambda j, i: (i, j))
    nxt = pl.BlockSpec((16, tc), lambda j, i: (jnp.minimum((i + 1) * hb16, nt * hb16 - 1), j))
    cws = lambda off: pl.BlockSpec((8, tc), lambda j, i, off=off: (0, j + off))
    cbs = pl.BlockSpec((1, tc), lambda j, i: (0, j))
    return _call(
        body, name=name, grid=(nc, nt),
        in_specs=[pl.BlockSpec((tm, D_MODEL), lambda j, i: (i, 0)),
                  pl.BlockSpec((16, D_MODEL), lambda j, i: (jnp.minimum((i + 1) * hb16, nt * hb16 - 1), 0)),
                  pl.BlockSpec((tc, D_MODEL), lambda j, i: (j, 0)), tile, nxt, tile, nxt, tile, tile, cws(0), cws(nc)],
        out_specs=[tile, tile, cws(0), cws(0), cbs, cbs],
        out_shape=[jax.ShapeDtypeStruct((T, F), BF16)] * 2 + [jax.ShapeDtypeStruct((8, F), F32)] * 2
        + [jax.ShapeDtypeStruct((1, F), F32)] * 2,
        compiler_params=_params("parallel", "arbitrary"),
    )(dyb, dyb, w_down, da, da, db, db, pg, pu, cw, cw)


def _adam_update(w, g, m, v):
    m2 = ADAM_B1 * m + (1.0 - ADAM_B1) * g
    v2 = ADAM_B2 * v + (1.0 - ADAM_B2) * (g * g)
    m_hat = m2 / (1.0 - ADAM_B1 ** ADAM_STEP)
    v_hat = v2 / (1.0 - ADAM_B2 ** ADAM_STEP)
    delta = -ADAM_LR * (m_hat / (jnp.sqrt(v_hat) + ADAM_EPS) + ADAM_WD * w)
    return delta, m2, v2


def _adam_sharded(p, r2, idx, w, m, v, name, transposed=False):
    r, n = p.shape[1:]
    nrecv = r2.shape[0]
    tr = (256 if r % 256 == 0 else r) if transposed else _row_tile(r)

    def body(c_ref, p_ref, r_ref, w_ref, m_ref, v_ref, g_ref, d_ref, m2_ref, v2_ref):
        g = p_ref[...].astype(F32)
        for k in range(nrecv):
            g = g + r_ref[k].astype(F32)
        if transposed:
            g = g.T
        g_ref[...] = g
        d_ref[...], m2_ref[...], v2_ref[...] = _adam_update(w_ref[...], g, m_ref[...], v_ref[...])

    blk = pl.BlockSpec((n, tr), lambda i, c_ref: (0, i)) if transposed else pl.BlockSpec((tr, n), lambda i, c_ref: (i, 0))
    spec = pltpu.PrefetchScalarGridSpec(
        num_scalar_prefetch=1, grid=(r // tr,),
        in_specs=[pl.BlockSpec((None, tr, n), lambda i, c_ref: (c_ref[0], i, 0)),
                  pl.BlockSpec((nrecv, tr, n), lambda i, c_ref: (0, i, 0)), blk, blk, blk],
        out_specs=[blk] * 4)
    return _call(body, name=name, grid_spec=spec, out_shape=[jax.ShapeDtypeStruct(w.shape, F32)] * 4,
                 compiler_params=_params("parallel"))(idx, p, r2, w, m, v)


def _sum_devices(allg, name):
    r, n = allg.shape[0] // N_DEV, allg.shape[1]

    def body(a_ref, o_ref):
        acc = a_ref[0:r, :]
        for k in range(1, N_DEV):
            acc = acc + a_ref[k * r:(k + 1) * r, :]
        o_ref[...] = acc

    return _call(body, name=name, out_shape=jax.ShapeDtypeStruct((r, n), F32))(allg)


def _adam_small(ws, gs, ms, vs, name):
    n = len(ws)

    def body(*refs):
        for i in range(n):
            d, m2, v2 = _adam_update(refs[i][...], refs[n + i][...], refs[2 * n + i][...], refs[3 * n + i][...])
            refs[4 * n + i][...] = d
            refs[5 * n + i][...] = m2
            refs[6 * n + i][...] = v2

    outs = _call(body, name=name, out_shape=[jax.ShapeDtypeStruct(w.shape, F32) for w in ws] * 3)(*ws, *gs, *ms, *vs)
    return outs[:n], outs[n:2 * n], outs[2 * n:]


_SMALL = (("g_mix", 1024), ("q_norm_g", 64), ("k_norm_g", 64), ("rec_conv_b", 512), ("w_rg", 32768), ("b_rg", 512),
          ("w_ig", 32768), ("b_ig", 512), ("lru_lambda", 512), ("g_attn_out", 512), ("g_rec_out", 512),
          ("g_ffn", 1024), ("ffn_conv_b", 6144))
_SMALL_SHAPES = {"g_mix": (1, 1024), "q_norm_g": (1, 64), "k_norm_g": (1, 64), "rec_conv_b": (1, 512),
                 "w_rg": (1, 8, 64, 64), "b_rg": (1, 8, 64), "w_ig": (1, 8, 64, 64), "b_ig": (1, 8, 64),
                 "lru_lambda": (1, 512), "g_attn_out": (1, 512), "g_rec_out": (1, 512), "g_ffn": (1, 1024),
                 "ffn_conv_b": (1, 6144)}


def _block_diag(w):
    eye = jnp.eye(8, dtype=w.dtype)
    return (w[:, :, None, :] * eye[:, None, :, None]).reshape(512, 512)


def kernel(x, positions, g_mix, w_in, q_norm_g, k_norm_g, rec_conv_w, rec_conv_b, w_rg, b_rg, w_ig, b_ig, lru_lambda, g_attn_out, g_rec_out, w_out, g_ffn, w_up, ffn_conv_w, ffn_conv_b, w_down, loss_target, m_g_mix, m_w_in, m_q_norm_g, m_k_norm_g, m_rec_conv_w, m_rec_conv_b, m_w_rg, m_b_rg, m_w_ig, m_b_ig, m_lru_lambda, m_g_attn_out, m_g_rec_out, m_w_out, m_g_ffn, m_w_up, m_ffn_conv_w, m_ffn_conv_b, m_w_down, v_g_mix, v_w_in, v_q_norm_g, v_k_norm_g, v_rec_conv_w, v_rec_conv_b, v_w_rg, v_b_rg, v_w_ig, v_b_ig, v_lru_lambda, v_g_attn_out, v_g_rec_out, v_w_out, v_g_ffn, v_w_up, v_ffn_conv_w, v_ffn_conv_b, v_w_down):
    T = x.shape[1]
    ix, iy, ic = lax.axis_index("x"), lax.axis_index("y"), lax.axis_index("c")
    dev = 4 * ix + 2 * iy + ic
    xs = x.reshape(T, D_MODEL)
    tgt = loss_target.reshape(T, D_MODEL)
    pos = positions.reshape(T, 1)

    shards = {"w_in": (w_in[0], m_w_in[0], v_w_in[0]), "w_out": (w_out[0], m_w_out[0], v_w_out[0]),
              "w_up": (w_up[0], m_w_up[0], v_w_up[0]), "w_down": (w_down[0], m_w_down[0], v_w_down[0])}
    taps = jnp.concatenate([rec_conv_w.reshape(-1), ffn_conv_w.reshape(-1), jnp.zeros((4096 - 2560,), F32)]).reshape(8, 512)
    W_inT, taps_all = _all_gather([w_in[0].T.astype(BF16), taps], "ag_w_in")
    late = [w_out[0].astype(BF16), w_up[0].T.astype(BF16), w_down[0].astype(BF16)]
    ag_send, ag_recv, late_thru, land_thru, ag_token = _exchange_start(
        late, [_landing((N_DEV * s.shape[0], 1024), BF16, s, dev * s.shape[0]) for s in late], "gather", taps_all,
        "ag_late_start")
    taps_all = taps_all.reshape(N_DEV, 4096)
    rcw = taps_all[:, :256].reshape(8, 4, 64).transpose(1, 0, 2).reshape(4, REC_W)
    fcw = taps_all[:, 256:2560].reshape(8, 3, 768).transpose(1, 0, 2).reshape(3, 2 * D_FF)
    rcw8 = jnp.pad(rcw, ((0, 4), (0, 0)))
    fcw8 = jnp.pad(fcw, ((0, 5), (0, 0)))
    fcb = ffn_conv_b.reshape(1, 2 * D_FF)

    half = HEAD_DIM // 2
    inv_freq = ROPE_THETA ** (-jnp.arange(half, dtype=F32) / half)
    invf = jnp.tile(inv_freq, 2 * N_HEADS).reshape(1, ATTN_W)
    bd = jnp.asarray(np.kron(np.eye(2), np.full((HEAD_DIM, HEAD_DIM), 1.0 / HEAD_DIM)), BF16)
    qg = jnp.tile(q_norm_g.reshape(HEAD_DIM), N_HEADS).reshape(1, ATTN_W)
    kg = jnp.tile(k_norm_g.reshape(HEAD_DIM), N_HEADS).reshape(1, ATTN_W)
    wrg_bd = _block_diag(w_rg[0]).astype(BF16)
    wig_bd = _block_diag(w_ig[0]).astype(BF16)
    brg, big = b_rg.reshape(1, REC_W), b_ig.reshape(1, REC_W)

    h1 = _rmsnorm(xs, g_mix + ag_token[0, 0], "norm_mix")
    proj = _mm(h1, W_inT, "nt", F32, "in_proj", tn=1280)
    qf, kf = _qk_prep(proj, pos, invf, qg, kg, bd, "qk_prep")
    attn, lse = _attn_fwd(qf, kf, proj, "attn_fwd")
    mix = _attn_norm(attn, g_attn_out, "attn_norm")
    xc, hstate, mix = _rec_fwd(proj, mix, rcw8, rec_conv_b, wrg_bd, wig_bd, brg, big, lru_lambda, g_rec_out, "rec_fwd")
    _, (W_out, W_upT, W_down) = _exchange_wait(ag_send, ag_recv, late_thru, land_thru, "gather", hstate, "ag_late_wait")
    x2 = _mm(mix, W_out, "nn", F32, "out_proj", add=xs)

    h2 = _rmsnorm(x2, g_ffn, "norm_ffn")
    act, da, db, pg, pu = _up_proj_act(h2, W_upT, fcw8, fcb, "up_proj_act")
    dy, dyb, lparts = _mm(act, W_down, "nn", F32, "down_proj_loss", add=x2, loss_target=tgt)
    loss_mine = 0.5 / D_MODEL * jnp.sum(lparts)

    g_down = _mm(act, dyb, "tn", BF16, "g_w_down", tk=2048)
    dpg, dpu, g_fcwg, g_fcwu, g_fcbg, g_fcbu = _ffn_bwd(dyb, W_down, da, db, pg, pu, fcw8, "ffn_bwd")
    g_upT = _mm(dpg, h2, "tn", BF16, "g_w_up_gate", tk=2048, o_rows=2 * D_FF)
    g_upT = _mm(dpu, h2, "tn", BF16, "g_w_up_up", tk=2048, into=g_upT, o_moff=D_FF // 1024)
    dh2 = _mm(dpg, W_upT, "nn", F32, "d_h2_gate", k=D_FF)
    dh2 = _mm(dpu, W_upT, "nn", F32, "d_h2_up", k=D_FF, b_koff=D_FF // 1024, add=dh2)
    ffn_g = [g_upT.reshape(N_DEV, 2 * D_FF // N_DEV, 1024), g_down.reshape(N_DEV, D_FF // N_DEV, 1024)]
    rs_send, rs_recv, ffn_g, ffn_land, rs_token = _exchange_start(
        ffn_g, [_landing((N_PEERS,) + g.shape[1:], BF16) for g in ffn_g], "scatter", dh2, "rs_ffn_start")
    dx2, dx2b, g_gffn = _rmsnorm_bwd(x2, dh2, dy, g_ffn + rs_token[0, 0], "norm_ffn_bwd")

    dmix = _mm(dx2b, W_out, "nt", F32, "d_mix")
    g_out = _mm(mix, dx2b, "tn", BF16, "g_w_out", tk=2048).reshape(N_DEV, D_MODEL // N_DEV, 1024)
    out_send, out_recv, (g_out,), out_land, out_token = _exchange_start(
        [g_out], [_landing((N_PEERS,) + g_out.shape[1:], BF16)], "scatter", dmix, "rs_out_start")
    do, delta, g_gattn = _attn_norm_bwd(dmix, attn, g_attn_out + out_token[0, 0], bd, "attn_norm_bwd")
    dqh, dkh, dv = _attn_bwd(qf, kf, proj, do, lse, delta, "attn_bwd")
    dqkv, g_qg, g_kg = _qk_prep_bwd(proj, dqh, dkh, dv, pos, invf, qg, kg, bd, "qk_prep_bwd")
    (drec, g_rcw, g_rcb, g_wrg, g_wig, g_brg, g_big, g_lam, g_grec) = _rec_bwd(
        dmix, proj, xc, hstate, rcw8, rec_conv_b, wrg_bd, wig_bd, brg, big, lru_lambda, g_rec_out, "rec_bwd")
    g_inT = _mm(dqkv, h1, "tn", BF16, "g_w_in_qkv", tm=512, o_rows=IN_W)
    g_inT = _mm(drec, h1, "tn", BF16, "g_w_in_rec", tm=512, into=g_inT, o_moff=3 * ATTN_W // 512)
    g_inT = g_inT.reshape(N_DEV, IN_W // N_DEV, 1024)
    in_send, in_recv, (g_inT,), in_land, in_token = _exchange_start(
        [g_inT], [_landing((N_PEERS,) + g_inT.shape[1:], BF16)], "scatter", drec, "rs_in_start")
    dh1 = _mm(dqkv, W_inT, "nn", F32, "d_h1_qkv", tk=512, k=3 * ATTN_W)
    dh1 = _mm(drec, W_inT, "nn", F32, "d_h1_rec", tk=512, k=2 * REC_W, b_koff=3 * ATTN_W // 512, add=dh1)
    grad_x, _, g_gmix = _rmsnorm_bwd(xs, dh1, dx2, g_mix + in_token[0, 0], "norm_mix_bwd")

    blocks = lambda g: jnp.stack([g[64 * n:64 * n + 64, 64 * n:64 * n + 64] for n in range(8)])
    small_g = {
        "g_mix": g_gmix, "q_norm_g": g_qg.reshape(N_HEADS, HEAD_DIM).sum(0), "k_norm_g": g_kg.reshape(N_HEADS, HEAD_DIM).sum(0),
        "rec_conv_b": g_rcb, "w_rg": blocks(g_wrg), "b_rg": g_brg, "w_ig": blocks(g_wig), "b_ig": g_big,
        "lru_lambda": g_lam, "g_attn_out": g_gattn, "g_rec_out": g_grec, "g_ffn": g_gffn,
        "ffn_conv_b": jnp.concatenate([g_fcbg, g_fcbu], axis=1)}
    g_fcw = jnp.concatenate([g_fcwg[:3], g_fcwu[:3]], axis=1)
    flat = jnp.concatenate([small_g[k].reshape(-1) for k, _ in _SMALL]
                           + [g_rcw[:4].reshape(-1), g_fcw.reshape(-1), loss_mine.reshape(1)])
    flat = jnp.pad(flat, (0, SMALL_ROWS * 1024 - flat.shape[0])).reshape(SMALL_ROWS, 1024)
    tot = _sum_devices(_all_gather([flat], "ag_small_grads")[0], "sum_small_grads").reshape(-1)
    g_small, o = {}, 0
    for k, n in _SMALL:
        g_small[k] = tot[o:o + n].reshape(_SMALL_SHAPES[k])
        o += n
    g_small["rec_conv_w"] = lax.dynamic_slice(tot[o:o + 2048].reshape(1, 4, REC_W), (0, 0, 64 * dev), (1, 4, 64))
    g_small["ffn_conv_w"] = lax.dynamic_slice(tot[o + 2048:o + 2048 + 18432].reshape(1, 3, 2 * D_FF), (0, 0, 768 * dev), (1, 3, 768))
    loss = tot[o + 2048 + 18432]

    devi = jnp.reshape(dev, (1,)).astype(jnp.int32)
    ffn_g, ffn_land = _exchange_wait(rs_send, rs_recv, ffn_g, ffn_land, "scatter", tot, "rs_ffn_wait")
    (g_out,), out_land = _exchange_wait(out_send, out_recv, [g_out], out_land, "scatter", tot, "rs_out_wait")
    (g_inT,), in_land = _exchange_wait(in_send, in_recv, [g_inT], in_land, "scatter", tot, "rs_in_wait")
    big_out = {"grad": {}, "delta": {}, "new_m": {}, "new_v": {}}
    for nm, p, r in (("w_up", ffn_g[0], ffn_land[0]), ("w_down", ffn_g[1], ffn_land[1]), ("w_out", g_out, out_land[0]),
                     ("w_in", g_inT, in_land[0])):
        w_, m_, v_ = shards[nm]
        res = _adam_sharded(p, r, devi, w_, m_, v_, "adam_" + nm, transposed=nm in ("w_in", "w_up"))
        for kind, a in zip(("grad", "delta", "new_m", "new_v"), res):
            big_out[kind][nm] = a[None]
    given = dict(rec_conv_w=rec_conv_w, ffn_conv_w=ffn_conv_w,g_mix=g_mix, q_norm_g=q_norm_g, k_norm_g=k_norm_g, rec_conv_b=rec_conv_b, w_rg=w_rg, b_rg=b_rg, w_ig=w_ig,
                 b_ig=b_ig, lru_lambda=lru_lambda, g_attn_out=g_attn_out, g_rec_out=g_rec_out, g_ffn=g_ffn, ffn_conv_b=ffn_conv_b)
    given_m = dict(rec_conv_w=m_rec_conv_w, ffn_conv_w=m_ffn_conv_w, g_mix=m_g_mix, q_norm_g=m_q_norm_g, k_norm_g=m_k_norm_g, rec_conv_b=m_rec_conv_b, w_rg=m_w_rg, b_rg=m_b_rg,
                   w_ig=m_w_ig, b_ig=m_b_ig, lru_lambda=m_lru_lambda, g_attn_out=m_g_attn_out, g_rec_out=m_g_rec_out,
                   g_ffn=m_g_ffn, ffn_conv_b=m_ffn_conv_b)
    given_v = dict(rec_conv_w=v_rec_conv_w, ffn_conv_w=v_ffn_conv_w, g_mix=v_g_mix, q_norm_g=v_q_norm_g, k_norm_g=v_k_norm_g, rec_conv_b=v_rec_conv_b, w_rg=v_w_rg, b_rg=v_b_rg,
                   w_ig=v_w_ig, b_ig=v_b_ig, lru_lambda=v_lru_lambda, g_attn_out=v_g_attn_out, g_rec_out=v_g_rec_out,
                   g_ffn=v_g_ffn, ffn_conv_b=v_ffn_conv_b)
    small = sorted(given)
    ds, m2s, v2s = _adam_small([given[k] for k in small], [g_small[k] for k in small], [given_m[k] for k in small],
                               [given_v[k] for k in small], "adam_small")
    small_out = {"grad": g_small, "delta": dict(zip(small, ds)), "new_m": dict(zip(small, m2s)), "new_v": dict(zip(small, v2s))}

    order = ("g_mix", "w_in", "q_norm_g", "k_norm_g", "rec_conv_w", "rec_conv_b", "w_rg", "b_rg", "w_ig", "b_ig",
             "lru_lambda", "g_attn_out", "g_rec_out", "w_out", "g_ffn", "w_up", "ffn_conv_w", "ffn_conv_b", "w_down")
    outs = [loss, grad_x.reshape(1, T, D_MODEL)]
    for kind in ("grad", "delta", "new_m", "new_v"):
        for name in order:
            outs.append(big_out[kind][name] if name in big_out[kind] else small_out[kind][name])
    return tuple(outs)
```

```python
import math

import numpy as np
import jax
import jax.numpy as jnp
from jax import lax
from jax.experimental import pallas as pl
from jax.experimental.pallas import tpu as pltpu

F32 = jnp.float32
BF16 = jnp.bfloat16

D_MODEL = 1024
HEAD_DIM = 64
ATTN_W = 512
REC_W = 512
N_HEADS = 8
D_FF = 3072
IN_W = 2560
REC_CONV = 4
FFN_CONV = 3
LRU_C = 8.0
ROPE_THETA = 10000.0
EPS = 1e-6
NEG_INF = -1e30
QBLK = 128
DILATIONS = (1, 4, 16)
N_DEV = 8
SMALL_ROWS = 96
ADAM_LR, ADAM_B1, ADAM_B2, ADAM_EPS, ADAM_WD, ADAM_STEP = 0.001, 0.9, 0.999, 1e-08, 0.01, 10
MESH = pl.DeviceIdType.MESH
ANY = pl.BlockSpec(memory_space=pl.ANY)


def _call(body, *, name, **kw):
    return pl.pallas_call(body, name=name, **kw)


def _params(*sem):
    return pltpu.CompilerParams(dimension_semantics=sem, vmem_limit_bytes=56 * 1024 * 1024)


def _gelu(x):
    c = math.sqrt(2.0 / math.pi)
    return 0.5 * x * (1.0 + jnp.tanh(c * (x + 0.044715 * (x * x * x))))


def _gelu_and_grad(x):
    c = math.sqrt(2.0 / math.pi)
    t = jnp.tanh(c * (x + 0.044715 * (x * x * x)))
    g = 0.5 * x * (1.0 + t)
    dg = 0.5 * (1.0 + t) + 0.5 * x * (1.0 - t * t) * (c * (1.0 + 3.0 * 0.044715 * (x * x)))
    return g, dg


def _sigmoid(x):
    return 1.0 / (1.0 + jnp.exp(-x))


def _softplus_neg(lam):
    y = jnp.exp(-jnp.abs(lam))
    u = 1.0 + y
    log1p = jnp.where(u == 1.0, y, jnp.log(u) * y / jnp.where(u == 1.0, 1.0, u - 1.0))
    return jnp.maximum(-lam, 0.0) + log1p


_NN = (((1,), (0,)), ((), ()))
_NT = (((1,), (1,)), ((), ()))
_TN = (((0,), (0,)), ((), ()))


def _dot(a, b, dims=_NN):
    return lax.dot_general(a, b, dims, preferred_element_type=F32)


def _group_mean(v, bd):
    hi = v.astype(BF16)
    lo = (v - hi.astype(F32)).astype(BF16)
    w = bd.shape[0]
    return jnp.concatenate([_dot(hi[:, c:c + w], bd) + _dot(lo[:, c:c + w], bd) for c in range(0, v.shape[1], w)], axis=1)


def _rope_tables(pos_ref, invf_ref):
    ang = pos_ref[...].astype(F32) * invf_ref[:, :2 * HEAD_DIM]
    reps = invf_ref.shape[1] // (2 * HEAD_DIM)
    return jnp.tile(jnp.cos(ang), (1, reps)), jnp.tile(jnp.sin(ang), (1, reps))


def _shift_down(x, halo, s):
    rolled = pltpu.roll(x, s, 0)
    hr = pltpu.roll(halo, s, 0)
    row = lax.broadcasted_iota(jnp.int32, hr.shape, 0)
    first = jnp.where(row < s, hr, rolled[:8])
    return jnp.concatenate([first, rolled[8:]], axis=0)


def _shift_up(x, halo, s):
    n = x.shape[0]
    rolled = pltpu.roll(x, n - s, 0)
    hr = pltpu.roll(halo, 8 - s, 0)
    row = lax.broadcasted_iota(jnp.int32, hr.shape, 0)
    last = jnp.where(row >= 8 - s, hr, rolled[n - 8:])
    return jnp.concatenate([rolled[:n - 8], last], axis=0)


def _scan_fwd(a, u):
    n = a.shape[0]
    row = lax.broadcasted_iota(jnp.int32, a.shape, 0)
    s = 1
    while s < n:
        a_s = jnp.where(row < s, 1.0, pltpu.roll(a, s, 0))
        u_s = jnp.where(row < s, 0.0, pltpu.roll(u, s, 0))
        u = u + a * u_s
        a = a * a_s
        s *= 2
    return a, u


def _scan_bwd(b, v):
    n = b.shape[0]
    row = lax.broadcasted_iota(jnp.int32, b.shape, 0)
    s = 1
    while s < n:
        b_s = jnp.where(row >= n - s, 1.0, pltpu.roll(b, n - s, 0))
        v_s = jnp.where(row >= n - s, 0.0, pltpu.roll(v, n - s, 0))
        v = v + b * v_s
        b = b * b_s
        s *= 2
    return b, v


def _rot_half(y):
    n = y.shape[1]
    lane = lax.broadcasted_iota(jnp.int32, y.shape, 1) & (HEAD_DIM - 1)
    return jnp.where(lane < HEAD_DIM // 2, -pltpu.roll(y, n - HEAD_DIM // 2, 1), pltpu.roll(y, HEAD_DIM // 2, 1))


def _row_tile(r, cap=256):
    return max(t for t in range(16, cap + 1, 16) if r % t == 0)


def _all_gather(shards, name):
    na = len(shards)
    ms = [s.shape[0] for s in shards]

    def body(*refs):
        x_refs, out_refs = refs[:na], refs[na:2 * na]
        send_sems, recv_sems, local_sems = refs[2 * na:]
        x, y, c = lax.axis_index("x"), lax.axis_index("y"), lax.axis_index("c")
        me, sibling = (x, y, c), (x, y, 1 - c)
        chips = [(1 - x, y), (x, 1 - y), (1 - x, 1 - y)]

        def rows(a, px, py, pc):
            return out_refs[a].at[pl.ds((4 * px + 2 * py + pc) * ms[a], ms[a]), :]

        def copy(a, k, block, to, src=None):
            return pltpu.make_async_remote_copy(
                src_ref=rows(a, *block) if src is None else src, dst_ref=rows(a, *block),
                send_sem=send_sems.at[7 * a + k], recv_sem=recv_sems.at[7 * a + k], device_id=to, device_id_type=MESH)

        mine = [pltpu.make_async_copy(x_refs[a], rows(a, *me), local_sems.at[a]) for a in range(na)]
        first = []
        for a in range(na):
            mine[a].start()
            first.append(copy(a, 0, me, sibling, src=x_refs[a]))
            first += [copy(a, 1 + j, me, (*chip, c), src=x_refs[a]) for j, chip in enumerate(chips)]
        for cp in first:
            cp.start()
        passed = []
        for a in range(na):
            for j, chip in enumerate(chips):
                copy(a, 1 + j, (*chip, c), me).wait_recv()
                fw = copy(a, 4 + j, (*chip, c), sibling)
                fw.start()
                passed.append(fw)
        for a in range(na):
            copy(a, 0, sibling, me).wait_recv()
            for j, chip in enumerate(chips):
                copy(a, 4 + j, (*chip, 1 - c), me).wait_recv()
        for cp in first + passed:
            cp.wait_send()
        for cp in mine:
            cp.wait()

    return _call(
        body, name=name, out_shape=[jax.ShapeDtypeStruct((N_DEV * s.shape[0], s.shape[1]), s.dtype) for s in shards],
        in_specs=[ANY] * na, out_specs=[ANY] * na,
        scratch_shapes=[pltpu.SemaphoreType.DMA((7 * na,)), pltpu.SemaphoreType.DMA((7 * na,)),
                        pltpu.SemaphoreType.DMA((na,))],
    )(*shards)


HBM = pl.BlockSpec(memory_space=pltpu.HBM)
SEM = pl.BlockSpec(memory_space=pltpu.SEMAPHORE)
EFFECT = pltpu.SideEffectType.DATAFLOW_SIDE_EFFECTING
N_PEERS = N_DEV - 1


def _peer(k):
    x, y, c = lax.axis_index("x"), lax.axis_index("y"), lax.axis_index("c")
    b = k + 1
    flip = lambda v, bit: 1 - v if bit else v
    return flip(x, b & 4), flip(y, b & 2), flip(c, b & 1)


def _in_hbm(a):
    return pltpu.with_memory_space_constraint(a, pltpu.HBM)


def _split_copy_descr(na, kind, src_refs, land_refs, send_sems, recv_sems):
    x, y, c = lax.axis_index("x"), lax.axis_index("y"), lax.axis_index("c")
    me = 4 * x + 2 * y + c
    copies = []
    for a in range(na):
        for k in range(N_PEERS):
            px, py, pc = _peer(k)
            if kind == "gather":
                m = src_refs[a].shape[0]
                src, dst = src_refs[a], land_refs[a].at[pl.ds(me * m, m), :]
            else:
                src, dst = src_refs[a].at[4 * px + 2 * py + pc], land_refs[a].at[k]
            copies.append(pltpu.make_async_remote_copy(
                src_ref=src, dst_ref=dst, send_sem=send_sems.at[N_PEERS * a + k], recv_sem=recv_sems.at[N_PEERS * a + k],
                device_id=(px, py, pc), device_id_type=MESH))
    return copies


def _landing(shape, dtype, own=None, at=None):
    buf = lax.empty(shape, dtype)
    return buf if own is None else lax.dynamic_update_slice(buf, own, (at, 0))


def _exchange_start(srcs, lands, kind, after, name):
    na = len(srcs)
    land_shapes = [l.shape for l in lands]

    def body(*refs):
        src_refs, land_refs = refs[:na], refs[na:2 * na]
        send_sems, recv_sems = refs[2 * na + 1], refs[2 * na + 2]
        token = refs[-1]
        for cp in _split_copy_descr(na, kind, src_refs, land_refs, send_sems, recv_sems):
            cp.start()
        token[...] = jnp.zeros_like(token)

    lands = [_in_hbm(l) for l in lands]
    sem = pltpu.SemaphoreType.DMA((N_PEERS * na,))
    outs = _call(
        body, name=name,
        out_shape=[sem, sem] + [pltpu.HBM(s.shape, s.dtype) for s in srcs] + [pltpu.HBM(s, srcs[0].dtype) for s in land_shapes]
        + [jax.ShapeDtypeStruct((8, 128), F32)],
        in_specs=[HBM] * (2 * na) + [ANY], out_specs=[SEM, SEM] + [HBM] * (2 * na) + [pl.BlockSpec(memory_space=pltpu.VMEM)],
        input_output_aliases={i: 2 + i for i in range(2 * na)},
        compiler_params=pltpu.CompilerParams(has_side_effects=EFFECT),
    )(*[_in_hbm(s) for s in srcs], *lands, after)
    return outs[0], outs[1], outs[2:2 + na], outs[2 + na:2 + 2 * na], outs[-1]


def _exchange_wait(send_sems, recv_sems, srcs, lands, kind, after, name):
    na = len(srcs)

    def body(*refs):
        src_refs, land_refs = refs[:na], refs[na:2 * na]
        s_sems, r_sems = refs[2 * na], refs[2 * na + 1]
        for cp in _split_copy_descr(na, kind, src_refs, land_refs, s_sems, r_sems):
            cp.wait_send()
            cp.wait_recv()

    outs = _call(
        body, name=name, out_shape=[pltpu.HBM(s.shape, s.dtype) for s in srcs] + [pltpu.HBM(l.shape, l.dtype) for l in lands],
        in_specs=[HBM] * (2 * na) + [SEM, SEM, ANY], out_specs=[HBM] * (2 * na),
        input_output_aliases={i: i for i in range(2 * na)},
        compiler_params=pltpu.CompilerParams(has_side_effects=EFFECT),
    )(*srcs, *lands, send_sems, recv_sems, after)
    return outs[:na], outs[na:]


def _mm(a, b, mode, out_dtype, name, add=None, tm=1024, tn=1024, tk=1024, b_noff=0, b_koff=0,
        n=None, k=None, into=None, o_rows=None, o_moff=0, loss_target=None):
    if mode == "tn":
        K, M = a.shape
    else:
        M, K = a.shape
    N = n if n is not None else (b.shape[0] if mode == "nt" else b.shape[1])
    if k is not None:
        assert k == K
    tm, tn, tk = min(tm, M), min(tn, N), min(tk, K)
    assert M % tm == 0 and N % tn == 0 and K % tk == 0, (name, M, N, K)
    nk = K // tk
    if mode == "nn":
        a_spec = pl.BlockSpec((tm, tk), lambda i, j, kk: (i, kk))
        b_spec, dims = pl.BlockSpec((tk, tn), lambda i, j, kk: (kk + b_koff, j + b_noff)), _NN
    elif mode == "nt":
        a_spec = pl.BlockSpec((tm, tk), lambda i, j, kk: (i, kk))
        b_spec, dims = pl.BlockSpec((tn, tk), lambda i, j, kk: (j + b_noff, kk + b_koff)), _NT
    else:
        a_spec = pl.BlockSpec((tk, tm), lambda i, j, kk: (kk, i))
        b_spec, dims = pl.BlockSpec((tk, tn), lambda i, j, kk: (kk + b_koff, j + b_noff)), _TN
    o_spec = pl.BlockSpec((tm, tn), lambda i, j, kk: (i + o_moff, j))
    has_add, has_into, has_loss = add is not None, into is not None, loss_target is not None
    assert not has_loss or (has_add and tn == N and not has_into)
    n_in = 2 + has_add + has_loss + has_into

    def body(*refs):
        a_ref, b_ref = refs[0], refs[1]
        add_ref = refs[2] if has_add else None
        outs = refs[n_in:]

        def finish(r):
            if has_add:
                r = r + add_ref[...]
            if has_loss:
                e = r - refs[3][...]
                dy = e * (1.0 / N)
                outs[0][...] = dy
                outs[1][...] = dy.astype(BF16)
                outs[2][...] = jnp.sum(e * e, axis=0, keepdims=True)[None]
            else:
                outs[0][...] = r.astype(out_dtype)

        if nk == 1:
            finish(_dot(a_ref[...], b_ref[...], dims))
        else:
            acc = refs[-1]
            kk = pl.program_id(2)

            @pl.when(kk == 0)
            def _():
                acc[...] = _dot(a_ref[...], b_ref[...], dims)

            @pl.when((kk > 0) & (kk < nk - 1))
            def _():
                acc[...] += _dot(a_ref[...], b_ref[...], dims)

            @pl.when(kk == nk - 1)
            def _():
                finish(acc[...] + _dot(a_ref[...], b_ref[...], dims))

    tile = pl.BlockSpec((tm, tn), lambda i, j, kk: (i, j))
    ins = [a, b] + ([add] if has_add else []) + ([loss_target] if has_loss else []) + ([into] if has_into else [])
    specs = [a_spec, b_spec] + [tile] * (has_add + has_loss) + ([ANY] if has_into else [])
    rows = into.shape[0] if has_into else (o_rows if o_rows is not None else M)
    if has_loss:
        out_specs = [tile, tile, pl.BlockSpec((1, 1, N), lambda i, j, kk: (i, 0, 0))]
        out_shape = [jax.ShapeDtypeStruct((M, N), F32), jax.ShapeDtypeStruct((M, N), BF16), jax.ShapeDtypeStruct((M // tm, 1, N), F32)]
    else:
        out_specs, out_shape = o_spec, jax.ShapeDtypeStruct((rows, N), out_dtype)
    return _call(
        body, name=name, grid=(M // tm, N // tn, nk), in_specs=specs, out_specs=out_specs, out_shape=out_shape,
        scratch_shapes=[pltpu.VMEM((tm, tn), F32)] if nk > 1 else [],
        input_output_aliases={len(ins) - 1: 0} if has_into else {},
        compiler_params=_params("parallel", "parallel", "arbitrary"),
    )(*ins)


def _mm_norm_bwd(parts, b, x, resid, g, name, tm=512, tk=512):
    T, N = x.shape
    counts = [p.shape[1] // tk for p in parts]
    starts = [sum(counts[:i]) for i in range(len(parts))]
    nsteps = sum(counts)
    assert all(p.shape[1] % tk == 0 for p in parts) and b.shape == (nsteps * tk, N)
    npart = len(parts)

    def body(*refs):
        a_refs, b_ref, x_ref, res_ref, g_ref = refs[:npart], refs[npart], refs[npart + 1], refs[npart + 2], refs[npart + 3]
        dx_ref, dxb_ref, dg_ref, acc = refs[npart + 4:]
        i, s = pl.program_id(0), pl.program_id(1)

        @pl.when((i == 0) & (s == 0))
        def _():
            dg_ref[...] = jnp.zeros_like(dg_ref)

        for p in range(npart):
            @pl.when((s >= starts[p]) & (s < starts[p] + counts[p]))
            def _(p=p):
                d = _dot(a_refs[p][...], b_ref[...])

                @pl.when(s == 0)
                def _():
                    acc[...] = d

                @pl.when(s > 0)
                def _():
                    acc[...] += d

        @pl.when(s == nsteps - 1)
        def _():
            xv, dhv = x_ref[...], acc[...]
            r = lax.rsqrt(jnp.mean(xv * xv, axis=-1, keepdims=True) + EPS)
            gd = dhv * g_ref[...]
            m = jnp.mean(gd * xv, axis=-1, keepdims=True)
            dx = res_ref[...] + r * gd - xv * (r * r * r) * m
            dx_ref[...] = dx
            dxb_ref[...] = dx.astype(BF16)
            dg_ref[...] += jnp.sum(dhv * xv * r, axis=0, keepdims=True)

    a_specs = [pl.BlockSpec((tm, tk), lambda i, s, st=st, c=c: (i, jnp.clip(s - st, 0, c - 1))) for st, c in zip(starts, counts)]
    row = pl.BlockSpec((tm, N), lambda i, s: (i, 0))
    vec = pl.BlockSpec((1, N), lambda i, s: (0, 0))
    return _call(
        body, name=name, grid=(T // tm, nsteps),
        in_specs=a_specs + [pl.BlockSpec((tk, N), lambda i, s: (s, 0)), row, row, vec], out_specs=[row, row, vec],
        out_shape=[jax.ShapeDtypeStruct((T, N), F32), jax.ShapeDtypeStruct((T, N), BF16), jax.ShapeDtypeStruct((1, N), F32)],
        scratch_shapes=[pltpu.VMEM((tm, N), F32)], compiler_params=_params("arbitrary", "arbitrary"),
    )(*parts, b, x, resid, g)


def _rmsnorm(x, g, name, tm=512):
    T, D = x.shape

    def body(x_ref, g_ref, o_ref):
        xv = x_ref[...]
        r = lax.rsqrt(jnp.mean(xv * xv, axis=-1, keepdims=True) + EPS)
        o_ref[...] = (xv * r * g_ref[...]).astype(BF16)

    return _call(
        body, name=name, grid=(T // tm,),
        in_specs=[pl.BlockSpec((tm, D), lambda i: (i, 0)), pl.BlockSpec((1, D), lambda i: (0, 0))],
        out_specs=pl.BlockSpec((tm, D), lambda i: (i, 0)), out_shape=jax.ShapeDtypeStruct((T, D), BF16),
        compiler_params=_params("parallel"),
    )(x, g)


def _qk_prep(proj, pos, invf, qg, kg, bd, name, tm=512):
    T = proj.shape[0]

    def body(q_ref, k_ref, pos_ref, invf_ref, qg_ref, kg_ref, bd_ref, qo_ref, ko_ref):
        cos, sin = _rope_tables(pos_ref, invf_ref)

        def prep(xv, gv, scale):
            r = lax.rsqrt(_group_mean(xv * xv, bd_ref[...]) + EPS)
            yv = xv * r * gv
            return ((yv * cos + _rot_half(yv) * sin) * scale).astype(BF16).astype(F32)

        qo_ref[...] = prep(q_ref[...], qg_ref[...], HEAD_DIM ** -0.5)
        ko_ref[...] = prep(k_ref[...], kg_ref[...], 1.0)

    col = lambda j: pl.BlockSpec((tm, ATTN_W), lambda i, j=j: (i, j))
    vec = pl.BlockSpec((1, ATTN_W), lambda i: (0, 0))
    out = pl.BlockSpec((tm, ATTN_W), lambda i: (i, 0))
    return _call(
        body, name=name, grid=(T // tm,),
        in_specs=[col(0), col(1), pl.BlockSpec((tm, 1), lambda i: (i, 0)), vec, vec, vec,
                  pl.BlockSpec((2 * HEAD_DIM, 2 * HEAD_DIM), lambda i: (0, 0))],
        out_specs=[out, out], out_shape=[jax.ShapeDtypeStruct((T, ATTN_W), F32)] * 2,
        compiler_params=_params("parallel"),
    )(proj, proj, pos, invf, qg, kg, bd)


def _qk_prep_bwd(proj, dqh, dkh, dv, pos, invf, qg, kg, bd, name, tm=512):
    T = proj.shape[0]

    def body(q_ref, k_ref, dq_ref, dk_ref, dv_ref, pos_ref, invf_ref, qg_ref, kg_ref, bd_ref, o_ref, gq_ref, gk_ref):
        @pl.when(pl.program_id(0) == 0)
        def _():
            gq_ref[...] = jnp.zeros_like(gq_ref)
            gk_ref[...] = jnp.zeros_like(gk_ref)

        cos, sin = _rope_tables(pos_ref, invf_ref)

        def back(xv, gv, dz, scale):
            dz = dz * scale
            dy = dz * cos - _rot_half(dz * sin)
            r = lax.rsqrt(_group_mean(xv * xv, bd_ref[...]) + EPS)
            gd = dy * gv
            m = _group_mean(gd * xv, bd_ref[...])
            dx = r * gd - xv * (r * r * r) * m
            return dx, jnp.sum(dy * xv * r, axis=0, keepdims=True)

        dxq, gs = back(q_ref[...], qg_ref[...], dq_ref[...], HEAD_DIM ** -0.5)
        gq_ref[...] += gs
        dxk, gs = back(k_ref[...], kg_ref[...], dk_ref[...], 1.0)
        gk_ref[...] += gs
        o_ref[...] = jnp.concatenate([dxq.astype(BF16), dxk.astype(BF16), dv_ref[...].astype(BF16)], axis=1)

    col = lambda j: pl.BlockSpec((tm, ATTN_W), lambda i, j=j: (i, j))
    row = pl.BlockSpec((tm, ATTN_W), lambda i: (i, 0))
    vec = pl.BlockSpec((1, ATTN_W), lambda i: (0, 0))
    return _call(
        body, name=name, grid=(T // tm,),
        in_specs=[col(0), col(1), row, row, row, pl.BlockSpec((tm, 1), lambda i: (i, 0)), vec, vec, vec,
                  pl.BlockSpec((2 * HEAD_DIM, 2 * HEAD_DIM), lambda i: (0, 0))],
        out_specs=[pl.BlockSpec((tm, 3 * ATTN_W), lambda i: (i, 0)), vec, vec],
        out_shape=[jax.ShapeDtypeStruct((T, 3 * ATTN_W), BF16)] + [jax.ShapeDtypeStruct((1, ATTN_W), F32)] * 2,
        compiler_params=_params("arbitrary"),
    )(proj, proj, dqh, dkh, dv, pos, invf, qg, kg, bd)


def _ld(ref, start, size, dil):
    return ref[pl.ds(start, size), :] if dil == 1 else ref[pl.ds(start, size, stride=dil), :]


def _st(ref, start, size, dil, val):
    if dil == 1:
        ref[pl.ds(start, size), :] = val
    else:
        ref[pl.ds(start, size, stride=dil), :] = val


def _attn_geometry(T, dil):
    nb = T // dil // QBLK
    kw = 2 * QBLK if nb >= 2 else QBLK
    return nb, kw


ATTN_UNROLL = 4


def _attn_unit(j, u, dil, nit):
    return ATTN_UNROLL * j + u if dil >= ATTN_UNROLL else j + u * (nit // ATTN_UNROLL)


def _attn_block(it, dil, kw):
    c, n = it & (dil - 1), lax.shift_right_logical(it, dil.bit_length() - 1)
    sq = n * (QBLK * dil) + c
    sk = jnp.maximum(n - (kw // QBLK - 1), 0) * (QBLK * dil) + c
    qi = lax.broadcasted_iota(jnp.int32, (2 * QBLK, kw), 0) & (QBLK - 1)
    kj = lax.broadcasted_iota(jnp.int32, (2 * QBLK, kw), 1)
    rel = jnp.where(n > 0, kw - QBLK, 0) + qi - kj
    return sq, sk, (rel >= 0) & (rel <= QBLK)


def _stack_heads(xv, head0):
    z = jnp.zeros_like(xv)
    return jnp.concatenate([jnp.where(head0, xv, z), jnp.where(head0, z, xv)], axis=0)


def _unstack_heads(x2, head0):
    return jnp.where(head0, x2[:QBLK], x2[QBLK:])


def _attn_fwd(qf, kf, proj, name):
    T = qf.shape[0]

    def body(q_ref, k_ref, v_ref, o_ref, lse_ref):
        head0 = lax.broadcasted_iota(jnp.int32, (QBLK, 2 * HEAD_DIM), 1) < HEAD_DIM
        for bi, dil in enumerate(DILATIONS):
            nb, kw = _attn_geometry(T, dil)

            nit = nb * dil

            def step(j, carry, bi=bi, dil=dil, kw=kw, nit=nit):
                units = []
                for u in range(ATTN_UNROLL):
                    sq, sk, ok = _attn_block(_attn_unit(j, u, dil, nit), dil, kw)
                    old = (_ld(o_ref, sq, QBLK, dil), _ld(lse_ref, sq, QBLK, dil)) if bi > 0 else None
                    units.append((sq, ok, _ld(q_ref, sq, QBLK, dil).astype(BF16), _ld(k_ref, sk, kw, dil).astype(BF16),
                                  _ld(v_ref, sk, kw, dil).astype(BF16), old))
                results = []
                for sq, ok, qv, kv, vv, old in units:
                    s = jnp.where(ok, _dot(_stack_heads(qv, head0), kv, _NT), NEG_INF)
                    m = jnp.max(s, axis=-1, keepdims=True)
                    p = jnp.exp(s - m).astype(BF16)
                    acc = _dot(p, jnp.concatenate([vv, jnp.ones_like(vv)], axis=1))
                    l = acc[:, 2 * HEAD_DIM:]
                    o_new = _unstack_heads(acc[:, :2 * HEAD_DIM] / l, head0)
                    l_new = _unstack_heads(m + jnp.log(l), head0)
                    if bi > 0:
                        o_old, l_old = old
                        mx = jnp.maximum(l_old, l_new)
                        e0, e1 = jnp.exp(l_old - mx), jnp.exp(l_new - mx)
                        z = e0 + e1
                        o_new = (e0 * o_old + e1 * o_new) / z
                        l_new = mx + jnp.log(z)
                    results.append((sq, o_new, l_new))
                for sq, o_new, l_new in results:
                    _st(o_ref, sq, QBLK, dil, o_new)
                    _st(lse_ref, sq, QBLK, dil, l_new)
                return carry

            lax.fori_loop(0, nit // ATTN_UNROLL, step, 0)

    blk = lambda off: pl.BlockSpec((T, 2 * HEAD_DIM), lambda hp, off=off: (0, off + hp))
    return _call(
        body, name=name, grid=(4,), in_specs=[blk(0), blk(0), blk(8)], out_specs=[blk(0), blk(0)],
        out_shape=[jax.ShapeDtypeStruct((T, ATTN_W), F32)] * 2, compiler_params=_params("parallel"),
    )(qf, kf, proj)


def _attn_bwd(qf, kf, proj, do, lse, delta, name):
    T = qf.shape[0]

    def body(q_ref, k_ref, v_ref, do_ref, lse_ref, dl_ref, dq_ref, dk_ref, dv_ref):
        head0 = lax.broadcasted_iota(jnp.int32, (QBLK, 2 * HEAD_DIM), 1) < HEAD_DIM
        for ref in (dq_ref, dk_ref, dv_ref):
            ref[...] = jnp.zeros_like(ref)
        for dil in DILATIONS:
            nb, kw = _attn_geometry(T, dil)

            nit = nb * dil

            def step(j, carry, dil=dil, kw=kw, nit=nit):
                units = []
                for u in range(ATTN_UNROLL):
                    sq, sk, ok = _attn_block(_attn_unit(j, u, dil, nit), dil, kw)
                    lsev, dlv = _ld(lse_ref, sq, QBLK, dil), _ld(dl_ref, sq, QBLK, dil)
                    units.append((sq, sk, ok, _ld(q_ref, sq, QBLK, dil).astype(BF16), _ld(do_ref, sq, QBLK, dil).astype(BF16),
                                  jnp.concatenate([lsev[:, 0:1], lsev[:, HEAD_DIM:HEAD_DIM + 1]], axis=0),
                                  jnp.concatenate([dlv[:, 0:1], dlv[:, HEAD_DIM:HEAD_DIM + 1]], axis=0),
                                  _ld(k_ref, sk, kw, dil).astype(BF16), _ld(v_ref, sk, kw, dil).astype(BF16),
                                  _ld(dq_ref, sq, QBLK, dil), _ld(dk_ref, sk, kw, dil), _ld(dv_ref, sk, kw, dil)))
                results = []
                for sq, sk, ok, qv, dov, lse2, dl2, kv, vv, dq0, dk0, dv0 in units:
                    q2, do2 = _stack_heads(qv, head0), _stack_heads(dov, head0)
                    p = jnp.where(ok, jnp.exp(_dot(q2, kv, _NT) - lse2), 0.0)
                    ds = (p * (_dot(do2, vv, _NT) - dl2)).astype(BF16)
                    results.append((sq, sk, dq0 + _unstack_heads(_dot(ds, kv), head0),
                                    dk0 + _dot(ds, q2, _TN), dv0 + _dot(p.astype(BF16), do2, _TN)))
                for sq, sk, dq, dk, dv in results:
                    _st(dq_ref, sq, QBLK, dil, dq)
                    _st(dk_ref, sk, kw, dil, dk)
                    _st(dv_ref, sk, kw, dil, dv)
                return carry

            lax.fori_loop(0, nit // ATTN_UNROLL, step, 0)

    blk = lambda off: pl.BlockSpec((T, 2 * HEAD_DIM), lambda hp, off=off: (0, off + hp))
    return _call(
        body, name=name, grid=(4,), in_specs=[blk(0), blk(0), blk(8), blk(0), blk(0), blk(0)], out_specs=[blk(0)] * 3,
        out_shape=[jax.ShapeDtypeStruct((T, ATTN_W), F32)] * 3, compiler_params=_params("parallel"),
    )(qf, kf, proj, do, lse, delta)


def _attn_norm(attn, g, name, tm=512):
    T = attn.shape[0]

    def body(a_ref, g_ref, o_ref):
        av = a_ref[...]
        r = lax.rsqrt(jnp.mean(av * av, axis=-1, keepdims=True) + EPS)
        o_ref[...] = (av * r * g_ref[...]).astype(BF16)

    row = pl.BlockSpec((tm, ATTN_W), lambda i: (i, 0))
    return _call(
        body, name=name, grid=(T // tm,), in_specs=[row, pl.BlockSpec((1, ATTN_W), lambda i: (0, 0))], out_specs=row,
        out_shape=jax.ShapeDtypeStruct((T, 2 * ATTN_W), BF16), compiler_params=_params("parallel"),
    )(attn, g)


def _attn_norm_bwd(dmix, attn, g, bd, name, tm=512):
    T = attn.shape[0]

    def body(d_ref, a_ref, g_ref, bd_ref, do_ref, dl_ref, dg_ref):
        @pl.when(pl.program_id(0) == 0)
        def _():
            dg_ref[...] = jnp.zeros_like(dg_ref)

        dy, av = d_ref[...], a_ref[...]
        r = lax.rsqrt(jnp.mean(av * av, axis=-1, keepdims=True) + EPS)
        gd = dy * g_ref[...]
        m = jnp.mean(gd * av, axis=-1, keepdims=True)
        da = r * gd - av * (r * r * r) * m
        do_ref[...] = da
        dl_ref[...] = _group_mean(da * av, bd_ref[...]) * float(HEAD_DIM)
        dg_ref[...] += jnp.sum(dy * av * r, axis=0, keepdims=True)

    row = pl.BlockSpec((tm, ATTN_W), lambda i: (i, 0))
    vec = pl.BlockSpec((1, ATTN_W), lambda i: (0, 0))
    return _call(
        body, name=name, grid=(T // tm,),
        in_specs=[row, row, vec, pl.BlockSpec((2 * HEAD_DIM, 2 * HEAD_DIM), lambda i: (0, 0))], out_specs=[row, row, vec],
        out_shape=[jax.ShapeDtypeStruct((T, ATTN_W), F32)] * 2 + [jax.ShapeDtypeStruct((1, ATTN_W), F32)],
        compiler_params=_params("arbitrary"),
    )(dmix, attn, g, bd)


def _rec_gates(xc, wrg_ref, wig_ref, brg_ref, big_ref, lam_ref):
    xb = xc.astype(BF16)
    r = _sigmoid(_dot(xb, wrg_ref[...]) + brg_ref[...])
    ig = _sigmoid(_dot(xb, wig_ref[...]) + big_ref[...])
    sp = _softplus_neg(lam_ref[...])
    log_a = -LRU_C * r * sp
    a = jnp.exp(log_a)
    th = jnp.tanh(log_a)
    mult = jnp.sqrt(-2.0 * th / (1.0 - th))
    return xb, r, ig, sp, a, mult


def _rec_fwd(proj, mix, cw, cb, wrg, wig, brg, big, lam, g, name, tm=256):
    T = proj.shape[0]
    hb = tm // 8

    def body(xr_ref, halo_ref, gr_ref, cw_ref, cb_ref, wrg_ref, wig_ref, brg_ref, big_ref, lam_ref, g_ref, mix_ref,
             xc_ref, h_ref, out_ref, carry):
        i = pl.program_id(0)

        @pl.when(i == 0)
        def _():
            carry[...] = jnp.zeros_like(carry)

        xr = xr_ref[...]
        halo = jnp.where(i > 0, halo_ref[...], 0.0)
        xc = cb_ref[...] + cw_ref[3:4, :] * xr
        for s in range(1, REC_CONV):
            xc = xc + cw_ref[3 - s:4 - s, :] * _shift_down(xr, halo, s)
        xc_ref[...] = xc
        _, _, ig, _, a, mult = _rec_gates(xc, wrg_ref, wig_ref, brg_ref, big_ref, lam_ref)
        pa, hl = _scan_fwd(a, mult * (ig * xc))
        h = hl + pa * carry[0:1, :]
        h_ref[...] = h
        carry[0:1, :] = h_ref[pl.ds(tm - 1, 1), :]
        hg = h * _gelu(gr_ref[...])
        r = lax.rsqrt(jnp.mean(hg * hg, axis=-1, keepdims=True) + EPS)
        out_ref[...] = (hg * r * g_ref[...]).astype(BF16)

    vec = pl.BlockSpec((1, REC_W), lambda i: (0, 0))
    row = pl.BlockSpec((tm, REC_W), lambda i: (i, 0))
    mat = pl.BlockSpec((REC_W, REC_W), lambda i: (0, 0))
    return _call(
        body, name=name, grid=(T // tm,),
        in_specs=[pl.BlockSpec((tm, REC_W), lambda i: (i, 3)),
                  pl.BlockSpec((8, REC_W), lambda i: (jnp.maximum(i * hb - 1, 0), 3)),
                  pl.BlockSpec((tm, REC_W), lambda i: (i, 4)),
                  pl.BlockSpec((8, REC_W), lambda i: (0, 0)), vec, mat, mat, vec, vec, vec, vec, ANY],
        out_specs=[row, row, pl.BlockSpec((tm, REC_W), lambda i: (i, 1))],
        out_shape=[jax.ShapeDtypeStruct((T, REC_W), F32)] * 2 + [jax.ShapeDtypeStruct(mix.shape, BF16)],
        scratch_shapes=[pltpu.VMEM((8, REC_W), F32)], input_output_aliases={11: 2},
        compiler_params=_params("arbitrary"),
    )(proj, proj, proj, cw, cb, wrg, wig, brg, big, lam, g, mix)


def _rec_bwd(dmix, proj, xc, h, cw, cb, wrg, wig, brg, big, lam, g, name, tm=256):
    T = proj.shape[0]
    nt = T // tm
    hb = tm // 8

    def body(d_ref, xr_ref, xhalo_ref, gr_ref, xc_ref, h_ref, hhalo_ref, cw_ref, cb_ref, wrg_ref, wig_ref, brg_ref,
             big_ref, lam_ref, g_ref,
             drec_ref, gcw_ref, gcb_ref, gwrg_ref, gwig_ref, gbrg_ref, gbig_ref, glam_ref, gg_ref,
             g_carry, a_first, dxc_next, gsp):
        i = pl.program_id(0)
        first_tile = i == nt - 1

        @pl.when(i == 0)
        def _():
            for ref in (gcw_ref, gcb_ref, gwrg_ref, gwig_ref, gbrg_ref, gbig_ref, glam_ref, gg_ref,
                        g_carry, a_first, dxc_next, gsp):
                ref[...] = jnp.zeros_like(ref)

        xr, xc, hv = xr_ref[...], xc_ref[...], h_ref[...]
        xhalo = jnp.where(first_tile, 0.0, xhalo_ref[...])
        hhalo = jnp.where(first_tile, 0.0, hhalo_ref[...])
        xb, r, ig, sp, a, mult = _rec_gates(xc, wrg_ref, wig_ref, brg_ref, big_ref, lam_ref)
        h_prev = _shift_down(hv, hhalo, 1)
        ge, dge = _gelu_and_grad(gr_ref[...])
        hg = hv * ge
        rr = lax.rsqrt(jnp.mean(hg * hg, axis=-1, keepdims=True) + EPS)
        dy = d_ref[...]
        gd = dy * g_ref[...]
        dhg = rr * gd - hg * (rr * rr * rr) * jnp.mean(gd * hg, axis=-1, keepdims=True)
        gg_ref[...] += jnp.sum(dy * hg * rr, axis=0, keepdims=True)
        dgr = (dhg * hv * dge).astype(BF16)
        dh = dhg * ge
        b = _shift_up(a, jnp.broadcast_to(a_first[0:1, :], (8, REC_W)), 1)
        pb, gl = _scan_bwd(b, dh)
        gs = gl + pb * g_carry[0:1, :]
        g_carry[0:1, :] = gs[0:1, :]
        a_first[0:1, :] = a[0:1, :]
        da = gs * h_prev
        dmult = gs * (ig * xc)
        di = gs * (mult * xc)
        dxc = gs * (mult * ig)
        dlog_a = da * a - dmult * (a * a) / mult
        gsp[...] += jnp.sum(dlog_a * (-LRU_C * r), axis=0, keepdims=True)
        dzr = (dlog_a * (-LRU_C * sp)) * (r * (1.0 - r))
        dzi = di * (ig * (1.0 - ig))
        dzr_b, dzi_b = dzr.astype(BF16), dzi.astype(BF16)
        dxc = dxc + _dot(dzr_b, wrg_ref[...], _NT) + _dot(dzi_b, wig_ref[...], _NT)
        gwrg_ref[...] += _dot(xb, dzr_b, _TN)
        gwig_ref[...] += _dot(xb, dzi_b, _TN)
        gbrg_ref[...] += jnp.sum(dzr, axis=0, keepdims=True)
        gbig_ref[...] += jnp.sum(dzi, axis=0, keepdims=True)
        nxt = dxc_next[...]
        dxr = cw_ref[3:4, :] * dxc
        gcw_ref[3:4, :] += jnp.sum(dxc * xr, axis=0, keepdims=True)
        for s in range(1, REC_CONV):
            dxr = dxr + cw_ref[3 - s:4 - s, :] * _shift_up(dxc, nxt, s)
            gcw_ref[3 - s:4 - s, :] += jnp.sum(dxc * _shift_down(xr, xhalo, s), axis=0, keepdims=True)
        gcb_ref[...] += jnp.sum(dxc, axis=0, keepdims=True)
        dxc_next[...] = dxc[:8]
        drec_ref[...] = jnp.concatenate([dxr.astype(BF16), dgr], axis=1)

        @pl.when(first_tile)
        def _():
            glam_ref[...] = gsp[...] * (-_sigmoid(-lam_ref[...]))

    rev = lambda i: nt - 1 - i
    vec = pl.BlockSpec((1, REC_W), lambda i: (0, 0))
    row = pl.BlockSpec((tm, REC_W), lambda i: (rev(i), 0))
    mat = pl.BlockSpec((REC_W, REC_W), lambda i: (0, 0))
    cwb = pl.BlockSpec((8, REC_W), lambda i: (0, 0))
    halo = lambda c: pl.BlockSpec((8, REC_W), lambda i, c=c: (jnp.maximum(rev(i) * hb - 1, 0), c))
    return _call(
        body, name=name, grid=(nt,),
        in_specs=[pl.BlockSpec((tm, REC_W), lambda i: (rev(i), 1)),
                  pl.BlockSpec((tm, REC_W), lambda i: (rev(i), 3)), halo(3),
                  pl.BlockSpec((tm, REC_W), lambda i: (rev(i), 4)),
                  row, row, halo(0), cwb, vec, mat, mat, vec, vec, vec, vec],
        out_specs=[pl.BlockSpec((tm, 2 * REC_W), lambda i: (rev(i), 0)), cwb, vec, mat, mat, vec, vec, vec, vec],
        out_shape=[jax.ShapeDtypeStruct((T, 2 * REC_W), BF16)]
        + [jax.ShapeDtypeStruct((8, REC_W), F32), jax.ShapeDtypeStruct((1, REC_W), F32)]
        + [jax.ShapeDtypeStruct((REC_W, REC_W), F32)] * 2 + [jax.ShapeDtypeStruct((1, REC_W), F32)] * 4,
        scratch_shapes=[pltpu.VMEM((8, REC_W), F32)] * 3 + [pltpu.VMEM((1, REC_W), F32)],
        compiler_params=_params("arbitrary"),
    )(dmix, proj, proj, proj, xc, h, h, cw, cb, wrg, wig, brg, big, lam, g)


def _ffn_conv(x_ext, cw_ref, cb_ref):
    return (cb_ref[...] + cw_ref[2:3, :] * x_ext + cw_ref[1:2, :] * pltpu.roll(x_ext, 1, 0)
            + cw_ref[0:1, :] * pltpu.roll(x_ext, 2, 0))


def _up_proj_act(h2, w_upT, cw, cb, name, tm=1024, tc=768):
    T = h2.shape[0]
    nc = D_FF // tc

    def body(h_ref, wg_ref, wu_ref, cwg_ref, cwu_ref, cbg_ref, cbu_ref, act_ref, da_ref, db_ref, pg_ref, pu_ref,
             hist_g, hist_u):
        i, j = pl.program_id(0), pl.program_id(1)
        hv = h_ref[...]
        pg, pu = _dot(hv, wg_ref[...], _NT), _dot(hv, wu_ref[...], _NT)
        ge = jnp.concatenate([jnp.where(i > 0, hist_g[j], 0.0), pg], axis=0)
        ue = jnp.concatenate([jnp.where(i > 0, hist_u[j], 0.0), pu], axis=0)
        gel, dgel = _gelu_and_grad(_ffn_conv(ge, cwg_ref, cbg_ref)[8:])
        uu = _ffn_conv(ue, cwu_ref, cbu_ref)[8:]
        act_ref[...] = (gel * uu).astype(BF16)
        da_ref[...] = (uu * dgel).astype(BF16)
        db_ref[...] = gel.astype(BF16)
        pg_ref[...] = pg.astype(BF16)
        pu_ref[...] = pu.astype(BF16)
        hist_g[j] = pg[tm - 8:]
        hist_u[j] = pu[tm - 8:]

    tile = pl.BlockSpec((tm, tc), lambda i, j: (i, j))
    wsp = lambda off: pl.BlockSpec((tc, D_MODEL), lambda i, j, off=off: (j + off, 0))
    cws = lambda off: pl.BlockSpec((8, tc), lambda i, j, off=off: (0, j + off))
    cbs = lambda off: pl.BlockSpec((1, tc), lambda i, j, off=off: (0, j + off))
    return _call(
        body, name=name, grid=(T // tm, nc),
        in_specs=[pl.BlockSpec((tm, D_MODEL), lambda i, j: (i, 0)), wsp(0), wsp(nc), cws(0), cws(nc), cbs(0), cbs(nc)],
        out_specs=[tile] * 5, out_shape=[jax.ShapeDtypeStruct((T, D_FF), BF16)] * 5,
        scratch_shapes=[pltpu.VMEM((nc, 8, tc), F32)] * 2, compiler_params=_params("arbitrary", "arbitrary"),
    )(h2, w_upT, w_upT, cw, cw, cb, cb)


def _ffn_bwd(dyb, w_down, da, db, pg, pu, cw, name, tm=512, tc=768):
    T, F = pg.shape
    nt = T // tm
    hb16 = tm // 16
    nc = F // tc
    n = tm + 8

    def body(dy_ref, dyn_ref, wd_ref, a_ref, an_ref, b_ref, bn_ref, g_ref, u_ref, cwg_ref, cwu_ref,
             dg_ref, du_ref, gcwg_ref, gcwu_ref, gcbg_ref, gcbu_ref):
        i = pl.program_id(1)
        last = i == nt - 1

        @pl.when(i == 0)
        def _():
            for ref in (gcwg_ref, gcwu_ref, gcbg_ref, gcbu_ref):
                ref[...] = jnp.zeros_like(ref)

        wd = wd_ref[...]
        dact_next = jnp.where(last, 0.0, _dot(dyn_ref[...], wd, _NT)[:8])
        de = jnp.concatenate([_dot(dy_ref[...], wd, _NT), dact_next], axis=0)
        ext = lambda t, nx: jnp.concatenate([t[...].astype(F32), nx[...].astype(F32)[:8]], axis=0)
        for dcv, x_ref, cw_ref, dx_ref, gcw_ref, gcb_ref in ((de * ext(a_ref, an_ref), g_ref, cwg_ref, dg_ref, gcwg_ref, gcbg_ref),
                                                               (de * ext(b_ref, bn_ref), u_ref, cwu_ref, du_ref, gcwu_ref, gcbu_ref)):
            s1, s2 = pltpu.roll(dcv, n - 1, 0), pltpu.roll(dcv, n - 2, 0)
            dx_ref[...] = (cw_ref[2:3, :] * dcv + cw_ref[1:2, :] * s1 + cw_ref[0:1, :] * s2)[:tm].astype(BF16)
            xv = x_ref[...].astype(F32)
            gcw_ref[2:3, :] += jnp.sum(xv * dcv[:tm], axis=0, keepdims=True)
            gcw_ref[1:2, :] += jnp.sum(xv * s1[:tm], axis=0, keepdims=True)
            gcw_ref[0:1, :] += jnp.sum(xv * s2[:tm], axis=0, keepdims=True)
            gcb_ref[...] += jnp.sum(dcv[:tm], axis=0, keepdims=True)

    tile = pl.BlockSpec((tm, tc), lambda j, i: (i, j))
    nxt = pl.BlockSpec((16, tc), lambda j, i: (jnp.minimum((i + 1) * hb16, nt * hb16 - 1), j))
    cws = lambda off: pl.BlockSpec((8, tc), lambda j, i, off=off: (0, j + off))
    cbs = pl.BlockSpec((1, tc), lambda j, i: (0, j))
    return _call(
        body, name=name, grid=(nc, nt),
        in_specs=[pl.BlockSpec((tm, D_MODEL), lambda j, i: (i, 0)),
                  pl.BlockSpec((16, D_MODEL), lambda j, i: (jnp.minimum((i + 1) * hb16, nt * hb16 - 1), 0)),
                  pl.BlockSpec((tc, D_MODEL), lambda j, i: (j, 0)), tile, nxt, tile, nxt, tile, tile, cws(0), cws(nc)],
        out_specs=[tile, tile, cws(0), cws(0), cbs, cbs],
        out_shape=[jax.ShapeDtypeStruct((T, F), BF16)] * 2 + [jax.ShapeDtypeStruct((8, F), F32)] * 2
        + [jax.ShapeDtypeStruct((1, F), F32)] * 2,
        compiler_params=_params("parallel", "arbitrary"),
    )(dyb, dyb, w_down, da, da, db, db, pg, pu, cw, cw)


def _adam_update(w, g, m, v):
    m2 = ADAM_B1 * m + (1.0 - ADAM_B1) * g
    v2 = ADAM_B2 * v + (1.0 - ADAM_B2) * (g * g)
    m_hat = m2 / (1.0 - ADAM_B1 ** ADAM_STEP)
    v_hat = v2 / (1.0 - ADAM_B2 ** ADAM_STEP)
    delta = -ADAM_LR * (m_hat / (jnp.sqrt(v_hat) + ADAM_EPS) + ADAM_WD * w)
    return delta, m2, v2


def _adam_sharded(p, r2, idx, w, m, v, name, transposed=False):
    r, n = p.shape[1:]
    nrecv = r2.shape[0]
    tr = (256 if r % 256 == 0 else r) if transposed else _row_tile(r)

    def body(c_ref, p_ref, r_ref, w_ref, m_ref, v_ref, g_ref, d_ref, m2_ref, v2_ref):
        g = p_ref[...].astype(F32)
        for k in range(nrecv):
            g = g + r_ref[k].astype(F32)
        if transposed:
            g = g.T
        g_ref[...] = g
        d_ref[...], m2_ref[...], v2_ref[...] = _adam_update(w_ref[...], g, m_ref[...], v_ref[...])

    blk = pl.BlockSpec((n, tr), lambda i, c_ref: (0, i)) if transposed else pl.BlockSpec((tr, n), lambda i, c_ref: (i, 0))
    spec = pltpu.PrefetchScalarGridSpec(
        num_scalar_prefetch=1, grid=(r // tr,),
        in_specs=[pl.BlockSpec((None, tr, n), lambda i, c_ref: (c_ref[0], i, 0)),
                  pl.BlockSpec((nrecv, tr, n), lambda i, c_ref: (0, i, 0)), blk, blk, blk],
        out_specs=[blk] * 4)
    return _call(body, name=name, grid_spec=spec, out_shape=[jax.ShapeDtypeStruct(w.shape, F32)] * 4,
                 compiler_params=_params("parallel"))(idx, p, r2, w, m, v)


def _sum_devices(allg, name):
    r, n = allg.shape[0] // N_DEV, allg.shape[1]

    def body(a_ref, o_ref):
        acc = a_ref[0:r, :]
        for k in range(1, N_DEV):
            acc = acc + a_ref[k * r:(k + 1) * r, :]
        o_ref[...] = acc

    return _call(body, name=name, out_shape=jax.ShapeDtypeStruct((r, n), F32))(allg)


def _adam_small(ws, gs, ms, vs, name):
    n = len(ws)

    def body(*refs):
        for i in range(n):
            d, m2, v2 = _adam_update(refs[i][...], refs[n + i][...], refs[2 * n + i][...], refs[3 * n + i][...])
            refs[4 * n + i][...] = d
            refs[5 * n + i][...] = m2
            refs[6 * n + i][...] = v2

    outs = _call(body, name=name, out_shape=[jax.ShapeDtypeStruct(w.shape, F32) for w in ws] * 3)(*ws, *gs, *ms, *vs)
    return outs[:n], outs[n:2 * n], outs[2 * n:]


_SMALL = (("g_mix", 1024), ("q_norm_g", 64), ("k_norm_g", 64), ("rec_conv_b", 512), ("w_rg", 32768), ("b_rg", 512),
          ("w_ig", 32768), ("b_ig", 512), ("lru_lambda", 512), ("g_attn_out", 512), ("g_rec_out", 512),
          ("g_ffn", 1024), ("ffn_conv_b", 6144))
_SMALL_SHAPES = {"g_mix": (1, 1024), "q_norm_g": (1, 64), "k_norm_g": (1, 64), "rec_conv_b": (1, 512),
                 "w_rg": (1, 8, 64, 64), "b_rg": (1, 8, 64), "w_ig": (1, 8, 64, 64), "b_ig": (1, 8, 64),
                 "lru_lambda": (1, 512), "g_attn_out": (1, 512), "g_rec_out": (1, 512), "g_ffn": (1, 1024),
                 "ffn_conv_b": (1, 6144)}


def _block_diag(w):
    eye = jnp.eye(8, dtype=w.dtype)
    return (w[:, :, None, :] * eye[:, None, :, None]).reshape(512, 512)


def kernel(x, positions, g_mix, w_in, q_norm_g, k_norm_g, rec_conv_w, rec_conv_b, w_rg, b_rg, w_ig, b_ig, lru_lambda, g_attn_out, g_rec_out, w_out, g_ffn, w_up, ffn_conv_w, ffn_conv_b, w_down, loss_target, m_g_mix, m_w_in, m_q_norm_g, m_k_norm_g, m_rec_conv_w, m_rec_conv_b, m_w_rg, m_b_rg, m_w_ig, m_b_ig, m_lru_lambda, m_g_attn_out, m_g_rec_out, m_w_out, m_g_ffn, m_w_up, m_ffn_conv_w, m_ffn_conv_b, m_w_down, v_g_mix, v_w_in, v_q_norm_g, v_k_norm_g, v_rec_conv_w, v_rec_conv_b, v_w_rg, v_b_rg, v_w_ig, v_b_ig, v_lru_lambda, v_g_attn_out, v_g_rec_out, v_w_out, v_g_ffn, v_w_up, v_ffn_conv_w, v_ffn_conv_b, v_w_down):
    T = x.shape[1]
    ix, iy, ic = lax.axis_index("x"), lax.axis_index("y"), lax.axis_index("c")
    dev = 4 * ix + 2 * iy + ic
    xs = x.reshape(T, D_MODEL)
    tgt = loss_target.reshape(T, D_MODEL)
    pos = positions.reshape(T, 1)

    shards = {"w_in": (w_in[0], m_w_in[0], v_w_in[0]), "w_out": (w_out[0], m_w_out[0], v_w_out[0]),
              "w_up": (w_up[0], m_w_up[0], v_w_up[0]), "w_down": (w_down[0], m_w_down[0], v_w_down[0])}
    taps = jnp.concatenate([rec_conv_w.reshape(-1), ffn_conv_w.reshape(-1), jnp.zeros((4096 - 2560,), F32)]).reshape(8, 512)
    W_inT, taps_all = _all_gather([w_in[0].T.astype(BF16), taps], "ag_w_in")
    late = [w_out[0].astype(BF16), w_up[0].T.astype(BF16), w_down[0].astype(BF16)]
    ag_send, ag_recv, late_thru, land_thru, ag_token = _exchange_start(
        late, [_landing((N_DEV * s.shape[0], 1024), BF16, s, dev * s.shape[0]) for s in late], "gather", taps_all,
        "ag_late_start")
    taps_all = taps_all.reshape(N_DEV, 4096)
    rcw = taps_all[:, :256].reshape(8, 4, 64).transpose(1, 0, 2).reshape(4, REC_W)
    fcw = taps_all[:, 256:2560].reshape(8, 3, 768).transpose(1, 0, 2).reshape(3, 2 * D_FF)
    rcw8 = jnp.pad(rcw, ((0, 4), (0, 0)))
    fcw8 = jnp.pad(fcw, ((0, 5), (0, 0)))
    fcb = ffn_conv_b.reshape(1, 2 * D_FF)

    half = HEAD_DIM // 2
    inv_freq = ROPE_THETA ** (-jnp.arange(half, dtype=F32) / half)
    invf = jnp.tile(inv_freq, 2 * N_HEADS).reshape(1, ATTN_W)
    bd = jnp.asarray(np.kron(np.eye(2), np.full((HEAD_DIM, HEAD_DIM), 1.0 / HEAD_DIM)), BF16)
    qg = jnp.tile(q_norm_g.reshape(HEAD_DIM), N_HEADS).reshape(1, ATTN_W)
    kg = jnp.tile(k_norm_g.reshape(HEAD_DIM), N_HEADS).reshape(1, ATTN_W)
    wrg_bd = _block_diag(w_rg[0]).astype(BF16)
    wig_bd = _block_diag(w_ig[0]).astype(BF16)
    brg, big = b_rg.reshape(1, REC_W), b_ig.reshape(1, REC_W)

    h1 = _rmsnorm(xs, g_mix + ag_token[0, 0], "norm_mix")
    proj = _mm(h1, W_inT, "nt", F32, "in_proj", tn=1280)
    qf, kf = _qk_prep(proj, pos, invf, qg, kg, bd, "qk_prep")
    attn, lse = _attn_fwd(qf, kf, proj, "attn_fwd")
    mix = _attn_norm(attn, g_attn_out, "attn_norm")
    xc, hstate, mix = _rec_fwd(proj, mix, rcw8, rec_conv_b, wrg_bd, wig_bd, brg, big, lru_lambda, g_rec_out, "rec_fwd")
    _, (W_out, W_upT, W_down) = _exchange_wait(ag_send, ag_recv, late_thru, land_thru, "gather", hstate, "ag_late_wait")
    x2 = _mm(mix, W_out, "nn", F32, "out_proj", add=xs)

    h2 = _rmsnorm(x2, g_ffn, "norm_ffn")
    act, da, db, pg, pu = _up_proj_act(h2, W_upT, fcw8, fcb, "up_proj_act")
    dy, dyb, lparts = _mm(act, W_down, "nn", F32, "down_proj_loss", add=x2, loss_target=tgt, tm=512, tk=D_FF)
    loss_mine = 0.5 / D_MODEL * jnp.sum(lparts)

    g_down = _mm(act, dyb, "tn", BF16, "g_w_down", tk=2048)
    dpg, dpu, g_fcwg, g_fcwu, g_fcbg, g_fcbu = _ffn_bwd(dyb, W_down, da, db, pg, pu, fcw8, "ffn_bwd")
    g_upT = _mm(dpg, h2, "tn", BF16, "g_w_up_gate", tk=2048, o_rows=2 * D_FF)
    g_upT = _mm(dpu, h2, "tn", BF16, "g_w_up_up", tk=2048, into=g_upT, o_moff=D_FF // 1024)
    ffn_g = [g_upT.reshape(N_DEV, 2 * D_FF // N_DEV, 1024), g_down.reshape(N_DEV, D_FF // N_DEV, 1024)]
    rs_send, rs_recv, ffn_g, ffn_land, rs_token = _exchange_start(
        ffn_g, [_landing((N_PEERS,) + g.shape[1:], BF16) for g in ffn_g], "scatter", dpu, "rs_ffn_start")
    dx2, dx2b, g_gffn = _mm_norm_bwd([dpg, dpu], W_upT, x2, dy, g_ffn + rs_token[0, 0], "d_h2_norm_bwd", tk=1024)

    dmix = _mm(dx2b, W_out, "nt", F32, "d_mix")
    g_out = _mm(mix, dx2b, "tn", BF16, "g_w_out", tk=2048).reshape(N_DEV, D_MODEL // N_DEV, 1024)
    out_send, out_recv, (g_out,), out_land, out_token = _exchange_start(
        [g_out], [_landing((N_PEERS,) + g_out.shape[1:], BF16)], "scatter", dmix, "rs_out_start")
    do, delta, g_gattn = _attn_norm_bwd(dmix, attn, g_attn_out + out_token[0, 0], bd, "attn_norm_bwd")
    dqh, dkh, dv = _attn_bwd(qf, kf, proj, do, lse, delta, "attn_bwd")
    dqkv, g_qg, g_kg = _qk_prep_bwd(proj, dqh, dkh, dv, pos, invf, qg, kg, bd, "qk_prep_bwd")
    (drec, g_rcw, g_rcb, g_wrg, g_wig, g_brg, g_big, g_lam, g_grec) = _rec_bwd(
        dmix, proj, xc, hstate, rcw8, rec_conv_b, wrg_bd, wig_bd, brg, big, lru_lambda, g_rec_out, "rec_bwd")
    g_inT = _mm(dqkv, h1, "tn", BF16, "g_w_in_qkv", tm=512, o_rows=IN_W)
    g_inT = _mm(drec, h1, "tn", BF16, "g_w_in_rec", tm=512, into=g_inT, o_moff=3 * ATTN_W // 512)
    g_inT = g_inT.reshape(N_DEV, IN_W // N_DEV, 1024)
    in_send, in_recv, (g_inT,), in_land, in_token = _exchange_start(
        [g_inT], [_landing((N_PEERS,) + g_inT.shape[1:], BF16)], "scatter", drec, "rs_in_start")
    grad_x, _, g_gmix = _mm_norm_bwd([dqkv, drec], W_inT, xs, dx2, g_mix + in_token[0, 0], "d_h1_norm_bwd", tk=512)

    blocks = lambda g: jnp.stack([g[64 * n:64 * n + 64, 64 * n:64 * n + 64] for n in range(8)])
    small_g = {
        "g_mix": g_gmix, "q_norm_g": g_qg.reshape(N_HEADS, HEAD_DIM).sum(0), "k_norm_g": g_kg.reshape(N_HEADS, HEAD_DIM).sum(0),
        "rec_conv_b": g_rcb, "w_rg": blocks(g_wrg), "b_rg": g_brg, "w_ig": blocks(g_wig), "b_ig": g_big,
        "lru_lambda": g_lam, "g_attn_out": g_gattn, "g_rec_out": g_grec, "g_ffn": g_gffn,
        "ffn_conv_b": jnp.concatenate([g_fcbg, g_fcbu], axis=1)}
    g_fcw = jnp.concatenate([g_fcwg[:3], g_fcwu[:3]], axis=1)
    flat = jnp.concatenate([small_g[k].reshape(-1) for k, _ in _SMALL]
                           + [g_rcw[:4].reshape(-1), g_fcw.reshape(-1), loss_mine.reshape(1)])
    flat = jnp.pad(flat, (0, SMALL_ROWS * 1024 - flat.shape[0])).reshape(SMALL_ROWS, 1024)
    tot = _sum_devices(_all_gather([flat], "ag_small_grads")[0], "sum_small_grads").reshape(-1)
    g_small, o = {}, 0
    for k, n in _SMALL:
        g_small[k] = tot[o:o + n].reshape(_SMALL_SHAPES[k])
        o += n
    g_small["rec_conv_w"] = lax.dynamic_slice(tot[o:o + 2048].reshape(1, 4, REC_W), (0, 0, 64 * dev), (1, 4, 64))
    g_small["ffn_conv_w"] = lax.dynamic_slice(tot[o + 2048:o + 2048 + 18432].reshape(1, 3, 2 * D_FF), (0, 0, 768 * dev), (1, 3, 768))
    loss = tot[o + 2048 + 18432]

    devi = jnp.reshape(dev, (1,)).astype(jnp.int32)
    ffn_g, ffn_land = _exchange_wait(rs_send, rs_recv, ffn_g, ffn_land, "scatter", tot, "rs_ffn_wait")
    (g_out,), out_land = _exchange_wait(out_send, out_recv, [g_out], out_land, "scatter", tot, "rs_out_wait")
    (g_inT,), in_land = _exchange_wait(in_send, in_recv, [g_inT], in_land, "scatter", tot, "rs_in_wait")
    big_out = {"grad": {}, "delta": {}, "new_m": {}, "new_v": {}}
    for nm, p, r in (("w_up", ffn_g[0], ffn_land[0]), ("w_down", ffn_g[1], ffn_land[1]), ("w_out", g_out, out_land[0]),
                     ("w_in", g_inT, in_land[0])):
        w_, m_, v_ = shards[nm]
        res = _adam_sharded(p, r, devi, w_, m_, v_, "adam_" + nm, transposed=nm in ("w_in", "w_up"))
        for kind, a in zip(("grad", "delta", "new_m", "new_v"), res):
            big_out[kind][nm] = a[None]
    given = dict(rec_conv_w=rec_conv_w, ffn_conv_w=ffn_conv_w,g_mix=g_mix, q_norm_g=q_norm_g, k_norm_g=k_norm_g, rec_conv_b=rec_conv_b, w_rg=w_rg, b_rg=b_rg, w_ig=w_ig,
                 b_ig=b_ig, lru_lambda=lru_lambda, g_attn_out=g_attn_out, g_rec_out=g_rec_out, g_ffn=g_ffn, ffn_conv_b=ffn_conv_b)
    given_m = dict(rec_conv_w=m_rec_conv_w, ffn_conv_w=m_ffn_conv_w, g_mix=m_g_mix, q_norm_g=m_q_norm_g, k_norm_g=m_k_norm_g, rec_conv_b=m_rec_conv_b, w_rg=m_w_rg, b_rg=m_b_rg,
                   w_ig=m_w_ig, b_ig=m_b_ig, lru_lambda=m_lru_lambda, g_attn_out=m_g_attn_out, g_rec_out=m_g_rec_out,
                   g_ffn=m_g_ffn, ffn_conv_b=m_ffn_conv_b)
    given_v = dict(rec_conv_w=v_rec_conv_w, ffn_conv_w=v_ffn_conv_w, g_mix=v_g_mix, q_norm_g=v_q_norm_g, k_norm_g=v_k_norm_g, rec_conv_b=v_rec_conv_b, w_rg=v_w_rg, b_rg=v_b_rg,
                   w_ig=v_w_ig, b_ig=v_b_ig, lru_lambda=v_lru_lambda, g_attn_out=v_g_attn_out, g_rec_out=v_g_rec_out,
                   g_ffn=v_g_ffn, ffn_conv_b=v_ffn_conv_b)
    small = sorted(given)
    ds, m2s, v2s = _adam_small([given[k] for k in small], [g_small[k] for k in small], [given_m[k] for k in small],
                               [given_v[k] for k in small], "adam_small")
    small_out = {"grad": g_small, "delta": dict(zip(small, ds)), "new_m": dict(zip(small, m2s)), "new_v": dict(zip(small, v2s))}

    order = ("g_mix", "w_in", "q_norm_g", "k_norm_g", "rec_conv_w", "rec_conv_b", "w_rg", "b_rg", "w_ig", "b_ig",
             "lru_lambda", "g_attn_out", "g_rec_out", "w_out", "g_ffn", "w_up", "ffn_conv_w", "ffn_conv_b", "w_down")
    outs = [loss, grad_x.reshape(1, T, D_MODEL)]
    for kind in ("grad", "delta", "new_m", "new_v"):
        for name in order:
            outs.append(big_out[kind][name] if name in big_out[kind] else small_out[kind][name])
    return tuple(outs)
```

```python
import math

import numpy as np
import jax
import jax.numpy as jnp
from jax import lax
from jax.experimental import pallas as pl
from jax.experimental.pallas import tpu as pltpu

F32 = jnp.float32
BF16 = jnp.bfloat16

D_MODEL = 1024
HEAD_DIM = 64
ATTN_W = 512
REC_W = 512
N_HEADS = 8
D_FF = 3072
IN_W = 2560
REC_CONV = 4
FFN_CONV = 3
LRU_C = 8.0
ROPE_THETA = 10000.0
EPS = 1e-6
NEG_INF = -1e30
QBLK = 128
DILATIONS = (1, 4, 16)
N_DEV = 8
SMALL_ROWS = 96
ADAM_LR, ADAM_B1, ADAM_B2, ADAM_EPS, ADAM_WD, ADAM_STEP = 0.001, 0.9, 0.999, 1e-08, 0.01, 10
MESH = pl.DeviceIdType.MESH
ANY = pl.BlockSpec(memory_space=pl.ANY)


def _call(body, *, name, **kw):
    return pl.pallas_call(body, name=name, **kw)


def _params(*sem):
    return pltpu.CompilerParams(dimension_semantics=sem, vmem_limit_bytes=56 * 1024 * 1024)


def _gelu(x):
    c = math.sqrt(2.0 / math.pi)
    return 0.5 * x * (1.0 + jnp.tanh(c * (x + 0.044715 * (x * x * x))))


def _gelu_and_grad(x):
    c = math.sqrt(2.0 / math.pi)
    t = jnp.tanh(c * (x + 0.044715 * (x * x * x)))
    g = 0.5 * x * (1.0 + t)
    dg = 0.5 * (1.0 + t) + 0.5 * x * (1.0 - t * t) * (c * (1.0 + 3.0 * 0.044715 * (x * x)))
    return g, dg


def _sigmoid(x):
    return 1.0 / (1.0 + jnp.exp(-x))


def _softplus_neg(lam):
    y = jnp.exp(-jnp.abs(lam))
    u = 1.0 + y
    log1p = jnp.where(u == 1.0, y, jnp.log(u) * y / jnp.where(u == 1.0, 1.0, u - 1.0))
    return jnp.maximum(-lam, 0.0) + log1p


_NN = (((1,), (0,)), ((), ()))
_NT = (((1,), (1,)), ((), ()))
_TN = (((0,), (0,)), ((), ()))


def _dot(a, b, dims=_NN):
    return lax.dot_general(a, b, dims, preferred_element_type=F32)


def _group_mean(v, bd):
    hi = v.astype(BF16)
    lo = (v - hi.astype(F32)).astype(BF16)
    w = bd.shape[0]
    return jnp.concatenate([_dot(hi[:, c:c + w], bd) + _dot(lo[:, c:c + w], bd) for c in range(0, v.shape[1], w)], axis=1)


def _rope_tables(pos_ref, invf_ref):
    ang = pos_ref[...].astype(F32) * invf_ref[:, :2 * HEAD_DIM]
    reps = invf_ref.shape[1] // (2 * HEAD_DIM)
    return jnp.tile(jnp.cos(ang), (1, reps)), jnp.tile(jnp.sin(ang), (1, reps))


def _shift_down(x, halo, s):
    rolled = pltpu.roll(x, s, 0)
    hr = pltpu.roll(halo, s, 0)
    row = lax.broadcasted_iota(jnp.int32, hr.shape, 0)
    first = jnp.where(row < s, hr, rolled[:8])
    return jnp.concatenate([first, rolled[8:]], axis=0)


def _shift_up(x, halo, s):
    n = x.shape[0]
    rolled = pltpu.roll(x, n - s, 0)
    hr = pltpu.roll(halo, 8 - s, 0)
    row = lax.broadcasted_iota(jnp.int32, hr.shape, 0)
    last = jnp.where(row >= 8 - s, hr, rolled[n - 8:])
    return jnp.concatenate([rolled[:n - 8], last], axis=0)


def _scan_fwd(a, u):
    n, w = a.shape
    a3, u3 = a.reshape(n // 8, 8, w), u.reshape(n // 8, 8, w)
    row = lax.broadcasted_iota(jnp.int32, a3.shape, 1)
    for s in (1, 2, 4):
        a_s = jnp.where(row < s, 1.0, pltpu.roll(a3, s, 1))
        u_s = jnp.where(row < s, 0.0, pltpu.roll(u3, s, 1))
        u3 = u3 + a3 * u_s
        a3 = a3 * a_s
    ps, hs = [a3[0]], [u3[0]]
    for k in range(1, n // 8):
        ps.append(a3[k] * ps[-1][7:8, :])
        hs.append(u3[k] + a3[k] * hs[-1][7:8, :])
    return jnp.concatenate(ps, axis=0), jnp.concatenate(hs, axis=0)


def _scan_bwd(b, v):
    n, w = b.shape
    b3, v3 = b.reshape(n // 8, 8, w), v.reshape(n // 8, 8, w)
    row = lax.broadcasted_iota(jnp.int32, b3.shape, 1)
    for s in (1, 2, 4):
        b_s = jnp.where(row >= 8 - s, 1.0, pltpu.roll(b3, 8 - s, 1))
        v_s = jnp.where(row >= 8 - s, 0.0, pltpu.roll(v3, 8 - s, 1))
        v3 = v3 + b3 * v_s
        b3 = b3 * b_s
    last = n // 8 - 1
    ps, gs = [b3[last]], [v3[last]]
    for k in range(last - 1, -1, -1):
        ps.append(b3[k] * ps[-1][0:1, :])
        gs.append(v3[k] + b3[k] * gs[-1][0:1, :])
    return jnp.concatenate(ps[::-1], axis=0), jnp.concatenate(gs[::-1], axis=0)


def _rot_half(y):
    n = y.shape[1]
    lane = lax.broadcasted_iota(jnp.int32, y.shape, 1) & (HEAD_DIM - 1)
    return jnp.where(lane < HEAD_DIM // 2, -pltpu.roll(y, n - HEAD_DIM // 2, 1), pltpu.roll(y, HEAD_DIM // 2, 1))


def _row_tile(r, cap=256):
    return max(t for t in range(16, cap + 1, 16) if r % t == 0)


def _all_gather(shards, name):
    na = len(shards)
    ms = [s.shape[0] for s in shards]

    def body(*refs):
        x_refs, out_refs = refs[:na], refs[na:2 * na]
        send_sems, recv_sems, local_sems = refs[2 * na:]
        x, y, c = lax.axis_index("x"), lax.axis_index("y"), lax.axis_index("c")
        me, sibling = (x, y, c), (x, y, 1 - c)
        chips = [(1 - x, y), (x, 1 - y), (1 - x, 1 - y)]

        def rows(a, px, py, pc):
            return out_refs[a].at[pl.ds((4 * px + 2 * py + pc) * ms[a], ms[a]), :]

        def copy(a, k, block, to, src=None):
            return pltpu.make_async_remote_copy(
                src_ref=rows(a, *block) if src is None else src, dst_ref=rows(a, *block),
                send_sem=send_sems.at[7 * a + k], recv_sem=recv_sems.at[7 * a + k], device_id=to, device_id_type=MESH)

        mine = [pltpu.make_async_copy(x_refs[a], rows(a, *me), local_sems.at[a]) for a in range(na)]
        first = []
        for a in range(na):
            mine[a].start()
            first.append(copy(a, 0, me, sibling, src=x_refs[a]))
            first += [copy(a, 1 + j, me, (*chip, c), src=x_refs[a]) for j, chip in enumerate(chips)]
        for cp in first:
            cp.start()
        passed = []
        for a in range(na):
            for j, chip in enumerate(chips):
                copy(a, 1 + j, (*chip, c), me).wait_recv()
                fw = copy(a, 4 + j, (*chip, c), sibling)
                fw.start()
                passed.append(fw)
        for a in range(na):
            copy(a, 0, sibling, me).wait_recv()
            for j, chip in enumerate(chips):
                copy(a, 4 + j, (*chip, 1 - c), me).wait_recv()
        for cp in first + passed:
            cp.wait_send()
        for cp in mine:
            cp.wait()

    return _call(
        body, name=name, out_shape=[jax.ShapeDtypeStruct((N_DEV * s.shape[0], s.shape[1]), s.dtype) for s in shards],
        in_specs=[ANY] * na, out_specs=[ANY] * na,
        scratch_shapes=[pltpu.SemaphoreType.DMA((7 * na,)), pltpu.SemaphoreType.DMA((7 * na,)),
                        pltpu.SemaphoreType.DMA((na,))],
    )(*shards)


HBM = pl.BlockSpec(memory_space=pltpu.HBM)
SEM = pl.BlockSpec(memory_space=pltpu.SEMAPHORE)
EFFECT = pltpu.SideEffectType.DATAFLOW_SIDE_EFFECTING
N_PEERS = N_DEV - 1


def _peer(k):
    x, y, c = lax.axis_index("x"), lax.axis_index("y"), lax.axis_index("c")
    b = k + 1
    flip = lambda v, bit: 1 - v if bit else v
    return flip(x, b & 4), flip(y, b & 2), flip(c, b & 1)


def _in_hbm(a):
    return pltpu.with_memory_space_constraint(a, pltpu.HBM)


def _split_copy_descr(na, kind, src_refs, land_refs, send_sems, recv_sems):
    x, y, c = lax.axis_index("x"), lax.axis_index("y"), lax.axis_index("c")
    me = 4 * x + 2 * y + c
    copies = []
    for a in range(na):
        for k in range(N_PEERS):
            px, py, pc = _peer(k)
            if kind == "gather":
                m = src_refs[a].shape[0]
                src, dst = src_refs[a], land_refs[a].at[pl.ds(me * m, m), :]
            else:
                src, dst = src_refs[a].at[4 * px + 2 * py + pc], land_refs[a].at[k]
            copies.append(pltpu.make_async_remote_copy(
                src_ref=src, dst_ref=dst, send_sem=send_sems.at[N_PEERS * a + k], recv_sem=recv_sems.at[N_PEERS * a + k],
                device_id=(px, py, pc), device_id_type=MESH))
    return copies


def _gather_landing(shards, name):
    na = len(shards)

    def body(*refs):
        me = 4 * lax.axis_index("x") + 2 * lax.axis_index("y") + lax.axis_index("c")
        copies = [pltpu.make_async_copy(refs[a], refs[na + a].at[pl.ds(me * refs[a].shape[0], refs[a].shape[0]), :],
                                        refs[2 * na].at[a]) for a in range(na)]
        for cp in copies:
            cp.start()
        for cp in copies:
            cp.wait()

    return _call(body, name=name, in_specs=[ANY] * na, out_specs=[ANY] * na,
                 out_shape=[jax.ShapeDtypeStruct((N_DEV * s.shape[0], s.shape[1]), s.dtype) for s in shards],
                 scratch_shapes=[pltpu.SemaphoreType.DMA((na,))])(*shards)


def _landing(shape, dtype):
    return lax.empty(shape, dtype)


def _exchange_start(srcs, lands, kind, after, name):
    na = len(srcs)
    land_shapes = [l.shape for l in lands]

    def body(*refs):
        src_refs, land_refs = refs[:na], refs[na:2 * na]
        send_sems, recv_sems = refs[2 * na + 1], refs[2 * na + 2]
        token = refs[-1]
        for cp in _split_copy_descr(na, kind, src_refs, land_refs, send_sems, recv_sems):
            cp.start()
        token[...] = jnp.zeros_like(token)

    lands = [_in_hbm(l) for l in lands]
    sem = pltpu.SemaphoreType.DMA((N_PEERS * na,))
    outs = _call(
        body, name=name,
        out_shape=[sem, sem] + [pltpu.HBM(s.shape, s.dtype) for s in srcs] + [pltpu.HBM(s, srcs[0].dtype) for s in land_shapes]
        + [jax.ShapeDtypeStruct((8, 128), F32)],
        in_specs=[HBM] * (2 * na) + [ANY], out_specs=[SEM, SEM] + [HBM] * (2 * na) + [pl.BlockSpec(memory_space=pltpu.VMEM)],
        input_output_aliases={i: 2 + i for i in range(2 * na)},
        compiler_params=pltpu.CompilerParams(has_side_effects=EFFECT),
    )(*[_in_hbm(s) for s in srcs], *lands, after)
    return outs[0], outs[1], outs[2:2 + na], outs[2 + na:2 + 2 * na], outs[-1]


def _exchange_wait(send_sems, recv_sems, srcs, lands, kind, after, name):
    na = len(srcs)

    def body(*refs):
        src_refs, land_refs = refs[:na], refs[na:2 * na]
        s_sems, r_sems = refs[2 * na], refs[2 * na + 1]
        for cp in _split_copy_descr(na, kind, src_refs, land_refs, s_sems, r_sems):
            cp.wait_send()
            cp.wait_recv()

    outs = _call(
        body, name=name, out_shape=[pltpu.HBM(s.shape, s.dtype) for s in srcs] + [pltpu.HBM(l.shape, l.dtype) for l in lands],
        in_specs=[HBM] * (2 * na) + [SEM, SEM, ANY], out_specs=[HBM] * (2 * na),
        input_output_aliases={i: i for i in range(2 * na)},
        compiler_params=pltpu.CompilerParams(has_side_effects=EFFECT),
    )(*srcs, *lands, send_sems, recv_sems, after)
    return outs[:na], outs[na:]


def _mm(a, b, mode, out_dtype, name, add=None, tm=1024, tn=1024, tk=1024, b_noff=0, b_koff=0,
        n=None, k=None, into=None, o_rows=None, o_moff=0, loss_target=None):
    if mode == "tn":
        K, M = a.shape
    else:
        M, K = a.shape
    N = n if n is not None else (b.shape[0] if mode == "nt" else b.shape[1])
    if k is not None:
        assert k == K
    tm, tn, tk = min(tm, M), min(tn, N), min(tk, K)
    assert M % tm == 0 and N % tn == 0 and K % tk == 0, (name, M, N, K)
    nk = K // tk
    if mode == "nn":
        a_spec = pl.BlockSpec((tm, tk), lambda i, j, kk: (i, kk))
        b_spec, dims = pl.BlockSpec((tk, tn), lambda i, j, kk: (kk + b_koff, j + b_noff)), _NN
    elif mode == "nt":
        a_spec = pl.BlockSpec((tm, tk), lambda i, j, kk: (i, kk))
        b_spec, dims = pl.BlockSpec((tn, tk), lambda i, j, kk: (j + b_noff, kk + b_koff)), _NT
    else:
        a_spec = pl.BlockSpec((tk, tm), lambda i, j, kk: (kk, i))
        b_spec, dims = pl.BlockSpec((tk, tn), lambda i, j, kk: (kk + b_koff, j + b_noff)), _TN
    o_spec = pl.BlockSpec((tm, tn), lambda i, j, kk: (i + o_moff, j))
    has_add, has_into, has_loss = add is not None, into is not None, loss_target is not None
    assert not has_loss or (has_add and tn == N and not has_into)
    n_in = 2 + has_add + has_loss + has_into

    def body(*refs):
        a_ref, b_ref = refs[0], refs[1]
        add_ref = refs[2] if has_add else None
        outs = refs[n_in:]

        def finish(r):
            if has_add:
                r = r + add_ref[...]
            if has_loss:
                e = r - refs[3][...]
                dy = e * (1.0 / N)
                outs[0][...] = dy
                outs[1][...] = dy.astype(BF16)
                outs[2][...] = jnp.sum(e * e, axis=0, keepdims=True)[None]
            else:
                outs[0][...] = r.astype(out_dtype)

        if nk == 1:
            finish(_dot(a_ref[...], b_ref[...], dims))
        else:
            acc = refs[-1]
            kk = pl.program_id(2)

            @pl.when(kk == 0)
            def _():
                acc[...] = _dot(a_ref[...], b_ref[...], dims)

            @pl.when((kk > 0) & (kk < nk - 1))
            def _():
                acc[...] += _dot(a_ref[...], b_ref[...], dims)

            @pl.when(kk == nk - 1)
            def _():
                finish(acc[...] + _dot(a_ref[...], b_ref[...], dims))

    tile = pl.BlockSpec((tm, tn), lambda i, j, kk: (i, j))
    ins = [a, b] + ([add] if has_add else []) + ([loss_target] if has_loss else []) + ([into] if has_into else [])
    specs = [a_spec, b_spec] + [tile] * (has_add + has_loss) + ([ANY] if has_into else [])
    rows = into.shape[0] if has_into else (o_rows if o_rows is not None else M)
    if has_loss:
        out_specs = [tile, tile, pl.BlockSpec((1, 1, N), lambda i, j, kk: (i, 0, 0))]
        out_shape = [jax.ShapeDtypeStruct((M, N), F32), jax.ShapeDtypeStruct((M, N), BF16), jax.ShapeDtypeStruct((M // tm, 1, N), F32)]
    else:
        out_specs, out_shape = o_spec, jax.ShapeDtypeStruct((rows, N), out_dtype)
    return _call(
        body, name=name, grid=(M // tm, N // tn, nk), in_specs=specs, out_specs=out_specs, out_shape=out_shape,
        scratch_shapes=[pltpu.VMEM((tm, tn), F32)] if nk > 1 else [],
        input_output_aliases={len(ins) - 1: 0} if has_into else {},
        compiler_params=_params("parallel", "parallel", "arbitrary"),
    )(*ins)


def _mm_norm_bwd(parts, b, x, resid, g, name, tm=512, tk=512):
    T, N = x.shape
    counts = [p.shape[1] // tk for p in parts]
    starts = [sum(counts[:i]) for i in range(len(parts))]
    nsteps = sum(counts)
    assert all(p.shape[1] % tk == 0 for p in parts) and b.shape == (nsteps * tk, N)
    npart = len(parts)

    def body(*refs):
        a_refs, b_ref, x_ref, res_ref, g_ref = refs[:npart], refs[npart], refs[npart + 1], refs[npart + 2], refs[npart + 3]
        dx_ref, dxb_ref, dg_ref, acc = refs[npart + 4:]
        i, s = pl.program_id(0), pl.program_id(1)

        @pl.when((i == 0) & (s == 0))
        def _():
            dg_ref[...] = jnp.zeros_like(dg_ref)

        for p in range(npart):
            @pl.when((s >= starts[p]) & (s < starts[p] + counts[p]))
            def _(p=p):
                d = _dot(a_refs[p][...], b_ref[...])

                @pl.when(s == 0)
                def _():
                    acc[...] = d

                @pl.when(s > 0)
                def _():
                    acc[...] += d

        @pl.when(s == nsteps - 1)
        def _():
            xv, dhv = x_ref[...], acc[...]
            r = lax.rsqrt(jnp.mean(xv * xv, axis=-1, keepdims=True) + EPS)
            gd = dhv * g_ref[...]
            m = jnp.mean(gd * xv, axis=-1, keepdims=True)
            dx = res_ref[...] + r * gd - xv * (r * r * r) * m
            dx_ref[...] = dx
            dxb_ref[...] = dx.astype(BF16)
            dg_ref[...] += jnp.sum(dhv * xv * r, axis=0, keepdims=True)

    a_specs = [pl.BlockSpec((tm, tk), lambda i, s, st=st, c=c: (i, jnp.clip(s - st, 0, c - 1))) for st, c in zip(starts, counts)]
    row = pl.BlockSpec((tm, N), lambda i, s: (i, 0))
    vec = pl.BlockSpec((1, N), lambda i, s: (0, 0))
    return _call(
        body, name=name, grid=(T // tm, nsteps),
        in_specs=a_specs + [pl.BlockSpec((tk, N), lambda i, s: (s, 0)), row, row, vec], out_specs=[row, row, vec],
        out_shape=[jax.ShapeDtypeStruct((T, N), F32), jax.ShapeDtypeStruct((T, N), BF16), jax.ShapeDtypeStruct((1, N), F32)],
        scratch_shapes=[pltpu.VMEM((tm, N), F32)], compiler_params=_params("arbitrary", "arbitrary"),
    )(*parts, b, x, resid, g)


def _rmsnorm(x, g, name, tm=512):
    T, D = x.shape

    def body(x_ref, g_ref, o_ref):
        xv = x_ref[...]
        r = lax.rsqrt(jnp.mean(xv * xv, axis=-1, keepdims=True) + EPS)
        o_ref[...] = (xv * r * g_ref[...]).astype(BF16)

    return _call(
        body, name=name, grid=(T // tm,),
        in_specs=[pl.BlockSpec((tm, D), lambda i: (i, 0)), pl.BlockSpec((1, D), lambda i: (0, 0))],
        out_specs=pl.BlockSpec((tm, D), lambda i: (i, 0)), out_shape=jax.ShapeDtypeStruct((T, D), BF16),
        compiler_params=_params("parallel"),
    )(x, g)


def _qk_prep(proj, pos, invf, qg, kg, bd, name, tm=512):
    T = proj.shape[0]

    def body(q_ref, k_ref, pos_ref, invf_ref, qg_ref, kg_ref, bd_ref, qo_ref, ko_ref):
        cos, sin = _rope_tables(pos_ref, invf_ref)

        def prep(xv, gv, scale):
            r = lax.rsqrt(_group_mean(xv * xv, bd_ref[...]) + EPS)
            yv = xv * r * gv
            return ((yv * cos + _rot_half(yv) * sin) * scale).astype(BF16).astype(F32)

        qo_ref[...] = prep(q_ref[...], qg_ref[...], HEAD_DIM ** -0.5)
        ko_ref[...] = prep(k_ref[...], kg_ref[...], 1.0)

    col = lambda j: pl.BlockSpec((tm, ATTN_W), lambda i, j=j: (i, j))
    vec = pl.BlockSpec((1, ATTN_W), lambda i: (0, 0))
    out = pl.BlockSpec((tm, ATTN_W), lambda i: (i, 0))
    return _call(
        body, name=name, grid=(T // tm,),
        in_specs=[col(0), col(1), pl.BlockSpec((tm, 1), lambda i: (i, 0)), vec, vec, vec,
                  pl.BlockSpec((2 * HEAD_DIM, 2 * HEAD_DIM), lambda i: (0, 0))],
        out_specs=[out, out], out_shape=[jax.ShapeDtypeStruct((T, ATTN_W), F32)] * 2,
        compiler_params=_params("parallel"),
    )(proj, proj, pos, invf, qg, kg, bd)


def _qk_prep_bwd(proj, dqh, dkh, dv, pos, invf, qg, kg, bd, name, tm=512):
    T = proj.shape[0]

    def body(q_ref, k_ref, dq_ref, dk_ref, dv_ref, pos_ref, invf_ref, qg_ref, kg_ref, bd_ref, o_ref, gq_ref, gk_ref):
        @pl.when(pl.program_id(0) == 0)
        def _():
            gq_ref[...] = jnp.zeros_like(gq_ref)
            gk_ref[...] = jnp.zeros_like(gk_ref)

        cos, sin = _rope_tables(pos_ref, invf_ref)

        def back(xv, gv, dz, scale):
            dz = dz * scale
            dy = dz * cos - _rot_half(dz * sin)
            r = lax.rsqrt(_group_mean(xv * xv, bd_ref[...]) + EPS)
            gd = dy * gv
            m = _group_mean(gd * xv, bd_ref[...])
            dx = r * gd - xv * (r * r * r) * m
            return dx, jnp.sum(dy * xv * r, axis=0, keepdims=True)

        dxq, gs = back(q_ref[...], qg_ref[...], dq_ref[...], HEAD_DIM ** -0.5)
        gq_ref[...] += gs
        dxk, gs = back(k_ref[...], kg_ref[...], dk_ref[...], 1.0)
        gk_ref[...] += gs
        o_ref[...] = jnp.concatenate([dxq.astype(BF16), dxk.astype(BF16), dv_ref[...].astype(BF16)], axis=1)

    col = lambda j: pl.BlockSpec((tm, ATTN_W), lambda i, j=j: (i, j))
    row = pl.BlockSpec((tm, ATTN_W), lambda i: (i, 0))
    vec = pl.BlockSpec((1, ATTN_W), lambda i: (0, 0))
    return _call(
        body, name=name, grid=(T // tm,),
        in_specs=[col(0), col(1), row, row, row, pl.BlockSpec((tm, 1), lambda i: (i, 0)), vec, vec, vec,
                  pl.BlockSpec((2 * HEAD_DIM, 2 * HEAD_DIM), lambda i: (0, 0))],
        out_specs=[pl.BlockSpec((tm, 3 * ATTN_W), lambda i: (i, 0)), vec, vec],
        out_shape=[jax.ShapeDtypeStruct((T, 3 * ATTN_W), BF16)] + [jax.ShapeDtypeStruct((1, ATTN_W), F32)] * 2,
        compiler_params=_params("arbitrary"),
    )(proj, proj, dqh, dkh, dv, pos, invf, qg, kg, bd)


def _ld(ref, start, size, dil):
    return ref[pl.ds(start, size), :] if dil == 1 else ref[pl.ds(start, size, stride=dil), :]


def _st(ref, start, size, dil, val):
    if dil == 1:
        ref[pl.ds(start, size), :] = val
    else:
        ref[pl.ds(start, size, stride=dil), :] = val


def _attn_geometry(T, dil):
    nb = T // dil // QBLK
    kw = 2 * QBLK if nb >= 2 else QBLK
    return nb, kw


ATTN_UNROLL = 4


def _attn_unit(j, u, dil, nit):
    return ATTN_UNROLL * j + u if dil >= ATTN_UNROLL else j + u * (nit // ATTN_UNROLL)


def _attn_block(it, dil, kw):
    c, n = it & (dil - 1), lax.shift_right_logical(it, dil.bit_length() - 1)
    sq = n * (QBLK * dil) + c
    sk = jnp.maximum(n - (kw // QBLK - 1), 0) * (QBLK * dil) + c
    qi = lax.broadcasted_iota(jnp.int32, (2 * QBLK, kw), 0) & (QBLK - 1)
    kj = lax.broadcasted_iota(jnp.int32, (2 * QBLK, kw), 1)
    rel = jnp.where(n > 0, kw - QBLK, 0) + qi - kj
    return sq, sk, (rel >= 0) & (rel <= QBLK)


def _stack_heads(xv, head0):
    z = jnp.zeros_like(xv)
    return jnp.concatenate([jnp.where(head0, xv, z), jnp.where(head0, z, xv)], axis=0)


def _unstack_heads(x2, head0):
    return jnp.where(head0, x2[:QBLK], x2[QBLK:])


def _attn_fwd(qf, kf, proj, name):
    T = qf.shape[0]

    def body(q_ref, k_ref, v_ref, o_ref, lse_ref):
        head0 = lax.broadcasted_iota(jnp.int32, (QBLK, 2 * HEAD_DIM), 1) < HEAD_DIM
        for bi, dil in enumerate(DILATIONS):
            nb, kw = _attn_geometry(T, dil)

            nit = nb * dil

            def step(j, carry, bi=bi, dil=dil, kw=kw, nit=nit):
                units = []
                for u in range(ATTN_UNROLL):
                    sq, sk, ok = _attn_block(_attn_unit(j, u, dil, nit), dil, kw)
                    old = (_ld(o_ref, sq, QBLK, dil), _ld(lse_ref, sq, QBLK, dil)) if bi > 0 else None
                    units.append((sq, ok, _ld(q_ref, sq, QBLK, dil).astype(BF16), _ld(k_ref, sk, kw, dil).astype(BF16),
                                  _ld(v_ref, sk, kw, dil).astype(BF16), old))
                results = []
                for sq, ok, qv, kv, vv, old in units:
                    s = jnp.where(ok, _dot(_stack_heads(qv, head0), kv, _NT), NEG_INF)
                    m = jnp.max(s, axis=-1, keepdims=True)
                    p = jnp.exp(s - m).astype(BF16)
                    acc = _dot(p, jnp.concatenate([vv, jnp.ones_like(vv)], axis=1))
                    l = acc[:, 2 * HEAD_DIM:]
                    o_new = _unstack_heads(acc[:, :2 * HEAD_DIM] / l, head0)
                    l_new = _unstack_heads(m + jnp.log(l), head0)
                    if bi > 0:
                        o_old, l_old = old
                        mx = jnp.maximum(l_old, l_new)
                        e0, e1 = jnp.exp(l_old - mx), jnp.exp(l_new - mx)
                        z = e0 + e1
                        o_new = (e0 * o_old + e1 * o_new) / z
                        l_new = mx + jnp.log(z)
                    results.append((sq, o_new, l_new))
                for sq, o_new, l_new in results:
                    _st(o_ref, sq, QBLK, dil, o_new)
                    _st(lse_ref, sq, QBLK, dil, l_new)
                return carry

            lax.fori_loop(0, nit // ATTN_UNROLL, step, 0)

    blk = lambda off: pl.BlockSpec((T, 2 * HEAD_DIM), lambda hp, off=off: (0, off + hp))
    return _call(
        body, name=name, grid=(4,), in_specs=[blk(0), blk(0), blk(8)], out_specs=[blk(0), blk(0)],
        out_shape=[jax.ShapeDtypeStruct((T, ATTN_W), F32)] * 2, compiler_params=_params("parallel"),
    )(qf, kf, proj)


def _attn_bwd(qf, kf, proj, do, lse, delta, name):
    T = qf.shape[0]

    def body(q_ref, k_ref, v_ref, do_ref, lse_ref, dl_ref, dq_ref, dk_ref, dv_ref):
        head0 = lax.broadcasted_iota(jnp.int32, (QBLK, 2 * HEAD_DIM), 1) < HEAD_DIM
        for ref in (dq_ref, dk_ref, dv_ref):
            ref[...] = jnp.zeros_like(ref)
        for dil in DILATIONS:
            nb, kw = _attn_geometry(T, dil)

            nit = nb * dil

            def step(j, carry, dil=dil, kw=kw, nit=nit):
                units = []
                for u in range(ATTN_UNROLL):
                    sq, sk, ok = _attn_block(_attn_unit(j, u, dil, nit), dil, kw)
                    lsev, dlv = _ld(lse_ref, sq, QBLK, dil), _ld(dl_ref, sq, QBLK, dil)
                    units.append((sq, sk, ok, _ld(q_ref, sq, QBLK, dil).astype(BF16), _ld(do_ref, sq, QBLK, dil).astype(BF16),
                                  jnp.concatenate([lsev[:, 0:1], lsev[:, HEAD_DIM:HEAD_DIM + 1]], axis=0),
                                  jnp.concatenate([dlv[:, 0:1], dlv[:, HEAD_DIM:HEAD_DIM + 1]], axis=0),
                                  _ld(k_ref, sk, kw, dil).astype(BF16), _ld(v_ref, sk, kw, dil).astype(BF16),
                                  _ld(dq_ref, sq, QBLK, dil), _ld(dk_ref, sk, kw, dil), _ld(dv_ref, sk, kw, dil)))
                results = []
                for sq, sk, ok, qv, dov, lse2, dl2, kv, vv, dq0, dk0, dv0 in units:
                    q2, do2 = _stack_heads(qv, head0), _stack_heads(dov, head0)
                    p = jnp.where(ok, jnp.exp(_dot(q2, kv, _NT) - lse2), 0.0)
                    ds = (p * (_dot(do2, vv, _NT) - dl2)).astype(BF16)
                    results.append((sq, sk, dq0 + _unstack_heads(_dot(ds, kv), head0),
                                    dk0 + _dot(ds, q2, _TN), dv0 + _dot(p.astype(BF16), do2, _TN)))
                for sq, sk, dq, dk, dv in results:
                    _st(dq_ref, sq, QBLK, dil, dq)
                    _st(dk_ref, sk, kw, dil, dk)
                    _st(dv_ref, sk, kw, dil, dv)
                return carry

            lax.fori_loop(0, nit // ATTN_UNROLL, step, 0)

    blk = lambda off: pl.BlockSpec((T, 2 * HEAD_DIM), lambda hp, off=off: (0, off + hp))
    return _call(
        body, name=name, grid=(4,), in_specs=[blk(0), blk(0), blk(8), blk(0), blk(0), blk(0)], out_specs=[blk(0)] * 3,
        out_shape=[jax.ShapeDtypeStruct((T, ATTN_W), F32)] * 3, compiler_params=_params("parallel"),
    )(qf, kf, proj, do, lse, delta)


def _attn_norm(attn, g, name, tm=512):
    T = attn.shape[0]

    def body(a_ref, g_ref, o_ref):
        av = a_ref[...]
        r = lax.rsqrt(jnp.mean(av * av, axis=-1, keepdims=True) + EPS)
        o_ref[...] = (av * r * g_ref[...]).astype(BF16)

    row = pl.BlockSpec((tm, ATTN_W), lambda i: (i, 0))
    return _call(
        body, name=name, grid=(T // tm,), in_specs=[row, pl.BlockSpec((1, ATTN_W), lambda i: (0, 0))], out_specs=row,
        out_shape=jax.ShapeDtypeStruct((T, 2 * ATTN_W), BF16), compiler_params=_params("parallel"),
    )(attn, g)


def _attn_norm_bwd(dmix, attn, g, bd, name, tm=512):
    T = attn.shape[0]

    def body(d_ref, a_ref, g_ref, bd_ref, do_ref, dl_ref, dg_ref):
        @pl.when(pl.program_id(0) == 0)
        def _():
            dg_ref[...] = jnp.zeros_like(dg_ref)

        dy, av = d_ref[...], a_ref[...]
        r = lax.rsqrt(jnp.mean(av * av, axis=-1, keepdims=True) + EPS)
        gd = dy * g_ref[...]
        m = jnp.mean(gd * av, axis=-1, keepdims=True)
        da = r * gd - av * (r * r * r) * m
        do_ref[...] = da
        dl_ref[...] = _group_mean(da * av, bd_ref[...]) * float(HEAD_DIM)
        dg_ref[...] += jnp.sum(dy * av * r, axis=0, keepdims=True)

    row = pl.BlockSpec((tm, ATTN_W), lambda i: (i, 0))
    vec = pl.BlockSpec((1, ATTN_W), lambda i: (0, 0))
    return _call(
        body, name=name, grid=(T // tm,),
        in_specs=[row, row, vec, pl.BlockSpec((2 * HEAD_DIM, 2 * HEAD_DIM), lambda i: (0, 0))], out_specs=[row, row, vec],
        out_shape=[jax.ShapeDtypeStruct((T, ATTN_W), F32)] * 2 + [jax.ShapeDtypeStruct((1, ATTN_W), F32)],
        compiler_params=_params("arbitrary"),
    )(dmix, attn, g, bd)


def _rec_gates(xc, wrg_ref, wig_ref, brg_ref, big_ref, lam_ref):
    xb = xc.astype(BF16)
    r = _sigmoid(_dot(xb, wrg_ref[...]) + brg_ref[...])
    ig = _sigmoid(_dot(xb, wig_ref[...]) + big_ref[...])
    sp = _softplus_neg(lam_ref[...])
    log_a = -LRU_C * r * sp
    a = jnp.exp(log_a)
    th = jnp.tanh(log_a)
    mult = jnp.sqrt(-2.0 * th / (1.0 - th))
    return xb, r, ig, sp, a, mult


def _rec_fwd(proj, mix, cw, cb, wrg, wig, brg, big, lam, g, name, tm=256):
    T = proj.shape[0]
    hb = tm // 8

    def body(xr_ref, halo_ref, gr_ref, cw_ref, cb_ref, wrg_ref, wig_ref, brg_ref, big_ref, lam_ref, g_ref, mix_ref,
             xc_ref, h_ref, out_ref, carry):
        i = pl.program_id(0)

        @pl.when(i == 0)
        def _():
            carry[...] = jnp.zeros_like(carry)

        xr = xr_ref[...]
        halo = jnp.where(i > 0, halo_ref[...], 0.0)
        xc = cb_ref[...] + cw_ref[3:4, :] * xr
        for s in range(1, REC_CONV):
            xc = xc + cw_ref[3 - s:4 - s, :] * _shift_down(xr, halo, s)
        xc_ref[...] = xc
        _, _, ig, _, a, mult = _rec_gates(xc, wrg_ref, wig_ref, brg_ref, big_ref, lam_ref)
        pa, hl = _scan_fwd(a, mult * (ig * xc))
        h = hl + pa * carry[0:1, :]
        h_ref[...] = h
        carry[0:1, :] = h_ref[pl.ds(tm - 1, 1), :]
        hg = h * _gelu(gr_ref[...])
        r = lax.rsqrt(jnp.mean(hg * hg, axis=-1, keepdims=True) + EPS)
        out_ref[...] = (hg * r * g_ref[...]).astype(BF16)

    vec = pl.BlockSpec((1, REC_W), lambda i: (0, 0))
    row = pl.BlockSpec((tm, REC_W), lambda i: (i, 0))
    mat = pl.BlockSpec((REC_W, REC_W), lambda i: (0, 0))
    return _call(
        body, name=name, grid=(T // tm,),
        in_specs=[pl.BlockSpec((tm, REC_W), lambda i: (i, 3)),
                  pl.BlockSpec((8, REC_W), lambda i: (jnp.maximum(i * hb - 1, 0), 3)),
                  pl.BlockSpec((tm, REC_W), lambda i: (i, 4)),
                  pl.BlockSpec((8, REC_W), lambda i: (0, 0)), vec, mat, mat, vec, vec, vec, vec, ANY],
        out_specs=[row, row, pl.BlockSpec((tm, REC_W), lambda i: (i, 1))],
        out_shape=[jax.ShapeDtypeStruct((T, REC_W), F32)] * 2 + [jax.ShapeDtypeStruct(mix.shape, BF16)],
        scratch_shapes=[pltpu.VMEM((8, REC_W), F32)], input_output_aliases={11: 2},
        compiler_params=_params("arbitrary"),
    )(proj, proj, proj, cw, cb, wrg, wig, brg, big, lam, g, mix)


def _rec_bwd(dmix, proj, xc, h, cw, cb, wrg, wig, brg, big, lam, g, name, tm=256):
    T = proj.shape[0]
    nt = T // tm
    hb = tm // 8

    def body(d_ref, xr_ref, xhalo_ref, gr_ref, xc_ref, h_ref, hhalo_ref, cw_ref, cb_ref, wrg_ref, wig_ref, brg_ref,
             big_ref, lam_ref, g_ref,
             drec_ref, gcw_ref, gcb_ref, gwrg_ref, gwig_ref, gbrg_ref, gbig_ref, glam_ref, gg_ref,
             g_carry, a_first, dxc_next, gsp):
        i = pl.program_id(0)
        first_tile = i == nt - 1

        @pl.when(i == 0)
        def _():
            for ref in (gcw_ref, gcb_ref, gwrg_ref, gwig_ref, gbrg_ref, gbig_ref, glam_ref, gg_ref,
                        g_carry, a_first, dxc_next, gsp):
                ref[...] = jnp.zeros_like(ref)

        xr, xc, hv = xr_ref[...], xc_ref[...], h_ref[...]
        xhalo = jnp.where(first_tile, 0.0, xhalo_ref[...])
        hhalo = jnp.where(first_tile, 0.0, hhalo_ref[...])
        xb, r, ig, sp, a, mult = _rec_gates(xc, wrg_ref, wig_ref, brg_ref, big_ref, lam_ref)
        h_prev = _shift_down(hv, hhalo, 1)
        ge, dge = _gelu_and_grad(gr_ref[...])
        hg = hv * ge
        rr = lax.rsqrt(jnp.mean(hg * hg, axis=-1, keepdims=True) + EPS)
        dy = d_ref[...]
        gd = dy * g_ref[...]
        dhg = rr * gd - hg * (rr * rr * rr) * jnp.mean(gd * hg, axis=-1, keepdims=True)
        gg_ref[...] += jnp.sum(dy * hg * rr, axis=0, keepdims=True)
        dgr = (dhg * hv * dge).astype(BF16)
        dh = dhg * ge
        b = _shift_up(a, jnp.broadcast_to(a_first[0:1, :], (8, REC_W)), 1)
        pb, gl = _scan_bwd(b, dh)
        gs = gl + pb * g_carry[0:1, :]
        g_carry[0:1, :] = gs[0:1, :]
        a_first[0:1, :] = a[0:1, :]
        da = gs * h_prev
        dmult = gs * (ig * xc)
        di = gs * (mult * xc)
        dxc = gs * (mult * ig)
        dlog_a = da * a - dmult * (a * a) / mult
        gsp[...] += jnp.sum(dlog_a * (-LRU_C * r), axis=0, keepdims=True)
        dzr = (dlog_a * (-LRU_C * sp)) * (r * (1.0 - r))
        dzi = di * (ig * (1.0 - ig))
        dzr_b, dzi_b = dzr.astype(BF16), dzi.astype(BF16)
        dxc = dxc + _dot(dzr_b, wrg_ref[...], _NT) + _dot(dzi_b, wig_ref[...], _NT)
        gwrg_ref[...] += _dot(xb, dzr_b, _TN)
        gwig_ref[...] += _dot(xb, dzi_b, _TN)
        gbrg_ref[...] += jnp.sum(dzr, axis=0, keepdims=True)
        gbig_ref[...] += jnp.sum(dzi, axis=0, keepdims=True)
        nxt = dxc_next[...]
        dxr = cw_ref[3:4, :] * dxc
        gcw_ref[3:4, :] += jnp.sum(dxc * xr, axis=0, keepdims=True)
        for s in range(1, REC_CONV):
            dxr = dxr + cw_ref[3 - s:4 - s, :] * _shift_up(dxc, nxt, s)
            gcw_ref[3 - s:4 - s, :] += jnp.sum(dxc * _shift_down(xr, xhalo, s), axis=0, keepdims=True)
        gcb_ref[...] += jnp.sum(dxc, axis=0, keepdims=True)
        dxc_next[...] = dxc[:8]
        drec_ref[...] = jnp.concatenate([dxr.astype(BF16), dgr], axis=1)

        @pl.when(first_tile)
        def _():
            glam_ref[...] = gsp[...] * (-_sigmoid(-lam_ref[...]))

    rev = lambda i: nt - 1 - i
    vec = pl.BlockSpec((1, REC_W), lambda i: (0, 0))
    row = pl.BlockSpec((tm, REC_W), lambda i: (rev(i), 0))
    mat = pl.BlockSpec((REC_W, REC_W), lambda i: (0, 0))
    cwb = pl.BlockSpec((8, REC_W), lambda i: (0, 0))
    halo = lambda c: pl.BlockSpec((8, REC_W), lambda i, c=c: (jnp.maximum(rev(i) * hb - 1, 0), c))
    return _call(
        body, name=name, grid=(nt,),
        in_specs=[pl.BlockSpec((tm, REC_W), lambda i: (rev(i), 1)),
                  pl.BlockSpec((tm, REC_W), lambda i: (rev(i), 3)), halo(3),
                  pl.BlockSpec((tm, REC_W), lambda i: (rev(i), 4)),
                  row, row, halo(0), cwb, vec, mat, mat, vec, vec, vec, vec],
        out_specs=[pl.BlockSpec((tm, 2 * REC_W), lambda i: (rev(i), 0)), cwb, vec, mat, mat, vec, vec, vec, vec],
        out_shape=[jax.ShapeDtypeStruct((T, 2 * REC_W), BF16)]
        + [jax.ShapeDtypeStruct((8, REC_W), F32), jax.ShapeDtypeStruct((1, REC_W), F32)]
        + [jax.ShapeDtypeStruct((REC_W, REC_W), F32)] * 2 + [jax.ShapeDtypeStruct((1, REC_W), F32)] * 4,
        scratch_shapes=[pltpu.VMEM((8, REC_W), F32)] * 3 + [pltpu.VMEM((1, REC_W), F32)],
        compiler_params=_params("arbitrary"),
    )(dmix, proj, proj, proj, xc, h, h, cw, cb, wrg, wig, brg, big, lam, g)


def _ffn_conv(x_ext, cw_ref, cb_ref):
    return (cb_ref[...] + cw_ref[2:3, :] * x_ext + cw_ref[1:2, :] * pltpu.roll(x_ext, 1, 0)
            + cw_ref[0:1, :] * pltpu.roll(x_ext, 2, 0))


def _up_proj_act(h2, w_upT, cw, cb, name, tm=1024, tc=768):
    T = h2.shape[0]
    nc = D_FF // tc

    def body(h_ref, wg_ref, wu_ref, cwg_ref, cwu_ref, cbg_ref, cbu_ref, act_ref, da_ref, db_ref, pg_ref, pu_ref,
             hist_g, hist_u):
        i, j = pl.program_id(0), pl.program_id(1)
        hv = h_ref[...]
        pg, pu = _dot(hv, wg_ref[...], _NT), _dot(hv, wu_ref[...], _NT)
        ge = jnp.concatenate([jnp.where(i > 0, hist_g[j], 0.0), pg], axis=0)
        ue = jnp.concatenate([jnp.where(i > 0, hist_u[j], 0.0), pu], axis=0)
        gel, dgel = _gelu_and_grad(_ffn_conv(ge, cwg_ref, cbg_ref)[8:])
        uu = _ffn_conv(ue, cwu_ref, cbu_ref)[8:]
        act_ref[...] = (gel * uu).astype(BF16)
        da_ref[...] = (uu * dgel).astype(BF16)
        db_ref[...] = gel.astype(BF16)
        pg_ref[...] = pg.astype(BF16)
        pu_ref[...] = pu.astype(BF16)
        hist_g[j] = pg[tm - 8:]
        hist_u[j] = pu[tm - 8:]

    tile = pl.BlockSpec((tm, tc), lambda i, j: (i, j))
    wsp = lambda off: pl.BlockSpec((tc, D_MODEL), lambda i, j, off=off: (j + off, 0))
    cws = lambda off: pl.BlockSpec((8, tc), lambda i, j, off=off: (0, j + off))
    cbs = lambda off: pl.BlockSpec((1, tc), lambda i, j, off=off: (0, j + off))
    return _call(
        body, name=name, grid=(T // tm, nc),
        in_specs=[pl.BlockSpec((tm, D_MODEL), lambda i, j: (i, 0)), wsp(0), wsp(nc), cws(0), cws(nc), cbs(0), cbs(nc)],
        out_specs=[tile] * 5, out_shape=[jax.ShapeDtypeStruct((T, D_FF), BF16)] * 5,
        scratch_shapes=[pltpu.VMEM((nc, 8, tc), F32)] * 2, compiler_params=_params("arbitrary", "arbitrary"),
    )(h2, w_upT, w_upT, cw, cw, cb, cb)


def _ffn_bwd(dyb, w_down, da, db, pg, pu, cw, name, tm=512, tc=768):
    T, F = pg.shape
    nt = T // tm
    hb16 = tm // 16
    nc = F // tc
    n = tm + 8

    def body(dy_ref, dyn_ref, wd_ref, a_ref, an_ref, b_ref, bn_ref, g_ref, u_ref, cwg_ref, cwu_ref,
             dg_ref, du_ref, gcwg_ref, gcwu_ref, gcbg_ref, gcbu_ref):
        i = pl.program_id(1)
        last = i == nt - 1

        @pl.when(i == 0)
        def _():
            for ref in (gcwg_ref, gcwu_ref, gcbg_ref, gcbu_ref):
                ref[...] = jnp.zeros_like(ref)

        wd = wd_ref[...]
        dact_next = jnp.where(last, 0.0, _dot(dyn_ref[...], wd, _NT)[:8])
        de = jnp.concatenate([_dot(dy_ref[...], wd, _NT), dact_next], axis=0)
        ext = lambda t, nx: jnp.concatenate([t[...].astype(F32), nx[...].astype(F32)[:8]], axis=0)
        for dcv, x_ref, cw_ref, dx_ref, gcw_ref, gcb_ref in ((de * ext(a_ref, an_ref), g_ref, cwg_ref, dg_ref, gcwg_ref, gcbg_ref),
                                                               (de * ext(b_ref, bn_ref), u_ref, cwu_ref, du_ref, gcwu_ref, gcbu_ref)):
            s1, s2 = pltpu.roll(dcv, n - 1, 0), pltpu.roll(dcv, n - 2, 0)
            dx_ref[...] = (cw_ref[2:3, :] * dcv + cw_ref[1:2, :] * s1 + cw_ref[0:1, :] * s2)[:tm].astype(BF16)
            xv = x_ref[...].astype(F32)
            gcw_ref[2:3, :] += jnp.sum(xv * dcv[:tm], axis=0, keepdims=True)
            gcw_ref[1:2, :] += jnp.sum(xv * s1[:tm], axis=0, keepdims=True)
            gcw_ref[0:1, :] += jnp.sum(xv * s2[:tm], axis=0, keepdims=True)
            gcb_ref[...] += jnp.sum(dcv[:tm], axis=0, keepdims=True)

    tile = pl.BlockSpec((tm, tc), lambda j, i: (i, j))
    nxt = pl.BlockSpec((16, tc), lambda j, i: (jnp.minimum((i + 1) * hb16, nt * hb16 - 1), j))
    cws = lambda off: pl.BlockSpec((8, tc), lambda j, i, off=off: (0, j + off))
    cbs = pl.BlockSpec((1, tc), lambda j, i: (0, j))
    return _call(
        body, name=name, grid=(nc, nt),
        in_specs=[pl.BlockSpec((tm, D_MODEL), lambda j, i: (i, 0)),
                  pl.BlockSpec((16, D_MODEL), lambda j, i: (jnp.minimum((i + 1) * hb16, nt * hb16 - 1), 0)),
                  pl.BlockSpec((tc, D_MODEL), lambda j, i: (j, 0)), tile, nxt, tile, nxt, tile, tile, cws(0), cws(nc)],
        out_specs=[tile, tile, cws(0), cws(0), cbs, cbs],
        out_shape=[jax.ShapeDtypeStruct((T, F), BF16)] * 2 + [jax.ShapeDtypeStruct((8, F), F32)] * 2
        + [jax.ShapeDtypeStruct((1, F), F32)] * 2,
        compiler_params=_params("parallel", "arbitrary"),
    )(dyb, dyb, w_down, da, da, db, db, pg, pu, cw, cw)


def _adam_update(w, g, m, v):
    m2 = ADAM_B1 * m + (1.0 - ADAM_B1) * g
    v2 = ADAM_B2 * v + (1.0 - ADAM_B2) * (g * g)
    m_hat = m2 / (1.0 - ADAM_B1 ** ADAM_STEP)
    v_hat = v2 / (1.0 - ADAM_B2 ** ADAM_STEP)
    delta = -ADAM_LR * (m_hat / (jnp.sqrt(v_hat) + ADAM_EPS) + ADAM_WD * w)
    return delta, m2, v2


def _adam_sharded(p, r2, idx, w, m, v, name, transposed=False):
    r, n = p.shape[1:]
    nrecv = r2.shape[0]
    tr = (256 if r % 256 == 0 else r) if transposed else _row_tile(r)

    def body(c_ref, p_ref, r_ref, w_ref, m_ref, v_ref, g_ref, d_ref, m2_ref, v2_ref):
        g = p_ref[...].astype(F32)
        for k in range(nrecv):
            g = g + r_ref[k].astype(F32)
        if transposed:
            g = g.T
        g_ref[...] = g
        d_ref[...], m2_ref[...], v2_ref[...] = _adam_update(w_ref[...], g, m_ref[...], v_ref[...])

    blk = pl.BlockSpec((n, tr), lambda i, c_ref: (0, i)) if transposed else pl.BlockSpec((tr, n), lambda i, c_ref: (i, 0))
    spec = pltpu.PrefetchScalarGridSpec(
        num_scalar_prefetch=1, grid=(r // tr,),
        in_specs=[pl.BlockSpec((None, tr, n), lambda i, c_ref: (c_ref[0], i, 0)),
                  pl.BlockSpec((nrecv, tr, n), lambda i, c_ref: (0, i, 0)), blk, blk, blk],
        out_specs=[blk] * 4)
    return _call(body, name=name, grid_spec=spec, out_shape=[jax.ShapeDtypeStruct(w.shape, F32)] * 4,
                 compiler_params=_params("parallel"))(idx, p, r2, w, m, v)


def _sum_devices(allg, name):
    r, n = allg.shape[0] // N_DEV, allg.shape[1]

    def body(a_ref, o_ref):
        acc = a_ref[0:r, :]
        for k in range(1, N_DEV):
            acc = acc + a_ref[k * r:(k + 1) * r, :]
        o_ref[...] = acc

    return _call(body, name=name, out_shape=jax.ShapeDtypeStruct((r, n), F32))(allg)


def _adam_small(ws, gs, ms, vs, name):
    n = len(ws)

    def body(*refs):
        for i in range(n):
            d, m2, v2 = _adam_update(refs[i][...], refs[n + i][...], refs[2 * n + i][...], refs[3 * n + i][...])
            refs[4 * n + i][...] = d
            refs[5 * n + i][...] = m2
            refs[6 * n + i][...] = v2

    outs = _call(body, name=name, out_shape=[jax.ShapeDtypeStruct(w.shape, F32) for w in ws] * 3)(*ws, *gs, *ms, *vs)
    return outs[:n], outs[n:2 * n], outs[2 * n:]


_SMALL = (("g_mix", 1024), ("q_norm_g", 64), ("k_norm_g", 64), ("rec_conv_b", 512), ("w_rg", 32768), ("b_rg", 512),
          ("w_ig", 32768), ("b_ig", 512), ("lru_lambda", 512), ("g_attn_out", 512), ("g_rec_out", 512),
          ("g_ffn", 1024), ("ffn_conv_b", 6144))
_SMALL_SHAPES = {"g_mix": (1, 1024), "q_norm_g": (1, 64), "k_norm_g": (1, 64), "rec_conv_b": (1, 512),
                 "w_rg": (1, 8, 64, 64), "b_rg": (1, 8, 64), "w_ig": (1, 8, 64, 64), "b_ig": (1, 8, 64),
                 "lru_lambda": (1, 512), "g_attn_out": (1, 512), "g_rec_out": (1, 512), "g_ffn": (1, 1024),
                 "ffn_conv_b": (1, 6144)}


def _block_diag(w):
    eye = jnp.eye(8, dtype=w.dtype)
    return (w[:, :, None, :] * eye[:, None, :, None]).reshape(512, 512)


def kernel(x, positions, g_mix, w_in, q_norm_g, k_norm_g, rec_conv_w, rec_conv_b, w_rg, b_rg, w_ig, b_ig, lru_lambda, g_attn_out, g_rec_out, w_out, g_ffn, w_up, ffn_conv_w, ffn_conv_b, w_down, loss_target, m_g_mix, m_w_in, m_q_norm_g, m_k_norm_g, m_rec_conv_w, m_rec_conv_b, m_w_rg, m_b_rg, m_w_ig, m_b_ig, m_lru_lambda, m_g_attn_out, m_g_rec_out, m_w_out, m_g_ffn, m_w_up, m_ffn_conv_w, m_ffn_conv_b, m_w_down, v_g_mix, v_w_in, v_q_norm_g, v_k_norm_g, v_rec_conv_w, v_rec_conv_b, v_w_rg, v_b_rg, v_w_ig, v_b_ig, v_lru_lambda, v_g_attn_out, v_g_rec_out, v_w_out, v_g_ffn, v_w_up, v_ffn_conv_w, v_ffn_conv_b, v_w_down):
    T = x.shape[1]
    ix, iy, ic = lax.axis_index("x"), lax.axis_index("y"), lax.axis_index("c")
    dev = 4 * ix + 2 * iy + ic
    xs = x.reshape(T, D_MODEL)
    tgt = loss_target.reshape(T, D_MODEL)
    pos = positions.reshape(T, 1)

    shards = {"w_in": (w_in[0], m_w_in[0], v_w_in[0]), "w_out": (w_out[0], m_w_out[0], v_w_out[0]),
              "w_up": (w_up[0], m_w_up[0], v_w_up[0]), "w_down": (w_down[0], m_w_down[0], v_w_down[0])}
    taps = jnp.concatenate([rec_conv_w.reshape(-1), ffn_conv_w.reshape(-1), jnp.zeros((4096 - 2560,), F32)]).reshape(8, 512)
    W_inT, taps_all = _all_gather([w_in[0].T.astype(BF16), taps], "ag_w_in")
    late = [w_out[0].astype(BF16), w_up[0].T.astype(BF16), w_down[0].astype(BF16)]
    ag_send, ag_recv, late_thru, land_thru, ag_token = _exchange_start(
        late, _gather_landing(late, "ag_late_own"), "gather", taps_all, "ag_late_start")
    taps_all = taps_all.reshape(N_DEV, 4096)
    rcw = taps_all[:, :256].reshape(8, 4, 64).transpose(1, 0, 2).reshape(4, REC_W)
    fcw = taps_all[:, 256:2560].reshape(8, 3, 768).transpose(1, 0, 2).reshape(3, 2 * D_FF)
    rcw8 = jnp.pad(rcw, ((0, 4), (0, 0)))
    fcw8 = jnp.pad(fcw, ((0, 5), (0, 0)))
    fcb = ffn_conv_b.reshape(1, 2 * D_FF)

    half = HEAD_DIM // 2
    inv_freq = ROPE_THETA ** (-jnp.arange(half, dtype=F32) / half)
    invf = jnp.tile(inv_freq, 2 * N_HEADS).reshape(1, ATTN_W)
    bd = jnp.asarray(np.kron(np.eye(2), np.full((HEAD_DIM, HEAD_DIM), 1.0 / HEAD_DIM)), BF16)
    qg = jnp.tile(q_norm_g.reshape(HEAD_DIM), N_HEADS).reshape(1, ATTN_W)
    kg = jnp.tile(k_norm_g.reshape(HEAD_DIM), N_HEADS).reshape(1, ATTN_W)
    wrg_bd = _block_diag(w_rg[0]).astype(BF16)
    wig_bd = _block_diag(w_ig[0]).astype(BF16)
    brg, big = b_rg.reshape(1, REC_W), b_ig.reshape(1, REC_W)

    h1 = _rmsnorm(xs, g_mix + ag_token[0, 0], "norm_mix")
    proj = _mm(h1, W_inT, "nt", F32, "in_proj", tn=1280)
    qf, kf = _qk_prep(proj, pos, invf, qg, kg, bd, "qk_prep")
    attn, lse = _attn_fwd(qf, kf, proj, "attn_fwd")
    mix = _attn_norm(attn, g_attn_out, "attn_norm")
    xc, hstate, mix = _rec_fwd(proj, mix, rcw8, rec_conv_b, wrg_bd, wig_bd, brg, big, lru_lambda, g_rec_out, "rec_fwd")
    _, (W_out, W_upT, W_down) = _exchange_wait(ag_send, ag_recv, late_thru, land_thru, "gather", hstate, "ag_late_wait")
    x2 = _mm(mix, W_out, "nn", F32, "out_proj", add=xs)

    h2 = _rmsnorm(x2, g_ffn, "norm_ffn")
    act, da, db, pg, pu = _up_proj_act(h2, W_upT, fcw8, fcb, "up_proj_act")
    dy, dyb, lparts = _mm(act, W_down, "nn", F32, "down_proj_loss", add=x2, loss_target=tgt, tm=512, tk=D_FF)
    loss_mine = 0.5 / D_MODEL * jnp.sum(lparts)

    g_down = _mm(act, dyb, "tn", BF16, "g_w_down", tk=2048)
    dpg, dpu, g_fcwg, g_fcwu, g_fcbg, g_fcbu = _ffn_bwd(dyb, W_down, da, db, pg, pu, fcw8, "ffn_bwd")
    g_upT = _mm(dpg, h2, "tn", BF16, "g_w_up_gate", tk=2048, o_rows=2 * D_FF)
    g_upT = _mm(dpu, h2, "tn", BF16, "g_w_up_up", tk=2048, into=g_upT, o_moff=D_FF // 1024)
    ffn_g = [g_upT.reshape(N_DEV, 2 * D_FF // N_DEV, 1024), g_down.reshape(N_DEV, D_FF // N_DEV, 1024)]
    rs_send, rs_recv, ffn_g, ffn_land, rs_token = _exchange_start(
        ffn_g, [_landing((N_PEERS,) + g.shape[1:], BF16) for g in ffn_g], "scatter", dpu, "rs_ffn_start")
    dx2, dx2b, g_gffn = _mm_norm_bwd([dpg, dpu], W_upT, x2, dy, g_ffn + rs_token[0, 0], "d_h2_norm_bwd", tm=1024, tk=1024)

    dmix = _mm(dx2b, W_out, "nt", F32, "d_mix")
    g_out = _mm(mix, dx2b, "tn", BF16, "g_w_out", tk=2048).reshape(N_DEV, D_MODEL // N_DEV, 1024)
    out_send, out_recv, (g_out,), out_land, out_token = _exchange_start(
        [g_out], [_landing((N_PEERS,) + g_out.shape[1:], BF16)], "scatter", dmix, "rs_out_start")
    do, delta, g_gattn = _attn_norm_bwd(dmix, attn, g_attn_out + out_token[0, 0], bd, "attn_norm_bwd")
    dqh, dkh, dv = _attn_bwd(qf, kf, proj, do, lse, delta, "attn_bwd")
    dqkv, g_qg, g_kg = _qk_prep_bwd(proj, dqh, dkh, dv, pos, invf, qg, kg, bd, "qk_prep_bwd")
    (drec, g_rcw, g_rcb, g_wrg, g_wig, g_brg, g_big, g_lam, g_grec) = _rec_bwd(
        dmix, proj, xc, hstate, rcw8, rec_conv_b, wrg_bd, wig_bd, brg, big, lru_lambda, g_rec_out, "rec_bwd")
    g_inT = _mm(dqkv, h1, "tn", BF16, "g_w_in_qkv", tm=512, o_rows=IN_W)
    g_inT = _mm(drec, h1, "tn", BF16, "g_w_in_rec", tm=512, into=g_inT, o_moff=3 * ATTN_W // 512)
    g_inT = g_inT.reshape(N_DEV, IN_W // N_DEV, 1024)
    in_send, in_recv, (g_inT,), in_land, in_token = _exchange_start(
        [g_inT], [_landing((N_PEERS,) + g_inT.shape[1:], BF16)], "scatter", drec, "rs_in_start")
    grad_x, _, g_gmix = _mm_norm_bwd([dqkv, drec], W_inT, xs, dx2, g_mix + in_token[0, 0], "d_h1_norm_bwd", tm=1024, tk=512)

    blocks = lambda g: jnp.stack([g[64 * n:64 * n + 64, 64 * n:64 * n + 64] for n in range(8)])
    small_g = {
        "g_mix": g_gmix, "q_norm_g": g_qg.reshape(N_HEADS, HEAD_DIM).sum(0), "k_norm_g": g_kg.reshape(N_HEADS, HEAD_DIM).sum(0),
        "rec_conv_b": g_rcb, "w_rg": blocks(g_wrg), "b_rg": g_brg, "w_ig": blocks(g_wig), "b_ig": g_big,
        "lru_lambda": g_lam, "g_attn_out": g_gattn, "g_rec_out": g_grec, "g_ffn": g_gffn,
        "ffn_conv_b": jnp.concatenate([g_fcbg, g_fcbu], axis=1)}
    g_fcw = jnp.concatenate([g_fcwg[:3], g_fcwu[:3]], axis=1)
    flat = jnp.concatenate([small_g[k].reshape(-1) for k, _ in _SMALL]
                           + [g_rcw[:4].reshape(-1), g_fcw.reshape(-1), loss_mine.reshape(1)])
    flat = jnp.pad(flat, (0, SMALL_ROWS * 1024 - flat.shape[0])).reshape(SMALL_ROWS, 1024)
    tot = _sum_devices(_all_gather([flat], "ag_small_grads")[0], "sum_small_grads").reshape(-1)
    g_small, o = {}, 0
    for k, n in _SMALL:
        g_small[k] = tot[o:o + n].reshape(_SMALL_SHAPES[k])
        o += n
    g_small["rec_conv_w"] = lax.dynamic_slice(tot[o:o + 2048].reshape(1, 4, REC_W), (0, 0, 64 * dev), (1, 4, 64))
    g_small["ffn_conv_w"] = lax.dynamic_slice(tot[o + 2048:o + 2048 + 18432].reshape(1, 3, 2 * D_FF), (0, 0, 768 * dev), (1, 3, 768))
    loss = tot[o + 2048 + 18432]

    devi = jnp.reshape(dev, (1,)).astype(jnp.int32)
    ffn_g, ffn_land = _exchange_wait(rs_send, rs_recv, ffn_g, ffn_land, "scatter", tot, "rs_ffn_wait")
    (g_out,), out_land = _exchange_wait(out_send, out_recv, [g_out], out_land, "scatter", tot, "rs_out_wait")
    (g_inT,), in_land = _exchange_wait(in_send, in_recv, [g_inT], in_land, "scatter", tot, "rs_in_wait")
    big_out = {"grad": {}, "delta": {}, "new_m": {}, "new_v": {}}
    for nm, p, r in (("w_up", ffn_g[0], ffn_land[0]), ("w_down", ffn_g[1], ffn_land[1]), ("w_out", g_out, out_land[0]),
                     ("w_in", g_inT, in_land[0])):
        w_, m_, v_ = shards[nm]
        res = _adam_sharded(p, r, devi, w_, m_, v_, "adam_" + nm, transposed=nm in ("w_in", "w_up"))
        for kind, a in zip(("grad", "delta", "new_m", "new_v"), res):
            big_out[kind][nm] = a[None]
    given = dict(rec_conv_w=rec_conv_w, ffn_conv_w=ffn_conv_w,g_mix=g_mix, q_norm_g=q_norm_g, k_norm_g=k_norm_g, rec_conv_b=rec_conv_b, w_rg=w_rg, b_rg=b_rg, w_ig=w_ig,
                 b_ig=b_ig, lru_lambda=lru_lambda, g_attn_out=g_attn_out, g_rec_out=g_rec_out, g_ffn=g_ffn, ffn_conv_b=ffn_conv_b)
    given_m = dict(rec_conv_w=m_rec_conv_w, ffn_conv_w=m_ffn_conv_w, g_mix=m_g_mix, q_norm_g=m_q_norm_g, k_norm_g=m_k_norm_g, rec_conv_b=m_rec_conv_b, w_rg=m_w_rg, b_rg=m_b_rg,
                   w_ig=m_w_ig, b_ig=m_b_ig, lru_lambda=m_lru_lambda, g_attn_out=m_g_attn_out, g_rec_out=m_g_rec_out,
                   g_ffn=m_g_ffn, ffn_conv_b=m_ffn_conv_b)
    given_v = dict(rec_conv_w=v_rec_conv_w, ffn_conv_w=v_ffn_conv_w, g_mix=v_g_mix, q_norm_g=v_q_norm_g, k_norm_g=v_k_norm_g, rec_conv_b=v_rec_conv_b, w_rg=v_w_rg, b_rg=v_b_rg,
                   w_ig=v_w_ig, b_ig=v_b_ig, lru_lambda=v_lru_lambda, g_attn_out=v_g_attn_out, g_rec_out=v_g_rec_out,
                   g_ffn=v_g_ffn, ffn_conv_b=v_ffn_conv_b)
    small = sorted(given)
    ds, m2s, v2s = _adam_small([given[k] for k in small], [g_small[k] for k in small], [given_m[k] for k in small],
                               [given_v[k] for k in small], "adam_small")
    small_out = {"grad": g_small, "delta": dict(zip(small, ds)), "new_m": dict(zip(small, m2s)), "new_v": dict(zip(small, v2s))}

    order = ("g_mix", "w_in", "q_norm_g", "k_norm_g", "rec_conv_w", "rec_conv_b", "w_rg", "b_rg", "w_ig", "b_ig",
             "lru_lambda", "g_attn_out", "g_rec_out", "w_out", "g_ffn", "w_up", "ffn_conv_w", "ffn_conv_b", "w_down")
    outs = [loss, grad_x.reshape(1, T, D_MODEL)]
    for kind in ("grad", "delta", "new_m", "new_v"):
        for name in order:
            outs.append(big_out[kind][name] if name in big_out[kind] else small_out[kind][name])
    return tuple(outs)
```

```python
import math

import numpy as np
import jax
import jax.numpy as jnp
from jax import lax
from jax.experimental import pallas as pl
from jax.experimental.pallas import tpu as pltpu

F32 = jnp.float32
BF16 = jnp.bfloat16

D_MODEL = 1024
HEAD_DIM = 64
ATTN_W = 512
REC_W = 512
N_HEADS = 8
D_FF = 3072
IN_W = 2560
REC_CONV = 4
FFN_CONV = 3
LRU_C = 8.0
ROPE_THETA = 10000.0
EPS = 1e-6
NEG_INF = -1e30
QBLK = 128
DILATIONS = (1, 4, 16)
N_DEV = 8
SMALL_ROWS = 96
ADAM_LR, ADAM_B1, ADAM_B2, ADAM_EPS, ADAM_WD, ADAM_STEP = 0.001, 0.9, 0.999, 1e-08, 0.01, 10
MESH = pl.DeviceIdType.MESH
ANY = pl.BlockSpec(memory_space=pl.ANY)


def _call(body, *, name, **kw):
    return pl.pallas_call(body, name=name, **kw)


def _params(*sem):
    return pltpu.CompilerParams(dimension_semantics=sem, vmem_limit_bytes=56 * 1024 * 1024)


def _gelu(x):
    c = math.sqrt(2.0 / math.pi)
    return 0.5 * x * (1.0 + jnp.tanh(c * (x + 0.044715 * (x * x * x))))


def _gelu_and_grad(x):
    c = math.sqrt(2.0 / math.pi)
    t = jnp.tanh(c * (x + 0.044715 * (x * x * x)))
    g = 0.5 * x * (1.0 + t)
    dg = 0.5 * (1.0 + t) + 0.5 * x * (1.0 - t * t) * (c * (1.0 + 3.0 * 0.044715 * (x * x)))
    return g, dg


def _sigmoid(x):
    return 1.0 / (1.0 + jnp.exp(-x))


def _softplus_neg(lam):
    y = jnp.exp(-jnp.abs(lam))
    u = 1.0 + y
    log1p = jnp.where(u == 1.0, y, jnp.log(u) * y / jnp.where(u == 1.0, 1.0, u - 1.0))
    return jnp.maximum(-lam, 0.0) + log1p


_NN = (((1,), (0,)), ((), ()))
_NT = (((1,), (1,)), ((), ()))
_TN = (((0,), (0,)), ((), ()))


def _dot(a, b, dims=_NN):
    return lax.dot_general(a, b, dims, preferred_element_type=F32)


def _group_mean(v, bd):
    hi = v.astype(BF16)
    lo = (v - hi.astype(F32)).astype(BF16)
    w = bd.shape[0]
    return jnp.concatenate([_dot(hi[:, c:c + w], bd) + _dot(lo[:, c:c + w], bd) for c in range(0, v.shape[1], w)], axis=1)


def _rope_tables(pos_ref, invf_ref):
    ang = pos_ref[...].astype(F32) * invf_ref[:, :2 * HEAD_DIM]
    reps = invf_ref.shape[1] // (2 * HEAD_DIM)
    return jnp.tile(jnp.cos(ang), (1, reps)), jnp.tile(jnp.sin(ang), (1, reps))


def _shift_down(x, halo, s):
    rolled = pltpu.roll(x, s, 0)
    hr = pltpu.roll(halo, s, 0)
    row = lax.broadcasted_iota(jnp.int32, hr.shape, 0)
    first = jnp.where(row < s, hr, rolled[:8])
    return jnp.concatenate([first, rolled[8:]], axis=0)


def _shift_up(x, halo, s):
    n = x.shape[0]
    rolled = pltpu.roll(x, n - s, 0)
    hr = pltpu.roll(halo, 8 - s, 0)
    row = lax.broadcasted_iota(jnp.int32, hr.shape, 0)
    last = jnp.where(row >= 8 - s, hr, rolled[n - 8:])
    return jnp.concatenate([rolled[:n - 8], last], axis=0)


def _scan_fwd(a, u):
    n, w = a.shape
    a3, u3 = a.reshape(n // 8, 8, w), u.reshape(n // 8, 8, w)
    row = lax.broadcasted_iota(jnp.int32, a3.shape, 1)
    for s in (1, 2, 4):
        a_s = jnp.where(row < s, 1.0, pltpu.roll(a3, s, 1))
        u_s = jnp.where(row < s, 0.0, pltpu.roll(u3, s, 1))
        u3 = u3 + a3 * u_s
        a3 = a3 * a_s
    ps, hs = [a3[0]], [u3[0]]
    for k in range(1, n // 8):
        ps.append(a3[k] * ps[-1][7:8, :])
        hs.append(u3[k] + a3[k] * hs[-1][7:8, :])
    return jnp.concatenate(ps, axis=0), jnp.concatenate(hs, axis=0)


def _scan_bwd(b, v):
    n, w = b.shape
    b3, v3 = b.reshape(n // 8, 8, w), v.reshape(n // 8, 8, w)
    row = lax.broadcasted_iota(jnp.int32, b3.shape, 1)
    for s in (1, 2, 4):
        b_s = jnp.where(row >= 8 - s, 1.0, pltpu.roll(b3, 8 - s, 1))
        v_s = jnp.where(row >= 8 - s, 0.0, pltpu.roll(v3, 8 - s, 1))
        v3 = v3 + b3 * v_s
        b3 = b3 * b_s
    last = n // 8 - 1
    ps, gs = [b3[last]], [v3[last]]
    for k in range(last - 1, -1, -1):
        ps.append(b3[k] * ps[-1][0:1, :])
        gs.append(v3[k] + b3[k] * gs[-1][0:1, :])
    return jnp.concatenate(ps[::-1], axis=0), jnp.concatenate(gs[::-1], axis=0)


def _rot_half(y):
    n = y.shape[1]
    lane = lax.broadcasted_iota(jnp.int32, y.shape, 1) & (HEAD_DIM - 1)
    return jnp.where(lane < HEAD_DIM // 2, -pltpu.roll(y, n - HEAD_DIM // 2, 1), pltpu.roll(y, HEAD_DIM // 2, 1))


def _row_tile(r, cap=256):
    return max(t for t in range(16, cap + 1, 16) if r % t == 0)


def _all_gather(shards, name):
    na = len(shards)
    ms = [s.shape[0] for s in shards]

    def body(*refs):
        x_refs, out_refs = refs[:na], refs[na:2 * na]
        send_sems, recv_sems, local_sems = refs[2 * na:]
        x, y, c = lax.axis_index("x"), lax.axis_index("y"), lax.axis_index("c")
        me, sibling = (x, y, c), (x, y, 1 - c)
        chips = [(1 - x, y), (x, 1 - y), (1 - x, 1 - y)]

        def rows(a, px, py, pc):
            return out_refs[a].at[pl.ds((4 * px + 2 * py + pc) * ms[a], ms[a]), :]

        def copy(a, k, block, to, src=None):
            return pltpu.make_async_remote_copy(
                src_ref=rows(a, *block) if src is None else src, dst_ref=rows(a, *block),
                send_sem=send_sems.at[7 * a + k], recv_sem=recv_sems.at[7 * a + k], device_id=to, device_id_type=MESH)

        mine = [pltpu.make_async_copy(x_refs[a], rows(a, *me), local_sems.at[a]) for a in range(na)]
        first = []
        for a in range(na):
            mine[a].start()
            first.append(copy(a, 0, me, sibling, src=x_refs[a]))
            first += [copy(a, 1 + j, me, (*chip, c), src=x_refs[a]) for j, chip in enumerate(chips)]
        for cp in first:
            cp.start()
        passed = []
        for a in range(na):
            for j, chip in enumerate(chips):
                copy(a, 1 + j, (*chip, c), me).wait_recv()
                fw = copy(a, 4 + j, (*chip, c), sibling)
                fw.start()
                passed.append(fw)
        for a in range(na):
            copy(a, 0, sibling, me).wait_recv()
            for j, chip in enumerate(chips):
                copy(a, 4 + j, (*chip, 1 - c), me).wait_recv()
        for cp in first + passed:
            cp.wait_send()
        for cp in mine:
            cp.wait()

    return _call(
        body, name=name, out_shape=[jax.ShapeDtypeStruct((N_DEV * s.shape[0], s.shape[1]), s.dtype) for s in shards],
        in_specs=[ANY] * na, out_specs=[ANY] * na,
        scratch_shapes=[pltpu.SemaphoreType.DMA((7 * na,)), pltpu.SemaphoreType.DMA((7 * na,)),
                        pltpu.SemaphoreType.DMA((na,))],
    )(*shards)


HBM = pl.BlockSpec(memory_space=pltpu.HBM)
SEM = pl.BlockSpec(memory_space=pltpu.SEMAPHORE)
EFFECT = pltpu.SideEffectType.DATAFLOW_SIDE_EFFECTING
N_PEERS = N_DEV - 1


def _peer(k):
    x, y, c = lax.axis_index("x"), lax.axis_index("y"), lax.axis_index("c")
    b = k + 1
    flip = lambda v, bit: 1 - v if bit else v
    return flip(x, b & 4), flip(y, b & 2), flip(c, b & 1)


def _in_hbm(a):
    return pltpu.with_memory_space_constraint(a, pltpu.HBM)


def _split_copy_descr(na, kind, src_refs, land_refs, send_sems, recv_sems):
    x, y, c = lax.axis_index("x"), lax.axis_index("y"), lax.axis_index("c")
    me = 4 * x + 2 * y + c
    copies = []
    for a in range(na):
        for k in range(N_PEERS):
            px, py, pc = _peer(k)
            if kind == "gather":
                m = src_refs[a].shape[0]
                src, dst = src_refs[a], land_refs[a].at[pl.ds(me * m, m), :]
            else:
                src, dst = src_refs[a].at[4 * px + 2 * py + pc], land_refs[a].at[k]
            copies.append(pltpu.make_async_remote_copy(
                src_ref=src, dst_ref=dst, send_sem=send_sems.at[N_PEERS * a + k], recv_sem=recv_sems.at[N_PEERS * a + k],
                device_id=(px, py, pc), device_id_type=MESH))
    return copies


def _landing(shape, dtype, own=None, at=None):
    buf = lax.empty(shape, dtype)
    return buf if own is None else lax.dynamic_update_slice(buf, own, (at, 0))


def _exchange_start(srcs, lands, kind, after, name):
    na = len(srcs)
    land_shapes = [l.shape for l in lands]

    def body(*refs):
        src_refs, land_refs = refs[:na], refs[na:2 * na]
        send_sems, recv_sems = refs[2 * na + 1], refs[2 * na + 2]
        token = refs[-1]
        for cp in _split_copy_descr(na, kind, src_refs, land_refs, send_sems, recv_sems):
            cp.start()
        token[...] = jnp.zeros_like(token)

    lands = [_in_hbm(l) for l in lands]
    sem = pltpu.SemaphoreType.DMA((N_PEERS * na,))
    outs = _call(
        body, name=name,
        out_shape=[sem, sem] + [pltpu.HBM(s.shape, s.dtype) for s in srcs] + [pltpu.HBM(s, srcs[0].dtype) for s in land_shapes]
        + [jax.ShapeDtypeStruct((8, 128), F32)],
        in_specs=[HBM] * (2 * na) + [ANY], out_specs=[SEM, SEM] + [HBM] * (2 * na) + [pl.BlockSpec(memory_space=pltpu.VMEM)],
        input_output_aliases={i: 2 + i for i in range(2 * na)},
        compiler_params=pltpu.CompilerParams(has_side_effects=EFFECT),
    )(*[_in_hbm(s) for s in srcs], *lands, after)
    return outs[0], outs[1], outs[2:2 + na], outs[2 + na:2 + 2 * na], outs[-1]


def _exchange_wait(send_sems, recv_sems, srcs, lands, kind, after, name):
    na = len(srcs)

    def body(*refs):
        src_refs, land_refs = refs[:na], refs[na:2 * na]
        s_sems, r_sems = refs[2 * na], refs[2 * na + 1]
        for cp in _split_copy_descr(na, kind, src_refs, land_refs, s_sems, r_sems):
            cp.wait_send()
            cp.wait_recv()

    outs = _call(
        body, name=name, out_shape=[pltpu.HBM(s.shape, s.dtype) for s in srcs] + [pltpu.HBM(l.shape, l.dtype) for l in lands],
        in_specs=[HBM] * (2 * na) + [SEM, SEM, ANY], out_specs=[HBM] * (2 * na),
        input_output_aliases={i: i for i in range(2 * na)},
        compiler_params=pltpu.CompilerParams(has_side_effects=EFFECT),
    )(*srcs, *lands, send_sems, recv_sems, after)
    return outs[:na], outs[na:]


def _mm(a, b, mode, out_dtype, name, add=None, tm=1024, tn=1024, tk=1024, b_noff=0, b_koff=0,
        n=None, k=None, into=None, o_rows=None, o_moff=0, loss_target=None):
    if mode == "tn":
        K, M = a.shape
    else:
        M, K = a.shape
    N = n if n is not None else (b.shape[0] if mode == "nt" else b.shape[1])
    if k is not None:
        assert k == K
    tm, tn, tk = min(tm, M), min(tn, N), min(tk, K)
    assert M % tm == 0 and N % tn == 0 and K % tk == 0, (name, M, N, K)
    nk = K // tk
    if mode == "nn":
        a_spec = pl.BlockSpec((tm, tk), lambda i, j, kk: (i, kk))
        b_spec, dims = pl.BlockSpec((tk, tn), lambda i, j, kk: (kk + b_koff, j + b_noff)), _NN
    elif mode == "nt":
        a_spec = pl.BlockSpec((tm, tk), lambda i, j, kk: (i, kk))
        b_spec, dims = pl.BlockSpec((tn, tk), lambda i, j, kk: (j + b_noff, kk + b_koff)), _NT
    else:
        a_spec = pl.BlockSpec((tk, tm), lambda i, j, kk: (kk, i))
        b_spec, dims = pl.BlockSpec((tk, tn), lambda i, j, kk: (kk + b_koff, j + b_noff)), _TN
    o_spec = pl.BlockSpec((tm, tn), lambda i, j, kk: (i + o_moff, j))
    has_add, has_into, has_loss = add is not None, into is not None, loss_target is not None
    assert not has_loss or (has_add and tn == N and not has_into)
    n_in = 2 + has_add + has_loss + has_into

    def body(*refs):
        a_ref, b_ref = refs[0], refs[1]
        add_ref = refs[2] if has_add else None
        outs = refs[n_in:]

        def finish(r):
            if has_add:
                r = r + add_ref[...]
            if has_loss:
                e = r - refs[3][...]
                dy = e * (1.0 / N)
                outs[0][...] = dy
                outs[1][...] = dy.astype(BF16)
                outs[2][...] = jnp.sum(e * e, axis=0, keepdims=True)[None]
            else:
                outs[0][...] = r.astype(out_dtype)

        if nk == 1:
            finish(_dot(a_ref[...], b_ref[...], dims))
        else:
            acc = refs[-1]
            kk = pl.program_id(2)

            @pl.when(kk == 0)
            def _():
                acc[...] = _dot(a_ref[...], b_ref[...], dims)

            @pl.when((kk > 0) & (kk < nk - 1))
            def _():
                acc[...] += _dot(a_ref[...], b_ref[...], dims)

            @pl.when(kk == nk - 1)
            def _():
                finish(acc[...] + _dot(a_ref[...], b_ref[...], dims))

    tile = pl.BlockSpec((tm, tn), lambda i, j, kk: (i, j))
    ins = [a, b] + ([add] if has_add else []) + ([loss_target] if has_loss else []) + ([into] if has_into else [])
    specs = [a_spec, b_spec] + [tile] * (has_add + has_loss) + ([ANY] if has_into else [])
    rows = into.shape[0] if has_into else (o_rows if o_rows is not None else M)
    if has_loss:
        out_specs = [tile, tile, pl.BlockSpec((1, 1, N), lambda i, j, kk: (i, 0, 0))]
        out_shape = [jax.ShapeDtypeStruct((M, N), F32), jax.ShapeDtypeStruct((M, N), BF16), jax.ShapeDtypeStruct((M // tm, 1, N), F32)]
    else:
        out_specs, out_shape = o_spec, jax.ShapeDtypeStruct((rows, N), out_dtype)
    return _call(
        body, name=name, grid=(M // tm, N // tn, nk), in_specs=specs, out_specs=out_specs, out_shape=out_shape,
        scratch_shapes=[pltpu.VMEM((tm, tn), F32)] if nk > 1 else [],
        input_output_aliases={len(ins) - 1: 0} if has_into else {},
        compiler_params=_params("parallel", "parallel", "arbitrary"),
    )(*ins)


def _mm_norm_bwd(parts, b, x, resid, g, name, tm=512, tk=512):
    T, N = x.shape
    counts = [p.shape[1] // tk for p in parts]
    starts = [sum(counts[:i]) for i in range(len(parts))]
    nsteps = sum(counts)
    assert all(p.shape[1] % tk == 0 for p in parts) and b.shape == (nsteps * tk, N)
    npart = len(parts)

    def body(*refs):
        a_refs, b_ref, x_ref, res_ref, g_ref = refs[:npart], refs[npart], refs[npart + 1], refs[npart + 2], refs[npart + 3]
        dx_ref, dxb_ref, dg_ref, acc = refs[npart + 4:]
        i, s = pl.program_id(0), pl.program_id(1)

        @pl.when((i == 0) & (s == 0))
        def _():
            dg_ref[...] = jnp.zeros_like(dg_ref)

        for p in range(npart):
            @pl.when((s >= starts[p]) & (s < starts[p] + counts[p]))
            def _(p=p):
                d = _dot(a_refs[p][...], b_ref[...])

                @pl.when(s == 0)
                def _():
                    acc[...] = d

                @pl.when(s > 0)
                def _():
                    acc[...] += d

        @pl.when(s == nsteps - 1)
        def _():
            xv, dhv = x_ref[...], acc[...]
            r = lax.rsqrt(jnp.mean(xv * xv, axis=-1, keepdims=True) + EPS)
            gd = dhv * g_ref[...]
            m = jnp.mean(gd * xv, axis=-1, keepdims=True)
            dx = res_ref[...] + r * gd - xv * (r * r * r) * m
            dx_ref[...] = dx
            dxb_ref[...] = dx.astype(BF16)
            dg_ref[...] += jnp.sum(dhv * xv * r, axis=0, keepdims=True)

    a_specs = [pl.BlockSpec((tm, tk), lambda i, s, st=st, c=c: (i, jnp.clip(s - st, 0, c - 1))) for st, c in zip(starts, counts)]
    row = pl.BlockSpec((tm, N), lambda i, s: (i, 0))
    vec = pl.BlockSpec((1, N), lambda i, s: (0, 0))
    return _call(
        body, name=name, grid=(T // tm, nsteps),
        in_specs=a_specs + [pl.BlockSpec((tk, N), lambda i, s: (s, 0)), row, row, vec], out_specs=[row, row, vec],
        out_shape=[jax.ShapeDtypeStruct((T, N), F32), jax.ShapeDtypeStruct((T, N), BF16), jax.ShapeDtypeStruct((1, N), F32)],
        scratch_shapes=[pltpu.VMEM((tm, N), F32)], compiler_params=_params("arbitrary", "arbitrary"),
    )(*parts, b, x, resid, g)


def _rmsnorm(x, g, name, tm=512):
    T, D = x.shape

    def body(x_ref, g_ref, o_ref):
        xv = x_ref[...]
        r = lax.rsqrt(jnp.mean(xv * xv, axis=-1, keepdims=True) + EPS)
        o_ref[...] = (xv * r * g_ref[...]).astype(BF16)

    return _call(
        body, name=name, grid=(T // tm,),
        in_specs=[pl.BlockSpec((tm, D), lambda i: (i, 0)), pl.BlockSpec((1, D), lambda i: (0, 0))],
        out_specs=pl.BlockSpec((tm, D), lambda i: (i, 0)), out_shape=jax.ShapeDtypeStruct((T, D), BF16),
        compiler_params=_params("parallel"),
    )(x, g)


def _qk_prep(proj, pos, invf, qg, kg, bd, name, tm=512):
    T = proj.shape[0]

    def body(q_ref, k_ref, pos_ref, invf_ref, qg_ref, kg_ref, bd_ref, qo_ref, ko_ref):
        cos, sin = _rope_tables(pos_ref, invf_ref)

        def prep(xv, gv, scale):
            r = lax.rsqrt(_group_mean(xv * xv, bd_ref[...]) + EPS)
            yv = xv * r * gv
            return ((yv * cos + _rot_half(yv) * sin) * scale).astype(BF16).astype(F32)

        qo_ref[...] = prep(q_ref[...], qg_ref[...], HEAD_DIM ** -0.5)
        ko_ref[...] = prep(k_ref[...], kg_ref[...], 1.0)

    col = lambda j: pl.BlockSpec((tm, ATTN_W), lambda i, j=j: (i, j))
    vec = pl.BlockSpec((1, ATTN_W), lambda i: (0, 0))
    out = pl.BlockSpec((tm, ATTN_W), lambda i: (i, 0))
    return _call(
        body, name=name, grid=(T // tm,),
        in_specs=[col(0), col(1), pl.BlockSpec((tm, 1), lambda i: (i, 0)), vec, vec, vec,
                  pl.BlockSpec((2 * HEAD_DIM, 2 * HEAD_DIM), lambda i: (0, 0))],
        out_specs=[out, out], out_shape=[jax.ShapeDtypeStruct((T, ATTN_W), F32)] * 2,
        compiler_params=_params("parallel"),
    )(proj, proj, pos, invf, qg, kg, bd)


def _qk_prep_bwd(proj, dqh, dkh, dv, pos, invf, qg, kg, bd, name, tm=512):
    T = proj.shape[0]

    def body(q_ref, k_ref, dq_ref, dk_ref, dv_ref, pos_ref, invf_ref, qg_ref, kg_ref, bd_ref, o_ref, gq_ref, gk_ref):
        @pl.when(pl.program_id(0) == 0)
        def _():
            gq_ref[...] = jnp.zeros_like(gq_ref)
            gk_ref[...] = jnp.zeros_like(gk_ref)

        cos, sin = _rope_tables(pos_ref, invf_ref)

        def back(xv, gv, dz, scale):
            dz = dz * scale
            dy = dz * cos - _rot_half(dz * sin)
            r = lax.rsqrt(_group_mean(xv * xv, bd_ref[...]) + EPS)
            gd = dy * gv
            m = _group_mean(gd * xv, bd_ref[...])
            dx = r * gd - xv * (r * r * r) * m
            return dx, jnp.sum(dy * xv * r, axis=0, keepdims=True)

        dxq, gs = back(q_ref[...], qg_ref[...], dq_ref[...], HEAD_DIM ** -0.5)
        gq_ref[...] += gs
        dxk, gs = back(k_ref[...], kg_ref[...], dk_ref[...], 1.0)
        gk_ref[...] += gs
        o_ref[...] = jnp.concatenate([dxq.astype(BF16), dxk.astype(BF16), dv_ref[...].astype(BF16)], axis=1)

    col = lambda j: pl.BlockSpec((tm, ATTN_W), lambda i, j=j: (i, j))
    row = pl.BlockSpec((tm, ATTN_W), lambda i: (i, 0))
    vec = pl.BlockSpec((1, ATTN_W), lambda i: (0, 0))
    return _call(
        body, name=name, grid=(T // tm,),
        in_specs=[col(0), col(1), row, row, row, pl.BlockSpec((tm, 1), lambda i: (i, 0)), vec, vec, vec,
                  pl.BlockSpec((2 * HEAD_DIM, 2 * HEAD_DIM), lambda i: (0, 0))],
        out_specs=[pl.BlockSpec((tm, 3 * ATTN_W), lambda i: (i, 0)), vec, vec],
        out_shape=[jax.ShapeDtypeStruct((T, 3 * ATTN_W), BF16)] + [jax.ShapeDtypeStruct((1, ATTN_W), F32)] * 2,
        compiler_params=_params("arbitrary"),
    )(proj, proj, dqh, dkh, dv, pos, invf, qg, kg, bd)


def _ld(ref, start, size, dil):
    return ref[pl.ds(start, size), :] if dil == 1 else ref[pl.ds(start, size, stride=dil), :]


def _st(ref, start, size, dil, val):
    if dil == 1:
        ref[pl.ds(start, size), :] = val
    else:
        ref[pl.ds(start, size, stride=dil), :] = val


def _attn_geometry(T, dil):
    nb = T // dil // QBLK
    if nb == 2:
        return 1, 2 * QBLK, 2 * QBLK
    return nb, QBLK, (2 * QBLK if nb >= 2 else QBLK)


ATTN_UNROLL = 4


def _attn_unit(j, u, dil, nit):
    return ATTN_UNROLL * j + u if dil >= ATTN_UNROLL else j + u * (nit // ATTN_UNROLL)


def _attn_block(it, dil, qb, kw):
    c, n = it & (dil - 1), lax.shift_right_logical(it, dil.bit_length() - 1)
    sq = n * (qb * dil) + c
    sk = jnp.maximum(n - (kw // qb - 1), 0) * (qb * dil) + c
    qi = lax.broadcasted_iota(jnp.int32, (2 * qb, kw), 0) & (qb - 1)
    kj = lax.broadcasted_iota(jnp.int32, (2 * qb, kw), 1)
    rel = jnp.where(n > 0, kw - qb, 0) + qi - kj
    return sq, sk, (rel >= 0) & (rel <= QBLK)


def _stack_heads(xv, head0):
    z = jnp.zeros_like(xv)
    return jnp.concatenate([jnp.where(head0, xv, z), jnp.where(head0, z, xv)], axis=0)


def _unstack_heads(x2, head0):
    qb = x2.shape[0] // 2
    return jnp.where(head0, x2[:qb], x2[qb:])


def _attn_fwd(qf, kf, proj, name):
    T = qf.shape[0]

    def body(q_ref, k_ref, v_ref, o_ref, lse_ref):
        for bi, dil in enumerate(DILATIONS):
            nb, qb, kw = _attn_geometry(T, dil)
            nit = nb * dil
            head0 = lax.broadcasted_iota(jnp.int32, (qb, 2 * HEAD_DIM), 1) < HEAD_DIM

            def step(j, carry, bi=bi, dil=dil, qb=qb, kw=kw, nit=nit, head0=head0):
                units = []
                for u in range(ATTN_UNROLL):
                    sq, sk, ok = _attn_block(_attn_unit(j, u, dil, nit), dil, qb, kw)
                    old = (_ld(o_ref, sq, qb, dil), _ld(lse_ref, sq, qb, dil)) if bi > 0 else None
                    units.append((sq, ok, _ld(q_ref, sq, qb, dil).astype(BF16), _ld(k_ref, sk, kw, dil).astype(BF16),
                                  _ld(v_ref, sk, kw, dil).astype(BF16), old))
                results = []
                for sq, ok, qv, kv, vv, old in units:
                    s = jnp.where(ok, _dot(_stack_heads(qv, head0), kv, _NT), NEG_INF)
                    m = jnp.max(s, axis=-1, keepdims=True)
                    p = jnp.exp(s - m).astype(BF16)
                    acc = _dot(p, jnp.concatenate([vv, jnp.ones_like(vv)], axis=1))
                    l = acc[:, 2 * HEAD_DIM:]
                    o_new = _unstack_heads(acc[:, :2 * HEAD_DIM] / l, head0)
                    l_new = _unstack_heads(m + jnp.log(l), head0)
                    if bi > 0:
                        o_old, l_old = old
                        mx = jnp.maximum(l_old, l_new)
                        e0, e1 = jnp.exp(l_old - mx), jnp.exp(l_new - mx)
                        z = e0 + e1
                        o_new = (e0 * o_old + e1 * o_new) / z
                        l_new = mx + jnp.log(z)
                    results.append((sq, o_new, l_new))
                for sq, o_new, l_new in results:
                    _st(o_ref, sq, qb, dil, o_new)
                    _st(lse_ref, sq, qb, dil, l_new)
                return carry

            lax.fori_loop(0, nit // ATTN_UNROLL, step, 0)

    blk = lambda off: pl.BlockSpec((T, 2 * HEAD_DIM), lambda hp, off=off: (0, off + hp))
    return _call(
        body, name=name, grid=(4,), in_specs=[blk(0), blk(0), blk(8)], out_specs=[blk(0), blk(0)],
        out_shape=[jax.ShapeDtypeStruct((T, ATTN_W), F32)] * 2, compiler_params=_params("parallel"),
    )(qf, kf, proj)


def _attn_bwd(qf, kf, proj, do, lse, delta, name):
    T = qf.shape[0]

    def body(q_ref, k_ref, v_ref, do_ref, lse_ref, dl_ref, dq_ref, dk_ref, dv_ref):
        for ref in (dq_ref, dk_ref, dv_ref):
            ref[...] = jnp.zeros_like(ref)
        for dil in DILATIONS:
            nb, qb, kw = _attn_geometry(T, dil)
            nit = nb * dil
            head0 = lax.broadcasted_iota(jnp.int32, (qb, 2 * HEAD_DIM), 1) < HEAD_DIM

            def step(j, carry, dil=dil, qb=qb, kw=kw, nit=nit, head0=head0):
                units = []
                for u in range(ATTN_UNROLL):
                    sq, sk, ok = _attn_block(_attn_unit(j, u, dil, nit), dil, qb, kw)
                    lsev, dlv = _ld(lse_ref, sq, qb, dil), _ld(dl_ref, sq, qb, dil)
                    units.append((sq, sk, ok, _ld(q_ref, sq, qb, dil).astype(BF16), _ld(do_ref, sq, qb, dil).astype(BF16),
                                  jnp.concatenate([lsev[:, 0:1], lsev[:, HEAD_DIM:HEAD_DIM + 1]], axis=0),
                                  jnp.concatenate([dlv[:, 0:1], dlv[:, HEAD_DIM:HEAD_DIM + 1]], axis=0),
                                  _ld(k_ref, sk, kw, dil).astype(BF16), _ld(v_ref, sk, kw, dil).astype(BF16),
                                  _ld(dq_ref, sq, qb, dil), _ld(dk_ref, sk, kw, dil), _ld(dv_ref, sk, kw, dil)))
                results = []
                for sq, sk, ok, qv, dov, lse2, dl2, kv, vv, dq0, dk0, dv0 in units:
                    q2, do2 = _stack_heads(qv, head0), _stack_heads(dov, head0)
                    p = jnp.where(ok, jnp.exp(_dot(q2, kv, _NT) - lse2), 0.0)
                    ds = (p * (_dot(do2, vv, _NT) - dl2)).astype(BF16)
                    results.append((sq, sk, dq0 + _unstack_heads(_dot(ds, kv), head0),
                                    dk0 + _dot(ds, q2, _TN), dv0 + _dot(p.astype(BF16), do2, _TN)))
                for sq, sk, dq, dk, dv in results:
                    _st(dq_ref, sq, qb, dil, dq)
                    _st(dk_ref, sk, kw, dil, dk)
                    _st(dv_ref, sk, kw, dil, dv)
                return carry

            lax.fori_loop(0, nit // ATTN_UNROLL, step, 0)

    blk = lambda off: pl.BlockSpec((T, 2 * HEAD_DIM), lambda hp, off=off: (0, off + hp))
    return _call(
        body, name=name, grid=(4,), in_specs=[blk(0), blk(0), blk(8), blk(0), blk(0), blk(0)], out_specs=[blk(0)] * 3,
        out_shape=[jax.ShapeDtypeStruct((T, ATTN_W), F32)] * 3, compiler_params=_params("parallel"),
    )(qf, kf, proj, do, lse, delta)


def _attn_norm(attn, g, name, tm=512):
    T = attn.shape[0]

    def body(a_ref, g_ref, o_ref):
        av = a_ref[...]
        r = lax.rsqrt(jnp.mean(av * av, axis=-1, keepdims=True) + EPS)
        o_ref[...] = (av * r * g_ref[...]).astype(BF16)

    row = pl.BlockSpec((tm, ATTN_W), lambda i: (i, 0))
    return _call(
        body, name=name, grid=(T // tm,), in_specs=[row, pl.BlockSpec((1, ATTN_W), lambda i: (0, 0))], out_specs=row,
        out_shape=jax.ShapeDtypeStruct((T, 2 * ATTN_W), BF16), compiler_params=_params("parallel"),
    )(attn, g)


def _attn_norm_bwd(dmix, attn, g, bd, name, tm=512):
    T = attn.shape[0]

    def body(d_ref, a_ref, g_ref, bd_ref, do_ref, dl_ref, dg_ref):
        @pl.when(pl.program_id(0) == 0)
        def _():
            dg_ref[...] = jnp.zeros_like(dg_ref)

        dy, av = d_ref[...], a_ref[...]
        r = lax.rsqrt(jnp.mean(av * av, axis=-1, keepdims=True) + EPS)
        gd = dy * g_ref[...]
        m = jnp.mean(gd * av, axis=-1, keepdims=True)
        da = r * gd - av * (r * r * r) * m
        do_ref[...] = da
        dl_ref[...] = _group_mean(da * av, bd_ref[...]) * float(HEAD_DIM)
        dg_ref[...] += jnp.sum(dy * av * r, axis=0, keepdims=True)

    row = pl.BlockSpec((tm, ATTN_W), lambda i: (i, 0))
    vec = pl.BlockSpec((1, ATTN_W), lambda i: (0, 0))
    return _call(
        body, name=name, grid=(T // tm,),
        in_specs=[row, row, vec, pl.BlockSpec((2 * HEAD_DIM, 2 * HEAD_DIM), lambda i: (0, 0))], out_specs=[row, row, vec],
        out_shape=[jax.ShapeDtypeStruct((T, ATTN_W), F32)] * 2 + [jax.ShapeDtypeStruct((1, ATTN_W), F32)],
        compiler_params=_params("arbitrary"),
    )(dmix, attn, g, bd)


def _rec_gates(xc, wrg_ref, wig_ref, brg_ref, big_ref, lam_ref):
    xb = xc.astype(BF16)
    r = _sigmoid(_dot(xb, wrg_ref[...]) + brg_ref[...])
    ig = _sigmoid(_dot(xb, wig_ref[...]) + big_ref[...])
    sp = _softplus_neg(lam_ref[...])
    log_a = -LRU_C * r * sp
    a = jnp.exp(log_a)
    th = jnp.tanh(log_a)
    mult = jnp.sqrt(-2.0 * th / (1.0 - th))
    return xb, r, ig, sp, a, mult


def _rec_fwd(proj, mix, cw, cb, wrg, wig, brg, big, lam, g, name, tm=256):
    T = proj.shape[0]
    hb = tm // 8

    def body(xr_ref, halo_ref, gr_ref, cw_ref, cb_ref, wrg_ref, wig_ref, brg_ref, big_ref, lam_ref, g_ref, mix_ref,
             xc_ref, h_ref, out_ref, carry):
        i = pl.program_id(0)

        @pl.when(i == 0)
        def _():
            carry[...] = jnp.zeros_like(carry)

        xr = xr_ref[...]
        halo = jnp.where(i > 0, halo_ref[...], 0.0)
        xc = cb_ref[...] + cw_ref[3:4, :] * xr
        for s in range(1, REC_CONV):
            xc = xc + cw_ref[3 - s:4 - s, :] * _shift_down(xr, halo, s)
        xc_ref[...] = xc
        _, _, ig, _, a, mult = _rec_gates(xc, wrg_ref, wig_ref, brg_ref, big_ref, lam_ref)
        pa, hl = _scan_fwd(a, mult * (ig * xc))
        h = hl + pa * carry[0:1, :]
        h_ref[...] = h
        carry[0:1, :] = h_ref[pl.ds(tm - 1, 1), :]
        hg = h * _gelu(gr_ref[...])
        r = lax.rsqrt(jnp.mean(hg * hg, axis=-1, keepdims=True) + EPS)
        out_ref[...] = (hg * r * g_ref[...]).astype(BF16)

    vec = pl.BlockSpec((1, REC_W), lambda i: (0, 0))
    row = pl.BlockSpec((tm, REC_W), lambda i: (i, 0))
    mat = pl.BlockSpec((REC_W, REC_W), lambda i: (0, 0))
    return _call(
        body, name=name, grid=(T // tm,),
        in_specs=[pl.BlockSpec((tm, REC_W), lambda i: (i, 3)),
                  pl.BlockSpec((8, REC_W), lambda i: (jnp.maximum(i * hb - 1, 0), 3)),
                  pl.BlockSpec((tm, REC_W), lambda i: (i, 4)),
                  pl.BlockSpec((8, REC_W), lambda i: (0, 0)), vec, mat, mat, vec, vec, vec, vec, ANY],
        out_specs=[row, row, pl.BlockSpec((tm, REC_W), lambda i: (i, 1))],
        out_shape=[jax.ShapeDtypeStruct((T, REC_W), F32)] * 2 + [jax.ShapeDtypeStruct(mix.shape, BF16)],
        scratch_shapes=[pltpu.VMEM((8, REC_W), F32)], input_output_aliases={11: 2},
        compiler_params=_params("arbitrary"),
    )(proj, proj, proj, cw, cb, wrg, wig, brg, big, lam, g, mix)


def _rec_bwd(dmix, proj, xc, h, cw, cb, wrg, wig, brg, big, lam, g, name, tm=256):
    T = proj.shape[0]
    nt = T // tm
    hb = tm // 8

    def body(d_ref, xr_ref, xhalo_ref, gr_ref, xc_ref, h_ref, hhalo_ref, cw_ref, cb_ref, wrg_ref, wig_ref, brg_ref,
             big_ref, lam_ref, g_ref,
             drec_ref, gcw_ref, gcb_ref, gwrg_ref, gwig_ref, gbrg_ref, gbig_ref, glam_ref, gg_ref,
             g_carry, a_first, dxc_next, gsp):
        i = pl.program_id(0)
        first_tile = i == nt - 1

        @pl.when(i == 0)
        def _():
            for ref in (gcw_ref, gcb_ref, gwrg_ref, gwig_ref, gbrg_ref, gbig_ref, glam_ref, gg_ref,
                        g_carry, a_first, dxc_next, gsp):
                ref[...] = jnp.zeros_like(ref)

        xr, xc, hv = xr_ref[...], xc_ref[...], h_ref[...]
        xhalo = jnp.where(first_tile, 0.0, xhalo_ref[...])
        hhalo = jnp.where(first_tile, 0.0, hhalo_ref[...])
        xb, r, ig, sp, a, mult = _rec_gates(xc, wrg_ref, wig_ref, brg_ref, big_ref, lam_ref)
        h_prev = _shift_down(hv, hhalo, 1)
        ge, dge = _gelu_and_grad(gr_ref[...])
        hg = hv * ge
        rr = lax.rsqrt(jnp.mean(hg * hg, axis=-1, keepdims=True) + EPS)
        dy = d_ref[...]
        gd = dy * g_ref[...]
        dhg = rr * gd - hg * (rr * rr * rr) * jnp.mean(gd * hg, axis=-1, keepdims=True)
        gg_ref[...] += jnp.sum(dy * hg * rr, axis=0, keepdims=True)
        dgr = (dhg * hv * dge).astype(BF16)
        dh = dhg * ge
        b = _shift_up(a, jnp.broadcast_to(a_first[0:1, :], (8, REC_W)), 1)
        pb, gl = _scan_bwd(b, dh)
        gs = gl + pb * g_carry[0:1, :]
        g_carry[0:1, :] = gs[0:1, :]
        a_first[0:1, :] = a[0:1, :]
        da = gs * h_prev
        dmult = gs * (ig * xc)
        di = gs * (mult * xc)
        dxc = gs * (mult * ig)
        dlog_a = da * a - dmult * (a * a) / mult
        gsp[...] += jnp.sum(dlog_a * (-LRU_C * r), axis=0, keepdims=True)
        dzr = (dlog_a * (-LRU_C * sp)) * (r * (1.0 - r))
        dzi = di * (ig * (1.0 - ig))
        dzr_b, dzi_b = dzr.astype(BF16), dzi.astype(BF16)
        dxc = dxc + _dot(dzr_b, wrg_ref[...], _NT) + _dot(dzi_b, wig_ref[...], _NT)
        gwrg_ref[...] += _dot(xb, dzr_b, _TN)
        gwig_ref[...] += _dot(xb, dzi_b, _TN)
        gbrg_ref[...] += jnp.sum(dzr, axis=0, keepdims=True)
        gbig_ref[...] += jnp.sum(dzi, axis=0, keepdims=True)
        nxt = dxc_next[...]
        dxr = cw_ref[3:4, :] * dxc
        gcw_ref[3:4, :] += jnp.sum(dxc * xr, axis=0, keepdims=True)
        for s in range(1, REC_CONV):
            dxr = dxr + cw_ref[3 - s:4 - s, :] * _shift_up(dxc, nxt, s)
            gcw_ref[3 - s:4 - s, :] += jnp.sum(dxc * _shift_down(xr, xhalo, s), axis=0, keepdims=True)
        gcb_ref[...] += jnp.sum(dxc, axis=0, keepdims=True)
        dxc_next[...] = dxc[:8]
        drec_ref[...] = jnp.concatenate([dxr.astype(BF16), dgr], axis=1)

        @pl.when(first_tile)
        def _():
            glam_ref[...] = gsp[...] * (-_sigmoid(-lam_ref[...]))

    rev = lambda i: nt - 1 - i
    vec = pl.BlockSpec((1, REC_W), lambda i: (0, 0))
    row = pl.BlockSpec((tm, REC_W), lambda i: (rev(i), 0))
    mat = pl.BlockSpec((REC_W, REC_W), lambda i: (0, 0))
    cwb = pl.BlockSpec((8, REC_W), lambda i: (0, 0))
    halo = lambda c: pl.BlockSpec((8, REC_W), lambda i, c=c: (jnp.maximum(rev(i) * hb - 1, 0), c))
    return _call(
        body, name=name, grid=(nt,),
        in_specs=[pl.BlockSpec((tm, REC_W), lambda i: (rev(i), 1)),
                  pl.BlockSpec((tm, REC_W), lambda i: (rev(i), 3)), halo(3),
                  pl.BlockSpec((tm, REC_W), lambda i: (rev(i), 4)),
                  row, row, halo(0), cwb, vec, mat, mat, vec, vec, vec, vec],
        out_specs=[pl.BlockSpec((tm, 2 * REC_W), lambda i: (rev(i), 0)), cwb, vec, mat, mat, vec, vec, vec, vec],
        out_shape=[jax.ShapeDtypeStruct((T, 2 * REC_W), BF16)]
        + [jax.ShapeDtypeStruct((8, REC_W), F32), jax.ShapeDtypeStruct((1, REC_W), F32)]
        + [jax.ShapeDtypeStruct((REC_W, REC_W), F32)] * 2 + [jax.ShapeDtypeStruct((1, REC_W), F32)] * 4,
        scratch_shapes=[pltpu.VMEM((8, REC_W), F32)] * 3 + [pltpu.VMEM((1, REC_W), F32)],
        compiler_params=_params("arbitrary"),
    )(dmix, proj, proj, proj, xc, h, h, cw, cb, wrg, wig, brg, big, lam, g)


def _ffn_conv(x_ext, cw_ref, cb_ref):
    return (cb_ref[...] + cw_ref[2:3, :] * x_ext + cw_ref[1:2, :] * pltpu.roll(x_ext, 1, 0)
            + cw_ref[0:1, :] * pltpu.roll(x_ext, 2, 0))


def _up_proj_act(h2, w_upT, cw, cb, name, tm=1024, tc=768):
    T = h2.shape[0]
    nc = D_FF // tc

    def body(h_ref, wg_ref, wu_ref, cwg_ref, cwu_ref, cbg_ref, cbu_ref, act_ref, da_ref, db_ref, pg_ref, pu_ref,
             hist_g, hist_u):
        i, j = pl.program_id(0), pl.program_id(1)
        hv = h_ref[...]
        pg, pu = _dot(hv, wg_ref[...], _NT), _dot(hv, wu_ref[...], _NT)
        ge = jnp.concatenate([jnp.where(i > 0, hist_g[j], 0.0), pg], axis=0)
        ue = jnp.concatenate([jnp.where(i > 0, hist_u[j], 0.0), pu], axis=0)
        gel, dgel = _gelu_and_grad(_ffn_conv(ge, cwg_ref, cbg_ref)[8:])
        uu = _ffn_conv(ue, cwu_ref, cbu_ref)[8:]
        act_ref[...] = (gel * uu).astype(BF16)
        da_ref[...] = (uu * dgel).astype(BF16)
        db_ref[...] = gel.astype(BF16)
        pg_ref[...] = pg.astype(BF16)
        pu_ref[...] = pu.astype(BF16)
        hist_g[j] = pg[tm - 8:]
        hist_u[j] = pu[tm - 8:]

    tile = pl.BlockSpec((tm, tc), lambda i, j: (i, j))
    wsp = lambda off: pl.BlockSpec((tc, D_MODEL), lambda i, j, off=off: (j + off, 0))
    cws = lambda off: pl.BlockSpec((8, tc), lambda i, j, off=off: (0, j + off))
    cbs = lambda off: pl.BlockSpec((1, tc), lambda i, j, off=off: (0, j + off))
    return _call(
        body, name=name, grid=(T // tm, nc),
        in_specs=[pl.BlockSpec((tm, D_MODEL), lambda i, j: (i, 0)), wsp(0), wsp(nc), cws(0), cws(nc), cbs(0), cbs(nc)],
        out_specs=[tile] * 5, out_shape=[jax.ShapeDtypeStruct((T, D_FF), BF16)] * 5,
        scratch_shapes=[pltpu.VMEM((nc, 8, tc), F32)] * 2, compiler_params=_params("arbitrary", "arbitrary"),
    )(h2, w_upT, w_upT, cw, cw, cb, cb)


def _ffn_bwd(dyb, w_down, da, db, pg, pu, cw, name, tm=512, tc=768):
    T, F = pg.shape
    nt = T // tm
    hb16 = tm // 16
    nc = F // tc
    n = tm + 8

    def body(dy_ref, dyn_ref, wd_ref, a_ref, an_ref, b_ref, bn_ref, g_ref, u_ref, cwg_ref, cwu_ref,
             dg_ref, du_ref, gcwg_ref, gcwu_ref, gcbg_ref, gcbu_ref):
        i = pl.program_id(1)
        last = i == nt - 1

        @pl.when(i == 0)
        def _():
            for ref in (gcwg_ref, gcwu_ref, gcbg_ref, gcbu_ref):
                ref[...] = jnp.zeros_like(ref)

        wd = wd_ref[...]
        dact_next = jnp.where(last, 0.0, _dot(dyn_ref[...], wd, _NT)[:8])
        de = jnp.concatenate([_dot(dy_ref[...], wd, _NT), dact_next], axis=0)
        ext = lambda t, nx: jnp.concatenate([t[...].astype(F32), nx[...].astype(F32)[:8]], axis=0)
        for dcv, x_ref, cw_ref, dx_ref, gcw_ref, gcb_ref in ((de * ext(a_ref, an_ref), g_ref, cwg_ref, dg_ref, gcwg_ref, gcbg_ref),
                                                               (de * ext(b_ref, bn_ref), u_ref, cwu_ref, du_ref, gcwu_ref, gcbu_ref)):
            s1, s2 = pltpu.roll(dcv, n - 1, 0), pltpu.roll(dcv, n - 2, 0)
            dx_ref[...] = (cw_ref[2:3, :] * dcv + cw_ref[1:2, :] * s1 + cw_ref[0:1, :] * s2)[:tm].astype(BF16)
            xv = x_ref[...].astype(F32)
            gcw_ref[2:3, :] += jnp.sum(xv * dcv[:tm], axis=0, keepdims=True)
            gcw_ref[1:2, :] += jnp.sum(xv * s1[:tm], axis=0, keepdims=True)
            gcw_ref[0:1, :] += jnp.sum(xv * s2[:tm], axis=0, keepdims=True)
            gcb_ref[...] += jnp.sum(dcv[:tm], axis=0, keepdims=True)

    tile = pl.BlockSpec((tm, tc), lambda j, i: (i, j))
    nxt = pl.BlockSpec((16, tc), lambda j, i: (jnp.minimum((i + 1) * hb16, nt * hb16 - 1), j))
    cws = lambda off: pl.BlockSpec((8, tc), lambda j, i, off=off: (0, j + off))
    cbs = pl.BlockSpec((1, tc), lambda j, i: (0, j))
    return _call(
        body, name=name, grid=(nc, nt),
        in_specs=[pl.BlockSpec((tm, D_MODEL), lambda j, i: (i, 0)),
                  pl.BlockSpec((16, D_MODEL), lambda j, i: (jnp.minimum((i + 1) * hb16, nt * hb16 - 1), 0)),
                  pl.BlockSpec((tc, D_MODEL), lambda j, i: (j, 0)), tile, nxt, tile, nxt, tile, tile, cws(0), cws(nc)],
        out_specs=[tile, tile, cws(0), cws(0), cbs, cbs],
        out_shape=[jax.ShapeDtypeStruct((T, F), BF16)] * 2 + [jax.ShapeDtypeStruct((8, F), F32)] * 2
        + [jax.ShapeDtypeStruct((1, F), F32)] * 2,
        compiler_params=_params("parallel", "arbitrary"),
    )(dyb, dyb, w_down, da, da, db, db, pg, pu, cw, cw)


def _adam_update(w, g, m, v):
    m2 = ADAM_B1 * m + (1.0 - ADAM_B1) * g
    v2 = ADAM_B2 * v + (1.0 - ADAM_B2) * (g * g)
    m_hat = m2 / (1.0 - ADAM_B1 ** ADAM_STEP)
    v_hat = v2 / (1.0 - ADAM_B2 ** ADAM_STEP)
    delta = -ADAM_LR * (m_hat / (jnp.sqrt(v_hat) + ADAM_EPS) + ADAM_WD * w)
    return delta, m2, v2


def _adam_sharded(p, r2, idx, w, m, v, name, transposed=False):
    r, n = p.shape[1:]
    nrecv = r2.shape[0]
    tr = (256 if r % 256 == 0 else r) if transposed else _row_tile(r)

    def body(c_ref, p_ref, r_ref, w_ref, m_ref, v_ref, g_ref, d_ref, m2_ref, v2_ref):
        g = p_ref[...].astype(F32)
        for k in range(nrecv):
            g = g + r_ref[k].astype(F32)
        if transposed:
            g = g.T
        g_ref[...] = g
        d_ref[...], m2_ref[...], v2_ref[...] = _adam_update(w_ref[...], g, m_ref[...], v_ref[...])

    blk = pl.BlockSpec((n, tr), lambda i, c_ref: (0, i)) if transposed else pl.BlockSpec((tr, n), lambda i, c_ref: (i, 0))
    spec = pltpu.PrefetchScalarGridSpec(
        num_scalar_prefetch=1, grid=(r // tr,),
        in_specs=[pl.BlockSpec((None, tr, n), lambda i, c_ref: (c_ref[0], i, 0)),
                  pl.BlockSpec((nrecv, tr, n), lambda i, c_ref: (0, i, 0)), blk, blk, blk],
        out_specs=[blk] * 4)
    return _call(body, name=name, grid_spec=spec, out_shape=[jax.ShapeDtypeStruct(w.shape, F32)] * 4,
                 compiler_params=_params("parallel"))(idx, p, r2, w, m, v)


def _sum_devices(allg, name):
    r, n = allg.shape[0] // N_DEV, allg.shape[1]

    def body(a_ref, o_ref):
        acc = a_ref[0:r, :]
        for k in range(1, N_DEV):
            acc = acc + a_ref[k * r:(k + 1) * r, :]
        o_ref[...] = acc

    return _call(body, name=name, out_shape=jax.ShapeDtypeStruct((r, n), F32))(allg)


def _adam_small(ws, gs, ms, vs, name):
    n = len(ws)

    def body(*refs):
        for i in range(n):
            d, m2, v2 = _adam_update(refs[i][...], refs[n + i][...], refs[2 * n + i][...], refs[3 * n + i][...])
            refs[4 * n + i][...] = d
            refs[5 * n + i][...] = m2
            refs[6 * n + i][...] = v2

    outs = _call(body, name=name, out_shape=[jax.ShapeDtypeStruct(w.shape, F32) for w in ws] * 3)(*ws, *gs, *ms, *vs)
    return outs[:n], outs[n:2 * n], outs[2 * n:]


_SMALL = (("g_mix", 1024), ("q_norm_g", 64), ("k_norm_g", 64), ("rec_conv_b", 512), ("w_rg", 32768), ("b_rg", 512),
          ("w_ig", 32768), ("b_ig", 512), ("lru_lambda", 512), ("g_attn_out", 512), ("g_rec_out", 512),
          ("g_ffn", 1024), ("ffn_conv_b", 6144))
_SMALL_SHAPES = {"g_mix": (1, 1024), "q_norm_g": (1, 64), "k_norm_g": (1, 64), "rec_conv_b": (1, 512),
                 "w_rg": (1, 8, 64, 64), "b_rg": (1, 8, 64), "w_ig": (1, 8, 64, 64), "b_ig": (1, 8, 64),
                 "lru_lambda": (1, 512), "g_attn_out": (1, 512), "g_rec_out": (1, 512), "g_ffn": (1, 1024),
                 "ffn_conv_b": (1, 6144)}


def _block_diag(w):
    eye = jnp.eye(8, dtype=w.dtype)
    return (w[:, :, None, :] * eye[:, None, :, None]).reshape(512, 512)


def kernel(x, positions, g_mix, w_in, q_norm_g, k_norm_g, rec_conv_w, rec_conv_b, w_rg, b_rg, w_ig, b_ig, lru_lambda, g_attn_out, g_rec_out, w_out, g_ffn, w_up, ffn_conv_w, ffn_conv_b, w_down, loss_target, m_g_mix, m_w_in, m_q_norm_g, m_k_norm_g, m_rec_conv_w, m_rec_conv_b, m_w_rg, m_b_rg, m_w_ig, m_b_ig, m_lru_lambda, m_g_attn_out, m_g_rec_out, m_w_out, m_g_ffn, m_w_up, m_ffn_conv_w, m_ffn_conv_b, m_w_down, v_g_mix, v_w_in, v_q_norm_g, v_k_norm_g, v_rec_conv_w, v_rec_conv_b, v_w_rg, v_b_rg, v_w_ig, v_b_ig, v_lru_lambda, v_g_attn_out, v_g_rec_out, v_w_out, v_g_ffn, v_w_up, v_ffn_conv_w, v_ffn_conv_b, v_w_down):
    T = x.shape[1]
    ix, iy, ic = lax.axis_index("x"), lax.axis_index("y"), lax.axis_index("c")
    dev = 4 * ix + 2 * iy + ic
    xs = x.reshape(T, D_MODEL)
    tgt = loss_target.reshape(T, D_MODEL)
    pos = positions.reshape(T, 1)

    shards = {"w_in": (w_in[0], m_w_in[0], v_w_in[0]), "w_out": (w_out[0], m_w_out[0], v_w_out[0]),
              "w_up": (w_up[0], m_w_up[0], v_w_up[0]), "w_down": (w_down[0], m_w_down[0], v_w_down[0])}
    taps = jnp.concatenate([rec_conv_w.reshape(-1), ffn_conv_w.reshape(-1), jnp.zeros((4096 - 2560,), F32)]).reshape(8, 512)
    W_inT, taps_all = _all_gather([w_in[0].T.astype(BF16), taps], "ag_w_in")
    late = [w_out[0].astype(BF16), w_up[0].T.astype(BF16), w_down[0].astype(BF16)]
    ag_send, ag_recv, late_thru, land_thru, ag_token = _exchange_start(
        late, [_landing((N_DEV * s.shape[0], 1024), BF16, s, dev * s.shape[0]) for s in late], "gather", taps_all,
        "ag_late_start")
    taps_all = taps_all.reshape(N_DEV, 4096)
    rcw = taps_all[:, :256].reshape(8, 4, 64).transpose(1, 0, 2).reshape(4, REC_W)
    fcw = taps_all[:, 256:2560].reshape(8, 3, 768).transpose(1, 0, 2).reshape(3, 2 * D_FF)
    rcw8 = jnp.pad(rcw, ((0, 4), (0, 0)))
    fcw8 = jnp.pad(fcw, ((0, 5), (0, 0)))
    fcb = ffn_conv_b.reshape(1, 2 * D_FF)

    half = HEAD_DIM // 2
    inv_freq = ROPE_THETA ** (-jnp.arange(half, dtype=F32) / half)
    invf = jnp.tile(inv_freq, 2 * N_HEADS).reshape(1, ATTN_W)
    bd = jnp.asarray(np.kron(np.eye(2), np.full((HEAD_DIM, HEAD_DIM), 1.0 / HEAD_DIM)), BF16)
    qg = jnp.tile(q_norm_g.reshape(HEAD_DIM), N_HEADS).reshape(1, ATTN_W)
    kg = jnp.tile(k_norm_g.reshape(HEAD_DIM), N_HEADS).reshape(1, ATTN_W)
    wrg_bd = _block_diag(w_rg[0]).astype(BF16)
    wig_bd = _block_diag(w_ig[0]).astype(BF16)
    brg, big = b_rg.reshape(1, REC_W), b_ig.reshape(1, REC_W)

    h1 = _rmsnorm(xs, g_mix + ag_token[0, 0], "norm_mix")
    proj = _mm(h1, W_inT, "nt", F32, "in_proj", tn=1280)
    qf, kf = _qk_prep(proj, pos, invf, qg, kg, bd, "qk_prep")
    attn, lse = _attn_fwd(qf, kf, proj, "attn_fwd")
    mix = _attn_norm(attn, g_attn_out, "attn_norm")
    xc, hstate, mix = _rec_fwd(proj, mix, rcw8, rec_conv_b, wrg_bd, wig_bd, brg, big, lru_lambda, g_rec_out, "rec_fwd")
    _, (W_out, W_upT, W_down) = _exchange_wait(ag_send, ag_recv, late_thru, land_thru, "gather", hstate, "ag_late_wait")
    x2 = _mm(mix, W_out, "nn", F32, "out_proj", add=xs)

    h2 = _rmsnorm(x2, g_ffn, "norm_ffn")
    act, da, db, pg, pu = _up_proj_act(h2, W_upT, fcw8, fcb, "up_proj_act")
    dy, dyb, lparts = _mm(act, W_down, "nn", F32, "down_proj_loss", add=x2, loss_target=tgt, tm=512, tk=D_FF)
    loss_mine = 0.5 / D_MODEL * jnp.sum(lparts)

    g_down = _mm(act, dyb, "tn", BF16, "g_w_down", tk=2048)
    dpg, dpu, g_fcwg, g_fcwu, g_fcbg, g_fcbu = _ffn_bwd(dyb, W_down, da, db, pg, pu, fcw8, "ffn_bwd")
    g_upT = _mm(dpg, h2, "tn", BF16, "g_w_up_gate", tk=2048, o_rows=2 * D_FF)
    g_upT = _mm(dpu, h2, "tn", BF16, "g_w_up_up", tk=2048, into=g_upT, o_moff=D_FF // 1024)
    ffn_g = [g_upT.reshape(N_DEV, 2 * D_FF // N_DEV, 1024), g_down.reshape(N_DEV, D_FF // N_DEV, 1024)]
    rs_send, rs_recv, ffn_g, ffn_land, rs_token = _exchange_start(
        ffn_g, [_landing((N_PEERS,) + g.shape[1:], BF16) for g in ffn_g], "scatter", dpu, "rs_ffn_start")
    dx2, dx2b, g_gffn = _mm_norm_bwd([dpg, dpu], W_upT, x2, dy, g_ffn + rs_token[0, 0], "d_h2_norm_bwd", tm=1024, tk=1024)

    dmix = _mm(dx2b, W_out, "nt", F32, "d_mix")
    g_out = _mm(mix, dx2b, "tn", BF16, "g_w_out", tk=2048).reshape(N_DEV, D_MODEL // N_DEV, 1024)
    out_send, out_recv, (g_out,), out_land, out_token = _exchange_start(
        [g_out], [_landing((N_PEERS,) + g_out.shape[1:], BF16)], "scatter", dmix, "rs_out_start")
    do, delta, g_gattn = _attn_norm_bwd(dmix, attn, g_attn_out + out_token[0, 0], bd, "attn_norm_bwd")
    dqh, dkh, dv = _attn_bwd(qf, kf, proj, do, lse, delta, "attn_bwd")
    dqkv, g_qg, g_kg = _qk_prep_bwd(proj, dqh, dkh, dv, pos, invf, qg, kg, bd, "qk_prep_bwd")
    (drec, g_rcw, g_rcb, g_wrg, g_wig, g_brg, g_big, g_lam, g_grec) = _rec_bwd(
        dmix, proj, xc, hstate, rcw8, rec_conv_b, wrg_bd, wig_bd, brg, big, lru_lambda, g_rec_out, "rec_bwd")
    g_inT = _mm(dqkv, h1, "tn", BF16, "g_w_in_qkv", tm=512, o_rows=IN_W)
    g_inT = _mm(drec, h1, "tn", BF16, "g_w_in_rec", tm=512, into=g_inT, o_moff=3 * ATTN_W // 512)
    g_inT = g_inT.reshape(N_DEV, IN_W // N_DEV, 1024)
    in_send, in_recv, (g_inT,), in_land, in_token = _exchange_start(
        [g_inT], [_landing((N_PEERS,) + g_inT.shape[1:], BF16)], "scatter", drec, "rs_in_start")
    grad_x, _, g_gmix = _mm_norm_bwd([dqkv, drec], W_inT, xs, dx2, g_mix + in_token[0, 0], "d_h1_norm_bwd", tm=1024, tk=512)

    blocks = lambda g: jnp.stack([g[64 * n:64 * n + 64, 64 * n:64 * n + 64] for n in range(8)])
    small_g = {
        "g_mix": g_gmix, "q_norm_g": g_qg.reshape(N_HEADS, HEAD_DIM).sum(0), "k_norm_g": g_kg.reshape(N_HEADS, HEAD_DIM).sum(0),
        "rec_conv_b": g_rcb, "w_rg": blocks(g_wrg), "b_rg": g_brg, "w_ig": blocks(g_wig), "b_ig": g_big,
        "lru_lambda": g_lam, "g_attn_out": g_gattn, "g_rec_out": g_grec, "g_ffn": g_gffn,
        "ffn_conv_b": jnp.concatenate([g_fcbg, g_fcbu], axis=1)}
    g_fcw = jnp.concatenate([g_fcwg[:3], g_fcwu[:3]], axis=1)
    flat = jnp.concatenate([small_g[k].reshape(-1) for k, _ in _SMALL]
                           + [g_rcw[:4].reshape(-1), g_fcw.reshape(-1), loss_mine.reshape(1)])
    flat = jnp.pad(flat, (0, SMALL_ROWS * 1024 - flat.shape[0])).reshape(SMALL_ROWS, 1024)
    tot = _sum_devices(_all_gather([flat], "ag_small_grads")[0], "sum_small_grads").reshape(-1)
    g_small, o = {}, 0
    for k, n in _SMALL:
        g_small[k] = tot[o:o + n].reshape(_SMALL_SHAPES[k])
        o += n
    g_small["rec_conv_w"] = lax.dynamic_slice(tot[o:o + 2048].reshape(1, 4, REC_W), (0, 0, 64 * dev), (1, 4, 64))
    g_small["ffn_conv_w"] = lax.dynamic_slice(tot[o + 2048:o + 2048 + 18432].reshape(1, 3, 2 * D_FF), (0, 0, 768 * dev), (1, 3, 768))
    loss = tot[o + 2048 + 18432]

    devi = jnp.reshape(dev, (1,)).astype(jnp.int32)
    ffn_g, ffn_land = _exchange_wait(rs_send, rs_recv, ffn_g, ffn_land, "scatter", tot, "rs_ffn_wait")
    (g_out,), out_land = _exchange_wait(out_send, out_recv, [g_out], out_land, "scatter", tot, "rs_out_wait")
    (g_inT,), in_land = _exchange_wait(in_send, in_recv, [g_inT], in_land, "scatter", tot, "rs_in_wait")
    big_out = {"grad": {}, "delta": {}, "new_m": {}, "new_v": {}}
    for nm, p, r in (("w_up", ffn_g[0], ffn_land[0]), ("w_down", ffn_g[1], ffn_land[1]), ("w_out", g_out, out_land[0]),
                     ("w_in", g_inT, in_land[0])):
        w_, m_, v_ = shards[nm]
        res = _adam_sharded(p, r, devi, w_, m_, v_, "adam_" + nm, transposed=nm in ("w_in", "w_up"))
        for kind, a in zip(("grad", "delta", "new_m", "new_v"), res):
            big_out[kind][nm] = a[None]
    given = dict(rec_conv_w=rec_conv_w, ffn_conv_w=ffn_conv_w,g_mix=g_mix, q_norm_g=q_norm_g, k_norm_g=k_norm_g, rec_conv_b=rec_conv_b, w_rg=w_rg, b_rg=b_rg, w_ig=w_ig,
                 b_ig=b_ig, lru_lambda=lru_lambda, g_attn_out=g_attn_out, g_rec_out=g_rec_out, g_ffn=g_ffn, ffn_conv_b=ffn_conv_b)
    given_m = dict(rec_conv_w=m_rec_conv_w, ffn_conv_w=m_ffn_conv_w, g_mix=m_g_mix, q_norm_g=m_q_norm_g, k_norm_g=m_k_norm_g, rec_conv_b=m_rec_conv_b, w_rg=m_w_rg, b_rg=m_b_rg,
                   w_ig=m_w_ig, b_ig=m_b_ig, lru_lambda=m_lru_lambda, g_attn_out=m_g_attn_out, g_rec_out=m_g_rec_out,
                   g_ffn=m_g_ffn, ffn_conv_b=m_ffn_conv_b)
    given_v = dict(rec_conv_w=v_rec_conv_w, ffn_conv_w=v_ffn_conv_w, g_mix=v_g_mix, q_norm_g=v_q_norm_g, k_norm_g=v_k_norm_g, rec_conv_b=v_rec_conv_b, w_rg=v_w_rg, b_rg=v_b_rg,
                   w_ig=v_w_ig, b_ig=v_b_ig, lru_lambda=v_lru_lambda, g_attn_out=v_g_attn_out, g_rec_out=v_g_rec_out,
                   g_ffn=v_g_ffn, ffn_conv_b=v_ffn_conv_b)
    small = sorted(given)
    ds, m2s, v2s = _adam_small([given[k] for k in small], [g_small[k] for k in small], [given_m[k] for k in small],
                               [given_v[k] for k in small], "adam_small")
    small_out = {"grad": g_small, "delta": dict(zip(small, ds)), "new_m": dict(zip(small, m2s)), "new_v": dict(zip(small, v2s))}

    order = ("g_mix", "w_in", "q_norm_g", "k_norm_g", "rec_conv_w", "rec_conv_b", "w_rg", "b_rg", "w_ig", "b_ig",
             "lru_lambda", "g_attn_out", "g_rec_out", "w_out", "g_ffn", "w_up", "ffn_conv_w", "ffn_conv_b", "w_down")
    outs = [loss, grad_x.reshape(1, T, D_MODEL)]
    for kind in ("grad", "delta", "new_m", "new_v"):
        for name in order:
            outs.append(big_out[kind][name] if name in big_out[kind] else small_out[kind][name])
    return tuple(outs)
```

```python
import math

import numpy as np
import jax
import jax.numpy as jnp
from jax import lax
from jax.experimental import pallas as pl
from jax.experimental.pallas import tpu as pltpu

F32 = jnp.float32
BF16 = jnp.bfloat16

D_MODEL = 1024
HEAD_DIM = 64
ATTN_W = 512
REC_W = 512
N_HEADS = 8
D_FF = 3072
IN_W = 2560
REC_CONV = 4
FFN_CONV = 3
LRU_C = 8.0
ROPE_THETA = 10000.0
EPS = 1e-6
NEG_INF = -1e30
QBLK = 128
DILATIONS = (1, 4, 16)
N_DEV = 8
SMALL_ROWS = 104
ADAM_LR, ADAM_B1, ADAM_B2, ADAM_EPS, ADAM_WD, ADAM_STEP = 0.001, 0.9, 0.999, 1e-08, 0.01, 10
MESH = pl.DeviceIdType.MESH
ANY = pl.BlockSpec(memory_space=pl.ANY)


def _call(body, *, name, **kw):
    return pl.pallas_call(body, name=name, **kw)


def _params(*sem):
    return pltpu.CompilerParams(dimension_semantics=sem, vmem_limit_bytes=56 * 1024 * 1024)


def _gelu(x):
    c = math.sqrt(2.0 / math.pi)
    return 0.5 * x * (1.0 + jnp.tanh(c * (x + 0.044715 * (x * x * x))))


def _gelu_and_grad(x):
    c = math.sqrt(2.0 / math.pi)
    t = jnp.tanh(c * (x + 0.044715 * (x * x * x)))
    g = 0.5 * x * (1.0 + t)
    dg = 0.5 * (1.0 + t) + 0.5 * x * (1.0 - t * t) * (c * (1.0 + 3.0 * 0.044715 * (x * x)))
    return g, dg


def _sigmoid(x):
    return 1.0 / (1.0 + jnp.exp(-x))


def _softplus_neg(lam):
    y = jnp.exp(-jnp.abs(lam))
    u = 1.0 + y
    log1p = jnp.where(u == 1.0, y, jnp.log(u) * y / jnp.where(u == 1.0, 1.0, u - 1.0))
    return jnp.maximum(-lam, 0.0) + log1p


_NN = (((1,), (0,)), ((), ()))
_NT = (((1,), (1,)), ((), ()))
_TN = (((0,), (0,)), ((), ()))


def _dot(a, b, dims=_NN):
    return lax.dot_general(a, b, dims, preferred_element_type=F32)


def _group_mean(v, bd):
    hi = v.astype(BF16)
    lo = (v - hi.astype(F32)).astype(BF16)
    w = bd.shape[0]
    return jnp.concatenate([_dot(hi[:, c:c + w], bd) + _dot(lo[:, c:c + w], bd) for c in range(0, v.shape[1], w)], axis=1)


def _rope_tables(pos_ref, invf_ref):
    ang = pos_ref[...].astype(F32) * invf_ref[:, :2 * HEAD_DIM]
    reps = invf_ref.shape[1] // (2 * HEAD_DIM)
    return jnp.tile(jnp.cos(ang), (1, reps)), jnp.tile(jnp.sin(ang), (1, reps))


def _shift_down(x, halo, s):
    rolled = pltpu.roll(x, s, 0)
    hr = pltpu.roll(halo, s, 0)
    row = lax.broadcasted_iota(jnp.int32, hr.shape, 0)
    first = jnp.where(row < s, hr, rolled[:8])
    return jnp.concatenate([first, rolled[8:]], axis=0)


def _shift_up(x, halo, s):
    n = x.shape[0]
    rolled = pltpu.roll(x, n - s, 0)
    hr = pltpu.roll(halo, 8 - s, 0)
    row = lax.broadcasted_iota(jnp.int32, hr.shape, 0)
    last = jnp.where(row >= 8 - s, hr, rolled[n - 8:])
    return jnp.concatenate([rolled[:n - 8], last], axis=0)


def _scan_fwd(a, u):
    n, w = a.shape
    a3, u3 = a.reshape(n // 8, 8, w), u.reshape(n // 8, 8, w)
    row = lax.broadcasted_iota(jnp.int32, a3.shape, 1)
    for s in (1, 2, 4):
        a_s = jnp.where(row < s, 1.0, pltpu.roll(a3, s, 1))
        u_s = jnp.where(row < s, 0.0, pltpu.roll(u3, s, 1))
        u3 = u3 + a3 * u_s
        a3 = a3 * a_s
    ps, hs = [a3[0]], [u3[0]]
    for k in range(1, n // 8):
        ps.append(a3[k] * ps[-1][7:8, :])
        hs.append(u3[k] + a3[k] * hs[-1][7:8, :])
    return jnp.concatenate(ps, axis=0), jnp.concatenate(hs, axis=0)


def _scan_bwd(b, v):
    n, w = b.shape
    b3, v3 = b.reshape(n // 8, 8, w), v.reshape(n // 8, 8, w)
    row = lax.broadcasted_iota(jnp.int32, b3.shape, 1)
    for s in (1, 2, 4):
        b_s = jnp.where(row >= 8 - s, 1.0, pltpu.roll(b3, 8 - s, 1))
        v_s = jnp.where(row >= 8 - s, 0.0, pltpu.roll(v3, 8 - s, 1))
        v3 = v3 + b3 * v_s
        b3 = b3 * b_s
    last = n // 8 - 1
    ps, gs = [b3[last]], [v3[last]]
    for k in range(last - 1, -1, -1):
        ps.append(b3[k] * ps[-1][0:1, :])
        gs.append(v3[k] + b3[k] * gs[-1][0:1, :])
    return jnp.concatenate(ps[::-1], axis=0), jnp.concatenate(gs[::-1], axis=0)


def _rot_half(y):
    n = y.shape[1]
    lane = lax.broadcasted_iota(jnp.int32, y.shape, 1) & (HEAD_DIM - 1)
    return jnp.where(lane < HEAD_DIM // 2, -pltpu.roll(y, n - HEAD_DIM // 2, 1), pltpu.roll(y, HEAD_DIM // 2, 1))


def _row_tile(r, cap=256):
    return max(t for t in range(16, cap + 1, 16) if r % t == 0)


def _all_gather(shards, name):
    na = len(shards)
    ms = [s.shape[0] for s in shards]

    def body(*refs):
        x_refs, out_refs = refs[:na], refs[na:2 * na]
        send_sems, recv_sems, local_sems = refs[2 * na:]
        x, y, c = lax.axis_index("x"), lax.axis_index("y"), lax.axis_index("c")
        me, sibling = (x, y, c), (x, y, 1 - c)
        chips = [(1 - x, y), (x, 1 - y), (1 - x, 1 - y)]

        def rows(a, px, py, pc):
            return out_refs[a].at[pl.ds((4 * px + 2 * py + pc) * ms[a], ms[a]), :]

        def copy(a, k, block, to, src=None):
            return pltpu.make_async_remote_copy(
                src_ref=rows(a, *block) if src is None else src, dst_ref=rows(a, *block),
                send_sem=send_sems.at[7 * a + k], recv_sem=recv_sems.at[7 * a + k], device_id=to, device_id_type=MESH)

        mine = [pltpu.make_async_copy(x_refs[a], rows(a, *me), local_sems.at[a]) for a in range(na)]
        first = []
        for a in range(na):
            mine[a].start()
            first.append(copy(a, 0, me, sibling, src=x_refs[a]))
            first += [copy(a, 1 + j, me, (*chip, c), src=x_refs[a]) for j, chip in enumerate(chips)]
        for cp in first:
            cp.start()
        passed = []
        for a in range(na):
            for j, chip in enumerate(chips):
                copy(a, 1 + j, (*chip, c), me).wait_recv()
                fw = copy(a, 4 + j, (*chip, c), sibling)
                fw.start()
                passed.append(fw)
        for a in range(na):
            copy(a, 0, sibling, me).wait_recv()
            for j, chip in enumerate(chips):
                copy(a, 4 + j, (*chip, 1 - c), me).wait_recv()
        for cp in first + passed:
            cp.wait_send()
        for cp in mine:
            cp.wait()

    return _call(
        body, name=name, out_shape=[jax.ShapeDtypeStruct((N_DEV * s.shape[0], s.shape[1]), s.dtype) for s in shards],
        in_specs=[ANY] * na, out_specs=[ANY] * na,
        scratch_shapes=[pltpu.SemaphoreType.DMA((7 * na,)), pltpu.SemaphoreType.DMA((7 * na,)),
                        pltpu.SemaphoreType.DMA((na,))],
    )(*shards)


HBM = pl.BlockSpec(memory_space=pltpu.HBM)
SEM = pl.BlockSpec(memory_space=pltpu.SEMAPHORE)
EFFECT = pltpu.SideEffectType.DATAFLOW_SIDE_EFFECTING
N_PEERS = N_DEV - 1


def _peer(k):
    x, y, c = lax.axis_index("x"), lax.axis_index("y"), lax.axis_index("c")
    b = k + 1
    flip = lambda v, bit: 1 - v if bit else v
    return flip(x, b & 4), flip(y, b & 2), flip(c, b & 1)


def _in_hbm(a):
    return pltpu.with_memory_space_constraint(a, pltpu.HBM)


def _split_copy_descr(na, kind, src_refs, land_refs, send_sems, recv_sems):
    x, y, c = lax.axis_index("x"), lax.axis_index("y"), lax.axis_index("c")
    me = 4 * x + 2 * y + c
    copies = []
    for a in range(na):
        for k in range(N_PEERS):
            px, py, pc = _peer(k)
            if kind == "gather":
                m = src_refs[a].shape[0]
                src, dst = src_refs[a], land_refs[a].at[pl.ds(me * m, m), :]
            else:
                src, dst = src_refs[a].at[4 * px + 2 * py + pc], land_refs[a].at[k]
            copies.append(pltpu.make_async_remote_copy(
                src_ref=src, dst_ref=dst, send_sem=send_sems.at[N_PEERS * a + k], recv_sem=recv_sems.at[N_PEERS * a + k],
                device_id=(px, py, pc), device_id_type=MESH))
    return copies


def _landing(shape, dtype, own=None, at=None):
    buf = lax.empty(shape, dtype)
    return buf if own is None else lax.dynamic_update_slice(buf, own, (at, 0))


def _exchange_start(srcs, lands, kind, after, name):
    na = len(srcs)
    land_shapes = [l.shape for l in lands]

    def body(*refs):
        src_refs, land_refs = refs[:na], refs[na:2 * na]
        send_sems, recv_sems = refs[2 * na + 1], refs[2 * na + 2]
        token = refs[-1]
        for cp in _split_copy_descr(na, kind, src_refs, land_refs, send_sems, recv_sems):
            cp.start()
        token[...] = jnp.zeros_like(token)

    lands = [_in_hbm(l) for l in lands]
    sem = pltpu.SemaphoreType.DMA((N_PEERS * na,))
    outs = _call(
        body, name=name,
        out_shape=[sem, sem] + [pltpu.HBM(s.shape, s.dtype) for s in srcs] + [pltpu.HBM(s, srcs[0].dtype) for s in land_shapes]
        + [jax.ShapeDtypeStruct((8, 128), F32)],
        in_specs=[HBM] * (2 * na) + [ANY], out_specs=[SEM, SEM] + [HBM] * (2 * na) + [pl.BlockSpec(memory_space=pltpu.VMEM)],
        input_output_aliases={i: 2 + i for i in range(2 * na)},
        compiler_params=pltpu.CompilerParams(has_side_effects=EFFECT),
    )(*[_in_hbm(s) for s in srcs], *lands, after)
    return outs[0], outs[1], outs[2:2 + na], outs[2 + na:2 + 2 * na], outs[-1]


def _exchange_wait(send_sems, recv_sems, srcs, lands, kind, after, name):
    na = len(srcs)

    def body(*refs):
        src_refs, land_refs = refs[:na], refs[na:2 * na]
        s_sems, r_sems = refs[2 * na], refs[2 * na + 1]
        for cp in _split_copy_descr(na, kind, src_refs, land_refs, s_sems, r_sems):
            cp.wait_send()
            cp.wait_recv()

    outs = _call(
        body, name=name, out_shape=[pltpu.HBM(s.shape, s.dtype) for s in srcs] + [pltpu.HBM(l.shape, l.dtype) for l in lands],
        in_specs=[HBM] * (2 * na) + [SEM, SEM, ANY], out_specs=[HBM] * (2 * na),
        input_output_aliases={i: i for i in range(2 * na)},
        compiler_params=pltpu.CompilerParams(has_side_effects=EFFECT),
    )(*srcs, *lands, send_sems, recv_sems, after)
    return outs[:na], outs[na:]


def _mm(a, b, mode, out_dtype, name, add=None, tm=1024, tn=1024, tk=1024, b_noff=0, b_koff=0,
        n=None, k=None, into=None, o_rows=None, o_moff=0, loss_target=None):
    if mode == "tn":
        K, M = a.shape
    else:
        M, K = a.shape
    N = n if n is not None else (b.shape[0] if mode == "nt" else b.shape[1])
    if k is not None:
        assert k == K
    tm, tn, tk = min(tm, M), min(tn, N), min(tk, K)
    assert M % tm == 0 and N % tn == 0 and K % tk == 0, (name, M, N, K)
    nk = K // tk
    if mode == "nn":
        a_spec = pl.BlockSpec((tm, tk), lambda i, j, kk: (i, kk))
        b_spec, dims = pl.BlockSpec((tk, tn), lambda i, j, kk: (kk + b_koff, j + b_noff)), _NN
    elif mode == "nt":
        a_spec = pl.BlockSpec((tm, tk), lambda i, j, kk: (i, kk))
        b_spec, dims = pl.BlockSpec((tn, tk), lambda i, j, kk: (j + b_noff, kk + b_koff)), _NT
    else:
        a_spec = pl.BlockSpec((tk, tm), lambda i, j, kk: (kk, i))
        b_spec, dims = pl.BlockSpec((tk, tn), lambda i, j, kk: (kk + b_koff, j + b_noff)), _TN
    o_spec = pl.BlockSpec((tm, tn), lambda i, j, kk: (i + o_moff, j))
    has_add, has_into, has_loss = add is not None, into is not None, loss_target is not None
    assert not has_loss or (has_add and tn == N and not has_into)
    n_in = 2 + has_add + has_loss + has_into

    def body(*refs):
        a_ref, b_ref = refs[0], refs[1]
        add_ref = refs[2] if has_add else None
        outs = refs[n_in:]

        def finish(r):
            if has_add:
                r = r + add_ref[...]
            if has_loss:
                e = r - refs[3][...]
                dy = e * (1.0 / N)
                outs[0][...] = dy
                outs[1][...] = dy.astype(BF16)
                outs[2][...] = jnp.sum(e * e, axis=0, keepdims=True)[None]
            else:
                outs[0][...] = r.astype(out_dtype)

        if nk == 1:
            finish(_dot(a_ref[...], b_ref[...], dims))
        else:
            acc = refs[-1]
            kk = pl.program_id(2)

            @pl.when(kk == 0)
            def _():
                acc[...] = _dot(a_ref[...], b_ref[...], dims)

            @pl.when((kk > 0) & (kk < nk - 1))
            def _():
                acc[...] += _dot(a_ref[...], b_ref[...], dims)

            @pl.when(kk == nk - 1)
            def _():
                finish(acc[...] + _dot(a_ref[...], b_ref[...], dims))

    tile = pl.BlockSpec((tm, tn), lambda i, j, kk: (i, j))
    ins = [a, b] + ([add] if has_add else []) + ([loss_target] if has_loss else []) + ([into] if has_into else [])
    specs = [a_spec, b_spec] + [tile] * (has_add + has_loss) + ([ANY] if has_into else [])
    rows = into.shape[0] if has_into else (o_rows if o_rows is not None else M)
    if has_loss:
        out_specs = [tile, tile, pl.BlockSpec((1, 1, N), lambda i, j, kk: (i, 0, 0))]
        out_shape = [jax.ShapeDtypeStruct((M, N), F32), jax.ShapeDtypeStruct((M, N), BF16), jax.ShapeDtypeStruct((M // tm, 1, N), F32)]
    else:
        out_specs, out_shape = o_spec, jax.ShapeDtypeStruct((rows, N), out_dtype)
    return _call(
        body, name=name, grid=(M // tm, N // tn, nk), in_specs=specs, out_specs=out_specs, out_shape=out_shape,
        scratch_shapes=[pltpu.VMEM((tm, tn), F32)] if nk > 1 else [],
        input_output_aliases={len(ins) - 1: 0} if has_into else {},
        compiler_params=_params("parallel", "parallel", "arbitrary"),
    )(*ins)


def _mm_norm_bwd(parts, b, x, resid, g, name, tm=512, tk=512):
    T, N = x.shape
    counts = [p.shape[1] // tk for p in parts]
    starts = [sum(counts[:i]) for i in range(len(parts))]
    nsteps = sum(counts)
    assert all(p.shape[1] % tk == 0 for p in parts) and b.shape == (nsteps * tk, N)
    npart = len(parts)

    def body(*refs):
        a_refs, b_ref, x_ref, res_ref, g_ref = refs[:npart], refs[npart], refs[npart + 1], refs[npart + 2], refs[npart + 3]
        dx_ref, dxb_ref, dg_ref, acc = refs[npart + 4:]
        i, s = pl.program_id(0), pl.program_id(1)

        @pl.when((i == 0) & (s == 0))
        def _():
            dg_ref[...] = jnp.zeros_like(dg_ref)

        for p in range(npart):
            @pl.when((s >= starts[p]) & (s < starts[p] + counts[p]))
            def _(p=p):
                d = _dot(a_refs[p][...], b_ref[...])

                @pl.when(s == 0)
                def _():
                    acc[...] = d

                @pl.when(s > 0)
                def _():
                    acc[...] += d

        @pl.when(s == nsteps - 1)
        def _():
            xv, dhv = x_ref[...], acc[...]
            r = lax.rsqrt(jnp.mean(xv * xv, axis=-1, keepdims=True) + EPS)
            gd = dhv * g_ref[...]
            m = jnp.mean(gd * xv, axis=-1, keepdims=True)
            dx = res_ref[...] + r * gd - xv * (r * r * r) * m
            dx_ref[...] = dx
            dxb_ref[...] = dx.astype(BF16)
            dg_ref[...] += jnp.sum(dhv * xv * r, axis=0, keepdims=True)

    a_specs = [pl.BlockSpec((tm, tk), lambda i, s, st=st, c=c: (i, jnp.clip(s - st, 0, c - 1))) for st, c in zip(starts, counts)]
    row = pl.BlockSpec((tm, N), lambda i, s: (i, 0))
    vec = pl.BlockSpec((1, N), lambda i, s: (0, 0))
    return _call(
        body, name=name, grid=(T // tm, nsteps),
        in_specs=a_specs + [pl.BlockSpec((tk, N), lambda i, s: (s, 0)), row, row, vec], out_specs=[row, row, vec],
        out_shape=[jax.ShapeDtypeStruct((T, N), F32), jax.ShapeDtypeStruct((T, N), BF16), jax.ShapeDtypeStruct((1, N), F32)],
        scratch_shapes=[pltpu.VMEM((tm, N), F32)], compiler_params=_params("arbitrary", "arbitrary"),
    )(*parts, b, x, resid, g)


def _rmsnorm(x, g, name, tm=512):
    T, D = x.shape

    def body(x_ref, g_ref, o_ref):
        xv = x_ref[...]
        r = lax.rsqrt(jnp.mean(xv * xv, axis=-1, keepdims=True) + EPS)
        o_ref[...] = (xv * r * g_ref[...]).astype(BF16)

    return _call(
        body, name=name, grid=(T // tm,),
        in_specs=[pl.BlockSpec((tm, D), lambda i: (i, 0)), pl.BlockSpec((1, D), lambda i: (0, 0))],
        out_specs=pl.BlockSpec((tm, D), lambda i: (i, 0)), out_shape=jax.ShapeDtypeStruct((T, D), BF16),
        compiler_params=_params("parallel"),
    )(x, g)


def _qk_prep(proj, pos, invf, qg, kg, bd, name, tm=512):
    T = proj.shape[0]

    def body(q_ref, k_ref, pos_ref, invf_ref, qg_ref, kg_ref, bd_ref, qo_ref, ko_ref):
        cos, sin = _rope_tables(pos_ref, invf_ref)

        def prep(xv, gv, scale):
            r = lax.rsqrt(_group_mean(xv * xv, bd_ref[...]) + EPS)
            yv = xv * r * gv
            return ((yv * cos + _rot_half(yv) * sin) * scale).astype(BF16).astype(F32)

        qo_ref[...] = prep(q_ref[...], qg_ref[...], HEAD_DIM ** -0.5)
        ko_ref[...] = prep(k_ref[...], kg_ref[...], 1.0)

    col = lambda j: pl.BlockSpec((tm, ATTN_W), lambda i, j=j: (i, j))
    vec = pl.BlockSpec((1, ATTN_W), lambda i: (0, 0))
    out = pl.BlockSpec((tm, ATTN_W), lambda i: (i, 0))
    return _call(
        body, name=name, grid=(T // tm,),
        in_specs=[col(0), col(1), pl.BlockSpec((tm, 1), lambda i: (i, 0)), vec, vec, vec,
                  pl.BlockSpec((2 * HEAD_DIM, 2 * HEAD_DIM), lambda i: (0, 0))],
        out_specs=[out, out], out_shape=[jax.ShapeDtypeStruct((T, ATTN_W), F32)] * 2,
        compiler_params=_params("parallel"),
    )(proj, proj, pos, invf, qg, kg, bd)


def _qk_prep_bwd(proj, dqh, dkh, dv, pos, invf, qg, kg, bd, name, tm=512):
    T = proj.shape[0]

    def body(q_ref, k_ref, dq_ref, dk_ref, dv_ref, pos_ref, invf_ref, qg_ref, kg_ref, bd_ref, o_ref, gq_ref, gk_ref):
        @pl.when(pl.program_id(0) == 0)
        def _():
            gq_ref[...] = jnp.zeros_like(gq_ref)
            gk_ref[...] = jnp.zeros_like(gk_ref)

        cos, sin = _rope_tables(pos_ref, invf_ref)

        def back(xv, gv, dz, scale):
            dz = dz * scale
            dy = dz * cos - _rot_half(dz * sin)
            r = lax.rsqrt(_group_mean(xv * xv, bd_ref[...]) + EPS)
            gd = dy * gv
            m = _group_mean(gd * xv, bd_ref[...])
            dx = r * gd - xv * (r * r * r) * m
            return dx, jnp.sum(dy * xv * r, axis=0, keepdims=True)

        dxq, gs = back(q_ref[...], qg_ref[...], dq_ref[...], HEAD_DIM ** -0.5)
        gq_ref[...] += gs
        dxk, gs = back(k_ref[...], kg_ref[...], dk_ref[...], 1.0)
        gk_ref[...] += gs
        o_ref[...] = jnp.concatenate([dxq.astype(BF16), dxk.astype(BF16), dv_ref[...].astype(BF16)], axis=1)

    col = lambda j: pl.BlockSpec((tm, ATTN_W), lambda i, j=j: (i, j))
    row = pl.BlockSpec((tm, ATTN_W), lambda i: (i, 0))
    vec = pl.BlockSpec((1, ATTN_W), lambda i: (0, 0))
    return _call(
        body, name=name, grid=(T // tm,),
        in_specs=[col(0), col(1), row, row, row, pl.BlockSpec((tm, 1), lambda i: (i, 0)), vec, vec, vec,
                  pl.BlockSpec((2 * HEAD_DIM, 2 * HEAD_DIM), lambda i: (0, 0))],
        out_specs=[pl.BlockSpec((tm, 3 * ATTN_W), lambda i: (i, 0)), vec, vec],
        out_shape=[jax.ShapeDtypeStruct((T, 3 * ATTN_W), BF16)] + [jax.ShapeDtypeStruct((1, ATTN_W), F32)] * 2,
        compiler_params=_params("arbitrary"),
    )(proj, proj, dqh, dkh, dv, pos, invf, qg, kg, bd)


def _ld(ref, start, size, dil):
    return ref[pl.ds(start, size), :] if dil == 1 else ref[pl.ds(start, size, stride=dil), :]


def _st(ref, start, size, dil, val):
    if dil == 1:
        ref[pl.ds(start, size), :] = val
    else:
        ref[pl.ds(start, size, stride=dil), :] = val


def _attn_geometry(T, dil):
    nb = T // dil // QBLK
    if nb == 2:
        return 1, 2 * QBLK, 2 * QBLK
    return nb, QBLK, (2 * QBLK if nb >= 2 else QBLK)


ATTN_UNROLL = 4


def _attn_unit(j, u, dil, nit):
    return ATTN_UNROLL * j + u if dil >= ATTN_UNROLL else j + u * (nit // ATTN_UNROLL)


def _attn_block(it, dil, qb, kw):
    c, n = it & (dil - 1), lax.shift_right_logical(it, dil.bit_length() - 1)
    sq = n * (qb * dil) + c
    sk = jnp.maximum(n - (kw // qb - 1), 0) * (qb * dil) + c
    qi = lax.broadcasted_iota(jnp.int32, (2 * qb, kw), 0) & (qb - 1)
    kj = lax.broadcasted_iota(jnp.int32, (2 * qb, kw), 1)
    rel = jnp.where(n > 0, kw - qb, 0) + qi - kj
    return sq, sk, (rel >= 0) & (rel <= QBLK)


def _stack_heads(xv, head0):
    z = jnp.zeros_like(xv)
    return jnp.concatenate([jnp.where(head0, xv, z), jnp.where(head0, z, xv)], axis=0)


def _unstack_heads(x2, head0):
    qb = x2.shape[0] // 2
    return jnp.where(head0, x2[:qb], x2[qb:])


def _attn_fwd(qf, kf, proj, name):
    T = qf.shape[0]

    def body(q_ref, k_ref, v_ref, o_ref, lse_ref):
        for bi, dil in enumerate(DILATIONS):
            nb, qb, kw = _attn_geometry(T, dil)
            nit = nb * dil
            head0 = lax.broadcasted_iota(jnp.int32, (qb, 2 * HEAD_DIM), 1) < HEAD_DIM

            def step(j, carry, bi=bi, dil=dil, qb=qb, kw=kw, nit=nit, head0=head0):
                units = []
                for u in range(ATTN_UNROLL):
                    sq, sk, ok = _attn_block(_attn_unit(j, u, dil, nit), dil, qb, kw)
                    old = (_ld(o_ref, sq, qb, dil), _ld(lse_ref, sq, qb, dil)) if bi > 0 else None
                    units.append((sq, ok, _ld(q_ref, sq, qb, dil).astype(BF16), _ld(k_ref, sk, kw, dil).astype(BF16),
                                  _ld(v_ref, sk, kw, dil).astype(BF16), old))
                results = []
                for sq, ok, qv, kv, vv, old in units:
                    s = jnp.where(ok, _dot(_stack_heads(qv, head0), kv, _NT), NEG_INF)
                    m = jnp.max(s, axis=-1, keepdims=True)
                    p = jnp.exp(s - m).astype(BF16)
                    acc = _dot(p, jnp.concatenate([vv, jnp.ones_like(vv)], axis=1))
                    l = acc[:, 2 * HEAD_DIM:]
                    o_new = _unstack_heads(acc[:, :2 * HEAD_DIM] / l, head0)
                    l_new = _unstack_heads(m + jnp.log(l), head0)
                    if bi > 0:
                        o_old, l_old = old
                        mx = jnp.maximum(l_old, l_new)
                        e0, e1 = jnp.exp(l_old - mx), jnp.exp(l_new - mx)
                        z = e0 + e1
                        o_new = (e0 * o_old + e1 * o_new) / z
                        l_new = mx + jnp.log(z)
                    results.append((sq, o_new, l_new))
                for sq, o_new, l_new in results:
                    _st(o_ref, sq, qb, dil, o_new)
                    _st(lse_ref, sq, qb, dil, l_new)
                return carry

            lax.fori_loop(0, nit // ATTN_UNROLL, step, 0)

    blk = lambda off: pl.BlockSpec((T, 2 * HEAD_DIM), lambda hp, off=off: (0, off + hp))
    return _call(
        body, name=name, grid=(4,), in_specs=[blk(0), blk(0), blk(8)], out_specs=[blk(0), blk(0)],
        out_shape=[jax.ShapeDtypeStruct((T, ATTN_W), F32)] * 2, compiler_params=_params("parallel"),
    )(qf, kf, proj)


def _attn_bwd(qf, kf, proj, do, lse, delta, name):
    T = qf.shape[0]

    def body(q_ref, k_ref, v_ref, do_ref, lse_ref, dl_ref, dq_ref, dk_ref, dv_ref):
        for ref in (dq_ref, dk_ref, dv_ref):
            ref[...] = jnp.zeros_like(ref)
        for dil in DILATIONS:
            nb, qb, kw = _attn_geometry(T, dil)
            nit = nb * dil
            head0 = lax.broadcasted_iota(jnp.int32, (qb, 2 * HEAD_DIM), 1) < HEAD_DIM

            def step(j, carry, dil=dil, qb=qb, kw=kw, nit=nit, head0=head0):
                units = []
                for u in range(ATTN_UNROLL):
                    sq, sk, ok = _attn_block(_attn_unit(j, u, dil, nit), dil, qb, kw)
                    lsev, dlv = _ld(lse_ref, sq, qb, dil), _ld(dl_ref, sq, qb, dil)
                    units.append((sq, sk, ok, _ld(q_ref, sq, qb, dil).astype(BF16), _ld(do_ref, sq, qb, dil).astype(BF16),
                                  jnp.concatenate([lsev[:, 0:1], lsev[:, HEAD_DIM:HEAD_DIM + 1]], axis=0),
                                  jnp.concatenate([dlv[:, 0:1], dlv[:, HEAD_DIM:HEAD_DIM + 1]], axis=0),
                                  _ld(k_ref, sk, kw, dil).astype(BF16), _ld(v_ref, sk, kw, dil).astype(BF16),
                                  _ld(dq_ref, sq, qb, dil), _ld(dk_ref, sk, kw, dil), _ld(dv_ref, sk, kw, dil)))
                results = []
                for sq, sk, ok, qv, dov, lse2, dl2, kv, vv, dq0, dk0, dv0 in units:
                    q2, do2 = _stack_heads(qv, head0), _stack_heads(dov, head0)
                    p = jnp.where(ok, jnp.exp(_dot(q2, kv, _NT) - lse2), 0.0)
                    ds = (p * (_dot(do2, vv, _NT) - dl2)).astype(BF16)
                    results.append((sq, sk, dq0 + _unstack_heads(_dot(ds, kv), head0),
                                    dk0 + _dot(ds, q2, _TN), dv0 + _dot(p.astype(BF16), do2, _TN)))
                for sq, sk, dq, dk, dv in results:
                    _st(dq_ref, sq, qb, dil, dq)
                    _st(dk_ref, sk, kw, dil, dk)
                    _st(dv_ref, sk, kw, dil, dv)
                return carry

            lax.fori_loop(0, nit // ATTN_UNROLL, step, 0)

    blk = lambda off: pl.BlockSpec((T, 2 * HEAD_DIM), lambda hp, off=off: (0, off + hp))
    return _call(
        body, name=name, grid=(4,), in_specs=[blk(0), blk(0), blk(8), blk(0), blk(0), blk(0)], out_specs=[blk(0)] * 3,
        out_shape=[jax.ShapeDtypeStruct((T, ATTN_W), F32)] * 3, compiler_params=_params("parallel"),
    )(qf, kf, proj, do, lse, delta)


def _attn_norm(attn, g, name, tm=512):
    T = attn.shape[0]

    def body(a_ref, g_ref, o_ref):
        av = a_ref[...]
        r = lax.rsqrt(jnp.mean(av * av, axis=-1, keepdims=True) + EPS)
        o_ref[...] = (av * r * g_ref[...]).astype(BF16)

    row = pl.BlockSpec((tm, ATTN_W), lambda i: (i, 0))
    return _call(
        body, name=name, grid=(T // tm,), in_specs=[row, pl.BlockSpec((1, ATTN_W), lambda i: (0, 0))], out_specs=row,
        out_shape=jax.ShapeDtypeStruct((T, 2 * ATTN_W), BF16), compiler_params=_params("parallel"),
    )(attn, g)


def _attn_norm_bwd(dmix, attn, g, bd, name, tm=512):
    T = attn.shape[0]

    def body(d_ref, a_ref, g_ref, bd_ref, do_ref, dl_ref, dg_ref):
        @pl.when(pl.program_id(0) == 0)
        def _():
            dg_ref[...] = jnp.zeros_like(dg_ref)

        dy, av = d_ref[...], a_ref[...]
        r = lax.rsqrt(jnp.mean(av * av, axis=-1, keepdims=True) + EPS)
        gd = dy * g_ref[...]
        m = jnp.mean(gd * av, axis=-1, keepdims=True)
        da = r * gd - av * (r * r * r) * m
        do_ref[...] = da
        dl_ref[...] = _group_mean(da * av, bd_ref[...]) * float(HEAD_DIM)
        dg_ref[...] += jnp.sum(dy * av * r, axis=0, keepdims=True)

    row = pl.BlockSpec((tm, ATTN_W), lambda i: (i, 0))
    vec = pl.BlockSpec((1, ATTN_W), lambda i: (0, 0))
    return _call(
        body, name=name, grid=(T // tm,),
        in_specs=[row, row, vec, pl.BlockSpec((2 * HEAD_DIM, 2 * HEAD_DIM), lambda i: (0, 0))], out_specs=[row, row, vec],
        out_shape=[jax.ShapeDtypeStruct((T, ATTN_W), F32)] * 2 + [jax.ShapeDtypeStruct((1, ATTN_W), F32)],
        compiler_params=_params("arbitrary"),
    )(dmix, attn, g, bd)


def _rec_gates(xc, wrg_ref, wig_ref, brg_ref, big_ref, lam_ref):
    xb = xc.astype(BF16)
    r = _sigmoid(_dot(xb, wrg_ref[...]) + brg_ref[...])
    ig = _sigmoid(_dot(xb, wig_ref[...]) + big_ref[...])
    sp = _softplus_neg(lam_ref[...])
    log_a = -LRU_C * r * sp
    a = jnp.exp(log_a)
    th = jnp.tanh(log_a)
    mult = jnp.sqrt(-2.0 * th / (1.0 - th))
    return xb, r, ig, sp, a, mult


def _rec_fwd(proj, mix, cw, cb, wrg, wig, brg, big, lam, g, name, tm=256):
    T = proj.shape[0]
    hb = tm // 8

    def body(xr_ref, halo_ref, gr_ref, cw_ref, cb_ref, wrg_ref, wig_ref, brg_ref, big_ref, lam_ref, g_ref, mix_ref,
             xc_ref, h_ref, out_ref, carry):
        i = pl.program_id(0)

        @pl.when(i == 0)
        def _():
            carry[...] = jnp.zeros_like(carry)

        xr = xr_ref[...]
        halo = jnp.where(i > 0, halo_ref[...], 0.0)
        xc = cb_ref[...] + cw_ref[3:4, :] * xr
        for s in range(1, REC_CONV):
            xc = xc + cw_ref[3 - s:4 - s, :] * _shift_down(xr, halo, s)
        xc_ref[...] = xc
        _, _, ig, _, a, mult = _rec_gates(xc, wrg_ref, wig_ref, brg_ref, big_ref, lam_ref)
        pa, hl = _scan_fwd(a, mult * (ig * xc))
        h = hl + pa * carry[0:1, :]
        h_ref[...] = h
        carry[0:1, :] = h_ref[pl.ds(tm - 1, 1), :]
        hg = h * _gelu(gr_ref[...])
        r = lax.rsqrt(jnp.mean(hg * hg, axis=-1, keepdims=True) + EPS)
        out_ref[...] = (hg * r * g_ref[...]).astype(BF16)

    vec = pl.BlockSpec((1, REC_W), lambda i: (0, 0))
    row = pl.BlockSpec((tm, REC_W), lambda i: (i, 0))
    mat = pl.BlockSpec((REC_W, REC_W), lambda i: (0, 0))
    return _call(
        body, name=name, grid=(T // tm,),
        in_specs=[pl.BlockSpec((tm, REC_W), lambda i: (i, 3)),
                  pl.BlockSpec((8, REC_W), lambda i: (jnp.maximum(i * hb - 1, 0), 3)),
                  pl.BlockSpec((tm, REC_W), lambda i: (i, 4)),
                  pl.BlockSpec((8, REC_W), lambda i: (0, 0)), vec, mat, mat, vec, vec, vec, vec, ANY],
        out_specs=[row, row, pl.BlockSpec((tm, REC_W), lambda i: (i, 1))],
        out_shape=[jax.ShapeDtypeStruct((T, REC_W), F32)] * 2 + [jax.ShapeDtypeStruct(mix.shape, BF16)],
        scratch_shapes=[pltpu.VMEM((8, REC_W), F32)], input_output_aliases={11: 2},
        compiler_params=_params("arbitrary"),
    )(proj, proj, proj, cw, cb, wrg, wig, brg, big, lam, g, mix)


def _rec_bwd(dmix, proj, xc, h, cw, cb, wrg, wig, brg, big, lam, g, name, tm=256):
    T = proj.shape[0]
    nt = T // tm
    hb = tm // 8

    def body(d_ref, xr_ref, xhalo_ref, gr_ref, xc_ref, h_ref, hhalo_ref, cw_ref, cb_ref, wrg_ref, wig_ref, brg_ref,
             big_ref, lam_ref, g_ref,
             drec_ref, gcw_ref, gcb_ref, gwrg_ref, gwig_ref, gbrg_ref, gbig_ref, glam_ref, gg_ref,
             g_carry, a_first, dxc_next, gsp):
        i = pl.program_id(0)
        first_tile = i == nt - 1

        @pl.when(i == 0)
        def _():
            for ref in (gcw_ref, gcb_ref, gwrg_ref, gwig_ref, gbrg_ref, gbig_ref, glam_ref, gg_ref,
                        g_carry, a_first, dxc_next, gsp):
                ref[...] = jnp.zeros_like(ref)

        xr, xc, hv = xr_ref[...], xc_ref[...], h_ref[...]
        xhalo = jnp.where(first_tile, 0.0, xhalo_ref[...])
        hhalo = jnp.where(first_tile, 0.0, hhalo_ref[...])
        xb, r, ig, sp, a, mult = _rec_gates(xc, wrg_ref, wig_ref, brg_ref, big_ref, lam_ref)
        h_prev = _shift_down(hv, hhalo, 1)
        ge, dge = _gelu_and_grad(gr_ref[...])
        hg = hv * ge
        rr = lax.rsqrt(jnp.mean(hg * hg, axis=-1, keepdims=True) + EPS)
        dy = d_ref[...]
        gd = dy * g_ref[...]
        dhg = rr * gd - hg * (rr * rr * rr) * jnp.mean(gd * hg, axis=-1, keepdims=True)
        gg_ref[...] += jnp.sum(dy * hg * rr, axis=0, keepdims=True)
        dgr = (dhg * hv * dge).astype(BF16)
        dh = dhg * ge
        b = _shift_up(a, jnp.broadcast_to(a_first[0:1, :], (8, REC_W)), 1)
        pb, gl = _scan_bwd(b, dh)
        gs = gl + pb * g_carry[0:1, :]
        g_carry[0:1, :] = gs[0:1, :]
        a_first[0:1, :] = a[0:1, :]
        da = gs * h_prev
        dmult = gs * (ig * xc)
        di = gs * (mult * xc)
        dxc = gs * (mult * ig)
        dlog_a = da * a - dmult * (a * a) / mult
        gsp[...] += jnp.sum(dlog_a * (-LRU_C * r), axis=0, keepdims=True)
        dzr = (dlog_a * (-LRU_C * sp)) * (r * (1.0 - r))
        dzi = di * (ig * (1.0 - ig))
        dzr_b, dzi_b = dzr.astype(BF16), dzi.astype(BF16)
        dxc = dxc + _dot(dzr_b, wrg_ref[...], _NT) + _dot(dzi_b, wig_ref[...], _NT)
        gwrg_ref[...] += _dot(xb, dzr_b, _TN)
        gwig_ref[...] += _dot(xb, dzi_b, _TN)
        gbrg_ref[...] += jnp.sum(dzr, axis=0, keepdims=True)
        gbig_ref[...] += jnp.sum(dzi, axis=0, keepdims=True)
        nxt = dxc_next[...]
        dxr = cw_ref[3:4, :] * dxc
        gcw_ref[3:4, :] += jnp.sum(dxc * xr, axis=0, keepdims=True)
        for s in range(1, REC_CONV):
            dxr = dxr + cw_ref[3 - s:4 - s, :] * _shift_up(dxc, nxt, s)
            gcw_ref[3 - s:4 - s, :] += jnp.sum(dxc * _shift_down(xr, xhalo, s), axis=0, keepdims=True)
        gcb_ref[...] += jnp.sum(dxc, axis=0, keepdims=True)
        dxc_next[...] = dxc[:8]
        drec_ref[...] = jnp.concatenate([dxr.astype(BF16), dgr], axis=1)

        @pl.when(first_tile)
        def _():
            glam_ref[...] = gsp[...] * (-_sigmoid(-lam_ref[...]))

    rev = lambda i: nt - 1 - i
    vec = pl.BlockSpec((1, REC_W), lambda i: (0, 0))
    row = pl.BlockSpec((tm, REC_W), lambda i: (rev(i), 0))
    mat = pl.BlockSpec((REC_W, REC_W), lambda i: (0, 0))
    cwb = pl.BlockSpec((8, REC_W), lambda i: (0, 0))
    halo = lambda c: pl.BlockSpec((8, REC_W), lambda i, c=c: (jnp.maximum(rev(i) * hb - 1, 0), c))
    return _call(
        body, name=name, grid=(nt,),
        in_specs=[pl.BlockSpec((tm, REC_W), lambda i: (rev(i), 1)),
                  pl.BlockSpec((tm, REC_W), lambda i: (rev(i), 3)), halo(3),
                  pl.BlockSpec((tm, REC_W), lambda i: (rev(i), 4)),
                  row, row, halo(0), cwb, vec, mat, mat, vec, vec, vec, vec],
        out_specs=[pl.BlockSpec((tm, 2 * REC_W), lambda i: (rev(i), 0)), cwb, vec, mat, mat, vec, vec, vec, vec],
        out_shape=[jax.ShapeDtypeStruct((T, 2 * REC_W), BF16)]
        + [jax.ShapeDtypeStruct((8, REC_W), F32), jax.ShapeDtypeStruct((1, REC_W), F32)]
        + [jax.ShapeDtypeStruct((REC_W, REC_W), F32)] * 2 + [jax.ShapeDtypeStruct((1, REC_W), F32)] * 4,
        scratch_shapes=[pltpu.VMEM((8, REC_W), F32)] * 3 + [pltpu.VMEM((1, REC_W), F32)],
        compiler_params=_params("arbitrary"),
    )(dmix, proj, proj, proj, xc, h, h, cw, cb, wrg, wig, brg, big, lam, g)


def _ffn_conv(x_ext, cw_ref, cb_ref):
    return (cb_ref[...] + cw_ref[2:3, :] * x_ext + cw_ref[1:2, :] * pltpu.roll(x_ext, 1, 0)
            + cw_ref[0:1, :] * pltpu.roll(x_ext, 2, 0))


def _up_proj_act(h2, w_upT, cw, cb, name, tm=1024, tc=768):
    T = h2.shape[0]
    nc = D_FF // tc

    def body(h_ref, wg_ref, wu_ref, cwg_ref, cwu_ref, cbg_ref, cbu_ref, act_ref, da_ref, db_ref, pg_ref, pu_ref,
             hist_g, hist_u):
        i, j = pl.program_id(0), pl.program_id(1)
        hv = h_ref[...]
        pg, pu = _dot(hv, wg_ref[...], _NT), _dot(hv, wu_ref[...], _NT)
        ge = jnp.concatenate([jnp.where(i > 0, hist_g[j], 0.0), pg], axis=0)
        ue = jnp.concatenate([jnp.where(i > 0, hist_u[j], 0.0), pu], axis=0)
        gel, dgel = _gelu_and_grad(_ffn_conv(ge, cwg_ref, cbg_ref)[8:])
        uu = _ffn_conv(ue, cwu_ref, cbu_ref)[8:]
        act_ref[...] = (gel * uu).astype(BF16)
        da_ref[...] = (uu * dgel).astype(BF16)
        db_ref[...] = gel.astype(BF16)
        pg_ref[...] = pg.astype(BF16)
        pu_ref[...] = pu.astype(BF16)
        hist_g[j] = pg[tm - 8:]
        hist_u[j] = pu[tm - 8:]

    tile = pl.BlockSpec((tm, tc), lambda i, j: (i, j))
    wsp = lambda off: pl.BlockSpec((tc, D_MODEL), lambda i, j, off=off: (j + off, 0))
    cws = lambda off: pl.BlockSpec((8, tc), lambda i, j, off=off: (0, j + off))
    cbs = lambda off: pl.BlockSpec((1, tc), lambda i, j, off=off: (0, j + off))
    return _call(
        body, name=name, grid=(T // tm, nc),
        in_specs=[pl.BlockSpec((tm, D_MODEL), lambda i, j: (i, 0)), wsp(0), wsp(nc), cws(0), cws(nc), cbs(0), cbs(nc)],
        out_specs=[tile] * 5, out_shape=[jax.ShapeDtypeStruct((T, D_FF), BF16)] * 5,
        scratch_shapes=[pltpu.VMEM((nc, 8, tc), F32)] * 2, compiler_params=_params("arbitrary", "arbitrary"),
    )(h2, w_upT, w_upT, cw, cw, cb, cb)


def _ffn_bwd(dyb, w_down, da, db, pg, pu, cw, name, tm=512, tc=768):
    T, F = pg.shape
    nt = T // tm
    hb16 = tm // 16
    nc = F // tc
    n = tm + 8

    def body(dy_ref, dyn_ref, wd_ref, a_ref, an_ref, b_ref, bn_ref, g_ref, u_ref, cwg_ref, cwu_ref,
             dg_ref, du_ref, gcwg_ref, gcwu_ref, gcbg_ref, gcbu_ref):
        i = pl.program_id(1)
        last = i == nt - 1

        @pl.when(i == 0)
        def _():
            for ref in (gcwg_ref, gcwu_ref, gcbg_ref, gcbu_ref):
                ref[...] = jnp.zeros_like(ref)

        wd = wd_ref[...]
        dact_next = jnp.where(last, 0.0, _dot(dyn_ref[...], wd, _NT)[:8])
        de = jnp.concatenate([_dot(dy_ref[...], wd, _NT), dact_next], axis=0)
        ext = lambda t, nx: jnp.concatenate([t[...].astype(F32), nx[...].astype(F32)[:8]], axis=0)
        for dcv, x_ref, cw_ref, dx_ref, gcw_ref, gcb_ref in ((de * ext(a_ref, an_ref), g_ref, cwg_ref, dg_ref, gcwg_ref, gcbg_ref),
                                                               (de * ext(b_ref, bn_ref), u_ref, cwu_ref, du_ref, gcwu_ref, gcbu_ref)):
            s1, s2 = pltpu.roll(dcv, n - 1, 0), pltpu.roll(dcv, n - 2, 0)
            dx_ref[...] = (cw_ref[2:3, :] * dcv + cw_ref[1:2, :] * s1 + cw_ref[0:1, :] * s2)[:tm].astype(BF16)
            xv = x_ref[...].astype(F32)
            gcw_ref[2:3, :] += jnp.sum(xv * dcv[:tm], axis=0, keepdims=True)
            gcw_ref[1:2, :] += jnp.sum(xv * s1[:tm], axis=0, keepdims=True)
            gcw_ref[0:1, :] += jnp.sum(xv * s2[:tm], axis=0, keepdims=True)
            gcb_ref[...] += jnp.sum(dcv[:tm], axis=0, keepdims=True)

    tile = pl.BlockSpec((tm, tc), lambda j, i: (i, j))
    nxt = pl.BlockSpec((16, tc), lambda j, i: (jnp.minimum((i + 1) * hb16, nt * hb16 - 1), j))
    cws = lambda off: pl.BlockSpec((8, tc), lambda j, i, off=off: (0, j + off))
    cbs = pl.BlockSpec((1, tc), lambda j, i: (0, j))
    return _call(
        body, name=name, grid=(nc, nt),
        in_specs=[pl.BlockSpec((tm, D_MODEL), lambda j, i: (i, 0)),
                  pl.BlockSpec((16, D_MODEL), lambda j, i: (jnp.minimum((i + 1) * hb16, nt * hb16 - 1), 0)),
                  pl.BlockSpec((tc, D_MODEL), lambda j, i: (j, 0)), tile, nxt, tile, nxt, tile, tile, cws(0), cws(nc)],
        out_specs=[tile, tile, cws(0), cws(0), cbs, cbs],
        out_shape=[jax.ShapeDtypeStruct((T, F), BF16)] * 2 + [jax.ShapeDtypeStruct((8, F), F32)] * 2
        + [jax.ShapeDtypeStruct((1, F), F32)] * 2,
        compiler_params=_params("parallel", "arbitrary"),
    )(dyb, dyb, w_down, da, da, db, db, pg, pu, cw, cw)


def _adam_update(w, g, m, v):
    m2 = ADAM_B1 * m + (1.0 - ADAM_B1) * g
    v2 = ADAM_B2 * v + (1.0 - ADAM_B2) * (g * g)
    m_hat = m2 / (1.0 - ADAM_B1 ** ADAM_STEP)
    v_hat = v2 / (1.0 - ADAM_B2 ** ADAM_STEP)
    delta = -ADAM_LR * (m_hat / (jnp.sqrt(v_hat) + ADAM_EPS) + ADAM_WD * w)
    return delta, m2, v2


def _adam_sharded(p, r2, idx, w, m, v, name, transposed=False):
    r, n = p.shape[1:]
    nrecv = r2.shape[0]
    tr = (256 if r % 256 == 0 else r) if transposed else _row_tile(r)

    def body(c_ref, p_ref, r_ref, w_ref, m_ref, v_ref, g_ref, d_ref, m2_ref, v2_ref):
        g = p_ref[...].astype(F32)
        for k in range(nrecv):
            g = g + r_ref[k].astype(F32)
        if transposed:
            g = g.T
        g_ref[...] = g
        d_ref[...], m2_ref[...], v2_ref[...] = _adam_update(w_ref[...], g, m_ref[...], v_ref[...])

    blk = pl.BlockSpec((n, tr), lambda i, c_ref: (0, i)) if transposed else pl.BlockSpec((tr, n), lambda i, c_ref: (i, 0))
    spec = pltpu.PrefetchScalarGridSpec(
        num_scalar_prefetch=1, grid=(r // tr,),
        in_specs=[pl.BlockSpec((None, tr, n), lambda i, c_ref: (c_ref[0], i, 0)),
                  pl.BlockSpec((nrecv, tr, n), lambda i, c_ref: (0, i, 0)), blk, blk, blk],
        out_specs=[blk] * 4)
    return _call(body, name=name, grid_spec=spec, out_shape=[jax.ShapeDtypeStruct(w.shape, F32)] * 4,
                 compiler_params=_params("parallel"))(idx, p, r2, w, m, v)


def _sum_devices(allg, name):
    r, n = allg.shape[0] // N_DEV, allg.shape[1]

    def body(a_ref, o_ref):
        acc = a_ref[0:r, :]
        for k in range(1, N_DEV):
            acc = acc + a_ref[k * r:(k + 1) * r, :]
        o_ref[...] = acc

    return _call(body, name=name, out_shape=jax.ShapeDtypeStruct((r, n), F32))(allg)


def _adam_small(ws, gs, ms, vs, name):
    n = len(ws)

    def body(*refs):
        for i in range(n):
            d, m2, v2 = _adam_update(refs[i][...], refs[n + i][...], refs[2 * n + i][...], refs[3 * n + i][...])
            refs[4 * n + i][...] = d
            refs[5 * n + i][...] = m2
            refs[6 * n + i][...] = v2

    outs = _call(body, name=name, out_shape=[jax.ShapeDtypeStruct(w.shape, F32) for w in ws] * 3)(*ws, *gs, *ms, *vs)
    return outs[:n], outs[n:2 * n], outs[2 * n:]


def _pack_small_grads(full, halves, rcw, fcwg, fcwu, wrg, wig, lparts, name):
    nf, nh = len(full), len(halves)

    def body(*refs):
        o = refs[-1]
        o[...] = jnp.zeros_like(o)
        row = 0
        for r in refs[:nf]:
            for j in range(r.shape[1] // 1024):
                o[row:row + 1, :] = r[:, 1024 * j:1024 * (j + 1)]
                row += 1
        for k in range(0, nh, 2):
            o[row:row + 1, 0:512] = refs[nf + k][...]
            o[row:row + 1, 512:1024] = refs[nf + k + 1][...]
            row += 1
        rcw_ref, fg_ref, fu_ref, wrg_ref, wig_ref, l_ref = refs[nf + nh:nf + nh + 6]
        for k in range(2):
            o[row:row + 1, 0:512] = rcw_ref[2 * k:2 * k + 1, :]
            o[row:row + 1, 512:1024] = rcw_ref[2 * k + 1:2 * k + 2, :]
            row += 1
        for f_ref in (fg_ref, fu_ref):
            for k in range(FFN_CONV):
                for j in range(D_FF // 1024):
                    o[row:row + 1, :] = f_ref[k:k + 1, 1024 * j:1024 * (j + 1)]
                    row += 1
        assert row == 32
        for n in range(8):
            o[32:96, 64 * n:64 * n + 64] = wrg_ref[64 * n:64 * n + 64, 64 * n:64 * n + 64]
            o[32:96, 512 + 64 * n:512 + 64 * n + 64] = wig_ref[64 * n:64 * n + 64, 64 * n:64 * n + 64]
        o[96:97, :] = jnp.sum(l_ref[...], axis=0, keepdims=True)

    return _call(body, name=name, out_shape=jax.ShapeDtypeStruct((SMALL_ROWS, 1024), F32))(
        *full, *halves, rcw, fcwg, fcwu, wrg, wig, lparts)


def _block_diag(w):
    eye = jnp.eye(8, dtype=w.dtype)
    return (w[:, :, None, :] * eye[:, None, :, None]).reshape(512, 512)


def kernel(x, positions, g_mix, w_in, q_norm_g, k_norm_g, rec_conv_w, rec_conv_b, w_rg, b_rg, w_ig, b_ig, lru_lambda, g_attn_out, g_rec_out, w_out, g_ffn, w_up, ffn_conv_w, ffn_conv_b, w_down, loss_target, m_g_mix, m_w_in, m_q_norm_g, m_k_norm_g, m_rec_conv_w, m_rec_conv_b, m_w_rg, m_b_rg, m_w_ig, m_b_ig, m_lru_lambda, m_g_attn_out, m_g_rec_out, m_w_out, m_g_ffn, m_w_up, m_ffn_conv_w, m_ffn_conv_b, m_w_down, v_g_mix, v_w_in, v_q_norm_g, v_k_norm_g, v_rec_conv_w, v_rec_conv_b, v_w_rg, v_b_rg, v_w_ig, v_b_ig, v_lru_lambda, v_g_attn_out, v_g_rec_out, v_w_out, v_g_ffn, v_w_up, v_ffn_conv_w, v_ffn_conv_b, v_w_down):
    T = x.shape[1]
    ix, iy, ic = lax.axis_index("x"), lax.axis_index("y"), lax.axis_index("c")
    dev = 4 * ix + 2 * iy + ic
    xs = x.reshape(T, D_MODEL)
    tgt = loss_target.reshape(T, D_MODEL)
    pos = positions.reshape(T, 1)

    shards = {"w_in": (w_in[0], m_w_in[0], v_w_in[0]), "w_out": (w_out[0], m_w_out[0], v_w_out[0]),
              "w_up": (w_up[0], m_w_up[0], v_w_up[0]), "w_down": (w_down[0], m_w_down[0], v_w_down[0])}
    taps = jnp.concatenate([rec_conv_w.reshape(-1), ffn_conv_w.reshape(-1), jnp.zeros((4096 - 2560,), F32)]).reshape(8, 512)
    W_inT, taps_all = _all_gather([w_in[0].T.astype(BF16), taps], "ag_w_in")
    late = [w_out[0].astype(BF16), w_up[0].T.astype(BF16), w_down[0].astype(BF16)]
    ag_send, ag_recv, late_thru, land_thru, ag_token = _exchange_start(
        late, [_landing((N_DEV * s.shape[0], 1024), BF16, s, dev * s.shape[0]) for s in late], "gather", taps_all,
        "ag_late_start")
    taps_all = taps_all.reshape(N_DEV, 4096)
    rcw = taps_all[:, :256].reshape(8, 4, 64).transpose(1, 0, 2).reshape(4, REC_W)
    fcw = taps_all[:, 256:2560].reshape(8, 3, 768).transpose(1, 0, 2).reshape(3, 2 * D_FF)
    rcw8 = jnp.pad(rcw, ((0, 4), (0, 0)))
    fcw8 = jnp.pad(fcw, ((0, 5), (0, 0)))
    fcb = ffn_conv_b.reshape(1, 2 * D_FF)

    half = HEAD_DIM // 2
    inv_freq = ROPE_THETA ** (-jnp.arange(half, dtype=F32) / half)
    invf = jnp.tile(inv_freq, 2 * N_HEADS).reshape(1, ATTN_W)
    bd = jnp.asarray(np.kron(np.eye(2), np.full((HEAD_DIM, HEAD_DIM), 1.0 / HEAD_DIM)), BF16)
    qg = jnp.tile(q_norm_g.reshape(HEAD_DIM), N_HEADS).reshape(1, ATTN_W)
    kg = jnp.tile(k_norm_g.reshape(HEAD_DIM), N_HEADS).reshape(1, ATTN_W)
    wrg_bd = _block_diag(w_rg[0]).astype(BF16)
    wig_bd = _block_diag(w_ig[0]).astype(BF16)
    brg, big = b_rg.reshape(1, REC_W), b_ig.reshape(1, REC_W)

    h1 = _rmsnorm(xs, g_mix + ag_token[0, 0], "norm_mix")
    proj = _mm(h1, W_inT, "nt", F32, "in_proj", tn=1280)
    qf, kf = _qk_prep(proj, pos, invf, qg, kg, bd, "qk_prep")
    attn, lse = _attn_fwd(qf, kf, proj, "attn_fwd")
    mix = _attn_norm(attn, g_attn_out, "attn_norm")
    xc, hstate, mix = _rec_fwd(proj, mix, rcw8, rec_conv_b, wrg_bd, wig_bd, brg, big, lru_lambda, g_rec_out, "rec_fwd")
    _, (W_out, W_upT, W_down) = _exchange_wait(ag_send, ag_recv, late_thru, land_thru, "gather", hstate, "ag_late_wait")
    x2 = _mm(mix, W_out, "nn", F32, "out_proj", add=xs)

    h2 = _rmsnorm(x2, g_ffn, "norm_ffn")
    act, da, db, pg, pu = _up_proj_act(h2, W_upT, fcw8, fcb, "up_proj_act")
    dy, dyb, lparts = _mm(act, W_down, "nn", F32, "down_proj_loss", add=x2, loss_target=tgt, tm=512, tk=D_FF)

    g_down = _mm(act, dyb, "tn", BF16, "g_w_down", tk=2048)
    dpg, dpu, g_fcwg, g_fcwu, g_fcbg, g_fcbu = _ffn_bwd(dyb, W_down, da, db, pg, pu, fcw8, "ffn_bwd")
    g_upT = _mm(dpg, h2, "tn", BF16, "g_w_up_gate", tk=2048, o_rows=2 * D_FF)
    g_upT = _mm(dpu, h2, "tn", BF16, "g_w_up_up", tk=2048, into=g_upT, o_moff=D_FF // 1024)
    ffn_g = [g_upT.reshape(N_DEV, 2 * D_FF // N_DEV, 1024), g_down.reshape(N_DEV, D_FF // N_DEV, 1024)]
    rs_send, rs_recv, ffn_g, ffn_land, rs_token = _exchange_start(
        ffn_g, [_landing((N_PEERS,) + g.shape[1:], BF16) for g in ffn_g], "scatter", dpu, "rs_ffn_start")
    dx2, dx2b, g_gffn = _mm_norm_bwd([dpg, dpu], W_upT, x2, dy, g_ffn + rs_token[0, 0], "d_h2_norm_bwd", tm=1024, tk=1024)

    dmix = _mm(dx2b, W_out, "nt", F32, "d_mix")
    g_out = _mm(mix, dx2b, "tn", BF16, "g_w_out", tk=2048).reshape(N_DEV, D_MODEL // N_DEV, 1024)
    out_send, out_recv, (g_out,), out_land, out_token = _exchange_start(
        [g_out], [_landing((N_PEERS,) + g_out.shape[1:], BF16)], "scatter", dmix, "rs_out_start")
    do, delta, g_gattn = _attn_norm_bwd(dmix, attn, g_attn_out + out_token[0, 0], bd, "attn_norm_bwd")
    dqh, dkh, dv = _attn_bwd(qf, kf, proj, do, lse, delta, "attn_bwd")
    dqkv, g_qg, g_kg = _qk_prep_bwd(proj, dqh, dkh, dv, pos, invf, qg, kg, bd, "qk_prep_bwd")
    (drec, g_rcw, g_rcb, g_wrg, g_wig, g_brg, g_big, g_lam, g_grec) = _rec_bwd(
        dmix, proj, xc, hstate, rcw8, rec_conv_b, wrg_bd, wig_bd, brg, big, lru_lambda, g_rec_out, "rec_bwd")
    g_inT = _mm(dqkv, h1, "tn", BF16, "g_w_in_qkv", tm=512, o_rows=IN_W)
    g_inT = _mm(drec, h1, "tn", BF16, "g_w_in_rec", tm=512, into=g_inT, o_moff=3 * ATTN_W // 512)
    g_inT = g_inT.reshape(N_DEV, IN_W // N_DEV, 1024)
    in_send, in_recv, (g_inT,), in_land, in_token = _exchange_start(
        [g_inT], [_landing((N_PEERS,) + g_inT.shape[1:], BF16)], "scatter", drec, "rs_in_start")
    grad_x, _, g_gmix = _mm_norm_bwd([dqkv, drec], W_inT, xs, dx2, g_mix + in_token[0, 0], "d_h1_norm_bwd", tm=1024, tk=512)

    flat = _pack_small_grads([g_gmix, g_gffn, g_fcbg, g_fcbu], [g_rcb, g_brg, g_big, g_lam, g_gattn, g_grec, g_qg, g_kg],
                             g_rcw, g_fcwg, g_fcwu, g_wrg, g_wig, lparts.reshape(-1, D_MODEL), "pack_small_grads")
    sm_send, sm_recv, (flat,), sm_land, sm_token = _exchange_start(
        [flat], [_landing((N_DEV * SMALL_ROWS, 1024), F32, flat, dev * SMALL_ROWS)], "gather", grad_x, "ag_small_start")

    devi = jnp.reshape(dev, (1,)).astype(jnp.int32)
    ffn_g, ffn_land = _exchange_wait(rs_send, rs_recv, ffn_g, ffn_land, "scatter", sm_token, "rs_ffn_wait")
    (g_out,), out_land = _exchange_wait(out_send, out_recv, [g_out], out_land, "scatter", sm_token, "rs_out_wait")
    big_out = {"grad": {}, "delta": {}, "new_m": {}, "new_v": {}}

    def adam_big(nm, p, r):
        w_, m_, v_ = shards[nm]
        res = _adam_sharded(p, r, devi, w_, m_, v_, "adam_" + nm, transposed=nm in ("w_in", "w_up"))
        for kind, a in zip(("grad", "delta", "new_m", "new_v"), res):
            big_out[kind][nm] = a[None]
        return res[0]

    adam_big("w_up", ffn_g[0], ffn_land[0])
    adam_big("w_down", ffn_g[1], ffn_land[1])
    last = adam_big("w_out", g_out, out_land[0])
    _, (allg,) = _exchange_wait(sm_send, sm_recv, [flat], sm_land, "gather", last, "ag_small_wait")
    tot = _sum_devices(allg, "sum_small_grads")
    (g_inT,), in_land = _exchange_wait(in_send, in_recv, [g_inT], in_land, "scatter", tot, "rs_in_wait")
    adam_big("w_in", g_inT, in_land[0])

    half = lambda r, h, shape: tot[r, 512 * h:512 * h + 512].reshape(shape)
    blocks = lambda h: tot[32:96, 512 * h:512 * h + 512].reshape(64, 8, 64).transpose(1, 0, 2)[None]
    fcw_full = jnp.concatenate([tot[14:23].reshape(1, 3, D_FF), tot[23:32].reshape(1, 3, D_FF)], axis=2)
    g_small = {
        "g_mix": tot[0:1], "g_ffn": tot[1:2], "ffn_conv_b": tot[2:8].reshape(1, 2 * D_FF),
        "rec_conv_b": half(8, 0, (1, 512)), "b_rg": half(8, 1, (1, 8, 64)), "b_ig": half(9, 0, (1, 8, 64)),
        "lru_lambda": half(9, 1, (1, 512)), "g_attn_out": half(10, 0, (1, 512)), "g_rec_out": half(10, 1, (1, 512)),
        "q_norm_g": half(11, 0, (N_HEADS, HEAD_DIM)).sum(0)[None], "k_norm_g": half(11, 1, (N_HEADS, HEAD_DIM)).sum(0)[None],
        "w_rg": blocks(0), "w_ig": blocks(1),
        "rec_conv_w": lax.dynamic_slice(tot[12:14].reshape(1, 4, REC_W), (0, 0, 64 * dev), (1, 4, 64)),
        "ffn_conv_w": lax.dynamic_slice(fcw_full, (0, 0, 768 * dev), (1, 3, 768))}
    loss = 0.5 / D_MODEL * jnp.sum(tot[96])
    given = dict(rec_conv_w=rec_conv_w, ffn_conv_w=ffn_conv_w,g_mix=g_mix, q_norm_g=q_norm_g, k_norm_g=k_norm_g, rec_conv_b=rec_conv_b, w_rg=w_rg, b_rg=b_rg, w_ig=w_ig,
                 b_ig=b_ig, lru_lambda=lru_lambda, g_attn_out=g_attn_out, g_rec_out=g_rec_out, g_ffn=g_ffn, ffn_conv_b=ffn_conv_b)
    given_m = dict(rec_conv_w=m_rec_conv_w, ffn_conv_w=m_ffn_conv_w, g_mix=m_g_mix, q_norm_g=m_q_norm_g, k_norm_g=m_k_norm_g, rec_conv_b=m_rec_conv_b, w_rg=m_w_rg, b_rg=m_b_rg,
                   w_ig=m_w_ig, b_ig=m_b_ig, lru_lambda=m_lru_lambda, g_attn_out=m_g_attn_out, g_rec_out=m_g_rec_out,
                   g_ffn=m_g_ffn, ffn_conv_b=m_ffn_conv_b)
    given_v = dict(rec_conv_w=v_rec_conv_w, ffn_conv_w=v_ffn_conv_w, g_mix=v_g_mix, q_norm_g=v_q_norm_g, k_norm_g=v_k_norm_g, rec_conv_b=v_rec_conv_b, w_rg=v_w_rg, b_rg=v_b_rg,
                   w_ig=v_w_ig, b_ig=v_b_ig, lru_lambda=v_lru_lambda, g_attn_out=v_g_attn_out, g_rec_out=v_g_rec_out,
                   g_ffn=v_g_ffn, ffn_conv_b=v_ffn_conv_b)
    small = sorted(given)
    ds, m2s, v2s = _adam_small([given[k] for k in small], [g_small[k] for k in small], [given_m[k] for k in small],
                               [given_v[k] for k in small], "adam_small")
    small_out = {"grad": g_small, "delta": dict(zip(small, ds)), "new_m": dict(zip(small, m2s)), "new_v": dict(zip(small, v2s))}

    order = ("g_mix", "w_in", "q_norm_g", "k_norm_g", "rec_conv_w", "rec_conv_b", "w_rg", "b_rg", "w_ig", "b_ig",
             "lru_lambda", "g_attn_out", "g_rec_out", "w_out", "g_ffn", "w_up", "ffn_conv_w", "ffn_conv_b", "w_down")
    outs = [loss, grad_x.reshape(1, T, D_MODEL)]
    for kind in ("grad", "delta", "new_m", "new_v"):
        for name in order:
            outs.append(big_out[kind][name] if name in big_out[kind] else small_out[kind][name])
    return tuple(outs)
```

```python
import math

import numpy as np
import jax
import jax.numpy as jnp
from jax import lax
from jax.experimental import pallas as pl
from jax.experimental.pallas import tpu as pltpu

F32 = jnp.float32
BF16 = jnp.bfloat16

D_MODEL = 1024
HEAD_DIM = 64
ATTN_W = 512
REC_W = 512
N_HEADS = 8
D_FF = 3072
IN_W = 2560
REC_CONV = 4
FFN_CONV = 3
LRU_C = 8.0
ROPE_THETA = 10000.0
EPS = 1e-6
NEG_INF = -1e30
QBLK = 128
DILATIONS = (1, 4, 16)
N_DEV = 8
SMALL_ROWS = 128
ADAM_LR, ADAM_B1, ADAM_B2, ADAM_EPS, ADAM_WD, ADAM_STEP = 0.001, 0.9, 0.999, 1e-08, 0.01, 10
MESH = pl.DeviceIdType.MESH
ANY = pl.BlockSpec(memory_space=pl.ANY)


def _call(body, *, name, **kw):
    return pl.pallas_call(body, name=name, **kw)


def _params(*sem):
    return pltpu.CompilerParams(dimension_semantics=sem, vmem_limit_bytes=56 * 1024 * 1024)


def _gelu(x):
    c = math.sqrt(2.0 / math.pi)
    return 0.5 * x * (1.0 + jnp.tanh(c * (x + 0.044715 * (x * x * x))))


def _gelu_and_grad(x):
    c = math.sqrt(2.0 / math.pi)
    t = jnp.tanh(c * (x + 0.044715 * (x * x * x)))
    g = 0.5 * x * (1.0 + t)
    dg = 0.5 * (1.0 + t) + 0.5 * x * (1.0 - t * t) * (c * (1.0 + 3.0 * 0.044715 * (x * x)))
    return g, dg


def _sigmoid(x):
    return 1.0 / (1.0 + jnp.exp(-x))


def _softplus_neg(lam):
    y = jnp.exp(-jnp.abs(lam))
    u = 1.0 + y
    log1p = jnp.where(u == 1.0, y, jnp.log(u) * y / jnp.where(u == 1.0, 1.0, u - 1.0))
    return jnp.maximum(-lam, 0.0) + log1p


_NN = (((1,), (0,)), ((), ()))
_NT = (((1,), (1,)), ((), ()))
_TN = (((0,), (0,)), ((), ()))


def _dot(a, b, dims=_NN):
    return lax.dot_general(a, b, dims, preferred_element_type=F32)


def _group_mean(v, bd):
    hi = v.astype(BF16)
    lo = (v - hi.astype(F32)).astype(BF16)
    w = bd.shape[0]
    return jnp.concatenate([_dot(hi[:, c:c + w], bd) + _dot(lo[:, c:c + w], bd) for c in range(0, v.shape[1], w)], axis=1)


def _rope_tables(pos_ref, invf_ref):
    ang = pos_ref[...].astype(F32) * invf_ref[:, :2 * HEAD_DIM]
    reps = invf_ref.shape[1] // (2 * HEAD_DIM)
    return jnp.tile(jnp.cos(ang), (1, reps)), jnp.tile(jnp.sin(ang), (1, reps))


def _shift_down(x, halo, s):
    rolled = pltpu.roll(x, s, 0)
    hr = pltpu.roll(halo, s, 0)
    row = lax.broadcasted_iota(jnp.int32, hr.shape, 0)
    first = jnp.where(row < s, hr, rolled[:8])
    return jnp.concatenate([first, rolled[8:]], axis=0)


def _shift_up(x, halo, s):
    n = x.shape[0]
    rolled = pltpu.roll(x, n - s, 0)
    hr = pltpu.roll(halo, 8 - s, 0)
    row = lax.broadcasted_iota(jnp.int32, hr.shape, 0)
    last = jnp.where(row >= 8 - s, hr, rolled[n - 8:])
    return jnp.concatenate([rolled[:n - 8], last], axis=0)


def _scan_fwd(a, u):
    n, w = a.shape
    a3, u3 = a.reshape(n // 8, 8, w), u.reshape(n // 8, 8, w)
    row = lax.broadcasted_iota(jnp.int32, a3.shape, 1)
    for s in (1, 2, 4):
        a_s = jnp.where(row < s, 1.0, pltpu.roll(a3, s, 1))
        u_s = jnp.where(row < s, 0.0, pltpu.roll(u3, s, 1))
        u3 = u3 + a3 * u_s
        a3 = a3 * a_s
    ps, hs = [a3[0]], [u3[0]]
    for k in range(1, n // 8):
        ps.append(a3[k] * ps[-1][7:8, :])
        hs.append(u3[k] + a3[k] * hs[-1][7:8, :])
    return jnp.concatenate(ps, axis=0), jnp.concatenate(hs, axis=0)


def _scan_bwd(b, v):
    n, w = b.shape
    b3, v3 = b.reshape(n // 8, 8, w), v.reshape(n // 8, 8, w)
    row = lax.broadcasted_iota(jnp.int32, b3.shape, 1)
    for s in (1, 2, 4):
        b_s = jnp.where(row >= 8 - s, 1.0, pltpu.roll(b3, 8 - s, 1))
        v_s = jnp.where(row >= 8 - s, 0.0, pltpu.roll(v3, 8 - s, 1))
        v3 = v3 + b3 * v_s
        b3 = b3 * b_s
    last = n // 8 - 1
    ps, gs = [b3[last]], [v3[last]]
    for k in range(last - 1, -1, -1):
        ps.append(b3[k] * ps[-1][0:1, :])
        gs.append(v3[k] + b3[k] * gs[-1][0:1, :])
    return jnp.concatenate(ps[::-1], axis=0), jnp.concatenate(gs[::-1], axis=0)


def _rot_half(y):
    n = y.shape[1]
    lane = lax.broadcasted_iota(jnp.int32, y.shape, 1) & (HEAD_DIM - 1)
    return jnp.where(lane < HEAD_DIM // 2, -pltpu.roll(y, n - HEAD_DIM // 2, 1), pltpu.roll(y, HEAD_DIM // 2, 1))


def _row_tile(r, cap=256):
    return max(t for t in range(16, cap + 1, 16) if r % t == 0)


def _all_gather(shards, name):
    na = len(shards)
    ms = [s.shape[0] for s in shards]

    def body(*refs):
        x_refs, out_refs = refs[:na], refs[na:2 * na]
        send_sems, recv_sems, local_sems = refs[2 * na:]
        x, y, c = lax.axis_index("x"), lax.axis_index("y"), lax.axis_index("c")
        me, sibling = (x, y, c), (x, y, 1 - c)
        chips = [(1 - x, y), (x, 1 - y), (1 - x, 1 - y)]

        def rows(a, px, py, pc):
            return out_refs[a].at[pl.ds((4 * px + 2 * py + pc) * ms[a], ms[a]), :]

        def copy(a, k, block, to, src=None):
            return pltpu.make_async_remote_copy(
                src_ref=rows(a, *block) if src is None else src, dst_ref=rows(a, *block),
                send_sem=send_sems.at[7 * a + k], recv_sem=recv_sems.at[7 * a + k], device_id=to, device_id_type=MESH)

        mine = [pltpu.make_async_copy(x_refs[a], rows(a, *me), local_sems.at[a]) for a in range(na)]
        first = []
        for a in range(na):
            mine[a].start()
            first.append(copy(a, 0, me, sibling, src=x_refs[a]))
            first += [copy(a, 1 + j, me, (*chip, c), src=x_refs[a]) for j, chip in enumerate(chips)]
        for cp in first:
            cp.start()
        passed = []
        for a in range(na):
            for j, chip in enumerate(chips):
                copy(a, 1 + j, (*chip, c), me).wait_recv()
                fw = copy(a, 4 + j, (*chip, c), sibling)
                fw.start()
                passed.append(fw)
        for a in range(na):
            copy(a, 0, sibling, me).wait_recv()
            for j, chip in enumerate(chips):
                copy(a, 4 + j, (*chip, 1 - c), me).wait_recv()
        for cp in first + passed:
            cp.wait_send()
        for cp in mine:
            cp.wait()

    return _call(
        body, name=name, out_shape=[jax.ShapeDtypeStruct((N_DEV * s.shape[0], s.shape[1]), s.dtype) for s in shards],
        in_specs=[ANY] * na, out_specs=[ANY] * na,
        scratch_shapes=[pltpu.SemaphoreType.DMA((7 * na,)), pltpu.SemaphoreType.DMA((7 * na,)),
                        pltpu.SemaphoreType.DMA((na,))],
    )(*shards)


HBM = pl.BlockSpec(memory_space=pltpu.HBM)
SEM = pl.BlockSpec(memory_space=pltpu.SEMAPHORE)
EFFECT = pltpu.SideEffectType.DATAFLOW_SIDE_EFFECTING
N_PEERS = N_DEV - 1


def _peer(k):
    x, y, c = lax.axis_index("x"), lax.axis_index("y"), lax.axis_index("c")
    b = k + 1
    flip = lambda v, bit: 1 - v if bit else v
    return flip(x, b & 4), flip(y, b & 2), flip(c, b & 1)


def _in_hbm(a):
    return pltpu.with_memory_space_constraint(a, pltpu.HBM)


def _split_copy_descr(na, kind, src_refs, land_refs, send_sems, recv_sems):
    x, y, c = lax.axis_index("x"), lax.axis_index("y"), lax.axis_index("c")
    me = 4 * x + 2 * y + c
    copies = []
    for a in range(na):
        for k in range(N_PEERS):
            px, py, pc = _peer(k)
            if kind == "gather":
                m = src_refs[a].shape[0]
                src, dst = src_refs[a], land_refs[a].at[pl.ds(me * m, m), :]
            else:
                src, dst = src_refs[a].at[4 * px + 2 * py + pc], land_refs[a].at[k]
            copies.append(pltpu.make_async_remote_copy(
                src_ref=src, dst_ref=dst, send_sem=send_sems.at[N_PEERS * a + k], recv_sem=recv_sems.at[N_PEERS * a + k],
                device_id=(px, py, pc), device_id_type=MESH))
    return copies


def _landing(shape, dtype, own=None, at=None):
    buf = lax.empty(shape, dtype)
    return buf if own is None else lax.dynamic_update_slice(buf, own, (at, 0))


def _exchange_start(srcs, lands, kind, after, name):
    na = len(srcs)
    land_shapes = [l.shape for l in lands]

    def body(*refs):
        src_refs, land_refs = refs[:na], refs[na:2 * na]
        send_sems, recv_sems = refs[2 * na + 1], refs[2 * na + 2]
        token = refs[-1]
        for cp in _split_copy_descr(na, kind, src_refs, land_refs, send_sems, recv_sems):
            cp.start()
        token[...] = jnp.zeros_like(token)

    lands = [_in_hbm(l) for l in lands]
    sem = pltpu.SemaphoreType.DMA((N_PEERS * na,))
    outs = _call(
        body, name=name,
        out_shape=[sem, sem] + [pltpu.HBM(s.shape, s.dtype) for s in srcs] + [pltpu.HBM(s, srcs[0].dtype) for s in land_shapes]
        + [jax.ShapeDtypeStruct((8, 128), F32)],
        in_specs=[HBM] * (2 * na) + [ANY], out_specs=[SEM, SEM] + [HBM] * (2 * na) + [pl.BlockSpec(memory_space=pltpu.VMEM)],
        input_output_aliases={i: 2 + i for i in range(2 * na)},
        compiler_params=pltpu.CompilerParams(has_side_effects=EFFECT),
    )(*[_in_hbm(s) for s in srcs], *lands, after)
    return outs[0], outs[1], outs[2:2 + na], outs[2 + na:2 + 2 * na], outs[-1]


def _exchange_wait(send_sems, recv_sems, srcs, lands, kind, after, name):
    na = len(srcs)

    def body(*refs):
        src_refs, land_refs = refs[:na], refs[na:2 * na]
        s_sems, r_sems = refs[2 * na], refs[2 * na + 1]
        for cp in _split_copy_descr(na, kind, src_refs, land_refs, s_sems, r_sems):
            cp.wait_send()
            cp.wait_recv()

    outs = _call(
        body, name=name, out_shape=[pltpu.HBM(s.shape, s.dtype) for s in srcs] + [pltpu.HBM(l.shape, l.dtype) for l in lands],
        in_specs=[HBM] * (2 * na) + [SEM, SEM, ANY], out_specs=[HBM] * (2 * na),
        input_output_aliases={i: i for i in range(2 * na)},
        compiler_params=pltpu.CompilerParams(has_side_effects=EFFECT),
    )(*srcs, *lands, send_sems, recv_sems, after)
    return outs[:na], outs[na:]


def _mm(a, b, mode, out_dtype, name, add=None, tm=1024, tn=1024, tk=1024, b_noff=0, b_koff=0,
        n=None, k=None, into=None, o_rows=None, o_moff=0, loss_target=None):
    if mode == "tn":
        K, M = a.shape
    else:
        M, K = a.shape
    N = n if n is not None else (b.shape[0] if mode == "nt" else b.shape[1])
    if k is not None:
        assert k == K
    tm, tn, tk = min(tm, M), min(tn, N), min(tk, K)
    assert M % tm == 0 and N % tn == 0 and K % tk == 0, (name, M, N, K)
    nk = K // tk
    if mode == "nn":
        a_spec = pl.BlockSpec((tm, tk), lambda i, j, kk: (i, kk))
        b_spec, dims = pl.BlockSpec((tk, tn), lambda i, j, kk: (kk + b_koff, j + b_noff)), _NN
    elif mode == "nt":
        a_spec = pl.BlockSpec((tm, tk), lambda i, j, kk: (i, kk))
        b_spec, dims = pl.BlockSpec((tn, tk), lambda i, j, kk: (j + b_noff, kk + b_koff)), _NT
    else:
        a_spec = pl.BlockSpec((tk, tm), lambda i, j, kk: (kk, i))
        b_spec, dims = pl.BlockSpec((tk, tn), lambda i, j, kk: (kk + b_koff, j + b_noff)), _TN
    o_spec = pl.BlockSpec((tm, tn), lambda i, j, kk: (i + o_moff, j))
    has_add, has_into, has_loss = add is not None, into is not None, loss_target is not None
    assert not has_loss or (has_add and tn == N and not has_into)
    n_in = 2 + has_add + has_loss + has_into

    def body(*refs):
        a_ref, b_ref = refs[0], refs[1]
        add_ref = refs[2] if has_add else None
        outs = refs[n_in:]

        def finish(r):
            if has_add:
                r = r + add_ref[...]
            if has_loss:
                e = r - refs[3][...]
                dy = e * (1.0 / N)
                outs[0][...] = dy
                outs[1][...] = dy.astype(BF16)
                outs[2][...] = jnp.sum(e * e, axis=0, keepdims=True)[None]
            else:
                outs[0][...] = r.astype(out_dtype)

        if nk == 1:
            finish(_dot(a_ref[...], b_ref[...], dims))
        else:
            acc = refs[-1]
            kk = pl.program_id(2)

            @pl.when(kk == 0)
            def _():
                acc[...] = _dot(a_ref[...], b_ref[...], dims)

            @pl.when((kk > 0) & (kk < nk - 1))
            def _():
                acc[...] += _dot(a_ref[...], b_ref[...], dims)

            @pl.when(kk == nk - 1)
            def _():
                finish(acc[...] + _dot(a_ref[...], b_ref[...], dims))

    tile = pl.BlockSpec((tm, tn), lambda i, j, kk: (i, j))
    ins = [a, b] + ([add] if has_add else []) + ([loss_target] if has_loss else []) + ([into] if has_into else [])
    specs = [a_spec, b_spec] + [tile] * (has_add + has_loss) + ([ANY] if has_into else [])
    rows = into.shape[0] if has_into else (o_rows if o_rows is not None else M)
    if has_loss:
        out_specs = [tile, tile, pl.BlockSpec((1, 1, N), lambda i, j, kk: (i, 0, 0))]
        out_shape = [jax.ShapeDtypeStruct((M, N), F32), jax.ShapeDtypeStruct((M, N), BF16), jax.ShapeDtypeStruct((M // tm, 1, N), F32)]
    else:
        out_specs, out_shape = o_spec, jax.ShapeDtypeStruct((rows, N), out_dtype)
    return _call(
        body, name=name, grid=(M // tm, N // tn, nk), in_specs=specs, out_specs=out_specs, out_shape=out_shape,
        scratch_shapes=[pltpu.VMEM((tm, tn), F32)] if nk > 1 else [],
        input_output_aliases={len(ins) - 1: 0} if has_into else {},
        compiler_params=_params("parallel", "parallel", "arbitrary"),
    )(*ins)


def _mm_norm_bwd(parts, b, x, resid, g, name, tm=512, tk=512):
    T, N = x.shape
    counts = [p.shape[1] // tk for p in parts]
    starts = [sum(counts[:i]) for i in range(len(parts))]
    nsteps = sum(counts)
    assert all(p.shape[1] % tk == 0 for p in parts) and b.shape == (nsteps * tk, N)
    npart = len(parts)

    def body(*refs):
        a_refs, b_ref, x_ref, res_ref, g_ref = refs[:npart], refs[npart], refs[npart + 1], refs[npart + 2], refs[npart + 3]
        dx_ref, dxb_ref, dg_ref, acc = refs[npart + 4:]
        i, s = pl.program_id(0), pl.program_id(1)

        @pl.when((i == 0) & (s == 0))
        def _():
            dg_ref[...] = jnp.zeros_like(dg_ref)

        for p in range(npart):
            @pl.when((s >= starts[p]) & (s < starts[p] + counts[p]))
            def _(p=p):
                d = _dot(a_refs[p][...], b_ref[...])

                @pl.when(s == 0)
                def _():
                    acc[...] = d

                @pl.when(s > 0)
                def _():
                    acc[...] += d

        @pl.when(s == nsteps - 1)
        def _():
            xv, dhv = x_ref[...], acc[...]
            r = lax.rsqrt(jnp.mean(xv * xv, axis=-1, keepdims=True) + EPS)
            gd = dhv * g_ref[...]
            m = jnp.mean(gd * xv, axis=-1, keepdims=True)
            dx = res_ref[...] + r * gd - xv * (r * r * r) * m
            dx_ref[...] = dx
            dxb_ref[...] = dx.astype(BF16)
            dg_ref[...] += jnp.sum(dhv * xv * r, axis=0, keepdims=True)

    a_specs = [pl.BlockSpec((tm, tk), lambda i, s, st=st, c=c: (i, jnp.clip(s - st, 0, c - 1))) for st, c in zip(starts, counts)]
    row = pl.BlockSpec((tm, N), lambda i, s: (i, 0))
    vec = pl.BlockSpec((1, N), lambda i, s: (0, 0))
    return _call(
        body, name=name, grid=(T // tm, nsteps),
        in_specs=a_specs + [pl.BlockSpec((tk, N), lambda i, s: (s, 0)), row, row, vec], out_specs=[row, row, vec],
        out_shape=[jax.ShapeDtypeStruct((T, N), F32), jax.ShapeDtypeStruct((T, N), BF16), jax.ShapeDtypeStruct((1, N), F32)],
        scratch_shapes=[pltpu.VMEM((tm, N), F32)], compiler_params=_params("arbitrary", "arbitrary"),
    )(*parts, b, x, resid, g)


def _norm_proj(x, g, wT, name, tm=1024, tn=1280):
    T, K = x.shape
    N = wT.shape[0]

    def body(x_ref, g_ref, w_ref, o_ref, h_ref):
        xv = x_ref[...]
        r = lax.rsqrt(jnp.mean(xv * xv, axis=-1, keepdims=True) + EPS)
        hv = (xv * r * g_ref[...]).astype(BF16)

        @pl.when(pl.program_id(1) == 0)
        def _():
            h_ref[...] = hv

        o_ref[...] = _dot(hv, w_ref[...], _NT)

    return _call(
        body, name=name, grid=(T // tm, N // tn),
        in_specs=[pl.BlockSpec((tm, K), lambda i, j: (i, 0)), pl.BlockSpec((1, K), lambda i, j: (0, 0)),
                  pl.BlockSpec((tn, K), lambda i, j: (j, 0))],
        out_specs=[pl.BlockSpec((tm, tn), lambda i, j: (i, j)), pl.BlockSpec((tm, K), lambda i, j: (i, 0))],
        out_shape=[jax.ShapeDtypeStruct((T, N), F32), jax.ShapeDtypeStruct((T, K), BF16)],
        compiler_params=_params("parallel", "arbitrary"),
    )(x, g, wT)


def _qk_prep(proj, pos, invf, qg, kg, bd, name, tm=512):
    T = proj.shape[0]

    def body(q_ref, k_ref, pos_ref, invf_ref, qg_ref, kg_ref, bd_ref, qo_ref, ko_ref):
        cos, sin = _rope_tables(pos_ref, invf_ref)

        def prep(xv, gv, scale):
            r = lax.rsqrt(_group_mean(xv * xv, bd_ref[...]) + EPS)
            yv = xv * r * gv
            return ((yv * cos + _rot_half(yv) * sin) * scale).astype(BF16).astype(F32)

        qo_ref[...] = prep(q_ref[...], qg_ref[...], HEAD_DIM ** -0.5)
        ko_ref[...] = prep(k_ref[...], kg_ref[...], 1.0)

    col = lambda j: pl.BlockSpec((tm, ATTN_W), lambda i, j=j: (i, j))
    vec = pl.BlockSpec((1, ATTN_W), lambda i: (0, 0))
    out = pl.BlockSpec((tm, ATTN_W), lambda i: (i, 0))
    return _call(
        body, name=name, grid=(T // tm,),
        in_specs=[col(0), col(1), pl.BlockSpec((tm, 1), lambda i: (i, 0)), vec, vec, vec,
                  pl.BlockSpec((2 * HEAD_DIM, 2 * HEAD_DIM), lambda i: (0, 0))],
        out_specs=[out, out], out_shape=[jax.ShapeDtypeStruct((T, ATTN_W), F32)] * 2,
        compiler_params=_params("parallel"),
    )(proj, proj, pos, invf, qg, kg, bd)


def _qk_prep_bwd(proj, dqh, dkh, dv, pos, invf, qg, kg, bd, name, tm=512):
    T = proj.shape[0]

    def body(q_ref, k_ref, dq_ref, dk_ref, dv_ref, pos_ref, invf_ref, qg_ref, kg_ref, bd_ref, o_ref, gq_ref, gk_ref):
        @pl.when(pl.program_id(0) == 0)
        def _():
            gq_ref[...] = jnp.zeros_like(gq_ref)
            gk_ref[...] = jnp.zeros_like(gk_ref)

        cos, sin = _rope_tables(pos_ref, invf_ref)

        def back(xv, gv, dz, scale):
            dz = dz * scale
            dy = dz * cos - _rot_half(dz * sin)
            r = lax.rsqrt(_group_mean(xv * xv, bd_ref[...]) + EPS)
            gd = dy * gv
            m = _group_mean(gd * xv, bd_ref[...])
            dx = r * gd - xv * (r * r * r) * m
            return dx, jnp.sum(dy * xv * r, axis=0, keepdims=True)

        dxq, gs = back(q_ref[...], qg_ref[...], dq_ref[...], HEAD_DIM ** -0.5)
        gq_ref[...] += gs
        dxk, gs = back(k_ref[...], kg_ref[...], dk_ref[...], 1.0)
        gk_ref[...] += gs
        o_ref[...] = jnp.concatenate([dxq.astype(BF16), dxk.astype(BF16), dv_ref[...].astype(BF16)], axis=1)

    col = lambda j: pl.BlockSpec((tm, ATTN_W), lambda i, j=j: (i, j))
    row = pl.BlockSpec((tm, ATTN_W), lambda i: (i, 0))
    vec = pl.BlockSpec((1, ATTN_W), lambda i: (0, 0))
    return _call(
        body, name=name, grid=(T // tm,),
        in_specs=[col(0), col(1), row, row, row, pl.BlockSpec((tm, 1), lambda i: (i, 0)), vec, vec, vec,
                  pl.BlockSpec((2 * HEAD_DIM, 2 * HEAD_DIM), lambda i: (0, 0))],
        out_specs=[pl.BlockSpec((tm, 3 * ATTN_W), lambda i: (i, 0)), vec, vec],
        out_shape=[jax.ShapeDtypeStruct((T, 3 * ATTN_W), BF16)] + [jax.ShapeDtypeStruct((1, ATTN_W), F32)] * 2,
        compiler_params=_params("arbitrary"),
    )(proj, proj, dqh, dkh, dv, pos, invf, qg, kg, bd)


def _ld(ref, start, size, dil):
    return ref[pl.ds(start, size), :] if dil == 1 else ref[pl.ds(start, size, stride=dil), :]


def _st(ref, start, size, dil, val):
    if dil == 1:
        ref[pl.ds(start, size), :] = val
    else:
        ref[pl.ds(start, size, stride=dil), :] = val


def _attn_geometry(T, dil):
    nb = T // dil // QBLK
    if nb == 2:
        return 1, 2 * QBLK, 2 * QBLK
    return nb, QBLK, (2 * QBLK if nb >= 2 else QBLK)


ATTN_UNROLL = 4


def _attn_unit(j, u, dil, nit):
    return ATTN_UNROLL * j + u if dil >= ATTN_UNROLL else j + u * (nit // ATTN_UNROLL)


def _attn_block(it, dil, qb, kw):
    c, n = it & (dil - 1), lax.shift_right_logical(it, dil.bit_length() - 1)
    sq = n * (qb * dil) + c
    sk = jnp.maximum(n - (kw // qb - 1), 0) * (qb * dil) + c
    qi = lax.broadcasted_iota(jnp.int32, (2 * qb, kw), 0) & (qb - 1)
    kj = lax.broadcasted_iota(jnp.int32, (2 * qb, kw), 1)
    rel = jnp.where(n > 0, kw - qb, 0) + qi - kj
    return sq, sk, (rel >= 0) & (rel <= QBLK)


def _stack_heads(xv, head0):
    z = jnp.zeros_like(xv)
    return jnp.concatenate([jnp.where(head0, xv, z), jnp.where(head0, z, xv)], axis=0)


def _unstack_heads(x2, head0):
    qb = x2.shape[0] // 2
    return jnp.where(head0, x2[:qb], x2[qb:])


def _attn_fwd(qf, kf, proj, name):
    T = qf.shape[0]

    def body(q_ref, k_ref, v_ref, o_ref, lse_ref):
        for bi, dil in enumerate(DILATIONS):
            nb, qb, kw = _attn_geometry(T, dil)
            nit = nb * dil
            head0 = lax.broadcasted_iota(jnp.int32, (qb, 2 * HEAD_DIM), 1) < HEAD_DIM

            def step(j, carry, bi=bi, dil=dil, qb=qb, kw=kw, nit=nit, head0=head0):
                units = []
                for u in range(ATTN_UNROLL):
                    sq, sk, ok = _attn_block(_attn_unit(j, u, dil, nit), dil, qb, kw)
                    old = (_ld(o_ref, sq, qb, dil), _ld(lse_ref, sq, qb, dil)) if bi > 0 else None
                    units.append((sq, ok, _ld(q_ref, sq, qb, dil).astype(BF16), _ld(k_ref, sk, kw, dil).astype(BF16),
                                  _ld(v_ref, sk, kw, dil).astype(BF16), old))
                results = []
                for sq, ok, qv, kv, vv, old in units:
                    s = jnp.where(ok, _dot(_stack_heads(qv, head0), kv, _NT), NEG_INF)
                    m = jnp.max(s, axis=-1, keepdims=True)
                    p = jnp.exp(s - m).astype(BF16)
                    acc = _dot(p, jnp.concatenate([vv, jnp.ones_like(vv)], axis=1))
                    l = acc[:, 2 * HEAD_DIM:]
                    o_new = _unstack_heads(acc[:, :2 * HEAD_DIM] / l, head0)
                    l_new = _unstack_heads(m + jnp.log(l), head0)
                    if bi > 0:
                        o_old, l_old = old
                        mx = jnp.maximum(l_old, l_new)
                        e0, e1 = jnp.exp(l_old - mx), jnp.exp(l_new - mx)
                        z = e0 + e1
                        o_new = (e0 * o_old + e1 * o_new) / z
                        l_new = mx + jnp.log(z)
                    results.append((sq, o_new, l_new))
                for sq, o_new, l_new in results:
                    _st(o_ref, sq, qb, dil, o_new)
                    _st(lse_ref, sq, qb, dil, l_new)
                return carry

            lax.fori_loop(0, nit // ATTN_UNROLL, step, 0)

    blk = lambda off: pl.BlockSpec((T, 2 * HEAD_DIM), lambda hp, off=off: (0, off + hp))
    return _call(
        body, name=name, grid=(4,), in_specs=[blk(0), blk(0), blk(8)], out_specs=[blk(0), blk(0)],
        out_shape=[jax.ShapeDtypeStruct((T, ATTN_W), F32)] * 2, compiler_params=_params("parallel"),
    )(qf, kf, proj)


def _attn_bwd(qf, kf, proj, do, lse, delta, name):
    T = qf.shape[0]

    def body(q_ref, k_ref, v_ref, do_ref, lse_ref, dl_ref, dq_ref, dk_ref, dv_ref):
        for ref in (dq_ref, dk_ref, dv_ref):
            ref[...] = jnp.zeros_like(ref)
        for dil in DILATIONS:
            nb, qb, kw = _attn_geometry(T, dil)
            nit = nb * dil
            head0 = lax.broadcasted_iota(jnp.int32, (qb, 2 * HEAD_DIM), 1) < HEAD_DIM

            def step(j, carry, dil=dil, qb=qb, kw=kw, nit=nit, head0=head0):
                units = []
                for u in range(ATTN_UNROLL):
                    sq, sk, ok = _attn_block(_attn_unit(j, u, dil, nit), dil, qb, kw)
                    lsev, dlv = _ld(lse_ref, sq, qb, dil), _ld(dl_ref, sq, qb, dil)
                    units.append((sq, sk, ok, _ld(q_ref, sq, qb, dil).astype(BF16), _ld(do_ref, sq, qb, dil).astype(BF16),
                                  jnp.concatenate([lsev[:, 0:1], lsev[:, HEAD_DIM:HEAD_DIM + 1]], axis=0),
                                  jnp.concatenate([dlv[:, 0:1], dlv[:, HEAD_DIM:HEAD_DIM + 1]], axis=0),
                                  _ld(k_ref, sk, kw, dil).astype(BF16), _ld(v_ref, sk, kw, dil).astype(BF16),
                                  _ld(dq_ref, sq, qb, dil), _ld(dk_ref, sk, kw, dil), _ld(dv_ref, sk, kw, dil)))
                results = []
                for sq, sk, ok, qv, dov, lse2, dl2, kv, vv, dq0, dk0, dv0 in units:
                    q2, do2 = _stack_heads(qv, head0), _stack_heads(dov, head0)
                    p = jnp.where(ok, jnp.exp(_dot(q2, kv, _NT) - lse2), 0.0)
                    ds = (p * (_dot(do2, vv, _NT) - dl2)).astype(BF16)
                    results.append((sq, sk, dq0 + _unstack_heads(_dot(ds, kv), head0),
                                    dk0 + _dot(ds, q2, _TN), dv0 + _dot(p.astype(BF16), do2, _TN)))
                for sq, sk, dq, dk, dv in results:
                    _st(dq_ref, sq, qb, dil, dq)
                    _st(dk_ref, sk, kw, dil, dk)
                    _st(dv_ref, sk, kw, dil, dv)
                return carry

            lax.fori_loop(0, nit // ATTN_UNROLL, step, 0)

    blk = lambda off: pl.BlockSpec((T, 2 * HEAD_DIM), lambda hp, off=off: (0, off + hp))
    return _call(
        body, name=name, grid=(4,), in_specs=[blk(0), blk(0), blk(8), blk(0), blk(0), blk(0)], out_specs=[blk(0)] * 3,
        out_shape=[jax.ShapeDtypeStruct((T, ATTN_W), F32)] * 3, compiler_params=_params("parallel"),
    )(qf, kf, proj, do, lse, delta)


def _attn_norm(attn, g, name, tm=512):
    T = attn.shape[0]

    def body(a_ref, g_ref, o_ref):
        av = a_ref[...]
        r = lax.rsqrt(jnp.mean(av * av, axis=-1, keepdims=True) + EPS)
        o_ref[...] = (av * r * g_ref[...]).astype(BF16)

    row = pl.BlockSpec((tm, ATTN_W), lambda i: (i, 0))
    return _call(
        body, name=name, grid=(T // tm,), in_specs=[row, pl.BlockSpec((1, ATTN_W), lambda i: (0, 0))], out_specs=row,
        out_shape=jax.ShapeDtypeStruct((T, 2 * ATTN_W), BF16), compiler_params=_params("parallel"),
    )(attn, g)


def _attn_norm_bwd(dmix, attn, g, bd, name, tm=512):
    T = attn.shape[0]

    def body(d_ref, a_ref, g_ref, bd_ref, do_ref, dl_ref, dg_ref):
        @pl.when(pl.program_id(0) == 0)
        def _():
            dg_ref[...] = jnp.zeros_like(dg_ref)

        dy, av = d_ref[...], a_ref[...]
        r = lax.rsqrt(jnp.mean(av * av, axis=-1, keepdims=True) + EPS)
        gd = dy * g_ref[...]
        m = jnp.mean(gd * av, axis=-1, keepdims=True)
        da = r * gd - av * (r * r * r) * m
        do_ref[...] = da
        dl_ref[...] = _group_mean(da * av, bd_ref[...]) * float(HEAD_DIM)
        dg_ref[...] += jnp.sum(dy * av * r, axis=0, keepdims=True)

    row = pl.BlockSpec((tm, ATTN_W), lambda i: (i, 0))
    vec = pl.BlockSpec((1, ATTN_W), lambda i: (0, 0))
    return _call(
        body, name=name, grid=(T // tm,),
        in_specs=[row, row, vec, pl.BlockSpec((2 * HEAD_DIM, 2 * HEAD_DIM), lambda i: (0, 0))], out_specs=[row, row, vec],
        out_shape=[jax.ShapeDtypeStruct((T, ATTN_W), F32)] * 2 + [jax.ShapeDtypeStruct((1, ATTN_W), F32)],
        compiler_params=_params("arbitrary"),
    )(dmix, attn, g, bd)


def _rec_gates(xc, wrg_ref, wig_ref, brg_ref, big_ref, lam_ref):
    xb = xc.astype(BF16)
    r = _sigmoid(_dot(xb, wrg_ref[...]) + brg_ref[...])
    ig = _sigmoid(_dot(xb, wig_ref[...]) + big_ref[...])
    sp = _softplus_neg(lam_ref[...])
    log_a = -LRU_C * r * sp
    a = jnp.exp(log_a)
    th = jnp.tanh(log_a)
    mult = jnp.sqrt(-2.0 * th / (1.0 - th))
    return xb, r, ig, sp, a, mult


def _rec_fwd(proj, mix, cw, cb, wrg, wig, brg, big, lam, g, name, tm=256):
    T = proj.shape[0]
    hb = tm // 8

    def body(xr_ref, halo_ref, gr_ref, cw_ref, cb_ref, wrg_ref, wig_ref, brg_ref, big_ref, lam_ref, g_ref, mix_ref,
             xc_ref, h_ref, out_ref, carry):
        i = pl.program_id(0)

        @pl.when(i == 0)
        def _():
            carry[...] = jnp.zeros_like(carry)

        xr = xr_ref[...]
        halo = jnp.where(i > 0, halo_ref[...], 0.0)
        xc = cb_ref[...] + cw_ref[3:4, :] * xr
        for s in range(1, REC_CONV):
            xc = xc + cw_ref[3 - s:4 - s, :] * _shift_down(xr, halo, s)
        xc_ref[...] = xc
        _, _, ig, _, a, mult = _rec_gates(xc, wrg_ref, wig_ref, brg_ref, big_ref, lam_ref)
        pa, hl = _scan_fwd(a, mult * (ig * xc))
        h = hl + pa * carry[0:1, :]
        h_ref[...] = h
        carry[0:1, :] = h_ref[pl.ds(tm - 1, 1), :]
        hg = h * _gelu(gr_ref[...])
        r = lax.rsqrt(jnp.mean(hg * hg, axis=-1, keepdims=True) + EPS)
        out_ref[...] = (hg * r * g_ref[...]).astype(BF16)

    vec = pl.BlockSpec((1, REC_W), lambda i: (0, 0))
    row = pl.BlockSpec((tm, REC_W), lambda i: (i, 0))
    mat = pl.BlockSpec((REC_W, REC_W), lambda i: (0, 0))
    return _call(
        body, name=name, grid=(T // tm,),
        in_specs=[pl.BlockSpec((tm, REC_W), lambda i: (i, 3)),
                  pl.BlockSpec((8, REC_W), lambda i: (jnp.maximum(i * hb - 1, 0), 3)),
                  pl.BlockSpec((tm, REC_W), lambda i: (i, 4)),
                  pl.BlockSpec((8, REC_W), lambda i: (0, 0)), vec, mat, mat, vec, vec, vec, vec, ANY],
        out_specs=[row, row, pl.BlockSpec((tm, REC_W), lambda i: (i, 1))],
        out_shape=[jax.ShapeDtypeStruct((T, REC_W), F32)] * 2 + [jax.ShapeDtypeStruct(mix.shape, BF16)],
        scratch_shapes=[pltpu.VMEM((8, REC_W), F32)], input_output_aliases={11: 2},
        compiler_params=_params("arbitrary"),
    )(proj, proj, proj, cw, cb, wrg, wig, brg, big, lam, g, mix)


def _rec_bwd(dmix, proj, xc, h, cw, cb, wrg, wig, brg, big, lam, g, name, tm=256):
    T = proj.shape[0]
    nt = T // tm
    hb = tm // 8

    def body(d_ref, xr_ref, xhalo_ref, gr_ref, xc_ref, h_ref, hhalo_ref, cw_ref, cb_ref, wrg_ref, wig_ref, brg_ref,
             big_ref, lam_ref, g_ref,
             drec_ref, gcw_ref, gcb_ref, gwrg_ref, gwig_ref, gbrg_ref, gbig_ref, glam_ref, gg_ref,
             g_carry, a_first, dxc_next, gsp):
        i = pl.program_id(0)
        first_tile = i == nt - 1

        @pl.when(i == 0)
        def _():
            for ref in (gcw_ref, gcb_ref, gwrg_ref, gwig_ref, gbrg_ref, gbig_ref, glam_ref, gg_ref,
                        g_carry, a_first, dxc_next, gsp):
                ref[...] = jnp.zeros_like(ref)

        xr, xc, hv = xr_ref[...], xc_ref[...], h_ref[...]
        xhalo = jnp.where(first_tile, 0.0, xhalo_ref[...])
        hhalo = jnp.where(first_tile, 0.0, hhalo_ref[...])
        xb, r, ig, sp, a, mult = _rec_gates(xc, wrg_ref, wig_ref, brg_ref, big_ref, lam_ref)
        h_prev = _shift_down(hv, hhalo, 1)
        ge, dge = _gelu_and_grad(gr_ref[...])
        hg = hv * ge
        rr = lax.rsqrt(jnp.mean(hg * hg, axis=-1, keepdims=True) + EPS)
        dy = d_ref[...]
        gd = dy * g_ref[...]
        dhg = rr * gd - hg * (rr * rr * rr) * jnp.mean(gd * hg, axis=-1, keepdims=True)
        gg_ref[...] += jnp.sum(dy * hg * rr, axis=0, keepdims=True)
        dgr = (dhg * hv * dge).astype(BF16)
        dh = dhg * ge
        b = _shift_up(a, jnp.broadcast_to(a_first[0:1, :], (8, REC_W)), 1)
        pb, gl = _scan_bwd(b, dh)
        gs = gl + pb * g_carry[0:1, :]
        g_carry[0:1, :] = gs[0:1, :]
        a_first[0:1, :] = a[0:1, :]
        da = gs * h_prev
        dmult = gs * (ig * xc)
        di = gs * (mult * xc)
        dxc = gs * (mult * ig)
        dlog_a = da * a - dmult * (a * a) / mult
        gsp[...] += jnp.sum(dlog_a * (-LRU_C * r), axis=0, keepdims=True)
        dzr = (dlog_a * (-LRU_C * sp)) * (r * (1.0 - r))
        dzi = di * (ig * (1.0 - ig))
        dzr_b, dzi_b = dzr.astype(BF16), dzi.astype(BF16)
        dxc = dxc + _dot(dzr_b, wrg_ref[...], _NT) + _dot(dzi_b, wig_ref[...], _NT)
        gwrg_ref[...] += _dot(xb, dzr_b, _TN)
        gwig_ref[...] += _dot(xb, dzi_b, _TN)
        gbrg_ref[...] += jnp.sum(dzr, axis=0, keepdims=True)
        gbig_ref[...] += jnp.sum(dzi, axis=0, keepdims=True)
        nxt = dxc_next[...]
        dxr = cw_ref[3:4, :] * dxc
        gcw_ref[3:4, :] += jnp.sum(dxc * xr, axis=0, keepdims=True)
        for s in range(1, REC_CONV):
            dxr = dxr + cw_ref[3 - s:4 - s, :] * _shift_up(dxc, nxt, s)
            gcw_ref[3 - s:4 - s, :] += jnp.sum(dxc * _shift_down(xr, xhalo, s), axis=0, keepdims=True)
        gcb_ref[...] += jnp.sum(dxc, axis=0, keepdims=True)
        dxc_next[...] = dxc[:8]
        drec_ref[...] = jnp.concatenate([dxr.astype(BF16), dgr], axis=1)

        @pl.when(first_tile)
        def _():
            glam_ref[...] = gsp[...] * (-_sigmoid(-lam_ref[...]))

    rev = lambda i: nt - 1 - i
    vec = pl.BlockSpec((1, REC_W), lambda i: (0, 0))
    row = pl.BlockSpec((tm, REC_W), lambda i: (rev(i), 0))
    mat = pl.BlockSpec((REC_W, REC_W), lambda i: (0, 0))
    cwb = pl.BlockSpec((8, REC_W), lambda i: (0, 0))
    halo = lambda c: pl.BlockSpec((8, REC_W), lambda i, c=c: (jnp.maximum(rev(i) * hb - 1, 0), c))
    return _call(
        body, name=name, grid=(nt,),
        in_specs=[pl.BlockSpec((tm, REC_W), lambda i: (rev(i), 1)),
                  pl.BlockSpec((tm, REC_W), lambda i: (rev(i), 3)), halo(3),
                  pl.BlockSpec((tm, REC_W), lambda i: (rev(i), 4)),
                  row, row, halo(0), cwb, vec, mat, mat, vec, vec, vec, vec],
        out_specs=[pl.BlockSpec((tm, 2 * REC_W), lambda i: (rev(i), 0)), cwb, vec, mat, mat, vec, vec, vec, vec],
        out_shape=[jax.ShapeDtypeStruct((T, 2 * REC_W), BF16)]
        + [jax.ShapeDtypeStruct((8, REC_W), F32), jax.ShapeDtypeStruct((1, REC_W), F32)]
        + [jax.ShapeDtypeStruct((REC_W, REC_W), F32)] * 2 + [jax.ShapeDtypeStruct((1, REC_W), F32)] * 4,
        scratch_shapes=[pltpu.VMEM((8, REC_W), F32)] * 3 + [pltpu.VMEM((1, REC_W), F32)],
        compiler_params=_params("arbitrary"),
    )(dmix, proj, proj, proj, xc, h, h, cw, cb, wrg, wig, brg, big, lam, g)


def _ffn_conv(x_ext, cw_ref, cb_ref):
    return (cb_ref[...] + cw_ref[2:3, :] * x_ext + cw_ref[1:2, :] * pltpu.roll(x_ext, 1, 0)
            + cw_ref[0:1, :] * pltpu.roll(x_ext, 2, 0))


def _up_proj_act(x2, g, w_upT, cw, cb, name, tm=1024, tc=768):
    T = x2.shape[0]
    nc = D_FF // tc

    def body(x_ref, g_ref, wg_ref, wu_ref, cwg_ref, cwu_ref, cbg_ref, cbu_ref, act_ref, da_ref, db_ref, pg_ref, pu_ref,
             h_ref, hist_g, hist_u, hs):
        i, j = pl.program_id(0), pl.program_id(1)

        @pl.when(j == 0)
        def _():
            xv = x_ref[...]
            r = lax.rsqrt(jnp.mean(xv * xv, axis=-1, keepdims=True) + EPS)
            hs[...] = (xv * r * g_ref[...]).astype(BF16)
            h_ref[...] = hs[...]

        hv = hs[...]
        pg, pu = _dot(hv, wg_ref[...], _NT), _dot(hv, wu_ref[...], _NT)
        ge = jnp.concatenate([jnp.where(i > 0, hist_g[j], 0.0), pg], axis=0)
        ue = jnp.concatenate([jnp.where(i > 0, hist_u[j], 0.0), pu], axis=0)
        gel, dgel = _gelu_and_grad(_ffn_conv(ge, cwg_ref, cbg_ref)[8:])
        uu = _ffn_conv(ue, cwu_ref, cbu_ref)[8:]
        act_ref[...] = (gel * uu).astype(BF16)
        da_ref[...] = (uu * dgel).astype(BF16)
        db_ref[...] = gel.astype(BF16)
        pg_ref[...] = pg.astype(BF16)
        pu_ref[...] = pu.astype(BF16)
        hist_g[j] = pg[tm - 8:]
        hist_u[j] = pu[tm - 8:]

    tile = pl.BlockSpec((tm, tc), lambda i, j: (i, j))
    wsp = lambda off: pl.BlockSpec((tc, D_MODEL), lambda i, j, off=off: (j + off, 0))
    cws = lambda off: pl.BlockSpec((8, tc), lambda i, j, off=off: (0, j + off))
    cbs = lambda off: pl.BlockSpec((1, tc), lambda i, j, off=off: (0, j + off))
    return _call(
        body, name=name, grid=(T // tm, nc),
        in_specs=[pl.BlockSpec((tm, D_MODEL), lambda i, j: (i, 0)), pl.BlockSpec((1, D_MODEL), lambda i, j: (0, 0)),
                  wsp(0), wsp(nc), cws(0), cws(nc), cbs(0), cbs(nc)],
        out_specs=[tile] * 5 + [pl.BlockSpec((tm, D_MODEL), lambda i, j: (i, 0))],
        out_shape=[jax.ShapeDtypeStruct((T, D_FF), BF16)] * 5 + [jax.ShapeDtypeStruct((T, D_MODEL), BF16)],
        scratch_shapes=[pltpu.VMEM((nc, 8, tc), F32)] * 2 + [pltpu.VMEM((tm, D_MODEL), BF16)],
        compiler_params=_params("arbitrary", "arbitrary"),
    )(x2, g, w_upT, w_upT, cw, cw, cb, cb)


def _ffn_bwd(dyb, w_down, da, db, pg, pu, cw, name, tm=512, tc=768):
    T, F = pg.shape
    nt = T // tm
    hb16 = tm // 16
    nc = F // tc
    n = tm + 8

    def body(dy_ref, dyn_ref, wd_ref, a_ref, an_ref, b_ref, bn_ref, g_ref, u_ref, cwg_ref, cwu_ref,
             dg_ref, du_ref, gcwg_ref, gcwu_ref, gcbg_ref, gcbu_ref):
        i = pl.program_id(1)
        last = i == nt - 1

        @pl.when(i == 0)
        def _():
            for ref in (gcwg_ref, gcwu_ref, gcbg_ref, gcbu_ref):
                ref[...] = jnp.zeros_like(ref)

        wd = wd_ref[...]
        dact_next = jnp.where(last, 0.0, _dot(dyn_ref[...], wd, _NT)[:8])
        de = jnp.concatenate([_dot(dy_ref[...], wd, _NT), dact_next], axis=0)
        ext = lambda t, nx: jnp.concatenate([t[...].astype(F32), nx[...].astype(F32)[:8]], axis=0)
        for dcv, x_ref, cw_ref, dx_ref, gcw_ref, gcb_ref in ((de * ext(a_ref, an_ref), g_ref, cwg_ref, dg_ref, gcwg_ref, gcbg_ref),
                                                               (de * ext(b_ref, bn_ref), u_ref, cwu_ref, du_ref, gcwu_ref, gcbu_ref)):
            s1, s2 = pltpu.roll(dcv, n - 1, 0), pltpu.roll(dcv, n - 2, 0)
            dx_ref[...] = (cw_ref[2:3, :] * dcv + cw_ref[1:2, :] * s1 + cw_ref[0:1, :] * s2)[:tm].astype(BF16)
            xv = x_ref[...].astype(F32)
            gcw_ref[2:3, :] += jnp.sum(xv * dcv[:tm], axis=0, keepdims=True)
            gcw_ref[1:2, :] += jnp.sum(xv * s1[:tm], axis=0, keepdims=True)
            gcw_ref[0:1, :] += jnp.sum(xv * s2[:tm], axis=0, keepdims=True)
            gcb_ref[...] += jnp.sum(dcv[:tm], axis=0, keepdims=True)

    tile = pl.BlockSpec((tm, tc), lambda j, i: (i, j))
    nxt = pl.BlockSpec((16, tc), lambda j, i: (jnp.minimum((i + 1) * hb16, nt * hb16 - 1), j))
    cws = lambda off: pl.BlockSpec((8, tc), lambda j, i, off=off: (0, j + off))
    cbs = pl.BlockSpec((1, tc), lambda j, i: (0, j))
    return _call(
        body, name=name, grid=(nc, nt),
        in_specs=[pl.BlockSpec((tm, D_MODEL), lambda j, i: (i, 0)),
                  pl.BlockSpec((16, D_MODEL), lambda j, i: (jnp.minimum((i + 1) * hb16, nt * hb16 - 1), 0)),
                  pl.BlockSpec((tc, D_MODEL), lambda j, i: (j, 0)), tile, nxt, tile, nxt, tile, tile, cws(0), cws(nc)],
        out_specs=[tile, tile, cws(0), cws(0), cbs, cbs],
        out_shape=[jax.ShapeDtypeStruct((T, F), BF16)] * 2 + [jax.ShapeDtypeStruct((8, F), F32)] * 2
        + [jax.ShapeDtypeStruct((1, F), F32)] * 2,
        compiler_params=_params("parallel", "arbitrary"),
    )(dyb, dyb, w_down, da, da, db, db, pg, pu, cw, cw)


def _adam_update(w, g, m, v):
    m2 = ADAM_B1 * m + (1.0 - ADAM_B1) * g
    v2 = ADAM_B2 * v + (1.0 - ADAM_B2) * (g * g)
    m_hat = m2 / (1.0 - ADAM_B1 ** ADAM_STEP)
    v_hat = v2 / (1.0 - ADAM_B2 ** ADAM_STEP)
    delta = -ADAM_LR * (m_hat / (jnp.sqrt(v_hat) + ADAM_EPS) + ADAM_WD * w)
    return delta, m2, v2


def _adam_sharded(p, r2, idx, w, m, v, name, transposed=False):
    r, n = p.shape[1:]
    nrecv = r2.shape[0]
    tr = (256 if r % 256 == 0 else r) if transposed else _row_tile(r)

    def body(c_ref, p_ref, r_ref, w_ref, m_ref, v_ref, g_ref, d_ref, m2_ref, v2_ref):
        g = p_ref[...].astype(F32)
        for k in range(nrecv):
            g = g + r_ref[k].astype(F32)
        if transposed:
            g = g.T
        g_ref[...] = g
        d_ref[...], m2_ref[...], v2_ref[...] = _adam_update(w_ref[...], g, m_ref[...], v_ref[...])

    blk = pl.BlockSpec((n, tr), lambda i, c_ref: (0, i)) if transposed else pl.BlockSpec((tr, n), lambda i, c_ref: (i, 0))
    spec = pltpu.PrefetchScalarGridSpec(
        num_scalar_prefetch=1, grid=(r // tr,),
        in_specs=[pl.BlockSpec((None, tr, n), lambda i, c_ref: (c_ref[0], i, 0)),
                  pl.BlockSpec((nrecv, tr, n), lambda i, c_ref: (0, i, 0)), blk, blk, blk],
        out_specs=[blk] * 4)
    return _call(body, name=name, grid_spec=spec, out_shape=[jax.ShapeDtypeStruct(w.shape, F32)] * 4,
                 compiler_params=_params("parallel"))(idx, p, r2, w, m, v)


def _sum_slabs(p, r2, idx, name):
    _, r, n = p.shape

    def body(c_ref, p_ref, r_ref, o_ref):
        acc = p_ref[...]
        for k in range(N_PEERS):
            acc = acc + r_ref[k]
        o_ref[...] = acc

    spec = pltpu.PrefetchScalarGridSpec(
        num_scalar_prefetch=1, grid=(1,),
        in_specs=[pl.BlockSpec((None, r, n), lambda i, c_ref: (c_ref[0], 0, 0)),
                  pl.BlockSpec((N_PEERS, r, n), lambda i, c_ref: (0, 0, 0))],
        out_specs=pl.BlockSpec((r, n), lambda i, c_ref: (0, 0)))
    return _call(body, name=name, grid_spec=spec, out_shape=jax.ShapeDtypeStruct((r, n), F32))(idx, p, r2)


def _adam_small(ws, gs, ms, vs, name):
    n = len(ws)

    def body(*refs):
        for i in range(n):
            d, m2, v2 = _adam_update(refs[i][...], refs[n + i][...], refs[2 * n + i][...], refs[3 * n + i][...])
            refs[4 * n + i][...] = d
            refs[5 * n + i][...] = m2
            refs[6 * n + i][...] = v2

    outs = _call(body, name=name, out_shape=[jax.ShapeDtypeStruct(w.shape, F32) for w in ws] * 3)(*ws, *gs, *ms, *vs)
    return outs[:n], outs[n:2 * n], outs[2 * n:]


def _pack_small_grads(full, halves, rcw, fcwg, fcwu, wrg, wig, lparts, name):
    nf, nh = len(full), len(halves)

    def body(*refs):
        o = refs[-1]
        o[...] = jnp.zeros_like(o)
        row = 0
        for r in refs[:nf]:
            for j in range(r.shape[1] // 1024):
                o[row:row + 1, :] = r[:, 1024 * j:1024 * (j + 1)]
                row += 1
        for k in range(0, nh, 2):
            o[row:row + 1, 0:512] = refs[nf + k][...]
            o[row:row + 1, 512:1024] = refs[nf + k + 1][...]
            row += 1
        rcw_ref, fg_ref, fu_ref, wrg_ref, wig_ref, l_ref = refs[nf + nh:nf + nh + 6]
        for k in range(2):
            o[row:row + 1, 0:512] = rcw_ref[2 * k:2 * k + 1, :]
            o[row:row + 1, 512:1024] = rcw_ref[2 * k + 1:2 * k + 2, :]
            row += 1
        for f_ref in (fg_ref, fu_ref):
            for k in range(FFN_CONV):
                for j in range(D_FF // 1024):
                    o[row:row + 1, :] = f_ref[k:k + 1, 1024 * j:1024 * (j + 1)]
                    row += 1
        assert row == 32
        for n in range(8):
            o[32:96, 64 * n:64 * n + 64] = wrg_ref[64 * n:64 * n + 64, 64 * n:64 * n + 64]
            o[32:96, 512 + 64 * n:512 + 64 * n + 64] = wig_ref[64 * n:64 * n + 64, 64 * n:64 * n + 64]
        o[96:97, :] = jnp.sum(l_ref[...], axis=0, keepdims=True)

    return _call(body, name=name, out_shape=jax.ShapeDtypeStruct((SMALL_ROWS, 1024), F32))(
        *full, *halves, rcw, fcwg, fcwu, wrg, wig, lparts)


def _block_diag(w):
    eye = jnp.eye(8, dtype=w.dtype)
    return (w[:, :, None, :] * eye[:, None, :, None]).reshape(512, 512)


def kernel(x, positions, g_mix, w_in, q_norm_g, k_norm_g, rec_conv_w, rec_conv_b, w_rg, b_rg, w_ig, b_ig, lru_lambda, g_attn_out, g_rec_out, w_out, g_ffn, w_up, ffn_conv_w, ffn_conv_b, w_down, loss_target, m_g_mix, m_w_in, m_q_norm_g, m_k_norm_g, m_rec_conv_w, m_rec_conv_b, m_w_rg, m_b_rg, m_w_ig, m_b_ig, m_lru_lambda, m_g_attn_out, m_g_rec_out, m_w_out, m_g_ffn, m_w_up, m_ffn_conv_w, m_ffn_conv_b, m_w_down, v_g_mix, v_w_in, v_q_norm_g, v_k_norm_g, v_rec_conv_w, v_rec_conv_b, v_w_rg, v_b_rg, v_w_ig, v_b_ig, v_lru_lambda, v_g_attn_out, v_g_rec_out, v_w_out, v_g_ffn, v_w_up, v_ffn_conv_w, v_ffn_conv_b, v_w_down):
    T = x.shape[1]
    ix, iy, ic = lax.axis_index("x"), lax.axis_index("y"), lax.axis_index("c")
    dev = 4 * ix + 2 * iy + ic
    xs = x.reshape(T, D_MODEL)
    tgt = loss_target.reshape(T, D_MODEL)
    pos = positions.reshape(T, 1)

    shards = {"w_in": (w_in[0], m_w_in[0], v_w_in[0]), "w_out": (w_out[0], m_w_out[0], v_w_out[0]),
              "w_up": (w_up[0], m_w_up[0], v_w_up[0]), "w_down": (w_down[0], m_w_down[0], v_w_down[0])}
    taps = jnp.concatenate([rec_conv_w.reshape(-1), ffn_conv_w.reshape(-1), jnp.zeros((4096 - 2560,), F32)]).reshape(8, 512)
    W_inT, taps_all = _all_gather([w_in[0].T.astype(BF16), taps], "ag_w_in")
    late = [w_out[0].astype(BF16), w_up[0].T.astype(BF16), w_down[0].astype(BF16)]
    ag_send, ag_recv, late_thru, land_thru, ag_token = _exchange_start(
        late, [_landing((N_DEV * s.shape[0], 1024), BF16, s, dev * s.shape[0]) for s in late], "gather", taps_all,
        "ag_late_start")
    taps_all = taps_all.reshape(N_DEV, 4096)
    rcw = taps_all[:, :256].reshape(8, 4, 64).transpose(1, 0, 2).reshape(4, REC_W)
    fcw = taps_all[:, 256:2560].reshape(8, 3, 768).transpose(1, 0, 2).reshape(3, 2 * D_FF)
    rcw8 = jnp.pad(rcw, ((0, 4), (0, 0)))
    fcw8 = jnp.pad(fcw, ((0, 5), (0, 0)))
    fcb = ffn_conv_b.reshape(1, 2 * D_FF)

    half = HEAD_DIM // 2
    inv_freq = ROPE_THETA ** (-jnp.arange(half, dtype=F32) / half)
    invf = jnp.tile(inv_freq, 2 * N_HEADS).reshape(1, ATTN_W)
    bd = jnp.asarray(np.kron(np.eye(2), np.full((HEAD_DIM, HEAD_DIM), 1.0 / HEAD_DIM)), BF16)
    qg = jnp.tile(q_norm_g.reshape(HEAD_DIM), N_HEADS).reshape(1, ATTN_W)
    kg = jnp.tile(k_norm_g.reshape(HEAD_DIM), N_HEADS).reshape(1, ATTN_W)
    wrg_bd = _block_diag(w_rg[0]).astype(BF16)
    wig_bd = _block_diag(w_ig[0]).astype(BF16)
    brg, big = b_rg.reshape(1, REC_W), b_ig.reshape(1, REC_W)

    proj, h1 = _norm_proj(xs, g_mix + ag_token[0, 0], W_inT, "in_proj")
    qf, kf = _qk_prep(proj, pos, invf, qg, kg, bd, "qk_prep")
    attn, lse = _attn_fwd(qf, kf, proj, "attn_fwd")
    mix = _attn_norm(attn, g_attn_out, "attn_norm")
    xc, hstate, mix = _rec_fwd(proj, mix, rcw8, rec_conv_b, wrg_bd, wig_bd, brg, big, lru_lambda, g_rec_out, "rec_fwd")
    _, (W_out, W_upT, W_down) = _exchange_wait(ag_send, ag_recv, late_thru, land_thru, "gather", hstate, "ag_late_wait")
    x2 = _mm(mix, W_out, "nn", F32, "out_proj", add=xs)

    act, da, db, pg, pu, h2 = _up_proj_act(x2, g_ffn, W_upT, fcw8, fcb, "up_proj_act")
    dy, dyb, lparts = _mm(act, W_down, "nn", F32, "down_proj_loss", add=x2, loss_target=tgt, tm=512, tk=D_FF)

    g_down = _mm(act, dyb, "tn", BF16, "g_w_down", tk=2048)
    dpg, dpu, g_fcwg, g_fcwu, g_fcbg, g_fcbu = _ffn_bwd(dyb, W_down, da, db, pg, pu, fcw8, "ffn_bwd")
    g_upT = _mm(dpg, h2, "tn", BF16, "g_w_up_gate", tk=2048, o_rows=2 * D_FF)
    g_upT = _mm(dpu, h2, "tn", BF16, "g_w_up_up", tk=2048, into=g_upT, o_moff=D_FF // 1024)
    ffn_g = [g_upT.reshape(N_DEV, 2 * D_FF // N_DEV, 1024), g_down.reshape(N_DEV, D_FF // N_DEV, 1024)]
    rs_send, rs_recv, ffn_g, ffn_land, rs_token = _exchange_start(
        ffn_g, [_landing((N_PEERS,) + g.shape[1:], BF16) for g in ffn_g], "scatter", dpu, "rs_ffn_start")
    dx2, dx2b, g_gffn = _mm_norm_bwd([dpg, dpu], W_upT, x2, dy, g_ffn + rs_token[0, 0], "d_h2_norm_bwd", tm=1024, tk=1024)

    dmix = _mm(dx2b, W_out, "nt", F32, "d_mix")
    g_out = _mm(mix, dx2b, "tn", BF16, "g_w_out", tk=2048).reshape(N_DEV, D_MODEL // N_DEV, 1024)
    out_send, out_recv, (g_out,), out_land, out_token = _exchange_start(
        [g_out], [_landing((N_PEERS,) + g_out.shape[1:], BF16)], "scatter", dmix, "rs_out_start")
    do, delta, g_gattn = _attn_norm_bwd(dmix, attn, g_attn_out + out_token[0, 0], bd, "attn_norm_bwd")
    dqh, dkh, dv = _attn_bwd(qf, kf, proj, do, lse, delta, "attn_bwd")
    dqkv, g_qg, g_kg = _qk_prep_bwd(proj, dqh, dkh, dv, pos, invf, qg, kg, bd, "qk_prep_bwd")
    (drec, g_rcw, g_rcb, g_wrg, g_wig, g_brg, g_big, g_lam, g_grec) = _rec_bwd(
        dmix, proj, xc, hstate, rcw8, rec_conv_b, wrg_bd, wig_bd, brg, big, lru_lambda, g_rec_out, "rec_bwd")
    g_inT = _mm(dqkv, h1, "tn", BF16, "g_w_in_qkv", tm=512, o_rows=IN_W)
    g_inT = _mm(drec, h1, "tn", BF16, "g_w_in_rec", tm=512, into=g_inT, o_moff=3 * ATTN_W // 512)
    g_inT = g_inT.reshape(N_DEV, IN_W // N_DEV, 1024)
    in_send, in_recv, (g_inT,), in_land, in_token = _exchange_start(
        [g_inT], [_landing((N_PEERS,) + g_inT.shape[1:], BF16)], "scatter", drec, "rs_in_start")
    grad_x, _, g_gmix = _mm_norm_bwd([dqkv, drec], W_inT, xs, dx2, g_mix + in_token[0, 0], "d_h1_norm_bwd", tm=1024, tk=512)

    flat = _pack_small_grads([g_gmix, g_gffn, g_fcbg, g_fcbu], [g_rcb, g_brg, g_big, g_lam, g_gattn, g_grec, g_qg, g_kg],
                             g_rcw, g_fcwg, g_fcwu, g_wrg, g_wig, lparts.reshape(-1, D_MODEL), "pack_small_grads")
    srows = SMALL_ROWS // N_DEV
    flat = flat.reshape(N_DEV, srows, 1024)
    sm_send, sm_recv, (flat,), sm_land, sm_token = _exchange_start(
        [flat], [_landing((N_PEERS, srows, 1024), F32)], "scatter", grad_x, "ar_small_rs_start")

    devi = jnp.reshape(dev, (1,)).astype(jnp.int32)
    ffn_g, ffn_land = _exchange_wait(rs_send, rs_recv, ffn_g, ffn_land, "scatter", sm_token, "rs_ffn_wait")
    (g_out,), out_land = _exchange_wait(out_send, out_recv, [g_out], out_land, "scatter", sm_token, "rs_out_wait")
    big_out = {"grad": {}, "delta": {}, "new_m": {}, "new_v": {}}

    def adam_big(nm, p, r):
        w_, m_, v_ = shards[nm]
        res = _adam_sharded(p, r, devi, w_, m_, v_, "adam_" + nm, transposed=nm in ("w_in", "w_up"))
        for kind, a in zip(("grad", "delta", "new_m", "new_v"), res):
            big_out[kind][nm] = a[None]
        return res[0]

    last = adam_big("w_up", ffn_g[0], ffn_land[0])
    (flat,), sm_land = _exchange_wait(sm_send, sm_recv, [flat], sm_land, "scatter", last, "ar_small_rs_wait")
    mine = _sum_slabs(flat, sm_land[0], devi, "sum_small_grads")
    sm_send, sm_recv, (mine,), sm_land, sm_token = _exchange_start(
        [mine], [_landing((SMALL_ROWS, 1024), F32, mine, dev * srows)], "gather", last, "ar_small_ag_start")
    adam_big("w_down", ffn_g[1], ffn_land[1])
    last = adam_big("w_out", g_out, out_land[0])
    _, (tot,) = _exchange_wait(sm_send, sm_recv, [mine], sm_land, "gather", last, "ar_small_ag_wait")
    (g_inT,), in_land = _exchange_wait(in_send, in_recv, [g_inT], in_land, "scatter", tot, "rs_in_wait")
    adam_big("w_in", g_inT, in_land[0])

    half = lambda r, h, shape: tot[r, 512 * h:512 * h + 512].reshape(shape)
    blocks = lambda h: tot[32:96, 512 * h:512 * h + 512].reshape(64, 8, 64).transpose(1, 0, 2)[None]
    fcw_full = jnp.concatenate([tot[14:23].reshape(1, 3, D_FF), tot[23:32].reshape(1, 3, D_FF)], axis=2)
    g_small = {
        "g_mix": tot[0:1], "g_ffn": tot[1:2], "ffn_conv_b": tot[2:8].reshape(1, 2 * D_FF),
        "rec_conv_b": half(8, 0, (1, 512)), "b_rg": half(8, 1, (1, 8, 64)), "b_ig": half(9, 0, (1, 8, 64)),
        "lru_lambda": half(9, 1, (1, 512)), "g_attn_out": half(10, 0, (1, 512)), "g_rec_out": half(10, 1, (1, 512)),
        "q_norm_g": half(11, 0, (N_HEADS, HEAD_DIM)).sum(0)[None], "k_norm_g": half(11, 1, (N_HEADS, HEAD_DIM)).sum(0)[None],
        "w_rg": blocks(0), "w_ig": blocks(1),
        "rec_conv_w": lax.dynamic_slice(tot[12:14].reshape(1, 4, REC_W), (0, 0, 64 * dev), (1, 4, 64)),
        "ffn_conv_w": lax.dynamic_slice(fcw_full, (0, 0, 768 * dev), (1, 3, 768))}
    loss = 0.5 / D_MODEL * jnp.sum(tot[96])
    given = dict(rec_conv_w=rec_conv_w, ffn_conv_w=ffn_conv_w,g_mix=g_mix, q_norm_g=q_norm_g, k_norm_g=k_norm_g, rec_conv_b=rec_conv_b, w_rg=w_rg, b_rg=b_rg, w_ig=w_ig,
                 b_ig=b_ig, lru_lambda=lru_lambda, g_attn_out=g_attn_out, g_rec_out=g_rec_out, g_ffn=g_ffn, ffn_conv_b=ffn_conv_b)
    given_m = dict(rec_conv_w=m_rec_conv_w, ffn_conv_w=m_ffn_conv_w, g_mix=m_g_mix, q_norm_g=m_q_norm_g, k_norm_g=m_k_norm_g, rec_conv_b=m_rec_conv_b, w_rg=m_w_rg, b_rg=m_b_rg,
                   w_ig=m_w_ig, b_ig=m_b_ig, lru_lambda=m_lru_lambda, g_attn_out=m_g_attn_out, g_rec_out=m_g_rec_out,
                   g_ffn=m_g_ffn, ffn_conv_b=m_ffn_conv_b)
    given_v = dict(rec_conv_w=v_rec_conv_w, ffn_conv_w=v_ffn_conv_w, g_mix=v_g_mix, q_norm_g=v_q_norm_g, k_norm_g=v_k_norm_g, rec_conv_b=v_rec_conv_b, w_rg=v_w_rg, b_rg=v_b_rg,
                   w_ig=v_w_ig, b_ig=v_b_ig, lru_lambda=v_lru_lambda, g_attn_out=v_g_attn_out, g_rec_out=v_g_rec_out,
                   g_ffn=v_g_ffn, ffn_conv_b=v_ffn_conv_b)
    small = sorted(given)
    ds, m2s, v2s = _adam_small([given[k] for k in small], [g_small[k] for k in small], [given_m[k] for k in small],
                               [given_v[k] for k in small], "adam_small")
    small_out = {"grad": g_small, "delta": dict(zip(small, ds)), "new_m": dict(zip(small, m2s)), "new_v": dict(zip(small, v2s))}

    order = ("g_mix", "w_in", "q_norm_g", "k_norm_g", "rec_conv_w", "rec_conv_b", "w_rg", "b_rg", "w_ig", "b_ig",
             "lru_lambda", "g_attn_out", "g_rec_out", "w_out", "g_ffn", "w_up", "ffn_conv_w", "ffn_conv_b", "w_down")
    outs = [loss, grad_x.reshape(1, T, D_MODEL)]
    for kind in ("grad", "delta", "new_m", "new_v"):
        for name in order:
            outs.append(big_out[kind][name] if name in big_out[kind] else small_out[kind][name])
    return tuple(outs)
```

```python
import math

import numpy as np
import jax
import jax.numpy as jnp
from jax import lax
from jax.experimental import pallas as pl
from jax.experimental.pallas import tpu as pltpu

F32 = jnp.float32
BF16 = jnp.bfloat16

D_MODEL = 1024
HEAD_DIM = 64
ATTN_W = 512
REC_W = 512
N_HEADS = 8
D_FF = 3072
IN_W = 2560
REC_CONV = 4
FFN_CONV = 3
LRU_C = 8.0
ROPE_THETA = 10000.0
EPS = 1e-6
NEG_INF = -1e30
QBLK = 128
DILATIONS = (1, 4, 16)
N_DEV = 8
SMALL_ROWS = 128
ADAM_LR, ADAM_B1, ADAM_B2, ADAM_EPS, ADAM_WD, ADAM_STEP = 0.001, 0.9, 0.999, 1e-08, 0.01, 10
MESH = pl.DeviceIdType.MESH
ANY = pl.BlockSpec(memory_space=pl.ANY)


def _call(body, *, name, **kw):
    return pl.pallas_call(body, name=name, **kw)


def _params(*sem):
    return pltpu.CompilerParams(dimension_semantics=sem, vmem_limit_bytes=56 * 1024 * 1024)


def _gelu(x):
    c = math.sqrt(2.0 / math.pi)
    return 0.5 * x * (1.0 + jnp.tanh(c * (x + 0.044715 * (x * x * x))))


def _gelu_and_grad(x):
    c = math.sqrt(2.0 / math.pi)
    t = jnp.tanh(c * (x + 0.044715 * (x * x * x)))
    g = 0.5 * x * (1.0 + t)
    dg = 0.5 * (1.0 + t) + 0.5 * x * (1.0 - t * t) * (c * (1.0 + 3.0 * 0.044715 * (x * x)))
    return g, dg


def _sigmoid(x):
    return 1.0 / (1.0 + jnp.exp(-x))


def _softplus_neg(lam):
    y = jnp.exp(-jnp.abs(lam))
    u = 1.0 + y
    log1p = jnp.where(u == 1.0, y, jnp.log(u) * y / jnp.where(u == 1.0, 1.0, u - 1.0))
    return jnp.maximum(-lam, 0.0) + log1p


_NN = (((1,), (0,)), ((), ()))
_NT = (((1,), (1,)), ((), ()))
_TN = (((0,), (0,)), ((), ()))


def _dot(a, b, dims=_NN):
    return lax.dot_general(a, b, dims, preferred_element_type=F32)


def _group_mean(v, bd):
    hi = v.astype(BF16)
    lo = (v - hi.astype(F32)).astype(BF16)
    w = bd.shape[0]
    return jnp.concatenate([_dot(hi[:, c:c + w], bd) + _dot(lo[:, c:c + w], bd) for c in range(0, v.shape[1], w)], axis=1)


def _rope_tables(pos_ref, invf_ref):
    ang = pos_ref[...].astype(F32) * invf_ref[:, :2 * HEAD_DIM]
    reps = invf_ref.shape[1] // (2 * HEAD_DIM)
    return jnp.tile(jnp.cos(ang), (1, reps)), jnp.tile(jnp.sin(ang), (1, reps))


def _shift_down(x, halo, s):
    rolled = pltpu.roll(x, s, 0)
    hr = pltpu.roll(halo, s, 0)
    row = lax.broadcasted_iota(jnp.int32, hr.shape, 0)
    first = jnp.where(row < s, hr, rolled[:8])
    return jnp.concatenate([first, rolled[8:]], axis=0)


def _shift_up(x, halo, s):
    n = x.shape[0]
    rolled = pltpu.roll(x, n - s, 0)
    hr = pltpu.roll(halo, 8 - s, 0)
    row = lax.broadcasted_iota(jnp.int32, hr.shape, 0)
    last = jnp.where(row >= 8 - s, hr, rolled[n - 8:])
    return jnp.concatenate([rolled[:n - 8], last], axis=0)


def _scan_fwd(a, u):
    n, w = a.shape
    a3, u3 = a.reshape(n // 8, 8, w), u.reshape(n // 8, 8, w)
    row = lax.broadcasted_iota(jnp.int32, a3.shape, 1)
    for s in (1, 2, 4):
        a_s = jnp.where(row < s, 1.0, pltpu.roll(a3, s, 1))
        u_s = jnp.where(row < s, 0.0, pltpu.roll(u3, s, 1))
        u3 = u3 + a3 * u_s
        a3 = a3 * a_s
    ps, hs = [a3[0]], [u3[0]]
    for k in range(1, n // 8):
        ps.append(a3[k] * ps[-1][7:8, :])
        hs.append(u3[k] + a3[k] * hs[-1][7:8, :])
    return jnp.concatenate(ps, axis=0), jnp.concatenate(hs, axis=0)


def _scan_bwd(b, v):
    n, w = b.shape
    b3, v3 = b.reshape(n // 8, 8, w), v.reshape(n // 8, 8, w)
    row = lax.broadcasted_iota(jnp.int32, b3.shape, 1)
    for s in (1, 2, 4):
        b_s = jnp.where(row >= 8 - s, 1.0, pltpu.roll(b3, 8 - s, 1))
        v_s = jnp.where(row >= 8 - s, 0.0, pltpu.roll(v3, 8 - s, 1))
        v3 = v3 + b3 * v_s
        b3 = b3 * b_s
    last = n // 8 - 1
    ps, gs = [b3[last]], [v3[last]]
    for k in range(last - 1, -1, -1):
        ps.append(b3[k] * ps[-1][0:1, :])
        gs.append(v3[k] + b3[k] * gs[-1][0:1, :])
    return jnp.concatenate(ps[::-1], axis=0), jnp.concatenate(gs[::-1], axis=0)


def _rot_half(y):
    n = y.shape[1]
    lane = lax.broadcasted_iota(jnp.int32, y.shape, 1) & (HEAD_DIM - 1)
    return jnp.where(lane < HEAD_DIM // 2, -pltpu.roll(y, n - HEAD_DIM // 2, 1), pltpu.roll(y, HEAD_DIM // 2, 1))


def _row_tile(r, cap=256):
    return max(t for t in range(16, cap + 1, 16) if r % t == 0)


def _all_gather(shards, name):
    na = len(shards)
    ms = [s.shape[0] for s in shards]

    def body(*refs):
        x_refs, out_refs = refs[:na], refs[na:2 * na]
        send_sems, recv_sems, local_sems = refs[2 * na:]
        x, y, c = lax.axis_index("x"), lax.axis_index("y"), lax.axis_index("c")
        me, sibling = (x, y, c), (x, y, 1 - c)
        chips = [(1 - x, y), (x, 1 - y), (1 - x, 1 - y)]

        def rows(a, px, py, pc):
            return out_refs[a].at[pl.ds((4 * px + 2 * py + pc) * ms[a], ms[a]), :]

        def copy(a, k, block, to, src=None):
            return pltpu.make_async_remote_copy(
                src_ref=rows(a, *block) if src is None else src, dst_ref=rows(a, *block),
                send_sem=send_sems.at[7 * a + k], recv_sem=recv_sems.at[7 * a + k], device_id=to, device_id_type=MESH)

        mine = [pltpu.make_async_copy(x_refs[a], rows(a, *me), local_sems.at[a]) for a in range(na)]
        first = []
        for a in range(na):
            mine[a].start()
            first.append(copy(a, 0, me, sibling, src=x_refs[a]))
            first += [copy(a, 1 + j, me, (*chip, c), src=x_refs[a]) for j, chip in enumerate(chips)]
        for cp in first:
            cp.start()
        passed = []
        for a in range(na):
            for j, chip in enumerate(chips):
                copy(a, 1 + j, (*chip, c), me).wait_recv()
                fw = copy(a, 4 + j, (*chip, c), sibling)
                fw.start()
                passed.append(fw)
        for a in range(na):
            copy(a, 0, sibling, me).wait_recv()
            for j, chip in enumerate(chips):
                copy(a, 4 + j, (*chip, 1 - c), me).wait_recv()
        for cp in first + passed:
            cp.wait_send()
        for cp in mine:
            cp.wait()

    return _call(
        body, name=name, out_shape=[jax.ShapeDtypeStruct((N_DEV * s.shape[0], s.shape[1]), s.dtype) for s in shards],
        in_specs=[ANY] * na, out_specs=[ANY] * na,
        scratch_shapes=[pltpu.SemaphoreType.DMA((7 * na,)), pltpu.SemaphoreType.DMA((7 * na,)),
                        pltpu.SemaphoreType.DMA((na,))],
    )(*shards)


HBM = pl.BlockSpec(memory_space=pltpu.HBM)
SEM = pl.BlockSpec(memory_space=pltpu.SEMAPHORE)
EFFECT = pltpu.SideEffectType.DATAFLOW_SIDE_EFFECTING
N_PEERS = N_DEV - 1


def _peer(k):
    x, y, c = lax.axis_index("x"), lax.axis_index("y"), lax.axis_index("c")
    b = k + 1
    flip = lambda v, bit: 1 - v if bit else v
    return flip(x, b & 4), flip(y, b & 2), flip(c, b & 1)


def _in_hbm(a):
    return pltpu.with_memory_space_constraint(a, pltpu.HBM)


def _split_copy_descr(na, kind, src_refs, land_refs, send_sems, recv_sems):
    x, y, c = lax.axis_index("x"), lax.axis_index("y"), lax.axis_index("c")
    me = 4 * x + 2 * y + c
    copies = []
    for a in range(na):
        for k in range(N_PEERS):
            px, py, pc = _peer(k)
            if kind == "gather":
                m = src_refs[a].shape[0]
                src, dst = src_refs[a], land_refs[a].at[pl.ds(me * m, m), :]
            else:
                src, dst = src_refs[a].at[4 * px + 2 * py + pc], land_refs[a].at[k]
            copies.append(pltpu.make_async_remote_copy(
                src_ref=src, dst_ref=dst, send_sem=send_sems.at[N_PEERS * a + k], recv_sem=recv_sems.at[N_PEERS * a + k],
                device_id=(px, py, pc), device_id_type=MESH))
    return copies


def _landing(shape, dtype, own=None, at=None):
    buf = lax.empty(shape, dtype)
    return buf if own is None else lax.dynamic_update_slice(buf, own, (at, 0))


def _exchange_start(srcs, lands, kind, after, name):
    na = len(srcs)
    land_shapes = [l.shape for l in lands]

    def body(*refs):
        src_refs, land_refs = refs[:na], refs[na:2 * na]
        send_sems, recv_sems = refs[2 * na + 1], refs[2 * na + 2]
        token = refs[-1]
        for cp in _split_copy_descr(na, kind, src_refs, land_refs, send_sems, recv_sems):
            cp.start()
        token[...] = jnp.zeros_like(token)

    lands = [_in_hbm(l) for l in lands]
    sem = pltpu.SemaphoreType.DMA((N_PEERS * na,))
    outs = _call(
        body, name=name,
        out_shape=[sem, sem] + [pltpu.HBM(s.shape, s.dtype) for s in srcs] + [pltpu.HBM(s, srcs[0].dtype) for s in land_shapes]
        + [jax.ShapeDtypeStruct((8, 128), F32)],
        in_specs=[HBM] * (2 * na) + [ANY], out_specs=[SEM, SEM] + [HBM] * (2 * na) + [pl.BlockSpec(memory_space=pltpu.VMEM)],
        input_output_aliases={i: 2 + i for i in range(2 * na)},
        compiler_params=pltpu.CompilerParams(has_side_effects=EFFECT),
    )(*[_in_hbm(s) for s in srcs], *lands, after)
    return outs[0], outs[1], outs[2:2 + na], outs[2 + na:2 + 2 * na], outs[-1]


def _exchange_wait(send_sems, recv_sems, srcs, lands, kind, after, name):
    na = len(srcs)

    def body(*refs):
        src_refs, land_refs = refs[:na], refs[na:2 * na]
        s_sems, r_sems = refs[2 * na], refs[2 * na + 1]
        for cp in _split_copy_descr(na, kind, src_refs, land_refs, s_sems, r_sems):
            cp.wait_send()
            cp.wait_recv()

    outs = _call(
        body, name=name, out_shape=[pltpu.HBM(s.shape, s.dtype) for s in srcs] + [pltpu.HBM(l.shape, l.dtype) for l in lands],
        in_specs=[HBM] * (2 * na) + [SEM, SEM, ANY], out_specs=[HBM] * (2 * na),
        input_output_aliases={i: i for i in range(2 * na)},
        compiler_params=pltpu.CompilerParams(has_side_effects=EFFECT),
    )(*srcs, *lands, send_sems, recv_sems, after)
    return outs[:na], outs[na:]


def _mm(a, b, mode, out_dtype, name, add=None, tm=1024, tn=1024, tk=1024, b_noff=0, b_koff=0,
        n=None, k=None, into=None, o_rows=None, o_moff=0, loss_target=None):
    if mode == "tn":
        K, M = a.shape
    else:
        M, K = a.shape
    N = n if n is not None else (b.shape[0] if mode == "nt" else b.shape[1])
    if k is not None:
        assert k == K
    tm, tn, tk = min(tm, M), min(tn, N), min(tk, K)
    assert M % tm == 0 and N % tn == 0 and K % tk == 0, (name, M, N, K)
    nk = K // tk
    if mode == "nn":
        a_spec = pl.BlockSpec((tm, tk), lambda i, j, kk: (i, kk))
        b_spec, dims = pl.BlockSpec((tk, tn), lambda i, j, kk: (kk + b_koff, j + b_noff)), _NN
    elif mode == "nt":
        a_spec = pl.BlockSpec((tm, tk), lambda i, j, kk: (i, kk))
        b_spec, dims = pl.BlockSpec((tn, tk), lambda i, j, kk: (j + b_noff, kk + b_koff)), _NT
    else:
        a_spec = pl.BlockSpec((tk, tm), lambda i, j, kk: (kk, i))
        b_spec, dims = pl.BlockSpec((tk, tn), lambda i, j, kk: (kk + b_koff, j + b_noff)), _TN
    o_spec = pl.BlockSpec((tm, tn), lambda i, j, kk: (i + o_moff, j))
    has_add, has_into, has_loss = add is not None, into is not None, loss_target is not None
    assert not has_loss or (has_add and tn == N and not has_into)
    n_in = 2 + has_add + has_loss + has_into

    def body(*refs):
        a_ref, b_ref = refs[0], refs[1]
        add_ref = refs[2] if has_add else None
        outs = refs[n_in:]

        def finish(r):
            if has_add:
                r = r + add_ref[...]
            if has_loss:
                e = r - refs[3][...]
                dy = e * (1.0 / N)
                outs[0][...] = dy
                outs[1][...] = dy.astype(BF16)
                outs[2][...] = jnp.sum(e * e, axis=0, keepdims=True)[None]
            else:
                outs[0][...] = r.astype(out_dtype)

        if nk == 1:
            finish(_dot(a_ref[...], b_ref[...], dims))
        else:
            acc = refs[-1]
            kk = pl.program_id(2)

            @pl.when(kk == 0)
            def _():
                acc[...] = _dot(a_ref[...], b_ref[...], dims)

            @pl.when((kk > 0) & (kk < nk - 1))
            def _():
                acc[...] += _dot(a_ref[...], b_ref[...], dims)

            @pl.when(kk == nk - 1)
            def _():
                finish(acc[...] + _dot(a_ref[...], b_ref[...], dims))

    tile = pl.BlockSpec((tm, tn), lambda i, j, kk: (i, j))
    ins = [a, b] + ([add] if has_add else []) + ([loss_target] if has_loss else []) + ([into] if has_into else [])
    specs = [a_spec, b_spec] + [tile] * (has_add + has_loss) + ([ANY] if has_into else [])
    rows = into.shape[0] if has_into else (o_rows if o_rows is not None else M)
    if has_loss:
        out_specs = [tile, tile, pl.BlockSpec((1, 1, N), lambda i, j, kk: (i, 0, 0))]
        out_shape = [jax.ShapeDtypeStruct((M, N), F32), jax.ShapeDtypeStruct((M, N), BF16), jax.ShapeDtypeStruct((M // tm, 1, N), F32)]
    else:
        out_specs, out_shape = o_spec, jax.ShapeDtypeStruct((rows, N), out_dtype)
    return _call(
        body, name=name, grid=(M // tm, N // tn, nk), in_specs=specs, out_specs=out_specs, out_shape=out_shape,
        scratch_shapes=[pltpu.VMEM((tm, tn), F32)] if nk > 1 else [],
        input_output_aliases={len(ins) - 1: 0} if has_into else {},
        compiler_params=_params("parallel", "parallel", "arbitrary"),
    )(*ins)


def _mm_norm_bwd(parts, b, x, resid, g, name, tm=512, tk=512):
    T, N = x.shape
    counts = [p.shape[1] // tk for p in parts]
    starts = [sum(counts[:i]) for i in range(len(parts))]
    nsteps = sum(counts)
    assert all(p.shape[1] % tk == 0 for p in parts) and b.shape == (nsteps * tk, N)
    npart = len(parts)

    def body(*refs):
        a_refs, b_ref, x_ref, res_ref, g_ref = refs[:npart], refs[npart], refs[npart + 1], refs[npart + 2], refs[npart + 3]
        dx_ref, dxb_ref, dg_ref, acc = refs[npart + 4:]
        i, s = pl.program_id(0), pl.program_id(1)

        @pl.when((i == 0) & (s == 0))
        def _():
            dg_ref[...] = jnp.zeros_like(dg_ref)

        for p in range(npart):
            @pl.when((s >= starts[p]) & (s < starts[p] + counts[p]))
            def _(p=p):
                d = _dot(a_refs[p][...], b_ref[...])

                @pl.when(s == 0)
                def _():
                    acc[...] = d

                @pl.when(s > 0)
                def _():
                    acc[...] += d

        @pl.when(s == nsteps - 1)
        def _():
            xv, dhv = x_ref[...], acc[...]
            r = lax.rsqrt(jnp.mean(xv * xv, axis=-1, keepdims=True) + EPS)
            gd = dhv * g_ref[...]
            m = jnp.mean(gd * xv, axis=-1, keepdims=True)
            dx = res_ref[...] + r * gd - xv * (r * r * r) * m
            dx_ref[...] = dx
            dxb_ref[...] = dx.astype(BF16)
            dg_ref[...] += jnp.sum(dhv * xv * r, axis=0, keepdims=True)

    a_specs = [pl.BlockSpec((tm, tk), lambda i, s, st=st, c=c: (i, jnp.clip(s - st, 0, c - 1))) for st, c in zip(starts, counts)]
    row = pl.BlockSpec((tm, N), lambda i, s: (i, 0))
    vec = pl.BlockSpec((1, N), lambda i, s: (0, 0))
    return _call(
        body, name=name, grid=(T // tm, nsteps),
        in_specs=a_specs + [pl.BlockSpec((tk, N), lambda i, s: (s, 0)), row, row, vec], out_specs=[row, row, vec],
        out_shape=[jax.ShapeDtypeStruct((T, N), F32), jax.ShapeDtypeStruct((T, N), BF16), jax.ShapeDtypeStruct((1, N), F32)],
        scratch_shapes=[pltpu.VMEM((tm, N), F32)], compiler_params=_params("arbitrary", "arbitrary"),
    )(*parts, b, x, resid, g)


def _norm_proj(x, g, wT, name, tm=1024, tn=1280):
    T, K = x.shape
    N = wT.shape[0]

    def body(x_ref, g_ref, w_ref, o_ref, h_ref):
        xv = x_ref[...]
        r = lax.rsqrt(jnp.mean(xv * xv, axis=-1, keepdims=True) + EPS)
        hv = (xv * r * g_ref[...]).astype(BF16)

        @pl.when(pl.program_id(1) == 0)
        def _():
            h_ref[...] = hv

        o_ref[...] = _dot(hv, w_ref[...], _NT)

    return _call(
        body, name=name, grid=(T // tm, N // tn),
        in_specs=[pl.BlockSpec((tm, K), lambda i, j: (i, 0)), pl.BlockSpec((1, K), lambda i, j: (0, 0)),
                  pl.BlockSpec((tn, K), lambda i, j: (j, 0))],
        out_specs=[pl.BlockSpec((tm, tn), lambda i, j: (i, j)), pl.BlockSpec((tm, K), lambda i, j: (i, 0))],
        out_shape=[jax.ShapeDtypeStruct((T, N), F32), jax.ShapeDtypeStruct((T, K), BF16)],
        compiler_params=_params("parallel", "arbitrary"),
    )(x, g, wT)


def _qk_prep(proj, pos, invf, qg, kg, bd, name, tm=512):
    T = proj.shape[0]

    def body(q_ref, k_ref, pos_ref, invf_ref, qg_ref, kg_ref, bd_ref, qo_ref, ko_ref):
        cos, sin = _rope_tables(pos_ref, invf_ref)

        def prep(xv, gv, scale):
            r = lax.rsqrt(_group_mean(xv * xv, bd_ref[...]) + EPS)
            yv = xv * r * gv
            return ((yv * cos + _rot_half(yv) * sin) * scale).astype(BF16).astype(F32)

        qo_ref[...] = prep(q_ref[...], qg_ref[...], HEAD_DIM ** -0.5)
        ko_ref[...] = prep(k_ref[...], kg_ref[...], 1.0)

    col = lambda j: pl.BlockSpec((tm, ATTN_W), lambda i, j=j: (i, j))
    vec = pl.BlockSpec((1, ATTN_W), lambda i: (0, 0))
    out = pl.BlockSpec((tm, ATTN_W), lambda i: (i, 0))
    return _call(
        body, name=name, grid=(T // tm,),
        in_specs=[col(0), col(1), pl.BlockSpec((tm, 1), lambda i: (i, 0)), vec, vec, vec,
                  pl.BlockSpec((2 * HEAD_DIM, 2 * HEAD_DIM), lambda i: (0, 0))],
        out_specs=[out, out], out_shape=[jax.ShapeDtypeStruct((T, ATTN_W), F32)] * 2,
        compiler_params=_params("parallel"),
    )(proj, proj, pos, invf, qg, kg, bd)


def _qk_prep_bwd(proj, dqh, dkh, dv, pos, invf, qg, kg, bd, name, tm=512):
    T = proj.shape[0]

    def body(q_ref, k_ref, dq_ref, dk_ref, dv_ref, pos_ref, invf_ref, qg_ref, kg_ref, bd_ref, o_ref, gq_ref, gk_ref):
        @pl.when(pl.program_id(0) == 0)
        def _():
            gq_ref[...] = jnp.zeros_like(gq_ref)
            gk_ref[...] = jnp.zeros_like(gk_ref)

        cos, sin = _rope_tables(pos_ref, invf_ref)

        def back(xv, gv, dz, scale):
            dz = dz * scale
            dy = dz * cos - _rot_half(dz * sin)
            r = lax.rsqrt(_group_mean(xv * xv, bd_ref[...]) + EPS)
            gd = dy * gv
            m = _group_mean(gd * xv, bd_ref[...])
            dx = r * gd - xv * (r * r * r) * m
            return dx, jnp.sum(dy * xv * r, axis=0, keepdims=True)

        dxq, gs = back(q_ref[...], qg_ref[...], dq_ref[...], HEAD_DIM ** -0.5)
        gq_ref[...] += gs
        dxk, gs = back(k_ref[...], kg_ref[...], dk_ref[...], 1.0)
        gk_ref[...] += gs
        o_ref[...] = jnp.concatenate([dxq.astype(BF16), dxk.astype(BF16), dv_ref[...].astype(BF16)], axis=1)

    col = lambda j: pl.BlockSpec((tm, ATTN_W), lambda i, j=j: (i, j))
    row = pl.BlockSpec((tm, ATTN_W), lambda i: (i, 0))
    vec = pl.BlockSpec((1, ATTN_W), lambda i: (0, 0))
    return _call(
        body, name=name, grid=(T // tm,),
        in_specs=[col(0), col(1), row, row, row, pl.BlockSpec((tm, 1), lambda i: (i, 0)), vec, vec, vec,
                  pl.BlockSpec((2 * HEAD_DIM, 2 * HEAD_DIM), lambda i: (0, 0))],
        out_specs=[pl.BlockSpec((tm, 3 * ATTN_W), lambda i: (i, 0)), vec, vec],
        out_shape=[jax.ShapeDtypeStruct((T, 3 * ATTN_W), BF16)] + [jax.ShapeDtypeStruct((1, ATTN_W), F32)] * 2,
        compiler_params=_params("arbitrary"),
    )(proj, proj, dqh, dkh, dv, pos, invf, qg, kg, bd)


def _ld(ref, start, size, dil):
    return ref[pl.ds(start, size), :] if dil == 1 else ref[pl.ds(start, size, stride=dil), :]


def _st(ref, start, size, dil, val):
    if dil == 1:
        ref[pl.ds(start, size), :] = val
    else:
        ref[pl.ds(start, size, stride=dil), :] = val


def _attn_geometry(T, dil):
    nb = T // dil // QBLK
    if nb == 2:
        return 1, 2 * QBLK, 2 * QBLK
    return nb, QBLK, (2 * QBLK if nb >= 2 else QBLK)


def _attn_unroll(qb):
    return 4


def _attn_unit(j, u, dil, nit, unroll):
    return unroll * j + u if dil >= unroll else j + u * (nit // unroll)


def _attn_block(it, dil, qb, kw):
    c, n = it & (dil - 1), lax.shift_right_logical(it, dil.bit_length() - 1)
    sq = n * (qb * dil) + c
    sk = jnp.maximum(n - (kw // qb - 1), 0) * (qb * dil) + c
    qi = lax.broadcasted_iota(jnp.int32, (2 * qb, kw), 0) & (qb - 1)
    kj = lax.broadcasted_iota(jnp.int32, (2 * qb, kw), 1)
    rel = jnp.where(n > 0, kw - qb, 0) + qi - kj
    return sq, sk, (rel >= 0) & (rel <= QBLK)


def _stack_heads(xv, head0):
    z = jnp.zeros_like(xv)
    return jnp.concatenate([jnp.where(head0, xv, z), jnp.where(head0, z, xv)], axis=0)


def _unstack_heads(x2, head0):
    qb = x2.shape[0] // 2
    return jnp.where(head0, x2[:qb], x2[qb:])


def _attn_fwd(qf, kf, proj, name):
    T = qf.shape[0]

    def body(q_ref, k_ref, v_ref, o_ref, lse_ref):
        for bi, dil in enumerate(DILATIONS):
            nb, qb, kw = _attn_geometry(T, dil)
            nit = nb * dil
            head0 = lax.broadcasted_iota(jnp.int32, (qb, 2 * HEAD_DIM), 1) < HEAD_DIM

            def step(j, carry, bi=bi, dil=dil, qb=qb, kw=kw, nit=nit, head0=head0):
                units = []
                for u in range(_attn_unroll(qb)):
                    sq, sk, ok = _attn_block(_attn_unit(j, u, dil, nit, _attn_unroll(qb)), dil, qb, kw)
                    old = (_ld(o_ref, sq, qb, dil), _ld(lse_ref, sq, qb, dil)) if bi > 0 else None
                    units.append((sq, ok, _ld(q_ref, sq, qb, dil).astype(BF16), _ld(k_ref, sk, kw, dil).astype(BF16),
                                  _ld(v_ref, sk, kw, dil).astype(BF16), old))
                results = []
                for sq, ok, qv, kv, vv, old in units:
                    s = jnp.where(ok, _dot(_stack_heads(qv, head0), kv, _NT), NEG_INF)
                    m = jnp.max(s, axis=-1, keepdims=True)
                    p = jnp.exp(s - m).astype(BF16)
                    acc = _dot(p, jnp.concatenate([vv, jnp.ones_like(vv)], axis=1))
                    l = acc[:, 2 * HEAD_DIM:]
                    o_new = _unstack_heads(acc[:, :2 * HEAD_DIM] / l, head0)
                    l_new = _unstack_heads(m + jnp.log(l), head0)
                    if bi > 0:
                        o_old, l_old = old
                        mx = jnp.maximum(l_old, l_new)
                        e0, e1 = jnp.exp(l_old - mx), jnp.exp(l_new - mx)
                        z = e0 + e1
                        o_new = (e0 * o_old + e1 * o_new) / z
                        l_new = mx + jnp.log(z)
                    results.append((sq, o_new, l_new))
                for sq, o_new, l_new in results:
                    _st(o_ref, sq, qb, dil, o_new)
                    _st(lse_ref, sq, qb, dil, l_new)
                return carry

            lax.fori_loop(0, nit // _attn_unroll(qb), step, 0)

    blk = lambda off: pl.BlockSpec((T, 2 * HEAD_DIM), lambda hp, off=off: (0, off + hp))
    return _call(
        body, name=name, grid=(4,), in_specs=[blk(0), blk(0), blk(8)], out_specs=[blk(0), blk(0)],
        out_shape=[jax.ShapeDtypeStruct((T, ATTN_W), F32)] * 2, compiler_params=_params("parallel"),
    )(qf, kf, proj)


def _attn_bwd(qf, kf, proj, do, lse, delta, name):
    T = qf.shape[0]

    def body(q_ref, k_ref, v_ref, do_ref, lse_ref, dl_ref, dq_ref, dk_ref, dv_ref):
        for ref in (dq_ref, dk_ref, dv_ref):
            ref[...] = jnp.zeros_like(ref)
        for dil in DILATIONS:
            nb, qb, kw = _attn_geometry(T, dil)
            nit = nb * dil
            head0 = lax.broadcasted_iota(jnp.int32, (qb, 2 * HEAD_DIM), 1) < HEAD_DIM

            def step(j, carry, dil=dil, qb=qb, kw=kw, nit=nit, head0=head0):
                units = []
                for u in range(_attn_unroll(qb)):
                    sq, sk, ok = _attn_block(_attn_unit(j, u, dil, nit, _attn_unroll(qb)), dil, qb, kw)
                    lsev, dlv = _ld(lse_ref, sq, qb, dil), _ld(dl_ref, sq, qb, dil)
                    units.append((sq, sk, ok, _ld(q_ref, sq, qb, dil).astype(BF16), _ld(do_ref, sq, qb, dil).astype(BF16),
                                  jnp.concatenate([lsev[:, 0:1], lsev[:, HEAD_DIM:HEAD_DIM + 1]], axis=0),
                                  jnp.concatenate([dlv[:, 0:1], dlv[:, HEAD_DIM:HEAD_DIM + 1]], axis=0),
                                  _ld(k_ref, sk, kw, dil).astype(BF16), _ld(v_ref, sk, kw, dil).astype(BF16),
                                  _ld(dq_ref, sq, qb, dil), _ld(dk_ref, sk, kw, dil), _ld(dv_ref, sk, kw, dil)))
                results = []
                for sq, sk, ok, qv, dov, lse2, dl2, kv, vv, dq0, dk0, dv0 in units:
                    q2, do2 = _stack_heads(qv, head0), _stack_heads(dov, head0)
                    p = jnp.where(ok, jnp.exp(_dot(q2, kv, _NT) - lse2), 0.0)
                    ds = (p * (_dot(do2, vv, _NT) - dl2)).astype(BF16)
                    results.append((sq, sk, dq0 + _unstack_heads(_dot(ds, kv), head0),
                                    dk0 + _dot(ds, q2, _TN), dv0 + _dot(p.astype(BF16), do2, _TN)))
                for sq, sk, dq, dk, dv in results:
                    _st(dq_ref, sq, qb, dil, dq)
                    _st(dk_ref, sk, kw, dil, dk)
                    _st(dv_ref, sk, kw, dil, dv)
                return carry

            lax.fori_loop(0, nit // _attn_unroll(qb), step, 0)

    blk = lambda off: pl.BlockSpec((T, 2 * HEAD_DIM), lambda hp, off=off: (0, off + hp))
    return _call(
        body, name=name, grid=(4,), in_specs=[blk(0), blk(0), blk(8), blk(0), blk(0), blk(0)], out_specs=[blk(0)] * 3,
        out_shape=[jax.ShapeDtypeStruct((T, ATTN_W), F32)] * 3, compiler_params=_params("parallel"),
    )(qf, kf, proj, do, lse, delta)


def _attn_norm(attn, g, name, tm=512):
    T = attn.shape[0]

    def body(a_ref, g_ref, o_ref):
        av = a_ref[...]
        r = lax.rsqrt(jnp.mean(av * av, axis=-1, keepdims=True) + EPS)
        o_ref[...] = (av * r * g_ref[...]).astype(BF16)

    row = pl.BlockSpec((tm, ATTN_W), lambda i: (i, 0))
    return _call(
        body, name=name, grid=(T // tm,), in_specs=[row, pl.BlockSpec((1, ATTN_W), lambda i: (0, 0))], out_specs=row,
        out_shape=jax.ShapeDtypeStruct((T, 2 * ATTN_W), BF16), compiler_params=_params("parallel"),
    )(attn, g)


def _attn_norm_bwd(dmix, attn, g, bd, name, tm=512):
    T = attn.shape[0]

    def body(d_ref, a_ref, g_ref, bd_ref, do_ref, dl_ref, dg_ref):
        @pl.when(pl.program_id(0) == 0)
        def _():
            dg_ref[...] = jnp.zeros_like(dg_ref)

        dy, av = d_ref[...], a_ref[...]
        r = lax.rsqrt(jnp.mean(av * av, axis=-1, keepdims=True) + EPS)
        gd = dy * g_ref[...]
        m = jnp.mean(gd * av, axis=-1, keepdims=True)
        da = r * gd - av * (r * r * r) * m
        do_ref[...] = da
        dl_ref[...] = _group_mean(da * av, bd_ref[...]) * float(HEAD_DIM)
        dg_ref[...] += jnp.sum(dy * av * r, axis=0, keepdims=True)

    row = pl.BlockSpec((tm, ATTN_W), lambda i: (i, 0))
    vec = pl.BlockSpec((1, ATTN_W), lambda i: (0, 0))
    return _call(
        body, name=name, grid=(T // tm,),
        in_specs=[row, row, vec, pl.BlockSpec((2 * HEAD_DIM, 2 * HEAD_DIM), lambda i: (0, 0))], out_specs=[row, row, vec],
        out_shape=[jax.ShapeDtypeStruct((T, ATTN_W), F32)] * 2 + [jax.ShapeDtypeStruct((1, ATTN_W), F32)],
        compiler_params=_params("arbitrary"),
    )(dmix, attn, g, bd)


def _rec_gates(xc, wrg_ref, wig_ref, brg_ref, big_ref, lam_ref):
    xb = xc.astype(BF16)
    r = _sigmoid(_dot(xb, wrg_ref[...]) + brg_ref[...])
    ig = _sigmoid(_dot(xb, wig_ref[...]) + big_ref[...])
    sp = _softplus_neg(lam_ref[...])
    log_a = -LRU_C * r * sp
    a = jnp.exp(log_a)
    th = jnp.tanh(log_a)
    mult = jnp.sqrt(-2.0 * th / (1.0 - th))
    return xb, r, ig, sp, a, mult


def _rec_fwd(proj, mix, cw, cb, wrg, wig, brg, big, lam, g, name, tm=256):
    T = proj.shape[0]
    hb = tm // 8

    def body(xr_ref, halo_ref, gr_ref, cw_ref, cb_ref, wrg_ref, wig_ref, brg_ref, big_ref, lam_ref, g_ref, mix_ref,
             xc_ref, h_ref, out_ref, carry):
        i = pl.program_id(0)

        @pl.when(i == 0)
        def _():
            carry[...] = jnp.zeros_like(carry)

        xr = xr_ref[...]
        halo = jnp.where(i > 0, halo_ref[...], 0.0)
        xc = cb_ref[...] + cw_ref[3:4, :] * xr
        for s in range(1, REC_CONV):
            xc = xc + cw_ref[3 - s:4 - s, :] * _shift_down(xr, halo, s)
        xc_ref[...] = xc
        _, _, ig, _, a, mult = _rec_gates(xc, wrg_ref, wig_ref, brg_ref, big_ref, lam_ref)
        pa, hl = _scan_fwd(a, mult * (ig * xc))
        h = hl + pa * carry[0:1, :]
        h_ref[...] = h
        carry[0:1, :] = h_ref[pl.ds(tm - 1, 1), :]
        hg = h * _gelu(gr_ref[...])
        r = lax.rsqrt(jnp.mean(hg * hg, axis=-1, keepdims=True) + EPS)
        out_ref[...] = (hg * r * g_ref[...]).astype(BF16)

    vec = pl.BlockSpec((1, REC_W), lambda i: (0, 0))
    row = pl.BlockSpec((tm, REC_W), lambda i: (i, 0))
    mat = pl.BlockSpec((REC_W, REC_W), lambda i: (0, 0))
    return _call(
        body, name=name, grid=(T // tm,),
        in_specs=[pl.BlockSpec((tm, REC_W), lambda i: (i, 3)),
                  pl.BlockSpec((8, REC_W), lambda i: (jnp.maximum(i * hb - 1, 0), 3)),
                  pl.BlockSpec((tm, REC_W), lambda i: (i, 4)),
                  pl.BlockSpec((8, REC_W), lambda i: (0, 0)), vec, mat, mat, vec, vec, vec, vec, ANY],
        out_specs=[row, row, pl.BlockSpec((tm, REC_W), lambda i: (i, 1))],
        out_shape=[jax.ShapeDtypeStruct((T, REC_W), F32)] * 2 + [jax.ShapeDtypeStruct(mix.shape, BF16)],
        scratch_shapes=[pltpu.VMEM((8, REC_W), F32)], input_output_aliases={11: 2},
        compiler_params=_params("arbitrary"),
    )(proj, proj, proj, cw, cb, wrg, wig, brg, big, lam, g, mix)


def _rec_bwd(dmix, proj, xc, h, cw, cb, wrg, wig, brg, big, lam, g, name, tm=256):
    T = proj.shape[0]
    nt = T // tm
    hb = tm // 8

    def body(d_ref, xr_ref, xhalo_ref, gr_ref, xc_ref, h_ref, hhalo_ref, cw_ref, cb_ref, wrg_ref, wig_ref, brg_ref,
             big_ref, lam_ref, g_ref,
             drec_ref, gcw_ref, gcb_ref, gwrg_ref, gwig_ref, gbrg_ref, gbig_ref, glam_ref, gg_ref,
             g_carry, a_first, dxc_next, gsp):
        i = pl.program_id(0)
        first_tile = i == nt - 1

        @pl.when(i == 0)
        def _():
            for ref in (gcw_ref, gcb_ref, gwrg_ref, gwig_ref, gbrg_ref, gbig_ref, glam_ref, gg_ref,
                        g_carry, a_first, dxc_next, gsp):
                ref[...] = jnp.zeros_like(ref)

        xr, xc, hv = xr_ref[...], xc_ref[...], h_ref[...]
        xhalo = jnp.where(first_tile, 0.0, xhalo_ref[...])
        hhalo = jnp.where(first_tile, 0.0, hhalo_ref[...])
        xb, r, ig, sp, a, mult = _rec_gates(xc, wrg_ref, wig_ref, brg_ref, big_ref, lam_ref)
        h_prev = _shift_down(hv, hhalo, 1)
        ge, dge = _gelu_and_grad(gr_ref[...])
        hg = hv * ge
        rr = lax.rsqrt(jnp.mean(hg * hg, axis=-1, keepdims=True) + EPS)
        dy = d_ref[...]
        gd = dy * g_ref[...]
        dhg = rr * gd - hg * (rr * rr * rr) * jnp.mean(gd * hg, axis=-1, keepdims=True)
        gg_ref[...] += jnp.sum(dy * hg * rr, axis=0, keepdims=True)
        dgr = (dhg * hv * dge).astype(BF16)
        dh = dhg * ge
        b = _shift_up(a, jnp.broadcast_to(a_first[0:1, :], (8, REC_W)), 1)
        pb, gl = _scan_bwd(b, dh)
        gs = gl + pb * g_carry[0:1, :]
        g_carry[0:1, :] = gs[0:1, :]
        a_first[0:1, :] = a[0:1, :]
        da = gs * h_prev
        dmult = gs * (ig * xc)
        di = gs * (mult * xc)
        dxc = gs * (mult * ig)
        dlog_a = da * a - dmult * (a * a) / mult
        gsp[...] += jnp.sum(dlog_a * (-LRU_C * r), axis=0, keepdims=True)
        dzr = (dlog_a * (-LRU_C * sp)) * (r * (1.0 - r))
        dzi = di * (ig * (1.0 - ig))
        dzr_b, dzi_b = dzr.astype(BF16), dzi.astype(BF16)
        dxc = dxc + _dot(dzr_b, wrg_ref[...], _NT) + _dot(dzi_b, wig_ref[...], _NT)
        gwrg_ref[...] += _dot(xb, dzr_b, _TN)
        gwig_ref[...] += _dot(xb, dzi_b, _TN)
        gbrg_ref[...] += jnp.sum(dzr, axis=0, keepdims=True)
        gbig_ref[...] += jnp.sum(dzi, axis=0, keepdims=True)
        nxt = dxc_next[...]
        dxr = cw_ref[3:4, :] * dxc
        gcw_ref[3:4, :] += jnp.sum(dxc * xr, axis=0, keepdims=True)
        for s in range(1, REC_CONV):
            dxr = dxr + cw_ref[3 - s:4 - s, :] * _shift_up(dxc, nxt, s)
            gcw_ref[3 - s:4 - s, :] += jnp.sum(dxc * _shift_down(xr, xhalo, s), axis=0, keepdims=True)
        gcb_ref[...] += jnp.sum(dxc, axis=0, keepdims=True)
        dxc_next[...] = dxc[:8]
        drec_ref[...] = jnp.concatenate([dxr.astype(BF16), dgr], axis=1)

        @pl.when(first_tile)
        def _():
            glam_ref[...] = gsp[...] * (-_sigmoid(-lam_ref[...]))

    rev = lambda i: nt - 1 - i
    vec = pl.BlockSpec((1, REC_W), lambda i: (0, 0))
    row = pl.BlockSpec((tm, REC_W), lambda i: (rev(i), 0))
    mat = pl.BlockSpec((REC_W, REC_W), lambda i: (0, 0))
    cwb = pl.BlockSpec((8, REC_W), lambda i: (0, 0))
    halo = lambda c: pl.BlockSpec((8, REC_W), lambda i, c=c: (jnp.maximum(rev(i) * hb - 1, 0), c))
    return _call(
        body, name=name, grid=(nt,),
        in_specs=[pl.BlockSpec((tm, REC_W), lambda i: (rev(i), 1)),
                  pl.BlockSpec((tm, REC_W), lambda i: (rev(i), 3)), halo(3),
                  pl.BlockSpec((tm, REC_W), lambda i: (rev(i), 4)),
                  row, row, halo(0), cwb, vec, mat, mat, vec, vec, vec, vec],
        out_specs=[pl.BlockSpec((tm, 2 * REC_W), lambda i: (rev(i), 0)), cwb, vec, mat, mat, vec, vec, vec, vec],
        out_shape=[jax.ShapeDtypeStruct((T, 2 * REC_W), BF16)]
        + [jax.ShapeDtypeStruct((8, REC_W), F32), jax.ShapeDtypeStruct((1, REC_W), F32)]
        + [jax.ShapeDtypeStruct((REC_W, REC_W), F32)] * 2 + [jax.ShapeDtypeStruct((1, REC_W), F32)] * 4,
        scratch_shapes=[pltpu.VMEM((8, REC_W), F32)] * 3 + [pltpu.VMEM((1, REC_W), F32)],
        compiler_params=_params("arbitrary"),
    )(dmix, proj, proj, proj, xc, h, h, cw, cb, wrg, wig, brg, big, lam, g)


def _ffn_conv(x_ext, cw_ref, cb_ref):
    return (cb_ref[...] + cw_ref[2:3, :] * x_ext + cw_ref[1:2, :] * pltpu.roll(x_ext, 1, 0)
            + cw_ref[0:1, :] * pltpu.roll(x_ext, 2, 0))


def _up_proj_act(x2, g, w_upT, cw, cb, name, tm=1024, tc=768):
    T = x2.shape[0]
    nc = D_FF // tc

    def body(x_ref, g_ref, wg_ref, wu_ref, cwg_ref, cwu_ref, cbg_ref, cbu_ref, act_ref, da_ref, db_ref, pg_ref, pu_ref,
             h_ref, hist_g, hist_u, hs):
        i, j = pl.program_id(0), pl.program_id(1)

        @pl.when(j == 0)
        def _():
            xv = x_ref[...]
            r = lax.rsqrt(jnp.mean(xv * xv, axis=-1, keepdims=True) + EPS)
            hs[...] = (xv * r * g_ref[...]).astype(BF16)
            h_ref[...] = hs[...]

        hv = hs[...]
        pg, pu = _dot(hv, wg_ref[...], _NT), _dot(hv, wu_ref[...], _NT)
        ge = jnp.concatenate([jnp.where(i > 0, hist_g[j], 0.0), pg], axis=0)
        ue = jnp.concatenate([jnp.where(i > 0, hist_u[j], 0.0), pu], axis=0)
        gel, dgel = _gelu_and_grad(_ffn_conv(ge, cwg_ref, cbg_ref)[8:])
        uu = _ffn_conv(ue, cwu_ref, cbu_ref)[8:]
        act_ref[...] = (gel * uu).astype(BF16)
        da_ref[...] = (uu * dgel).astype(BF16)
        db_ref[...] = gel.astype(BF16)
        pg_ref[...] = pg.astype(BF16)
        pu_ref[...] = pu.astype(BF16)
        hist_g[j] = pg[tm - 8:]
        hist_u[j] = pu[tm - 8:]

    tile = pl.BlockSpec((tm, tc), lambda i, j: (i, j))
    wsp = lambda off: pl.BlockSpec((tc, D_MODEL), lambda i, j, off=off: (j + off, 0))
    cws = lambda off: pl.BlockSpec((8, tc), lambda i, j, off=off: (0, j + off))
    cbs = lambda off: pl.BlockSpec((1, tc), lambda i, j, off=off: (0, j + off))
    return _call(
        body, name=name, grid=(T // tm, nc),
        in_specs=[pl.BlockSpec((tm, D_MODEL), lambda i, j: (i, 0)), pl.BlockSpec((1, D_MODEL), lambda i, j: (0, 0)),
                  wsp(0), wsp(nc), cws(0), cws(nc), cbs(0), cbs(nc)],
        out_specs=[tile] * 5 + [pl.BlockSpec((tm, D_MODEL), lambda i, j: (i, 0))],
        out_shape=[jax.ShapeDtypeStruct((T, D_FF), BF16)] * 5 + [jax.ShapeDtypeStruct((T, D_MODEL), BF16)],
        scratch_shapes=[pltpu.VMEM((nc, 8, tc), F32)] * 2 + [pltpu.VMEM((tm, D_MODEL), BF16)],
        compiler_params=_params("arbitrary", "arbitrary"),
    )(x2, g, w_upT, w_upT, cw, cw, cb, cb)


def _ffn_bwd(dyb, w_down, da, db, pg, pu, cw, name, tm=512, tc=768):
    T, F = pg.shape
    nt = T // tm
    hb16 = tm // 16
    nc = F // tc
    n = tm + 8

    def body(dy_ref, dyn_ref, wd_ref, a_ref, an_ref, b_ref, bn_ref, g_ref, u_ref, cwg_ref, cwu_ref,
             dg_ref, du_ref, gcwg_ref, gcwu_ref, gcbg_ref, gcbu_ref):
        i = pl.program_id(1)
        last = i == nt - 1

        @pl.when(i == 0)
        def _():
            for ref in (gcwg_ref, gcwu_ref, gcbg_ref, gcbu_ref):
                ref[...] = jnp.zeros_like(ref)

        wd = wd_ref[...]
        dact_next = jnp.where(last, 0.0, _dot(dyn_ref[...], wd, _NT)[:8])
        de = jnp.concatenate([_dot(dy_ref[...], wd, _NT), dact_next], axis=0)
        ext = lambda t, nx: jnp.concatenate([t[...].astype(F32), nx[...].astype(F32)[:8]], axis=0)
        for dcv, x_ref, cw_ref, dx_ref, gcw_ref, gcb_ref in ((de * ext(a_ref, an_ref), g_ref, cwg_ref, dg_ref, gcwg_ref, gcbg_ref),
                                                               (de * ext(b_ref, bn_ref), u_ref, cwu_ref, du_ref, gcwu_ref, gcbu_ref)):
            s1, s2 = pltpu.roll(dcv, n - 1, 0), pltpu.roll(dcv, n - 2, 0)
            dx_ref[...] = (cw_ref[2:3, :] * dcv + cw_ref[1:2, :] * s1 + cw_ref[0:1, :] * s2)[:tm].astype(BF16)
            xv = x_ref[...].astype(F32)
            gcw_ref[2:3, :] += jnp.sum(xv * dcv[:tm], axis=0, keepdims=True)
            gcw_ref[1:2, :] += jnp.sum(xv * s1[:tm], axis=0, keepdims=True)
            gcw_ref[0:1, :] += jnp.sum(xv * s2[:tm], axis=0, keepdims=True)
            gcb_ref[...] += jnp.sum(dcv[:tm], axis=0, keepdims=True)

    tile = pl.BlockSpec((tm, tc), lambda j, i: (i, j))
    nxt = pl.BlockSpec((16, tc), lambda j, i: (jnp.minimum((i + 1) * hb16, nt * hb16 - 1), j))
    cws = lambda off: pl.BlockSpec((8, tc), lambda j, i, off=off: (0, j + off))
    cbs = pl.BlockSpec((1, tc), lambda j, i: (0, j))
    return _call(
        body, name=name, grid=(nc, nt),
        in_specs=[pl.BlockSpec((tm, D_MODEL), lambda j, i: (i, 0)),
                  pl.BlockSpec((16, D_MODEL), lambda j, i: (jnp.minimum((i + 1) * hb16, nt * hb16 - 1), 0)),
                  pl.BlockSpec((tc, D_MODEL), lambda j, i: (j, 0)), tile, nxt, tile, nxt, tile, tile, cws(0), cws(nc)],
        out_specs=[tile, tile, cws(0), cws(0), cbs, cbs],
        out_shape=[jax.ShapeDtypeStruct((T, F), BF16)] * 2 + [jax.ShapeDtypeStruct((8, F), F32)] * 2
        + [jax.ShapeDtypeStruct((1, F), F32)] * 2,
        compiler_params=_params("parallel", "arbitrary"),
    )(dyb, dyb, w_down, da, da, db, db, pg, pu, cw, cw)


def _adam_update(w, g, m, v):
    m2 = ADAM_B1 * m + (1.0 - ADAM_B1) * g
    v2 = ADAM_B2 * v + (1.0 - ADAM_B2) * (g * g)
    m_hat = m2 / (1.0 - ADAM_B1 ** ADAM_STEP)
    v_hat = v2 / (1.0 - ADAM_B2 ** ADAM_STEP)
    delta = -ADAM_LR * (m_hat / (jnp.sqrt(v_hat) + ADAM_EPS) + ADAM_WD * w)
    return delta, m2, v2


def _adam_sharded(p, r2, idx, w, m, v, name, transposed=False):
    r, n = p.shape[1:]
    nrecv = r2.shape[0]
    tr = (256 if r % 256 == 0 else r) if transposed else _row_tile(r)

    def body(c_ref, p_ref, r_ref, w_ref, m_ref, v_ref, g_ref, d_ref, m2_ref, v2_ref):
        g = p_ref[...].astype(F32)
        for k in range(nrecv):
            g = g + r_ref[k].astype(F32)
        if transposed:
            g = g.T
        g_ref[...] = g
        d_ref[...], m2_ref[...], v2_ref[...] = _adam_update(w_ref[...], g, m_ref[...], v_ref[...])

    blk = pl.BlockSpec((n, tr), lambda i, c_ref: (0, i)) if transposed else pl.BlockSpec((tr, n), lambda i, c_ref: (i, 0))
    spec = pltpu.PrefetchScalarGridSpec(
        num_scalar_prefetch=1, grid=(r // tr,),
        in_specs=[pl.BlockSpec((None, tr, n), lambda i, c_ref: (c_ref[0], i, 0)),
                  pl.BlockSpec((nrecv, tr, n), lambda i, c_ref: (0, i, 0)), blk, blk, blk],
        out_specs=[blk] * 4)
    return _call(body, name=name, grid_spec=spec, out_shape=[jax.ShapeDtypeStruct(w.shape, F32)] * 4,
                 compiler_params=_params("parallel"))(idx, p, r2, w, m, v)


def _sum_slabs(p, r2, idx, name):
    _, r, n = p.shape

    def body(c_ref, p_ref, r_ref, o_ref):
        acc = p_ref[...]
        for k in range(N_PEERS):
            acc = acc + r_ref[k]
        o_ref[...] = acc

    spec = pltpu.PrefetchScalarGridSpec(
        num_scalar_prefetch=1, grid=(1,),
        in_specs=[pl.BlockSpec((None, r, n), lambda i, c_ref: (c_ref[0], 0, 0)),
                  pl.BlockSpec((N_PEERS, r, n), lambda i, c_ref: (0, 0, 0))],
        out_specs=pl.BlockSpec((r, n), lambda i, c_ref: (0, 0)))
    return _call(body, name=name, grid_spec=spec, out_shape=jax.ShapeDtypeStruct((r, n), F32))(idx, p, r2)


def _adam_small(ws, gs, ms, vs, name):
    n = len(ws)

    def body(*refs):
        for i in range(n):
            d, m2, v2 = _adam_update(refs[i][...], refs[n + i][...], refs[2 * n + i][...], refs[3 * n + i][...])
            refs[4 * n + i][...] = d
            refs[5 * n + i][...] = m2
            refs[6 * n + i][...] = v2

    outs = _call(body, name=name, out_shape=[jax.ShapeDtypeStruct(w.shape, F32) for w in ws] * 3)(*ws, *gs, *ms, *vs)
    return outs[:n], outs[n:2 * n], outs[2 * n:]


def _pack_small_grads(full, halves, rcw, fcwg, fcwu, wrg, wig, lparts, name):
    nf, nh = len(full), len(halves)

    def body(*refs):
        o = refs[-1]
        o[...] = jnp.zeros_like(o)
        row = 0
        for r in refs[:nf]:
            for j in range(r.shape[1] // 1024):
                o[row:row + 1, :] = r[:, 1024 * j:1024 * (j + 1)]
                row += 1
        for k in range(0, nh, 2):
            o[row:row + 1, 0:512] = refs[nf + k][...]
            o[row:row + 1, 512:1024] = refs[nf + k + 1][...]
            row += 1
        rcw_ref, fg_ref, fu_ref, wrg_ref, wig_ref, l_ref = refs[nf + nh:nf + nh + 6]
        for k in range(2):
            o[row:row + 1, 0:512] = rcw_ref[2 * k:2 * k + 1, :]
            o[row:row + 1, 512:1024] = rcw_ref[2 * k + 1:2 * k + 2, :]
            row += 1
        for f_ref in (fg_ref, fu_ref):
            for k in range(FFN_CONV):
                for j in range(D_FF // 1024):
                    o[row:row + 1, :] = f_ref[k:k + 1, 1024 * j:1024 * (j + 1)]
                    row += 1
        assert row == 32
        for n in range(8):
            o[32:96, 64 * n:64 * n + 64] = wrg_ref[64 * n:64 * n + 64, 64 * n:64 * n + 64]
            o[32:96, 512 + 64 * n:512 + 64 * n + 64] = wig_ref[64 * n:64 * n + 64, 64 * n:64 * n + 64]
        o[96:97, :] = jnp.sum(l_ref[...], axis=0, keepdims=True)

    return _call(body, name=name, out_shape=jax.ShapeDtypeStruct((SMALL_ROWS, 1024), F32))(
        *full, *halves, rcw, fcwg, fcwu, wrg, wig, lparts)


def _block_diag(w):
    eye = jnp.eye(8, dtype=w.dtype)
    return (w[:, :, None, :] * eye[:, None, :, None]).reshape(512, 512)


def kernel(x, positions, g_mix, w_in, q_norm_g, k_norm_g, rec_conv_w, rec_conv_b, w_rg, b_rg, w_ig, b_ig, lru_lambda, g_attn_out, g_rec_out, w_out, g_ffn, w_up, ffn_conv_w, ffn_conv_b, w_down, loss_target, m_g_mix, m_w_in, m_q_norm_g, m_k_norm_g, m_rec_conv_w, m_rec_conv_b, m_w_rg, m_b_rg, m_w_ig, m_b_ig, m_lru_lambda, m_g_attn_out, m_g_rec_out, m_w_out, m_g_ffn, m_w_up, m_ffn_conv_w, m_ffn_conv_b, m_w_down, v_g_mix, v_w_in, v_q_norm_g, v_k_norm_g, v_rec_conv_w, v_rec_conv_b, v_w_rg, v_b_rg, v_w_ig, v_b_ig, v_lru_lambda, v_g_attn_out, v_g_rec_out, v_w_out, v_g_ffn, v_w_up, v_ffn_conv_w, v_ffn_conv_b, v_w_down):
    T = x.shape[1]
    ix, iy, ic = lax.axis_index("x"), lax.axis_index("y"), lax.axis_index("c")
    dev = 4 * ix + 2 * iy + ic
    xs = x.reshape(T, D_MODEL)
    tgt = loss_target.reshape(T, D_MODEL)
    pos = positions.reshape(T, 1)

    shards = {"w_in": (w_in[0], m_w_in[0], v_w_in[0]), "w_out": (w_out[0], m_w_out[0], v_w_out[0]),
              "w_up": (w_up[0], m_w_up[0], v_w_up[0]), "w_down": (w_down[0], m_w_down[0], v_w_down[0])}
    taps = jnp.concatenate([rec_conv_w.reshape(-1), ffn_conv_w.reshape(-1), jnp.zeros((4096 - 2560,), F32)]).reshape(8, 512)
    W_inT, taps_all = _all_gather([w_in[0].T.astype(BF16), taps], "ag_w_in")
    late = [w_out[0].astype(BF16), w_up[0].T.astype(BF16), w_down[0].astype(BF16)]
    ag_send, ag_recv, late_thru, land_thru, ag_token = _exchange_start(
        late, [_landing((N_DEV * s.shape[0], 1024), BF16, s, dev * s.shape[0]) for s in late], "gather", taps_all,
        "ag_late_start")
    taps_all = taps_all.reshape(N_DEV, 4096)
    rcw = taps_all[:, :256].reshape(8, 4, 64).transpose(1, 0, 2).reshape(4, REC_W)
    fcw = taps_all[:, 256:2560].reshape(8, 3, 768).transpose(1, 0, 2).reshape(3, 2 * D_FF)
    rcw8 = jnp.pad(rcw, ((0, 4), (0, 0)))
    fcw8 = jnp.pad(fcw, ((0, 5), (0, 0)))
    fcb = ffn_conv_b.reshape(1, 2 * D_FF)

    half = HEAD_DIM // 2
    inv_freq = ROPE_THETA ** (-jnp.arange(half, dtype=F32) / half)
    invf = jnp.tile(inv_freq, 2 * N_HEADS).reshape(1, ATTN_W)
    bd = jnp.asarray(np.kron(np.eye(2), np.full((HEAD_DIM, HEAD_DIM), 1.0 / HEAD_DIM)), BF16)
    qg = jnp.tile(q_norm_g.reshape(HEAD_DIM), N_HEADS).reshape(1, ATTN_W)
    kg = jnp.tile(k_norm_g.reshape(HEAD_DIM), N_HEADS).reshape(1, ATTN_W)
    wrg_bd = _block_diag(w_rg[0]).astype(BF16)
    wig_bd = _block_diag(w_ig[0]).astype(BF16)
    brg, big = b_rg.reshape(1, REC_W), b_ig.reshape(1, REC_W)

    proj, h1 = _norm_proj(xs, g_mix + ag_token[0, 0], W_inT, "in_proj")
    qf, kf = _qk_prep(proj, pos, invf, qg, kg, bd, "qk_prep")
    attn, lse = _attn_fwd(qf, kf, proj, "attn_fwd")
    mix = _attn_norm(attn, g_attn_out, "attn_norm")
    xc, hstate, mix = _rec_fwd(proj, mix, rcw8, rec_conv_b, wrg_bd, wig_bd, brg, big, lru_lambda, g_rec_out, "rec_fwd")
    _, (W_out, W_upT, W_down) = _exchange_wait(ag_send, ag_recv, late_thru, land_thru, "gather", hstate, "ag_late_wait")
    x2 = _mm(mix, W_out, "nn", F32, "out_proj", add=xs)

    act, da, db, pg, pu, h2 = _up_proj_act(x2, g_ffn, W_upT, fcw8, fcb, "up_proj_act")
    dy, dyb, lparts = _mm(act, W_down, "nn", F32, "down_proj_loss", add=x2, loss_target=tgt, tm=512, tk=D_FF)

    g_down = _mm(act, dyb, "tn", BF16, "g_w_down", tk=4096)
    dpg, dpu, g_fcwg, g_fcwu, g_fcbg, g_fcbu = _ffn_bwd(dyb, W_down, da, db, pg, pu, fcw8, "ffn_bwd")
    g_upT = _mm(dpg, h2, "tn", BF16, "g_w_up_gate", tk=4096, o_rows=2 * D_FF)
    g_upT = _mm(dpu, h2, "tn", BF16, "g_w_up_up", tk=4096, into=g_upT, o_moff=D_FF // 1024)
    ffn_g = [g_upT.reshape(N_DEV, 2 * D_FF // N_DEV, 1024), g_down.reshape(N_DEV, D_FF // N_DEV, 1024)]
    rs_send, rs_recv, ffn_g, ffn_land, rs_token = _exchange_start(
        ffn_g, [_landing((N_PEERS,) + g.shape[1:], BF16) for g in ffn_g], "scatter", dpu, "rs_ffn_start")
    dx2, dx2b, g_gffn = _mm_norm_bwd([dpg, dpu], W_upT, x2, dy, g_ffn + rs_token[0, 0], "d_h2_norm_bwd", tm=1024, tk=1024)

    dmix = _mm(dx2b, W_out, "nt", F32, "d_mix")
    g_out = _mm(mix, dx2b, "tn", BF16, "g_w_out", tk=4096).reshape(N_DEV, D_MODEL // N_DEV, 1024)
    out_send, out_recv, (g_out,), out_land, out_token = _exchange_start(
        [g_out], [_landing((N_PEERS,) + g_out.shape[1:], BF16)], "scatter", dmix, "rs_out_start")
    do, delta, g_gattn = _attn_norm_bwd(dmix, attn, g_attn_out + out_token[0, 0], bd, "attn_norm_bwd")
    dqh, dkh, dv = _attn_bwd(qf, kf, proj, do, lse, delta, "attn_bwd")
    dqkv, g_qg, g_kg = _qk_prep_bwd(proj, dqh, dkh, dv, pos, invf, qg, kg, bd, "qk_prep_bwd")
    (drec, g_rcw, g_rcb, g_wrg, g_wig, g_brg, g_big, g_lam, g_grec) = _rec_bwd(
        dmix, proj, xc, hstate, rcw8, rec_conv_b, wrg_bd, wig_bd, brg, big, lru_lambda, g_rec_out, "rec_bwd")
    g_inT = _mm(dqkv, h1, "tn", BF16, "g_w_in_qkv", tm=512, tk=4096, o_rows=IN_W)
    g_inT = _mm(drec, h1, "tn", BF16, "g_w_in_rec", tm=512, tk=4096, into=g_inT, o_moff=3 * ATTN_W // 512)
    g_inT = g_inT.reshape(N_DEV, IN_W // N_DEV, 1024)
    in_send, in_recv, (g_inT,), in_land, in_token = _exchange_start(
        [g_inT], [_landing((N_PEERS,) + g_inT.shape[1:], BF16)], "scatter", drec, "rs_in_start")
    grad_x, _, g_gmix = _mm_norm_bwd([dqkv, drec], W_inT, xs, dx2, g_mix + in_token[0, 0], "d_h1_norm_bwd", tm=1024, tk=512)

    flat = _pack_small_grads([g_gmix, g_gffn, g_fcbg, g_fcbu], [g_rcb, g_brg, g_big, g_lam, g_gattn, g_grec, g_qg, g_kg],
                             g_rcw, g_fcwg, g_fcwu, g_wrg, g_wig, lparts.reshape(-1, D_MODEL), "pack_small_grads")
    srows = SMALL_ROWS // N_DEV
    flat = flat.reshape(N_DEV, srows, 1024)
    sm_send, sm_recv, (flat,), sm_land, sm_token = _exchange_start(
        [flat], [_landing((N_PEERS, srows, 1024), F32)], "scatter", grad_x, "ar_small_rs_start")

    devi = jnp.reshape(dev, (1,)).astype(jnp.int32)
    ffn_g, ffn_land = _exchange_wait(rs_send, rs_recv, ffn_g, ffn_land, "scatter", sm_token, "rs_ffn_wait")
    (g_out,), out_land = _exchange_wait(out_send, out_recv, [g_out], out_land, "scatter", sm_token, "rs_out_wait")
    big_out = {"grad": {}, "delta": {}, "new_m": {}, "new_v": {}}

    def adam_big(nm, p, r):
        w_, m_, v_ = shards[nm]
        res = _adam_sharded(p, r, devi, w_, m_, v_, "adam_" + nm, transposed=nm in ("w_in", "w_up"))
        for kind, a in zip(("grad", "delta", "new_m", "new_v"), res):
            big_out[kind][nm] = a[None]
        return res[0]

    last = adam_big("w_up", ffn_g[0], ffn_land[0])
    (flat,), sm_land = _exchange_wait(sm_send, sm_recv, [flat], sm_land, "scatter", last, "ar_small_rs_wait")
    mine = _sum_slabs(flat, sm_land[0], devi, "sum_small_grads")
    sm_send, sm_recv, (mine,), sm_land, sm_token = _exchange_start(
        [mine], [_landing((SMALL_ROWS, 1024), F32, mine, dev * srows)], "gather", last, "ar_small_ag_start")
    adam_big("w_down", ffn_g[1], ffn_land[1])
    last = adam_big("w_out", g_out, out_land[0])
    _, (tot,) = _exchange_wait(sm_send, sm_recv, [mine], sm_land, "gather", last, "ar_small_ag_wait")
    (g_inT,), in_land = _exchange_wait(in_send, in_recv, [g_inT], in_land, "scatter", tot, "rs_in_wait")
    adam_big("w_in", g_inT, in_land[0])

    half = lambda r, h, shape: tot[r, 512 * h:512 * h + 512].reshape(shape)
    blocks = lambda h: tot[32:96, 512 * h:512 * h + 512].reshape(64, 8, 64).transpose(1, 0, 2)[None]
    fcw_full = jnp.concatenate([tot[14:23].reshape(1, 3, D_FF), tot[23:32].reshape(1, 3, D_FF)], axis=2)
    g_small = {
        "g_mix": tot[0:1], "g_ffn": tot[1:2], "ffn_conv_b": tot[2:8].reshape(1, 2 * D_FF),
        "rec_conv_b": half(8, 0, (1, 512)), "b_rg": half(8, 1, (1, 8, 64)), "b_ig": half(9, 0, (1, 8, 64)),
        "lru_lambda": half(9, 1, (1, 512)), "g_attn_out": half(10, 0, (1, 512)), "g_rec_out": half(10, 1, (1, 512)),
        "q_norm_g": half(11, 0, (N_HEADS, HEAD_DIM)).sum(0)[None], "k_norm_g": half(11, 1, (N_HEADS, HEAD_DIM)).sum(0)[None],
        "w_rg": blocks(0), "w_ig": blocks(1),
        "rec_conv_w": lax.dynamic_slice(tot[12:14].reshape(1, 4, REC_W), (0, 0, 64 * dev), (1, 4, 64)),
        "ffn_conv_w": lax.dynamic_slice(fcw_full, (0, 0, 768 * dev), (1, 3, 768))}
    loss = 0.5 / D_MODEL * jnp.sum(tot[96])
    given = dict(rec_conv_w=rec_conv_w, ffn_conv_w=ffn_conv_w,g_mix=g_mix, q_norm_g=q_norm_g, k_norm_g=k_norm_g, rec_conv_b=rec_conv_b, w_rg=w_rg, b_rg=b_rg, w_ig=w_ig,
                 b_ig=b_ig, lru_lambda=lru_lambda, g_attn_out=g_attn_out, g_rec_out=g_rec_out, g_ffn=g_ffn, ffn_conv_b=ffn_conv_b)
    given_m = dict(rec_conv_w=m_rec_conv_w, ffn_conv_w=m_ffn_conv_w, g_mix=m_g_mix, q_norm_g=m_q_norm_g, k_norm_g=m_k_norm_g, rec_conv_b=m_rec_conv_b, w_rg=m_w_rg, b_rg=m_b_rg,
                   w_ig=m_w_ig, b_ig=m_b_ig, lru_lambda=m_lru_lambda, g_attn_out=m_g_attn_out, g_rec_out=m_g_rec_out,
                   g_ffn=m_g_ffn, ffn_conv_b=m_ffn_conv_b)
    given_v = dict(rec_conv_w=v_rec_conv_w, ffn_conv_w=v_ffn_conv_w, g_mix=v_g_mix, q_norm_g=v_q_norm_g, k_norm_g=v_k_norm_g, rec_conv_b=v_rec_conv_b, w_rg=v_w_rg, b_rg=v_b_rg,
                   w_ig=v_w_ig, b_ig=v_b_ig, lru_lambda=v_lru_lambda, g_attn_out=v_g_attn_out, g_rec_out=v_g_rec_out,
                   g_ffn=v_g_ffn, ffn_conv_b=v_ffn_conv_b)
    small = sorted(given)
    ds, m2s, v2s = _adam_small([given[k] for k in small], [g_small[k] for k in small], [given_m[k] for k in small],
                               [given_v[k] for k in small], "adam_small")
    small_out = {"grad": g_small, "delta": dict(zip(small, ds)), "new_m": dict(zip(small, m2s)), "new_v": dict(zip(small, v2s))}

    order = ("g_mix", "w_in", "q_norm_g", "k_norm_g", "rec_conv_w", "rec_conv_b", "w_rg", "b_rg", "w_ig", "b_ig",
             "lru_lambda", "g_attn_out", "g_rec_out", "w_out", "g_ffn", "w_up", "ffn_conv_w", "ffn_conv_b", "w_down")
    outs = [loss, grad_x.reshape(1, T, D_MODEL)]
    for kind in ("grad", "delta", "new_m", "new_v"):
        for name in order:
            outs.append(big_out[kind][name] if name in big_out[kind] else small_out[kind][name])
    return tuple(outs)
```

```python
import math

import numpy as np
import jax
import jax.numpy as jnp
from jax import lax
from jax.experimental import pallas as pl
from jax.experimental.pallas import tpu as pltpu

F32 = jnp.float32
BF16 = jnp.bfloat16

D_MODEL = 1024
HEAD_DIM = 64
ATTN_W = 512
REC_W = 512
N_HEADS = 8
D_FF = 3072
IN_W = 2560
REC_CONV = 4
FFN_CONV = 3
LRU_C = 8.0
ROPE_THETA = 10000.0
EPS = 1e-6
NEG_INF = -1e30
QBLK = 128
DILATIONS = (1, 4, 16)
N_DEV = 8
SMALL_ROWS = 128
ADAM_LR, ADAM_B1, ADAM_B2, ADAM_EPS, ADAM_WD, ADAM_STEP = 0.001, 0.9, 0.999, 1e-08, 0.01, 10
MESH = pl.DeviceIdType.MESH
ANY = pl.BlockSpec(memory_space=pl.ANY)


def _call(body, *, name, **kw):
    return pl.pallas_call(body, name=name, **kw)


def _params(*sem):
    return pltpu.CompilerParams(dimension_semantics=sem, vmem_limit_bytes=56 * 1024 * 1024)


_GELU_C = math.sqrt(2.0 / math.pi)
_GELU_A = 0.044715


def _gelu(x):
    return (0.5 * x) * (1.0 + jnp.tanh(x * (_GELU_C + (_GELU_C * _GELU_A) * (x * x))))


def _gelu_and_grad(x):
    x2 = x * x
    u = 1.0 + jnp.tanh(x * (_GELU_C + (_GELU_C * _GELU_A) * x2))
    hx = 0.5 * x
    return hx * u, 0.5 * u + (hx * ((2.0 - u) * u)) * (_GELU_C + (3.0 * _GELU_C * _GELU_A) * x2)


def _sigmoid(x):
    return 1.0 / (1.0 + jnp.exp(-x))


def _softplus_neg(lam):
    y = jnp.exp(-jnp.abs(lam))
    u = 1.0 + y
    log1p = jnp.where(u == 1.0, y, jnp.log(u) * y / jnp.where(u == 1.0, 1.0, u - 1.0))
    return jnp.maximum(-lam, 0.0) + log1p


_NN = (((1,), (0,)), ((), ()))
_NT = (((1,), (1,)), ((), ()))
_TN = (((0,), (0,)), ((), ()))


def _dot(a, b, dims=_NN):
    return lax.dot_general(a, b, dims, preferred_element_type=F32)


def _group_mean(v, bd):
    hi = v.astype(BF16)
    lo = (v - hi.astype(F32)).astype(BF16)
    w = bd.shape[0]
    return jnp.concatenate([_dot(hi[:, c:c + w], bd) + _dot(lo[:, c:c + w], bd) for c in range(0, v.shape[1], w)], axis=1)


def _rope_tables(pos_ref, invf_ref):
    ang = pos_ref[...].astype(F32) * invf_ref[:, :2 * HEAD_DIM]
    reps = invf_ref.shape[1] // (2 * HEAD_DIM)
    return jnp.tile(jnp.cos(ang), (1, reps)), jnp.tile(jnp.sin(ang), (1, reps))


def _shift_down(x, halo, s):
    rolled = pltpu.roll(x, s, 0)
    hr = pltpu.roll(halo, s, 0)
    row = lax.broadcasted_iota(jnp.int32, hr.shape, 0)
    first = jnp.where(row < s, hr, rolled[:8])
    return jnp.concatenate([first, rolled[8:]], axis=0)


def _shift_up(x, halo, s):
    n = x.shape[0]
    rolled = pltpu.roll(x, n - s, 0)
    hr = pltpu.roll(halo, 8 - s, 0)
    row = lax.broadcasted_iota(jnp.int32, hr.shape, 0)
    last = jnp.where(row >= 8 - s, hr, rolled[n - 8:])
    return jnp.concatenate([rolled[:n - 8], last], axis=0)


def _scan_fwd(a, u):
    n, w = a.shape
    a3, u3 = a.reshape(n // 8, 8, w), u.reshape(n // 8, 8, w)
    row = lax.broadcasted_iota(jnp.int32, a3.shape, 1)
    for s in (1, 2, 4):
        a_s = jnp.where(row < s, 1.0, pltpu.roll(a3, s, 1))
        u_s = jnp.where(row < s, 0.0, pltpu.roll(u3, s, 1))
        u3 = u3 + a3 * u_s
        a3 = a3 * a_s
    ps, hs = [a3[0]], [u3[0]]
    for k in range(1, n // 8):
        ps.append(a3[k] * ps[-1][7:8, :])
        hs.append(u3[k] + a3[k] * hs[-1][7:8, :])
    return jnp.concatenate(ps, axis=0), jnp.concatenate(hs, axis=0)


def _scan_bwd(b, v):
    n, w = b.shape
    b3, v3 = b.reshape(n // 8, 8, w), v.reshape(n // 8, 8, w)
    row = lax.broadcasted_iota(jnp.int32, b3.shape, 1)
    for s in (1, 2, 4):
        b_s = jnp.where(row >= 8 - s, 1.0, pltpu.roll(b3, 8 - s, 1))
        v_s = jnp.where(row >= 8 - s, 0.0, pltpu.roll(v3, 8 - s, 1))
        v3 = v3 + b3 * v_s
        b3 = b3 * b_s
    last = n // 8 - 1
    ps, gs = [b3[last]], [v3[last]]
    for k in range(last - 1, -1, -1):
        ps.append(b3[k] * ps[-1][0:1, :])
        gs.append(v3[k] + b3[k] * gs[-1][0:1, :])
    return jnp.concatenate(ps[::-1], axis=0), jnp.concatenate(gs[::-1], axis=0)


def _rot_half(y):
    n = y.shape[1]
    lane = lax.broadcasted_iota(jnp.int32, y.shape, 1) & (HEAD_DIM - 1)
    return jnp.where(lane < HEAD_DIM // 2, -pltpu.roll(y, n - HEAD_DIM // 2, 1), pltpu.roll(y, HEAD_DIM // 2, 1))


def _row_tile(r, cap=256):
    return max(t for t in range(16, cap + 1, 16) if r % t == 0)


def _all_gather(shards, name):
    na = len(shards)
    ms = [s.shape[0] for s in shards]

    def body(*refs):
        x_refs, out_refs = refs[:na], refs[na:2 * na]
        send_sems, recv_sems, local_sems = refs[2 * na:]
        x, y, c = lax.axis_index("x"), lax.axis_index("y"), lax.axis_index("c")
        me, sibling = (x, y, c), (x, y, 1 - c)
        chips = [(1 - x, y), (x, 1 - y), (1 - x, 1 - y)]

        def rows(a, px, py, pc):
            return out_refs[a].at[pl.ds((4 * px + 2 * py + pc) * ms[a], ms[a]), :]

        def copy(a, k, block, to, src=None):
            return pltpu.make_async_remote_copy(
                src_ref=rows(a, *block) if src is None else src, dst_ref=rows(a, *block),
                send_sem=send_sems.at[7 * a + k], recv_sem=recv_sems.at[7 * a + k], device_id=to, device_id_type=MESH)

        mine = [pltpu.make_async_copy(x_refs[a], rows(a, *me), local_sems.at[a]) for a in range(na)]
        first = []
        for a in range(na):
            mine[a].start()
            first.append(copy(a, 0, me, sibling, src=x_refs[a]))
            first += [copy(a, 1 + j, me, (*chip, c), src=x_refs[a]) for j, chip in enumerate(chips)]
        for cp in first:
            cp.start()
        passed = []
        for a in range(na):
            for j, chip in enumerate(chips):
                copy(a, 1 + j, (*chip, c), me).wait_recv()
                fw = copy(a, 4 + j, (*chip, c), sibling)
                fw.start()
                passed.append(fw)
        for a in range(na):
            copy(a, 0, sibling, me).wait_recv()
            for j, chip in enumerate(chips):
                copy(a, 4 + j, (*chip, 1 - c), me).wait_recv()
        for cp in first + passed:
            cp.wait_send()
        for cp in mine:
            cp.wait()

    return _call(
        body, name=name, out_shape=[jax.ShapeDtypeStruct((N_DEV * s.shape[0], s.shape[1]), s.dtype) for s in shards],
        in_specs=[ANY] * na, out_specs=[ANY] * na,
        scratch_shapes=[pltpu.SemaphoreType.DMA((7 * na,)), pltpu.SemaphoreType.DMA((7 * na,)),
                        pltpu.SemaphoreType.DMA((na,))],
    )(*shards)


HBM = pl.BlockSpec(memory_space=pltpu.HBM)
SEM = pl.BlockSpec(memory_space=pltpu.SEMAPHORE)
EFFECT = pltpu.SideEffectType.DATAFLOW_SIDE_EFFECTING
N_PEERS = N_DEV - 1


def _peer(k):
    x, y, c = lax.axis_index("x"), lax.axis_index("y"), lax.axis_index("c")
    b = k + 1
    flip = lambda v, bit: 1 - v if bit else v
    return flip(x, b & 4), flip(y, b & 2), flip(c, b & 1)


def _in_hbm(a):
    return pltpu.with_memory_space_constraint(a, pltpu.HBM)


def _split_copy_descr(na, kind, src_refs, land_refs, send_sems, recv_sems):
    x, y, c = lax.axis_index("x"), lax.axis_index("y"), lax.axis_index("c")
    me = 4 * x + 2 * y + c
    copies = []
    for a in range(na):
        for k in range(N_PEERS):
            px, py, pc = _peer(k)
            if kind == "gather":
                m = src_refs[a].shape[0]
                src, dst = src_refs[a], land_refs[a].at[pl.ds(me * m, m), :]
            else:
                src, dst = src_refs[a].at[4 * px + 2 * py + pc], land_refs[a].at[k]
            copies.append(pltpu.make_async_remote_copy(
                src_ref=src, dst_ref=dst, send_sem=send_sems.at[N_PEERS * a + k], recv_sem=recv_sems.at[N_PEERS * a + k],
                device_id=(px, py, pc), device_id_type=MESH))
    return copies


def _landing(shape, dtype, own=None, at=None):
    buf = lax.empty(shape, dtype)
    return buf if own is None else lax.dynamic_update_slice(buf, own, (at, 0))


def _exchange_start(srcs, lands, kind, after, name):
    na = len(srcs)
    land_shapes = [l.shape for l in lands]

    def body(*refs):
        src_refs, land_refs = refs[:na], refs[na:2 * na]
        send_sems, recv_sems = refs[2 * na + 1], refs[2 * na + 2]
        token = refs[-1]
        for cp in _split_copy_descr(na, kind, src_refs, land_refs, send_sems, recv_sems):
            cp.start()
        token[...] = jnp.zeros_like(token)

    lands = [_in_hbm(l) for l in lands]
    sem = pltpu.SemaphoreType.DMA((N_PEERS * na,))
    outs = _call(
        body, name=name,
        out_shape=[sem, sem] + [pltpu.HBM(s.shape, s.dtype) for s in srcs] + [pltpu.HBM(s, srcs[0].dtype) for s in land_shapes]
        + [jax.ShapeDtypeStruct((8, 128), F32)],
        in_specs=[HBM] * (2 * na) + [ANY], out_specs=[SEM, SEM] + [HBM] * (2 * na) + [pl.BlockSpec(memory_space=pltpu.VMEM)],
        input_output_aliases={i: 2 + i for i in range(2 * na)},
        compiler_params=pltpu.CompilerParams(has_side_effects=EFFECT),
    )(*[_in_hbm(s) for s in srcs], *lands, after)
    return outs[0], outs[1], outs[2:2 + na], outs[2 + na:2 + 2 * na], outs[-1]


def _exchange_wait(send_sems, recv_sems, srcs, lands, kind, after, name):
    na = len(srcs)

    def body(*refs):
        src_refs, land_refs = refs[:na], refs[na:2 * na]
        s_sems, r_sems = refs[2 * na], refs[2 * na + 1]
        for cp in _split_copy_descr(na, kind, src_refs, land_refs, s_sems, r_sems):
            cp.wait_send()
            cp.wait_recv()

    outs = _call(
        body, name=name, out_shape=[pltpu.HBM(s.shape, s.dtype) for s in srcs] + [pltpu.HBM(l.shape, l.dtype) for l in lands],
        in_specs=[HBM] * (2 * na) + [SEM, SEM, ANY], out_specs=[HBM] * (2 * na),
        input_output_aliases={i: i for i in range(2 * na)},
        compiler_params=pltpu.CompilerParams(has_side_effects=EFFECT),
    )(*srcs, *lands, send_sems, recv_sems, after)
    return outs[:na], outs[na:]


def _mm(a, b, mode, out_dtype, name, add=None, tm=1024, tn=1024, tk=1024, b_noff=0, b_koff=0,
        n=None, k=None, into=None, o_rows=None, o_moff=0, loss_target=None):
    if mode == "tn":
        K, M = a.shape
    else:
        M, K = a.shape
    N = n if n is not None else (b.shape[0] if mode == "nt" else b.shape[1])
    if k is not None:
        assert k == K
    tm, tn, tk = min(tm, M), min(tn, N), min(tk, K)
    assert M % tm == 0 and N % tn == 0 and K % tk == 0, (name, M, N, K)
    nk = K // tk
    if mode == "nn":
        a_spec = pl.BlockSpec((tm, tk), lambda i, j, kk: (i, kk))
        b_spec, dims = pl.BlockSpec((tk, tn), lambda i, j, kk: (kk + b_koff, j + b_noff)), _NN
    elif mode == "nt":
        a_spec = pl.BlockSpec((tm, tk), lambda i, j, kk: (i, kk))
        b_spec, dims = pl.BlockSpec((tn, tk), lambda i, j, kk: (j + b_noff, kk + b_koff)), _NT
    else:
        a_spec = pl.BlockSpec((tk, tm), lambda i, j, kk: (kk, i))
        b_spec, dims = pl.BlockSpec((tk, tn), lambda i, j, kk: (kk + b_koff, j + b_noff)), _TN
    o_spec = pl.BlockSpec((tm, tn), lambda i, j, kk: (i + o_moff, j))
    has_add, has_into, has_loss = add is not None, into is not None, loss_target is not None
    assert not has_loss or (has_add and tn == N and not has_into)
    n_in = 2 + has_add + has_loss + has_into

    def body(*refs):
        a_ref, b_ref = refs[0], refs[1]
        add_ref = refs[2] if has_add else None
        outs = refs[n_in:]

        def finish(r):
            if has_add:
                r = r + add_ref[...]
            if has_loss:
                e = r - refs[3][...]
                dy = e * (1.0 / N)
                outs[0][...] = dy
                outs[1][...] = dy.astype(BF16)
                outs[2][...] = jnp.sum(e * e, axis=0, keepdims=True)[None]
            else:
                outs[0][...] = r.astype(out_dtype)

        if nk == 1:
            finish(_dot(a_ref[...], b_ref[...], dims))
        else:
            acc = refs[-1]
            kk = pl.program_id(2)

            @pl.when(kk == 0)
            def _():
                acc[...] = _dot(a_ref[...], b_ref[...], dims)

            @pl.when((kk > 0) & (kk < nk - 1))
            def _():
                acc[...] += _dot(a_ref[...], b_ref[...], dims)

            @pl.when(kk == nk - 1)
            def _():
                finish(acc[...] + _dot(a_ref[...], b_ref[...], dims))

    tile = pl.BlockSpec((tm, tn), lambda i, j, kk: (i, j))
    ins = [a, b] + ([add] if has_add else []) + ([loss_target] if has_loss else []) + ([into] if has_into else [])
    specs = [a_spec, b_spec] + [tile] * (has_add + has_loss) + ([ANY] if has_into else [])
    rows = into.shape[0] if has_into else (o_rows if o_rows is not None else M)
    if has_loss:
        out_specs = [tile, tile, pl.BlockSpec((1, 1, N), lambda i, j, kk: (i, 0, 0))]
        out_shape = [jax.ShapeDtypeStruct((M, N), F32), jax.ShapeDtypeStruct((M, N), BF16), jax.ShapeDtypeStruct((M // tm, 1, N), F32)]
    else:
        out_specs, out_shape = o_spec, jax.ShapeDtypeStruct((rows, N), out_dtype)
    return _call(
        body, name=name, grid=(M // tm, N // tn, nk), in_specs=specs, out_specs=out_specs, out_shape=out_shape,
        scratch_shapes=[pltpu.VMEM((tm, tn), F32)] if nk > 1 else [],
        input_output_aliases={len(ins) - 1: 0} if has_into else {},
        compiler_params=_params("parallel", "parallel", "arbitrary"),
    )(*ins)


def _mm_norm_bwd(parts, b, x, resid, g, name, tm=512, tk=512):
    T, N = x.shape
    counts = [p.shape[1] // tk for p in parts]
    starts = [sum(counts[:i]) for i in range(len(parts))]
    nsteps = sum(counts)
    assert all(p.shape[1] % tk == 0 for p in parts) and b.shape == (nsteps * tk, N)
    npart = len(parts)

    def body(*refs):
        a_refs, b_ref, x_ref, res_ref, g_ref = refs[:npart], refs[npart], refs[npart + 1], refs[npart + 2], refs[npart + 3]
        dx_ref, dxb_ref, dg_ref, acc = refs[npart + 4:]
        i, s = pl.program_id(0), pl.program_id(1)

        @pl.when((i == 0) & (s == 0))
        def _():
            dg_ref[...] = jnp.zeros_like(dg_ref)

        for p in range(npart):
            @pl.when((s >= starts[p]) & (s < starts[p] + counts[p]))
            def _(p=p):
                d = _dot(a_refs[p][...], b_ref[...])

                @pl.when(s == 0)
                def _():
                    acc[...] = d

                @pl.when(s > 0)
                def _():
                    acc[...] += d

        @pl.when(s == nsteps - 1)
        def _():
            xv, dhv = x_ref[...], acc[...]
            r = lax.rsqrt(jnp.mean(xv * xv, axis=-1, keepdims=True) + EPS)
            gd = dhv * g_ref[...]
            m = jnp.mean(gd * xv, axis=-1, keepdims=True)
            dx = res_ref[...] + r * gd - xv * (r * r * r) * m
            dx_ref[...] = dx
            dxb_ref[...] = dx.astype(BF16)
            dg_ref[...] += jnp.sum(dhv * xv * r, axis=0, keepdims=True)

    a_specs = [pl.BlockSpec((tm, tk), lambda i, s, st=st, c=c: (i, jnp.clip(s - st, 0, c - 1))) for st, c in zip(starts, counts)]
    row = pl.BlockSpec((tm, N), lambda i, s: (i, 0))
    vec = pl.BlockSpec((1, N), lambda i, s: (0, 0))
    return _call(
        body, name=name, grid=(T // tm, nsteps),
        in_specs=a_specs + [pl.BlockSpec((tk, N), lambda i, s: (s, 0)), row, row, vec], out_specs=[row, row, vec],
        out_shape=[jax.ShapeDtypeStruct((T, N), F32), jax.ShapeDtypeStruct((T, N), BF16), jax.ShapeDtypeStruct((1, N), F32)],
        scratch_shapes=[pltpu.VMEM((tm, N), F32)], compiler_params=_params("arbitrary", "arbitrary"),
    )(*parts, b, x, resid, g)


def _norm_proj(x, g, wT, name, tm=1024, tn=1280):
    T, K = x.shape
    N = wT.shape[0]

    def body(x_ref, g_ref, w_ref, o_ref, h_ref):
        xv = x_ref[...]
        r = lax.rsqrt(jnp.mean(xv * xv, axis=-1, keepdims=True) + EPS)
        hv = (xv * r * g_ref[...]).astype(BF16)

        @pl.when(pl.program_id(1) == 0)
        def _():
            h_ref[...] = hv

        o_ref[...] = _dot(hv, w_ref[...], _NT)

    return _call(
        body, name=name, grid=(T // tm, N // tn),
        in_specs=[pl.BlockSpec((tm, K), lambda i, j: (i, 0)), pl.BlockSpec((1, K), lambda i, j: (0, 0)),
                  pl.BlockSpec((tn, K), lambda i, j: (j, 0))],
        out_specs=[pl.BlockSpec((tm, tn), lambda i, j: (i, j)), pl.BlockSpec((tm, K), lambda i, j: (i, 0))],
        out_shape=[jax.ShapeDtypeStruct((T, N), F32), jax.ShapeDtypeStruct((T, K), BF16)],
        compiler_params=_params("parallel", "arbitrary"),
    )(x, g, wT)


def _qk_prep(proj, pos, invf, qg, kg, bd, name, tm=512):
    T = proj.shape[0]

    def body(q_ref, k_ref, pos_ref, invf_ref, qg_ref, kg_ref, bd_ref, qo_ref, ko_ref):
        cos, sin = _rope_tables(pos_ref, invf_ref)

        def prep(xv, gv, scale):
            r = lax.rsqrt(_group_mean(xv * xv, bd_ref[...]) + EPS)
            yv = xv * r * gv
            return ((yv * cos + _rot_half(yv) * sin) * scale).astype(BF16).astype(F32)

        qo_ref[...] = prep(q_ref[...], qg_ref[...], HEAD_DIM ** -0.5)
        ko_ref[...] = prep(k_ref[...], kg_ref[...], 1.0)

    col = lambda j: pl.BlockSpec((tm, ATTN_W), lambda i, j=j: (i, j))
    vec = pl.BlockSpec((1, ATTN_W), lambda i: (0, 0))
    out = pl.BlockSpec((tm, ATTN_W), lambda i: (i, 0))
    return _call(
        body, name=name, grid=(T // tm,),
        in_specs=[col(0), col(1), pl.BlockSpec((tm, 1), lambda i: (i, 0)), vec, vec, vec,
                  pl.BlockSpec((2 * HEAD_DIM, 2 * HEAD_DIM), lambda i: (0, 0))],
        out_specs=[out, out], out_shape=[jax.ShapeDtypeStruct((T, ATTN_W), F32)] * 2,
        compiler_params=_params("parallel"),
    )(proj, proj, pos, invf, qg, kg, bd)


def _qk_prep_bwd(proj, dqh, dkh, dv, pos, invf, qg, kg, bd, name, tm=512):
    T = proj.shape[0]

    def body(q_ref, k_ref, dq_ref, dk_ref, dv_ref, pos_ref, invf_ref, qg_ref, kg_ref, bd_ref, o_ref, gq_ref, gk_ref):
        @pl.when(pl.program_id(0) == 0)
        def _():
            gq_ref[...] = jnp.zeros_like(gq_ref)
            gk_ref[...] = jnp.zeros_like(gk_ref)

        cos, sin = _rope_tables(pos_ref, invf_ref)

        def back(xv, gv, dz, scale):
            dz = dz * scale
            dy = dz * cos - _rot_half(dz * sin)
            r = lax.rsqrt(_group_mean(xv * xv, bd_ref[...]) + EPS)
            gd = dy * gv
            m = _group_mean(gd * xv, bd_ref[...])
            dx = r * gd - xv * (r * r * r) * m
            return dx, jnp.sum(dy * xv * r, axis=0, keepdims=True)

        dxq, gs = back(q_ref[...], qg_ref[...], dq_ref[...], HEAD_DIM ** -0.5)
        gq_ref[...] += gs
        dxk, gs = back(k_ref[...], kg_ref[...], dk_ref[...], 1.0)
        gk_ref[...] += gs
        o_ref[...] = jnp.concatenate([dxq.astype(BF16), dxk.astype(BF16), dv_ref[...].astype(BF16)], axis=1)

    col = lambda j: pl.BlockSpec((tm, ATTN_W), lambda i, j=j: (i, j))
    row = pl.BlockSpec((tm, ATTN_W), lambda i: (i, 0))
    vec = pl.BlockSpec((1, ATTN_W), lambda i: (0, 0))
    return _call(
        body, name=name, grid=(T // tm,),
        in_specs=[col(0), col(1), row, row, row, pl.BlockSpec((tm, 1), lambda i: (i, 0)), vec, vec, vec,
                  pl.BlockSpec((2 * HEAD_DIM, 2 * HEAD_DIM), lambda i: (0, 0))],
        out_specs=[pl.BlockSpec((tm, 3 * ATTN_W), lambda i: (i, 0)), vec, vec],
        out_shape=[jax.ShapeDtypeStruct((T, 3 * ATTN_W), BF16)] + [jax.ShapeDtypeStruct((1, ATTN_W), F32)] * 2,
        compiler_params=_params("arbitrary"),
    )(proj, proj, dqh, dkh, dv, pos, invf, qg, kg, bd)


def _ld(ref, start, size, dil):
    return ref[pl.ds(start, size), :] if dil == 1 else ref[pl.ds(start, size, stride=dil), :]


def _st(ref, start, size, dil, val):
    if dil == 1:
        ref[pl.ds(start, size), :] = val
    else:
        ref[pl.ds(start, size, stride=dil), :] = val


def _attn_geometry(T, dil):
    nb = T // dil // QBLK
    if nb == 2:
        return 1, 2 * QBLK, 2 * QBLK
    return nb, QBLK, (2 * QBLK if nb >= 2 else QBLK)


def _attn_unroll(qb):
    return 4


def _attn_unit(j, u, dil, nit, unroll):
    return unroll * j + u if dil >= unroll else j + u * (nit // unroll)


def _attn_block(it, dil, qb, kw):
    c, n = it & (dil - 1), lax.shift_right_logical(it, dil.bit_length() - 1)
    sq = n * (qb * dil) + c
    sk = jnp.maximum(n - (kw // qb - 1), 0) * (qb * dil) + c
    qi = lax.broadcasted_iota(jnp.int32, (2 * qb, kw), 0) & (qb - 1)
    kj = lax.broadcasted_iota(jnp.int32, (2 * qb, kw), 1)
    rel = jnp.where(n > 0, kw - qb, 0) + qi - kj
    return sq, sk, (rel >= 0) & (rel <= QBLK)


def _stack_heads(xv, head0):
    z = jnp.zeros_like(xv)
    return jnp.concatenate([jnp.where(head0, xv, z), jnp.where(head0, z, xv)], axis=0)


def _unstack_heads(x2, head0):
    qb = x2.shape[0] // 2
    return jnp.where(head0, x2[:qb], x2[qb:])


def _attn_fwd(qf, kf, proj, name):
    T = qf.shape[0]

    def body(q_ref, k_ref, v_ref, o_ref, lse_ref):
        for bi, dil in enumerate(DILATIONS):
            nb, qb, kw = _attn_geometry(T, dil)
            nit = nb * dil
            head0 = lax.broadcasted_iota(jnp.int32, (qb, 2 * HEAD_DIM), 1) < HEAD_DIM

            def step(j, carry, bi=bi, dil=dil, qb=qb, kw=kw, nit=nit, head0=head0):
                units = []
                for u in range(_attn_unroll(qb)):
                    sq, sk, ok = _attn_block(_attn_unit(j, u, dil, nit, _attn_unroll(qb)), dil, qb, kw)
                    old = (_ld(o_ref, sq, qb, dil), _ld(lse_ref, sq, qb, dil)) if bi > 0 else None
                    units.append((sq, ok, _ld(q_ref, sq, qb, dil).astype(BF16), _ld(k_ref, sk, kw, dil).astype(BF16),
                                  _ld(v_ref, sk, kw, dil).astype(BF16), old))
                results = []
                for sq, ok, qv, kv, vv, old in units:
                    s = jnp.where(ok, _dot(_stack_heads(qv, head0), kv, _NT), NEG_INF)
                    m = jnp.max(s, axis=-1, keepdims=True)
                    p = jnp.exp(s - m).astype(BF16)
                    acc = _dot(p, jnp.concatenate([vv, jnp.ones_like(vv)], axis=1))
                    l = acc[:, 2 * HEAD_DIM:]
                    o_new = _unstack_heads(acc[:, :2 * HEAD_DIM] / l, head0)
                    l_new = _unstack_heads(m + jnp.log(l), head0)
                    if bi > 0:
                        o_old, l_old = old
                        mx = jnp.maximum(l_old, l_new)
                        e0, e1 = jnp.exp(l_old - mx), jnp.exp(l_new - mx)
                        z = e0 + e1
                        o_new = (e0 * o_old + e1 * o_new) / z
                        l_new = mx + jnp.log(z)
                    results.append((sq, o_new, l_new))
                for sq, o_new, l_new in results:
                    _st(o_ref, sq, qb, dil, o_new)
                    _st(lse_ref, sq, qb, dil, l_new)
                return carry

            lax.fori_loop(0, nit // _attn_unroll(qb), step, 0)

    blk = lambda off: pl.BlockSpec((T, 2 * HEAD_DIM), lambda hp, off=off: (0, off + hp))
    return _call(
        body, name=name, grid=(4,), in_specs=[blk(0), blk(0), blk(8)], out_specs=[blk(0), blk(0)],
        out_shape=[jax.ShapeDtypeStruct((T, ATTN_W), F32)] * 2, compiler_params=_params("parallel"),
    )(qf, kf, proj)


def _attn_bwd(qf, kf, proj, do, lse, delta, name):
    T = qf.shape[0]

    def body(q_ref, k_ref, v_ref, do_ref, lse_ref, dl_ref, dq_ref, dk_ref, dv_ref):
        for ref in (dq_ref, dk_ref, dv_ref):
            ref[...] = jnp.zeros_like(ref)
        for dil in DILATIONS:
            nb, qb, kw = _attn_geometry(T, dil)
            nit = nb * dil
            head0 = lax.broadcasted_iota(jnp.int32, (qb, 2 * HEAD_DIM), 1) < HEAD_DIM

            def step(j, carry, dil=dil, qb=qb, kw=kw, nit=nit, head0=head0):
                units = []
                for u in range(_attn_unroll(qb)):
                    sq, sk, ok = _attn_block(_attn_unit(j, u, dil, nit, _attn_unroll(qb)), dil, qb, kw)
                    lsev, dlv = _ld(lse_ref, sq, qb, dil), _ld(dl_ref, sq, qb, dil)
                    units.append((sq, sk, ok, _ld(q_ref, sq, qb, dil).astype(BF16), _ld(do_ref, sq, qb, dil).astype(BF16),
                                  jnp.concatenate([lsev[:, 0:1], lsev[:, HEAD_DIM:HEAD_DIM + 1]], axis=0),
                                  jnp.concatenate([dlv[:, 0:1], dlv[:, HEAD_DIM:HEAD_DIM + 1]], axis=0),
                                  _ld(k_ref, sk, kw, dil).astype(BF16), _ld(v_ref, sk, kw, dil).astype(BF16),
                                  _ld(dq_ref, sq, qb, dil), _ld(dk_ref, sk, kw, dil), _ld(dv_ref, sk, kw, dil)))
                results = []
                for sq, sk, ok, qv, dov, lse2, dl2, kv, vv, dq0, dk0, dv0 in units:
                    q2, do2 = _stack_heads(qv, head0), _stack_heads(dov, head0)
                    p = jnp.where(ok, jnp.exp(_dot(q2, kv, _NT) - lse2), 0.0)
                    ds = (p * (_dot(do2, vv, _NT) - dl2)).astype(BF16)
                    results.append((sq, sk, dq0 + _unstack_heads(_dot(ds, kv), head0),
                                    dk0 + _dot(ds, q2, _TN), dv0 + _dot(p.astype(BF16), do2, _TN)))
                for sq, sk, dq, dk, dv in results:
                    _st(dq_ref, sq, qb, dil, dq)
                    _st(dk_ref, sk, kw, dil, dk)
                    _st(dv_ref, sk, kw, dil, dv)
                return carry

            lax.fori_loop(0, nit // _attn_unroll(qb), step, 0)

    blk = lambda off: pl.BlockSpec((T, 2 * HEAD_DIM), lambda hp, off=off: (0, off + hp))
    return _call(
        body, name=name, grid=(4,), in_specs=[blk(0), blk(0), blk(8), blk(0), blk(0), blk(0)], out_specs=[blk(0)] * 3,
        out_shape=[jax.ShapeDtypeStruct((T, ATTN_W), F32)] * 3, compiler_params=_params("parallel"),
    )(qf, kf, proj, do, lse, delta)


def _attn_norm(attn, g, name, tm=512):
    T = attn.shape[0]

    def body(a_ref, g_ref, o_ref):
        av = a_ref[...]
        r = lax.rsqrt(jnp.mean(av * av, axis=-1, keepdims=True) + EPS)
        o_ref[...] = (av * r * g_ref[...]).astype(BF16)

    row = pl.BlockSpec((tm, ATTN_W), lambda i: (i, 0))
    return _call(
        body, name=name, grid=(T // tm,), in_specs=[row, pl.BlockSpec((1, ATTN_W), lambda i: (0, 0))], out_specs=row,
        out_shape=jax.ShapeDtypeStruct((T, 2 * ATTN_W), BF16), compiler_params=_params("parallel"),
    )(attn, g)


def _attn_norm_bwd(dmix, attn, g, bd, name, tm=512):
    T = attn.shape[0]

    def body(d_ref, a_ref, g_ref, bd_ref, do_ref, dl_ref, dg_ref):
        @pl.when(pl.program_id(0) == 0)
        def _():
            dg_ref[...] = jnp.zeros_like(dg_ref)

        dy, av = d_ref[...], a_ref[...]
        r = lax.rsqrt(jnp.mean(av * av, axis=-1, keepdims=True) + EPS)
        gd = dy * g_ref[...]
        m = jnp.mean(gd * av, axis=-1, keepdims=True)
        da = r * gd - av * (r * r * r) * m
        do_ref[...] = da
        dl_ref[...] = _group_mean(da * av, bd_ref[...]) * float(HEAD_DIM)
        dg_ref[...] += jnp.sum(dy * av * r, axis=0, keepdims=True)

    row = pl.BlockSpec((tm, ATTN_W), lambda i: (i, 0))
    vec = pl.BlockSpec((1, ATTN_W), lambda i: (0, 0))
    return _call(
        body, name=name, grid=(T // tm,),
        in_specs=[row, row, vec, pl.BlockSpec((2 * HEAD_DIM, 2 * HEAD_DIM), lambda i: (0, 0))], out_specs=[row, row, vec],
        out_shape=[jax.ShapeDtypeStruct((T, ATTN_W), F32)] * 2 + [jax.ShapeDtypeStruct((1, ATTN_W), F32)],
        compiler_params=_params("arbitrary"),
    )(dmix, attn, g, bd)


def _rec_gates(xc, wrg_ref, wig_ref, brg_ref, big_ref, lam_ref):
    xb = xc.astype(BF16)
    r = _sigmoid(_dot(xb, wrg_ref[...]) + brg_ref[...])
    ig = _sigmoid(_dot(xb, wig_ref[...]) + big_ref[...])
    sp = _softplus_neg(lam_ref[...])
    log_a = -LRU_C * r * sp
    a = jnp.exp(log_a)
    th = jnp.tanh(log_a)
    mult = jnp.sqrt(-2.0 * th / (1.0 - th))
    return xb, r, ig, sp, a, mult


def _rec_fwd(proj, mix, cw, cb, wrg, wig, brg, big, lam, g, name, tm=256):
    T = proj.shape[0]
    hb = tm // 8

    def body(xr_ref, halo_ref, gr_ref, cw_ref, cb_ref, wrg_ref, wig_ref, brg_ref, big_ref, lam_ref, g_ref, mix_ref,
             xc_ref, h_ref, out_ref, carry):
        i = pl.program_id(0)

        @pl.when(i == 0)
        def _():
            carry[...] = jnp.zeros_like(carry)

        xr = xr_ref[...]
        halo = jnp.where(i > 0, halo_ref[...], 0.0)
        xc = cb_ref[...] + cw_ref[3:4, :] * xr
        for s in range(1, REC_CONV):
            xc = xc + cw_ref[3 - s:4 - s, :] * _shift_down(xr, halo, s)
        xc_ref[...] = xc
        _, _, ig, _, a, mult = _rec_gates(xc, wrg_ref, wig_ref, brg_ref, big_ref, lam_ref)
        pa, hl = _scan_fwd(a, mult * (ig * xc))
        h = hl + pa * carry[0:1, :]
        h_ref[...] = h
        carry[0:1, :] = h_ref[pl.ds(tm - 1, 1), :]
        hg = h * _gelu(gr_ref[...])
        r = lax.rsqrt(jnp.mean(hg * hg, axis=-1, keepdims=True) + EPS)
        out_ref[...] = (hg * r * g_ref[...]).astype(BF16)

    vec = pl.BlockSpec((1, REC_W), lambda i: (0, 0))
    row = pl.BlockSpec((tm, REC_W), lambda i: (i, 0))
    mat = pl.BlockSpec((REC_W, REC_W), lambda i: (0, 0))
    return _call(
        body, name=name, grid=(T // tm,),
        in_specs=[pl.BlockSpec((tm, REC_W), lambda i: (i, 3)),
                  pl.BlockSpec((8, REC_W), lambda i: (jnp.maximum(i * hb - 1, 0), 3)),
                  pl.BlockSpec((tm, REC_W), lambda i: (i, 4)),
                  pl.BlockSpec((8, REC_W), lambda i: (0, 0)), vec, mat, mat, vec, vec, vec, vec, ANY],
        out_specs=[row, row, pl.BlockSpec((tm, REC_W), lambda i: (i, 1))],
        out_shape=[jax.ShapeDtypeStruct((T, REC_W), F32)] * 2 + [jax.ShapeDtypeStruct(mix.shape, BF16)],
        scratch_shapes=[pltpu.VMEM((8, REC_W), F32)], input_output_aliases={11: 2},
        compiler_params=_params("arbitrary"),
    )(proj, proj, proj, cw, cb, wrg, wig, brg, big, lam, g, mix)


def _rec_bwd(dmix, proj, xc, h, cw, cb, wrg, wig, brg, big, lam, g, name, tm=256):
    T = proj.shape[0]
    nt = T // tm
    hb = tm // 8

    def body(d_ref, xr_ref, xhalo_ref, gr_ref, xc_ref, h_ref, hhalo_ref, cw_ref, cb_ref, wrg_ref, wig_ref, brg_ref,
             big_ref, lam_ref, g_ref,
             drec_ref, gcw_ref, gcb_ref, gwrg_ref, gwig_ref, gbrg_ref, gbig_ref, glam_ref, gg_ref,
             g_carry, a_first, dxc_next, gsp):
        i = pl.program_id(0)
        first_tile = i == nt - 1

        @pl.when(i == 0)
        def _():
            for ref in (gcw_ref, gcb_ref, gwrg_ref, gwig_ref, gbrg_ref, gbig_ref, glam_ref, gg_ref,
                        g_carry, a_first, dxc_next, gsp):
                ref[...] = jnp.zeros_like(ref)

        xr, xc, hv = xr_ref[...], xc_ref[...], h_ref[...]
        xhalo = jnp.where(first_tile, 0.0, xhalo_ref[...])
        hhalo = jnp.where(first_tile, 0.0, hhalo_ref[...])
        xb, r, ig, sp, a, mult = _rec_gates(xc, wrg_ref, wig_ref, brg_ref, big_ref, lam_ref)
        h_prev = _shift_down(hv, hhalo, 1)
        ge, dge = _gelu_and_grad(gr_ref[...])
        hg = hv * ge
        rr = lax.rsqrt(jnp.mean(hg * hg, axis=-1, keepdims=True) + EPS)
        dy = d_ref[...]
        gd = dy * g_ref[...]
        dhg = rr * gd - hg * (rr * rr * rr) * jnp.mean(gd * hg, axis=-1, keepdims=True)
        gg_ref[...] += jnp.sum(dy * hg * rr, axis=0, keepdims=True)
        dgr = (dhg * hv * dge).astype(BF16)
        dh = dhg * ge
        b = _shift_up(a, jnp.broadcast_to(a_first[0:1, :], (8, REC_W)), 1)
        pb, gl = _scan_bwd(b, dh)
        gs = gl + pb * g_carry[0:1, :]
        g_carry[0:1, :] = gs[0:1, :]
        a_first[0:1, :] = a[0:1, :]
        da = gs * h_prev
        dmult = gs * (ig * xc)
        di = gs * (mult * xc)
        dxc = gs * (mult * ig)
        dlog_a = da * a - dmult * (a * a) / mult
        gsp[...] += jnp.sum(dlog_a * (-LRU_C * r), axis=0, keepdims=True)
        dzr = (dlog_a * (-LRU_C * sp)) * (r * (1.0 - r))
        dzi = di * (ig * (1.0 - ig))
        dzr_b, dzi_b = dzr.astype(BF16), dzi.astype(BF16)
        dxc = dxc + _dot(dzr_b, wrg_ref[...], _NT) + _dot(dzi_b, wig_ref[...], _NT)
        gwrg_ref[...] += _dot(xb, dzr_b, _TN)
        gwig_ref[...] += _dot(xb, dzi_b, _TN)
        gbrg_ref[...] += jnp.sum(dzr, axis=0, keepdims=True)
        gbig_ref[...] += jnp.sum(dzi, axis=0, keepdims=True)
        nxt = dxc_next[...]
        dxr = cw_ref[3:4, :] * dxc
        gcw_ref[3:4, :] += jnp.sum(dxc * xr, axis=0, keepdims=True)
        for s in range(1, REC_CONV):
            dxr = dxr + cw_ref[3 - s:4 - s, :] * _shift_up(dxc, nxt, s)
            gcw_ref[3 - s:4 - s, :] += jnp.sum(dxc * _shift_down(xr, xhalo, s), axis=0, keepdims=True)
        gcb_ref[...] += jnp.sum(dxc, axis=0, keepdims=True)
        dxc_next[...] = dxc[:8]
        drec_ref[...] = jnp.concatenate([dxr.astype(BF16), dgr], axis=1)

        @pl.when(first_tile)
        def _():
            glam_ref[...] = gsp[...] * (-_sigmoid(-lam_ref[...]))

    rev = lambda i: nt - 1 - i
    vec = pl.BlockSpec((1, REC_W), lambda i: (0, 0))
    row = pl.BlockSpec((tm, REC_W), lambda i: (rev(i), 0))
    mat = pl.BlockSpec((REC_W, REC_W), lambda i: (0, 0))
    cwb = pl.BlockSpec((8, REC_W), lambda i: (0, 0))
    halo = lambda c: pl.BlockSpec((8, REC_W), lambda i, c=c: (jnp.maximum(rev(i) * hb - 1, 0), c))
    return _call(
        body, name=name, grid=(nt,),
        in_specs=[pl.BlockSpec((tm, REC_W), lambda i: (rev(i), 1)),
                  pl.BlockSpec((tm, REC_W), lambda i: (rev(i), 3)), halo(3),
                  pl.BlockSpec((tm, REC_W), lambda i: (rev(i), 4)),
                  row, row, halo(0), cwb, vec, mat, mat, vec, vec, vec, vec],
        out_specs=[pl.BlockSpec((tm, 2 * REC_W), lambda i: (rev(i), 0)), cwb, vec, mat, mat, vec, vec, vec, vec],
        out_shape=[jax.ShapeDtypeStruct((T, 2 * REC_W), BF16)]
        + [jax.ShapeDtypeStruct((8, REC_W), F32), jax.ShapeDtypeStruct((1, REC_W), F32)]
        + [jax.ShapeDtypeStruct((REC_W, REC_W), F32)] * 2 + [jax.ShapeDtypeStruct((1, REC_W), F32)] * 4,
        scratch_shapes=[pltpu.VMEM((8, REC_W), F32)] * 3 + [pltpu.VMEM((1, REC_W), F32)],
        compiler_params=_params("arbitrary"),
    )(dmix, proj, proj, proj, xc, h, h, cw, cb, wrg, wig, brg, big, lam, g)


def _ffn_conv(x_ext, cw_ref, cb_ref):
    return (cb_ref[...] + cw_ref[2:3, :] * x_ext + cw_ref[1:2, :] * pltpu.roll(x_ext, 1, 0)
            + cw_ref[0:1, :] * pltpu.roll(x_ext, 2, 0))


def _up_proj_act(x2, g, w_upT, cw, cb, name, tm=1024, tc=768):
    T = x2.shape[0]
    nc = D_FF // tc

    def body(x_ref, g_ref, wg_ref, wu_ref, cwg_ref, cwu_ref, cbg_ref, cbu_ref, act_ref, da_ref, db_ref, pg_ref, pu_ref,
             h_ref, hist_g, hist_u, hs):
        i, j = pl.program_id(0), pl.program_id(1)

        @pl.when(j == 0)
        def _():
            xv = x_ref[...]
            r = lax.rsqrt(jnp.mean(xv * xv, axis=-1, keepdims=True) + EPS)
            hs[...] = (xv * r * g_ref[...]).astype(BF16)
            h_ref[...] = hs[...]

        hv = hs[...]
        pg, pu = _dot(hv, wg_ref[...], _NT), _dot(hv, wu_ref[...], _NT)
        ge = jnp.concatenate([jnp.where(i > 0, hist_g[j], 0.0), pg], axis=0)
        ue = jnp.concatenate([jnp.where(i > 0, hist_u[j], 0.0), pu], axis=0)
        gel, dgel = _gelu_and_grad(_ffn_conv(ge, cwg_ref, cbg_ref)[8:])
        uu = _ffn_conv(ue, cwu_ref, cbu_ref)[8:]
        act_ref[...] = (gel * uu).astype(BF16)
        da_ref[...] = (uu * dgel).astype(BF16)
        db_ref[...] = gel.astype(BF16)
        pg_ref[...] = pg.astype(BF16)
        pu_ref[...] = pu.astype(BF16)
        hist_g[j] = pg[tm - 8:]
        hist_u[j] = pu[tm - 8:]

    tile = pl.BlockSpec((tm, tc), lambda i, j: (i, j))
    wsp = lambda off: pl.BlockSpec((tc, D_MODEL), lambda i, j, off=off: (j + off, 0))
    cws = lambda off: pl.BlockSpec((8, tc), lambda i, j, off=off: (0, j + off))
    cbs = lambda off: pl.BlockSpec((1, tc), lambda i, j, off=off: (0, j + off))
    return _call(
        body, name=name, grid=(T // tm, nc),
        in_specs=[pl.BlockSpec((tm, D_MODEL), lambda i, j: (i, 0)), pl.BlockSpec((1, D_MODEL), lambda i, j: (0, 0)),
                  wsp(0), wsp(nc), cws(0), cws(nc), cbs(0), cbs(nc)],
        out_specs=[tile] * 5 + [pl.BlockSpec((tm, D_MODEL), lambda i, j: (i, 0))],
        out_shape=[jax.ShapeDtypeStruct((T, D_FF), BF16)] * 5 + [jax.ShapeDtypeStruct((T, D_MODEL), BF16)],
        scratch_shapes=[pltpu.VMEM((nc, 8, tc), F32)] * 2 + [pltpu.VMEM((tm, D_MODEL), BF16)],
        compiler_params=_params("arbitrary", "arbitrary"),
    )(x2, g, w_upT, w_upT, cw, cw, cb, cb)


def _ffn_bwd(dyb, w_down, da, db, pg, pu, cw, name, tm=512, tc=768):
    T, F = pg.shape
    nt = T // tm
    hb16 = tm // 16
    nc = F // tc
    n = tm + 8

    def body(dy_ref, dyn_ref, wd_ref, a_ref, an_ref, b_ref, bn_ref, g_ref, u_ref, cwg_ref, cwu_ref,
             dg_ref, du_ref, gcwg_ref, gcwu_ref, gcbg_ref, gcbu_ref):
        i = pl.program_id(1)
        last = i == nt - 1

        @pl.when(i == 0)
        def _():
            for ref in (gcwg_ref, gcwu_ref, gcbg_ref, gcbu_ref):
                ref[...] = jnp.zeros_like(ref)

        wd = wd_ref[...]
        dact_next = jnp.where(last, 0.0, _dot(dyn_ref[...], wd, _NT)[:8])
        de = jnp.concatenate([_dot(dy_ref[...], wd, _NT), dact_next], axis=0)
        ext = lambda t, nx: jnp.concatenate([t[...].astype(F32), nx[...].astype(F32)[:8]], axis=0)
        for dcv, x_ref, cw_ref, dx_ref, gcw_ref, gcb_ref in ((de * ext(a_ref, an_ref), g_ref, cwg_ref, dg_ref, gcwg_ref, gcbg_ref),
                                                               (de * ext(b_ref, bn_ref), u_ref, cwu_ref, du_ref, gcwu_ref, gcbu_ref)):
            s1, s2 = pltpu.roll(dcv, n - 1, 0), pltpu.roll(dcv, n - 2, 0)
            dx_ref[...] = (cw_ref[2:3, :] * dcv + cw_ref[1:2, :] * s1 + cw_ref[0:1, :] * s2)[:tm].astype(BF16)
            xv = x_ref[...].astype(F32)
            gcw_ref[2:3, :] += jnp.sum(xv * dcv[:tm], axis=0, keepdims=True)
            gcw_ref[1:2, :] += jnp.sum(xv * s1[:tm], axis=0, keepdims=True)
            gcw_ref[0:1, :] += jnp.sum(xv * s2[:tm], axis=0, keepdims=True)
            gcb_ref[...] += jnp.sum(dcv[:tm], axis=0, keepdims=True)

    tile = pl.BlockSpec((tm, tc), lambda j, i: (i, j))
    nxt = pl.BlockSpec((16, tc), lambda j, i: (jnp.minimum((i + 1) * hb16, nt * hb16 - 1), j))
    cws = lambda off: pl.BlockSpec((8, tc), lambda j, i, off=off: (0, j + off))
    cbs = pl.BlockSpec((1, tc), lambda j, i: (0, j))
    return _call(
        body, name=name, grid=(nc, nt),
        in_specs=[pl.BlockSpec((tm, D_MODEL), lambda j, i: (i, 0)),
                  pl.BlockSpec((16, D_MODEL), lambda j, i: (jnp.minimum((i + 1) * hb16, nt * hb16 - 1), 0)),
                  pl.BlockSpec((tc, D_MODEL), lambda j, i: (j, 0)), tile, nxt, tile, nxt, tile, tile, cws(0), cws(nc)],
        out_specs=[tile, tile, cws(0), cws(0), cbs, cbs],
        out_shape=[jax.ShapeDtypeStruct((T, F), BF16)] * 2 + [jax.ShapeDtypeStruct((8, F), F32)] * 2
        + [jax.ShapeDtypeStruct((1, F), F32)] * 2,
        compiler_params=_params("parallel", "arbitrary"),
    )(dyb, dyb, w_down, da, da, db, db, pg, pu, cw, cw)


def _adam_update(w, g, m, v):
    m2 = ADAM_B1 * m + (1.0 - ADAM_B1) * g
    v2 = ADAM_B2 * v + (1.0 - ADAM_B2) * (g * g)
    m_hat = m2 / (1.0 - ADAM_B1 ** ADAM_STEP)
    v_hat = v2 / (1.0 - ADAM_B2 ** ADAM_STEP)
    delta = -ADAM_LR * (m_hat / (jnp.sqrt(v_hat) + ADAM_EPS) + ADAM_WD * w)
    return delta, m2, v2


def _adam_sharded(p, r2, idx, w, m, v, name, transposed=False):
    r, n = p.shape[1:]
    nrecv = r2.shape[0]
    tr = (256 if r % 256 == 0 else r) if transposed else _row_tile(r)

    def body(c_ref, p_ref, r_ref, w_ref, m_ref, v_ref, g_ref, d_ref, m2_ref, v2_ref):
        g = p_ref[...].astype(F32)
        for k in range(nrecv):
            g = g + r_ref[k].astype(F32)
        if transposed:
            g = g.T
        g_ref[...] = g
        d_ref[...], m2_ref[...], v2_ref[...] = _adam_update(w_ref[...], g, m_ref[...], v_ref[...])

    blk = pl.BlockSpec((n, tr), lambda i, c_ref: (0, i)) if transposed else pl.BlockSpec((tr, n), lambda i, c_ref: (i, 0))
    spec = pltpu.PrefetchScalarGridSpec(
        num_scalar_prefetch=1, grid=(r // tr,),
        in_specs=[pl.BlockSpec((None, tr, n), lambda i, c_ref: (c_ref[0], i, 0)),
                  pl.BlockSpec((nrecv, tr, n), lambda i, c_ref: (0, i, 0)), blk, blk, blk],
        out_specs=[blk] * 4)
    return _call(body, name=name, grid_spec=spec, out_shape=[jax.ShapeDtypeStruct(w.shape, F32)] * 4,
                 compiler_params=_params("parallel"))(idx, p, r2, w, m, v)


def _sum_slabs(p, r2, idx, name):
    _, r, n = p.shape

    def body(c_ref, p_ref, r_ref, o_ref):
        acc = p_ref[...]
        for k in range(N_PEERS):
            acc = acc + r_ref[k]
        o_ref[...] = acc

    spec = pltpu.PrefetchScalarGridSpec(
        num_scalar_prefetch=1, grid=(1,),
        in_specs=[pl.BlockSpec((None, r, n), lambda i, c_ref: (c_ref[0], 0, 0)),
                  pl.BlockSpec((N_PEERS, r, n), lambda i, c_ref: (0, 0, 0))],
        out_specs=pl.BlockSpec((r, n), lambda i, c_ref: (0, 0)))
    return _call(body, name=name, grid_spec=spec, out_shape=jax.ShapeDtypeStruct((r, n), F32))(idx, p, r2)


def _adam_small(ws, gs, ms, vs, name):
    n = len(ws)

    def body(*refs):
        for i in range(n):
            d, m2, v2 = _adam_update(refs[i][...], refs[n + i][...], refs[2 * n + i][...], refs[3 * n + i][...])
            refs[4 * n + i][...] = d
            refs[5 * n + i][...] = m2
            refs[6 * n + i][...] = v2

    outs = _call(body, name=name, out_shape=[jax.ShapeDtypeStruct(w.shape, F32) for w in ws] * 3)(*ws, *gs, *ms, *vs)
    return outs[:n], outs[n:2 * n], outs[2 * n:]


def _pack_small_grads(full, halves, rcw, fcwg, fcwu, wrg, wig, lparts, name):
    nf, nh = len(full), len(halves)

    def body(*refs):
        o = refs[-1]
        o[...] = jnp.zeros_like(o)
        row = 0
        for r in refs[:nf]:
            for j in range(r.shape[1] // 1024):
                o[row:row + 1, :] = r[:, 1024 * j:1024 * (j + 1)]
                row += 1
        for k in range(0, nh, 2):
            o[row:row + 1, 0:512] = refs[nf + k][...]
            o[row:row + 1, 512:1024] = refs[nf + k + 1][...]
            row += 1
        rcw_ref, fg_ref, fu_ref, wrg_ref, wig_ref, l_ref = refs[nf + nh:nf + nh + 6]
        for k in range(2):
            o[row:row + 1, 0:512] = rcw_ref[2 * k:2 * k + 1, :]
            o[row:row + 1, 512:1024] = rcw_ref[2 * k + 1:2 * k + 2, :]
            row += 1
        for f_ref in (fg_ref, fu_ref):
            for k in range(FFN_CONV):
                for j in range(D_FF // 1024):
                    o[row:row + 1, :] = f_ref[k:k + 1, 1024 * j:1024 * (j + 1)]
                    row += 1
        assert row == 32
        for n in range(8):
            o[32:96, 64 * n:64 * n + 64] = wrg_ref[64 * n:64 * n + 64, 64 * n:64 * n + 64]
            o[32:96, 512 + 64 * n:512 + 64 * n + 64] = wig_ref[64 * n:64 * n + 64, 64 * n:64 * n + 64]
        o[96:97, :] = jnp.sum(l_ref[...], axis=0, keepdims=True)

    return _call(body, name=name, out_shape=jax.ShapeDtypeStruct((SMALL_ROWS, 1024), F32))(
        *full, *halves, rcw, fcwg, fcwu, wrg, wig, lparts)


def _block_diag(w):
    eye = jnp.eye(8, dtype=w.dtype)
    return (w[:, :, None, :] * eye[:, None, :, None]).reshape(512, 512)


def kernel(x, positions, g_mix, w_in, q_norm_g, k_norm_g, rec_conv_w, rec_conv_b, w_rg, b_rg, w_ig, b_ig, lru_lambda, g_attn_out, g_rec_out, w_out, g_ffn, w_up, ffn_conv_w, ffn_conv_b, w_down, loss_target, m_g_mix, m_w_in, m_q_norm_g, m_k_norm_g, m_rec_conv_w, m_rec_conv_b, m_w_rg, m_b_rg, m_w_ig, m_b_ig, m_lru_lambda, m_g_attn_out, m_g_rec_out, m_w_out, m_g_ffn, m_w_up, m_ffn_conv_w, m_ffn_conv_b, m_w_down, v_g_mix, v_w_in, v_q_norm_g, v_k_norm_g, v_rec_conv_w, v_rec_conv_b, v_w_rg, v_b_rg, v_w_ig, v_b_ig, v_lru_lambda, v_g_attn_out, v_g_rec_out, v_w_out, v_g_ffn, v_w_up, v_ffn_conv_w, v_ffn_conv_b, v_w_down):
    T = x.shape[1]
    ix, iy, ic = lax.axis_index("x"), lax.axis_index("y"), lax.axis_index("c")
    dev = 4 * ix + 2 * iy + ic
    xs = x.reshape(T, D_MODEL)
    tgt = loss_target.reshape(T, D_MODEL)
    pos = positions.reshape(T, 1)

    shards = {"w_in": (w_in[0], m_w_in[0], v_w_in[0]), "w_out": (w_out[0], m_w_out[0], v_w_out[0]),
              "w_up": (w_up[0], m_w_up[0], v_w_up[0]), "w_down": (w_down[0], m_w_down[0], v_w_down[0])}
    taps = jnp.concatenate([rec_conv_w.reshape(-1), ffn_conv_w.reshape(-1), jnp.zeros((4096 - 2560,), F32)]).reshape(8, 512)
    W_inT, taps_all = _all_gather([w_in[0].T.astype(BF16), taps], "ag_w_in")
    late = [w_out[0].astype(BF16), w_up[0].T.astype(BF16), w_down[0].astype(BF16)]
    ag_send, ag_recv, late_thru, land_thru, ag_token = _exchange_start(
        late, [_landing((N_DEV * s.shape[0], 1024), BF16, s, dev * s.shape[0]) for s in late], "gather", taps_all,
        "ag_late_start")
    taps_all = taps_all.reshape(N_DEV, 4096)
    rcw = taps_all[:, :256].reshape(8, 4, 64).transpose(1, 0, 2).reshape(4, REC_W)
    fcw = taps_all[:, 256:2560].reshape(8, 3, 768).transpose(1, 0, 2).reshape(3, 2 * D_FF)
    rcw8 = jnp.pad(rcw, ((0, 4), (0, 0)))
    fcw8 = jnp.pad(fcw, ((0, 5), (0, 0)))
    fcb = ffn_conv_b.reshape(1, 2 * D_FF)

    half = HEAD_DIM // 2
    inv_freq = ROPE_THETA ** (-jnp.arange(half, dtype=F32) / half)
    invf = jnp.tile(inv_freq, 2 * N_HEADS).reshape(1, ATTN_W)
    bd = jnp.asarray(np.kron(np.eye(2), np.full((HEAD_DIM, HEAD_DIM), 1.0 / HEAD_DIM)), BF16)
    qg = jnp.tile(q_norm_g.reshape(HEAD_DIM), N_HEADS).reshape(1, ATTN_W)
    kg = jnp.tile(k_norm_g.reshape(HEAD_DIM), N_HEADS).reshape(1, ATTN_W)
    wrg_bd = _block_diag(w_rg[0]).astype(BF16)
    wig_bd = _block_diag(w_ig[0]).astype(BF16)
    brg, big = b_rg.reshape(1, REC_W), b_ig.reshape(1, REC_W)

    proj, h1 = _norm_proj(xs, g_mix + ag_token[0, 0], W_inT, "in_proj")
    qf, kf = _qk_prep(proj, pos, invf, qg, kg, bd, "qk_prep")
    attn, lse = _attn_fwd(qf, kf, proj, "attn_fwd")
    mix = _attn_norm(attn, g_attn_out, "attn_norm")
    xc, hstate, mix = _rec_fwd(proj, mix, rcw8, rec_conv_b, wrg_bd, wig_bd, brg, big, lru_lambda, g_rec_out, "rec_fwd")
    _, (W_out, W_upT, W_down) = _exchange_wait(ag_send, ag_recv, late_thru, land_thru, "gather", hstate, "ag_late_wait")
    x2 = _mm(mix, W_out, "nn", F32, "out_proj", add=xs)

    act, da, db, pg, pu, h2 = _up_proj_act(x2, g_ffn, W_upT, fcw8, fcb, "up_proj_act")
    dy, dyb, lparts = _mm(act, W_down, "nn", F32, "down_proj_loss", add=x2, loss_target=tgt, tm=512, tk=D_FF)

    g_down = _mm(act, dyb, "tn", BF16, "g_w_down", tk=4096)
    dpg, dpu, g_fcwg, g_fcwu, g_fcbg, g_fcbu = _ffn_bwd(dyb, W_down, da, db, pg, pu, fcw8, "ffn_bwd")
    g_upT = _mm(dpg, h2, "tn", BF16, "g_w_up_gate", tk=4096, o_rows=2 * D_FF)
    g_upT = _mm(dpu, h2, "tn", BF16, "g_w_up_up", tk=4096, into=g_upT, o_moff=D_FF // 1024)
    ffn_g = [g_upT.reshape(N_DEV, 2 * D_FF // N_DEV, 1024), g_down.reshape(N_DEV, D_FF // N_DEV, 1024)]
    rs_send, rs_recv, ffn_g, ffn_land, rs_token = _exchange_start(
        ffn_g, [_landing((N_PEERS,) + g.shape[1:], BF16) for g in ffn_g], "scatter", dpu, "rs_ffn_start")
    dx2, dx2b, g_gffn = _mm_norm_bwd([dpg, dpu], W_upT, x2, dy, g_ffn + rs_token[0, 0], "d_h2_norm_bwd", tm=1024, tk=1536)

    dmix = _mm(dx2b, W_out, "nt", F32, "d_mix")
    g_out = _mm(mix, dx2b, "tn", BF16, "g_w_out", tk=4096).reshape(N_DEV, D_MODEL // N_DEV, 1024)
    out_send, out_recv, (g_out,), out_land, out_token = _exchange_start(
        [g_out], [_landing((N_PEERS,) + g_out.shape[1:], BF16)], "scatter", dmix, "rs_out_start")
    do, delta, g_gattn = _attn_norm_bwd(dmix, attn, g_attn_out + out_token[0, 0], bd, "attn_norm_bwd")
    dqh, dkh, dv = _attn_bwd(qf, kf, proj, do, lse, delta, "attn_bwd")
    dqkv, g_qg, g_kg = _qk_prep_bwd(proj, dqh, dkh, dv, pos, invf, qg, kg, bd, "qk_prep_bwd")
    (drec, g_rcw, g_rcb, g_wrg, g_wig, g_brg, g_big, g_lam, g_grec) = _rec_bwd(
        dmix, proj, xc, hstate, rcw8, rec_conv_b, wrg_bd, wig_bd, brg, big, lru_lambda, g_rec_out, "rec_bwd")
    g_inT = _mm(dqkv, h1, "tn", BF16, "g_w_in_qkv", tm=512, tk=4096, o_rows=IN_W)
    g_inT = _mm(drec, h1, "tn", BF16, "g_w_in_rec", tm=512, tk=4096, into=g_inT, o_moff=3 * ATTN_W // 512)
    g_inT = g_inT.reshape(N_DEV, IN_W // N_DEV, 1024)
    in_send, in_recv, (g_inT,), in_land, in_token = _exchange_start(
        [g_inT], [_landing((N_PEERS,) + g_inT.shape[1:], BF16)], "scatter", drec, "rs_in_start")
    grad_x, _, g_gmix = _mm_norm_bwd([dqkv, drec], W_inT, xs, dx2, g_mix + in_token[0, 0], "d_h1_norm_bwd", tm=1024, tk=512)

    flat = _pack_small_grads([g_gmix, g_gffn, g_fcbg, g_fcbu], [g_rcb, g_brg, g_big, g_lam, g_gattn, g_grec, g_qg, g_kg],
                             g_rcw, g_fcwg, g_fcwu, g_wrg, g_wig, lparts.reshape(-1, D_MODEL), "pack_small_grads")
    srows = SMALL_ROWS // N_DEV
    flat = flat.reshape(N_DEV, srows, 1024)
    sm_send, sm_recv, (flat,), sm_land, sm_token = _exchange_start(
        [flat], [_landing((N_PEERS, srows, 1024), F32)], "scatter", grad_x, "ar_small_rs_start")

    devi = jnp.reshape(dev, (1,)).astype(jnp.int32)
    ffn_g, ffn_land = _exchange_wait(rs_send, rs_recv, ffn_g, ffn_land, "scatter", sm_token, "rs_ffn_wait")
    (g_out,), out_land = _exchange_wait(out_send, out_recv, [g_out], out_land, "scatter", sm_token, "rs_out_wait")
    big_out = {"grad": {}, "delta": {}, "new_m": {}, "new_v": {}}

    def adam_big(nm, p, r):
        w_, m_, v_ = shards[nm]
        res = _adam_sharded(p, r, devi, w_, m_, v_, "adam_" + nm, transposed=nm in ("w_in", "w_up"))
        for kind, a in zip(("grad", "delta", "new_m", "new_v"), res):
            big_out[kind][nm] = a[None]
        return res[0]

    last = adam_big("w_up", ffn_g[0], ffn_land[0])
    (flat,), sm_land = _exchange_wait(sm_send, sm_recv, [flat], sm_land, "scatter", last, "ar_small_rs_wait")
    mine = _sum_slabs(flat, sm_land[0], devi, "sum_small_grads")
    sm_send, sm_recv, (mine,), sm_land, sm_token = _exchange_start(
        [mine], [_landing((SMALL_ROWS, 1024), F32, mine, dev * srows)], "gather", last, "ar_small_ag_start")
    adam_big("w_down", ffn_g[1], ffn_land[1])
    last = adam_big("w_out", g_out, out_land[0])
    _, (tot,) = _exchange_wait(sm_send, sm_recv, [mine], sm_land, "gather", last, "ar_small_ag_wait")
    (g_inT,), in_land = _exchange_wait(in_send, in_recv, [g_inT], in_land, "scatter", tot, "rs_in_wait")
    adam_big("w_in", g_inT, in_land[0])

    half = lambda r, h, shape: tot[r, 512 * h:512 * h + 512].reshape(shape)
    blocks = lambda h: tot[32:96, 512 * h:512 * h + 512].reshape(64, 8, 64).transpose(1, 0, 2)[None]
    fcw_full = jnp.concatenate([tot[14:23].reshape(1, 3, D_FF), tot[23:32].reshape(1, 3, D_FF)], axis=2)
    g_small = {
        "g_mix": tot[0:1], "g_ffn": tot[1:2], "ffn_conv_b": tot[2:8].reshape(1, 2 * D_FF),
        "rec_conv_b": half(8, 0, (1, 512)), "b_rg": half(8, 1, (1, 8, 64)), "b_ig": half(9, 0, (1, 8, 64)),
        "lru_lambda": half(9, 1, (1, 512)), "g_attn_out": half(10, 0, (1, 512)), "g_rec_out": half(10, 1, (1, 512)),
        "q_norm_g": half(11, 0, (N_HEADS, HEAD_DIM)).sum(0)[None], "k_norm_g": half(11, 1, (N_HEADS, HEAD_DIM)).sum(0)[None],
        "w_rg": blocks(0), "w_ig": blocks(1),
        "rec_conv_w": lax.dynamic_slice(tot[12:14].reshape(1, 4, REC_W), (0, 0, 64 * dev), (1, 4, 64)),
        "ffn_conv_w": lax.dynamic_slice(fcw_full, (0, 0, 768 * dev), (1, 3, 768))}
    loss = 0.5 / D_MODEL * jnp.sum(tot[96])
    given = dict(rec_conv_w=rec_conv_w, ffn_conv_w=ffn_conv_w,g_mix=g_mix, q_norm_g=q_norm_g, k_norm_g=k_norm_g, rec_conv_b=rec_conv_b, w_rg=w_rg, b_rg=b_rg, w_ig=w_ig,
                 b_ig=b_ig, lru_lambda=lru_lambda, g_attn_out=g_attn_out, g_rec_out=g_rec_out, g_ffn=g_ffn, ffn_conv_b=ffn_conv_b)
    given_m = dict(rec_conv_w=m_rec_conv_w, ffn_conv_w=m_ffn_conv_w, g_mix=m_g_mix, q_norm_g=m_q_norm_g, k_norm_g=m_k_norm_g, rec_conv_b=m_rec_conv_b, w_rg=m_w_rg, b_rg=m_b_rg,
                   w_ig=m_w_ig, b_ig=m_b_ig, lru_lambda=m_lru_lambda, g_attn_out=m_g_attn_out, g_rec_out=m_g_rec_out,
                   g_ffn=m_g_ffn, ffn_conv_b=m_ffn_conv_b)
    given_v = dict(rec_conv_w=v_rec_conv_w, ffn_conv_w=v_ffn_conv_w, g_mix=v_g_mix, q_norm_g=v_q_norm_g, k_norm_g=v_k_norm_g, rec_conv_b=v_rec_conv_b, w_rg=v_w_rg, b_rg=v_b_rg,
                   w_ig=v_w_ig, b_ig=v_b_ig, lru_lambda=v_lru_lambda, g_attn_out=v_g_attn_out, g_rec_out=v_g_rec_out,
                   g_ffn=v_g_ffn, ffn_conv_b=v_ffn_conv_b)
    small = sorted(given)
    ds, m2s, v2s = _adam_small([given[k] for k in small], [g_small[k] for k in small], [given_m[k] for k in small],
                               [given_v[k] for k in small], "adam_small")
    small_out = {"grad": g_small, "delta": dict(zip(small, ds)), "new_m": dict(zip(small, m2s)), "new_v": dict(zip(small, v2s))}

    order = ("g_mix", "w_in", "q_norm_g", "k_norm_g", "rec_conv_w", "rec_conv_b", "w_rg", "b_rg", "w_ig", "b_ig",
             "lru_lambda", "g_attn_out", "g_rec_out", "w_out", "g_ffn", "w_up", "ffn_conv_w", "ffn_conv_b", "w_down")
    outs = [loss, grad_x.reshape(1, T, D_MODEL)]
    for kind in ("grad", "delta", "new_m", "new_v"):
        for name in order:
            outs.append(big_out[kind][name] if name in big_out[kind] else small_out[kind][name])
    return tuple(outs)
```

```python
import math

import numpy as np
import jax
import jax.numpy as jnp
from jax import lax
from jax.experimental import pallas as pl
from jax.experimental.pallas import tpu as pltpu

F32 = jnp.float32
BF16 = jnp.bfloat16

D_MODEL = 1024
HEAD_DIM = 64
ATTN_W = 512
REC_W = 512
N_HEADS = 8
D_FF = 3072
IN_W = 2560
REC_CONV = 4
FFN_CONV = 3
LRU_C = 8.0
ROPE_THETA = 10000.0
EPS = 1e-6
NEG_INF = -1e30
QBLK = 128
DILATIONS = (1, 4, 16)
N_DEV = 8
SMALL_ROWS = 128
ADAM_LR, ADAM_B1, ADAM_B2, ADAM_EPS, ADAM_WD, ADAM_STEP = 0.001, 0.9, 0.999, 1e-08, 0.01, 10
MESH = pl.DeviceIdType.MESH
ANY = pl.BlockSpec(memory_space=pl.ANY)


def _call(body, *, name, **kw):
    return pl.pallas_call(body, name=name, **kw)


def _params(*sem):
    return pltpu.CompilerParams(dimension_semantics=sem, vmem_limit_bytes=56 * 1024 * 1024)


_GELU_C = math.sqrt(2.0 / math.pi)
_GELU_A = 0.044715


def _gelu(x):
    return (0.5 * x) * (1.0 + jnp.tanh(x * (_GELU_C + (_GELU_C * _GELU_A) * (x * x))))


def _gelu_and_grad(x):
    x2 = x * x
    u = 1.0 + jnp.tanh(x * (_GELU_C + (_GELU_C * _GELU_A) * x2))
    hx = 0.5 * x
    return hx * u, 0.5 * u + (hx * ((2.0 - u) * u)) * (_GELU_C + (3.0 * _GELU_C * _GELU_A) * x2)


def _sigmoid(x):
    return 1.0 / (1.0 + jnp.exp(-x))


def _softplus_neg(lam):
    y = jnp.exp(-jnp.abs(lam))
    u = 1.0 + y
    log1p = jnp.where(u == 1.0, y, jnp.log(u) * y / jnp.where(u == 1.0, 1.0, u - 1.0))
    return jnp.maximum(-lam, 0.0) + log1p


_NN = (((1,), (0,)), ((), ()))
_NT = (((1,), (1,)), ((), ()))
_TN = (((0,), (0,)), ((), ()))


def _dot(a, b, dims=_NN):
    return lax.dot_general(a, b, dims, preferred_element_type=F32)


def _group_mean(v, bd):
    hi = v.astype(BF16)
    lo = (v - hi.astype(F32)).astype(BF16)
    w = bd.shape[0]
    return jnp.concatenate([_dot(hi[:, c:c + w], bd) + _dot(lo[:, c:c + w], bd) for c in range(0, v.shape[1], w)], axis=1)


def _rope_tables(pos_ref, invf_ref):
    ang = pos_ref[...].astype(F32) * invf_ref[:, :2 * HEAD_DIM]
    reps = invf_ref.shape[1] // (2 * HEAD_DIM)
    return jnp.tile(jnp.cos(ang), (1, reps)), jnp.tile(jnp.sin(ang), (1, reps))


def _shift_down(x, halo, s):
    rolled = pltpu.roll(x, s, 0)
    hr = pltpu.roll(halo, s, 0)
    row = lax.broadcasted_iota(jnp.int32, hr.shape, 0)
    first = jnp.where(row < s, hr, rolled[:8])
    return jnp.concatenate([first, rolled[8:]], axis=0)


def _shift_up(x, halo, s):
    n = x.shape[0]
    rolled = pltpu.roll(x, n - s, 0)
    hr = pltpu.roll(halo, 8 - s, 0)
    row = lax.broadcasted_iota(jnp.int32, hr.shape, 0)
    last = jnp.where(row >= 8 - s, hr, rolled[n - 8:])
    return jnp.concatenate([rolled[:n - 8], last], axis=0)


def _scan_fwd(a, u):
    n, w = a.shape
    a3, u3 = a.reshape(n // 8, 8, w), u.reshape(n // 8, 8, w)
    row = lax.broadcasted_iota(jnp.int32, a3.shape, 1)
    for s in (1, 2, 4):
        a_s = jnp.where(row < s, 1.0, pltpu.roll(a3, s, 1))
        u_s = jnp.where(row < s, 0.0, pltpu.roll(u3, s, 1))
        u3 = u3 + a3 * u_s
        a3 = a3 * a_s
    ps, hs = [a3[0]], [u3[0]]
    for k in range(1, n // 8):
        ps.append(a3[k] * ps[-1][7:8, :])
        hs.append(u3[k] + a3[k] * hs[-1][7:8, :])
    return jnp.concatenate(ps, axis=0), jnp.concatenate(hs, axis=0)


def _scan_bwd(b, v):
    n, w = b.shape
    b3, v3 = b.reshape(n // 8, 8, w), v.reshape(n // 8, 8, w)
    row = lax.broadcasted_iota(jnp.int32, b3.shape, 1)
    for s in (1, 2, 4):
        b_s = jnp.where(row >= 8 - s, 1.0, pltpu.roll(b3, 8 - s, 1))
        v_s = jnp.where(row >= 8 - s, 0.0, pltpu.roll(v3, 8 - s, 1))
        v3 = v3 + b3 * v_s
        b3 = b3 * b_s
    last = n // 8 - 1
    ps, gs = [b3[last]], [v3[last]]
    for k in range(last - 1, -1, -1):
        ps.append(b3[k] * ps[-1][0:1, :])
        gs.append(v3[k] + b3[k] * gs[-1][0:1, :])
    return jnp.concatenate(ps[::-1], axis=0), jnp.concatenate(gs[::-1], axis=0)


def _rot_half(y):
    n = y.shape[1]
    lane = lax.broadcasted_iota(jnp.int32, y.shape, 1) & (HEAD_DIM - 1)
    return jnp.where(lane < HEAD_DIM // 2, -pltpu.roll(y, n - HEAD_DIM // 2, 1), pltpu.roll(y, HEAD_DIM // 2, 1))


def _row_tile(r, cap=256):
    return max(t for t in range(16, cap + 1, 16) if r % t == 0)


def _all_gather(shards, name):
    na = len(shards)
    ms = [s.shape[0] for s in shards]

    def body(*refs):
        x_refs, out_refs = refs[:na], refs[na:2 * na]
        send_sems, recv_sems, local_sems = refs[2 * na:]
        x, y, c = lax.axis_index("x"), lax.axis_index("y"), lax.axis_index("c")
        me, sibling = (x, y, c), (x, y, 1 - c)
        chips = [(1 - x, y), (x, 1 - y), (1 - x, 1 - y)]

        def rows(a, px, py, pc):
            return out_refs[a].at[pl.ds((4 * px + 2 * py + pc) * ms[a], ms[a]), :]

        def copy(a, k, block, to, src=None):
            return pltpu.make_async_remote_copy(
                src_ref=rows(a, *block) if src is None else src, dst_ref=rows(a, *block),
                send_sem=send_sems.at[7 * a + k], recv_sem=recv_sems.at[7 * a + k], device_id=to, device_id_type=MESH)

        mine = [pltpu.make_async_copy(x_refs[a], rows(a, *me), local_sems.at[a]) for a in range(na)]
        first = []
        for a in range(na):
            mine[a].start()
            first.append(copy(a, 0, me, sibling, src=x_refs[a]))
            first += [copy(a, 1 + j, me, (*chip, c), src=x_refs[a]) for j, chip in enumerate(chips)]
        for cp in first:
            cp.start()
        passed = []
        for a in range(na):
            for j, chip in enumerate(chips):
                copy(a, 1 + j, (*chip, c), me).wait_recv()
                fw = copy(a, 4 + j, (*chip, c), sibling)
                fw.start()
                passed.append(fw)
        for a in range(na):
            copy(a, 0, sibling, me).wait_recv()
            for j, chip in enumerate(chips):
                copy(a, 4 + j, (*chip, 1 - c), me).wait_recv()
        for cp in first + passed:
            cp.wait_send()
        for cp in mine:
            cp.wait()

    return _call(
        body, name=name, out_shape=[jax.ShapeDtypeStruct((N_DEV * s.shape[0], s.shape[1]), s.dtype) for s in shards],
        in_specs=[ANY] * na, out_specs=[ANY] * na,
        scratch_shapes=[pltpu.SemaphoreType.DMA((7 * na,)), pltpu.SemaphoreType.DMA((7 * na,)),
                        pltpu.SemaphoreType.DMA((na,))],
    )(*shards)


HBM = pl.BlockSpec(memory_space=pltpu.HBM)
SEM = pl.BlockSpec(memory_space=pltpu.SEMAPHORE)
EFFECT = pltpu.SideEffectType.DATAFLOW_SIDE_EFFECTING
N_PEERS = N_DEV - 1


def _peer(k):
    x, y, c = lax.axis_index("x"), lax.axis_index("y"), lax.axis_index("c")
    b = k + 1
    flip = lambda v, bit: 1 - v if bit else v
    return flip(x, b & 4), flip(y, b & 2), flip(c, b & 1)


def _in_hbm(a):
    return pltpu.with_memory_space_constraint(a, pltpu.HBM)


def _split_copy_descr(na, kind, src_refs, land_refs, send_sems, recv_sems):
    x, y, c = lax.axis_index("x"), lax.axis_index("y"), lax.axis_index("c")
    me = 4 * x + 2 * y + c
    copies = []
    for a in range(na):
        for k in range(N_PEERS):
            px, py, pc = _peer(k)
            if kind == "gather":
                m = src_refs[a].shape[0]
                src, dst = src_refs[a], land_refs[a].at[pl.ds(me * m, m), :]
            else:
                src, dst = src_refs[a].at[4 * px + 2 * py + pc], land_refs[a].at[k]
            copies.append(pltpu.make_async_remote_copy(
                src_ref=src, dst_ref=dst, send_sem=send_sems.at[N_PEERS * a + k], recv_sem=recv_sems.at[N_PEERS * a + k],
                device_id=(px, py, pc), device_id_type=MESH))
    return copies


def _landing(shape, dtype, own=None, at=None):
    buf = lax.empty(shape, dtype)
    return buf if own is None else lax.dynamic_update_slice(buf, own, (at, 0))


def _exchange_start(srcs, lands, kind, after, name):
    na = len(srcs)
    land_shapes = [l.shape for l in lands]

    def body(*refs):
        src_refs, land_refs = refs[:na], refs[na:2 * na]
        send_sems, recv_sems = refs[2 * na + 1], refs[2 * na + 2]
        token = refs[-1]
        for cp in _split_copy_descr(na, kind, src_refs, land_refs, send_sems, recv_sems):
            cp.start()
        token[...] = jnp.zeros_like(token)

    lands = [_in_hbm(l) for l in lands]
    sem = pltpu.SemaphoreType.DMA((N_PEERS * na,))
    outs = _call(
        body, name=name,
        out_shape=[sem, sem] + [pltpu.HBM(s.shape, s.dtype) for s in srcs] + [pltpu.HBM(s, srcs[0].dtype) for s in land_shapes]
        + [jax.ShapeDtypeStruct((8, 128), F32)],
        in_specs=[HBM] * (2 * na) + [ANY], out_specs=[SEM, SEM] + [HBM] * (2 * na) + [pl.BlockSpec(memory_space=pltpu.VMEM)],
        input_output_aliases={i: 2 + i for i in range(2 * na)},
        compiler_params=pltpu.CompilerParams(has_side_effects=EFFECT),
    )(*[_in_hbm(s) for s in srcs], *lands, after)
    return outs[0], outs[1], outs[2:2 + na], outs[2 + na:2 + 2 * na], outs[-1]


def _exchange_wait(send_sems, recv_sems, srcs, lands, kind, after, name):
    na = len(srcs)

    def body(*refs):
        src_refs, land_refs = refs[:na], refs[na:2 * na]
        s_sems, r_sems = refs[2 * na], refs[2 * na + 1]
        for cp in _split_copy_descr(na, kind, src_refs, land_refs, s_sems, r_sems):
            cp.wait_send()
            cp.wait_recv()

    outs = _call(
        body, name=name, out_shape=[pltpu.HBM(s.shape, s.dtype) for s in srcs] + [pltpu.HBM(l.shape, l.dtype) for l in lands],
        in_specs=[HBM] * (2 * na) + [SEM, SEM, ANY], out_specs=[HBM] * (2 * na),
        input_output_aliases={i: i for i in range(2 * na)},
        compiler_params=pltpu.CompilerParams(has_side_effects=EFFECT),
    )(*srcs, *lands, send_sems, recv_sems, after)
    return outs[:na], outs[na:]


def _mm(a, b, mode, out_dtype, name, add=None, tm=1024, tn=1024, tk=1024, b_noff=0, b_koff=0,
        n=None, k=None, into=None, o_rows=None, o_moff=0, loss_target=None):
    if mode == "tn":
        K, M = a.shape
    else:
        M, K = a.shape
    N = n if n is not None else (b.shape[0] if mode == "nt" else b.shape[1])
    if k is not None:
        assert k == K
    tm, tn, tk = min(tm, M), min(tn, N), min(tk, K)
    assert M % tm == 0 and N % tn == 0 and K % tk == 0, (name, M, N, K)
    nk = K // tk
    if mode == "nn":
        a_spec = pl.BlockSpec((tm, tk), lambda i, j, kk: (i, kk))
        b_spec, dims = pl.BlockSpec((tk, tn), lambda i, j, kk: (kk + b_koff, j + b_noff)), _NN
    elif mode == "nt":
        a_spec = pl.BlockSpec((tm, tk), lambda i, j, kk: (i, kk))
        b_spec, dims = pl.BlockSpec((tn, tk), lambda i, j, kk: (j + b_noff, kk + b_koff)), _NT
    else:
        a_spec = pl.BlockSpec((tk, tm), lambda i, j, kk: (kk, i))
        b_spec, dims = pl.BlockSpec((tk, tn), lambda i, j, kk: (kk + b_koff, j + b_noff)), _TN
    o_spec = pl.BlockSpec((tm, tn), lambda i, j, kk: (i + o_moff, j))
    has_add, has_into, has_loss = add is not None, into is not None, loss_target is not None
    assert not has_loss or (has_add and tn == N and not has_into)
    n_in = 2 + has_add + has_loss + has_into

    def body(*refs):
        a_ref, b_ref = refs[0], refs[1]
        add_ref = refs[2] if has_add else None
        outs = refs[n_in:]

        def finish(r):
            if has_add:
                r = r + add_ref[...]
            if has_loss:
                e = r - refs[3][...]
                dy = e * (1.0 / N)
                outs[0][...] = dy
                outs[1][...] = dy.astype(BF16)
                outs[2][...] = jnp.sum(e * e, axis=0, keepdims=True)[None]
            else:
                outs[0][...] = r.astype(out_dtype)

        if nk == 1:
            finish(_dot(a_ref[...], b_ref[...], dims))
        else:
            acc = refs[-1]
            kk = pl.program_id(2)

            @pl.when(kk == 0)
            def _():
                acc[...] = _dot(a_ref[...], b_ref[...], dims)

            @pl.when((kk > 0) & (kk < nk - 1))
            def _():
                acc[...] += _dot(a_ref[...], b_ref[...], dims)

            @pl.when(kk == nk - 1)
            def _():
                finish(acc[...] + _dot(a_ref[...], b_ref[...], dims))

    tile = pl.BlockSpec((tm, tn), lambda i, j, kk: (i, j))
    ins = [a, b] + ([add] if has_add else []) + ([loss_target] if has_loss else []) + ([into] if has_into else [])
    specs = [a_spec, b_spec] + [tile] * (has_add + has_loss) + ([ANY] if has_into else [])
    rows = into.shape[0] if has_into else (o_rows if o_rows is not None else M)
    if has_loss:
        out_specs = [tile, tile, pl.BlockSpec((1, 1, N), lambda i, j, kk: (i, 0, 0))]
        out_shape = [jax.ShapeDtypeStruct((M, N), F32), jax.ShapeDtypeStruct((M, N), BF16), jax.ShapeDtypeStruct((M // tm, 1, N), F32)]
    else:
        out_specs, out_shape = o_spec, jax.ShapeDtypeStruct((rows, N), out_dtype)
    return _call(
        body, name=name, grid=(M // tm, N // tn, nk), in_specs=specs, out_specs=out_specs, out_shape=out_shape,
        scratch_shapes=[pltpu.VMEM((tm, tn), F32)] if nk > 1 else [],
        input_output_aliases={len(ins) - 1: 0} if has_into else {},
        compiler_params=_params("parallel", "parallel", "arbitrary"),
    )(*ins)


def _mm_norm_bwd(parts, b, x, resid, g, name, tm=512, tk=512):
    T, N = x.shape
    counts = [p.shape[1] // tk for p in parts]
    starts = [sum(counts[:i]) for i in range(len(parts))]
    nsteps = sum(counts)
    assert all(p.shape[1] % tk == 0 for p in parts) and b.shape == (nsteps * tk, N)
    npart = len(parts)

    def body(*refs):
        a_refs, b_ref, x_ref, res_ref, g_ref = refs[:npart], refs[npart], refs[npart + 1], refs[npart + 2], refs[npart + 3]
        dx_ref, dxb_ref, dg_ref, acc = refs[npart + 4:]
        i, s = pl.program_id(0), pl.program_id(1)

        @pl.when((i == 0) & (s == 0))
        def _():
            dg_ref[...] = jnp.zeros_like(dg_ref)

        for p in range(npart):
            @pl.when((s >= starts[p]) & (s < starts[p] + counts[p]))
            def _(p=p):
                d = _dot(a_refs[p][...], b_ref[...])

                @pl.when(s == 0)
                def _():
                    acc[...] = d

                @pl.when(s > 0)
                def _():
                    acc[...] += d

        @pl.when(s == nsteps - 1)
        def _():
            xv, dhv = x_ref[...], acc[...]
            r = lax.rsqrt(jnp.mean(xv * xv, axis=-1, keepdims=True) + EPS)
            gd = dhv * g_ref[...]
            m = jnp.mean(gd * xv, axis=-1, keepdims=True)
            dx = res_ref[...] + r * gd - xv * (r * r * r) * m
            dx_ref[...] = dx
            dxb_ref[...] = dx.astype(BF16)
            dg_ref[...] += jnp.sum(dhv * xv * r, axis=0, keepdims=True)

    a_specs = [pl.BlockSpec((tm, tk), lambda i, s, st=st, c=c: (i, jnp.clip(s - st, 0, c - 1))) for st, c in zip(starts, counts)]
    row = pl.BlockSpec((tm, N), lambda i, s: (i, 0))
    vec = pl.BlockSpec((1, N), lambda i, s: (0, 0))
    return _call(
        body, name=name, grid=(T // tm, nsteps),
        in_specs=a_specs + [pl.BlockSpec((tk, N), lambda i, s: (s, 0)), row, row, vec], out_specs=[row, row, vec],
        out_shape=[jax.ShapeDtypeStruct((T, N), F32), jax.ShapeDtypeStruct((T, N), BF16), jax.ShapeDtypeStruct((1, N), F32)],
        scratch_shapes=[pltpu.VMEM((tm, N), F32)], compiler_params=_params("arbitrary", "arbitrary"),
    )(*parts, b, x, resid, g)


def _norm_proj(x, g, wT, name, tm=1024, tn=1280):
    T, K = x.shape
    N = wT.shape[0]

    def body(x_ref, g_ref, w_ref, o_ref, h_ref):
        xv = x_ref[...]
        r = lax.rsqrt(jnp.mean(xv * xv, axis=-1, keepdims=True) + EPS)
        hv = (xv * r * g_ref[...]).astype(BF16)

        @pl.when(pl.program_id(1) == 0)
        def _():
            h_ref[...] = hv

        o_ref[...] = _dot(hv, w_ref[...], _NT)

    return _call(
        body, name=name, grid=(T // tm, N // tn),
        in_specs=[pl.BlockSpec((tm, K), lambda i, j: (i, 0)), pl.BlockSpec((1, K), lambda i, j: (0, 0)),
                  pl.BlockSpec((tn, K), lambda i, j: (j, 0))],
        out_specs=[pl.BlockSpec((tm, tn), lambda i, j: (i, j)), pl.BlockSpec((tm, K), lambda i, j: (i, 0))],
        out_shape=[jax.ShapeDtypeStruct((T, N), F32), jax.ShapeDtypeStruct((T, K), BF16)],
        compiler_params=_params("parallel", "arbitrary"),
    )(x, g, wT)


def _qk_prep(proj, pos, invf, qg, kg, bd, name, tm=512):
    T = proj.shape[0]

    def body(q_ref, k_ref, pos_ref, invf_ref, qg_ref, kg_ref, bd_ref, qo_ref, ko_ref):
        cos, sin = _rope_tables(pos_ref, invf_ref)

        def prep(xv, gv, scale):
            r = lax.rsqrt(_group_mean(xv * xv, bd_ref[...]) + EPS)
            yv = xv * r * gv
            return ((yv * cos + _rot_half(yv) * sin) * scale).astype(BF16).astype(F32)

        qo_ref[...] = prep(q_ref[...], qg_ref[...], HEAD_DIM ** -0.5)
        ko_ref[...] = prep(k_ref[...], kg_ref[...], 1.0)

    col = lambda j: pl.BlockSpec((tm, ATTN_W), lambda i, j=j: (i, j))
    vec = pl.BlockSpec((1, ATTN_W), lambda i: (0, 0))
    out = pl.BlockSpec((tm, ATTN_W), lambda i: (i, 0))
    return _call(
        body, name=name, grid=(T // tm,),
        in_specs=[col(0), col(1), pl.BlockSpec((tm, 1), lambda i: (i, 0)), vec, vec, vec,
                  pl.BlockSpec((2 * HEAD_DIM, 2 * HEAD_DIM), lambda i: (0, 0))],
        out_specs=[out, out], out_shape=[jax.ShapeDtypeStruct((T, ATTN_W), F32)] * 2,
        compiler_params=_params("parallel"),
    )(proj, proj, pos, invf, qg, kg, bd)


def _qk_prep_bwd(proj, dqh, dkh, dv, pos, invf, qg, kg, bd, name, tm=512):
    T = proj.shape[0]

    def body(q_ref, k_ref, dq_ref, dk_ref, dv_ref, pos_ref, invf_ref, qg_ref, kg_ref, bd_ref, o_ref, gq_ref, gk_ref):
        @pl.when(pl.program_id(0) == 0)
        def _():
            gq_ref[...] = jnp.zeros_like(gq_ref)
            gk_ref[...] = jnp.zeros_like(gk_ref)

        cos, sin = _rope_tables(pos_ref, invf_ref)

        def back(xv, gv, dz, scale):
            dz = dz * scale
            dy = dz * cos - _rot_half(dz * sin)
            r = lax.rsqrt(_group_mean(xv * xv, bd_ref[...]) + EPS)
            gd = dy * gv
            m = _group_mean(gd * xv, bd_ref[...])
            dx = r * gd - xv * (r * r * r) * m
            return dx, jnp.sum(dy * xv * r, axis=0, keepdims=True)

        dxq, gs = back(q_ref[...], qg_ref[...], dq_ref[...], HEAD_DIM ** -0.5)
        gq_ref[...] += gs
        dxk, gs = back(k_ref[...], kg_ref[...], dk_ref[...], 1.0)
        gk_ref[...] += gs
        o_ref[...] = jnp.concatenate([dxq.astype(BF16), dxk.astype(BF16), dv_ref[...].astype(BF16)], axis=1)

    col = lambda j: pl.BlockSpec((tm, ATTN_W), lambda i, j=j: (i, j))
    row = pl.BlockSpec((tm, ATTN_W), lambda i: (i, 0))
    vec = pl.BlockSpec((1, ATTN_W), lambda i: (0, 0))
    return _call(
        body, name=name, grid=(T // tm,),
        in_specs=[col(0), col(1), row, row, row, pl.BlockSpec((tm, 1), lambda i: (i, 0)), vec, vec, vec,
                  pl.BlockSpec((2 * HEAD_DIM, 2 * HEAD_DIM), lambda i: (0, 0))],
        out_specs=[pl.BlockSpec((tm, 3 * ATTN_W), lambda i: (i, 0)), vec, vec],
        out_shape=[jax.ShapeDtypeStruct((T, 3 * ATTN_W), BF16)] + [jax.ShapeDtypeStruct((1, ATTN_W), F32)] * 2,
        compiler_params=_params("arbitrary"),
    )(proj, proj, dqh, dkh, dv, pos, invf, qg, kg, bd)


def _ld(ref, start, size, dil):
    return ref[pl.ds(start, size), :] if dil == 1 else ref[pl.ds(start, size, stride=dil), :]


def _st(ref, start, size, dil, val):
    if dil == 1:
        ref[pl.ds(start, size), :] = val
    else:
        ref[pl.ds(start, size, stride=dil), :] = val


def _attn_geometry(T, dil):
    nb = T // dil // QBLK
    if nb == 2:
        return 1, 2 * QBLK, 2 * QBLK
    return nb, QBLK, (2 * QBLK if nb >= 2 else QBLK)


def _attn_unroll(qb):
    return 4


def _attn_unit(j, u, dil, nit, unroll):
    return unroll * j + u if dil >= unroll else j + u * (nit // unroll)


def _attn_block(it, dil, qb, kw):
    c, n = it & (dil - 1), lax.shift_right_logical(it, dil.bit_length() - 1)
    sq = n * (qb * dil) + c
    sk = jnp.maximum(n - (kw // qb - 1), 0) * (qb * dil) + c
    qi = lax.broadcasted_iota(jnp.int32, (2 * qb, kw), 0) & (qb - 1)
    kj = lax.broadcasted_iota(jnp.int32, (2 * qb, kw), 1)
    rel = jnp.where(n > 0, kw - qb, 0) + qi - kj
    return sq, sk, (rel >= 0) & (rel <= QBLK)


def _stack_heads(xv, head0):
    z = jnp.zeros_like(xv)
    return jnp.concatenate([jnp.where(head0, xv, z), jnp.where(head0, z, xv)], axis=0)


def _unstack_heads(x2, head0):
    qb = x2.shape[0] // 2
    return jnp.where(head0, x2[:qb], x2[qb:])


def _attn_fwd(qf, kf, proj, name):
    T = qf.shape[0]

    def body(q_ref, k_ref, v_ref, o_ref, lse_ref):
        for bi, dil in enumerate(DILATIONS):
            nb, qb, kw = _attn_geometry(T, dil)
            nit = nb * dil
            head0 = lax.broadcasted_iota(jnp.int32, (qb, 2 * HEAD_DIM), 1) < HEAD_DIM

            def step(j, carry, bi=bi, dil=dil, qb=qb, kw=kw, nit=nit, head0=head0):
                units = []
                for u in range(_attn_unroll(qb)):
                    sq, sk, ok = _attn_block(_attn_unit(j, u, dil, nit, _attn_unroll(qb)), dil, qb, kw)
                    old = (_ld(o_ref, sq, qb, dil), _ld(lse_ref, sq, qb, dil)) if bi > 0 else None
                    units.append((sq, ok, _ld(q_ref, sq, qb, dil).astype(BF16), _ld(k_ref, sk, kw, dil).astype(BF16),
                                  _ld(v_ref, sk, kw, dil).astype(BF16), old))
                results = []
                for sq, ok, qv, kv, vv, old in units:
                    s = jnp.where(ok, _dot(_stack_heads(qv, head0), kv, _NT), NEG_INF)
                    m = jnp.max(s, axis=-1, keepdims=True)
                    p = jnp.exp(s - m).astype(BF16)
                    acc = _dot(p, jnp.concatenate([vv, jnp.ones_like(vv)], axis=1))
                    l = acc[:, 2 * HEAD_DIM:]
                    o_new = _unstack_heads(acc[:, :2 * HEAD_DIM] / l, head0)
                    l_new = _unstack_heads(m + jnp.log(l), head0)
                    if bi > 0:
                        o_old, l_old = old
                        mx = jnp.maximum(l_old, l_new)
                        e0, e1 = jnp.exp(l_old - mx), jnp.exp(l_new - mx)
                        z = e0 + e1
                        o_new = (e0 * o_old + e1 * o_new) / z
                        l_new = mx + jnp.log(z)
                    results.append((sq, o_new, l_new))
                for sq, o_new, l_new in results:
                    _st(o_ref, sq, qb, dil, o_new)
                    _st(lse_ref, sq, qb, dil, l_new)
                return carry

            lax.fori_loop(0, nit // _attn_unroll(qb), step, 0)

    blk = lambda off: pl.BlockSpec((T, 2 * HEAD_DIM), lambda hp, off=off: (0, off + hp))
    return _call(
        body, name=name, grid=(4,), in_specs=[blk(0), blk(0), blk(8)], out_specs=[blk(0), blk(0)],
        out_shape=[jax.ShapeDtypeStruct((T, ATTN_W), F32)] * 2, compiler_params=_params("parallel"),
    )(qf, kf, proj)


def _attn_bwd(qf, kf, proj, do, lse, delta, name):
    T = qf.shape[0]

    def body(q_ref, k_ref, v_ref, do_ref, lse_ref, dl_ref, dq_ref, dk_ref, dv_ref):
        for ref in (dq_ref, dk_ref, dv_ref):
            ref[...] = jnp.zeros_like(ref)
        for dil in DILATIONS:
            nb, qb, kw = _attn_geometry(T, dil)
            nit = nb * dil
            head0 = lax.broadcasted_iota(jnp.int32, (qb, 2 * HEAD_DIM), 1) < HEAD_DIM

            def step(j, carry, dil=dil, qb=qb, kw=kw, nit=nit, head0=head0):
                units = []
                for u in range(_attn_unroll(qb)):
                    sq, sk, ok = _attn_block(_attn_unit(j, u, dil, nit, _attn_unroll(qb)), dil, qb, kw)
                    lsev, dlv = _ld(lse_ref, sq, qb, dil), _ld(dl_ref, sq, qb, dil)
                    units.append((sq, sk, ok, _ld(q_ref, sq, qb, dil).astype(BF16), _ld(do_ref, sq, qb, dil).astype(BF16),
                                  jnp.concatenate([lsev[:, 0:1], lsev[:, HEAD_DIM:HEAD_DIM + 1]], axis=0),
                                  jnp.concatenate([dlv[:, 0:1], dlv[:, HEAD_DIM:HEAD_DIM + 1]], axis=0),
                                  _ld(k_ref, sk, kw, dil).astype(BF16), _ld(v_ref, sk, kw, dil).astype(BF16),
                                  _ld(dq_ref, sq, qb, dil), _ld(dk_ref, sk, kw, dil), _ld(dv_ref, sk, kw, dil)))
                results = []
                for sq, sk, ok, qv, dov, lse2, dl2, kv, vv, dq0, dk0, dv0 in units:
                    q2, do2 = _stack_heads(qv, head0), _stack_heads(dov, head0)
                    p = jnp.where(ok, jnp.exp(_dot(q2, kv, _NT) - lse2), 0.0)
                    ds = (p * (_dot(do2, vv, _NT) - dl2)).astype(BF16)
                    results.append((sq, sk, dq0 + _unstack_heads(_dot(ds, kv), head0),
                                    dk0 + _dot(ds, q2, _TN), dv0 + _dot(p.astype(BF16), do2, _TN)))
                for sq, sk, dq, dk, dv in results:
                    _st(dq_ref, sq, qb, dil, dq)
                    _st(dk_ref, sk, kw, dil, dk)
                    _st(dv_ref, sk, kw, dil, dv)
                return carry

            lax.fori_loop(0, nit // _attn_unroll(qb), step, 0)

    blk = lambda off: pl.BlockSpec((T, 2 * HEAD_DIM), lambda hp, off=off: (0, off + hp))
    return _call(
        body, name=name, grid=(4,), in_specs=[blk(0), blk(0), blk(8), blk(0), blk(0), blk(0)], out_specs=[blk(0)] * 3,
        out_shape=[jax.ShapeDtypeStruct((T, ATTN_W), F32)] * 3, compiler_params=_params("parallel"),
    )(qf, kf, proj, do, lse, delta)


def _attn_norm(attn, g, name, tm=512):
    T = attn.shape[0]

    def body(a_ref, g_ref, o_ref):
        av = a_ref[...]
        r = lax.rsqrt(jnp.mean(av * av, axis=-1, keepdims=True) + EPS)
        o_ref[...] = (av * r * g_ref[...]).astype(BF16)

    row = pl.BlockSpec((tm, ATTN_W), lambda i: (i, 0))
    return _call(
        body, name=name, grid=(T // tm,), in_specs=[row, pl.BlockSpec((1, ATTN_W), lambda i: (0, 0))], out_specs=row,
        out_shape=jax.ShapeDtypeStruct((T, 2 * ATTN_W), BF16), compiler_params=_params("parallel"),
    )(attn, g)


def _attn_norm_bwd(dmix, attn, g, bd, name, tm=512):
    T = attn.shape[0]

    def body(d_ref, a_ref, g_ref, bd_ref, do_ref, dl_ref, dg_ref):
        @pl.when(pl.program_id(0) == 0)
        def _():
            dg_ref[...] = jnp.zeros_like(dg_ref)

        dy, av = d_ref[...], a_ref[...]
        r = lax.rsqrt(jnp.mean(av * av, axis=-1, keepdims=True) + EPS)
        gd = dy * g_ref[...]
        m = jnp.mean(gd * av, axis=-1, keepdims=True)
        da = r * gd - av * (r * r * r) * m
        do_ref[...] = da
        dl_ref[...] = _group_mean(da * av, bd_ref[...]) * float(HEAD_DIM)
        dg_ref[...] += jnp.sum(dy * av * r, axis=0, keepdims=True)

    row = pl.BlockSpec((tm, ATTN_W), lambda i: (i, 0))
    vec = pl.BlockSpec((1, ATTN_W), lambda i: (0, 0))
    return _call(
        body, name=name, grid=(T // tm,),
        in_specs=[row, row, vec, pl.BlockSpec((2 * HEAD_DIM, 2 * HEAD_DIM), lambda i: (0, 0))], out_specs=[row, row, vec],
        out_shape=[jax.ShapeDtypeStruct((T, ATTN_W), F32)] * 2 + [jax.ShapeDtypeStruct((1, ATTN_W), F32)],
        compiler_params=_params("arbitrary"),
    )(dmix, attn, g, bd)


def _rec_gates(xc, wrg_ref, wig_ref, brg_ref, big_ref, lam_ref):
    xb = xc.astype(BF16)
    r = _sigmoid(_dot(xb, wrg_ref[...]) + brg_ref[...])
    ig = _sigmoid(_dot(xb, wig_ref[...]) + big_ref[...])
    sp = _softplus_neg(lam_ref[...])
    log_a = -LRU_C * r * sp
    a = jnp.exp(log_a)
    th = jnp.tanh(log_a)
    mult = jnp.sqrt(-2.0 * th / (1.0 - th))
    return xb, r, ig, sp, a, mult


def _rec_fwd(proj, mix, cw, cb, wrg, wig, brg, big, lam, g, name, tm=256):
    T = proj.shape[0]
    hb = tm // 8

    def body(xr_ref, halo_ref, gr_ref, cw_ref, cb_ref, wrg_ref, wig_ref, brg_ref, big_ref, lam_ref, g_ref, mix_ref,
             xc_ref, h_ref, out_ref, carry):
        i = pl.program_id(0)

        @pl.when(i == 0)
        def _():
            carry[...] = jnp.zeros_like(carry)

        xr = xr_ref[...]
        halo = jnp.where(i > 0, halo_ref[...], 0.0)
        xc = cb_ref[...] + cw_ref[3:4, :] * xr
        for s in range(1, REC_CONV):
            xc = xc + cw_ref[3 - s:4 - s, :] * _shift_down(xr, halo, s)
        xc_ref[...] = xc
        _, _, ig, _, a, mult = _rec_gates(xc, wrg_ref, wig_ref, brg_ref, big_ref, lam_ref)
        pa, hl = _scan_fwd(a, mult * (ig * xc))
        h = hl + pa * carry[0:1, :]
        h_ref[...] = h
        carry[0:1, :] = h_ref[pl.ds(tm - 1, 1), :]
        hg = h * _gelu(gr_ref[...])
        r = lax.rsqrt(jnp.mean(hg * hg, axis=-1, keepdims=True) + EPS)
        out_ref[...] = (hg * r * g_ref[...]).astype(BF16)

    vec = pl.BlockSpec((1, REC_W), lambda i: (0, 0))
    row = pl.BlockSpec((tm, REC_W), lambda i: (i, 0))
    mat = pl.BlockSpec((REC_W, REC_W), lambda i: (0, 0))
    return _call(
        body, name=name, grid=(T // tm,),
        in_specs=[pl.BlockSpec((tm, REC_W), lambda i: (i, 3)),
                  pl.BlockSpec((8, REC_W), lambda i: (jnp.maximum(i * hb - 1, 0), 3)),
                  pl.BlockSpec((tm, REC_W), lambda i: (i, 4)),
                  pl.BlockSpec((8, REC_W), lambda i: (0, 0)), vec, mat, mat, vec, vec, vec, vec, ANY],
        out_specs=[row, row, pl.BlockSpec((tm, REC_W), lambda i: (i, 1))],
        out_shape=[jax.ShapeDtypeStruct((T, REC_W), F32)] * 2 + [jax.ShapeDtypeStruct(mix.shape, BF16)],
        scratch_shapes=[pltpu.VMEM((8, REC_W), F32)], input_output_aliases={11: 2},
        compiler_params=_params("arbitrary"),
    )(proj, proj, proj, cw, cb, wrg, wig, brg, big, lam, g, mix)


def _rec_bwd(dmix, proj, xc, h, cw, cb, wrg, wig, brg, big, lam, g, name, tm=256):
    T = proj.shape[0]
    nt = T // tm
    hb = tm // 8

    def body(d_ref, xr_ref, xhalo_ref, gr_ref, xc_ref, h_ref, hhalo_ref, cw_ref, cb_ref, wrg_ref, wig_ref, brg_ref,
             big_ref, lam_ref, g_ref,
             drec_ref, gcw_ref, gcb_ref, gwrg_ref, gwig_ref, gbrg_ref, gbig_ref, glam_ref, gg_ref,
             g_carry, a_first, dxc_next, gsp):
        i = pl.program_id(0)
        first_tile = i == nt - 1

        @pl.when(i == 0)
        def _():
            for ref in (gcw_ref, gcb_ref, gwrg_ref, gwig_ref, gbrg_ref, gbig_ref, glam_ref, gg_ref,
                        g_carry, a_first, dxc_next, gsp):
                ref[...] = jnp.zeros_like(ref)

        xr, xc, hv = xr_ref[...], xc_ref[...], h_ref[...]
        xhalo = jnp.where(first_tile, 0.0, xhalo_ref[...])
        hhalo = jnp.where(first_tile, 0.0, hhalo_ref[...])
        xb, r, ig, sp, a, mult = _rec_gates(xc, wrg_ref, wig_ref, brg_ref, big_ref, lam_ref)
        h_prev = _shift_down(hv, hhalo, 1)
        ge, dge = _gelu_and_grad(gr_ref[...])
        hg = hv * ge
        rr = lax.rsqrt(jnp.mean(hg * hg, axis=-1, keepdims=True) + EPS)
        dy = d_ref[...]
        gd = dy * g_ref[...]
        dhg = rr * gd - hg * (rr * rr * rr) * jnp.mean(gd * hg, axis=-1, keepdims=True)
        gg_ref[...] += jnp.sum(dy * hg * rr, axis=0, keepdims=True)
        dgr = (dhg * hv * dge).astype(BF16)
        dh = dhg * ge
        b = _shift_up(a, jnp.broadcast_to(a_first[0:1, :], (8, REC_W)), 1)
        pb, gl = _scan_bwd(b, dh)
        gs = gl + pb * g_carry[0:1, :]
        g_carry[0:1, :] = gs[0:1, :]
        a_first[0:1, :] = a[0:1, :]
        da = gs * h_prev
        dmult = gs * (ig * xc)
        di = gs * (mult * xc)
        dxc = gs * (mult * ig)
        dlog_a = da * a - dmult * (a * a) / mult
        gsp[...] += jnp.sum(dlog_a * (-LRU_C * r), axis=0, keepdims=True)
        dzr = (dlog_a * (-LRU_C * sp)) * (r * (1.0 - r))
        dzi = di * (ig * (1.0 - ig))
        dzr_b, dzi_b = dzr.astype(BF16), dzi.astype(BF16)
        dxc = dxc + _dot(dzr_b, wrg_ref[...], _NT) + _dot(dzi_b, wig_ref[...], _NT)
        gwrg_ref[...] += _dot(xb, dzr_b, _TN)
        gwig_ref[...] += _dot(xb, dzi_b, _TN)
        gbrg_ref[...] += jnp.sum(dzr, axis=0, keepdims=True)
        gbig_ref[...] += jnp.sum(dzi, axis=0, keepdims=True)
        nxt = dxc_next[...]
        dxr = cw_ref[3:4, :] * dxc
        gcw_ref[3:4, :] += jnp.sum(dxc * xr, axis=0, keepdims=True)
        for s in range(1, REC_CONV):
            dxr = dxr + cw_ref[3 - s:4 - s, :] * _shift_up(dxc, nxt, s)
            gcw_ref[3 - s:4 - s, :] += jnp.sum(dxc * _shift_down(xr, xhalo, s), axis=0, keepdims=True)
        gcb_ref[...] += jnp.sum(dxc, axis=0, keepdims=True)
        dxc_next[...] = dxc[:8]
        drec_ref[...] = jnp.concatenate([dxr.astype(BF16), dgr], axis=1)

        @pl.when(first_tile)
        def _():
            glam_ref[...] = gsp[...] * (-_sigmoid(-lam_ref[...]))

    rev = lambda i: nt - 1 - i
    vec = pl.BlockSpec((1, REC_W), lambda i: (0, 0))
    row = pl.BlockSpec((tm, REC_W), lambda i: (rev(i), 0))
    mat = pl.BlockSpec((REC_W, REC_W), lambda i: (0, 0))
    cwb = pl.BlockSpec((8, REC_W), lambda i: (0, 0))
    halo = lambda c: pl.BlockSpec((8, REC_W), lambda i, c=c: (jnp.maximum(rev(i) * hb - 1, 0), c))
    return _call(
        body, name=name, grid=(nt,),
        in_specs=[pl.BlockSpec((tm, REC_W), lambda i: (rev(i), 1)),
                  pl.BlockSpec((tm, REC_W), lambda i: (rev(i), 3)), halo(3),
                  pl.BlockSpec((tm, REC_W), lambda i: (rev(i), 4)),
                  row, row, halo(0), cwb, vec, mat, mat, vec, vec, vec, vec],
        out_specs=[pl.BlockSpec((tm, 2 * REC_W), lambda i: (rev(i), 0)), cwb, vec, mat, mat, vec, vec, vec, vec],
        out_shape=[jax.ShapeDtypeStruct((T, 2 * REC_W), BF16)]
        + [jax.ShapeDtypeStruct((8, REC_W), F32), jax.ShapeDtypeStruct((1, REC_W), F32)]
        + [jax.ShapeDtypeStruct((REC_W, REC_W), F32)] * 2 + [jax.ShapeDtypeStruct((1, REC_W), F32)] * 4,
        scratch_shapes=[pltpu.VMEM((8, REC_W), F32)] * 3 + [pltpu.VMEM((1, REC_W), F32)],
        compiler_params=_params("arbitrary"),
    )(dmix, proj, proj, proj, xc, h, h, cw, cb, wrg, wig, brg, big, lam, g)


def _ffn_conv(x_ext, cw_ref, cb_ref):
    return (cb_ref[...] + cw_ref[2:3, :] * x_ext + cw_ref[1:2, :] * pltpu.roll(x_ext, 1, 0)
            + cw_ref[0:1, :] * pltpu.roll(x_ext, 2, 0))


def _up_proj_act(x2, g, w_upT, cw, cb, name, tm=1024, tc=768):
    T = x2.shape[0]
    nc = D_FF // tc

    def body(x_ref, g_ref, wg_ref, wu_ref, cwg_ref, cwu_ref, cbg_ref, cbu_ref, act_ref, da_ref, db_ref, pg_ref, pu_ref,
             h_ref, hist_g, hist_u, hs):
        i, j = pl.program_id(0), pl.program_id(1)

        @pl.when(j == 0)
        def _():
            xv = x_ref[...]
            r = lax.rsqrt(jnp.mean(xv * xv, axis=-1, keepdims=True) + EPS)
            hs[...] = (xv * r * g_ref[...]).astype(BF16)
            h_ref[...] = hs[...]

        hv = hs[...]
        pg, pu = _dot(hv, wg_ref[...], _NT), _dot(hv, wu_ref[...], _NT)
        ge = jnp.concatenate([jnp.where(i > 0, hist_g[j], 0.0), pg], axis=0)
        ue = jnp.concatenate([jnp.where(i > 0, hist_u[j], 0.0), pu], axis=0)
        gel, dgel = _gelu_and_grad(_ffn_conv(ge, cwg_ref, cbg_ref)[8:])
        uu = _ffn_conv(ue, cwu_ref, cbu_ref)[8:]
        act_ref[...] = (gel * uu).astype(BF16)
        da_ref[...] = (uu * dgel).astype(BF16)
        db_ref[...] = gel.astype(BF16)
        pg_ref[...] = pg.astype(BF16)
        pu_ref[...] = pu.astype(BF16)
        hist_g[j] = pg[tm - 8:]
        hist_u[j] = pu[tm - 8:]

    tile = pl.BlockSpec((tm, tc), lambda i, j: (i, j))
    wsp = lambda off: pl.BlockSpec((tc, D_MODEL), lambda i, j, off=off: (j + off, 0))
    cws = lambda off: pl.BlockSpec((8, tc), lambda i, j, off=off: (0, j + off))
    cbs = lambda off: pl.BlockSpec((1, tc), lambda i, j, off=off: (0, j + off))
    return _call(
        body, name=name, grid=(T // tm, nc),
        in_specs=[pl.BlockSpec((tm, D_MODEL), lambda i, j: (i, 0)), pl.BlockSpec((1, D_MODEL), lambda i, j: (0, 0)),
                  wsp(0), wsp(nc), cws(0), cws(nc), cbs(0), cbs(nc)],
        out_specs=[tile] * 5 + [pl.BlockSpec((tm, D_MODEL), lambda i, j: (i, 0))],
        out_shape=[jax.ShapeDtypeStruct((T, D_FF), BF16)] * 5 + [jax.ShapeDtypeStruct((T, D_MODEL), BF16)],
        scratch_shapes=[pltpu.VMEM((nc, 8, tc), F32)] * 2 + [pltpu.VMEM((tm, D_MODEL), BF16)],
        compiler_params=_params("arbitrary", "arbitrary"),
    )(x2, g, w_upT, w_upT, cw, cw, cb, cb)


def _ffn_bwd(dyb, w_down, da, db, pg, pu, cw, name, tm=1024, tc=768):
    T, F = pg.shape
    nt = T // tm
    hb16 = tm // 16
    nc = F // tc
    n = tm + 8

    def body(dy_ref, dyn_ref, wd_ref, a_ref, an_ref, b_ref, bn_ref, g_ref, u_ref, cwg_ref, cwu_ref,
             dg_ref, du_ref, gcwg_ref, gcwu_ref, gcbg_ref, gcbu_ref):
        i = pl.program_id(1)
        last = i == nt - 1

        @pl.when(i == 0)
        def _():
            for ref in (gcwg_ref, gcwu_ref, gcbg_ref, gcbu_ref):
                ref[...] = jnp.zeros_like(ref)

        wd = wd_ref[...]
        dact_next = jnp.where(last, 0.0, _dot(dyn_ref[...], wd, _NT)[:8])
        de = jnp.concatenate([_dot(dy_ref[...], wd, _NT), dact_next], axis=0)
        ext = lambda t, nx: jnp.concatenate([t[...].astype(F32), nx[...].astype(F32)[:8]], axis=0)
        for dcv, x_ref, cw_ref, dx_ref, gcw_ref, gcb_ref in ((de * ext(a_ref, an_ref), g_ref, cwg_ref, dg_ref, gcwg_ref, gcbg_ref),
                                                               (de * ext(b_ref, bn_ref), u_ref, cwu_ref, du_ref, gcwu_ref, gcbu_ref)):
            s1, s2 = pltpu.roll(dcv, n - 1, 0), pltpu.roll(dcv, n - 2, 0)
            dx_ref[...] = (cw_ref[2:3, :] * dcv + cw_ref[1:2, :] * s1 + cw_ref[0:1, :] * s2)[:tm].astype(BF16)
            xv = x_ref[...].astype(F32)
            gcw_ref[2:3, :] += jnp.sum(xv * dcv[:tm], axis=0, keepdims=True)
            gcw_ref[1:2, :] += jnp.sum(xv * s1[:tm], axis=0, keepdims=True)
            gcw_ref[0:1, :] += jnp.sum(xv * s2[:tm], axis=0, keepdims=True)
            gcb_ref[...] += jnp.sum(dcv[:tm], axis=0, keepdims=True)

    tile = pl.BlockSpec((tm, tc), lambda j, i: (i, j))
    nxt = pl.BlockSpec((16, tc), lambda j, i: (jnp.minimum((i + 1) * hb16, nt * hb16 - 1), j))
    cws = lambda off: pl.BlockSpec((8, tc), lambda j, i, off=off: (0, j + off))
    cbs = pl.BlockSpec((1, tc), lambda j, i: (0, j))
    return _call(
        body, name=name, grid=(nc, nt),
        in_specs=[pl.BlockSpec((tm, D_MODEL), lambda j, i: (i, 0)),
                  pl.BlockSpec((16, D_MODEL), lambda j, i: (jnp.minimum((i + 1) * hb16, nt * hb16 - 1), 0)),
                  pl.BlockSpec((tc, D_MODEL), lambda j, i: (j, 0)), tile, nxt, tile, nxt, tile, tile, cws(0), cws(nc)],
        out_specs=[tile, tile, cws(0), cws(0), cbs, cbs],
        out_shape=[jax.ShapeDtypeStruct((T, F), BF16)] * 2 + [jax.ShapeDtypeStruct((8, F), F32)] * 2
        + [jax.ShapeDtypeStruct((1, F), F32)] * 2,
        compiler_params=_params("parallel", "arbitrary"),
    )(dyb, dyb, w_down, da, da, db, db, pg, pu, cw, cw)


def _adam_update(w, g, m, v):
    m2 = ADAM_B1 * m + (1.0 - ADAM_B1) * g
    v2 = ADAM_B2 * v + (1.0 - ADAM_B2) * (g * g)
    m_hat = m2 / (1.0 - ADAM_B1 ** ADAM_STEP)
    v_hat = v2 / (1.0 - ADAM_B2 ** ADAM_STEP)
    delta = -ADAM_LR * (m_hat / (jnp.sqrt(v_hat) + ADAM_EPS) + ADAM_WD * w)
    return delta, m2, v2


def _adam_sharded(p, r2, idx, w, m, v, name, transposed=False):
    r, n = p.shape[1:]
    nrecv = r2.shape[0]
    tr = (256 if r % 256 == 0 else r) if transposed else _row_tile(r)

    def body(c_ref, p_ref, r_ref, w_ref, m_ref, v_ref, g_ref, d_ref, m2_ref, v2_ref):
        g = p_ref[...].astype(F32)
        for k in range(nrecv):
            g = g + r_ref[k].astype(F32)
        if transposed:
            g = g.T
        g_ref[...] = g
        d_ref[...], m2_ref[...], v2_ref[...] = _adam_update(w_ref[...], g, m_ref[...], v_ref[...])

    blk = pl.BlockSpec((n, tr), lambda i, c_ref: (0, i)) if transposed else pl.BlockSpec((tr, n), lambda i, c_ref: (i, 0))
    spec = pltpu.PrefetchScalarGridSpec(
        num_scalar_prefetch=1, grid=(r // tr,),
        in_specs=[pl.BlockSpec((None, tr, n), lambda i, c_ref: (c_ref[0], i, 0)),
                  pl.BlockSpec((nrecv, tr, n), lambda i, c_ref: (0, i, 0)), blk, blk, blk],
        out_specs=[blk] * 4)
    return _call(body, name=name, grid_spec=spec, out_shape=[jax.ShapeDtypeStruct(w.shape, F32)] * 4,
                 compiler_params=_params("parallel"))(idx, p, r2, w, m, v)


def _sum_slabs(p, r2, idx, name):
    _, r, n = p.shape

    def body(c_ref, p_ref, r_ref, o_ref):
        acc = p_ref[...]
        for k in range(N_PEERS):
            acc = acc + r_ref[k]
        o_ref[...] = acc

    spec = pltpu.PrefetchScalarGridSpec(
        num_scalar_prefetch=1, grid=(1,),
        in_specs=[pl.BlockSpec((None, r, n), lambda i, c_ref: (c_ref[0], 0, 0)),
                  pl.BlockSpec((N_PEERS, r, n), lambda i, c_ref: (0, 0, 0))],
        out_specs=pl.BlockSpec((r, n), lambda i, c_ref: (0, 0)))
    return _call(body, name=name, grid_spec=spec, out_shape=jax.ShapeDtypeStruct((r, n), F32))(idx, p, r2)


def _adam_small(ws, gs, ms, vs, name):
    n = len(ws)

    def body(*refs):
        for i in range(n):
            d, m2, v2 = _adam_update(refs[i][...], refs[n + i][...], refs[2 * n + i][...], refs[3 * n + i][...])
            refs[4 * n + i][...] = d
            refs[5 * n + i][...] = m2
            refs[6 * n + i][...] = v2

    outs = _call(body, name=name, out_shape=[jax.ShapeDtypeStruct(w.shape, F32) for w in ws] * 3)(*ws, *gs, *ms, *vs)
    return outs[:n], outs[n:2 * n], outs[2 * n:]


def _pack_small_grads(full, halves, rcw, fcwg, fcwu, wrg, wig, lparts, name):
    nf, nh = len(full), len(halves)

    def body(*refs):
        o = refs[-1]
        o[...] = jnp.zeros_like(o)
        row = 0
        for r in refs[:nf]:
            for j in range(r.shape[1] // 1024):
                o[row:row + 1, :] = r[:, 1024 * j:1024 * (j + 1)]
                row += 1
        for k in range(0, nh, 2):
            o[row:row + 1, 0:512] = refs[nf + k][...]
            o[row:row + 1, 512:1024] = refs[nf + k + 1][...]
            row += 1
        rcw_ref, fg_ref, fu_ref, wrg_ref, wig_ref, l_ref = refs[nf + nh:nf + nh + 6]
        for k in range(2):
            o[row:row + 1, 0:512] = rcw_ref[2 * k:2 * k + 1, :]
            o[row:row + 1, 512:1024] = rcw_ref[2 * k + 1:2 * k + 2, :]
            row += 1
        for f_ref in (fg_ref, fu_ref):
            for k in range(FFN_CONV):
                for j in range(D_FF // 1024):
                    o[row:row + 1, :] = f_ref[k:k + 1, 1024 * j:1024 * (j + 1)]
                    row += 1
        assert row == 32
        for n in range(8):
            o[32:96, 64 * n:64 * n + 64] = wrg_ref[64 * n:64 * n + 64, 64 * n:64 * n + 64]
            o[32:96, 512 + 64 * n:512 + 64 * n + 64] = wig_ref[64 * n:64 * n + 64, 64 * n:64 * n + 64]
        o[96:97, :] = jnp.sum(l_ref[...], axis=0, keepdims=True)

    return _call(body, name=name, out_shape=jax.ShapeDtypeStruct((SMALL_ROWS, 1024), F32))(
        *full, *halves, rcw, fcwg, fcwu, wrg, wig, lparts)


def _block_diag(w):
    eye = jnp.eye(8, dtype=w.dtype)
    return (w[:, :, None, :] * eye[:, None, :, None]).reshape(512, 512)


def kernel(x, positions, g_mix, w_in, q_norm_g, k_norm_g, rec_conv_w, rec_conv_b, w_rg, b_rg, w_ig, b_ig, lru_lambda, g_attn_out, g_rec_out, w_out, g_ffn, w_up, ffn_conv_w, ffn_conv_b, w_down, loss_target, m_g_mix, m_w_in, m_q_norm_g, m_k_norm_g, m_rec_conv_w, m_rec_conv_b, m_w_rg, m_b_rg, m_w_ig, m_b_ig, m_lru_lambda, m_g_attn_out, m_g_rec_out, m_w_out, m_g_ffn, m_w_up, m_ffn_conv_w, m_ffn_conv_b, m_w_down, v_g_mix, v_w_in, v_q_norm_g, v_k_norm_g, v_rec_conv_w, v_rec_conv_b, v_w_rg, v_b_rg, v_w_ig, v_b_ig, v_lru_lambda, v_g_attn_out, v_g_rec_out, v_w_out, v_g_ffn, v_w_up, v_ffn_conv_w, v_ffn_conv_b, v_w_down):
    T = x.shape[1]
    ix, iy, ic = lax.axis_index("x"), lax.axis_index("y"), lax.axis_index("c")
    dev = 4 * ix + 2 * iy + ic
    xs = x.reshape(T, D_MODEL)
    tgt = loss_target.reshape(T, D_MODEL)
    pos = positions.reshape(T, 1)

    shards = {"w_in": (w_in[0], m_w_in[0], v_w_in[0]), "w_out": (w_out[0], m_w_out[0], v_w_out[0]),
              "w_up": (w_up[0], m_w_up[0], v_w_up[0]), "w_down": (w_down[0], m_w_down[0], v_w_down[0])}
    taps = jnp.concatenate([rec_conv_w.reshape(-1), ffn_conv_w.reshape(-1), jnp.zeros((4096 - 2560,), F32)]).reshape(8, 512)
    W_inT, taps_all = _all_gather([w_in[0].T.astype(BF16), taps], "ag_w_in")
    late = [w_out[0].astype(BF16), w_up[0].T.astype(BF16), w_down[0].astype(BF16)]
    ag_send, ag_recv, late_thru, land_thru, ag_token = _exchange_start(
        late, [_landing((N_DEV * s.shape[0], 1024), BF16, s, dev * s.shape[0]) for s in late], "gather", taps_all,
        "ag_late_start")
    taps_all = taps_all.reshape(N_DEV, 4096)
    rcw = taps_all[:, :256].reshape(8, 4, 64).transpose(1, 0, 2).reshape(4, REC_W)
    fcw = taps_all[:, 256:2560].reshape(8, 3, 768).transpose(1, 0, 2).reshape(3, 2 * D_FF)
    rcw8 = jnp.pad(rcw, ((0, 4), (0, 0)))
    fcw8 = jnp.pad(fcw, ((0, 5), (0, 0)))
    fcb = ffn_conv_b.reshape(1, 2 * D_FF)

    half = HEAD_DIM // 2
    inv_freq = ROPE_THETA ** (-jnp.arange(half, dtype=F32) / half)
    invf = jnp.tile(inv_freq, 2 * N_HEADS).reshape(1, ATTN_W)
    bd = jnp.asarray(np.kron(np.eye(2), np.full((HEAD_DIM, HEAD_DIM), 1.0 / HEAD_DIM)), BF16)
    qg = jnp.tile(q_norm_g.reshape(HEAD_DIM), N_HEADS).reshape(1, ATTN_W)
    kg = jnp.tile(k_norm_g.reshape(HEAD_DIM), N_HEADS).reshape(1, ATTN_W)
    wrg_bd = _block_diag(w_rg[0]).astype(BF16)
    wig_bd = _block_diag(w_ig[0]).astype(BF16)
    brg, big = b_rg.reshape(1, REC_W), b_ig.reshape(1, REC_W)

    proj, h1 = _norm_proj(xs, g_mix + ag_token[0, 0], W_inT, "in_proj", tn=IN_W)
    qf, kf = _qk_prep(proj, pos, invf, qg, kg, bd, "qk_prep")
    attn, lse = _attn_fwd(qf, kf, proj, "attn_fwd")
    mix = _attn_norm(attn, g_attn_out, "attn_norm")
    xc, hstate, mix = _rec_fwd(proj, mix, rcw8, rec_conv_b, wrg_bd, wig_bd, brg, big, lru_lambda, g_rec_out, "rec_fwd")
    _, (W_out, W_upT, W_down) = _exchange_wait(ag_send, ag_recv, late_thru, land_thru, "gather", hstate, "ag_late_wait")
    x2 = _mm(mix, W_out, "nn", F32, "out_proj", add=xs)

    act, da, db, pg, pu, h2 = _up_proj_act(x2, g_ffn, W_upT, fcw8, fcb, "up_proj_act")
    dy, dyb, lparts = _mm(act, W_down, "nn", F32, "down_proj_loss", add=x2, loss_target=tgt, tm=512, tk=D_FF)

    g_down = _mm(act, dyb, "tn", BF16, "g_w_down", tk=4096)
    dpg, dpu, g_fcwg, g_fcwu, g_fcbg, g_fcbu = _ffn_bwd(dyb, W_down, da, db, pg, pu, fcw8, "ffn_bwd")
    g_upT = _mm(dpg, h2, "tn", BF16, "g_w_up_gate", tk=4096, o_rows=2 * D_FF)
    g_upT = _mm(dpu, h2, "tn", BF16, "g_w_up_up", tk=4096, into=g_upT, o_moff=D_FF // 1024)
    ffn_g = [g_upT.reshape(N_DEV, 2 * D_FF // N_DEV, 1024), g_down.reshape(N_DEV, D_FF // N_DEV, 1024)]
    rs_send, rs_recv, ffn_g, ffn_land, rs_token = _exchange_start(
        ffn_g, [_landing((N_PEERS,) + g.shape[1:], BF16) for g in ffn_g], "scatter", dpu, "rs_ffn_start")
    dx2, dx2b, g_gffn = _mm_norm_bwd([dpg, dpu], W_upT, x2, dy, g_ffn + rs_token[0, 0], "d_h2_norm_bwd", tm=1024, tk=1536)

    dmix = _mm(dx2b, W_out, "nt", F32, "d_mix")
    g_out = _mm(mix, dx2b, "tn", BF16, "g_w_out", tk=4096).reshape(N_DEV, D_MODEL // N_DEV, 1024)
    out_send, out_recv, (g_out,), out_land, out_token = _exchange_start(
        [g_out], [_landing((N_PEERS,) + g_out.shape[1:], BF16)], "scatter", dmix, "rs_out_start")
    do, delta, g_gattn = _attn_norm_bwd(dmix, attn, g_attn_out + out_token[0, 0], bd, "attn_norm_bwd")
    dqh, dkh, dv = _attn_bwd(qf, kf, proj, do, lse, delta, "attn_bwd")
    dqkv, g_qg, g_kg = _qk_prep_bwd(proj, dqh, dkh, dv, pos, invf, qg, kg, bd, "qk_prep_bwd")
    (drec, g_rcw, g_rcb, g_wrg, g_wig, g_brg, g_big, g_lam, g_grec) = _rec_bwd(
        dmix, proj, xc, hstate, rcw8, rec_conv_b, wrg_bd, wig_bd, brg, big, lru_lambda, g_rec_out, "rec_bwd")
    g_inT = _mm(dqkv, h1, "tn", BF16, "g_w_in_qkv", tm=512, tk=4096, o_rows=IN_W)
    g_inT = _mm(drec, h1, "tn", BF16, "g_w_in_rec", tm=512, tk=4096, into=g_inT, o_moff=3 * ATTN_W // 512)
    g_inT = g_inT.reshape(N_DEV, IN_W // N_DEV, 1024)
    in_send, in_recv, (g_inT,), in_land, in_token = _exchange_start(
        [g_inT], [_landing((N_PEERS,) + g_inT.shape[1:], BF16)], "scatter", drec, "rs_in_start")
    grad_x, _, g_gmix = _mm_norm_bwd([dqkv, drec], W_inT, xs, dx2, g_mix + in_token[0, 0], "d_h1_norm_bwd", tm=1024, tk=512)

    flat = _pack_small_grads([g_gmix, g_gffn, g_fcbg, g_fcbu], [g_rcb, g_brg, g_big, g_lam, g_gattn, g_grec, g_qg, g_kg],
                             g_rcw, g_fcwg, g_fcwu, g_wrg, g_wig, lparts.reshape(-1, D_MODEL), "pack_small_grads")
    srows = SMALL_ROWS // N_DEV
    flat = flat.reshape(N_DEV, srows, 1024)
    sm_send, sm_recv, (flat,), sm_land, sm_token = _exchange_start(
        [flat], [_landing((N_PEERS, srows, 1024), F32)], "scatter", grad_x, "ar_small_rs_start")

    devi = jnp.reshape(dev, (1,)).astype(jnp.int32)
    ffn_g, ffn_land = _exchange_wait(rs_send, rs_recv, ffn_g, ffn_land, "scatter", sm_token, "rs_ffn_wait")
    (g_out,), out_land = _exchange_wait(out_send, out_recv, [g_out], out_land, "scatter", sm_token, "rs_out_wait")
    big_out = {"grad": {}, "delta": {}, "new_m": {}, "new_v": {}}

    def adam_big(nm, p, r):
        w_, m_, v_ = shards[nm]
        res = _adam_sharded(p, r, devi, w_, m_, v_, "adam_" + nm, transposed=nm in ("w_in", "w_up"))
        for kind, a in zip(("grad", "delta", "new_m", "new_v"), res):
            big_out[kind][nm] = a[None]
        return res[0]

    last = adam_big("w_up", ffn_g[0], ffn_land[0])
    (flat,), sm_land = _exchange_wait(sm_send, sm_recv, [flat], sm_land, "scatter", last, "ar_small_rs_wait")
    mine = _sum_slabs(flat, sm_land[0], devi, "sum_small_grads")
    sm_send, sm_recv, (mine,), sm_land, sm_token = _exchange_start(
        [mine], [_landing((SMALL_ROWS, 1024), F32, mine, dev * srows)], "gather", last, "ar_small_ag_start")
    adam_big("w_down", ffn_g[1], ffn_land[1])
    last = adam_big("w_out", g_out, out_land[0])
    _, (tot,) = _exchange_wait(sm_send, sm_recv, [mine], sm_land, "gather", last, "ar_small_ag_wait")
    (g_inT,), in_land = _exchange_wait(in_send, in_recv, [g_inT], in_land, "scatter", tot, "rs_in_wait")
    adam_big("w_in", g_inT, in_land[0])

    half = lambda r, h, shape: tot[r, 512 * h:512 * h + 512].reshape(shape)
    blocks = lambda h: tot[32:96, 512 * h:512 * h + 512].reshape(64, 8, 64).transpose(1, 0, 2)[None]
    fcw_full = jnp.concatenate([tot[14:23].reshape(1, 3, D_FF), tot[23:32].reshape(1, 3, D_FF)], axis=2)
    g_small = {
        "g_mix": tot[0:1], "g_ffn": tot[1:2], "ffn_conv_b": tot[2:8].reshape(1, 2 * D_FF),
        "rec_conv_b": half(8, 0, (1, 512)), "b_rg": half(8, 1, (1, 8, 64)), "b_ig": half(9, 0, (1, 8, 64)),
        "lru_lambda": half(9, 1, (1, 512)), "g_attn_out": half(10, 0, (1, 512)), "g_rec_out": half(10, 1, (1, 512)),
        "q_norm_g": half(11, 0, (N_HEADS, HEAD_DIM)).sum(0)[None], "k_norm_g": half(11, 1, (N_HEADS, HEAD_DIM)).sum(0)[None],
        "w_rg": blocks(0), "w_ig": blocks(1),
        "rec_conv_w": lax.dynamic_slice(tot[12:14].reshape(1, 4, REC_W), (0, 0, 64 * dev), (1, 4, 64)),
        "ffn_conv_w": lax.dynamic_slice(fcw_full, (0, 0, 768 * dev), (1, 3, 768))}
    loss = 0.5 / D_MODEL * jnp.sum(tot[96])
    given = dict(rec_conv_w=rec_conv_w, ffn_conv_w=ffn_conv_w,g_mix=g_mix, q_norm_g=q_norm_g, k_norm_g=k_norm_g, rec_conv_b=rec_conv_b, w_rg=w_rg, b_rg=b_rg, w_ig=w_ig,
                 b_ig=b_ig, lru_lambda=lru_lambda, g_attn_out=g_attn_out, g_rec_out=g_rec_out, g_ffn=g_ffn, ffn_conv_b=ffn_conv_b)
    given_m = dict(rec_conv_w=m_rec_conv_w, ffn_conv_w=m_ffn_conv_w, g_mix=m_g_mix, q_norm_g=m_q_norm_g, k_norm_g=m_k_norm_g, rec_conv_b=m_rec_conv_b, w_rg=m_w_rg, b_rg=m_b_rg,
                   w_ig=m_w_ig, b_ig=m_b_ig, lru_lambda=m_lru_lambda, g_attn_out=m_g_attn_out, g_rec_out=m_g_rec_out,
                   g_ffn=m_g_ffn, ffn_conv_b=m_ffn_conv_b)
    given_v = dict(rec_conv_w=v_rec_conv_w, ffn_conv_w=v_ffn_conv_w, g_mix=v_g_mix, q_norm_g=v_q_norm_g, k_norm_g=v_k_norm_g, rec_conv_b=v_rec_conv_b, w_rg=v_w_rg, b_rg=v_b_rg,
                   w_ig=v_w_ig, b_ig=v_b_ig, lru_lambda=v_lru_lambda, g_attn_out=v_g_attn_out, g_rec_out=v_g_rec_out,
                   g_ffn=v_g_ffn, ffn_conv_b=v_ffn_conv_b)
    small = sorted(given)
    ds, m2s, v2s = _adam_small([given[k] for k in small], [g_small[k] for k in small], [given_m[k] for k in small],
                               [given_v[k] for k in small], "adam_small")
    small_out = {"grad": g_small, "delta": dict(zip(small, ds)), "new_m": dict(zip(small, m2s)), "new_v": dict(zip(small, v2s))}

    order = ("g_mix", "w_in", "q_norm_g", "k_norm_g", "rec_conv_w", "rec_conv_b", "w_rg", "b_rg", "w_ig", "b_ig",
             "lru_lambda", "g_attn_out", "g_rec_out", "w_out", "g_ffn", "w_up", "ffn_conv_w", "ffn_conv_b", "w_down")
    outs = [loss, grad_x.reshape(1, T, D_MODEL)]
    for kind in ("grad", "delta", "new_m", "new_v"):
        for name in order:
            outs.append(big_out[kind][name] if name in big_out[kind] else small_out[kind][name])
    return tuple(outs)
```

```python
import math

import numpy as np
import jax
import jax.numpy as jnp
from jax import lax
from jax.experimental import pallas as pl
from jax.experimental.pallas import tpu as pltpu

F32 = jnp.float32
BF16 = jnp.bfloat16

D_MODEL = 1024
HEAD_DIM = 64
ATTN_W = 512
REC_W = 512
N_HEADS = 8
D_FF = 3072
IN_W = 2560
REC_CONV = 4
FFN_CONV = 3
LRU_C = 8.0
ROPE_THETA = 10000.0
EPS = 1e-6
NEG_INF = -1e30
QBLK = 128
DILATIONS = (1, 4, 16)
N_DEV = 8
SMALL_ROWS = 128
ADAM_LR, ADAM_B1, ADAM_B2, ADAM_EPS, ADAM_WD, ADAM_STEP = 0.001, 0.9, 0.999, 1e-08, 0.01, 10
MESH = pl.DeviceIdType.MESH
ANY = pl.BlockSpec(memory_space=pl.ANY)


def _call(body, *, name, **kw):
    return pl.pallas_call(body, name=name, **kw)


def _params(*sem):
    return pltpu.CompilerParams(dimension_semantics=sem, vmem_limit_bytes=56 * 1024 * 1024)


_GELU_C = math.sqrt(2.0 / math.pi)
_GELU_A = 0.044715


def _gelu(x):
    return (0.5 * x) * (1.0 + jnp.tanh(x * (_GELU_C + (_GELU_C * _GELU_A) * (x * x))))


def _gelu_and_grad(x):
    x2 = x * x
    u = 1.0 + jnp.tanh(x * (_GELU_C + (_GELU_C * _GELU_A) * x2))
    hx = 0.5 * x
    return hx * u, 0.5 * u + (hx * ((2.0 - u) * u)) * (_GELU_C + (3.0 * _GELU_C * _GELU_A) * x2)


def _sigmoid(x):
    return 1.0 / (1.0 + jnp.exp(-x))


def _softplus_neg(lam):
    y = jnp.exp(-jnp.abs(lam))
    u = 1.0 + y
    log1p = jnp.where(u == 1.0, y, jnp.log(u) * y / jnp.where(u == 1.0, 1.0, u - 1.0))
    return jnp.maximum(-lam, 0.0) + log1p


_NN = (((1,), (0,)), ((), ()))
_NT = (((1,), (1,)), ((), ()))
_TN = (((0,), (0,)), ((), ()))


def _dot(a, b, dims=_NN):
    return lax.dot_general(a, b, dims, preferred_element_type=F32)


def _group_mean(v, bd):
    hi = v.astype(BF16)
    lo = (v - hi.astype(F32)).astype(BF16)
    w = bd.shape[0]
    return jnp.concatenate([_dot(hi[:, c:c + w], bd) + _dot(lo[:, c:c + w], bd) for c in range(0, v.shape[1], w)], axis=1)


def _rope_tables(pos_ref, invf_ref):
    ang = pos_ref[...].astype(F32) * invf_ref[:, :2 * HEAD_DIM]
    reps = invf_ref.shape[1] // (2 * HEAD_DIM)
    return jnp.tile(jnp.cos(ang), (1, reps)), jnp.tile(jnp.sin(ang), (1, reps))


def _shift_down(x, halo, s):
    rolled = pltpu.roll(x, s, 0)
    hr = pltpu.roll(halo, s, 0)
    row = lax.broadcasted_iota(jnp.int32, hr.shape, 0)
    first = jnp.where(row < s, hr, rolled[:8])
    return jnp.concatenate([first, rolled[8:]], axis=0)


def _shift_up(x, halo, s):
    n = x.shape[0]
    rolled = pltpu.roll(x, n - s, 0)
    hr = pltpu.roll(halo, 8 - s, 0)
    row = lax.broadcasted_iota(jnp.int32, hr.shape, 0)
    last = jnp.where(row >= 8 - s, hr, rolled[n - 8:])
    return jnp.concatenate([rolled[:n - 8], last], axis=0)


def _scan_fwd(a, u):
    n, w = a.shape
    a3, u3 = a.reshape(n // 8, 8, w), u.reshape(n // 8, 8, w)
    row = lax.broadcasted_iota(jnp.int32, a3.shape, 1)
    for s in (1, 2, 4):
        a_s = jnp.where(row < s, 1.0, pltpu.roll(a3, s, 1))
        u_s = jnp.where(row < s, 0.0, pltpu.roll(u3, s, 1))
        u3 = u3 + a3 * u_s
        a3 = a3 * a_s
    ps, hs = [a3[0]], [u3[0]]
    for k in range(1, n // 8):
        ps.append(a3[k] * ps[-1][7:8, :])
        hs.append(u3[k] + a3[k] * hs[-1][7:8, :])
    return jnp.concatenate(ps, axis=0), jnp.concatenate(hs, axis=0)


def _scan_bwd(b, v):
    n, w = b.shape
    b3, v3 = b.reshape(n // 8, 8, w), v.reshape(n // 8, 8, w)
    row = lax.broadcasted_iota(jnp.int32, b3.shape, 1)
    for s in (1, 2, 4):
        b_s = jnp.where(row >= 8 - s, 1.0, pltpu.roll(b3, 8 - s, 1))
        v_s = jnp.where(row >= 8 - s, 0.0, pltpu.roll(v3, 8 - s, 1))
        v3 = v3 + b3 * v_s
        b3 = b3 * b_s
    last = n // 8 - 1
    ps, gs = [b3[last]], [v3[last]]
    for k in range(last - 1, -1, -1):
        ps.append(b3[k] * ps[-1][0:1, :])
        gs.append(v3[k] + b3[k] * gs[-1][0:1, :])
    return jnp.concatenate(ps[::-1], axis=0), jnp.concatenate(gs[::-1], axis=0)


def _rot_half(y):
    n = y.shape[1]
    lane = lax.broadcasted_iota(jnp.int32, y.shape, 1) & (HEAD_DIM - 1)
    return jnp.where(lane < HEAD_DIM // 2, -pltpu.roll(y, n - HEAD_DIM // 2, 1), pltpu.roll(y, HEAD_DIM // 2, 1))


def _row_tile(r, cap=256):
    return max(t for t in range(16, cap + 1, 16) if r % t == 0)


def _all_gather(shards, name):
    na = len(shards)
    ms = [s.shape[0] for s in shards]

    def body(*refs):
        x_refs, out_refs = refs[:na], refs[na:2 * na]
        send_sems, recv_sems, local_sems = refs[2 * na:]
        x, y, c = lax.axis_index("x"), lax.axis_index("y"), lax.axis_index("c")
        me, sibling = (x, y, c), (x, y, 1 - c)
        chips = [(1 - x, y), (x, 1 - y), (1 - x, 1 - y)]

        def rows(a, px, py, pc):
            return out_refs[a].at[pl.ds(pl.multiple_of((4 * px + 2 * py + pc) * ms[a], 8), ms[a]), :]

        def copy(a, k, block, to, src=None):
            return pltpu.make_async_remote_copy(
                src_ref=rows(a, *block) if src is None else src, dst_ref=rows(a, *block),
                send_sem=send_sems.at[7 * a + k], recv_sem=recv_sems.at[7 * a + k], device_id=to, device_id_type=MESH)

        mine = [pltpu.make_async_copy(x_refs[a], rows(a, *me), local_sems.at[a]) for a in range(na)]
        first = []
        for a in range(na):
            mine[a].start()
            first.append(copy(a, 0, me, sibling, src=x_refs[a]))
            first += [copy(a, 1 + j, me, (*chip, c), src=x_refs[a]) for j, chip in enumerate(chips)]
        for cp in first:
            cp.start()
        passed = []
        for a in range(na):
            for j, chip in enumerate(chips):
                copy(a, 1 + j, (*chip, c), me).wait_recv()
                fw = copy(a, 4 + j, (*chip, c), sibling)
                fw.start()
                passed.append(fw)
        for a in range(na):
            copy(a, 0, sibling, me).wait_recv()
            for j, chip in enumerate(chips):
                copy(a, 4 + j, (*chip, 1 - c), me).wait_recv()
        for cp in first + passed:
            cp.wait_send()
        for cp in mine:
            cp.wait()

    return _call(
        body, name=name, out_shape=[jax.ShapeDtypeStruct((N_DEV * s.shape[0], s.shape[1]), s.dtype) for s in shards],
        in_specs=[ANY] * na, out_specs=[ANY] * na,
        scratch_shapes=[pltpu.SemaphoreType.DMA((7 * na,)), pltpu.SemaphoreType.DMA((7 * na,)),
                        pltpu.SemaphoreType.DMA((na,))],
    )(*shards)


HBM = pl.BlockSpec(memory_space=pltpu.HBM)
SEM = pl.BlockSpec(memory_space=pltpu.SEMAPHORE)
EFFECT = pltpu.SideEffectType.DATAFLOW_SIDE_EFFECTING
N_PEERS = N_DEV - 1


def _peer(k):
    x, y, c = lax.axis_index("x"), lax.axis_index("y"), lax.axis_index("c")
    b = k + 1
    flip = lambda v, bit: 1 - v if bit else v
    return flip(x, b & 4), flip(y, b & 2), flip(c, b & 1)


def _in_hbm(a):
    return pltpu.with_memory_space_constraint(a, pltpu.HBM)


def _split_copy_descr(na, kind, src_refs, land_refs, send_sems, recv_sems):
    x, y, c = lax.axis_index("x"), lax.axis_index("y"), lax.axis_index("c")
    me = 4 * x + 2 * y + c
    copies = []
    for a in range(na):
        for k in range(N_PEERS):
            px, py, pc = _peer(k)
            if kind == "gather":
                m = src_refs[a].shape[0]
                src, dst = src_refs[a], land_refs[a].at[pl.ds(pl.multiple_of(me * m, 8), m), :]
            else:
                src, dst = src_refs[a].at[4 * px + 2 * py + pc], land_refs[a].at[k]
            copies.append(pltpu.make_async_remote_copy(
                src_ref=src, dst_ref=dst, send_sem=send_sems.at[N_PEERS * a + k], recv_sem=recv_sems.at[N_PEERS * a + k],
                device_id=(px, py, pc), device_id_type=MESH))
    return copies


def _landing(shape, dtype, own=None, at=None):
    buf = lax.empty(shape, dtype)
    return buf if own is None else lax.dynamic_update_slice(buf, own, (at, 0))


def _exchange_start(srcs, lands, kind, after, name):
    na = len(srcs)
    land_shapes = [l.shape for l in lands]

    def body(*refs):
        src_refs, land_refs = refs[:na], refs[na:2 * na]
        send_sems, recv_sems = refs[2 * na + 1], refs[2 * na + 2]
        token = refs[-1]
        for cp in _split_copy_descr(na, kind, src_refs, land_refs, send_sems, recv_sems):
            cp.start()
        token[...] = jnp.zeros_like(token)

    lands = [_in_hbm(l) for l in lands]
    sem = pltpu.SemaphoreType.DMA((N_PEERS * na,))
    outs = _call(
        body, name=name,
        out_shape=[sem, sem] + [pltpu.HBM(s.shape, s.dtype) for s in srcs] + [pltpu.HBM(s, srcs[0].dtype) for s in land_shapes]
        + [jax.ShapeDtypeStruct((8, 128), F32)],
        in_specs=[HBM] * (2 * na) + [ANY], out_specs=[SEM, SEM] + [HBM] * (2 * na) + [pl.BlockSpec(memory_space=pltpu.VMEM)],
        input_output_aliases={i: 2 + i for i in range(2 * na)},
        compiler_params=pltpu.CompilerParams(has_side_effects=EFFECT),
    )(*[_in_hbm(s) for s in srcs], *lands, after)
    return outs[0], outs[1], outs[2:2 + na], outs[2 + na:2 + 2 * na], outs[-1]


def _exchange_wait(send_sems, recv_sems, srcs, lands, kind, after, name):
    na = len(srcs)

    def body(*refs):
        src_refs, land_refs = refs[:na], refs[na:2 * na]
        s_sems, r_sems = refs[2 * na], refs[2 * na + 1]
        for cp in _split_copy_descr(na, kind, src_refs, land_refs, s_sems, r_sems):
            cp.wait_send()
            cp.wait_recv()

    outs = _call(
        body, name=name, out_shape=[pltpu.HBM(s.shape, s.dtype) for s in srcs] + [pltpu.HBM(l.shape, l.dtype) for l in lands],
        in_specs=[HBM] * (2 * na) + [SEM, SEM, ANY], out_specs=[HBM] * (2 * na),
        input_output_aliases={i: i for i in range(2 * na)},
        compiler_params=pltpu.CompilerParams(has_side_effects=EFFECT),
    )(*srcs, *lands, send_sems, recv_sems, after)
    return outs[:na], outs[na:]


def _mm(a, b, mode, out_dtype, name, add=None, tm=1024, tn=1024, tk=1024, b_noff=0, b_koff=0,
        n=None, k=None, into=None, o_rows=None, o_moff=0, loss_target=None):
    if mode == "tn":
        K, M = a.shape
    else:
        M, K = a.shape
    N = n if n is not None else (b.shape[0] if mode == "nt" else b.shape[1])
    if k is not None:
        assert k == K
    tm, tn, tk = min(tm, M), min(tn, N), min(tk, K)
    assert M % tm == 0 and N % tn == 0 and K % tk == 0, (name, M, N, K)
    nk = K // tk
    if mode == "nn":
        a_spec = pl.BlockSpec((tm, tk), lambda i, j, kk: (i, kk))
        b_spec, dims = pl.BlockSpec((tk, tn), lambda i, j, kk: (kk + b_koff, j + b_noff)), _NN
    elif mode == "nt":
        a_spec = pl.BlockSpec((tm, tk), lambda i, j, kk: (i, kk))
        b_spec, dims = pl.BlockSpec((tn, tk), lambda i, j, kk: (j + b_noff, kk + b_koff)), _NT
    else:
        a_spec = pl.BlockSpec((tk, tm), lambda i, j, kk: (kk, i))
        b_spec, dims = pl.BlockSpec((tk, tn), lambda i, j, kk: (kk + b_koff, j + b_noff)), _TN
    o_spec = pl.BlockSpec((tm, tn), lambda i, j, kk: (i + o_moff, j))
    has_add, has_into, has_loss = add is not None, into is not None, loss_target is not None
    assert not has_loss or (has_add and tn == N and not has_into)
    n_in = 2 + has_add + has_loss + has_into

    def body(*refs):
        a_ref, b_ref = refs[0], refs[1]
        add_ref = refs[2] if has_add else None
        outs = refs[n_in:]

        def finish(r):
            if has_add:
                r = r + add_ref[...]
            if has_loss:
                e = r - refs[3][...]
                dy = e * (1.0 / N)
                outs[0][...] = dy
                outs[1][...] = dy.astype(BF16)
                outs[2][...] = jnp.sum(e * e, axis=0, keepdims=True)[None]
            else:
                outs[0][...] = r.astype(out_dtype)

        if nk == 1:
            finish(_dot(a_ref[...], b_ref[...], dims))
        else:
            acc = refs[-1]
            kk = pl.program_id(2)

            @pl.when(kk == 0)
            def _():
                acc[...] = _dot(a_ref[...], b_ref[...], dims)

            @pl.when((kk > 0) & (kk < nk - 1))
            def _():
                acc[...] += _dot(a_ref[...], b_ref[...], dims)

            @pl.when(kk == nk - 1)
            def _():
                finish(acc[...] + _dot(a_ref[...], b_ref[...], dims))

    tile = pl.BlockSpec((tm, tn), lambda i, j, kk: (i, j))
    ins = [a, b] + ([add] if has_add else []) + ([loss_target] if has_loss else []) + ([into] if has_into else [])
    specs = [a_spec, b_spec] + [tile] * (has_add + has_loss) + ([ANY] if has_into else [])
    rows = into.shape[0] if has_into else (o_rows if o_rows is not None else M)
    if has_loss:
        out_specs = [tile, tile, pl.BlockSpec((1, 1, N), lambda i, j, kk: (i, 0, 0))]
        out_shape = [jax.ShapeDtypeStruct((M, N), F32), jax.ShapeDtypeStruct((M, N), BF16), jax.ShapeDtypeStruct((M // tm, 1, N), F32)]
    else:
        out_specs, out_shape = o_spec, jax.ShapeDtypeStruct((rows, N), out_dtype)
    return _call(
        body, name=name, grid=(M // tm, N // tn, nk), in_specs=specs, out_specs=out_specs, out_shape=out_shape,
        scratch_shapes=[pltpu.VMEM((tm, tn), F32)] if nk > 1 else [],
        input_output_aliases={len(ins) - 1: 0} if has_into else {},
        compiler_params=_params("parallel", "parallel", "arbitrary"),
    )(*ins)


def _mm_norm_bwd(parts, b, x, resid, g, name, tm=512, tk=512):
    T, N = x.shape
    counts = [p.shape[1] // tk for p in parts]
    starts = [sum(counts[:i]) for i in range(len(parts))]
    nsteps = sum(counts)
    assert all(p.shape[1] % tk == 0 for p in parts) and b.shape == (nsteps * tk, N)
    npart = len(parts)

    def body(*refs):
        a_refs, b_ref, x_ref, res_ref, g_ref = refs[:npart], refs[npart], refs[npart + 1], refs[npart + 2], refs[npart + 3]
        dx_ref, dxb_ref, dg_ref, acc = refs[npart + 4:]
        i, s = pl.program_id(0), pl.program_id(1)

        @pl.when((i == 0) & (s == 0))
        def _():
            dg_ref[...] = jnp.zeros_like(dg_ref)

        for p in range(npart):
            @pl.when((s >= starts[p]) & (s < starts[p] + counts[p]))
            def _(p=p):
                d = _dot(a_refs[p][...], b_ref[...])

                @pl.when(s == 0)
                def _():
                    acc[...] = d

                @pl.when(s > 0)
                def _():
                    acc[...] += d

        @pl.when(s == nsteps - 1)
        def _():
            xv, dhv = x_ref[...], acc[...]
            r = lax.rsqrt(jnp.mean(xv * xv, axis=-1, keepdims=True) + EPS)
            gd = dhv * g_ref[...]
            m = jnp.mean(gd * xv, axis=-1, keepdims=True)
            dx = res_ref[...] + r * gd - xv * (r * r * r) * m
            dx_ref[...] = dx
            dxb_ref[...] = dx.astype(BF16)
            dg_ref[...] += jnp.sum(dhv * xv * r, axis=0, keepdims=True)

    a_specs = [pl.BlockSpec((tm, tk), lambda i, s, st=st, c=c: (i, jnp.clip(s - st, 0, c - 1))) for st, c in zip(starts, counts)]
    row = pl.BlockSpec((tm, N), lambda i, s: (i, 0))
    vec = pl.BlockSpec((1, N), lambda i, s: (0, 0))
    return _call(
        body, name=name, grid=(T // tm, nsteps),
        in_specs=a_specs + [pl.BlockSpec((tk, N), lambda i, s: (s, 0)), row, row, vec], out_specs=[row, row, vec],
        out_shape=[jax.ShapeDtypeStruct((T, N), F32), jax.ShapeDtypeStruct((T, N), BF16), jax.ShapeDtypeStruct((1, N), F32)],
        scratch_shapes=[pltpu.VMEM((tm, N), F32)], compiler_params=_params("arbitrary", "arbitrary"),
    )(*parts, b, x, resid, g)


def _norm_proj(x, g, wT, name, tm=1024, tn=1280):
    T, K = x.shape
    N = wT.shape[0]

    def body(x_ref, g_ref, w_ref, o_ref, h_ref):
        xv = x_ref[...]
        r = lax.rsqrt(jnp.mean(xv * xv, axis=-1, keepdims=True) + EPS)
        hv = (xv * r * g_ref[...]).astype(BF16)

        @pl.when(pl.program_id(1) == 0)
        def _():
            h_ref[...] = hv

        o_ref[...] = _dot(hv, w_ref[...], _NT)

    return _call(
        body, name=name, grid=(T // tm, N // tn),
        in_specs=[pl.BlockSpec((tm, K), lambda i, j: (i, 0)), pl.BlockSpec((1, K), lambda i, j: (0, 0)),
                  pl.BlockSpec((tn, K), lambda i, j: (j, 0))],
        out_specs=[pl.BlockSpec((tm, tn), lambda i, j: (i, j)), pl.BlockSpec((tm, K), lambda i, j: (i, 0))],
        out_shape=[jax.ShapeDtypeStruct((T, N), F32), jax.ShapeDtypeStruct((T, K), BF16)],
        compiler_params=_params("parallel", "arbitrary"),
    )(x, g, wT)


def _qk_prep(proj, pos, invf, qg, kg, bd, name, tm=512):
    T = proj.shape[0]

    def body(q_ref, k_ref, pos_ref, invf_ref, qg_ref, kg_ref, bd_ref, qo_ref, ko_ref):
        cos, sin = _rope_tables(pos_ref, invf_ref)

        def prep(xv, gv, scale):
            r = lax.rsqrt(_group_mean(xv * xv, bd_ref[...]) + EPS)
            yv = xv * r * gv
            return ((yv * cos + _rot_half(yv) * sin) * scale).astype(BF16).astype(F32)

        qo_ref[...] = prep(q_ref[...], qg_ref[...], HEAD_DIM ** -0.5)
        ko_ref[...] = prep(k_ref[...], kg_ref[...], 1.0)

    col = lambda j: pl.BlockSpec((tm, ATTN_W), lambda i, j=j: (i, j))
    vec = pl.BlockSpec((1, ATTN_W), lambda i: (0, 0))
    out = pl.BlockSpec((tm, ATTN_W), lambda i: (i, 0))
    return _call(
        body, name=name, grid=(T // tm,),
        in_specs=[col(0), col(1), pl.BlockSpec((tm, 1), lambda i: (i, 0)), vec, vec, vec,
                  pl.BlockSpec((2 * HEAD_DIM, 2 * HEAD_DIM), lambda i: (0, 0))],
        out_specs=[out, out], out_shape=[jax.ShapeDtypeStruct((T, ATTN_W), F32)] * 2,
        compiler_params=_params("parallel"),
    )(proj, proj, pos, invf, qg, kg, bd)


def _qk_prep_bwd(proj, dqh, dkh, dv, pos, invf, qg, kg, bd, name, tm=512):
    T = proj.shape[0]

    def body(q_ref, k_ref, dq_ref, dk_ref, dv_ref, pos_ref, invf_ref, qg_ref, kg_ref, bd_ref, o_ref, gq_ref, gk_ref):
        @pl.when(pl.program_id(0) == 0)
        def _():
            gq_ref[...] = jnp.zeros_like(gq_ref)
            gk_ref[...] = jnp.zeros_like(gk_ref)

        cos, sin = _rope_tables(pos_ref, invf_ref)

        def back(xv, gv, dz, scale):
            dz = dz * scale
            dy = dz * cos - _rot_half(dz * sin)
            r = lax.rsqrt(_group_mean(xv * xv, bd_ref[...]) + EPS)
            gd = dy * gv
            m = _group_mean(gd * xv, bd_ref[...])
            dx = r * gd - xv * (r * r * r) * m
            return dx, jnp.sum(dy * xv * r, axis=0, keepdims=True)

        dxq, gs = back(q_ref[...], qg_ref[...], dq_ref[...], HEAD_DIM ** -0.5)
        gq_ref[...] += gs
        dxk, gs = back(k_ref[...], kg_ref[...], dk_ref[...], 1.0)
        gk_ref[...] += gs
        o_ref[...] = jnp.concatenate([dxq.astype(BF16), dxk.astype(BF16), dv_ref[...].astype(BF16)], axis=1)

    col = lambda j: pl.BlockSpec((tm, ATTN_W), lambda i, j=j: (i, j))
    row = pl.BlockSpec((tm, ATTN_W), lambda i: (i, 0))
    vec = pl.BlockSpec((1, ATTN_W), lambda i: (0, 0))
    return _call(
        body, name=name, grid=(T // tm,),
        in_specs=[col(0), col(1), row, row, row, pl.BlockSpec((tm, 1), lambda i: (i, 0)), vec, vec, vec,
                  pl.BlockSpec((2 * HEAD_DIM, 2 * HEAD_DIM), lambda i: (0, 0))],
        out_specs=[pl.BlockSpec((tm, 3 * ATTN_W), lambda i: (i, 0)), vec, vec],
        out_shape=[jax.ShapeDtypeStruct((T, 3 * ATTN_W), BF16)] + [jax.ShapeDtypeStruct((1, ATTN_W), F32)] * 2,
        compiler_params=_params("arbitrary"),
    )(proj, proj, dqh, dkh, dv, pos, invf, qg, kg, bd)


def _ld(ref, start, size, dil):
    return ref[pl.ds(start, size), :] if dil == 1 else ref[pl.ds(start, size, stride=dil), :]


def _st(ref, start, size, dil, val):
    if dil == 1:
        ref[pl.ds(start, size), :] = val
    else:
        ref[pl.ds(start, size, stride=dil), :] = val


def _attn_geometry(T, dil):
    nb = T // dil // QBLK
    if nb == 2:
        return 1, 2 * QBLK, 2 * QBLK
    return nb, QBLK, (2 * QBLK if nb >= 2 else QBLK)


def _attn_unroll(qb):
    return 4


def _attn_unit(j, u, dil, nit, unroll):
    return unroll * j + u if dil >= unroll else j + u * (nit // unroll)


def _attn_block(it, dil, qb, kw):
    c, n = it & (dil - 1), lax.shift_right_logical(it, dil.bit_length() - 1)
    sq = n * (qb * dil) + c
    sk = jnp.maximum(n - (kw // qb - 1), 0) * (qb * dil) + c
    qi = lax.broadcasted_iota(jnp.int32, (2 * qb, kw), 0) & (qb - 1)
    kj = lax.broadcasted_iota(jnp.int32, (2 * qb, kw), 1)
    rel = jnp.where(n > 0, kw - qb, 0) + qi - kj
    return sq, sk, (rel >= 0) & (rel <= QBLK)


def _stack_heads(xv, head0):
    z = jnp.zeros_like(xv)
    return jnp.concatenate([jnp.where(head0, xv, z), jnp.where(head0, z, xv)], axis=0)


def _unstack_heads(x2, head0):
    qb = x2.shape[0] // 2
    return jnp.where(head0, x2[:qb], x2[qb:])


def _attn_fwd(qf, kf, proj, name):
    T = qf.shape[0]

    def body(q_ref, k_ref, v_ref, o_ref, lse_ref):
        for bi, dil in enumerate(DILATIONS):
            nb, qb, kw = _attn_geometry(T, dil)
            nit = nb * dil
            head0 = lax.broadcasted_iota(jnp.int32, (qb, 2 * HEAD_DIM), 1) < HEAD_DIM

            def step(j, carry, bi=bi, dil=dil, qb=qb, kw=kw, nit=nit, head0=head0):
                units = []
                for u in range(_attn_unroll(qb)):
                    sq, sk, ok = _attn_block(_attn_unit(j, u, dil, nit, _attn_unroll(qb)), dil, qb, kw)
                    old = (_ld(o_ref, sq, qb, dil), _ld(lse_ref, sq, qb, dil)) if bi > 0 else None
                    units.append((sq, ok, _ld(q_ref, sq, qb, dil).astype(BF16), _ld(k_ref, sk, kw, dil).astype(BF16),
                                  _ld(v_ref, sk, kw, dil).astype(BF16), old))
                results = []
                for sq, ok, qv, kv, vv, old in units:
                    s = jnp.where(ok, _dot(_stack_heads(qv, head0), kv, _NT), NEG_INF)
                    m = jnp.max(s, axis=-1, keepdims=True)
                    p = jnp.exp(s - m).astype(BF16)
                    acc = _dot(p, jnp.concatenate([vv, jnp.ones_like(vv)], axis=1))
                    l = acc[:, 2 * HEAD_DIM:]
                    o_new = _unstack_heads(acc[:, :2 * HEAD_DIM] / l, head0)
                    l_new = _unstack_heads(m + jnp.log(l), head0)
                    if bi > 0:
                        o_old, l_old = old
                        mx = jnp.maximum(l_old, l_new)
                        e0, e1 = jnp.exp(l_old - mx), jnp.exp(l_new - mx)
                        z = e0 + e1
                        o_new = (e0 * o_old + e1 * o_new) / z
                        l_new = mx + jnp.log(z)
                    results.append((sq, o_new, l_new))
                for sq, o_new, l_new in results:
                    _st(o_ref, sq, qb, dil, o_new)
                    _st(lse_ref, sq, qb, dil, l_new)
                return carry

            lax.fori_loop(0, nit // _attn_unroll(qb), step, 0)

    blk = lambda off: pl.BlockSpec((T, 2 * HEAD_DIM), lambda hp, off=off: (0, off + hp))
    return _call(
        body, name=name, grid=(4,), in_specs=[blk(0), blk(0), blk(8)], out_specs=[blk(0), blk(0)],
        out_shape=[jax.ShapeDtypeStruct((T, ATTN_W), F32)] * 2, compiler_params=_params("parallel"),
    )(qf, kf, proj)


def _attn_bwd(qf, kf, proj, do, lse, delta, name):
    T = qf.shape[0]

    def body(q_ref, k_ref, v_ref, do_ref, lse_ref, dl_ref, dq_ref, dk_ref, dv_ref):
        for ref in (dq_ref, dk_ref, dv_ref):
            ref[...] = jnp.zeros_like(ref)
        for dil in DILATIONS:
            nb, qb, kw = _attn_geometry(T, dil)
            nit = nb * dil
            head0 = lax.broadcasted_iota(jnp.int32, (qb, 2 * HEAD_DIM), 1) < HEAD_DIM

            def step(j, carry, dil=dil, qb=qb, kw=kw, nit=nit, head0=head0):
                units = []
                for u in range(_attn_unroll(qb)):
                    sq, sk, ok = _attn_block(_attn_unit(j, u, dil, nit, _attn_unroll(qb)), dil, qb, kw)
                    lsev, dlv = _ld(lse_ref, sq, qb, dil), _ld(dl_ref, sq, qb, dil)
                    units.append((sq, sk, ok, _ld(q_ref, sq, qb, dil).astype(BF16), _ld(do_ref, sq, qb, dil).astype(BF16),
                                  jnp.concatenate([lsev[:, 0:1], lsev[:, HEAD_DIM:HEAD_DIM + 1]], axis=0),
                                  jnp.concatenate([dlv[:, 0:1], dlv[:, HEAD_DIM:HEAD_DIM + 1]], axis=0),
                                  _ld(k_ref, sk, kw, dil).astype(BF16), _ld(v_ref, sk, kw, dil).astype(BF16),
                                  _ld(dq_ref, sq, qb, dil), _ld(dk_ref, sk, kw, dil), _ld(dv_ref, sk, kw, dil)))
                results = []
                for sq, sk, ok, qv, dov, lse2, dl2, kv, vv, dq0, dk0, dv0 in units:
                    q2, do2 = _stack_heads(qv, head0), _stack_heads(dov, head0)
                    p = jnp.where(ok, jnp.exp(_dot(q2, kv, _NT) - lse2), 0.0)
                    ds = (p * (_dot(do2, vv, _NT) - dl2)).astype(BF16)
                    results.append((sq, sk, dq0 + _unstack_heads(_dot(ds, kv), head0),
                                    dk0 + _dot(ds, q2, _TN), dv0 + _dot(p.astype(BF16), do2, _TN)))
                for sq, sk, dq, dk, dv in results:
                    _st(dq_ref, sq, qb, dil, dq)
                    _st(dk_ref, sk, kw, dil, dk)
                    _st(dv_ref, sk, kw, dil, dv)
                return carry

            lax.fori_loop(0, nit // _attn_unroll(qb), step, 0)

    blk = lambda off: pl.BlockSpec((T, 2 * HEAD_DIM), lambda hp, off=off: (0, off + hp))
    return _call(
        body, name=name, grid=(4,), in_specs=[blk(0), blk(0), blk(8), blk(0), blk(0), blk(0)], out_specs=[blk(0)] * 3,
        out_shape=[jax.ShapeDtypeStruct((T, ATTN_W), F32)] * 3, compiler_params=_params("parallel"),
    )(qf, kf, proj, do, lse, delta)


def _attn_norm(attn, g, name, tm=512):
    T = attn.shape[0]

    def body(a_ref, g_ref, o_ref):
        av = a_ref[...]
        r = lax.rsqrt(jnp.mean(av * av, axis=-1, keepdims=True) + EPS)
        o_ref[...] = (av * r * g_ref[...]).astype(BF16)

    row = pl.BlockSpec((tm, ATTN_W), lambda i: (i, 0))
    return _call(
        body, name=name, grid=(T // tm,), in_specs=[row, pl.BlockSpec((1, ATTN_W), lambda i: (0, 0))], out_specs=row,
        out_shape=jax.ShapeDtypeStruct((T, 2 * ATTN_W), BF16), compiler_params=_params("parallel"),
    )(attn, g)


def _attn_norm_bwd(dmix, attn, g, bd, name, tm=512):
    T = attn.shape[0]

    def body(d_ref, a_ref, g_ref, bd_ref, do_ref, dl_ref, dg_ref):
        @pl.when(pl.program_id(0) == 0)
        def _():
            dg_ref[...] = jnp.zeros_like(dg_ref)

        dy, av = d_ref[...], a_ref[...]
        r = lax.rsqrt(jnp.mean(av * av, axis=-1, keepdims=True) + EPS)
        gd = dy * g_ref[...]
        m = jnp.mean(gd * av, axis=-1, keepdims=True)
        da = r * gd - av * (r * r * r) * m
        do_ref[...] = da
        dl_ref[...] = _group_mean(da * av, bd_ref[...]) * float(HEAD_DIM)
        dg_ref[...] += jnp.sum(dy * av * r, axis=0, keepdims=True)

    row = pl.BlockSpec((tm, ATTN_W), lambda i: (i, 0))
    vec = pl.BlockSpec((1, ATTN_W), lambda i: (0, 0))
    return _call(
        body, name=name, grid=(T // tm,),
        in_specs=[row, row, vec, pl.BlockSpec((2 * HEAD_DIM, 2 * HEAD_DIM), lambda i: (0, 0))], out_specs=[row, row, vec],
        out_shape=[jax.ShapeDtypeStruct((T, ATTN_W), F32)] * 2 + [jax.ShapeDtypeStruct((1, ATTN_W), F32)],
        compiler_params=_params("arbitrary"),
    )(dmix, attn, g, bd)


def _rec_gates(xc, wrg_ref, wig_ref, brg_ref, big_ref, lam_ref):
    xb = xc.astype(BF16)
    r = _sigmoid(_dot(xb, wrg_ref[...]) + brg_ref[...])
    ig = _sigmoid(_dot(xb, wig_ref[...]) + big_ref[...])
    sp = _softplus_neg(lam_ref[...])
    log_a = -LRU_C * r * sp
    a = jnp.exp(log_a)
    th = jnp.tanh(log_a)
    mult = jnp.sqrt(-2.0 * th / (1.0 - th))
    return xb, r, ig, sp, a, mult


def _rec_fwd(proj, mix, cw, cb, wrg, wig, brg, big, lam, g, name, tm=256):
    T = proj.shape[0]
    hb = tm // 8

    def body(xr_ref, halo_ref, gr_ref, cw_ref, cb_ref, wrg_ref, wig_ref, brg_ref, big_ref, lam_ref, g_ref, mix_ref,
             xc_ref, h_ref, out_ref, carry):
        i = pl.program_id(0)

        @pl.when(i == 0)
        def _():
            carry[...] = jnp.zeros_like(carry)

        xr = xr_ref[...]
        halo = jnp.where(i > 0, halo_ref[...], 0.0)
        xc = cb_ref[...] + cw_ref[3:4, :] * xr
        for s in range(1, REC_CONV):
            xc = xc + cw_ref[3 - s:4 - s, :] * _shift_down(xr, halo, s)
        xc_ref[...] = xc
        _, _, ig, _, a, mult = _rec_gates(xc, wrg_ref, wig_ref, brg_ref, big_ref, lam_ref)
        pa, hl = _scan_fwd(a, mult * (ig * xc))
        h = hl + pa * carry[0:1, :]
        h_ref[...] = h
        carry[0:1, :] = h_ref[pl.ds(tm - 1, 1), :]
        hg = h * _gelu(gr_ref[...])
        r = lax.rsqrt(jnp.mean(hg * hg, axis=-1, keepdims=True) + EPS)
        out_ref[...] = (hg * r * g_ref[...]).astype(BF16)

    vec = pl.BlockSpec((1, REC_W), lambda i: (0, 0))
    row = pl.BlockSpec((tm, REC_W), lambda i: (i, 0))
    mat = pl.BlockSpec((REC_W, REC_W), lambda i: (0, 0))
    return _call(
        body, name=name, grid=(T // tm,),
        in_specs=[pl.BlockSpec((tm, REC_W), lambda i: (i, 3)),
                  pl.BlockSpec((8, REC_W), lambda i: (jnp.maximum(i * hb - 1, 0), 3)),
                  pl.BlockSpec((tm, REC_W), lambda i: (i, 4)),
                  pl.BlockSpec((8, REC_W), lambda i: (0, 0)), vec, mat, mat, vec, vec, vec, vec, ANY],
        out_specs=[row, row, pl.BlockSpec((tm, REC_W), lambda i: (i, 1))],
        out_shape=[jax.ShapeDtypeStruct((T, REC_W), F32)] * 2 + [jax.ShapeDtypeStruct(mix.shape, BF16)],
        scratch_shapes=[pltpu.VMEM((8, REC_W), F32)], input_output_aliases={11: 2},
        compiler_params=_params("arbitrary"),
    )(proj, proj, proj, cw, cb, wrg, wig, brg, big, lam, g, mix)


def _rec_bwd(dmix, proj, xc, h, cw, cb, wrg, wig, brg, big, lam, g, name, tm=256):
    T = proj.shape[0]
    nt = T // tm
    hb = tm // 8

    def body(d_ref, xr_ref, xhalo_ref, gr_ref, xc_ref, h_ref, hhalo_ref, cw_ref, cb_ref, wrg_ref, wig_ref, brg_ref,
             big_ref, lam_ref, g_ref,
             drec_ref, gcw_ref, gcb_ref, gwrg_ref, gwig_ref, gbrg_ref, gbig_ref, glam_ref, gg_ref,
             g_carry, a_first, dxc_next, gsp):
        i = pl.program_id(0)
        first_tile = i == nt - 1

        @pl.when(i == 0)
        def _():
            for ref in (gcw_ref, gcb_ref, gwrg_ref, gwig_ref, gbrg_ref, gbig_ref, glam_ref, gg_ref,
                        g_carry, a_first, dxc_next, gsp):
                ref[...] = jnp.zeros_like(ref)

        xr, xc, hv = xr_ref[...], xc_ref[...], h_ref[...]
        xhalo = jnp.where(first_tile, 0.0, xhalo_ref[...])
        hhalo = jnp.where(first_tile, 0.0, hhalo_ref[...])
        xb, r, ig, sp, a, mult = _rec_gates(xc, wrg_ref, wig_ref, brg_ref, big_ref, lam_ref)
        h_prev = _shift_down(hv, hhalo, 1)
        ge, dge = _gelu_and_grad(gr_ref[...])
        hg = hv * ge
        rr = lax.rsqrt(jnp.mean(hg * hg, axis=-1, keepdims=True) + EPS)
        dy = d_ref[...]
        gd = dy * g_ref[...]
        dhg = rr * gd - hg * (rr * rr * rr) * jnp.mean(gd * hg, axis=-1, keepdims=True)
        gg_ref[...] += jnp.sum(dy * hg * rr, axis=0, keepdims=True)
        dgr = (dhg * hv * dge).astype(BF16)
        dh = dhg * ge
        b = _shift_up(a, jnp.broadcast_to(a_first[0:1, :], (8, REC_W)), 1)
        pb, gl = _scan_bwd(b, dh)
        gs = gl + pb * g_carry[0:1, :]
        g_carry[0:1, :] = gs[0:1, :]
        a_first[0:1, :] = a[0:1, :]
        da = gs * h_prev
        dmult = gs * (ig * xc)
        di = gs * (mult * xc)
        dxc = gs * (mult * ig)
        dlog_a = da * a - dmult * (a * a) / mult
        gsp[...] += jnp.sum(dlog_a * (-LRU_C * r), axis=0, keepdims=True)
        dzr = (dlog_a * (-LRU_C * sp)) * (r * (1.0 - r))
        dzi = di * (ig * (1.0 - ig))
        dzr_b, dzi_b = dzr.astype(BF16), dzi.astype(BF16)
        dxc = dxc + _dot(dzr_b, wrg_ref[...], _NT) + _dot(dzi_b, wig_ref[...], _NT)
        gwrg_ref[...] += _dot(xb, dzr_b, _TN)
        gwig_ref[...] += _dot(xb, dzi_b, _TN)
        gbrg_ref[...] += jnp.sum(dzr, axis=0, keepdims=True)
        gbig_ref[...] += jnp.sum(dzi, axis=0, keepdims=True)
        nxt = dxc_next[...]
        dxr = cw_ref[3:4, :] * dxc
        gcw_ref[3:4, :] += jnp.sum(dxc * xr, axis=0, keepdims=True)
        for s in range(1, REC_CONV):
            dxr = dxr + cw_ref[3 - s:4 - s, :] * _shift_up(dxc, nxt, s)
            gcw_ref[3 - s:4 - s, :] += jnp.sum(dxc * _shift_down(xr, xhalo, s), axis=0, keepdims=True)
        gcb_ref[...] += jnp.sum(dxc, axis=0, keepdims=True)
        dxc_next[...] = dxc[:8]
        drec_ref[...] = jnp.concatenate([dxr.astype(BF16), dgr], axis=1)

        @pl.when(first_tile)
        def _():
            glam_ref[...] = gsp[...] * (-_sigmoid(-lam_ref[...]))

    rev = lambda i: nt - 1 - i
    vec = pl.BlockSpec((1, REC_W), lambda i: (0, 0))
    row = pl.BlockSpec((tm, REC_W), lambda i: (rev(i), 0))
    mat = pl.BlockSpec((REC_W, REC_W), lambda i: (0, 0))
    cwb = pl.BlockSpec((8, REC_W), lambda i: (0, 0))
    halo = lambda c: pl.BlockSpec((8, REC_W), lambda i, c=c: (jnp.maximum(rev(i) * hb - 1, 0), c))
    return _call(
        body, name=name, grid=(nt,),
        in_specs=[pl.BlockSpec((tm, REC_W), lambda i: (rev(i), 1)),
                  pl.BlockSpec((tm, REC_W), lambda i: (rev(i), 3)), halo(3),
                  pl.BlockSpec((tm, REC_W), lambda i: (rev(i), 4)),
                  row, row, halo(0), cwb, vec, mat, mat, vec, vec, vec, vec],
        out_specs=[pl.BlockSpec((tm, 2 * REC_W), lambda i: (rev(i), 0)), cwb, vec, mat, mat, vec, vec, vec, vec],
        out_shape=[jax.ShapeDtypeStruct((T, 2 * REC_W), BF16)]
        + [jax.ShapeDtypeStruct((8, REC_W), F32), jax.ShapeDtypeStruct((1, REC_W), F32)]
        + [jax.ShapeDtypeStruct((REC_W, REC_W), F32)] * 2 + [jax.ShapeDtypeStruct((1, REC_W), F32)] * 4,
        scratch_shapes=[pltpu.VMEM((8, REC_W), F32)] * 3 + [pltpu.VMEM((1, REC_W), F32)],
        compiler_params=_params("arbitrary"),
    )(dmix, proj, proj, proj, xc, h, h, cw, cb, wrg, wig, brg, big, lam, g)


def _ffn_conv(x_ext, cw_ref, cb_ref):
    return (cb_ref[...] + cw_ref[2:3, :] * x_ext + cw_ref[1:2, :] * pltpu.roll(x_ext, 1, 0)
            + cw_ref[0:1, :] * pltpu.roll(x_ext, 2, 0))


def _up_proj_act(x2, g, w_upT, cw, cb, name, tm=1024, tc=768):
    T = x2.shape[0]
    nc = D_FF // tc

    def body(x_ref, g_ref, wg_ref, wu_ref, cwg_ref, cwu_ref, cbg_ref, cbu_ref, act_ref, da_ref, db_ref, pg_ref, pu_ref,
             h_ref, hist_g, hist_u, hs):
        i, j = pl.program_id(0), pl.program_id(1)

        @pl.when(j == 0)
        def _():
            xv = x_ref[...]
            r = lax.rsqrt(jnp.mean(xv * xv, axis=-1, keepdims=True) + EPS)
            hs[...] = (xv * r * g_ref[...]).astype(BF16)
            h_ref[...] = hs[...]

        hv = hs[...]
        pg, pu = _dot(hv, wg_ref[...], _NT), _dot(hv, wu_ref[...], _NT)
        ge = jnp.concatenate([jnp.where(i > 0, hist_g[j], 0.0), pg], axis=0)
        ue = jnp.concatenate([jnp.where(i > 0, hist_u[j], 0.0), pu], axis=0)
        gel, dgel = _gelu_and_grad(_ffn_conv(ge, cwg_ref, cbg_ref)[8:])
        uu = _ffn_conv(ue, cwu_ref, cbu_ref)[8:]
        act_ref[...] = (gel * uu).astype(BF16)
        da_ref[...] = (uu * dgel).astype(BF16)
        db_ref[...] = gel.astype(BF16)
        pg_ref[...] = pg.astype(BF16)
        pu_ref[...] = pu.astype(BF16)
        hist_g[j] = pg[tm - 8:]
        hist_u[j] = pu[tm - 8:]

    tile = pl.BlockSpec((tm, tc), lambda i, j: (i, j))
    wsp = lambda off: pl.BlockSpec((tc, D_MODEL), lambda i, j, off=off: (j + off, 0))
    cws = lambda off: pl.BlockSpec((8, tc), lambda i, j, off=off: (0, j + off))
    cbs = lambda off: pl.BlockSpec((1, tc), lambda i, j, off=off: (0, j + off))
    return _call(
        body, name=name, grid=(T // tm, nc),
        in_specs=[pl.BlockSpec((tm, D_MODEL), lambda i, j: (i, 0)), pl.BlockSpec((1, D_MODEL), lambda i, j: (0, 0)),
                  wsp(0), wsp(nc), cws(0), cws(nc), cbs(0), cbs(nc)],
        out_specs=[tile] * 5 + [pl.BlockSpec((tm, D_MODEL), lambda i, j: (i, 0))],
        out_shape=[jax.ShapeDtypeStruct((T, D_FF), BF16)] * 5 + [jax.ShapeDtypeStruct((T, D_MODEL), BF16)],
        scratch_shapes=[pltpu.VMEM((nc, 8, tc), F32)] * 2 + [pltpu.VMEM((tm, D_MODEL), BF16)],
        compiler_params=_params("arbitrary", "arbitrary"),
    )(x2, g, w_upT, w_upT, cw, cw, cb, cb)


def _ffn_bwd(dyb, w_down, da, db, pg, pu, cw, name, tm=1024, tc=768):
    T, F = pg.shape
    nt = T // tm
    hb16 = tm // 16
    nc = F // tc
    n = tm + 8

    def body(dy_ref, dyn_ref, wd_ref, a_ref, an_ref, b_ref, bn_ref, g_ref, u_ref, cwg_ref, cwu_ref,
             dg_ref, du_ref, gcwg_ref, gcwu_ref, gcbg_ref, gcbu_ref):
        i = pl.program_id(1)
        last = i == nt - 1

        @pl.when(i == 0)
        def _():
            for ref in (gcwg_ref, gcwu_ref, gcbg_ref, gcbu_ref):
                ref[...] = jnp.zeros_like(ref)

        wd = wd_ref[...]
        dact_next = jnp.where(last, 0.0, _dot(dyn_ref[...], wd, _NT)[:8])
        de = jnp.concatenate([_dot(dy_ref[...], wd, _NT), dact_next], axis=0)
        ext = lambda t, nx: jnp.concatenate([t[...].astype(F32), nx[...].astype(F32)[:8]], axis=0)
        for dcv, x_ref, cw_ref, dx_ref, gcw_ref, gcb_ref in ((de * ext(a_ref, an_ref), g_ref, cwg_ref, dg_ref, gcwg_ref, gcbg_ref),
                                                               (de * ext(b_ref, bn_ref), u_ref, cwu_ref, du_ref, gcwu_ref, gcbu_ref)):
            s1, s2 = pltpu.roll(dcv, n - 1, 0), pltpu.roll(dcv, n - 2, 0)
            dx_ref[...] = (cw_ref[2:3, :] * dcv + cw_ref[1:2, :] * s1 + cw_ref[0:1, :] * s2)[:tm].astype(BF16)
            xv = x_ref[...].astype(F32)
            gcw_ref[2:3, :] += jnp.sum(xv * dcv[:tm], axis=0, keepdims=True)
            gcw_ref[1:2, :] += jnp.sum(xv * s1[:tm], axis=0, keepdims=True)
            gcw_ref[0:1, :] += jnp.sum(xv * s2[:tm], axis=0, keepdims=True)
            gcb_ref[...] += jnp.sum(dcv[:tm], axis=0, keepdims=True)

    tile = pl.BlockSpec((tm, tc), lambda j, i: (i, j))
    nxt = pl.BlockSpec((16, tc), lambda j, i: (jnp.minimum((i + 1) * hb16, nt * hb16 - 1), j))
    cws = lambda off: pl.BlockSpec((8, tc), lambda j, i, off=off: (0, j + off))
    cbs = pl.BlockSpec((1, tc), lambda j, i: (0, j))
    return _call(
        body, name=name, grid=(nc, nt),
        in_specs=[pl.BlockSpec((tm, D_MODEL), lambda j, i: (i, 0)),
                  pl.BlockSpec((16, D_MODEL), lambda j, i: (jnp.minimum((i + 1) * hb16, nt * hb16 - 1), 0)),
                  pl.BlockSpec((tc, D_MODEL), lambda j, i: (j, 0)), tile, nxt, tile, nxt, tile, tile, cws(0), cws(nc)],
        out_specs=[tile, tile, cws(0), cws(0), cbs, cbs],
        out_shape=[jax.ShapeDtypeStruct((T, F), BF16)] * 2 + [jax.ShapeDtypeStruct((8, F), F32)] * 2
        + [jax.ShapeDtypeStruct((1, F), F32)] * 2,
        compiler_params=_params("parallel", "arbitrary"),
    )(dyb, dyb, w_down, da, da, db, db, pg, pu, cw, cw)


def _adam_update(w, g, m, v):
    m2 = ADAM_B1 * m + (1.0 - ADAM_B1) * g
    v2 = ADAM_B2 * v + (1.0 - ADAM_B2) * (g * g)
    m_hat = m2 / (1.0 - ADAM_B1 ** ADAM_STEP)
    v_hat = v2 / (1.0 - ADAM_B2 ** ADAM_STEP)
    delta = -ADAM_LR * (m_hat / (jnp.sqrt(v_hat) + ADAM_EPS) + ADAM_WD * w)
    return delta, m2, v2


def _adam_sharded(p, r2, idx, w, m, v, name, transposed=False):
    r, n = p.shape[1:]
    nrecv = r2.shape[0]
    tr = (256 if r % 256 == 0 else r) if transposed else _row_tile(r)

    def body(c_ref, p_ref, r_ref, w_ref, m_ref, v_ref, g_ref, d_ref, m2_ref, v2_ref):
        g = p_ref[...].astype(F32)
        for k in range(nrecv):
            g = g + r_ref[k].astype(F32)
        if transposed:
            g = g.T
        g_ref[...] = g
        d_ref[...], m2_ref[...], v2_ref[...] = _adam_update(w_ref[...], g, m_ref[...], v_ref[...])

    blk = pl.BlockSpec((n, tr), lambda i, c_ref: (0, i)) if transposed else pl.BlockSpec((tr, n), lambda i, c_ref: (i, 0))
    spec = pltpu.PrefetchScalarGridSpec(
        num_scalar_prefetch=1, grid=(r // tr,),
        in_specs=[pl.BlockSpec((None, tr, n), lambda i, c_ref: (c_ref[0], i, 0)),
                  pl.BlockSpec((nrecv, tr, n), lambda i, c_ref: (0, i, 0)), blk, blk, blk],
        out_specs=[blk] * 4)
    return _call(body, name=name, grid_spec=spec, out_shape=[jax.ShapeDtypeStruct(w.shape, F32)] * 4,
                 compiler_params=_params("parallel"))(idx, p, r2, w, m, v)


def _sum_slabs(p, r2, idx, name):
    _, r, n = p.shape

    def body(c_ref, p_ref, r_ref, o_ref):
        acc = p_ref[...]
        for k in range(N_PEERS):
            acc = acc + r_ref[k]
        o_ref[...] = acc

    spec = pltpu.PrefetchScalarGridSpec(
        num_scalar_prefetch=1, grid=(1,),
        in_specs=[pl.BlockSpec((None, r, n), lambda i, c_ref: (c_ref[0], 0, 0)),
                  pl.BlockSpec((N_PEERS, r, n), lambda i, c_ref: (0, 0, 0))],
        out_specs=pl.BlockSpec((r, n), lambda i, c_ref: (0, 0)))
    return _call(body, name=name, grid_spec=spec, out_shape=jax.ShapeDtypeStruct((r, n), F32))(idx, p, r2)


def _adam_small(ws, gs, ms, vs, name):
    n = len(ws)

    def body(*refs):
        for i in range(n):
            d, m2, v2 = _adam_update(refs[i][...], refs[n + i][...], refs[2 * n + i][...], refs[3 * n + i][...])
            refs[4 * n + i][...] = d
            refs[5 * n + i][...] = m2
            refs[6 * n + i][...] = v2

    outs = _call(body, name=name, out_shape=[jax.ShapeDtypeStruct(w.shape, F32) for w in ws] * 3)(*ws, *gs, *ms, *vs)
    return outs[:n], outs[n:2 * n], outs[2 * n:]


def _pack_small_grads(full, halves, rcw, fcwg, fcwu, wrg, wig, lparts, name):
    nf, nh = len(full), len(halves)

    def body(*refs):
        o = refs[-1]
        o[...] = jnp.zeros_like(o)
        row = 0
        for r in refs[:nf]:
            for j in range(r.shape[1] // 1024):
                o[row:row + 1, :] = r[:, 1024 * j:1024 * (j + 1)]
                row += 1
        for k in range(0, nh, 2):
            o[row:row + 1, 0:512] = refs[nf + k][...]
            o[row:row + 1, 512:1024] = refs[nf + k + 1][...]
            row += 1
        rcw_ref, fg_ref, fu_ref, wrg_ref, wig_ref, l_ref = refs[nf + nh:nf + nh + 6]
        for k in range(2):
            o[row:row + 1, 0:512] = rcw_ref[2 * k:2 * k + 1, :]
            o[row:row + 1, 512:1024] = rcw_ref[2 * k + 1:2 * k + 2, :]
            row += 1
        for f_ref in (fg_ref, fu_ref):
            for k in range(FFN_CONV):
                for j in range(D_FF // 1024):
                    o[row:row + 1, :] = f_ref[k:k + 1, 1024 * j:1024 * (j + 1)]
                    row += 1
        assert row == 32
        for n in range(8):
            o[32:96, 64 * n:64 * n + 64] = wrg_ref[64 * n:64 * n + 64, 64 * n:64 * n + 64]
            o[32:96, 512 + 64 * n:512 + 64 * n + 64] = wig_ref[64 * n:64 * n + 64, 64 * n:64 * n + 64]
        o[96:97, :] = jnp.sum(l_ref[...], axis=0, keepdims=True)

    return _call(body, name=name, out_shape=jax.ShapeDtypeStruct((SMALL_ROWS, 1024), F32))(
        *full, *halves, rcw, fcwg, fcwu, wrg, wig, lparts)


def _block_diag(w):
    eye = jnp.eye(8, dtype=w.dtype)
    return (w[:, :, None, :] * eye[:, None, :, None]).reshape(512, 512)


def kernel(x, positions, g_mix, w_in, q_norm_g, k_norm_g, rec_conv_w, rec_conv_b, w_rg, b_rg, w_ig, b_ig, lru_lambda, g_attn_out, g_rec_out, w_out, g_ffn, w_up, ffn_conv_w, ffn_conv_b, w_down, loss_target, m_g_mix, m_w_in, m_q_norm_g, m_k_norm_g, m_rec_conv_w, m_rec_conv_b, m_w_rg, m_b_rg, m_w_ig, m_b_ig, m_lru_lambda, m_g_attn_out, m_g_rec_out, m_w_out, m_g_ffn, m_w_up, m_ffn_conv_w, m_ffn_conv_b, m_w_down, v_g_mix, v_w_in, v_q_norm_g, v_k_norm_g, v_rec_conv_w, v_rec_conv_b, v_w_rg, v_b_rg, v_w_ig, v_b_ig, v_lru_lambda, v_g_attn_out, v_g_rec_out, v_w_out, v_g_ffn, v_w_up, v_ffn_conv_w, v_ffn_conv_b, v_w_down):
    T = x.shape[1]
    ix, iy, ic = lax.axis_index("x"), lax.axis_index("y"), lax.axis_index("c")
    dev = 4 * ix + 2 * iy + ic
    xs = x.reshape(T, D_MODEL)
    tgt = loss_target.reshape(T, D_MODEL)
    pos = positions.reshape(T, 1)

    shards = {"w_in": (w_in[0], m_w_in[0], v_w_in[0]), "w_out": (w_out[0], m_w_out[0], v_w_out[0]),
              "w_up": (w_up[0], m_w_up[0], v_w_up[0]), "w_down": (w_down[0], m_w_down[0], v_w_down[0])}
    taps = jnp.concatenate([rec_conv_w.reshape(-1), ffn_conv_w.reshape(-1), jnp.zeros((4096 - 2560,), F32)]).reshape(8, 512)
    W_inT, taps_all = _all_gather([w_in[0].T.astype(BF16), taps], "ag_w_in")
    late = [w_out[0].astype(BF16), w_up[0].T.astype(BF16), w_down[0].astype(BF16)]
    ag_send, ag_recv, late_thru, land_thru, ag_token = _exchange_start(
        late, [_landing((N_DEV * s.shape[0], 1024), BF16, s, dev * s.shape[0]) for s in late], "gather", taps_all,
        "ag_late_start")
    taps_all = taps_all.reshape(N_DEV, 4096)
    rcw = taps_all[:, :256].reshape(8, 4, 64).transpose(1, 0, 2).reshape(4, REC_W)
    fcw = taps_all[:, 256:2560].reshape(8, 3, 768).transpose(1, 0, 2).reshape(3, 2 * D_FF)
    rcw8 = jnp.pad(rcw, ((0, 4), (0, 0)))
    fcw8 = jnp.pad(fcw, ((0, 5), (0, 0)))
    fcb = ffn_conv_b.reshape(1, 2 * D_FF)

    half = HEAD_DIM // 2
    inv_freq = ROPE_THETA ** (-jnp.arange(half, dtype=F32) / half)
    invf = jnp.tile(inv_freq, 2 * N_HEADS).reshape(1, ATTN_W)
    bd = jnp.asarray(np.kron(np.eye(2), np.full((HEAD_DIM, HEAD_DIM), 1.0 / HEAD_DIM)), BF16)
    qg = jnp.tile(q_norm_g.reshape(HEAD_DIM), N_HEADS).reshape(1, ATTN_W)
    kg = jnp.tile(k_norm_g.reshape(HEAD_DIM), N_HEADS).reshape(1, ATTN_W)
    wrg_bd = _block_diag(w_rg[0]).astype(BF16)
    wig_bd = _block_diag(w_ig[0]).astype(BF16)
    brg, big = b_rg.reshape(1, REC_W), b_ig.reshape(1, REC_W)

    proj, h1 = _norm_proj(xs, g_mix + ag_token[0, 0], W_inT, "in_proj", tn=IN_W)
    qf, kf = _qk_prep(proj, pos, invf, qg, kg, bd, "qk_prep")
    attn, lse = _attn_fwd(qf, kf, proj, "attn_fwd")
    mix = _attn_norm(attn, g_attn_out, "attn_norm")
    xc, hstate, mix = _rec_fwd(proj, mix, rcw8, rec_conv_b, wrg_bd, wig_bd, brg, big, lru_lambda, g_rec_out, "rec_fwd")
    _, (W_out, W_upT, W_down) = _exchange_wait(ag_send, ag_recv, late_thru, land_thru, "gather", hstate, "ag_late_wait")
    x2 = _mm(mix, W_out, "nn", F32, "out_proj", add=xs)

    act, da, db, pg, pu, h2 = _up_proj_act(x2, g_ffn, W_upT, fcw8, fcb, "up_proj_act")
    dy, dyb, lparts = _mm(act, W_down, "nn", F32, "down_proj_loss", add=x2, loss_target=tgt, tm=512, tk=D_FF)

    g_down = _mm(act, dyb, "tn", BF16, "g_w_down", tk=4096)
    dpg, dpu, g_fcwg, g_fcwu, g_fcbg, g_fcbu = _ffn_bwd(dyb, W_down, da, db, pg, pu, fcw8, "ffn_bwd")
    g_upT = _mm(dpg, h2, "tn", BF16, "g_w_up_gate", tk=4096, o_rows=2 * D_FF)
    g_upT = _mm(dpu, h2, "tn", BF16, "g_w_up_up", tk=4096, into=g_upT, o_moff=D_FF // 1024)
    ffn_g = [g_upT.reshape(N_DEV, 2 * D_FF // N_DEV, 1024), g_down.reshape(N_DEV, D_FF // N_DEV, 1024)]
    rs_send, rs_recv, ffn_g, ffn_land, rs_token = _exchange_start(
        ffn_g, [_landing((N_PEERS,) + g.shape[1:], BF16) for g in ffn_g], "scatter", dpu, "rs_ffn_start")
    dx2, dx2b, g_gffn = _mm_norm_bwd([dpg, dpu], W_upT, x2, dy, g_ffn + rs_token[0, 0], "d_h2_norm_bwd", tm=1024, tk=1536)

    dmix = _mm(dx2b, W_out, "nt", F32, "d_mix")
    g_out = _mm(mix, dx2b, "tn", BF16, "g_w_out", tk=4096).reshape(N_DEV, D_MODEL // N_DEV, 1024)
    out_send, out_recv, (g_out,), out_land, out_token = _exchange_start(
        [g_out], [_landing((N_PEERS,) + g_out.shape[1:], BF16)], "scatter", dmix, "rs_out_start")
    do, delta, g_gattn = _attn_norm_bwd(dmix, attn, g_attn_out + out_token[0, 0], bd, "attn_norm_bwd")
    dqh, dkh, dv = _attn_bwd(qf, kf, proj, do, lse, delta, "attn_bwd")
    dqkv, g_qg, g_kg = _qk_prep_bwd(proj, dqh, dkh, dv, pos, invf, qg, kg, bd, "qk_prep_bwd")
    (drec, g_rcw, g_rcb, g_wrg, g_wig, g_brg, g_big, g_lam, g_grec) = _rec_bwd(
        dmix, proj, xc, hstate, rcw8, rec_conv_b, wrg_bd, wig_bd, brg, big, lru_lambda, g_rec_out, "rec_bwd")
    g_inT = _mm(dqkv, h1, "tn", BF16, "g_w_in_qkv", tm=512, tk=4096, o_rows=IN_W)
    g_inT = _mm(drec, h1, "tn", BF16, "g_w_in_rec", tm=512, tk=4096, into=g_inT, o_moff=3 * ATTN_W // 512)
    g_inT = g_inT.reshape(N_DEV, IN_W // N_DEV, 1024)
    in_send, in_recv, (g_inT,), in_land, in_token = _exchange_start(
        [g_inT], [_landing((N_PEERS,) + g_inT.shape[1:], BF16)], "scatter", drec, "rs_in_start")
    grad_x, _, g_gmix = _mm_norm_bwd([dqkv, drec], W_inT, xs, dx2, g_mix + in_token[0, 0], "d_h1_norm_bwd", tm=1024, tk=512)

    flat = _pack_small_grads([g_gmix, g_gffn, g_fcbg, g_fcbu], [g_rcb, g_brg, g_big, g_lam, g_gattn, g_grec, g_qg, g_kg],
                             g_rcw, g_fcwg, g_fcwu, g_wrg, g_wig, lparts.reshape(-1, D_MODEL), "pack_small_grads")
    srows = SMALL_ROWS // N_DEV
    flat = flat.reshape(N_DEV, srows, 1024)
    sm_send, sm_recv, (flat,), sm_land, sm_token = _exchange_start(
        [flat], [_landing((N_PEERS, srows, 1024), F32)], "scatter", grad_x, "ar_small_rs_start")

    devi = jnp.reshape(dev, (1,)).astype(jnp.int32)
    ffn_g, ffn_land = _exchange_wait(rs_send, rs_recv, ffn_g, ffn_land, "scatter", sm_token, "rs_ffn_wait")
    (g_out,), out_land = _exchange_wait(out_send, out_recv, [g_out], out_land, "scatter", sm_token, "rs_out_wait")
    big_out = {"grad": {}, "delta": {}, "new_m": {}, "new_v": {}}

    def adam_big(nm, p, r):
        w_, m_, v_ = shards[nm]
        res = _adam_sharded(p, r, devi, w_, m_, v_, "adam_" + nm, transposed=nm in ("w_in", "w_up"))
        for kind, a in zip(("grad", "delta", "new_m", "new_v"), res):
            big_out[kind][nm] = a[None]
        return res[0]

    last = adam_big("w_up", ffn_g[0], ffn_land[0])
    (flat,), sm_land = _exchange_wait(sm_send, sm_recv, [flat], sm_land, "scatter", last, "ar_small_rs_wait")
    mine = _sum_slabs(flat, sm_land[0], devi, "sum_small_grads")
    sm_send, sm_recv, (mine,), sm_land, sm_token = _exchange_start(
        [mine], [_landing((SMALL_ROWS, 1024), F32, mine, dev * srows)], "gather", last, "ar_small_ag_start")
    adam_big("w_down", ffn_g[1], ffn_land[1])
    last = adam_big("w_out", g_out, out_land[0])
    _, (tot,) = _exchange_wait(sm_send, sm_recv, [mine], sm_land, "gather", last, "ar_small_ag_wait")
    (g_inT,), in_land = _exchange_wait(in_send, in_recv, [g_inT], in_land, "scatter", tot, "rs_in_wait")
    adam_big("w_in", g_inT, in_land[0])

    half = lambda r, h, shape: tot[r, 512 * h:512 * h + 512].reshape(shape)
    blocks = lambda h: tot[32:96, 512 * h:512 * h + 512].reshape(64, 8, 64).transpose(1, 0, 2)[None]
    fcw_full = jnp.concatenate([tot[14:23].reshape(1, 3, D_FF), tot[23:32].reshape(1, 3, D_FF)], axis=2)
    g_small = {
        "g_mix": tot[0:1], "g_ffn": tot[1:2], "ffn_conv_b": tot[2:8].reshape(1, 2 * D_FF),
        "rec_conv_b": half(8, 0, (1, 512)), "b_rg": half(8, 1, (1, 8, 64)), "b_ig": half(9, 0, (1, 8, 64)),
        "lru_lambda": half(9, 1, (1, 512)), "g_attn_out": half(10, 0, (1, 512)), "g_rec_out": half(10, 1, (1, 512)),
        "q_norm_g": half(11, 0, (N_HEADS, HEAD_DIM)).sum(0)[None], "k_norm_g": half(11, 1, (N_HEADS, HEAD_DIM)).sum(0)[None],
        "w_rg": blocks(0), "w_ig": blocks(1),
        "rec_conv_w": lax.dynamic_slice(tot[12:14].reshape(1, 4, REC_W), (0, 0, 64 * dev), (1, 4, 64)),
        "ffn_conv_w": lax.dynamic_slice(fcw_full, (0, 0, 768 * dev), (1, 3, 768))}
    loss = 0.5 / D_MODEL * jnp.sum(tot[96])
    given = dict(rec_conv_w=rec_conv_w, ffn_conv_w=ffn_conv_w,g_mix=g_mix, q_norm_g=q_norm_g, k_norm_g=k_norm_g, rec_conv_b=rec_conv_b, w_rg=w_rg, b_rg=b_rg, w_ig=w_ig,
                 b_ig=b_ig, lru_lambda=lru_lambda, g_attn_out=g_attn_out, g_rec_out=g_rec_out, g_ffn=g_ffn, ffn_conv_b=ffn_conv_b)
    given_m = dict(rec_conv_w=m_rec_conv_w, ffn_conv_w=m_ffn_conv_w, g_mix=m_g_mix, q_norm_g=m_q_norm_g, k_norm_g=m_k_norm_g, rec_conv_b=m_rec_conv_b, w_rg=m_w_rg, b_rg=m_b_rg,
                   w_ig=m_w_ig, b_ig=m_b_ig, lru_lambda=m_lru_lambda, g_attn_out=m_g_attn_out, g_rec_out=m_g_rec_out,
                   g_ffn=m_g_ffn, ffn_conv_b=m_ffn_conv_b)
    given_v = dict(rec_conv_w=v_rec_conv_w, ffn_conv_w=v_ffn_conv_w, g_mix=v_g_mix, q_norm_g=v_q_norm_g, k_norm_g=v_k_norm_g, rec_conv_b=v_rec_conv_b, w_rg=v_w_rg, b_rg=v_b_rg,
                   w_ig=v_w_ig, b_ig=v_b_ig, lru_lambda=v_lru_lambda, g_attn_out=v_g_attn_out, g_rec_out=v_g_rec_out,
                   g_ffn=v_g_ffn, ffn_conv_b=v_ffn_conv_b)
    small = sorted(given)
    ds, m2s, v2s = _adam_small([given[k] for k in small], [g_small[k] for k in small], [given_m[k] for k in small],
                               [given_v[k] for k in small], "adam_small")
    small_out = {"grad": g_small, "delta": dict(zip(small, ds)), "new_m": dict(zip(small, m2s)), "new_v": dict(zip(small, v2s))}

    order = ("g_mix", "w_in", "q_norm_g", "k_norm_g", "rec_conv_w", "rec_conv_b", "w_rg", "b_rg", "w_ig", "b_ig",
             "lru_lambda", "g_attn_out", "g_rec_out", "w_out", "g_ffn", "w_up", "ffn_conv_w", "ffn_conv_b", "w_down")
    outs = [loss, grad_x.reshape(1, T, D_MODEL)]
    for kind in ("grad", "delta", "new_m", "new_v"):
        for name in order:
            outs.append(big_out[kind][name] if name in big_out[kind] else small_out[kind][name])
    return tuple(outs)
```

```python
import math

import numpy as np
import jax
import jax.numpy as jnp
from jax import lax
from jax.experimental import pallas as pl
from jax.experimental.pallas import tpu as pltpu

F32 = jnp.float32
BF16 = jnp.bfloat16

D_MODEL = 1024
HEAD_DIM = 64
ATTN_W = 512
REC_W = 512
N_HEADS = 8
D_FF = 3072
IN_W = 2560
REC_CONV = 4
FFN_CONV = 3
LRU_C = 8.0
ROPE_THETA = 10000.0
EPS = 1e-6
NEG_INF = -1e30
QBLK = 128
DILATIONS = (1, 4, 16)
N_DEV = 8
SMALL_ROWS = 128
ADAM_LR, ADAM_B1, ADAM_B2, ADAM_EPS, ADAM_WD, ADAM_STEP = 0.001, 0.9, 0.999, 1e-08, 0.01, 10
MESH = pl.DeviceIdType.MESH
ANY = pl.BlockSpec(memory_space=pl.ANY)


def _call(body, *, name, **kw):
    return pl.pallas_call(body, name=name, **kw)


def _params(*sem):
    return pltpu.CompilerParams(dimension_semantics=sem, vmem_limit_bytes=56 * 1024 * 1024)


_GELU_C = math.sqrt(2.0 / math.pi)
_GELU_A = 0.044715


def _gelu(x):
    return (0.5 * x) * (1.0 + jnp.tanh(x * (_GELU_C + (_GELU_C * _GELU_A) * (x * x))))


def _gelu_and_grad(x):
    x2 = x * x
    u = 1.0 + jnp.tanh(x * (_GELU_C + (_GELU_C * _GELU_A) * x2))
    hx = 0.5 * x
    return hx * u, 0.5 * u + (hx * ((2.0 - u) * u)) * (_GELU_C + (3.0 * _GELU_C * _GELU_A) * x2)


def _sigmoid(x):
    return 1.0 / (1.0 + jnp.exp(-x))


def _softplus_neg(lam):
    y = jnp.exp(-jnp.abs(lam))
    u = 1.0 + y
    log1p = jnp.where(u == 1.0, y, jnp.log(u) * y / jnp.where(u == 1.0, 1.0, u - 1.0))
    return jnp.maximum(-lam, 0.0) + log1p


_NN = (((1,), (0,)), ((), ()))
_NT = (((1,), (1,)), ((), ()))
_TN = (((0,), (0,)), ((), ()))


def _dot(a, b, dims=_NN):
    return lax.dot_general(a, b, dims, preferred_element_type=F32)


def _group_mean(v, bd):
    hi = v.astype(BF16)
    lo = (v - hi.astype(F32)).astype(BF16)
    w = bd.shape[0]
    return jnp.concatenate([_dot(hi[:, c:c + w], bd) + _dot(lo[:, c:c + w], bd) for c in range(0, v.shape[1], w)], axis=1)


def _rope_tables(pos_ref, invf_ref):
    ang = pos_ref[...].astype(F32) * invf_ref[:, :2 * HEAD_DIM]
    reps = invf_ref.shape[1] // (2 * HEAD_DIM)
    return jnp.tile(jnp.cos(ang), (1, reps)), jnp.tile(jnp.sin(ang), (1, reps))


def _shift_down(x, halo, s):
    rolled = pltpu.roll(x, s, 0)
    hr = pltpu.roll(halo, s, 0)
    row = lax.broadcasted_iota(jnp.int32, hr.shape, 0)
    first = jnp.where(row < s, hr, rolled[:8])
    return jnp.concatenate([first, rolled[8:]], axis=0)


def _shift_up(x, halo, s):
    n = x.shape[0]
    rolled = pltpu.roll(x, n - s, 0)
    hr = pltpu.roll(halo, 8 - s, 0)
    row = lax.broadcasted_iota(jnp.int32, hr.shape, 0)
    last = jnp.where(row >= 8 - s, hr, rolled[n - 8:])
    return jnp.concatenate([rolled[:n - 8], last], axis=0)


def _scan_fwd(a, u):
    n, w = a.shape
    a3, u3 = a.reshape(n // 8, 8, w), u.reshape(n // 8, 8, w)
    row = lax.broadcasted_iota(jnp.int32, a3.shape, 1)
    for s in (1, 2, 4):
        a_s = jnp.where(row < s, 1.0, pltpu.roll(a3, s, 1))
        u_s = jnp.where(row < s, 0.0, pltpu.roll(u3, s, 1))
        u3 = u3 + a3 * u_s
        a3 = a3 * a_s
    ps, hs = [a3[0]], [u3[0]]
    for k in range(1, n // 8):
        ps.append(a3[k] * ps[-1][7:8, :])
        hs.append(u3[k] + a3[k] * hs[-1][7:8, :])
    return jnp.concatenate(ps, axis=0), jnp.concatenate(hs, axis=0)


def _scan_bwd(b, v):
    n, w = b.shape
    b3, v3 = b.reshape(n // 8, 8, w), v.reshape(n // 8, 8, w)
    row = lax.broadcasted_iota(jnp.int32, b3.shape, 1)
    for s in (1, 2, 4):
        b_s = jnp.where(row >= 8 - s, 1.0, pltpu.roll(b3, 8 - s, 1))
        v_s = jnp.where(row >= 8 - s, 0.0, pltpu.roll(v3, 8 - s, 1))
        v3 = v3 + b3 * v_s
        b3 = b3 * b_s
    last = n // 8 - 1
    ps, gs = [b3[last]], [v3[last]]
    for k in range(last - 1, -1, -1):
        ps.append(b3[k] * ps[-1][0:1, :])
        gs.append(v3[k] + b3[k] * gs[-1][0:1, :])
    return jnp.concatenate(ps[::-1], axis=0), jnp.concatenate(gs[::-1], axis=0)


def _rot_half(y):
    n = y.shape[1]
    lane = lax.broadcasted_iota(jnp.int32, y.shape, 1) & (HEAD_DIM - 1)
    return jnp.where(lane < HEAD_DIM // 2, -pltpu.roll(y, n - HEAD_DIM // 2, 1), pltpu.roll(y, HEAD_DIM // 2, 1))


def _row_tile(r, cap=256):
    return max(t for t in range(16, cap + 1, 16) if r % t == 0)


def _all_gather(shards, name):
    na = len(shards)
    ms = [s.shape[0] for s in shards]

    def body(*refs):
        x_refs, out_refs = refs[:na], refs[na:2 * na]
        send_sems, recv_sems, local_sems = refs[2 * na:]
        x, y, c = lax.axis_index("x"), lax.axis_index("y"), lax.axis_index("c")
        me, sibling = (x, y, c), (x, y, 1 - c)
        chips = [(1 - x, y), (x, 1 - y), (1 - x, 1 - y)]

        def rows(a, px, py, pc):
            return out_refs[a].at[pl.ds(pl.multiple_of((4 * px + 2 * py + pc) * ms[a], 8), ms[a]), :]

        def copy(a, k, block, to, src=None):
            return pltpu.make_async_remote_copy(
                src_ref=rows(a, *block) if src is None else src, dst_ref=rows(a, *block),
                send_sem=send_sems.at[7 * a + k], recv_sem=recv_sems.at[7 * a + k], device_id=to, device_id_type=MESH)

        mine = [pltpu.make_async_copy(x_refs[a], rows(a, *me), local_sems.at[a]) for a in range(na)]
        first = []
        for a in range(na):
            mine[a].start()
            first.append(copy(a, 0, me, sibling, src=x_refs[a]))
            first += [copy(a, 1 + j, me, (*chip, c), src=x_refs[a]) for j, chip in enumerate(chips)]
        for cp in first:
            cp.start()
        passed = []
        for a in range(na):
            for j, chip in enumerate(chips):
                copy(a, 1 + j, (*chip, c), me).wait_recv()
                fw = copy(a, 4 + j, (*chip, c), sibling)
                fw.start()
                passed.append(fw)
        for a in range(na):
            copy(a, 0, sibling, me).wait_recv()
            for j, chip in enumerate(chips):
                copy(a, 4 + j, (*chip, 1 - c), me).wait_recv()
        for cp in first + passed:
            cp.wait_send()
        for cp in mine:
            cp.wait()

    return _call(
        body, name=name, out_shape=[jax.ShapeDtypeStruct((N_DEV * s.shape[0], s.shape[1]), s.dtype) for s in shards],
        in_specs=[ANY] * na, out_specs=[ANY] * na,
        scratch_shapes=[pltpu.SemaphoreType.DMA((7 * na,)), pltpu.SemaphoreType.DMA((7 * na,)),
                        pltpu.SemaphoreType.DMA((na,))],
    )(*shards)


HBM = pl.BlockSpec(memory_space=pltpu.HBM)
SEM = pl.BlockSpec(memory_space=pltpu.SEMAPHORE)
EFFECT = pltpu.SideEffectType.DATAFLOW_SIDE_EFFECTING
N_PEERS = N_DEV - 1


def _peer(k):
    x, y, c = lax.axis_index("x"), lax.axis_index("y"), lax.axis_index("c")
    b = k + 1
    flip = lambda v, bit: 1 - v if bit else v
    return flip(x, b & 4), flip(y, b & 2), flip(c, b & 1)


def _in_hbm(a):
    return pltpu.with_memory_space_constraint(a, pltpu.HBM)


def _split_copy_descr(na, kind, src_refs, land_refs, send_sems, recv_sems):
    x, y, c = lax.axis_index("x"), lax.axis_index("y"), lax.axis_index("c")
    me = 4 * x + 2 * y + c
    copies = []
    for a in range(na):
        for k in range(N_PEERS):
            px, py, pc = _peer(k)
            if kind == "gather":
                m = src_refs[a].shape[0]
                src, dst = src_refs[a], land_refs[a].at[pl.ds(pl.multiple_of(me * m, 8), m), :]
            else:
                src, dst = src_refs[a].at[4 * px + 2 * py + pc], land_refs[a].at[k]
            copies.append(pltpu.make_async_remote_copy(
                src_ref=src, dst_ref=dst, send_sem=send_sems.at[N_PEERS * a + k], recv_sem=recv_sems.at[N_PEERS * a + k],
                device_id=(px, py, pc), device_id_type=MESH))
    return copies


def _landing(shape, dtype, own=None, at=None):
    buf = lax.empty(shape, dtype)
    return buf if own is None else lax.dynamic_update_slice(buf, own, (at, 0))


def _exchange_start(srcs, lands, kind, after, name):
    na = len(srcs)
    land_shapes = [l.shape for l in lands]

    def body(*refs):
        src_refs, land_refs = refs[:na], refs[na:2 * na]
        send_sems, recv_sems = refs[2 * na + 1], refs[2 * na + 2]
        token = refs[-1]
        for cp in _split_copy_descr(na, kind, src_refs, land_refs, send_sems, recv_sems):
            cp.start()
        token[...] = jnp.zeros_like(token)

    lands = [_in_hbm(l) for l in lands]
    sem = pltpu.SemaphoreType.DMA((N_PEERS * na,))
    outs = _call(
        body, name=name,
        out_shape=[sem, sem] + [pltpu.HBM(s.shape, s.dtype) for s in srcs] + [pltpu.HBM(s, srcs[0].dtype) for s in land_shapes]
        + [jax.ShapeDtypeStruct((8, 128), F32)],
        in_specs=[HBM] * (2 * na) + [ANY], out_specs=[SEM, SEM] + [HBM] * (2 * na) + [pl.BlockSpec(memory_space=pltpu.VMEM)],
        input_output_aliases={i: 2 + i for i in range(2 * na)},
        compiler_params=pltpu.CompilerParams(has_side_effects=EFFECT),
    )(*[_in_hbm(s) for s in srcs], *lands, after)
    return outs[0], outs[1], outs[2:2 + na], outs[2 + na:2 + 2 * na], outs[-1]


def _exchange_wait(send_sems, recv_sems, srcs, lands, kind, after, name):
    na = len(srcs)

    def body(*refs):
        src_refs, land_refs = refs[:na], refs[na:2 * na]
        s_sems, r_sems = refs[2 * na], refs[2 * na + 1]
        for cp in _split_copy_descr(na, kind, src_refs, land_refs, s_sems, r_sems):
            cp.wait_send()
            cp.wait_recv()

    outs = _call(
        body, name=name, out_shape=[pltpu.HBM(s.shape, s.dtype) for s in srcs] + [pltpu.HBM(l.shape, l.dtype) for l in lands],
        in_specs=[HBM] * (2 * na) + [SEM, SEM, ANY], out_specs=[HBM] * (2 * na),
        input_output_aliases={i: i for i in range(2 * na)},
        compiler_params=pltpu.CompilerParams(has_side_effects=EFFECT),
    )(*srcs, *lands, send_sems, recv_sems, after)
    return outs[:na], outs[na:]


def _mm(a, b, mode, out_dtype, name, add=None, tm=1024, tn=1024, tk=1024, b_noff=0, b_koff=0,
        n=None, k=None, into=None, o_rows=None, o_moff=0, loss_target=None):
    if mode == "tn":
        K, M = a.shape
    else:
        M, K = a.shape
    N = n if n is not None else (b.shape[0] if mode == "nt" else b.shape[1])
    if k is not None:
        assert k == K
    tm, tn, tk = min(tm, M), min(tn, N), min(tk, K)
    assert M % tm == 0 and N % tn == 0 and K % tk == 0, (name, M, N, K)
    nk = K // tk
    if mode == "nn":
        a_spec = pl.BlockSpec((tm, tk), lambda i, j, kk: (i, kk))
        b_spec, dims = pl.BlockSpec((tk, tn), lambda i, j, kk: (kk + b_koff, j + b_noff)), _NN
    elif mode == "nt":
        a_spec = pl.BlockSpec((tm, tk), lambda i, j, kk: (i, kk))
        b_spec, dims = pl.BlockSpec((tn, tk), lambda i, j, kk: (j + b_noff, kk + b_koff)), _NT
    else:
        a_spec = pl.BlockSpec((tk, tm), lambda i, j, kk: (kk, i))
        b_spec, dims = pl.BlockSpec((tk, tn), lambda i, j, kk: (kk + b_koff, j + b_noff)), _TN
    o_spec = pl.BlockSpec((tm, tn), lambda i, j, kk: (i + o_moff, j))
    has_add, has_into, has_loss = add is not None, into is not None, loss_target is not None
    assert not has_loss or (has_add and tn == N and not has_into)
    n_in = 2 + has_add + has_loss + has_into

    def body(*refs):
        a_ref, b_ref = refs[0], refs[1]
        add_ref = refs[2] if has_add else None
        outs = refs[n_in:]

        def finish(r):
            if has_add:
                r = r + add_ref[...]
            if has_loss:
                e = r - refs[3][...]
                dy = e * (1.0 / N)
                outs[0][...] = dy
                outs[1][...] = dy.astype(BF16)
                outs[2][...] = jnp.sum(e * e, axis=0, keepdims=True)[None]
            else:
                outs[0][...] = r.astype(out_dtype)

        if nk == 1:
            finish(_dot(a_ref[...], b_ref[...], dims))
        else:
            acc = refs[-1]
            kk = pl.program_id(2)

            @pl.when(kk == 0)
            def _():
                acc[...] = _dot(a_ref[...], b_ref[...], dims)

            @pl.when((kk > 0) & (kk < nk - 1))
            def _():
                acc[...] += _dot(a_ref[...], b_ref[...], dims)

            @pl.when(kk == nk - 1)
            def _():
                finish(acc[...] + _dot(a_ref[...], b_ref[...], dims))

    tile = pl.BlockSpec((tm, tn), lambda i, j, kk: (i, j))
    ins = [a, b] + ([add] if has_add else []) + ([loss_target] if has_loss else []) + ([into] if has_into else [])
    specs = [a_spec, b_spec] + [tile] * (has_add + has_loss) + ([ANY] if has_into else [])
    rows = into.shape[0] if has_into else (o_rows if o_rows is not None else M)
    if has_loss:
        out_specs = [tile, tile, pl.BlockSpec((1, 1, N), lambda i, j, kk: (i, 0, 0))]
        out_shape = [jax.ShapeDtypeStruct((M, N), F32), jax.ShapeDtypeStruct((M, N), BF16), jax.ShapeDtypeStruct((M // tm, 1, N), F32)]
    else:
        out_specs, out_shape = o_spec, jax.ShapeDtypeStruct((rows, N), out_dtype)
    return _call(
        body, name=name, grid=(M // tm, N // tn, nk), in_specs=specs, out_specs=out_specs, out_shape=out_shape,
        scratch_shapes=[pltpu.VMEM((tm, tn), F32)] if nk > 1 else [],
        input_output_aliases={len(ins) - 1: 0} if has_into else {},
        compiler_params=_params("parallel", "parallel", "arbitrary"),
    )(*ins)


def _mm_norm_bwd(parts, b, x, resid, g, name, tm=512, tk=512):
    T, N = x.shape
    counts = [p.shape[1] // tk for p in parts]
    starts = [sum(counts[:i]) for i in range(len(parts))]
    nsteps = sum(counts)
    assert all(p.shape[1] % tk == 0 for p in parts) and b.shape == (nsteps * tk, N)
    npart = len(parts)

    def body(*refs):
        a_refs, b_ref, x_ref, res_ref, g_ref = refs[:npart], refs[npart], refs[npart + 1], refs[npart + 2], refs[npart + 3]
        dx_ref, dxb_ref, dg_ref, acc = refs[npart + 4:]
        i, s = pl.program_id(0), pl.program_id(1)

        @pl.when((i == 0) & (s == 0))
        def _():
            dg_ref[...] = jnp.zeros_like(dg_ref)

        for p in range(npart):
            @pl.when((s >= starts[p]) & (s < starts[p] + counts[p]))
            def _(p=p):
                d = _dot(a_refs[p][...], b_ref[...])

                @pl.when(s == 0)
                def _():
                    acc[...] = d

                @pl.when(s > 0)
                def _():
                    acc[...] += d

        @pl.when(s == nsteps - 1)
        def _():
            xv, dhv = x_ref[...], acc[...]
            r = lax.rsqrt(jnp.mean(xv * xv, axis=-1, keepdims=True) + EPS)
            gd = dhv * g_ref[...]
            m = jnp.mean(gd * xv, axis=-1, keepdims=True)
            dx = res_ref[...] + r * gd - xv * (r * r * r) * m
            dx_ref[...] = dx
            dxb_ref[...] = dx.astype(BF16)
            dg_ref[...] += jnp.sum(dhv * xv * r, axis=0, keepdims=True)

    a_specs = [pl.BlockSpec((tm, tk), lambda i, s, st=st, c=c: (i, jnp.clip(s - st, 0, c - 1))) for st, c in zip(starts, counts)]
    row = pl.BlockSpec((tm, N), lambda i, s: (i, 0))
    vec = pl.BlockSpec((1, N), lambda i, s: (0, 0))
    return _call(
        body, name=name, grid=(T // tm, nsteps),
        in_specs=a_specs + [pl.BlockSpec((tk, N), lambda i, s: (s, 0)), row, row, vec], out_specs=[row, row, vec],
        out_shape=[jax.ShapeDtypeStruct((T, N), F32), jax.ShapeDtypeStruct((T, N), BF16), jax.ShapeDtypeStruct((1, N), F32)],
        scratch_shapes=[pltpu.VMEM((tm, N), F32)], compiler_params=_params("arbitrary", "arbitrary"),
    )(*parts, b, x, resid, g)


def _norm_proj(x, g, wT, name, tm=1024, tn=1280):
    T, K = x.shape
    N = wT.shape[0]

    def body(x_ref, g_ref, w_ref, o_ref, h_ref):
        xv = x_ref[...]
        r = lax.rsqrt(jnp.mean(xv * xv, axis=-1, keepdims=True) + EPS)
        hv = (xv * r * g_ref[...]).astype(BF16)

        @pl.when(pl.program_id(1) == 0)
        def _():
            h_ref[...] = hv

        o_ref[...] = _dot(hv, w_ref[...], _NT)

    return _call(
        body, name=name, grid=(T // tm, N // tn),
        in_specs=[pl.BlockSpec((tm, K), lambda i, j: (i, 0)), pl.BlockSpec((1, K), lambda i, j: (0, 0)),
                  pl.BlockSpec((tn, K), lambda i, j: (j, 0))],
        out_specs=[pl.BlockSpec((tm, tn), lambda i, j: (i, j)), pl.BlockSpec((tm, K), lambda i, j: (i, 0))],
        out_shape=[jax.ShapeDtypeStruct((T, N), F32), jax.ShapeDtypeStruct((T, K), BF16)],
        compiler_params=_params("parallel", "arbitrary"),
    )(x, g, wT)


def _qk_prep(proj, pos, invf, qg, kg, bd, name, tm=512):
    T = proj.shape[0]

    def body(q_ref, k_ref, pos_ref, invf_ref, qg_ref, kg_ref, bd_ref, qo_ref, ko_ref):
        cos, sin = _rope_tables(pos_ref, invf_ref)

        def prep(xv, gv, scale):
            r = lax.rsqrt(_group_mean(xv * xv, bd_ref[...]) + EPS)
            yv = xv * r * gv
            return ((yv * cos + _rot_half(yv) * sin) * scale).astype(BF16).astype(F32)

        qo_ref[...] = prep(q_ref[...], qg_ref[...], HEAD_DIM ** -0.5)
        ko_ref[...] = prep(k_ref[...], kg_ref[...], 1.0)

    col = lambda j: pl.BlockSpec((tm, ATTN_W), lambda i, j=j: (i, j))
    vec = pl.BlockSpec((1, ATTN_W), lambda i: (0, 0))
    out = pl.BlockSpec((tm, ATTN_W), lambda i: (i, 0))
    return _call(
        body, name=name, grid=(T // tm,),
        in_specs=[col(0), col(1), pl.BlockSpec((tm, 1), lambda i: (i, 0)), vec, vec, vec,
                  pl.BlockSpec((2 * HEAD_DIM, 2 * HEAD_DIM), lambda i: (0, 0))],
        out_specs=[out, out], out_shape=[jax.ShapeDtypeStruct((T, ATTN_W), F32)] * 2,
        compiler_params=_params("parallel"),
    )(proj, proj, pos, invf, qg, kg, bd)


def _qk_prep_bwd(proj, dqh, dkh, dv, pos, invf, qg, kg, bd, name, tm=512):
    T = proj.shape[0]

    def body(q_ref, k_ref, dq_ref, dk_ref, dv_ref, pos_ref, invf_ref, qg_ref, kg_ref, bd_ref, o_ref, gq_ref, gk_ref):
        @pl.when(pl.program_id(0) == 0)
        def _():
            gq_ref[...] = jnp.zeros_like(gq_ref)
            gk_ref[...] = jnp.zeros_like(gk_ref)

        cos, sin = _rope_tables(pos_ref, invf_ref)

        def back(xv, gv, dz, scale):
            dz = dz * scale
            dy = dz * cos - _rot_half(dz * sin)
            r = lax.rsqrt(_group_mean(xv * xv, bd_ref[...]) + EPS)
            gd = dy * gv
            m = _group_mean(gd * xv, bd_ref[...])
            dx = r * gd - xv * (r * r * r) * m
            return dx, jnp.sum(dy * xv * r, axis=0, keepdims=True)

        dxq, gs = back(q_ref[...], qg_ref[...], dq_ref[...], HEAD_DIM ** -0.5)
        gq_ref[...] += gs
        dxk, gs = back(k_ref[...], kg_ref[...], dk_ref[...], 1.0)
        gk_ref[...] += gs
        o_ref[...] = jnp.concatenate([dxq.astype(BF16), dxk.astype(BF16), dv_ref[...].astype(BF16)], axis=1)

    col = lambda j: pl.BlockSpec((tm, ATTN_W), lambda i, j=j: (i, j))
    row = pl.BlockSpec((tm, ATTN_W), lambda i: (i, 0))
    vec = pl.BlockSpec((1, ATTN_W), lambda i: (0, 0))
    return _call(
        body, name=name, grid=(T // tm,),
        in_specs=[col(0), col(1), row, row, row, pl.BlockSpec((tm, 1), lambda i: (i, 0)), vec, vec, vec,
                  pl.BlockSpec((2 * HEAD_DIM, 2 * HEAD_DIM), lambda i: (0, 0))],
        out_specs=[pl.BlockSpec((tm, 3 * ATTN_W), lambda i: (i, 0)), vec, vec],
        out_shape=[jax.ShapeDtypeStruct((T, 3 * ATTN_W), BF16)] + [jax.ShapeDtypeStruct((1, ATTN_W), F32)] * 2,
        compiler_params=_params("arbitrary"),
    )(proj, proj, dqh, dkh, dv, pos, invf, qg, kg, bd)


def _ld(ref, start, size, dil):
    return ref[pl.ds(start, size), :] if dil == 1 else ref[pl.ds(start, size, stride=dil), :]


def _st(ref, start, size, dil, val):
    if dil == 1:
        ref[pl.ds(start, size), :] = val
    else:
        ref[pl.ds(start, size, stride=dil), :] = val


def _attn_geometry(T, dil):
    nb = T // dil // QBLK
    if nb == 2:
        return 1, 2 * QBLK, 2 * QBLK
    return nb, QBLK, (2 * QBLK if nb >= 2 else QBLK)


def _attn_unroll(qb):
    return 4


def _attn_unit(j, u, dil, nit, unroll):
    return unroll * j + u if dil >= unroll else j + u * (nit // unroll)


def _attn_block(it, dil, qb, kw):
    c, n = it & (dil - 1), lax.shift_right_logical(it, dil.bit_length() - 1)
    sq = n * (qb * dil) + c
    sk = jnp.maximum(n - (kw // qb - 1), 0) * (qb * dil) + c
    qi = lax.broadcasted_iota(jnp.int32, (2 * qb, kw), 0) & (qb - 1)
    kj = lax.broadcasted_iota(jnp.int32, (2 * qb, kw), 1)
    rel = jnp.where(n > 0, kw - qb, 0) + qi - kj
    return sq, sk, (rel >= 0) & (rel <= QBLK)


def _stack_heads(xv, head0):
    z = jnp.zeros_like(xv)
    return jnp.concatenate([jnp.where(head0, xv, z), jnp.where(head0, z, xv)], axis=0)


def _unstack_heads(x2, head0):
    qb = x2.shape[0] // 2
    return jnp.where(head0, x2[:qb], x2[qb:])


def _attn_fwd(qf, kf, proj, name):
    T = qf.shape[0]

    def body(q_ref, k_ref, v_ref, o_ref, lse_ref):
        for bi, dil in enumerate(DILATIONS):
            nb, qb, kw = _attn_geometry(T, dil)
            nit = nb * dil
            head0 = lax.broadcasted_iota(jnp.int32, (qb, 2 * HEAD_DIM), 1) < HEAD_DIM

            def step(j, carry, bi=bi, dil=dil, qb=qb, kw=kw, nit=nit, head0=head0):
                units = []
                for u in range(_attn_unroll(qb)):
                    sq, sk, ok = _attn_block(_attn_unit(j, u, dil, nit, _attn_unroll(qb)), dil, qb, kw)
                    old = (_ld(o_ref, sq, qb, dil), _ld(lse_ref, sq, qb, dil)) if bi > 0 else None
                    units.append((sq, ok, _ld(q_ref, sq, qb, dil).astype(BF16), _ld(k_ref, sk, kw, dil).astype(BF16),
                                  _ld(v_ref, sk, kw, dil).astype(BF16), old))
                results = []
                for sq, ok, qv, kv, vv, old in units:
                    s = jnp.where(ok, _dot(_stack_heads(qv, head0), kv, _NT), NEG_INF)
                    m = jnp.max(s, axis=-1, keepdims=True)
                    p = jnp.exp(s - m).astype(BF16)
                    acc = _dot(p, jnp.concatenate([vv, jnp.ones_like(vv)], axis=1))
                    l = acc[:, 2 * HEAD_DIM:]
                    o_new = _unstack_heads(acc[:, :2 * HEAD_DIM] / l, head0)
                    l_new = _unstack_heads(m + jnp.log(l), head0)
                    if bi > 0:
                        o_old, l_old = old
                        mx = jnp.maximum(l_old, l_new)
                        e0, e1 = jnp.exp(l_old - mx), jnp.exp(l_new - mx)
                        z = e0 + e1
                        o_new = (e0 * o_old + e1 * o_new) / z
                        l_new = mx + jnp.log(z)
                    results.append((sq, o_new, l_new))
                for sq, o_new, l_new in results:
                    _st(o_ref, sq, qb, dil, o_new)
                    _st(lse_ref, sq, qb, dil, l_new)
                return carry

            lax.fori_loop(0, nit // _attn_unroll(qb), step, 0)

    blk = lambda off: pl.BlockSpec((T, 2 * HEAD_DIM), lambda hp, off=off: (0, off + hp))
    return _call(
        body, name=name, grid=(4,), in_specs=[blk(0), blk(0), blk(8)], out_specs=[blk(0), blk(0)],
        out_shape=[jax.ShapeDtypeStruct((T, ATTN_W), F32)] * 2, compiler_params=_params("parallel"),
    )(qf, kf, proj)


def _attn_bwd(qf, kf, proj, do, lse, delta, name):
    T = qf.shape[0]

    def body(q_ref, k_ref, v_ref, do_ref, lse_ref, dl_ref, dq_ref, dk_ref, dv_ref):
        for ref in (dq_ref, dk_ref, dv_ref):
            ref[...] = jnp.zeros_like(ref)
        for dil in DILATIONS:
            nb, qb, kw = _attn_geometry(T, dil)
            nit = nb * dil
            head0 = lax.broadcasted_iota(jnp.int32, (qb, 2 * HEAD_DIM), 1) < HEAD_DIM

            def step(j, carry, dil=dil, qb=qb, kw=kw, nit=nit, head0=head0):
                units = []
                for u in range(_attn_unroll(qb)):
                    sq, sk, ok = _attn_block(_attn_unit(j, u, dil, nit, _attn_unroll(qb)), dil, qb, kw)
                    lsev, dlv = _ld(lse_ref, sq, qb, dil), _ld(dl_ref, sq, qb, dil)
                    units.append((sq, sk, ok, _ld(q_ref, sq, qb, dil).astype(BF16), _ld(do_ref, sq, qb, dil).astype(BF16),
                                  jnp.concatenate([lsev[:, 0:1], lsev[:, HEAD_DIM:HEAD_DIM + 1]], axis=0),
                                  jnp.concatenate([dlv[:, 0:1], dlv[:, HEAD_DIM:HEAD_DIM + 1]], axis=0),
                                  _ld(k_ref, sk, kw, dil).astype(BF16), _ld(v_ref, sk, kw, dil).astype(BF16),
                                  _ld(dq_ref, sq, qb, dil), _ld(dk_ref, sk, kw, dil), _ld(dv_ref, sk, kw, dil)))
                results = []
                for sq, sk, ok, qv, dov, lse2, dl2, kv, vv, dq0, dk0, dv0 in units:
                    q2, do2 = _stack_heads(qv, head0), _stack_heads(dov, head0)
                    p = jnp.where(ok, jnp.exp(_dot(q2, kv, _NT) - lse2), 0.0)
                    ds = (p * (_dot(do2, vv, _NT) - dl2)).astype(BF16)
                    results.append((sq, sk, dq0 + _unstack_heads(_dot(ds, kv), head0),
                                    dk0 + _dot(ds, q2, _TN), dv0 + _dot(p.astype(BF16), do2, _TN)))
                for sq, sk, dq, dk, dv in results:
                    _st(dq_ref, sq, qb, dil, dq)
                    _st(dk_ref, sk, kw, dil, dk)
                    _st(dv_ref, sk, kw, dil, dv)
                return carry

            lax.fori_loop(0, nit // _attn_unroll(qb), step, 0)

    blk = lambda off: pl.BlockSpec((T, 2 * HEAD_DIM), lambda hp, off=off: (0, off + hp))
    return _call(
        body, name=name, grid=(4,), in_specs=[blk(0), blk(0), blk(8), blk(0), blk(0), blk(0)], out_specs=[blk(0)] * 3,
        out_shape=[jax.ShapeDtypeStruct((T, ATTN_W), F32)] * 3, compiler_params=_params("parallel"),
    )(qf, kf, proj, do, lse, delta)


def _attn_norm(attn, g, name, tm=512):
    T = attn.shape[0]

    def body(a_ref, g_ref, o_ref):
        av = a_ref[...]
        r = lax.rsqrt(jnp.mean(av * av, axis=-1, keepdims=True) + EPS)
        o_ref[...] = (av * r * g_ref[...]).astype(BF16)

    row = pl.BlockSpec((tm, ATTN_W), lambda i: (i, 0))
    return _call(
        body, name=name, grid=(T // tm,), in_specs=[row, pl.BlockSpec((1, ATTN_W), lambda i: (0, 0))], out_specs=row,
        out_shape=jax.ShapeDtypeStruct((T, 2 * ATTN_W), BF16), compiler_params=_params("parallel"),
    )(attn, g)


def _attn_norm_bwd(dmix, attn, g, bd, name, tm=512):
    T = attn.shape[0]

    def body(d_ref, a_ref, g_ref, bd_ref, do_ref, dl_ref, dg_ref):
        @pl.when(pl.program_id(0) == 0)
        def _():
            dg_ref[...] = jnp.zeros_like(dg_ref)

        dy, av = d_ref[...], a_ref[...]
        r = lax.rsqrt(jnp.mean(av * av, axis=-1, keepdims=True) + EPS)
        gd = dy * g_ref[...]
        m = jnp.mean(gd * av, axis=-1, keepdims=True)
        da = r * gd - av * (r * r * r) * m
        do_ref[...] = da
        dl_ref[...] = _group_mean(da * av, bd_ref[...]) * float(HEAD_DIM)
        dg_ref[...] += jnp.sum(dy * av * r, axis=0, keepdims=True)

    row = pl.BlockSpec((tm, ATTN_W), lambda i: (i, 0))
    vec = pl.BlockSpec((1, ATTN_W), lambda i: (0, 0))
    return _call(
        body, name=name, grid=(T // tm,),
        in_specs=[row, row, vec, pl.BlockSpec((2 * HEAD_DIM, 2 * HEAD_DIM), lambda i: (0, 0))], out_specs=[row, row, vec],
        out_shape=[jax.ShapeDtypeStruct((T, ATTN_W), F32)] * 2 + [jax.ShapeDtypeStruct((1, ATTN_W), F32)],
        compiler_params=_params("arbitrary"),
    )(dmix, attn, g, bd)


def _rec_gates(xc, wrg_ref, wig_ref, brg_ref, big_ref, lam_ref):
    xb = xc.astype(BF16)
    r = _sigmoid(_dot(xb, wrg_ref[...]) + brg_ref[...])
    ig = _sigmoid(_dot(xb, wig_ref[...]) + big_ref[...])
    sp = _softplus_neg(lam_ref[...])
    log_a = -LRU_C * r * sp
    a = jnp.exp(log_a)
    th = jnp.tanh(log_a)
    mult = jnp.sqrt(-2.0 * th / (1.0 - th))
    return xb, r, ig, sp, a, mult


def _rec_fwd(proj, mix, cw, cb, wrg, wig, brg, big, lam, g, name, tm=256):
    T = proj.shape[0]
    hb = tm // 8

    def body(xr_ref, halo_ref, gr_ref, cw_ref, cb_ref, wrg_ref, wig_ref, brg_ref, big_ref, lam_ref, g_ref, mix_ref,
             xc_ref, h_ref, out_ref, carry):
        i = pl.program_id(0)

        @pl.when(i == 0)
        def _():
            carry[...] = jnp.zeros_like(carry)

        xr = xr_ref[...]
        halo = jnp.where(i > 0, halo_ref[...], 0.0)
        xc = cb_ref[...] + cw_ref[3:4, :] * xr
        for s in range(1, REC_CONV):
            xc = xc + cw_ref[3 - s:4 - s, :] * _shift_down(xr, halo, s)
        xc_ref[...] = xc
        _, _, ig, _, a, mult = _rec_gates(xc, wrg_ref, wig_ref, brg_ref, big_ref, lam_ref)
        pa, hl = _scan_fwd(a, mult * (ig * xc))
        h = hl + pa * carry[0:1, :]
        h_ref[...] = h
        carry[0:1, :] = h_ref[pl.ds(tm - 1, 1), :]
        hg = h * _gelu(gr_ref[...])
        r = lax.rsqrt(jnp.mean(hg * hg, axis=-1, keepdims=True) + EPS)
        out_ref[...] = (hg * r * g_ref[...]).astype(BF16)

    vec = pl.BlockSpec((1, REC_W), lambda i: (0, 0))
    row = pl.BlockSpec((tm, REC_W), lambda i: (i, 0))
    mat = pl.BlockSpec((REC_W, REC_W), lambda i: (0, 0))
    return _call(
        body, name=name, grid=(T // tm,),
        in_specs=[pl.BlockSpec((tm, REC_W), lambda i: (i, 3)),
                  pl.BlockSpec((8, REC_W), lambda i: (jnp.maximum(i * hb - 1, 0), 3)),
                  pl.BlockSpec((tm, REC_W), lambda i: (i, 4)),
                  pl.BlockSpec((8, REC_W), lambda i: (0, 0)), vec, mat, mat, vec, vec, vec, vec, ANY],
        out_specs=[row, row, pl.BlockSpec((tm, REC_W), lambda i: (i, 1))],
        out_shape=[jax.ShapeDtypeStruct((T, REC_W), F32)] * 2 + [jax.ShapeDtypeStruct(mix.shape, BF16)],
        scratch_shapes=[pltpu.VMEM((8, REC_W), F32)], input_output_aliases={11: 2},
        compiler_params=_params("arbitrary"),
    )(proj, proj, proj, cw, cb, wrg, wig, brg, big, lam, g, mix)


def _rec_bwd(dmix, proj, xc, h, cw, cb, wrg, wig, brg, big, lam, g, name, tm=256):
    T = proj.shape[0]
    nt = T // tm
    hb = tm // 8

    def body(d_ref, xr_ref, xhalo_ref, gr_ref, xc_ref, h_ref, hhalo_ref, cw_ref, cb_ref, wrg_ref, wig_ref, brg_ref,
             big_ref, lam_ref, g_ref,
             drec_ref, gcw_ref, gcb_ref, gwrg_ref, gwig_ref, gbrg_ref, gbig_ref, glam_ref, gg_ref,
             g_carry, a_first, dxc_next, gsp):
        i = pl.program_id(0)
        first_tile = i == nt - 1

        @pl.when(i == 0)
        def _():
            for ref in (gcw_ref, gcb_ref, gwrg_ref, gwig_ref, gbrg_ref, gbig_ref, glam_ref, gg_ref,
                        g_carry, a_first, dxc_next, gsp):
                ref[...] = jnp.zeros_like(ref)

        xr, xc, hv = xr_ref[...], xc_ref[...], h_ref[...]
        xhalo = jnp.where(first_tile, 0.0, xhalo_ref[...])
        hhalo = jnp.where(first_tile, 0.0, hhalo_ref[...])
        xb, r, ig, sp, a, mult = _rec_gates(xc, wrg_ref, wig_ref, brg_ref, big_ref, lam_ref)
        h_prev = _shift_down(hv, hhalo, 1)
        ge, dge = _gelu_and_grad(gr_ref[...])
        hg = hv * ge
        rr = lax.rsqrt(jnp.mean(hg * hg, axis=-1, keepdims=True) + EPS)
        dy = d_ref[...]
        gd = dy * g_ref[...]
        dhg = rr * gd - hg * (rr * rr * rr) * jnp.mean(gd * hg, axis=-1, keepdims=True)
        gg_ref[...] += jnp.sum(dy * hg * rr, axis=0, keepdims=True)
        dgr = (dhg * hv * dge).astype(BF16)
        dh = dhg * ge
        b = _shift_up(a, jnp.broadcast_to(a_first[0:1, :], (8, REC_W)), 1)
        pb, gl = _scan_bwd(b, dh)
        gs = gl + pb * g_carry[0:1, :]
        g_carry[0:1, :] = gs[0:1, :]
        a_first[0:1, :] = a[0:1, :]
        da = gs * h_prev
        dmult = gs * (ig * xc)
        di = gs * (mult * xc)
        dxc = gs * (mult * ig)
        dlog_a = da * a - dmult * (a * a) / mult
        gsp[...] += jnp.sum(dlog_a * (-LRU_C * r), axis=0, keepdims=True)
        dzr = (dlog_a * (-LRU_C * sp)) * (r * (1.0 - r))
        dzi = di * (ig * (1.0 - ig))
        dzr_b, dzi_b = dzr.astype(BF16), dzi.astype(BF16)
        dxc = dxc + _dot(dzr_b, wrg_ref[...], _NT) + _dot(dzi_b, wig_ref[...], _NT)
        gwrg_ref[...] += _dot(xb, dzr_b, _TN)
        gwig_ref[...] += _dot(xb, dzi_b, _TN)
        gbrg_ref[...] += jnp.sum(dzr, axis=0, keepdims=True)
        gbig_ref[...] += jnp.sum(dzi, axis=0, keepdims=True)
        nxt = dxc_next[...]
        dxr = cw_ref[3:4, :] * dxc
        gcw_ref[3:4, :] += jnp.sum(dxc * xr, axis=0, keepdims=True)
        for s in range(1, REC_CONV):
            dxr = dxr + cw_ref[3 - s:4 - s, :] * _shift_up(dxc, nxt, s)
            gcw_ref[3 - s:4 - s, :] += jnp.sum(dxc * _shift_down(xr, xhalo, s), axis=0, keepdims=True)
        gcb_ref[...] += jnp.sum(dxc, axis=0, keepdims=True)
        dxc_next[...] = dxc[:8]
        drec_ref[...] = jnp.concatenate([dxr.astype(BF16), dgr], axis=1)

        @pl.when(first_tile)
        def _():
            glam_ref[...] = gsp[...] * (-_sigmoid(-lam_ref[...]))

    rev = lambda i: nt - 1 - i
    vec = pl.BlockSpec((1, REC_W), lambda i: (0, 0))
    row = pl.BlockSpec((tm, REC_W), lambda i: (rev(i), 0))
    mat = pl.BlockSpec((REC_W, REC_W), lambda i: (0, 0))
    cwb = pl.BlockSpec((8, REC_W), lambda i: (0, 0))
    halo = lambda c: pl.BlockSpec((8, REC_W), lambda i, c=c: (jnp.maximum(rev(i) * hb - 1, 0), c))
    return _call(
        body, name=name, grid=(nt,),
        in_specs=[pl.BlockSpec((tm, REC_W), lambda i: (rev(i), 1)),
                  pl.BlockSpec((tm, REC_W), lambda i: (rev(i), 3)), halo(3),
                  pl.BlockSpec((tm, REC_W), lambda i: (rev(i), 4)),
                  row, row, halo(0), cwb, vec, mat, mat, vec, vec, vec, vec],
        out_specs=[pl.BlockSpec((tm, 2 * REC_W), lambda i: (rev(i), 0)), cwb, vec, mat, mat, vec, vec, vec, vec],
        out_shape=[jax.ShapeDtypeStruct((T, 2 * REC_W), BF16)]
        + [jax.ShapeDtypeStruct((8, REC_W), F32), jax.ShapeDtypeStruct((1, REC_W), F32)]
        + [jax.ShapeDtypeStruct((REC_W, REC_W), F32)] * 2 + [jax.ShapeDtypeStruct((1, REC_W), F32)] * 4,
        scratch_shapes=[pltpu.VMEM((8, REC_W), F32)] * 3 + [pltpu.VMEM((1, REC_W), F32)],
        compiler_params=_params("arbitrary"),
    )(dmix, proj, proj, proj, xc, h, h, cw, cb, wrg, wig, brg, big, lam, g)


def _ffn_conv(x_ext, cw_ref, cb_ref):
    return (cb_ref[...] + cw_ref[2:3, :] * x_ext + cw_ref[1:2, :] * pltpu.roll(x_ext, 1, 0)
            + cw_ref[0:1, :] * pltpu.roll(x_ext, 2, 0))


def _up_proj_act(x2, g, w_upT, cw, cb, name, tm=1024, tc=768):
    T = x2.shape[0]
    nc = D_FF // tc

    def body(x_ref, g_ref, wg_ref, wu_ref, cwg_ref, cwu_ref, cbg_ref, cbu_ref, act_ref, da_ref, db_ref, pg_ref, pu_ref,
             h_ref, hist_g, hist_u, hs):
        i, j = pl.program_id(0), pl.program_id(1)

        @pl.when(j == 0)
        def _():
            xv = x_ref[...]
            r = lax.rsqrt(jnp.mean(xv * xv, axis=-1, keepdims=True) + EPS)
            hs[...] = (xv * r * g_ref[...]).astype(BF16)
            h_ref[...] = hs[...]

        hv = hs[...]
        pg, pu = _dot(hv, wg_ref[...], _NT), _dot(hv, wu_ref[...], _NT)
        ge = jnp.concatenate([jnp.where(i > 0, hist_g[j], 0.0), pg], axis=0)
        ue = jnp.concatenate([jnp.where(i > 0, hist_u[j], 0.0), pu], axis=0)
        gel, dgel = _gelu_and_grad(_ffn_conv(ge, cwg_ref, cbg_ref)[8:])
        uu = _ffn_conv(ue, cwu_ref, cbu_ref)[8:]
        act_ref[...] = (gel * uu).astype(BF16)
        da_ref[...] = (uu * dgel).astype(BF16)
        db_ref[...] = gel.astype(BF16)
        pg_ref[...] = pg.astype(BF16)
        pu_ref[...] = pu.astype(BF16)
        hist_g[j] = pg[tm - 8:]
        hist_u[j] = pu[tm - 8:]

    tile = pl.BlockSpec((tm, tc), lambda i, j: (i, j))
    wsp = lambda off: pl.BlockSpec((tc, D_MODEL), lambda i, j, off=off: (j + off, 0))
    cws = lambda off: pl.BlockSpec((8, tc), lambda i, j, off=off: (0, j + off))
    cbs = lambda off: pl.BlockSpec((1, tc), lambda i, j, off=off: (0, j + off))
    return _call(
        body, name=name, grid=(T // tm, nc),
        in_specs=[pl.BlockSpec((tm, D_MODEL), lambda i, j: (i, 0)), pl.BlockSpec((1, D_MODEL), lambda i, j: (0, 0)),
                  wsp(0), wsp(nc), cws(0), cws(nc), cbs(0), cbs(nc)],
        out_specs=[tile] * 5 + [pl.BlockSpec((tm, D_MODEL), lambda i, j: (i, 0))],
        out_shape=[jax.ShapeDtypeStruct((T, D_FF), BF16)] * 5 + [jax.ShapeDtypeStruct((T, D_MODEL), BF16)],
        scratch_shapes=[pltpu.VMEM((nc, 8, tc), F32)] * 2 + [pltpu.VMEM((tm, D_MODEL), BF16)],
        compiler_params=_params("arbitrary", "arbitrary"),
    )(x2, g, w_upT, w_upT, cw, cw, cb, cb)


def _ffn_bwd(dyb, w_down, da, db, pg, pu, cw, name, tm=1024, tc=768):
    T, F = pg.shape
    nt = T // tm
    hb16 = tm // 16
    nc = F // tc
    n = tm + 8

    def body(dy_ref, dyn_ref, wd_ref, a_ref, an_ref, b_ref, bn_ref, g_ref, u_ref, cwg_ref, cwu_ref,
             dg_ref, du_ref, gcwg_ref, gcwu_ref, gcbg_ref, gcbu_ref):
        i = pl.program_id(1)
        last = i == nt - 1

        @pl.when(i == 0)
        def _():
            for ref in (gcwg_ref, gcwu_ref, gcbg_ref, gcbu_ref):
                ref[...] = jnp.zeros_like(ref)

        wd = wd_ref[...]
        dact_next = jnp.where(last, 0.0, _dot(dyn_ref[...], wd, _NT)[:8])
        de = jnp.concatenate([_dot(dy_ref[...], wd, _NT), dact_next], axis=0)
        ext = lambda t, nx: jnp.concatenate([t[...].astype(F32), nx[...].astype(F32)[:8]], axis=0)
        for dcv, x_ref, cw_ref, dx_ref, gcw_ref, gcb_ref in ((de * ext(a_ref, an_ref), g_ref, cwg_ref, dg_ref, gcwg_ref, gcbg_ref),
                                                               (de * ext(b_ref, bn_ref), u_ref, cwu_ref, du_ref, gcwu_ref, gcbu_ref)):
            s1, s2 = pltpu.roll(dcv, n - 1, 0), pltpu.roll(dcv, n - 2, 0)
            dx_ref[...] = (cw_ref[2:3, :] * dcv + cw_ref[1:2, :] * s1 + cw_ref[0:1, :] * s2)[:tm].astype(BF16)
            xv = x_ref[...].astype(F32)
            gcw_ref[2:3, :] += jnp.sum(xv * dcv[:tm], axis=0, keepdims=True)
            gcw_ref[1:2, :] += jnp.sum(xv * s1[:tm], axis=0, keepdims=True)
            gcw_ref[0:1, :] += jnp.sum(xv * s2[:tm], axis=0, keepdims=True)
            gcb_ref[...] += jnp.sum(dcv[:tm], axis=0, keepdims=True)

    tile = pl.BlockSpec((tm, tc), lambda j, i: (i, j))
    nxt = pl.BlockSpec((16, tc), lambda j, i: (jnp.minimum((i + 1) * hb16, nt * hb16 - 1), j))
    cws = lambda off: pl.BlockSpec((8, tc), lambda j, i, off=off: (0, j + off))
    cbs = pl.BlockSpec((1, tc), lambda j, i: (0, j))
    return _call(
        body, name=name, grid=(nc, nt),
        in_specs=[pl.BlockSpec((tm, D_MODEL), lambda j, i: (i, 0)),
                  pl.BlockSpec((16, D_MODEL), lambda j, i: (jnp.minimum((i + 1) * hb16, nt * hb16 - 1), 0)),
                  pl.BlockSpec((tc, D_MODEL), lambda j, i: (j, 0)), tile, nxt, tile, nxt, tile, tile, cws(0), cws(nc)],
        out_specs=[tile, tile, cws(0), cws(0), cbs, cbs],
        out_shape=[jax.ShapeDtypeStruct((T, F), BF16)] * 2 + [jax.ShapeDtypeStruct((8, F), F32)] * 2
        + [jax.ShapeDtypeStruct((1, F), F32)] * 2,
        compiler_params=_params("parallel", "arbitrary"),
    )(dyb, dyb, w_down, da, da, db, db, pg, pu, cw, cw)


def _adam_update(w, g, m, v):
    m2 = ADAM_B1 * m + (1.0 - ADAM_B1) * g
    v2 = ADAM_B2 * v + (1.0 - ADAM_B2) * (g * g)
    m_hat = m2 / (1.0 - ADAM_B1 ** ADAM_STEP)
    v_hat = v2 / (1.0 - ADAM_B2 ** ADAM_STEP)
    delta = -ADAM_LR * (m_hat / (jnp.sqrt(v_hat) + ADAM_EPS) + ADAM_WD * w)
    return delta, m2, v2


def _adam_sharded(p, r2, idx, w, m, v, name, transposed=False):
    r, n = p.shape[1:]
    nrecv = r2.shape[0]
    tr = (256 if r % 256 == 0 else r) if transposed else _row_tile(r)

    def body(c_ref, p_ref, r_ref, w_ref, m_ref, v_ref, g_ref, d_ref, m2_ref, v2_ref):
        g = p_ref[...].astype(F32)
        for k in range(nrecv):
            g = g + r_ref[k].astype(F32)
        if transposed:
            g = g.T
        g_ref[...] = g
        d_ref[...], m2_ref[...], v2_ref[...] = _adam_update(w_ref[...], g, m_ref[...], v_ref[...])

    blk = pl.BlockSpec((n, tr), lambda i, c_ref: (0, i)) if transposed else pl.BlockSpec((tr, n), lambda i, c_ref: (i, 0))
    spec = pltpu.PrefetchScalarGridSpec(
        num_scalar_prefetch=1, grid=(r // tr,),
        in_specs=[pl.BlockSpec((None, tr, n), lambda i, c_ref: (c_ref[0], i, 0)),
                  pl.BlockSpec((nrecv, tr, n), lambda i, c_ref: (0, i, 0)), blk, blk, blk],
        out_specs=[blk] * 4)
    return _call(body, name=name, grid_spec=spec, out_shape=[jax.ShapeDtypeStruct(w.shape, F32)] * 4,
                 compiler_params=_params("parallel"))(idx, p, r2, w, m, v)


def _sum_slabs(p, r2, idx, name):
    _, r, n = p.shape

    def body(c_ref, p_ref, r_ref, o_ref):
        acc = p_ref[...]
        for k in range(N_PEERS):
            acc = acc + r_ref[k]
        o_ref[...] = acc

    spec = pltpu.PrefetchScalarGridSpec(
        num_scalar_prefetch=1, grid=(1,),
        in_specs=[pl.BlockSpec((None, r, n), lambda i, c_ref: (c_ref[0], 0, 0)),
                  pl.BlockSpec((N_PEERS, r, n), lambda i, c_ref: (0, 0, 0))],
        out_specs=pl.BlockSpec((r, n), lambda i, c_ref: (0, 0)))
    return _call(body, name=name, grid_spec=spec, out_shape=jax.ShapeDtypeStruct((r, n), F32))(idx, p, r2)


def _adam_small(ws, gs, ms, vs, name):
    n = len(ws)

    def body(*refs):
        for i in range(n):
            d, m2, v2 = _adam_update(refs[i][...], refs[n + i][...], refs[2 * n + i][...], refs[3 * n + i][...])
            refs[4 * n + i][...] = d
            refs[5 * n + i][...] = m2
            refs[6 * n + i][...] = v2

    outs = _call(body, name=name, out_shape=[jax.ShapeDtypeStruct(w.shape, F32) for w in ws] * 3)(*ws, *gs, *ms, *vs)
    return outs[:n], outs[n:2 * n], outs[2 * n:]


def _pack_small_grads(full, halves, rcw, fcwg, fcwu, wrg, wig, lparts, name):
    nf, nh = len(full), len(halves)

    def body(*refs):
        o = refs[-1]
        o[...] = jnp.zeros_like(o)
        row = 0
        for r in refs[:nf]:
            for j in range(r.shape[1] // 1024):
                o[row:row + 1, :] = r[:, 1024 * j:1024 * (j + 1)]
                row += 1
        for k in range(0, nh, 2):
            o[row:row + 1, 0:512] = refs[nf + k][...]
            o[row:row + 1, 512:1024] = refs[nf + k + 1][...]
            row += 1
        rcw_ref, fg_ref, fu_ref, wrg_ref, wig_ref, l_ref = refs[nf + nh:nf + nh + 6]
        for k in range(2):
            o[row:row + 1, 0:512] = rcw_ref[2 * k:2 * k + 1, :]
            o[row:row + 1, 512:1024] = rcw_ref[2 * k + 1:2 * k + 2, :]
            row += 1
        for f_ref in (fg_ref, fu_ref):
            for k in range(FFN_CONV):
                for j in range(D_FF // 1024):
                    o[row:row + 1, :] = f_ref[k:k + 1, 1024 * j:1024 * (j + 1)]
                    row += 1
        assert row == 32
        for n in range(8):
            o[32:96, 64 * n:64 * n + 64] = wrg_ref[64 * n:64 * n + 64, 64 * n:64 * n + 64]
            o[32:96, 512 + 64 * n:512 + 64 * n + 64] = wig_ref[64 * n:64 * n + 64, 64 * n:64 * n + 64]
        o[96:97, :] = jnp.sum(l_ref[...], axis=0, keepdims=True)

    return _call(body, name=name, out_shape=jax.ShapeDtypeStruct((SMALL_ROWS, 1024), F32))(
        *full, *halves, rcw, fcwg, fcwu, wrg, wig, lparts)


def _block_diag(w):
    eye = jnp.eye(8, dtype=w.dtype)
    return (w[:, :, None, :] * eye[:, None, :, None]).reshape(512, 512)


def kernel(x, positions, g_mix, w_in, q_norm_g, k_norm_g, rec_conv_w, rec_conv_b, w_rg, b_rg, w_ig, b_ig, lru_lambda, g_attn_out, g_rec_out, w_out, g_ffn, w_up, ffn_conv_w, ffn_conv_b, w_down, loss_target, m_g_mix, m_w_in, m_q_norm_g, m_k_norm_g, m_rec_conv_w, m_rec_conv_b, m_w_rg, m_b_rg, m_w_ig, m_b_ig, m_lru_lambda, m_g_attn_out, m_g_rec_out, m_w_out, m_g_ffn, m_w_up, m_ffn_conv_w, m_ffn_conv_b, m_w_down, v_g_mix, v_w_in, v_q_norm_g, v_k_norm_g, v_rec_conv_w, v_rec_conv_b, v_w_rg, v_b_rg, v_w_ig, v_b_ig, v_lru_lambda, v_g_attn_out, v_g_rec_out, v_w_out, v_g_ffn, v_w_up, v_ffn_conv_w, v_ffn_conv_b, v_w_down):
    T = x.shape[1]
    ix, iy, ic = lax.axis_index("x"), lax.axis_index("y"), lax.axis_index("c")
    dev = 4 * ix + 2 * iy + ic
    xs = x.reshape(T, D_MODEL)
    tgt = loss_target.reshape(T, D_MODEL)
    pos = positions.reshape(T, 1)

    shards = {"w_in": (w_in[0], m_w_in[0], v_w_in[0]), "w_out": (w_out[0], m_w_out[0], v_w_out[0]),
              "w_up": (w_up[0], m_w_up[0], v_w_up[0]), "w_down": (w_down[0], m_w_down[0], v_w_down[0])}
    taps = jnp.concatenate([rec_conv_w.reshape(-1), ffn_conv_w.reshape(-1), jnp.zeros((4096 - 2560,), F32)]).reshape(8, 512)
    W_inT, taps_all = _all_gather([w_in[0].T.astype(BF16), taps], "ag_w_in")
    gather_landing = lambda s: _landing((N_DEV * s.shape[0], 1024), BF16, s, dev * s.shape[0])
    late = [w_out[0].astype(BF16), w_up[0].T.astype(BF16)]
    ag_send, ag_recv, late_thru, land_thru, ag_token = _exchange_start(
        late, [gather_landing(s) for s in late], "gather", taps_all, "ag_late_start")
    w_down_b = w_down[0].astype(BF16)
    down_landing = gather_landing(w_down_b)
    taps_all = taps_all.reshape(N_DEV, 4096)
    rcw = taps_all[:, :256].reshape(8, 4, 64).transpose(1, 0, 2).reshape(4, REC_W)
    fcw = taps_all[:, 256:2560].reshape(8, 3, 768).transpose(1, 0, 2).reshape(3, 2 * D_FF)
    rcw8 = jnp.pad(rcw, ((0, 4), (0, 0)))
    fcw8 = jnp.pad(fcw, ((0, 5), (0, 0)))
    fcb = ffn_conv_b.reshape(1, 2 * D_FF)

    half = HEAD_DIM // 2
    inv_freq = ROPE_THETA ** (-jnp.arange(half, dtype=F32) / half)
    invf = jnp.tile(inv_freq, 2 * N_HEADS).reshape(1, ATTN_W)
    bd = jnp.asarray(np.kron(np.eye(2), np.full((HEAD_DIM, HEAD_DIM), 1.0 / HEAD_DIM)), BF16)
    qg = jnp.tile(q_norm_g.reshape(HEAD_DIM), N_HEADS).reshape(1, ATTN_W)
    kg = jnp.tile(k_norm_g.reshape(HEAD_DIM), N_HEADS).reshape(1, ATTN_W)
    wrg_bd = _block_diag(w_rg[0]).astype(BF16)
    wig_bd = _block_diag(w_ig[0]).astype(BF16)
    brg, big = b_rg.reshape(1, REC_W), b_ig.reshape(1, REC_W)

    proj, h1 = _norm_proj(xs, g_mix + ag_token[0, 0], W_inT, "in_proj", tn=IN_W)
    qf, kf = _qk_prep(proj, pos, invf, qg, kg, bd, "qk_prep")
    attn, lse = _attn_fwd(qf, kf, proj, "attn_fwd")
    dn_send, dn_recv, dn_thru, dn_land, dn_token = _exchange_start(
        [w_down_b], [down_landing], "gather", attn, "ag_down_start")
    mix = _attn_norm(attn, g_attn_out + dn_token[0, 0], "attn_norm")
    xc, hstate, mix = _rec_fwd(proj, mix, rcw8, rec_conv_b, wrg_bd, wig_bd, brg, big, lru_lambda, g_rec_out, "rec_fwd")
    _, (W_out, W_upT) = _exchange_wait(ag_send, ag_recv, late_thru, land_thru, "gather", hstate, "ag_late_wait")
    x2 = _mm(mix, W_out, "nn", F32, "out_proj", add=xs)

    act, da, db, pg, pu, h2 = _up_proj_act(x2, g_ffn, W_upT, fcw8, fcb, "up_proj_act")
    _, (W_down,) = _exchange_wait(dn_send, dn_recv, dn_thru, dn_land, "gather", h2, "ag_down_wait")
    dy, dyb, lparts = _mm(act, W_down, "nn", F32, "down_proj_loss", add=x2, loss_target=tgt, tm=512, tk=D_FF)

    g_down = _mm(act, dyb, "tn", BF16, "g_w_down", tk=4096)
    dpg, dpu, g_fcwg, g_fcwu, g_fcbg, g_fcbu = _ffn_bwd(dyb, W_down, da, db, pg, pu, fcw8, "ffn_bwd")
    g_upT = _mm(dpg, h2, "tn", BF16, "g_w_up_gate", tk=4096, o_rows=2 * D_FF)
    g_upT = _mm(dpu, h2, "tn", BF16, "g_w_up_up", tk=4096, into=g_upT, o_moff=D_FF // 1024)
    ffn_g = [g_upT.reshape(N_DEV, 2 * D_FF // N_DEV, 1024), g_down.reshape(N_DEV, D_FF // N_DEV, 1024)]
    rs_send, rs_recv, ffn_g, ffn_land, rs_token = _exchange_start(
        ffn_g, [_landing((N_PEERS,) + g.shape[1:], BF16) for g in ffn_g], "scatter", dpu, "rs_ffn_start")
    dx2, dx2b, g_gffn = _mm_norm_bwd([dpg, dpu], W_upT, x2, dy, g_ffn + rs_token[0, 0], "d_h2_norm_bwd", tm=1024, tk=1536)

    dmix = _mm(dx2b, W_out, "nt", F32, "d_mix")
    g_out = _mm(mix, dx2b, "tn", BF16, "g_w_out", tk=4096).reshape(N_DEV, D_MODEL // N_DEV, 1024)
    out_send, out_recv, (g_out,), out_land, out_token = _exchange_start(
        [g_out], [_landing((N_PEERS,) + g_out.shape[1:], BF16)], "scatter", dmix, "rs_out_start")
    do, delta, g_gattn = _attn_norm_bwd(dmix, attn, g_attn_out + out_token[0, 0], bd, "attn_norm_bwd")
    dqh, dkh, dv = _attn_bwd(qf, kf, proj, do, lse, delta, "attn_bwd")
    dqkv, g_qg, g_kg = _qk_prep_bwd(proj, dqh, dkh, dv, pos, invf, qg, kg, bd, "qk_prep_bwd")
    (drec, g_rcw, g_rcb, g_wrg, g_wig, g_brg, g_big, g_lam, g_grec) = _rec_bwd(
        dmix, proj, xc, hstate, rcw8, rec_conv_b, wrg_bd, wig_bd, brg, big, lru_lambda, g_rec_out, "rec_bwd")
    g_inT = _mm(dqkv, h1, "tn", BF16, "g_w_in_qkv", tm=512, tk=4096, o_rows=IN_W)
    g_inT = _mm(drec, h1, "tn", BF16, "g_w_in_rec", tm=512, tk=4096, into=g_inT, o_moff=3 * ATTN_W // 512)
    g_inT = g_inT.reshape(N_DEV, IN_W // N_DEV, 1024)
    in_send, in_recv, (g_inT,), in_land, in_token = _exchange_start(
        [g_inT], [_landing((N_PEERS,) + g_inT.shape[1:], BF16)], "scatter", drec, "rs_in_start")
    grad_x, _, g_gmix = _mm_norm_bwd([dqkv, drec], W_inT, xs, dx2, g_mix + in_token[0, 0], "d_h1_norm_bwd", tm=1024, tk=512)

    flat = _pack_small_grads([g_gmix, g_gffn, g_fcbg, g_fcbu], [g_rcb, g_brg, g_big, g_lam, g_gattn, g_grec, g_qg, g_kg],
                             g_rcw, g_fcwg, g_fcwu, g_wrg, g_wig, lparts.reshape(-1, D_MODEL), "pack_small_grads")
    srows = SMALL_ROWS // N_DEV
    flat = flat.reshape(N_DEV, srows, 1024)
    sm_send, sm_recv, (flat,), sm_land, sm_token = _exchange_start(
        [flat], [_landing((N_PEERS, srows, 1024), F32)], "scatter", grad_x, "ar_small_rs_start")

    devi = jnp.reshape(dev, (1,)).astype(jnp.int32)
    ffn_g, ffn_land = _exchange_wait(rs_send, rs_recv, ffn_g, ffn_land, "scatter", sm_token, "rs_ffn_wait")
    (g_out,), out_land = _exchange_wait(out_send, out_recv, [g_out], out_land, "scatter", sm_token, "rs_out_wait")
    big_out = {"grad": {}, "delta": {}, "new_m": {}, "new_v": {}}

    def adam_big(nm, p, r):
        w_, m_, v_ = shards[nm]
        res = _adam_sharded(p, r, devi, w_, m_, v_, "adam_" + nm, transposed=nm in ("w_in", "w_up"))
        for kind, a in zip(("grad", "delta", "new_m", "new_v"), res):
            big_out[kind][nm] = a[None]
        return res[0]

    last = adam_big("w_up", ffn_g[0], ffn_land[0])
    (flat,), sm_land = _exchange_wait(sm_send, sm_recv, [flat], sm_land, "scatter", last, "ar_small_rs_wait")
    mine = _sum_slabs(flat, sm_land[0], devi, "sum_small_grads")
    sm_send, sm_recv, (mine,), sm_land, sm_token = _exchange_start(
        [mine], [_landing((SMALL_ROWS, 1024), F32, mine, dev * srows)], "gather", last, "ar_small_ag_start")
    adam_big("w_down", ffn_g[1], ffn_land[1])
    last = adam_big("w_out", g_out, out_land[0])
    _, (tot,) = _exchange_wait(sm_send, sm_recv, [mine], sm_land, "gather", last, "ar_small_ag_wait")
    (g_inT,), in_land = _exchange_wait(in_send, in_recv, [g_inT], in_land, "scatter", tot, "rs_in_wait")
    adam_big("w_in", g_inT, in_land[0])

    half = lambda r, h, shape: tot[r, 512 * h:512 * h + 512].reshape(shape)
    blocks = lambda h: tot[32:96, 512 * h:512 * h + 512].reshape(64, 8, 64).transpose(1, 0, 2)[None]
    fcw_full = jnp.concatenate([tot[14:23].reshape(1, 3, D_FF), tot[23:32].reshape(1, 3, D_FF)], axis=2)
    g_small = {
        "g_mix": tot[0:1], "g_ffn": tot[1:2], "ffn_conv_b": tot[2:8].reshape(1, 2 * D_FF),
        "rec_conv_b": half(8, 0, (1, 512)), "b_rg": half(8, 1, (1, 8, 64)), "b_ig": half(9, 0, (1, 8, 64)),
        "lru_lambda": half(9, 1, (1, 512)), "g_attn_out": half(10, 0, (1, 512)), "g_rec_out": half(10, 1, (1, 512)),
        "q_norm_g": half(11, 0, (N_HEADS, HEAD_DIM)).sum(0)[None], "k_norm_g": half(11, 1, (N_HEADS, HEAD_DIM)).sum(0)[None],
        "w_rg": blocks(0), "w_ig": blocks(1),
        "rec_conv_w": lax.dynamic_slice(tot[12:14].reshape(1, 4, REC_W), (0, 0, 64 * dev), (1, 4, 64)),
        "ffn_conv_w": lax.dynamic_slice(fcw_full, (0, 0, 768 * dev), (1, 3, 768))}
    loss = 0.5 / D_MODEL * jnp.sum(tot[96])
    given = dict(rec_conv_w=rec_conv_w, ffn_conv_w=ffn_conv_w,g_mix=g_mix, q_norm_g=q_norm_g, k_norm_g=k_norm_g, rec_conv_b=rec_conv_b, w_rg=w_rg, b_rg=b_rg, w_ig=w_ig,
                 b_ig=b_ig, lru_lambda=lru_lambda, g_attn_out=g_attn_out, g_rec_out=g_rec_out, g_ffn=g_ffn, ffn_conv_b=ffn_conv_b)
    given_m = dict(rec_conv_w=m_rec_conv_w, ffn_conv_w=m_ffn_conv_w, g_mix=m_g_mix, q_norm_g=m_q_norm_g, k_norm_g=m_k_norm_g, rec_conv_b=m_rec_conv_b, w_rg=m_w_rg, b_rg=m_b_rg,
                   w_ig=m_w_ig, b_ig=m_b_ig, lru_lambda=m_lru_lambda, g_attn_out=m_g_attn_out, g_rec_out=m_g_rec_out,
                   g_ffn=m_g_ffn, ffn_conv_b=m_ffn_conv_b)
    given_v = dict(rec_conv_w=v_rec_conv_w, ffn_conv_w=v_ffn_conv_w, g_mix=v_g_mix, q_norm_g=v_q_norm_g, k_norm_g=v_k_norm_g, rec_conv_b=v_rec_conv_b, w_rg=v_w_rg, b_rg=v_b_rg,
                   w_ig=v_w_ig, b_ig=v_b_ig, lru_lambda=v_lru_lambda, g_attn_out=v_g_attn_out, g_rec_out=v_g_rec_out,
                   g_ffn=v_g_ffn, ffn_conv_b=v_ffn_conv_b)
    small = sorted(given)
    ds, m2s, v2s = _adam_small([given[k] for k in small], [g_small[k] for k in small], [given_m[k] for k in small],
                               [given_v[k] for k in small], "adam_small")
    small_out = {"grad": g_small, "delta": dict(zip(small, ds)), "new_m": dict(zip(small, m2s)), "new_v": dict(zip(small, v2s))}

    order = ("g_mix", "w_in", "q_norm_g", "k_norm_g", "rec_conv_w", "rec_conv_b", "w_rg", "b_rg", "w_ig", "b_ig",
             "lru_lambda", "g_attn_out", "g_rec_out", "w_out", "g_ffn", "w_up", "ffn_conv_w", "ffn_conv_b", "w_down")
    outs = [loss, grad_x.reshape(1, T, D_MODEL)]
    for kind in ("grad", "delta", "new_m", "new_v"):
        for name in order:
            outs.append(big_out[kind][name] if name in big_out[kind] else small_out[kind][name])
    return tuple(outs)
```

```python
import math

import numpy as np
import jax
import jax.numpy as jnp
from jax import lax
from jax.experimental import pallas as pl
from jax.experimental.pallas import tpu as pltpu

F32 = jnp.float32
BF16 = jnp.bfloat16

D_MODEL = 1024
HEAD_DIM = 64
ATTN_W = 512
REC_W = 512
N_HEADS = 8
D_FF = 3072
IN_W = 2560
REC_CONV = 4
FFN_CONV = 3
LRU_C = 8.0
ROPE_THETA = 10000.0
EPS = 1e-6
NEG_INF = -1e30
QBLK = 128
DILATIONS = (1, 4, 16)
N_DEV = 8
SMALL_ROWS = 128
ADAM_LR, ADAM_B1, ADAM_B2, ADAM_EPS, ADAM_WD, ADAM_STEP = 0.001, 0.9, 0.999, 1e-08, 0.01, 10
MESH = pl.DeviceIdType.MESH
ANY = pl.BlockSpec(memory_space=pl.ANY)


def _call(body, *, name, **kw):
    return pl.pallas_call(body, name=name, **kw)


def _params(*sem):
    return pltpu.CompilerParams(dimension_semantics=sem, vmem_limit_bytes=56 * 1024 * 1024)


_GELU_C = math.sqrt(2.0 / math.pi)
_GELU_A = 0.044715


def _gelu(x):
    return (0.5 * x) * (1.0 + jnp.tanh(x * (_GELU_C + (_GELU_C * _GELU_A) * (x * x))))


def _gelu_and_grad(x):
    x2 = x * x
    u = 1.0 + jnp.tanh(x * (_GELU_C + (_GELU_C * _GELU_A) * x2))
    hx = 0.5 * x
    return hx * u, 0.5 * u + (hx * ((2.0 - u) * u)) * (_GELU_C + (3.0 * _GELU_C * _GELU_A) * x2)


def _sigmoid(x):
    return 1.0 / (1.0 + jnp.exp(-x))


def _softplus_neg(lam):
    y = jnp.exp(-jnp.abs(lam))
    u = 1.0 + y
    log1p = jnp.where(u == 1.0, y, jnp.log(u) * y / jnp.where(u == 1.0, 1.0, u - 1.0))
    return jnp.maximum(-lam, 0.0) + log1p


_NN = (((1,), (0,)), ((), ()))
_NT = (((1,), (1,)), ((), ()))
_TN = (((0,), (0,)), ((), ()))


def _dot(a, b, dims=_NN):
    return lax.dot_general(a, b, dims, preferred_element_type=F32)


def _group_mean(v, bd):
    hi = v.astype(BF16)
    lo = (v - hi.astype(F32)).astype(BF16)
    w = bd.shape[0]
    return jnp.concatenate([_dot(hi[:, c:c + w], bd) + _dot(lo[:, c:c + w], bd) for c in range(0, v.shape[1], w)], axis=1)


def _rope_tables(pos_ref, invf_ref):
    ang = pos_ref[...].astype(F32) * invf_ref[:, :2 * HEAD_DIM]
    reps = invf_ref.shape[1] // (2 * HEAD_DIM)
    return jnp.tile(jnp.cos(ang), (1, reps)), jnp.tile(jnp.sin(ang), (1, reps))


def _shift_down(x, halo, s):
    rolled = pltpu.roll(x, s, 0)
    hr = pltpu.roll(halo, s, 0)
    row = lax.broadcasted_iota(jnp.int32, hr.shape, 0)
    first = jnp.where(row < s, hr, rolled[:8])
    return jnp.concatenate([first, rolled[8:]], axis=0)


def _shift_up(x, halo, s):
    n = x.shape[0]
    rolled = pltpu.roll(x, n - s, 0)
    hr = pltpu.roll(halo, 8 - s, 0)
    row = lax.broadcasted_iota(jnp.int32, hr.shape, 0)
    last = jnp.where(row >= 8 - s, hr, rolled[n - 8:])
    return jnp.concatenate([rolled[:n - 8], last], axis=0)


def _scan_fwd(a, u):
    n, w = a.shape
    a3, u3 = a.reshape(n // 8, 8, w), u.reshape(n // 8, 8, w)
    row = lax.broadcasted_iota(jnp.int32, a3.shape, 1)
    for s in (1, 2, 4):
        a_s = jnp.where(row < s, 1.0, pltpu.roll(a3, s, 1))
        u_s = jnp.where(row < s, 0.0, pltpu.roll(u3, s, 1))
        u3 = u3 + a3 * u_s
        a3 = a3 * a_s
    ps, hs = [a3[0]], [u3[0]]
    for k in range(1, n // 8):
        ps.append(a3[k] * ps[-1][7:8, :])
        hs.append(u3[k] + a3[k] * hs[-1][7:8, :])
    return jnp.concatenate(ps, axis=0), jnp.concatenate(hs, axis=0)


def _scan_bwd(b, v):
    n, w = b.shape
    b3, v3 = b.reshape(n // 8, 8, w), v.reshape(n // 8, 8, w)
    row = lax.broadcasted_iota(jnp.int32, b3.shape, 1)
    for s in (1, 2, 4):
        b_s = jnp.where(row >= 8 - s, 1.0, pltpu.roll(b3, 8 - s, 1))
        v_s = jnp.where(row >= 8 - s, 0.0, pltpu.roll(v3, 8 - s, 1))
        v3 = v3 + b3 * v_s
        b3 = b3 * b_s
    last = n // 8 - 1
    ps, gs = [b3[last]], [v3[last]]
    for k in range(last - 1, -1, -1):
        ps.append(b3[k] * ps[-1][0:1, :])
        gs.append(v3[k] + b3[k] * gs[-1][0:1, :])
    return jnp.concatenate(ps[::-1], axis=0), jnp.concatenate(gs[::-1], axis=0)


def _rot_half(y):
    n = y.shape[1]
    lane = lax.broadcasted_iota(jnp.int32, y.shape, 1) & (HEAD_DIM - 1)
    return jnp.where(lane < HEAD_DIM // 2, -pltpu.roll(y, n - HEAD_DIM // 2, 1), pltpu.roll(y, HEAD_DIM // 2, 1))


def _row_tile(r, cap=256):
    return max(t for t in range(16, cap + 1, 16) if r % t == 0)


def _all_gather(shards, name):
    na = len(shards)
    ms = [s.shape[0] for s in shards]

    def body(*refs):
        x_refs, out_refs = refs[:na], refs[na:2 * na]
        send_sems, recv_sems, local_sems = refs[2 * na:]
        x, y, c = lax.axis_index("x"), lax.axis_index("y"), lax.axis_index("c")
        me, sibling = (x, y, c), (x, y, 1 - c)
        chips = [(1 - x, y), (x, 1 - y), (1 - x, 1 - y)]

        def rows(a, px, py, pc):
            return out_refs[a].at[pl.ds(pl.multiple_of((4 * px + 2 * py + pc) * ms[a], 8), ms[a]), :]

        def copy(a, k, block, to, src=None):
            return pltpu.make_async_remote_copy(
                src_ref=rows(a, *block) if src is None else src, dst_ref=rows(a, *block),
                send_sem=send_sems.at[7 * a + k], recv_sem=recv_sems.at[7 * a + k], device_id=to, device_id_type=MESH)

        mine = [pltpu.make_async_copy(x_refs[a], rows(a, *me), local_sems.at[a]) for a in range(na)]
        first = []
        for a in range(na):
            mine[a].start()
            first.append(copy(a, 0, me, sibling, src=x_refs[a]))
            first += [copy(a, 1 + j, me, (*chip, c), src=x_refs[a]) for j, chip in enumerate(chips)]
        for cp in first:
            cp.start()
        passed = []
        for a in range(na):
            for j, chip in enumerate(chips):
                copy(a, 1 + j, (*chip, c), me).wait_recv()
                fw = copy(a, 4 + j, (*chip, c), sibling)
                fw.start()
                passed.append(fw)
        for a in range(na):
            copy(a, 0, sibling, me).wait_recv()
            for j, chip in enumerate(chips):
                copy(a, 4 + j, (*chip, 1 - c), me).wait_recv()
        for cp in first + passed:
            cp.wait_send()
        for cp in mine:
            cp.wait()

    return _call(
        body, name=name, out_shape=[jax.ShapeDtypeStruct((N_DEV * s.shape[0], s.shape[1]), s.dtype) for s in shards],
        in_specs=[ANY] * na, out_specs=[ANY] * na,
        scratch_shapes=[pltpu.SemaphoreType.DMA((7 * na,)), pltpu.SemaphoreType.DMA((7 * na,)),
                        pltpu.SemaphoreType.DMA((na,))],
    )(*shards)


HBM = pl.BlockSpec(memory_space=pltpu.HBM)
SEM = pl.BlockSpec(memory_space=pltpu.SEMAPHORE)
EFFECT = pltpu.SideEffectType.DATAFLOW_SIDE_EFFECTING
N_PEERS = N_DEV - 1


def _peer(k):
    x, y, c = lax.axis_index("x"), lax.axis_index("y"), lax.axis_index("c")
    b = k + 1
    flip = lambda v, bit: 1 - v if bit else v
    return flip(x, b & 4), flip(y, b & 2), flip(c, b & 1)


def _in_hbm(a):
    return pltpu.with_memory_space_constraint(a, pltpu.HBM)


def _split_copy_descr(na, kind, src_refs, land_refs, send_sems, recv_sems):
    x, y, c = lax.axis_index("x"), lax.axis_index("y"), lax.axis_index("c")
    me = 4 * x + 2 * y + c
    copies = []
    for a in range(na):
        for k in range(N_PEERS):
            px, py, pc = _peer(k)
            if kind == "gather":
                m = src_refs[a].shape[0]
                src, dst = src_refs[a], land_refs[a].at[pl.ds(pl.multiple_of(me * m, 8), m), :]
            else:
                src, dst = src_refs[a].at[4 * px + 2 * py + pc], land_refs[a].at[k]
            copies.append(pltpu.make_async_remote_copy(
                src_ref=src, dst_ref=dst, send_sem=send_sems.at[N_PEERS * a + k], recv_sem=recv_sems.at[N_PEERS * a + k],
                device_id=(px, py, pc), device_id_type=MESH))
    return copies


def _landing(shape, dtype, own=None, at=None):
    buf = lax.empty(shape, dtype)
    return buf if own is None else lax.dynamic_update_slice(buf, own, (at, 0))


def _exchange_start(srcs, lands, kind, after, name):
    na = len(srcs)
    land_shapes = [l.shape for l in lands]

    def body(*refs):
        src_refs, land_refs = refs[:na], refs[na:2 * na]
        send_sems, recv_sems = refs[2 * na + 1], refs[2 * na + 2]
        token = refs[-1]
        for cp in _split_copy_descr(na, kind, src_refs, land_refs, send_sems, recv_sems):
            cp.start()
        token[...] = jnp.zeros_like(token)

    lands = [_in_hbm(l) for l in lands]
    sem = pltpu.SemaphoreType.DMA((N_PEERS * na,))
    outs = _call(
        body, name=name,
        out_shape=[sem, sem] + [pltpu.HBM(s.shape, s.dtype) for s in srcs] + [pltpu.HBM(s, srcs[0].dtype) for s in land_shapes]
        + [jax.ShapeDtypeStruct((8, 128), F32)],
        in_specs=[HBM] * (2 * na) + [ANY], out_specs=[SEM, SEM] + [HBM] * (2 * na) + [pl.BlockSpec(memory_space=pltpu.VMEM)],
        input_output_aliases={i: 2 + i for i in range(2 * na)},
        compiler_params=pltpu.CompilerParams(has_side_effects=EFFECT),
    )(*[_in_hbm(s) for s in srcs], *lands, after)
    return outs[0], outs[1], outs[2:2 + na], outs[2 + na:2 + 2 * na], outs[-1]


def _exchange_wait(send_sems, recv_sems, srcs, lands, kind, after, name):
    na = len(srcs)

    def body(*refs):
        src_refs, land_refs = refs[:na], refs[na:2 * na]
        s_sems, r_sems = refs[2 * na], refs[2 * na + 1]
        for cp in _split_copy_descr(na, kind, src_refs, land_refs, s_sems, r_sems):
            cp.wait_send()
            cp.wait_recv()

    outs = _call(
        body, name=name, out_shape=[pltpu.HBM(s.shape, s.dtype) for s in srcs] + [pltpu.HBM(l.shape, l.dtype) for l in lands],
        in_specs=[HBM] * (2 * na) + [SEM, SEM, ANY], out_specs=[HBM] * (2 * na),
        input_output_aliases={i: i for i in range(2 * na)},
        compiler_params=pltpu.CompilerParams(has_side_effects=EFFECT),
    )(*srcs, *lands, send_sems, recv_sems, after)
    return outs[:na], outs[na:]


def _mm(a, b, mode, out_dtype, name, add=None, tm=1024, tn=1024, tk=1024, into=None, o_rows=None, o_moff=0,
        loss_target=None):
    if mode == "tn":
        K, M = a.shape
    else:
        M, K = a.shape
    N = b.shape[0] if mode == "nt" else b.shape[1]
    tm, tn, tk = min(tm, M), min(tn, N), min(tk, K)
    assert M % tm == 0 and N % tn == 0 and K % tk == 0, (name, M, N, K)
    nk = K // tk
    if mode == "nn":
        a_spec = pl.BlockSpec((tm, tk), lambda i, j, kk: (i, kk))
        b_spec, dims = pl.BlockSpec((tk, tn), lambda i, j, kk: (kk, j)), _NN
    elif mode == "nt":
        a_spec = pl.BlockSpec((tm, tk), lambda i, j, kk: (i, kk))
        b_spec, dims = pl.BlockSpec((tn, tk), lambda i, j, kk: (j, kk)), _NT
    else:
        a_spec = pl.BlockSpec((tk, tm), lambda i, j, kk: (kk, i))
        b_spec, dims = pl.BlockSpec((tk, tn), lambda i, j, kk: (kk, j)), _TN
    o_spec = pl.BlockSpec((tm, tn), lambda i, j, kk: (i + o_moff, j))
    has_add, has_into, has_loss = add is not None, into is not None, loss_target is not None
    assert not has_loss or (has_add and tn == N and not has_into)
    n_in = 2 + has_add + has_loss + has_into

    def body(*refs):
        a_ref, b_ref = refs[0], refs[1]
        add_ref = refs[2] if has_add else None
        outs = refs[n_in:]

        def finish(r):
            if has_add:
                r = r + add_ref[...]
            if has_loss:
                e = r - refs[3][...]
                dy = e * (1.0 / N)
                outs[0][...] = dy
                outs[1][...] = dy.astype(BF16)
                outs[2][...] = jnp.sum(e * e, axis=0, keepdims=True)[None]
            else:
                outs[0][...] = r.astype(out_dtype)

        if nk == 1:
            finish(_dot(a_ref[...], b_ref[...], dims))
        else:
            acc = refs[-1]
            kk = pl.program_id(2)

            @pl.when(kk == 0)
            def _():
                acc[...] = _dot(a_ref[...], b_ref[...], dims)

            @pl.when((kk > 0) & (kk < nk - 1))
            def _():
                acc[...] += _dot(a_ref[...], b_ref[...], dims)

            @pl.when(kk == nk - 1)
            def _():
                finish(acc[...] + _dot(a_ref[...], b_ref[...], dims))

    tile = pl.BlockSpec((tm, tn), lambda i, j, kk: (i, j))
    ins = [a, b] + ([add] if has_add else []) + ([loss_target] if has_loss else []) + ([into] if has_into else [])
    specs = [a_spec, b_spec] + [tile] * (has_add + has_loss) + ([ANY] if has_into else [])
    rows = into.shape[0] if has_into else (o_rows if o_rows is not None else M)
    if has_loss:
        out_specs = [tile, tile, pl.BlockSpec((1, 1, N), lambda i, j, kk: (i, 0, 0))]
        out_shape = [jax.ShapeDtypeStruct((M, N), F32), jax.ShapeDtypeStruct((M, N), BF16), jax.ShapeDtypeStruct((M // tm, 1, N), F32)]
    else:
        out_specs, out_shape = o_spec, jax.ShapeDtypeStruct((rows, N), out_dtype)
    return _call(
        body, name=name, grid=(M // tm, N // tn, nk), in_specs=specs, out_specs=out_specs, out_shape=out_shape,
        scratch_shapes=[pltpu.VMEM((tm, tn), F32)] if nk > 1 else [],
        input_output_aliases={len(ins) - 1: 0} if has_into else {},
        compiler_params=_params("parallel", "parallel", "arbitrary"),
    )(*ins)


def _mm_norm_bwd(parts, b, x, resid, g, name, tm=512, tk=512):
    T, N = x.shape
    counts = [p.shape[1] // tk for p in parts]
    starts = [sum(counts[:i]) for i in range(len(parts))]
    nsteps = sum(counts)
    assert all(p.shape[1] % tk == 0 for p in parts) and b.shape == (nsteps * tk, N)
    npart = len(parts)

    def body(*refs):
        a_refs, b_ref, x_ref, res_ref, g_ref = refs[:npart], refs[npart], refs[npart + 1], refs[npart + 2], refs[npart + 3]
        dx_ref, dxb_ref, dg_ref, acc = refs[npart + 4:]
        i, s = pl.program_id(0), pl.program_id(1)

        @pl.when((i == 0) & (s == 0))
        def _():
            dg_ref[...] = jnp.zeros_like(dg_ref)

        for p in range(npart):
            @pl.when((s >= starts[p]) & (s < starts[p] + counts[p]))
            def _(p=p):
                d = _dot(a_refs[p][...], b_ref[...])

                @pl.when(s == 0)
                def _():
                    acc[...] = d

                @pl.when(s > 0)
                def _():
                    acc[...] += d

        @pl.when(s == nsteps - 1)
        def _():
            xv, dhv = x_ref[...], acc[...]
            r = lax.rsqrt(jnp.mean(xv * xv, axis=-1, keepdims=True) + EPS)
            gd = dhv * g_ref[...]
            m = jnp.mean(gd * xv, axis=-1, keepdims=True)
            dx = res_ref[...] + r * gd - xv * (r * r * r) * m
            dx_ref[...] = dx
            dxb_ref[...] = dx.astype(BF16)
            dg_ref[...] += jnp.sum(dhv * xv * r, axis=0, keepdims=True)

    a_specs = [pl.BlockSpec((tm, tk), lambda i, s, st=st, c=c: (i, jnp.clip(s - st, 0, c - 1))) for st, c in zip(starts, counts)]
    row = pl.BlockSpec((tm, N), lambda i, s: (i, 0))
    vec = pl.BlockSpec((1, N), lambda i, s: (0, 0))
    return _call(
        body, name=name, grid=(T // tm, nsteps),
        in_specs=a_specs + [pl.BlockSpec((tk, N), lambda i, s: (s, 0)), row, row, vec], out_specs=[row, row, vec],
        out_shape=[jax.ShapeDtypeStruct((T, N), F32), jax.ShapeDtypeStruct((T, N), BF16), jax.ShapeDtypeStruct((1, N), F32)],
        scratch_shapes=[pltpu.VMEM((tm, N), F32)], compiler_params=_params("arbitrary", "arbitrary"),
    )(*parts, b, x, resid, g)


def _norm_proj(x, g, wT, name, tm=1024, tn=1280):
    T, K = x.shape
    N = wT.shape[0]

    def body(x_ref, g_ref, w_ref, o_ref, h_ref):
        xv = x_ref[...]
        r = lax.rsqrt(jnp.mean(xv * xv, axis=-1, keepdims=True) + EPS)
        hv = (xv * r * g_ref[...]).astype(BF16)

        @pl.when(pl.program_id(1) == 0)
        def _():
            h_ref[...] = hv

        o_ref[...] = _dot(hv, w_ref[...], _NT)

    return _call(
        body, name=name, grid=(T // tm, N // tn),
        in_specs=[pl.BlockSpec((tm, K), lambda i, j: (i, 0)), pl.BlockSpec((1, K), lambda i, j: (0, 0)),
                  pl.BlockSpec((tn, K), lambda i, j: (j, 0))],
        out_specs=[pl.BlockSpec((tm, tn), lambda i, j: (i, j)), pl.BlockSpec((tm, K), lambda i, j: (i, 0))],
        out_shape=[jax.ShapeDtypeStruct((T, N), F32), jax.ShapeDtypeStruct((T, K), BF16)],
        compiler_params=_params("parallel", "arbitrary"),
    )(x, g, wT)


def _qk_prep(proj, pos, invf, qg, kg, bd, name, tm=512):
    T = proj.shape[0]

    def body(q_ref, k_ref, pos_ref, invf_ref, qg_ref, kg_ref, bd_ref, qo_ref, ko_ref):
        cos, sin = _rope_tables(pos_ref, invf_ref)

        def prep(xv, gv, scale):
            r = lax.rsqrt(_group_mean(xv * xv, bd_ref[...]) + EPS)
            yv = xv * r * gv
            return ((yv * cos + _rot_half(yv) * sin) * scale).astype(BF16).astype(F32)

        qo_ref[...] = prep(q_ref[...], qg_ref[...], HEAD_DIM ** -0.5)
        ko_ref[...] = prep(k_ref[...], kg_ref[...], 1.0)

    col = lambda j: pl.BlockSpec((tm, ATTN_W), lambda i, j=j: (i, j))
    vec = pl.BlockSpec((1, ATTN_W), lambda i: (0, 0))
    out = pl.BlockSpec((tm, ATTN_W), lambda i: (i, 0))
    return _call(
        body, name=name, grid=(T // tm,),
        in_specs=[col(0), col(1), pl.BlockSpec((tm, 1), lambda i: (i, 0)), vec, vec, vec,
                  pl.BlockSpec((2 * HEAD_DIM, 2 * HEAD_DIM), lambda i: (0, 0))],
        out_specs=[out, out], out_shape=[jax.ShapeDtypeStruct((T, ATTN_W), F32)] * 2,
        compiler_params=_params("parallel"),
    )(proj, proj, pos, invf, qg, kg, bd)


def _qk_prep_bwd(proj, dqh, dkh, dv, pos, invf, qg, kg, bd, name, tm=512):
    T = proj.shape[0]

    def body(q_ref, k_ref, dq_ref, dk_ref, dv_ref, pos_ref, invf_ref, qg_ref, kg_ref, bd_ref, o_ref, gq_ref, gk_ref):
        @pl.when(pl.program_id(0) == 0)
        def _():
            gq_ref[...] = jnp.zeros_like(gq_ref)
            gk_ref[...] = jnp.zeros_like(gk_ref)

        cos, sin = _rope_tables(pos_ref, invf_ref)

        def back(xv, gv, dz, scale):
            dz = dz * scale
            dy = dz * cos - _rot_half(dz * sin)
            r = lax.rsqrt(_group_mean(xv * xv, bd_ref[...]) + EPS)
            gd = dy * gv
            m = _group_mean(gd * xv, bd_ref[...])
            dx = r * gd - xv * (r * r * r) * m
            return dx, jnp.sum(dy * xv * r, axis=0, keepdims=True)

        dxq, gs = back(q_ref[...], qg_ref[...], dq_ref[...], HEAD_DIM ** -0.5)
        gq_ref[...] += gs
        dxk, gs = back(k_ref[...], kg_ref[...], dk_ref[...], 1.0)
        gk_ref[...] += gs
        o_ref[...] = jnp.concatenate([dxq.astype(BF16), dxk.astype(BF16), dv_ref[...].astype(BF16)], axis=1)

    col = lambda j: pl.BlockSpec((tm, ATTN_W), lambda i, j=j: (i, j))
    row = pl.BlockSpec((tm, ATTN_W), lambda i: (i, 0))
    vec = pl.BlockSpec((1, ATTN_W), lambda i: (0, 0))
    return _call(
        body, name=name, grid=(T // tm,),
        in_specs=[col(0), col(1), row, row, row, pl.BlockSpec((tm, 1), lambda i: (i, 0)), vec, vec, vec,
                  pl.BlockSpec((2 * HEAD_DIM, 2 * HEAD_DIM), lambda i: (0, 0))],
        out_specs=[pl.BlockSpec((tm, 3 * ATTN_W), lambda i: (i, 0)), vec, vec],
        out_shape=[jax.ShapeDtypeStruct((T, 3 * ATTN_W), BF16)] + [jax.ShapeDtypeStruct((1, ATTN_W), F32)] * 2,
        compiler_params=_params("arbitrary"),
    )(proj, proj, dqh, dkh, dv, pos, invf, qg, kg, bd)


def _ld(ref, start, size, dil):
    return ref[pl.ds(start, size), :] if dil == 1 else ref[pl.ds(start, size, stride=dil), :]


def _st(ref, start, size, dil, val):
    if dil == 1:
        ref[pl.ds(start, size), :] = val
    else:
        ref[pl.ds(start, size, stride=dil), :] = val


def _attn_geometry(T, dil):
    nb = T // dil // QBLK
    if nb == 2:
        return 1, 2 * QBLK, 2 * QBLK
    return nb, QBLK, (2 * QBLK if nb >= 2 else QBLK)


ATTN_UNROLL = 4


def _attn_unit(j, u, dil, nit):
    return ATTN_UNROLL * j + u if dil >= ATTN_UNROLL else j + u * (nit // ATTN_UNROLL)


def _attn_block(it, dil, qb, kw):
    c, n = it & (dil - 1), lax.shift_right_logical(it, dil.bit_length() - 1)
    sq = n * (qb * dil) + c
    sk = jnp.maximum(n - (kw // qb - 1), 0) * (qb * dil) + c
    qi = lax.broadcasted_iota(jnp.int32, (2 * qb, kw), 0) & (qb - 1)
    kj = lax.broadcasted_iota(jnp.int32, (2 * qb, kw), 1)
    rel = jnp.where(n > 0, kw - qb, 0) + qi - kj
    return sq, sk, (rel >= 0) & (rel <= QBLK)


def _stack_heads(xv, head0):
    z = jnp.zeros_like(xv)
    return jnp.concatenate([jnp.where(head0, xv, z), jnp.where(head0, z, xv)], axis=0)


def _unstack_heads(x2, head0):
    qb = x2.shape[0] // 2
    return jnp.where(head0, x2[:qb], x2[qb:])


def _attn_fwd(qf, kf, proj, name):
    T = qf.shape[0]

    def body(q_ref, k_ref, v_ref, o_ref, lse_ref):
        for bi, dil in enumerate(DILATIONS):
            nb, qb, kw = _attn_geometry(T, dil)
            nit = nb * dil
            head0 = lax.broadcasted_iota(jnp.int32, (qb, 2 * HEAD_DIM), 1) < HEAD_DIM

            def step(j, carry, bi=bi, dil=dil, qb=qb, kw=kw, nit=nit, head0=head0):
                units = []
                for u in range(ATTN_UNROLL):
                    sq, sk, ok = _attn_block(_attn_unit(j, u, dil, nit), dil, qb, kw)
                    old = (_ld(o_ref, sq, qb, dil), _ld(lse_ref, sq, qb, dil)) if bi > 0 else None
                    units.append((sq, ok, _ld(q_ref, sq, qb, dil).astype(BF16), _ld(k_ref, sk, kw, dil).astype(BF16),
                                  _ld(v_ref, sk, kw, dil).astype(BF16), old))
                results = []
                for sq, ok, qv, kv, vv, old in units:
                    s = jnp.where(ok, _dot(_stack_heads(qv, head0), kv, _NT), NEG_INF)
                    m = jnp.max(s, axis=-1, keepdims=True)
                    p = jnp.exp(s - m).astype(BF16)
                    acc = _dot(p, jnp.concatenate([vv, jnp.ones_like(vv)], axis=1))
                    l = acc[:, 2 * HEAD_DIM:]
                    o_new = _unstack_heads(acc[:, :2 * HEAD_DIM] / l, head0)
                    l_new = _unstack_heads(m + jnp.log(l), head0)
                    if bi > 0:
                        o_old, l_old = old
                        mx = jnp.maximum(l_old, l_new)
                        e0, e1 = jnp.exp(l_old - mx), jnp.exp(l_new - mx)
                        z = e0 + e1
                        o_new = (e0 * o_old + e1 * o_new) / z
                        l_new = mx + jnp.log(z)
                    results.append((sq, o_new, l_new))
                for sq, o_new, l_new in results:
                    _st(o_ref, sq, qb, dil, o_new)
                    _st(lse_ref, sq, qb, dil, l_new)
                return carry

            lax.fori_loop(0, nit // ATTN_UNROLL, step, 0)

    blk = lambda off: pl.BlockSpec((T, 2 * HEAD_DIM), lambda hp, off=off: (0, off + hp))
    return _call(
        body, name=name, grid=(4,), in_specs=[blk(0), blk(0), blk(8)], out_specs=[blk(0), blk(0)],
        out_shape=[jax.ShapeDtypeStruct((T, ATTN_W), F32)] * 2, compiler_params=_params("parallel"),
    )(qf, kf, proj)


def _attn_bwd(qf, kf, proj, do, lse, delta, name):
    T = qf.shape[0]

    def body(q_ref, k_ref, v_ref, do_ref, lse_ref, dl_ref, dq_ref, dk_ref, dv_ref):
        for ref in (dq_ref, dk_ref, dv_ref):
            ref[...] = jnp.zeros_like(ref)
        for dil in DILATIONS:
            nb, qb, kw = _attn_geometry(T, dil)
            nit = nb * dil
            head0 = lax.broadcasted_iota(jnp.int32, (qb, 2 * HEAD_DIM), 1) < HEAD_DIM

            def step(j, carry, dil=dil, qb=qb, kw=kw, nit=nit, head0=head0):
                units = []
                for u in range(ATTN_UNROLL):
                    sq, sk, ok = _attn_block(_attn_unit(j, u, dil, nit), dil, qb, kw)
                    lsev, dlv = _ld(lse_ref, sq, qb, dil), _ld(dl_ref, sq, qb, dil)
                    units.append((sq, sk, ok, _ld(q_ref, sq, qb, dil).astype(BF16), _ld(do_ref, sq, qb, dil).astype(BF16),
                                  jnp.concatenate([lsev[:, 0:1], lsev[:, HEAD_DIM:HEAD_DIM + 1]], axis=0),
                                  jnp.concatenate([dlv[:, 0:1], dlv[:, HEAD_DIM:HEAD_DIM + 1]], axis=0),
                                  _ld(k_ref, sk, kw, dil).astype(BF16), _ld(v_ref, sk, kw, dil).astype(BF16),
                                  _ld(dq_ref, sq, qb, dil), _ld(dk_ref, sk, kw, dil), _ld(dv_ref, sk, kw, dil)))
                results = []
                for sq, sk, ok, qv, dov, lse2, dl2, kv, vv, dq0, dk0, dv0 in units:
                    q2, do2 = _stack_heads(qv, head0), _stack_heads(dov, head0)
                    p = jnp.where(ok, jnp.exp(_dot(q2, kv, _NT) - lse2), 0.0)
                    ds = (p * (_dot(do2, vv, _NT) - dl2)).astype(BF16)
                    results.append((sq, sk, dq0 + _unstack_heads(_dot(ds, kv), head0),
                                    dk0 + _dot(ds, q2, _TN), dv0 + _dot(p.astype(BF16), do2, _TN)))
                for sq, sk, dq, dk, dv in results:
                    _st(dq_ref, sq, qb, dil, dq)
                    _st(dk_ref, sk, kw, dil, dk)
                    _st(dv_ref, sk, kw, dil, dv)
                return carry

            lax.fori_loop(0, nit // ATTN_UNROLL, step, 0)

    blk = lambda off: pl.BlockSpec((T, 2 * HEAD_DIM), lambda hp, off=off: (0, off + hp))
    return _call(
        body, name=name, grid=(4,), in_specs=[blk(0), blk(0), blk(8), blk(0), blk(0), blk(0)], out_specs=[blk(0)] * 3,
        out_shape=[jax.ShapeDtypeStruct((T, ATTN_W), F32)] * 3, compiler_params=_params("parallel"),
    )(qf, kf, proj, do, lse, delta)


def _attn_norm(attn, g, name, tm=512):
    T = attn.shape[0]

    def body(a_ref, g_ref, o_ref):
        av = a_ref[...]
        r = lax.rsqrt(jnp.mean(av * av, axis=-1, keepdims=True) + EPS)
        o_ref[...] = (av * r * g_ref[...]).astype(BF16)

    row = pl.BlockSpec((tm, ATTN_W), lambda i: (i, 0))
    return _call(
        body, name=name, grid=(T // tm,), in_specs=[row, pl.BlockSpec((1, ATTN_W), lambda i: (0, 0))], out_specs=row,
        out_shape=jax.ShapeDtypeStruct((T, 2 * ATTN_W), BF16), compiler_params=_params("parallel"),
    )(attn, g)


def _attn_norm_bwd(dmix, attn, g, bd, name, tm=512):
    T = attn.shape[0]

    def body(d_ref, a_ref, g_ref, bd_ref, do_ref, dl_ref, dg_ref):
        @pl.when(pl.program_id(0) == 0)
        def _():
            dg_ref[...] = jnp.zeros_like(dg_ref)

        dy, av = d_ref[...], a_ref[...]
        r = lax.rsqrt(jnp.mean(av * av, axis=-1, keepdims=True) + EPS)
        gd = dy * g_ref[...]
        m = jnp.mean(gd * av, axis=-1, keepdims=True)
        da = r * gd - av * (r * r * r) * m
        do_ref[...] = da
        dl_ref[...] = _group_mean(da * av, bd_ref[...]) * float(HEAD_DIM)
        dg_ref[...] += jnp.sum(dy * av * r, axis=0, keepdims=True)

    row = pl.BlockSpec((tm, ATTN_W), lambda i: (i, 0))
    vec = pl.BlockSpec((1, ATTN_W), lambda i: (0, 0))
    return _call(
        body, name=name, grid=(T // tm,),
        in_specs=[row, row, vec, pl.BlockSpec((2 * HEAD_DIM, 2 * HEAD_DIM), lambda i: (0, 0))], out_specs=[row, row, vec],
        out_shape=[jax.ShapeDtypeStruct((T, ATTN_W), F32)] * 2 + [jax.ShapeDtypeStruct((1, ATTN_W), F32)],
        compiler_params=_params("arbitrary"),
    )(dmix, attn, g, bd)


def _rec_gates(xc, wrg_ref, wig_ref, brg_ref, big_ref, lam_ref):
    xb = xc.astype(BF16)
    r = _sigmoid(_dot(xb, wrg_ref[...]) + brg_ref[...])
    ig = _sigmoid(_dot(xb, wig_ref[...]) + big_ref[...])
    sp = _softplus_neg(lam_ref[...])
    log_a = -LRU_C * r * sp
    a = jnp.exp(log_a)
    th = jnp.tanh(log_a)
    mult = jnp.sqrt(-2.0 * th / (1.0 - th))
    return xb, r, ig, sp, a, mult


def _rec_fwd(proj, mix, cw, cb, wrg, wig, brg, big, lam, g, name, tm=256):
    T = proj.shape[0]
    hb = tm // 8

    def body(xr_ref, halo_ref, gr_ref, cw_ref, cb_ref, wrg_ref, wig_ref, brg_ref, big_ref, lam_ref, g_ref, mix_ref,
             xc_ref, h_ref, out_ref, carry):
        i = pl.program_id(0)

        @pl.when(i == 0)
        def _():
            carry[...] = jnp.zeros_like(carry)

        xr = xr_ref[...]
        halo = jnp.where(i > 0, halo_ref[...], 0.0)
        xc = cb_ref[...] + cw_ref[3:4, :] * xr
        for s in range(1, REC_CONV):
            xc = xc + cw_ref[3 - s:4 - s, :] * _shift_down(xr, halo, s)
        xc_ref[...] = xc
        _, _, ig, _, a, mult = _rec_gates(xc, wrg_ref, wig_ref, brg_ref, big_ref, lam_ref)
        pa, hl = _scan_fwd(a, mult * (ig * xc))
        h = hl + pa * carry[0:1, :]
        h_ref[...] = h
        carry[0:1, :] = h_ref[pl.ds(tm - 1, 1), :]
        hg = h * _gelu(gr_ref[...])
        r = lax.rsqrt(jnp.mean(hg * hg, axis=-1, keepdims=True) + EPS)
        out_ref[...] = (hg * r * g_ref[...]).astype(BF16)

    vec = pl.BlockSpec((1, REC_W), lambda i: (0, 0))
    row = pl.BlockSpec((tm, REC_W), lambda i: (i, 0))
    mat = pl.BlockSpec((REC_W, REC_W), lambda i: (0, 0))
    return _call(
        body, name=name, grid=(T // tm,),
        in_specs=[pl.BlockSpec((tm, REC_W), lambda i: (i, 3)),
                  pl.BlockSpec((8, REC_W), lambda i: (jnp.maximum(i * hb - 1, 0), 3)),
                  pl.BlockSpec((tm, REC_W), lambda i: (i, 4)),
                  pl.BlockSpec((8, REC_W), lambda i: (0, 0)), vec, mat, mat, vec, vec, vec, vec, ANY],
        out_specs=[row, row, pl.BlockSpec((tm, REC_W), lambda i: (i, 1))],
        out_shape=[jax.ShapeDtypeStruct((T, REC_W), F32)] * 2 + [jax.ShapeDtypeStruct(mix.shape, BF16)],
        scratch_shapes=[pltpu.VMEM((8, REC_W), F32)], input_output_aliases={11: 2},
        compiler_params=_params("arbitrary"),
    )(proj, proj, proj, cw, cb, wrg, wig, brg, big, lam, g, mix)


def _rec_bwd(dmix, proj, xc, h, cw, cb, wrg, wig, brg, big, lam, g, name, tm=256):
    T = proj.shape[0]
    nt = T // tm
    hb = tm // 8

    def body(d_ref, xr_ref, xhalo_ref, gr_ref, xc_ref, h_ref, hhalo_ref, cw_ref, cb_ref, wrg_ref, wig_ref, brg_ref,
             big_ref, lam_ref, g_ref,
             drec_ref, gcw_ref, gcb_ref, gwrg_ref, gwig_ref, gbrg_ref, gbig_ref, glam_ref, gg_ref,
             g_carry, a_first, dxc_next, gsp):
        i = pl.program_id(0)
        first_tile = i == nt - 1

        @pl.when(i == 0)
        def _():
            for ref in (gcw_ref, gcb_ref, gwrg_ref, gwig_ref, gbrg_ref, gbig_ref, glam_ref, gg_ref,
                        g_carry, a_first, dxc_next, gsp):
                ref[...] = jnp.zeros_like(ref)

        xr, xc, hv = xr_ref[...], xc_ref[...], h_ref[...]
        xhalo = jnp.where(first_tile, 0.0, xhalo_ref[...])
        hhalo = jnp.where(first_tile, 0.0, hhalo_ref[...])
        xb, r, ig, sp, a, mult = _rec_gates(xc, wrg_ref, wig_ref, brg_ref, big_ref, lam_ref)
        h_prev = _shift_down(hv, hhalo, 1)
        ge, dge = _gelu_and_grad(gr_ref[...])
        hg = hv * ge
        rr = lax.rsqrt(jnp.mean(hg * hg, axis=-1, keepdims=True) + EPS)
        dy = d_ref[...]
        gd = dy * g_ref[...]
        dhg = rr * gd - hg * (rr * rr * rr) * jnp.mean(gd * hg, axis=-1, keepdims=True)
        gg_ref[...] += jnp.sum(dy * hg * rr, axis=0, keepdims=True)
        dgr = (dhg * hv * dge).astype(BF16)
        dh = dhg * ge
        b = _shift_up(a, jnp.broadcast_to(a_first[0:1, :], (8, REC_W)), 1)
        pb, gl = _scan_bwd(b, dh)
        gs = gl + pb * g_carry[0:1, :]
        g_carry[0:1, :] = gs[0:1, :]
        a_first[0:1, :] = a[0:1, :]
        da = gs * h_prev
        dmult = gs * (ig * xc)
        di = gs * (mult * xc)
        dxc = gs * (mult * ig)
        dlog_a = da * a - dmult * (a * a) / mult
        gsp[...] += jnp.sum(dlog_a * (-LRU_C * r), axis=0, keepdims=True)
        dzr = (dlog_a * (-LRU_C * sp)) * (r * (1.0 - r))
        dzi = di * (ig * (1.0 - ig))
        dzr_b, dzi_b = dzr.astype(BF16), dzi.astype(BF16)
        dxc = dxc + _dot(dzr_b, wrg_ref[...], _NT) + _dot(dzi_b, wig_ref[...], _NT)
        gwrg_ref[...] += _dot(xb, dzr_b, _TN)
        gwig_ref[...] += _dot(xb, dzi_b, _TN)
        gbrg_ref[...] += jnp.sum(dzr, axis=0, keepdims=True)
        gbig_ref[...] += jnp.sum(dzi, axis=0, keepdims=True)
        nxt = dxc_next[...]
        dxr = cw_ref[3:4, :] * dxc
        gcw_ref[3:4, :] += jnp.sum(dxc * xr, axis=0, keepdims=True)
        for s in range(1, REC_CONV):
            dxr = dxr + cw_ref[3 - s:4 - s, :] * _shift_up(dxc, nxt, s)
            gcw_ref[3 - s:4 - s, :] += jnp.sum(dxc * _shift_down(xr, xhalo, s), axis=0, keepdims=True)
        gcb_ref[...] += jnp.sum(dxc, axis=0, keepdims=True)
        dxc_next[...] = dxc[:8]
        drec_ref[...] = jnp.concatenate([dxr.astype(BF16), dgr], axis=1)

        @pl.when(first_tile)
        def _():
            glam_ref[...] = gsp[...] * (-_sigmoid(-lam_ref[...]))

    rev = lambda i: nt - 1 - i
    vec = pl.BlockSpec((1, REC_W), lambda i: (0, 0))
    row = pl.BlockSpec((tm, REC_W), lambda i: (rev(i), 0))
    mat = pl.BlockSpec((REC_W, REC_W), lambda i: (0, 0))
    cwb = pl.BlockSpec((8, REC_W), lambda i: (0, 0))
    halo = lambda c: pl.BlockSpec((8, REC_W), lambda i, c=c: (jnp.maximum(rev(i) * hb - 1, 0), c))
    return _call(
        body, name=name, grid=(nt,),
        in_specs=[pl.BlockSpec((tm, REC_W), lambda i: (rev(i), 1)),
                  pl.BlockSpec((tm, REC_W), lambda i: (rev(i), 3)), halo(3),
                  pl.BlockSpec((tm, REC_W), lambda i: (rev(i), 4)),
                  row, row, halo(0), cwb, vec, mat, mat, vec, vec, vec, vec],
        out_specs=[pl.BlockSpec((tm, 2 * REC_W), lambda i: (rev(i), 0)), cwb, vec, mat, mat, vec, vec, vec, vec],
        out_shape=[jax.ShapeDtypeStruct((T, 2 * REC_W), BF16)]
        + [jax.ShapeDtypeStruct((8, REC_W), F32), jax.ShapeDtypeStruct((1, REC_W), F32)]
        + [jax.ShapeDtypeStruct((REC_W, REC_W), F32)] * 2 + [jax.ShapeDtypeStruct((1, REC_W), F32)] * 4,
        scratch_shapes=[pltpu.VMEM((8, REC_W), F32)] * 3 + [pltpu.VMEM((1, REC_W), F32)],
        compiler_params=_params("arbitrary"),
    )(dmix, proj, proj, proj, xc, h, h, cw, cb, wrg, wig, brg, big, lam, g)


def _ffn_conv(x_ext, cw_ref, cb_ref):
    return (cb_ref[...] + cw_ref[2:3, :] * x_ext + cw_ref[1:2, :] * pltpu.roll(x_ext, 1, 0)
            + cw_ref[0:1, :] * pltpu.roll(x_ext, 2, 0))


def _up_proj_act(x2, g, w_upT, cw, cb, name, tm=1024, tc=768):
    T = x2.shape[0]
    nc = D_FF // tc

    def body(x_ref, g_ref, wg_ref, wu_ref, cwg_ref, cwu_ref, cbg_ref, cbu_ref, act_ref, da_ref, db_ref, pg_ref, pu_ref,
             h_ref, hist_g, hist_u, hs):
        i, j = pl.program_id(0), pl.program_id(1)

        @pl.when(j == 0)
        def _():
            xv = x_ref[...]
            r = lax.rsqrt(jnp.mean(xv * xv, axis=-1, keepdims=True) + EPS)
            hs[...] = (xv * r * g_ref[...]).astype(BF16)
            h_ref[...] = hs[...]

        hv = hs[...]
        pg, pu = _dot(hv, wg_ref[...], _NT), _dot(hv, wu_ref[...], _NT)
        ge = jnp.concatenate([jnp.where(i > 0, hist_g[j], 0.0), pg], axis=0)
        ue = jnp.concatenate([jnp.where(i > 0, hist_u[j], 0.0), pu], axis=0)
        gel, dgel = _gelu_and_grad(_ffn_conv(ge, cwg_ref, cbg_ref)[8:])
        uu = _ffn_conv(ue, cwu_ref, cbu_ref)[8:]
        act_ref[...] = (gel * uu).astype(BF16)
        da_ref[...] = (uu * dgel).astype(BF16)
        db_ref[...] = gel.astype(BF16)
        pg_ref[...] = pg.astype(BF16)
        pu_ref[...] = pu.astype(BF16)
        hist_g[j] = pg[tm - 8:]
        hist_u[j] = pu[tm - 8:]

    tile = pl.BlockSpec((tm, tc), lambda i, j: (i, j))
    wsp = lambda off: pl.BlockSpec((tc, D_MODEL), lambda i, j, off=off: (j + off, 0))
    cws = lambda off: pl.BlockSpec((8, tc), lambda i, j, off=off: (0, j + off))
    cbs = lambda off: pl.BlockSpec((1, tc), lambda i, j, off=off: (0, j + off))
    return _call(
        body, name=name, grid=(T // tm, nc),
        in_specs=[pl.BlockSpec((tm, D_MODEL), lambda i, j: (i, 0)), pl.BlockSpec((1, D_MODEL), lambda i, j: (0, 0)),
                  wsp(0), wsp(nc), cws(0), cws(nc), cbs(0), cbs(nc)],
        out_specs=[tile] * 5 + [pl.BlockSpec((tm, D_MODEL), lambda i, j: (i, 0))],
        out_shape=[jax.ShapeDtypeStruct((T, D_FF), BF16)] * 5 + [jax.ShapeDtypeStruct((T, D_MODEL), BF16)],
        scratch_shapes=[pltpu.VMEM((nc, 8, tc), F32)] * 2 + [pltpu.VMEM((tm, D_MODEL), BF16)],
        compiler_params=_params("arbitrary", "arbitrary"),
    )(x2, g, w_upT, w_upT, cw, cw, cb, cb)


def _ffn_bwd(dyb, w_down, da, db, pg, pu, cw, name, tm=1024, tc=768):
    T, F = pg.shape
    nt = T // tm
    hb16 = tm // 16
    nc = F // tc
    n = tm + 8

    def body(dy_ref, dyn_ref, wd_ref, a_ref, an_ref, b_ref, bn_ref, g_ref, u_ref, cwg_ref, cwu_ref,
             dg_ref, du_ref, gcwg_ref, gcwu_ref, gcbg_ref, gcbu_ref):
        i = pl.program_id(1)
        last = i == nt - 1

        @pl.when(i == 0)
        def _():
            for ref in (gcwg_ref, gcwu_ref, gcbg_ref, gcbu_ref):
                ref[...] = jnp.zeros_like(ref)

        wd = wd_ref[...]
        dact_next = jnp.where(last, 0.0, _dot(dyn_ref[...], wd, _NT)[:8])
        de = jnp.concatenate([_dot(dy_ref[...], wd, _NT), dact_next], axis=0)
        ext = lambda t, nx: jnp.concatenate([t[...].astype(F32), nx[...].astype(F32)[:8]], axis=0)
        for dcv, x_ref, cw_ref, dx_ref, gcw_ref, gcb_ref in ((de * ext(a_ref, an_ref), g_ref, cwg_ref, dg_ref, gcwg_ref, gcbg_ref),
                                                               (de * ext(b_ref, bn_ref), u_ref, cwu_ref, du_ref, gcwu_ref, gcbu_ref)):
            s1, s2 = pltpu.roll(dcv, n - 1, 0), pltpu.roll(dcv, n - 2, 0)
            dx_ref[...] = (cw_ref[2:3, :] * dcv + cw_ref[1:2, :] * s1 + cw_ref[0:1, :] * s2)[:tm].astype(BF16)
            xv = x_ref[...].astype(F32)
            gcw_ref[2:3, :] += jnp.sum(xv * dcv[:tm], axis=0, keepdims=True)
            gcw_ref[1:2, :] += jnp.sum(xv * s1[:tm], axis=0, keepdims=True)
            gcw_ref[0:1, :] += jnp.sum(xv * s2[:tm], axis=0, keepdims=True)
            gcb_ref[...] += jnp.sum(dcv[:tm], axis=0, keepdims=True)

    tile = pl.BlockSpec((tm, tc), lambda j, i: (i, j))
    nxt = pl.BlockSpec((16, tc), lambda j, i: (jnp.minimum((i + 1) * hb16, nt * hb16 - 1), j))
    cws = lambda off: pl.BlockSpec((8, tc), lambda j, i, off=off: (0, j + off))
    cbs = pl.BlockSpec((1, tc), lambda j, i: (0, j))
    return _call(
        body, name=name, grid=(nc, nt),
        in_specs=[pl.BlockSpec((tm, D_MODEL), lambda j, i: (i, 0)),
                  pl.BlockSpec((16, D_MODEL), lambda j, i: (jnp.minimum((i + 1) * hb16, nt * hb16 - 1), 0)),
                  pl.BlockSpec((tc, D_MODEL), lambda j, i: (j, 0)), tile, nxt, tile, nxt, tile, tile, cws(0), cws(nc)],
        out_specs=[tile, tile, cws(0), cws(0), cbs, cbs],
        out_shape=[jax.ShapeDtypeStruct((T, F), BF16)] * 2 + [jax.ShapeDtypeStruct((8, F), F32)] * 2
        + [jax.ShapeDtypeStruct((1, F), F32)] * 2,
        compiler_params=_params("parallel", "arbitrary"),
    )(dyb, dyb, w_down, da, da, db, db, pg, pu, cw, cw)


def _adam_update(w, g, m, v):
    m2 = ADAM_B1 * m + (1.0 - ADAM_B1) * g
    v2 = ADAM_B2 * v + (1.0 - ADAM_B2) * (g * g)
    m_hat = m2 / (1.0 - ADAM_B1 ** ADAM_STEP)
    v_hat = v2 / (1.0 - ADAM_B2 ** ADAM_STEP)
    delta = -ADAM_LR * (m_hat / (jnp.sqrt(v_hat) + ADAM_EPS) + ADAM_WD * w)
    return delta, m2, v2


def _adam_sharded(p, r2, idx, w, m, v, name, transposed=False):
    r, n = p.shape[1:]
    nrecv = r2.shape[0]
    tr = (256 if r % 256 == 0 else r) if transposed else _row_tile(r)

    def body(c_ref, p_ref, r_ref, w_ref, m_ref, v_ref, g_ref, d_ref, m2_ref, v2_ref):
        g = p_ref[...].astype(F32)
        for k in range(nrecv):
            g = g + r_ref[k].astype(F32)
        if transposed:
            g = g.T
        g_ref[...] = g
        d_ref[...], m2_ref[...], v2_ref[...] = _adam_update(w_ref[...], g, m_ref[...], v_ref[...])

    blk = pl.BlockSpec((n, tr), lambda i, c_ref: (0, i)) if transposed else pl.BlockSpec((tr, n), lambda i, c_ref: (i, 0))
    spec = pltpu.PrefetchScalarGridSpec(
        num_scalar_prefetch=1, grid=(r // tr,),
        in_specs=[pl.BlockSpec((None, tr, n), lambda i, c_ref: (c_ref[0], i, 0)),
                  pl.BlockSpec((nrecv, tr, n), lambda i, c_ref: (0, i, 0)), blk, blk, blk],
        out_specs=[blk] * 4)
    return _call(body, name=name, grid_spec=spec, out_shape=[jax.ShapeDtypeStruct(w.shape, F32)] * 4,
                 compiler_params=_params("parallel"))(idx, p, r2, w, m, v)


def _sum_slabs(p, r2, idx, name):
    _, r, n = p.shape

    def body(c_ref, p_ref, r_ref, o_ref):
        acc = p_ref[...]
        for k in range(N_PEERS):
            acc = acc + r_ref[k]
        o_ref[...] = acc

    spec = pltpu.PrefetchScalarGridSpec(
        num_scalar_prefetch=1, grid=(1,),
        in_specs=[pl.BlockSpec((None, r, n), lambda i, c_ref: (c_ref[0], 0, 0)),
                  pl.BlockSpec((N_PEERS, r, n), lambda i, c_ref: (0, 0, 0))],
        out_specs=pl.BlockSpec((r, n), lambda i, c_ref: (0, 0)))
    return _call(body, name=name, grid_spec=spec, out_shape=jax.ShapeDtypeStruct((r, n), F32))(idx, p, r2)


def _adam_small(ws, gs, ms, vs, name):
    n = len(ws)

    def body(*refs):
        for i in range(n):
            d, m2, v2 = _adam_update(refs[i][...], refs[n + i][...], refs[2 * n + i][...], refs[3 * n + i][...])
            refs[4 * n + i][...] = d
            refs[5 * n + i][...] = m2
            refs[6 * n + i][...] = v2

    outs = _call(body, name=name, out_shape=[jax.ShapeDtypeStruct(w.shape, F32) for w in ws] * 3)(*ws, *gs, *ms, *vs)
    return outs[:n], outs[n:2 * n], outs[2 * n:]


def _pack_small_grads(full, halves, rcw, fcwg, fcwu, wrg, wig, lparts, name):
    nf, nh = len(full), len(halves)

    def body(*refs):
        o = refs[-1]
        o[...] = jnp.zeros_like(o)
        row = 0
        for r in refs[:nf]:
            for j in range(r.shape[1] // 1024):
                o[row:row + 1, :] = r[:, 1024 * j:1024 * (j + 1)]
                row += 1
        for k in range(0, nh, 2):
            o[row:row + 1, 0:512] = refs[nf + k][...]
            o[row:row + 1, 512:1024] = refs[nf + k + 1][...]
            row += 1
        rcw_ref, fg_ref, fu_ref, wrg_ref, wig_ref, l_ref = refs[nf + nh:nf + nh + 6]
        for k in range(2):
            o[row:row + 1, 0:512] = rcw_ref[2 * k:2 * k + 1, :]
            o[row:row + 1, 512:1024] = rcw_ref[2 * k + 1:2 * k + 2, :]
            row += 1
        for f_ref in (fg_ref, fu_ref):
            for k in range(FFN_CONV):
                for j in range(D_FF // 1024):
                    o[row:row + 1, :] = f_ref[k:k + 1, 1024 * j:1024 * (j + 1)]
                    row += 1
        assert row == 32
        for n in range(8):
            o[32:96, 64 * n:64 * n + 64] = wrg_ref[64 * n:64 * n + 64, 64 * n:64 * n + 64]
            o[32:96, 512 + 64 * n:512 + 64 * n + 64] = wig_ref[64 * n:64 * n + 64, 64 * n:64 * n + 64]
        o[96:97, :] = jnp.sum(l_ref[...], axis=0, keepdims=True)

    return _call(body, name=name, out_shape=jax.ShapeDtypeStruct((SMALL_ROWS, 1024), F32))(
        *full, *halves, rcw, fcwg, fcwu, wrg, wig, lparts)


def _block_diag(w):
    eye = jnp.eye(8, dtype=w.dtype)
    return (w[:, :, None, :] * eye[:, None, :, None]).reshape(512, 512)


def kernel(x, positions, g_mix, w_in, q_norm_g, k_norm_g, rec_conv_w, rec_conv_b, w_rg, b_rg, w_ig, b_ig, lru_lambda, g_attn_out, g_rec_out, w_out, g_ffn, w_up, ffn_conv_w, ffn_conv_b, w_down, loss_target, m_g_mix, m_w_in, m_q_norm_g, m_k_norm_g, m_rec_conv_w, m_rec_conv_b, m_w_rg, m_b_rg, m_w_ig, m_b_ig, m_lru_lambda, m_g_attn_out, m_g_rec_out, m_w_out, m_g_ffn, m_w_up, m_ffn_conv_w, m_ffn_conv_b, m_w_down, v_g_mix, v_w_in, v_q_norm_g, v_k_norm_g, v_rec_conv_w, v_rec_conv_b, v_w_rg, v_b_rg, v_w_ig, v_b_ig, v_lru_lambda, v_g_attn_out, v_g_rec_out, v_w_out, v_g_ffn, v_w_up, v_ffn_conv_w, v_ffn_conv_b, v_w_down):
    T = x.shape[1]
    ix, iy, ic = lax.axis_index("x"), lax.axis_index("y"), lax.axis_index("c")
    dev = 4 * ix + 2 * iy + ic
    xs = x.reshape(T, D_MODEL)
    tgt = loss_target.reshape(T, D_MODEL)
    pos = positions.reshape(T, 1)

    shards = {"w_in": (w_in[0], m_w_in[0], v_w_in[0]), "w_out": (w_out[0], m_w_out[0], v_w_out[0]),
              "w_up": (w_up[0], m_w_up[0], v_w_up[0]), "w_down": (w_down[0], m_w_down[0], v_w_down[0])}
    taps = jnp.concatenate([rec_conv_w.reshape(-1), ffn_conv_w.reshape(-1), jnp.zeros((4096 - 2560,), F32)]).reshape(8, 512)
    W_inT, taps_all = _all_gather([w_in[0].T.astype(BF16), taps], "ag_w_in")
    gather_landing = lambda s: _landing((N_DEV * s.shape[0], 1024), BF16, s, dev * s.shape[0])
    late = [w_out[0].astype(BF16), w_up[0].T.astype(BF16)]
    ag_send, ag_recv, late_thru, land_thru, ag_token = _exchange_start(
        late, [gather_landing(s) for s in late], "gather", taps_all, "ag_late_start")
    w_down_b = w_down[0].astype(BF16)
    down_landing = gather_landing(w_down_b)
    taps_all = taps_all.reshape(N_DEV, 4096)
    rcw = taps_all[:, :256].reshape(8, 4, 64).transpose(1, 0, 2).reshape(4, REC_W)
    fcw = taps_all[:, 256:2560].reshape(8, 3, 768).transpose(1, 0, 2).reshape(3, 2 * D_FF)
    rcw8 = jnp.pad(rcw, ((0, 4), (0, 0)))
    fcw8 = jnp.pad(fcw, ((0, 5), (0, 0)))
    fcb = ffn_conv_b.reshape(1, 2 * D_FF)

    half = HEAD_DIM // 2
    inv_freq = ROPE_THETA ** (-jnp.arange(half, dtype=F32) / half)
    invf = jnp.tile(inv_freq, 2 * N_HEADS).reshape(1, ATTN_W)
    bd = jnp.asarray(np.kron(np.eye(2), np.full((HEAD_DIM, HEAD_DIM), 1.0 / HEAD_DIM)), BF16)
    qg = jnp.tile(q_norm_g.reshape(HEAD_DIM), N_HEADS).reshape(1, ATTN_W)
    kg = jnp.tile(k_norm_g.reshape(HEAD_DIM), N_HEADS).reshape(1, ATTN_W)
    wrg_bd = _block_diag(w_rg[0]).astype(BF16)
    wig_bd = _block_diag(w_ig[0]).astype(BF16)
    brg, big = b_rg.reshape(1, REC_W), b_ig.reshape(1, REC_W)

    proj, h1 = _norm_proj(xs, g_mix + ag_token[0, 0], W_inT, "in_proj", tn=IN_W)
    qf, kf = _qk_prep(proj, pos, invf, qg, kg, bd, "qk_prep")
    attn, lse = _attn_fwd(qf, kf, proj, "attn_fwd")
    dn_send, dn_recv, dn_thru, dn_land, dn_token = _exchange_start(
        [w_down_b], [down_landing], "gather", attn, "ag_down_start")
    mix = _attn_norm(attn, g_attn_out + dn_token[0, 0], "attn_norm")
    xc, hstate, mix = _rec_fwd(proj, mix, rcw8, rec_conv_b, wrg_bd, wig_bd, brg, big, lru_lambda, g_rec_out, "rec_fwd")
    _, (W_out, W_upT) = _exchange_wait(ag_send, ag_recv, late_thru, land_thru, "gather", hstate, "ag_late_wait")
    x2 = _mm(mix, W_out, "nn", F32, "out_proj", add=xs)

    act, da, db, pg, pu, h2 = _up_proj_act(x2, g_ffn, W_upT, fcw8, fcb, "up_proj_act")
    _, (W_down,) = _exchange_wait(dn_send, dn_recv, dn_thru, dn_land, "gather", h2, "ag_down_wait")
    dy, dyb, lparts = _mm(act, W_down, "nn", F32, "down_proj_loss", add=x2, loss_target=tgt, tm=512, tk=D_FF)

    g_down = _mm(act, dyb, "tn", BF16, "g_w_down", tk=4096)
    dpg, dpu, g_fcwg, g_fcwu, g_fcbg, g_fcbu = _ffn_bwd(dyb, W_down, da, db, pg, pu, fcw8, "ffn_bwd")
    g_upT = _mm(dpg, h2, "tn", BF16, "g_w_up_gate", tk=4096, o_rows=2 * D_FF)
    g_upT = _mm(dpu, h2, "tn", BF16, "g_w_up_up", tk=4096, into=g_upT, o_moff=D_FF // 1024)
    ffn_g = [g_upT.reshape(N_DEV, 2 * D_FF // N_DEV, 1024), g_down.reshape(N_DEV, D_FF // N_DEV, 1024)]
    rs_send, rs_recv, ffn_g, ffn_land, rs_token = _exchange_start(
        ffn_g, [_landing((N_PEERS,) + g.shape[1:], BF16) for g in ffn_g], "scatter", dpu, "rs_ffn_start")
    dx2, dx2b, g_gffn = _mm_norm_bwd([dpg, dpu], W_upT, x2, dy, g_ffn + rs_token[0, 0], "d_h2_norm_bwd", tm=1024, tk=1536)

    dmix = _mm(dx2b, W_out, "nt", F32, "d_mix")
    g_out = _mm(mix, dx2b, "tn", BF16, "g_w_out", tk=4096).reshape(N_DEV, D_MODEL // N_DEV, 1024)
    out_send, out_recv, (g_out,), out_land, out_token = _exchange_start(
        [g_out], [_landing((N_PEERS,) + g_out.shape[1:], BF16)], "scatter", dmix, "rs_out_start")
    do, delta, g_gattn = _attn_norm_bwd(dmix, attn, g_attn_out + out_token[0, 0], bd, "attn_norm_bwd")
    dqh, dkh, dv = _attn_bwd(qf, kf, proj, do, lse, delta, "attn_bwd")
    dqkv, g_qg, g_kg = _qk_prep_bwd(proj, dqh, dkh, dv, pos, invf, qg, kg, bd, "qk_prep_bwd")
    (drec, g_rcw, g_rcb, g_wrg, g_wig, g_brg, g_big, g_lam, g_grec) = _rec_bwd(
        dmix, proj, xc, hstate, rcw8, rec_conv_b, wrg_bd, wig_bd, brg, big, lru_lambda, g_rec_out, "rec_bwd")
    g_inT = _mm(dqkv, h1, "tn", BF16, "g_w_in_qkv", tm=512, tk=4096, o_rows=IN_W)
    g_inT = _mm(drec, h1, "tn", BF16, "g_w_in_rec", tm=512, tk=4096, into=g_inT, o_moff=3 * ATTN_W // 512)
    g_inT = g_inT.reshape(N_DEV, IN_W // N_DEV, 1024)
    in_send, in_recv, (g_inT,), in_land, in_token = _exchange_start(
        [g_inT], [_landing((N_PEERS,) + g_inT.shape[1:], BF16)], "scatter", drec, "rs_in_start")
    grad_x, _, g_gmix = _mm_norm_bwd([dqkv, drec], W_inT, xs, dx2, g_mix + in_token[0, 0], "d_h1_norm_bwd", tm=1024, tk=512)

    flat = _pack_small_grads([g_gmix, g_gffn, g_fcbg, g_fcbu], [g_rcb, g_brg, g_big, g_lam, g_gattn, g_grec, g_qg, g_kg],
                             g_rcw, g_fcwg, g_fcwu, g_wrg, g_wig, lparts.reshape(-1, D_MODEL), "pack_small_grads")
    srows = SMALL_ROWS // N_DEV
    flat = flat.reshape(N_DEV, srows, 1024)
    sm_send, sm_recv, (flat,), sm_land, sm_token = _exchange_start(
        [flat], [_landing((N_PEERS, srows, 1024), F32)], "scatter", grad_x, "ar_small_rs_start")

    devi = jnp.reshape(dev, (1,)).astype(jnp.int32)
    ffn_g, ffn_land = _exchange_wait(rs_send, rs_recv, ffn_g, ffn_land, "scatter", sm_token, "rs_ffn_wait")
    (g_out,), out_land = _exchange_wait(out_send, out_recv, [g_out], out_land, "scatter", sm_token, "rs_out_wait")
    big_out = {"grad": {}, "delta": {}, "new_m": {}, "new_v": {}}

    def adam_big(nm, p, r):
        w_, m_, v_ = shards[nm]
        res = _adam_sharded(p, r, devi, w_, m_, v_, "adam_" + nm, transposed=nm in ("w_in", "w_up"))
        for kind, a in zip(("grad", "delta", "new_m", "new_v"), res):
            big_out[kind][nm] = a[None]
        return res[0]

    last = adam_big("w_up", ffn_g[0], ffn_land[0])
    (flat,), sm_land = _exchange_wait(sm_send, sm_recv, [flat], sm_land, "scatter", last, "ar_small_rs_wait")
    mine = _sum_slabs(flat, sm_land[0], devi, "sum_small_grads")
    sm_send, sm_recv, (mine,), sm_land, sm_token = _exchange_start(
        [mine], [_landing((SMALL_ROWS, 1024), F32, mine, dev * srows)], "gather", last, "ar_small_ag_start")
    adam_big("w_down", ffn_g[1], ffn_land[1])
    last = adam_big("w_out", g_out, out_land[0])
    _, (tot,) = _exchange_wait(sm_send, sm_recv, [mine], sm_land, "gather", last, "ar_small_ag_wait")
    (g_inT,), in_land = _exchange_wait(in_send, in_recv, [g_inT], in_land, "scatter", tot, "rs_in_wait")
    adam_big("w_in", g_inT, in_land[0])

    half = lambda r, h, shape: tot[r, 512 * h:512 * h + 512].reshape(shape)
    blocks = lambda h: tot[32:96, 512 * h:512 * h + 512].reshape(64, 8, 64).transpose(1, 0, 2)[None]
    fcw_full = jnp.concatenate([tot[14:23].reshape(1, 3, D_FF), tot[23:32].reshape(1, 3, D_FF)], axis=2)
    g_small = {
        "g_mix": tot[0:1], "g_ffn": tot[1:2], "ffn_conv_b": tot[2:8].reshape(1, 2 * D_FF),
        "rec_conv_b": half(8, 0, (1, 512)), "b_rg": half(8, 1, (1, 8, 64)), "b_ig": half(9, 0, (1, 8, 64)),
        "lru_lambda": half(9, 1, (1, 512)), "g_attn_out": half(10, 0, (1, 512)), "g_rec_out": half(10, 1, (1, 512)),
        "q_norm_g": half(11, 0, (N_HEADS, HEAD_DIM)).sum(0)[None], "k_norm_g": half(11, 1, (N_HEADS, HEAD_DIM)).sum(0)[None],
        "w_rg": blocks(0), "w_ig": blocks(1),
        "rec_conv_w": lax.dynamic_slice(tot[12:14].reshape(1, 4, REC_W), (0, 0, 64 * dev), (1, 4, 64)),
        "ffn_conv_w": lax.dynamic_slice(fcw_full, (0, 0, 768 * dev), (1, 3, 768))}
    loss = 0.5 / D_MODEL * jnp.sum(tot[96])
    given = dict(rec_conv_w=rec_conv_w, ffn_conv_w=ffn_conv_w,g_mix=g_mix, q_norm_g=q_norm_g, k_norm_g=k_norm_g, rec_conv_b=rec_conv_b, w_rg=w_rg, b_rg=b_rg, w_ig=w_ig,
                 b_ig=b_ig, lru_lambda=lru_lambda, g_attn_out=g_attn_out, g_rec_out=g_rec_out, g_ffn=g_ffn, ffn_conv_b=ffn_conv_b)
    given_m = dict(rec_conv_w=m_rec_conv_w, ffn_conv_w=m_ffn_conv_w, g_mix=m_g_mix, q_norm_g=m_q_norm_g, k_norm_g=m_k_norm_g, rec_conv_b=m_rec_conv_b, w_rg=m_w_rg, b_rg=m_b_rg,
                   w_ig=m_w_ig, b_ig=m_b_ig, lru_lambda=m_lru_lambda, g_attn_out=m_g_attn_out, g_rec_out=m_g_rec_out,
                   g_ffn=m_g_ffn, ffn_conv_b=m_ffn_conv_b)
    given_v = dict(rec_conv_w=v_rec_conv_w, ffn_conv_w=v_ffn_conv_w, g_mix=v_g_mix, q_norm_g=v_q_norm_g, k_norm_g=v_k_norm_g, rec_conv_b=v_rec_conv_b, w_rg=v_w_rg, b_rg=v_b_rg,
                   w_ig=v_w_ig, b_ig=v_b_ig, lru_lambda=v_lru_lambda, g_attn_out=v_g_attn_out, g_rec_out=v_g_rec_out,
                   g_ffn=v_g_ffn, ffn_conv_b=v_ffn_conv_b)
    small = sorted(given)
    ds, m2s, v2s = _adam_small([given[k] for k in small], [g_small[k] for k in small], [given_m[k] for k in small],
                               [given_v[k] for k in small], "adam_small")
    small_out = {"grad": g_small, "delta": dict(zip(small, ds)), "new_m": dict(zip(small, m2s)), "new_v": dict(zip(small, v2s))}

    order = ("g_mix", "w_in", "q_norm_g", "k_norm_g", "rec_conv_w", "rec_conv_b", "w_rg", "b_rg", "w_ig", "b_ig",
             "lru_lambda", "g_attn_out", "g_rec_out", "w_out", "g_ffn", "w_up", "ffn_conv_w", "ffn_conv_b", "w_down")
    outs = [loss, grad_x.reshape(1, T, D_MODEL)]
    for kind in ("grad", "delta", "new_m", "new_v"):
        for name in order:
            outs.append(big_out[kind][name] if name in big_out[kind] else small_out[kind][name])
    return tuple(outs)
```

```python
import math

import numpy as np
import jax
import jax.numpy as jnp
from jax import lax
from jax.experimental import pallas as pl
from jax.experimental.pallas import tpu as pltpu

F32 = jnp.float32
BF16 = jnp.bfloat16

D_MODEL = 1024
HEAD_DIM = 64
ATTN_W = 512
REC_W = 512
N_HEADS = 8
D_FF = 3072
IN_W = 2560
REC_CONV = 4
FFN_CONV = 3
LRU_C = 8.0
ROPE_THETA = 10000.0
EPS = 1e-6
NEG_INF = -1e30
QBLK = 128
DILATIONS = (1, 4, 16)
N_DEV = 8
SMALL_ROWS = 128
ADAM_LR, ADAM_B1, ADAM_B2, ADAM_EPS, ADAM_WD, ADAM_STEP = 0.001, 0.9, 0.999, 1e-08, 0.01, 10
MESH = pl.DeviceIdType.MESH
ANY = pl.BlockSpec(memory_space=pl.ANY)


def _call(body, *, name, **kw):
    return pl.pallas_call(body, name=name, **kw)


def _params(*sem):
    return pltpu.CompilerParams(dimension_semantics=sem, vmem_limit_bytes=56 * 1024 * 1024)


_GELU_C = math.sqrt(2.0 / math.pi)
_GELU_A = 0.044715


def _gelu(x):
    return (0.5 * x) * (1.0 + jnp.tanh(x * (_GELU_C + (_GELU_C * _GELU_A) * (x * x))))


def _gelu_and_grad(x):
    x2 = x * x
    u = 1.0 + jnp.tanh(x * (_GELU_C + (_GELU_C * _GELU_A) * x2))
    hx = 0.5 * x
    return hx * u, 0.5 * u + (hx * ((2.0 - u) * u)) * (_GELU_C + (3.0 * _GELU_C * _GELU_A) * x2)


def _sigmoid(x):
    return 1.0 / (1.0 + jnp.exp(-x))


def _softplus_neg(lam):
    y = jnp.exp(-jnp.abs(lam))
    u = 1.0 + y
    log1p = jnp.where(u == 1.0, y, jnp.log(u) * y / jnp.where(u == 1.0, 1.0, u - 1.0))
    return jnp.maximum(-lam, 0.0) + log1p


_NN = (((1,), (0,)), ((), ()))
_NT = (((1,), (1,)), ((), ()))
_TN = (((0,), (0,)), ((), ()))


def _dot(a, b, dims=_NN):
    return lax.dot_general(a, b, dims, preferred_element_type=F32)


def _group_mean(v, bd):
    hi = v.astype(BF16)
    lo = (v - hi.astype(F32)).astype(BF16)
    w = bd.shape[0]
    return jnp.concatenate([_dot(hi[:, c:c + w], bd) + _dot(lo[:, c:c + w], bd) for c in range(0, v.shape[1], w)], axis=1)


def _rope_tables(pos_ref, invf_ref):
    ang = pos_ref[...].astype(F32) * invf_ref[:, :2 * HEAD_DIM]
    reps = invf_ref.shape[1] // (2 * HEAD_DIM)
    return jnp.tile(jnp.cos(ang), (1, reps)), jnp.tile(jnp.sin(ang), (1, reps))


def _shift_down(x, halo, s):
    rolled = pltpu.roll(x, s, 0)
    hr = pltpu.roll(halo, s, 0)
    row = lax.broadcasted_iota(jnp.int32, hr.shape, 0)
    first = jnp.where(row < s, hr, rolled[:8])
    return jnp.concatenate([first, rolled[8:]], axis=0)


def _shift_up(x, halo, s):
    n = x.shape[0]
    rolled = pltpu.roll(x, n - s, 0)
    hr = pltpu.roll(halo, 8 - s, 0)
    row = lax.broadcasted_iota(jnp.int32, hr.shape, 0)
    last = jnp.where(row >= 8 - s, hr, rolled[n - 8:])
    return jnp.concatenate([rolled[:n - 8], last], axis=0)


def _scan_fwd(a, u):
    n, w = a.shape
    a3, u3 = a.reshape(n // 8, 8, w), u.reshape(n // 8, 8, w)
    row = lax.broadcasted_iota(jnp.int32, a3.shape, 1)
    for s in (1, 2, 4):
        a_s = jnp.where(row < s, 1.0, pltpu.roll(a3, s, 1))
        u_s = jnp.where(row < s, 0.0, pltpu.roll(u3, s, 1))
        u3 = u3 + a3 * u_s
        a3 = a3 * a_s
    ps, hs = [a3[0]], [u3[0]]
    for k in range(1, n // 8):
        ps.append(a3[k] * ps[-1][7:8, :])
        hs.append(u3[k] + a3[k] * hs[-1][7:8, :])
    return jnp.concatenate(ps, axis=0), jnp.concatenate(hs, axis=0)


def _scan_bwd(b, v):
    n, w = b.shape
    b3, v3 = b.reshape(n // 8, 8, w), v.reshape(n // 8, 8, w)
    row = lax.broadcasted_iota(jnp.int32, b3.shape, 1)
    for s in (1, 2, 4):
        b_s = jnp.where(row >= 8 - s, 1.0, pltpu.roll(b3, 8 - s, 1))
        v_s = jnp.where(row >= 8 - s, 0.0, pltpu.roll(v3, 8 - s, 1))
        v3 = v3 + b3 * v_s
        b3 = b3 * b_s
    last = n // 8 - 1
    ps, gs = [b3[last]], [v3[last]]
    for k in range(last - 1, -1, -1):
        ps.append(b3[k] * ps[-1][0:1, :])
        gs.append(v3[k] + b3[k] * gs[-1][0:1, :])
    return jnp.concatenate(ps[::-1], axis=0), jnp.concatenate(gs[::-1], axis=0)


def _rot_half(y):
    n = y.shape[1]
    lane = lax.broadcasted_iota(jnp.int32, y.shape, 1) & (HEAD_DIM - 1)
    return jnp.where(lane < HEAD_DIM // 2, -pltpu.roll(y, n - HEAD_DIM // 2, 1), pltpu.roll(y, HEAD_DIM // 2, 1))


def _row_tile(r, cap=256):
    return max(t for t in range(16, cap + 1, 16) if r % t == 0)


def _all_gather(shards, name):
    na = len(shards)
    ms = [s.shape[0] for s in shards]

    def body(*refs):
        x_refs, out_refs = refs[:na], refs[na:2 * na]
        send_sems, recv_sems, local_sems = refs[2 * na:]
        x, y, c = lax.axis_index("x"), lax.axis_index("y"), lax.axis_index("c")
        me, sibling = (x, y, c), (x, y, 1 - c)
        chips = [(1 - x, y), (x, 1 - y), (1 - x, 1 - y)]

        def rows(a, px, py, pc):
            return out_refs[a].at[pl.ds(pl.multiple_of((4 * px + 2 * py + pc) * ms[a], 8), ms[a]), :]

        def copy(a, k, block, to, src=None):
            return pltpu.make_async_remote_copy(
                src_ref=rows(a, *block) if src is None else src, dst_ref=rows(a, *block),
                send_sem=send_sems.at[7 * a + k], recv_sem=recv_sems.at[7 * a + k], device_id=to, device_id_type=MESH)

        mine = [pltpu.make_async_copy(x_refs[a], rows(a, *me), local_sems.at[a]) for a in range(na)]
        first = []
        for a in range(na):
            mine[a].start()
            first.append(copy(a, 0, me, sibling, src=x_refs[a]))
            first += [copy(a, 1 + j, me, (*chip, c), src=x_refs[a]) for j, chip in enumerate(chips)]
        for cp in first:
            cp.start()
        passed = []
        for a in range(na):
            for j, chip in enumerate(chips):
                copy(a, 1 + j, (*chip, c), me).wait_recv()
                fw = copy(a, 4 + j, (*chip, c), sibling)
                fw.start()
                passed.append(fw)
        for a in range(na):
            copy(a, 0, sibling, me).wait_recv()
            for j, chip in enumerate(chips):
                copy(a, 4 + j, (*chip, 1 - c), me).wait_recv()
        for cp in first + passed:
            cp.wait_send()
        for cp in mine:
            cp.wait()

    return _call(
        body, name=name, out_shape=[jax.ShapeDtypeStruct((N_DEV * s.shape[0], s.shape[1]), s.dtype) for s in shards],
        in_specs=[ANY] * na, out_specs=[ANY] * na,
        scratch_shapes=[pltpu.SemaphoreType.DMA((7 * na,)), pltpu.SemaphoreType.DMA((7 * na,)),
                        pltpu.SemaphoreType.DMA((na,))],
    )(*shards)


HBM = pl.BlockSpec(memory_space=pltpu.HBM)
SEM = pl.BlockSpec(memory_space=pltpu.SEMAPHORE)
EFFECT = pltpu.SideEffectType.DATAFLOW_SIDE_EFFECTING
N_PEERS = N_DEV - 1


def _peer(k):
    x, y, c = lax.axis_index("x"), lax.axis_index("y"), lax.axis_index("c")
    b = k + 1
    flip = lambda v, bit: 1 - v if bit else v
    return flip(x, b & 4), flip(y, b & 2), flip(c, b & 1)


def _in_hbm(a):
    return pltpu.with_memory_space_constraint(a, pltpu.HBM)


def _split_copy_descr(na, kind, src_refs, land_refs, send_sems, recv_sems):
    x, y, c = lax.axis_index("x"), lax.axis_index("y"), lax.axis_index("c")
    me = 4 * x + 2 * y + c
    copies = []
    for a in range(na):
        for k in range(N_PEERS):
            px, py, pc = _peer(k)
            if kind == "gather":
                m = src_refs[a].shape[0]
                src, dst = src_refs[a], land_refs[a].at[pl.ds(pl.multiple_of(me * m, 8), m), :]
            else:
                src, dst = src_refs[a].at[4 * px + 2 * py + pc], land_refs[a].at[k]
            copies.append(pltpu.make_async_remote_copy(
                src_ref=src, dst_ref=dst, send_sem=send_sems.at[N_PEERS * a + k], recv_sem=recv_sems.at[N_PEERS * a + k],
                device_id=(px, py, pc), device_id_type=MESH))
    return copies


def _landing(shape, dtype, own=None, at=None):
    buf = lax.empty(shape, dtype)
    return buf if own is None else lax.dynamic_update_slice(buf, own, (at, 0))


def _exchange_start(srcs, lands, kind, after, name):
    na = len(srcs)
    land_shapes = [l.shape for l in lands]

    def body(*refs):
        src_refs, land_refs = refs[:na], refs[na:2 * na]
        send_sems, recv_sems = refs[2 * na + 1], refs[2 * na + 2]
        token = refs[-1]
        for cp in _split_copy_descr(na, kind, src_refs, land_refs, send_sems, recv_sems):
            cp.start()
        token[...] = jnp.zeros_like(token)

    lands = [_in_hbm(l) for l in lands]
    sem = pltpu.SemaphoreType.DMA((N_PEERS * na,))
    outs = _call(
        body, name=name,
        out_shape=[sem, sem] + [pltpu.HBM(s.shape, s.dtype) for s in srcs] + [pltpu.HBM(s, srcs[0].dtype) for s in land_shapes]
        + [jax.ShapeDtypeStruct((8, 128), F32)],
        in_specs=[HBM] * (2 * na) + [ANY], out_specs=[SEM, SEM] + [HBM] * (2 * na) + [pl.BlockSpec(memory_space=pltpu.VMEM)],
        input_output_aliases={i: 2 + i for i in range(2 * na)},
        compiler_params=pltpu.CompilerParams(has_side_effects=EFFECT),
    )(*[_in_hbm(s) for s in srcs], *lands, after)
    return outs[0], outs[1], outs[2:2 + na], outs[2 + na:2 + 2 * na], outs[-1]


def _exchange_wait(send_sems, recv_sems, srcs, lands, kind, after, name):
    na = len(srcs)

    def body(*refs):
        src_refs, land_refs = refs[:na], refs[na:2 * na]
        s_sems, r_sems = refs[2 * na], refs[2 * na + 1]
        for cp in _split_copy_descr(na, kind, src_refs, land_refs, s_sems, r_sems):
            cp.wait_send()
            cp.wait_recv()

    outs = _call(
        body, name=name, out_shape=[pltpu.HBM(s.shape, s.dtype) for s in srcs] + [pltpu.HBM(l.shape, l.dtype) for l in lands],
        in_specs=[HBM] * (2 * na) + [SEM, SEM, ANY], out_specs=[HBM] * (2 * na),
        input_output_aliases={i: i for i in range(2 * na)},
        compiler_params=pltpu.CompilerParams(has_side_effects=EFFECT),
    )(*srcs, *lands, send_sems, recv_sems, after)
    return outs[:na], outs[na:]


def _mm(a, b, mode, out_dtype, name, add=None, tm=1024, tn=1024, tk=1024, into=None, o_rows=None, o_moff=0,
        loss_target=None):
    if mode == "tn":
        K, M = a.shape
    else:
        M, K = a.shape
    N = b.shape[0] if mode == "nt" else b.shape[1]
    tm, tn, tk = min(tm, M), min(tn, N), min(tk, K)
    assert M % tm == 0 and N % tn == 0 and K % tk == 0, (name, M, N, K)
    nk = K // tk
    if mode == "nn":
        a_spec = pl.BlockSpec((tm, tk), lambda i, j, kk: (i, kk))
        b_spec, dims = pl.BlockSpec((tk, tn), lambda i, j, kk: (kk, j)), _NN
    elif mode == "nt":
        a_spec = pl.BlockSpec((tm, tk), lambda i, j, kk: (i, kk))
        b_spec, dims = pl.BlockSpec((tn, tk), lambda i, j, kk: (j, kk)), _NT
    else:
        a_spec = pl.BlockSpec((tk, tm), lambda i, j, kk: (kk, i))
        b_spec, dims = pl.BlockSpec((tk, tn), lambda i, j, kk: (kk, j)), _TN
    o_spec = pl.BlockSpec((tm, tn), lambda i, j, kk: (i + o_moff, j))
    has_add, has_into, has_loss = add is not None, into is not None, loss_target is not None
    assert not has_loss or (has_add and tn == N and not has_into)
    n_in = 2 + has_add + has_loss + has_into

    def body(*refs):
        a_ref, b_ref = refs[0], refs[1]
        add_ref = refs[2] if has_add else None
        outs = refs[n_in:]

        def finish(r):
            if has_add:
                r = r + add_ref[...]
            if has_loss:
                e = r - refs[3][...]
                dy = e * (1.0 / N)
                outs[0][...] = dy
                outs[1][...] = dy.astype(BF16)
                outs[2][...] = jnp.sum(e * e, axis=0, keepdims=True)[None]
            else:
                outs[0][...] = r.astype(out_dtype)

        if nk == 1:
            finish(_dot(a_ref[...], b_ref[...], dims))
        else:
            acc = refs[-1]
            kk = pl.program_id(2)

            @pl.when(kk == 0)
            def _():
                acc[...] = _dot(a_ref[...], b_ref[...], dims)

            @pl.when((kk > 0) & (kk < nk - 1))
            def _():
                acc[...] += _dot(a_ref[...], b_ref[...], dims)

            @pl.when(kk == nk - 1)
            def _():
                finish(acc[...] + _dot(a_ref[...], b_ref[...], dims))

    tile = pl.BlockSpec((tm, tn), lambda i, j, kk: (i, j))
    ins = [a, b] + ([add] if has_add else []) + ([loss_target] if has_loss else []) + ([into] if has_into else [])
    specs = [a_spec, b_spec] + [tile] * (has_add + has_loss) + ([ANY] if has_into else [])
    rows = into.shape[0] if has_into else (o_rows if o_rows is not None else M)
    if has_loss:
        out_specs = [tile, tile, pl.BlockSpec((1, 1, N), lambda i, j, kk: (i, 0, 0))]
        out_shape = [jax.ShapeDtypeStruct((M, N), F32), jax.ShapeDtypeStruct((M, N), BF16), jax.ShapeDtypeStruct((M // tm, 1, N), F32)]
    else:
        out_specs, out_shape = o_spec, jax.ShapeDtypeStruct((rows, N), out_dtype)
    return _call(
        body, name=name, grid=(M // tm, N // tn, nk), in_specs=specs, out_specs=out_specs, out_shape=out_shape,
        scratch_shapes=[pltpu.VMEM((tm, tn), F32)] if nk > 1 else [],
        input_output_aliases={len(ins) - 1: 0} if has_into else {},
        compiler_params=_params("parallel", "parallel", "arbitrary"),
    )(*ins)


def _mm_norm_bwd(parts, b, x, resid, g, name, tm=512, tk=512):
    T, N = x.shape
    counts = [p.shape[1] // tk for p in parts]
    starts = [sum(counts[:i]) for i in range(len(parts))]
    nsteps = sum(counts)
    assert all(p.shape[1] % tk == 0 for p in parts) and b.shape == (nsteps * tk, N)
    npart = len(parts)

    def body(*refs):
        a_refs, b_ref, x_ref, res_ref, g_ref = refs[:npart], refs[npart], refs[npart + 1], refs[npart + 2], refs[npart + 3]
        dx_ref, dxb_ref, dg_ref, acc = refs[npart + 4:]
        i, s = pl.program_id(0), pl.program_id(1)

        @pl.when((i == 0) & (s == 0))
        def _():
            dg_ref[...] = jnp.zeros_like(dg_ref)

        for p in range(npart):
            @pl.when((s >= starts[p]) & (s < starts[p] + counts[p]))
            def _(p=p):
                d = _dot(a_refs[p][...], b_ref[...])

                @pl.when(s == 0)
                def _():
                    acc[...] = d

                @pl.when(s > 0)
                def _():
                    acc[...] += d

        @pl.when(s == nsteps - 1)
        def _():
            xv, dhv = x_ref[...], acc[...]
            r = lax.rsqrt(jnp.mean(xv * xv, axis=-1, keepdims=True) + EPS)
            gd = dhv * g_ref[...]
            m = jnp.mean(gd * xv, axis=-1, keepdims=True)
            dx = res_ref[...] + r * gd - xv * (r * r * r) * m
            dx_ref[...] = dx
            dxb_ref[...] = dx.astype(BF16)
            dg_ref[...] += jnp.sum(dhv * xv * r, axis=0, keepdims=True)

    a_specs = [pl.BlockSpec((tm, tk), lambda i, s, st=st, c=c: (i, jnp.clip(s - st, 0, c - 1))) for st, c in zip(starts, counts)]
    row = pl.BlockSpec((tm, N), lambda i, s: (i, 0))
    vec = pl.BlockSpec((1, N), lambda i, s: (0, 0))
    return _call(
        body, name=name, grid=(T // tm, nsteps),
        in_specs=a_specs + [pl.BlockSpec((tk, N), lambda i, s: (s, 0)), row, row, vec], out_specs=[row, row, vec],
        out_shape=[jax.ShapeDtypeStruct((T, N), F32), jax.ShapeDtypeStruct((T, N), BF16), jax.ShapeDtypeStruct((1, N), F32)],
        scratch_shapes=[pltpu.VMEM((tm, N), F32)], compiler_params=_params("arbitrary", "arbitrary"),
    )(*parts, b, x, resid, g)


def _norm_proj(x, g, wT, name, tm=1024, tn=1280):
    T, K = x.shape
    N = wT.shape[0]

    def body(x_ref, g_ref, w_ref, o_ref, h_ref):
        xv = x_ref[...]
        r = lax.rsqrt(jnp.mean(xv * xv, axis=-1, keepdims=True) + EPS)
        hv = (xv * r * g_ref[...]).astype(BF16)

        @pl.when(pl.program_id(1) == 0)
        def _():
            h_ref[...] = hv

        o_ref[...] = _dot(hv, w_ref[...], _NT)

    return _call(
        body, name=name, grid=(T // tm, N // tn),
        in_specs=[pl.BlockSpec((tm, K), lambda i, j: (i, 0)), pl.BlockSpec((1, K), lambda i, j: (0, 0)),
                  pl.BlockSpec((tn, K), lambda i, j: (j, 0))],
        out_specs=[pl.BlockSpec((tm, tn), lambda i, j: (i, j)), pl.BlockSpec((tm, K), lambda i, j: (i, 0))],
        out_shape=[jax.ShapeDtypeStruct((T, N), F32), jax.ShapeDtypeStruct((T, K), BF16)],
        compiler_params=_params("parallel", "arbitrary"),
    )(x, g, wT)


def _qk_prep(proj, pos, invf, qg, kg, bd, name, tm=512):
    T = proj.shape[0]

    def body(q_ref, k_ref, pos_ref, invf_ref, qg_ref, kg_ref, bd_ref, qo_ref, ko_ref):
        cos, sin = _rope_tables(pos_ref, invf_ref)

        def prep(xv, gv, scale):
            r = lax.rsqrt(_group_mean(xv * xv, bd_ref[...]) + EPS)
            yv = xv * r * gv
            return ((yv * cos + _rot_half(yv) * sin) * scale).astype(BF16).astype(F32)

        qo_ref[...] = prep(q_ref[...], qg_ref[...], HEAD_DIM ** -0.5)
        ko_ref[...] = prep(k_ref[...], kg_ref[...], 1.0)

    col = lambda j: pl.BlockSpec((tm, ATTN_W), lambda i, j=j: (i, j))
    vec = pl.BlockSpec((1, ATTN_W), lambda i: (0, 0))
    out = pl.BlockSpec((tm, ATTN_W), lambda i: (i, 0))
    return _call(
        body, name=name, grid=(T // tm,),
        in_specs=[col(0), col(1), pl.BlockSpec((tm, 1), lambda i: (i, 0)), vec, vec, vec,
                  pl.BlockSpec((2 * HEAD_DIM, 2 * HEAD_DIM), lambda i: (0, 0))],
        out_specs=[out, out], out_shape=[jax.ShapeDtypeStruct((T, ATTN_W), F32)] * 2,
        compiler_params=_params("parallel"),
    )(proj, proj, pos, invf, qg, kg, bd)


def _qk_prep_bwd(proj, dqh, dkh, dv, pos, invf, qg, kg, bd, name, tm=512):
    T = proj.shape[0]

    def body(q_ref, k_ref, dq_ref, dk_ref, dv_ref, pos_ref, invf_ref, qg_ref, kg_ref, bd_ref, o_ref, gq_ref, gk_ref):
        @pl.when(pl.program_id(0) == 0)
        def _():
            gq_ref[...] = jnp.zeros_like(gq_ref)
            gk_ref[...] = jnp.zeros_like(gk_ref)

        cos, sin = _rope_tables(pos_ref, invf_ref)

        def back(xv, gv, dz, scale):
            dz = dz * scale
            dy = dz * cos - _rot_half(dz * sin)
            r = lax.rsqrt(_group_mean(xv * xv, bd_ref[...]) + EPS)
            gd = dy * gv
            m = _group_mean(gd * xv, bd_ref[...])
            dx = r * gd - xv * (r * r * r) * m
            return dx, jnp.sum(dy * xv * r, axis=0, keepdims=True)

        dxq, gs = back(q_ref[...], qg_ref[...], dq_ref[...], HEAD_DIM ** -0.5)
        gq_ref[...] += gs
        dxk, gs = back(k_ref[...], kg_ref[...], dk_ref[...], 1.0)
        gk_ref[...] += gs
        o_ref[...] = jnp.concatenate([dxq.astype(BF16), dxk.astype(BF16), dv_ref[...].astype(BF16)], axis=1)

    col = lambda j: pl.BlockSpec((tm, ATTN_W), lambda i, j=j: (i, j))
    row = pl.BlockSpec((tm, ATTN_W), lambda i: (i, 0))
    vec = pl.BlockSpec((1, ATTN_W), lambda i: (0, 0))
    return _call(
        body, name=name, grid=(T // tm,),
        in_specs=[col(0), col(1), row, row, row, pl.BlockSpec((tm, 1), lambda i: (i, 0)), vec, vec, vec,
                  pl.BlockSpec((2 * HEAD_DIM, 2 * HEAD_DIM), lambda i: (0, 0))],
        out_specs=[pl.BlockSpec((tm, 3 * ATTN_W), lambda i: (i, 0)), vec, vec],
        out_shape=[jax.ShapeDtypeStruct((T, 3 * ATTN_W), BF16)] + [jax.ShapeDtypeStruct((1, ATTN_W), F32)] * 2,
        compiler_params=_params("arbitrary"),
    )(proj, proj, dqh, dkh, dv, pos, invf, qg, kg, bd)


def _ld(ref, start, size, dil):
    return ref[pl.ds(start, size), :] if dil == 1 else ref[pl.ds(start, size, stride=dil), :]


def _st(ref, start, size, dil, val):
    if dil == 1:
        ref[pl.ds(start, size), :] = val
    else:
        ref[pl.ds(start, size, stride=dil), :] = val


def _attn_geometry(T, dil):
    nb = T // dil // QBLK
    if nb == 2:
        return 1, 2 * QBLK, 2 * QBLK
    return nb, QBLK, (2 * QBLK if nb >= 2 else QBLK)


ATTN_UNROLL = 4


def _attn_unit(j, u, dil, nit):
    return ATTN_UNROLL * j + u if dil >= ATTN_UNROLL else j + u * (nit // ATTN_UNROLL)


def _attn_block(it, dil, qb, kw):
    c, n = it & (dil - 1), lax.shift_right_logical(it, dil.bit_length() - 1)
    sq = n * (qb * dil) + c
    sk = jnp.maximum(n - (kw // qb - 1), 0) * (qb * dil) + c
    qi = lax.broadcasted_iota(jnp.int32, (2 * qb, kw), 0) & (qb - 1)
    kj = lax.broadcasted_iota(jnp.int32, (2 * qb, kw), 1)
    rel = jnp.where(n > 0, kw - qb, 0) + qi - kj
    return sq, sk, (rel >= 0) & (rel <= QBLK)


def _stack_heads(xv, head0):
    z = jnp.zeros_like(xv)
    return jnp.concatenate([jnp.where(head0, xv, z), jnp.where(head0, z, xv)], axis=0)


def _unstack_heads(x2, head0):
    qb = x2.shape[0] // 2
    return jnp.where(head0, x2[:qb], x2[qb:])


def _attn_fwd(qf, kf, proj, name):
    T = qf.shape[0]

    def body(q_ref, k_ref, v_ref, o_ref, lse_ref):
        for bi, dil in enumerate(DILATIONS):
            nb, qb, kw = _attn_geometry(T, dil)
            nit = nb * dil
            head0 = lax.broadcasted_iota(jnp.int32, (qb, 2 * HEAD_DIM), 1) < HEAD_DIM

            def step(j, carry, bi=bi, dil=dil, qb=qb, kw=kw, nit=nit, head0=head0):
                units = []
                for u in range(ATTN_UNROLL):
                    sq, sk, ok = _attn_block(_attn_unit(j, u, dil, nit), dil, qb, kw)
                    old = (_ld(o_ref, sq, qb, dil), _ld(lse_ref, sq, qb, dil)) if bi > 0 else None
                    units.append((sq, ok, _ld(q_ref, sq, qb, dil).astype(BF16), _ld(k_ref, sk, kw, dil).astype(BF16),
                                  _ld(v_ref, sk, kw, dil).astype(BF16), old))
                results = []
                for sq, ok, qv, kv, vv, old in units:
                    s = jnp.where(ok, _dot(_stack_heads(qv, head0), kv, _NT), NEG_INF)
                    m = jnp.max(s, axis=-1, keepdims=True)
                    p = jnp.exp(s - m).astype(BF16)
                    acc = _dot(p, jnp.concatenate([vv, jnp.ones_like(vv)], axis=1))
                    l = acc[:, 2 * HEAD_DIM:]
                    o_new = _unstack_heads(acc[:, :2 * HEAD_DIM] / l, head0)
                    l_new = _unstack_heads(m + jnp.log(l), head0)
                    if bi > 0:
                        o_old, l_old = old
                        mx = jnp.maximum(l_old, l_new)
                        e0, e1 = jnp.exp(l_old - mx), jnp.exp(l_new - mx)
                        z = e0 + e1
                        o_new = (e0 * o_old + e1 * o_new) / z
                        l_new = mx + jnp.log(z)
                    results.append((sq, o_new, l_new))
                for sq, o_new, l_new in results:
                    _st(o_ref, sq, qb, dil, o_new)
                    _st(lse_ref, sq, qb, dil, l_new)
                return carry

            lax.fori_loop(0, nit // ATTN_UNROLL, step, 0)

    blk = lambda off: pl.BlockSpec((T, 2 * HEAD_DIM), lambda hp, off=off: (0, off + hp))
    return _call(
        body, name=name, grid=(4,), in_specs=[blk(0), blk(0), blk(8)], out_specs=[blk(0), blk(0)],
        out_shape=[jax.ShapeDtypeStruct((T, ATTN_W), F32)] * 2, compiler_params=_params("parallel"),
    )(qf, kf, proj)


def _attn_bwd(qf, kf, proj, do, lse, delta, name):
    T = qf.shape[0]

    def body(q_ref, k_ref, v_ref, do_ref, lse_ref, dl_ref, dq_ref, dk_ref, dv_ref):
        for ref in (dq_ref, dk_ref, dv_ref):
            ref[...] = jnp.zeros_like(ref)
        for dil in DILATIONS:
            nb, qb, kw = _attn_geometry(T, dil)
            nit = nb * dil
            head0 = lax.broadcasted_iota(jnp.int32, (qb, 2 * HEAD_DIM), 1) < HEAD_DIM

            def step(j, carry, dil=dil, qb=qb, kw=kw, nit=nit, head0=head0):
                units = []
                for u in range(ATTN_UNROLL):
                    sq, sk, ok = _attn_block(_attn_unit(j, u, dil, nit), dil, qb, kw)
                    lsev, dlv = _ld(lse_ref, sq, qb, dil), _ld(dl_ref, sq, qb, dil)
                    units.append((sq, sk, ok, _ld(q_ref, sq, qb, dil).astype(BF16), _ld(do_ref, sq, qb, dil).astype(BF16),
                                  jnp.concatenate([lsev[:, 0:1], lsev[:, HEAD_DIM:HEAD_DIM + 1]], axis=0),
                                  jnp.concatenate([dlv[:, 0:1], dlv[:, HEAD_DIM:HEAD_DIM + 1]], axis=0),
                                  _ld(k_ref, sk, kw, dil).astype(BF16), _ld(v_ref, sk, kw, dil).astype(BF16),
                                  _ld(dq_ref, sq, qb, dil), _ld(dk_ref, sk, kw, dil), _ld(dv_ref, sk, kw, dil)))
                results = []
                for sq, sk, ok, qv, dov, lse2, dl2, kv, vv, dq0, dk0, dv0 in units:
                    q2, do2 = _stack_heads(qv, head0), _stack_heads(dov, head0)
                    p = jnp.where(ok, jnp.exp(_dot(q2, kv, _NT) - lse2), 0.0)
                    ds = (p * (_dot(do2, vv, _NT) - dl2)).astype(BF16)
                    results.append((sq, sk, dq0 + _unstack_heads(_dot(ds, kv), head0),
                                    dk0 + _dot(ds, q2, _TN), dv0 + _dot(p.astype(BF16), do2, _TN)))
                for sq, sk, dq, dk, dv in results:
                    _st(dq_ref, sq, qb, dil, dq)
                    _st(dk_ref, sk, kw, dil, dk)
                    _st(dv_ref, sk, kw, dil, dv)
                return carry

            lax.fori_loop(0, nit // ATTN_UNROLL, step, 0)

    blk = lambda off: pl.BlockSpec((T, 2 * HEAD_DIM), lambda hp, off=off: (0, off + hp))
    return _call(
        body, name=name, grid=(4,), in_specs=[blk(0), blk(0), blk(8), blk(0), blk(0), blk(0)], out_specs=[blk(0)] * 3,
        out_shape=[jax.ShapeDtypeStruct((T, ATTN_W), F32)] * 3, compiler_params=_params("parallel"),
    )(qf, kf, proj, do, lse, delta)


def _attn_norm(attn, g, name, tm=512):
    T = attn.shape[0]

    def body(a_ref, g_ref, o_ref):
        av = a_ref[...]
        r = lax.rsqrt(jnp.mean(av * av, axis=-1, keepdims=True) + EPS)
        o_ref[...] = (av * r * g_ref[...]).astype(BF16)

    row = pl.BlockSpec((tm, ATTN_W), lambda i: (i, 0))
    return _call(
        body, name=name, grid=(T // tm,), in_specs=[row, pl.BlockSpec((1, ATTN_W), lambda i: (0, 0))], out_specs=row,
        out_shape=jax.ShapeDtypeStruct((T, 2 * ATTN_W), BF16), compiler_params=_params("parallel"),
    )(attn, g)


def _attn_norm_bwd(dmix, attn, g, bd, name, tm=512):
    T = attn.shape[0]

    def body(d_ref, a_ref, g_ref, bd_ref, do_ref, dl_ref, dg_ref):
        @pl.when(pl.program_id(0) == 0)
        def _():
            dg_ref[...] = jnp.zeros_like(dg_ref)

        dy, av = d_ref[...], a_ref[...]
        r = lax.rsqrt(jnp.mean(av * av, axis=-1, keepdims=True) + EPS)
        gd = dy * g_ref[...]
        m = jnp.mean(gd * av, axis=-1, keepdims=True)
        da = r * gd - av * (r * r * r) * m
        do_ref[...] = da
        dl_ref[...] = _group_mean(da * av, bd_ref[...]) * float(HEAD_DIM)
        dg_ref[...] += jnp.sum(dy * av * r, axis=0, keepdims=True)

    row = pl.BlockSpec((tm, ATTN_W), lambda i: (i, 0))
    vec = pl.BlockSpec((1, ATTN_W), lambda i: (0, 0))
    return _call(
        body, name=name, grid=(T // tm,),
        in_specs=[row, row, vec, pl.BlockSpec((2 * HEAD_DIM, 2 * HEAD_DIM), lambda i: (0, 0))], out_specs=[row, row, vec],
        out_shape=[jax.ShapeDtypeStruct((T, ATTN_W), F32)] * 2 + [jax.ShapeDtypeStruct((1, ATTN_W), F32)],
        compiler_params=_params("arbitrary"),
    )(dmix, attn, g, bd)


def _rec_gates(xc, wrg_ref, wig_ref, brg_ref, big_ref, lam_ref):
    xb = xc.astype(BF16)
    r = _sigmoid(_dot(xb, wrg_ref[...]) + brg_ref[...])
    ig = _sigmoid(_dot(xb, wig_ref[...]) + big_ref[...])
    sp = _softplus_neg(lam_ref[...])
    log_a = -LRU_C * r * sp
    a = jnp.exp(log_a)
    th = jnp.tanh(log_a)
    mult = jnp.sqrt(-2.0 * th / (1.0 - th))
    return xb, r, ig, sp, a, mult


def _rec_fwd(proj, mix, cw, cb, wrg, wig, brg, big, lam, g, name, tm=256):
    T = proj.shape[0]
    hb = tm // 8

    def body(xr_ref, halo_ref, gr_ref, cw_ref, cb_ref, wrg_ref, wig_ref, brg_ref, big_ref, lam_ref, g_ref, mix_ref,
             xc_ref, h_ref, out_ref, carry):
        i = pl.program_id(0)

        @pl.when(i == 0)
        def _():
            carry[...] = jnp.zeros_like(carry)

        xr = xr_ref[...]
        halo = jnp.where(i > 0, halo_ref[...], 0.0)
        xc = cb_ref[...] + cw_ref[3:4, :] * xr
        for s in range(1, REC_CONV):
            xc = xc + cw_ref[3 - s:4 - s, :] * _shift_down(xr, halo, s)
        xc_ref[...] = xc
        _, _, ig, _, a, mult = _rec_gates(xc, wrg_ref, wig_ref, brg_ref, big_ref, lam_ref)
        pa, hl = _scan_fwd(a, mult * (ig * xc))
        h = hl + pa * carry[0:1, :]
        h_ref[...] = h
        carry[0:1, :] = h_ref[pl.ds(tm - 1, 1), :]
        hg = h * _gelu(gr_ref[...])
        r = lax.rsqrt(jnp.mean(hg * hg, axis=-1, keepdims=True) + EPS)
        out_ref[...] = (hg * r * g_ref[...]).astype(BF16)

    vec = pl.BlockSpec((1, REC_W), lambda i: (0, 0))
    row = pl.BlockSpec((tm, REC_W), lambda i: (i, 0))
    mat = pl.BlockSpec((REC_W, REC_W), lambda i: (0, 0))
    return _call(
        body, name=name, grid=(T // tm,),
        in_specs=[pl.BlockSpec((tm, REC_W), lambda i: (i, 3)),
                  pl.BlockSpec((8, REC_W), lambda i: (jnp.maximum(i * hb - 1, 0), 3)),
                  pl.BlockSpec((tm, REC_W), lambda i: (i, 4)),
                  pl.BlockSpec((8, REC_W), lambda i: (0, 0)), vec, mat, mat, vec, vec, vec, vec, ANY],
        out_specs=[row, row, pl.BlockSpec((tm, REC_W), lambda i: (i, 1))],
        out_shape=[jax.ShapeDtypeStruct((T, REC_W), F32)] * 2 + [jax.ShapeDtypeStruct(mix.shape, BF16)],
        scratch_shapes=[pltpu.VMEM((8, REC_W), F32)], input_output_aliases={11: 2},
        compiler_params=_params("arbitrary"),
    )(proj, proj, proj, cw, cb, wrg, wig, brg, big, lam, g, mix)


def _rec_bwd(dmix, proj, xc, h, cw, cb, wrg, wig, brg, big, lam, g, name, tm=256):
    T = proj.shape[0]
    nt = T // tm
    hb = tm // 8

    def body(d_ref, xr_ref, xhalo_ref, gr_ref, xc_ref, h_ref, hhalo_ref, cw_ref, cb_ref, wrg_ref, wig_ref, brg_ref,
             big_ref, lam_ref, g_ref,
             drec_ref, gcw_ref, gcb_ref, gwrg_ref, gwig_ref, gbrg_ref, gbig_ref, glam_ref, gg_ref,
             g_carry, a_first, dxc_next, gsp):
        i = pl.program_id(0)
        first_tile = i == nt - 1

        @pl.when(i == 0)
        def _():
            for ref in (gcw_ref, gcb_ref, gwrg_ref, gwig_ref, gbrg_ref, gbig_ref, glam_ref, gg_ref,
                        g_carry, a_first, dxc_next, gsp):
                ref[...] = jnp.zeros_like(ref)

        xr, xc, hv = xr_ref[...], xc_ref[...], h_ref[...]
        xhalo = jnp.where(first_tile, 0.0, xhalo_ref[...])
        hhalo = jnp.where(first_tile, 0.0, hhalo_ref[...])
        xb, r, ig, sp, a, mult = _rec_gates(xc, wrg_ref, wig_ref, brg_ref, big_ref, lam_ref)
        h_prev = _shift_down(hv, hhalo, 1)
        ge, dge = _gelu_and_grad(gr_ref[...])
        hg = hv * ge
        rr = lax.rsqrt(jnp.mean(hg * hg, axis=-1, keepdims=True) + EPS)
        dy = d_ref[...]
        gd = dy * g_ref[...]
        dhg = rr * gd - hg * (rr * rr * rr) * jnp.mean(gd * hg, axis=-1, keepdims=True)
        gg_ref[...] += jnp.sum(dy * hg * rr, axis=0, keepdims=True)
        dgr = (dhg * hv * dge).astype(BF16)
        dh = dhg * ge
        b = _shift_up(a, jnp.broadcast_to(a_first[0:1, :], (8, REC_W)), 1)
        pb, gl = _scan_bwd(b, dh)
        gs = gl + pb * g_carry[0:1, :]
        g_carry[0:1, :] = gs[0:1, :]
        a_first[0:1, :] = a[0:1, :]
        da = gs * h_prev
        dmult = gs * (ig * xc)
        di = gs * (mult * xc)
        dxc = gs * (mult * ig)
        dlog_a = da * a - dmult * (a * a) / mult
        gsp[...] += jnp.sum(dlog_a * (-LRU_C * r), axis=0, keepdims=True)
        dzr = (dlog_a * (-LRU_C * sp)) * (r * (1.0 - r))
        dzi = di * (ig * (1.0 - ig))
        dzr_b, dzi_b = dzr.astype(BF16), dzi.astype(BF16)
        dxc = dxc + _dot(dzr_b, wrg_ref[...], _NT) + _dot(dzi_b, wig_ref[...], _NT)
        gwrg_ref[...] += _dot(xb, dzr_b, _TN)
        gwig_ref[...] += _dot(xb, dzi_b, _TN)
        gbrg_ref[...] += jnp.sum(dzr, axis=0, keepdims=True)
        gbig_ref[...] += jnp.sum(dzi, axis=0, keepdims=True)
        nxt = dxc_next[...]
        dxr = cw_ref[3:4, :] * dxc
        gcw_ref[3:4, :] += jnp.sum(dxc * xr, axis=0, keepdims=True)
        for s in range(1, REC_CONV):
            dxr = dxr + cw_ref[3 - s:4 - s, :] * _shift_up(dxc, nxt, s)
            gcw_ref[3 - s:4 - s, :] += jnp.sum(dxc * _shift_down(xr, xhalo, s), axis=0, keepdims=True)
        gcb_ref[...] += jnp.sum(dxc, axis=0, keepdims=True)
        dxc_next[...] = dxc[:8]
        drec_ref[...] = jnp.concatenate([dxr.astype(BF16), dgr], axis=1)

        @pl.when(first_tile)
        def _():
            glam_ref[...] = gsp[...] * (-_sigmoid(-lam_ref[...]))

    rev = lambda i: nt - 1 - i
    vec = pl.BlockSpec((1, REC_W), lambda i: (0, 0))
    row = pl.BlockSpec((tm, REC_W), lambda i: (rev(i), 0))
    mat = pl.BlockSpec((REC_W, REC_W), lambda i: (0, 0))
    cwb = pl.BlockSpec((8, REC_W), lambda i: (0, 0))
    halo = lambda c: pl.BlockSpec((8, REC_W), lambda i, c=c: (jnp.maximum(rev(i) * hb - 1, 0), c))
    return _call(
        body, name=name, grid=(nt,),
        in_specs=[pl.BlockSpec((tm, REC_W), lambda i: (rev(i), 1)),
                  pl.BlockSpec((tm, REC_W), lambda i: (rev(i), 3)), halo(3),
                  pl.BlockSpec((tm, REC_W), lambda i: (rev(i), 4)),
                  row, row, halo(0), cwb, vec, mat, mat, vec, vec, vec, vec],
        out_specs=[pl.BlockSpec((tm, 2 * REC_W), lambda i: (rev(i), 0)), cwb, vec, mat, mat, vec, vec, vec, vec],
        out_shape=[jax.ShapeDtypeStruct((T, 2 * REC_W), BF16)]
        + [jax.ShapeDtypeStruct((8, REC_W), F32), jax.ShapeDtypeStruct((1, REC_W), F32)]
        + [jax.ShapeDtypeStruct((REC_W, REC_W), F32)] * 2 + [jax.ShapeDtypeStruct((1, REC_W), F32)] * 4,
        scratch_shapes=[pltpu.VMEM((8, REC_W), F32)] * 3 + [pltpu.VMEM((1, REC_W), F32)],
        compiler_params=_params("arbitrary"),
    )(dmix, proj, proj, proj, xc, h, h, cw, cb, wrg, wig, brg, big, lam, g)


def _ffn_conv(x_ext, cw_ref, cb_ref):
    return (cb_ref[...] + cw_ref[2:3, :] * x_ext + cw_ref[1:2, :] * pltpu.roll(x_ext, 1, 0)
            + cw_ref[0:1, :] * pltpu.roll(x_ext, 2, 0))


def _up_proj_act(x2, g, w_upT, cw, cb, name, tm=1024, tc=768):
    T = x2.shape[0]
    nc = D_FF // tc

    def body(x_ref, g_ref, wg_ref, wu_ref, cwg_ref, cwu_ref, cbg_ref, cbu_ref, act_ref, da_ref, db_ref, pg_ref, pu_ref,
             h_ref, hist_g, hist_u, hs):
        i, j = pl.program_id(0), pl.program_id(1)

        @pl.when(j == 0)
        def _():
            xv = x_ref[...]
            r = lax.rsqrt(jnp.mean(xv * xv, axis=-1, keepdims=True) + EPS)
            hs[...] = (xv * r * g_ref[...]).astype(BF16)
            h_ref[...] = hs[...]

        hv = hs[...]
        pg, pu = _dot(hv, wg_ref[...], _NT), _dot(hv, wu_ref[...], _NT)
        ge = jnp.concatenate([jnp.where(i > 0, hist_g[j], 0.0), pg], axis=0)
        ue = jnp.concatenate([jnp.where(i > 0, hist_u[j], 0.0), pu], axis=0)
        gel, dgel = _gelu_and_grad(_ffn_conv(ge, cwg_ref, cbg_ref)[8:])
        uu = _ffn_conv(ue, cwu_ref, cbu_ref)[8:]
        act_ref[...] = (gel * uu).astype(BF16)
        da_ref[...] = (uu * dgel).astype(BF16)
        db_ref[...] = gel.astype(BF16)
        pg_ref[...] = pg.astype(BF16)
        pu_ref[...] = pu.astype(BF16)
        hist_g[j] = pg[tm - 8:]
        hist_u[j] = pu[tm - 8:]

    tile = pl.BlockSpec((tm, tc), lambda i, j: (i, j))
    wsp = lambda off: pl.BlockSpec((tc, D_MODEL), lambda i, j, off=off: (j + off, 0))
    cws = lambda off: pl.BlockSpec((8, tc), lambda i, j, off=off: (0, j + off))
    cbs = lambda off: pl.BlockSpec((1, tc), lambda i, j, off=off: (0, j + off))
    return _call(
        body, name=name, grid=(T // tm, nc),
        in_specs=[pl.BlockSpec((tm, D_MODEL), lambda i, j: (i, 0)), pl.BlockSpec((1, D_MODEL), lambda i, j: (0, 0)),
                  wsp(0), wsp(nc), cws(0), cws(nc), cbs(0), cbs(nc)],
        out_specs=[tile] * 5 + [pl.BlockSpec((tm, D_MODEL), lambda i, j: (i, 0))],
        out_shape=[jax.ShapeDtypeStruct((T, D_FF), BF16)] * 5 + [jax.ShapeDtypeStruct((T, D_MODEL), BF16)],
        scratch_shapes=[pltpu.VMEM((nc, 8, tc), F32)] * 2 + [pltpu.VMEM((tm, D_MODEL), BF16)],
        compiler_params=_params("arbitrary", "arbitrary"),
    )(x2, g, w_upT, w_upT, cw, cw, cb, cb)


def _ffn_bwd(dyb, w_down, da, db, pg, pu, cw, name, tm=1024, tc=768):
    T, F = pg.shape
    nt = T // tm
    hb16 = tm // 16
    nc = F // tc
    n = tm + 8

    def body(dy_ref, dyn_ref, wd_ref, a_ref, an_ref, b_ref, bn_ref, g_ref, u_ref, cwg_ref, cwu_ref,
             dg_ref, du_ref, gcwg_ref, gcwu_ref, gcbg_ref, gcbu_ref):
        i = pl.program_id(1)
        last = i == nt - 1

        @pl.when(i == 0)
        def _():
            for ref in (gcwg_ref, gcwu_ref, gcbg_ref, gcbu_ref):
                ref[...] = jnp.zeros_like(ref)

        wd = wd_ref[...]
        dact_next = jnp.where(last, 0.0, _dot(dyn_ref[...], wd, _NT)[:8])
        de = jnp.concatenate([_dot(dy_ref[...], wd, _NT), dact_next], axis=0)
        ext = lambda t, nx: jnp.concatenate([t[...].astype(F32), nx[...].astype(F32)[:8]], axis=0)
        for dcv, x_ref, cw_ref, dx_ref, gcw_ref, gcb_ref in ((de * ext(a_ref, an_ref), g_ref, cwg_ref, dg_ref, gcwg_ref, gcbg_ref),
                                                               (de * ext(b_ref, bn_ref), u_ref, cwu_ref, du_ref, gcwu_ref, gcbu_ref)):
            s1, s2 = pltpu.roll(dcv, n - 1, 0), pltpu.roll(dcv, n - 2, 0)
            dx_ref[...] = (cw_ref[2:3, :] * dcv + cw_ref[1:2, :] * s1 + cw_ref[0:1, :] * s2)[:tm].astype(BF16)
            xv = x_ref[...].astype(F32)
            gcw_ref[2:3, :] += jnp.sum(xv * dcv[:tm], axis=0, keepdims=True)
            gcw_ref[1:2, :] += jnp.sum(xv * s1[:tm], axis=0, keepdims=True)
            gcw_ref[0:1, :] += jnp.sum(xv * s2[:tm], axis=0, keepdims=True)
            gcb_ref[...] += jnp.sum(dcv[:tm], axis=0, keepdims=True)

    tile = pl.BlockSpec((tm, tc), lambda j, i: (i, j))
    nxt = pl.BlockSpec((16, tc), lambda j, i: (jnp.minimum((i + 1) * hb16, nt * hb16 - 1), j))
    cws = lambda off: pl.BlockSpec((8, tc), lambda j, i, off=off: (0, j + off))
    cbs = pl.BlockSpec((1, tc), lambda j, i: (0, j))
    return _call(
        body, name=name, grid=(nc, nt),
        in_specs=[pl.BlockSpec((tm, D_MODEL), lambda j, i: (i, 0)),
                  pl.BlockSpec((16, D_MODEL), lambda j, i: (jnp.minimum((i + 1) * hb16, nt * hb16 - 1), 0)),
                  pl.BlockSpec((tc, D_MODEL), lambda j, i: (j, 0)), tile, nxt, tile, nxt, tile, tile, cws(0), cws(nc)],
        out_specs=[tile, tile, cws(0), cws(0), cbs, cbs],
        out_shape=[jax.ShapeDtypeStruct((T, F), BF16)] * 2 + [jax.ShapeDtypeStruct((8, F), F32)] * 2
        + [jax.ShapeDtypeStruct((1, F), F32)] * 2,
        compiler_params=_params("parallel", "arbitrary"),
    )(dyb, dyb, w_down, da, da, db, db, pg, pu, cw, cw)


def _adam_update(w, g, m, v):
    m2 = ADAM_B1 * m + (1.0 - ADAM_B1) * g
    v2 = ADAM_B2 * v + (1.0 - ADAM_B2) * (g * g)
    m_hat = m2 / (1.0 - ADAM_B1 ** ADAM_STEP)
    v_hat = v2 / (1.0 - ADAM_B2 ** ADAM_STEP)
    delta = -ADAM_LR * (m_hat / (jnp.sqrt(v_hat) + ADAM_EPS) + ADAM_WD * w)
    return delta, m2, v2


def _adam_sharded(p, r2, idx, w, m, v, name, transposed=False):
    r, n = p.shape[1:]
    nrecv = r2.shape[0]
    tr = (256 if r % 256 == 0 else r) if transposed else _row_tile(r)

    def body(c_ref, p_ref, r_ref, w_ref, m_ref, v_ref, g_ref, d_ref, m2_ref, v2_ref):
        g = p_ref[...].astype(F32)
        for k in range(nrecv):
            g = g + r_ref[k].astype(F32)
        if transposed:
            g = g.T
        g_ref[...] = g
        d_ref[...], m2_ref[...], v2_ref[...] = _adam_update(w_ref[...], g, m_ref[...], v_ref[...])

    blk = pl.BlockSpec((n, tr), lambda i, c_ref: (0, i)) if transposed else pl.BlockSpec((tr, n), lambda i, c_ref: (i, 0))
    spec = pltpu.PrefetchScalarGridSpec(
        num_scalar_prefetch=1, grid=(r // tr,),
        in_specs=[pl.BlockSpec((None, tr, n), lambda i, c_ref: (c_ref[0], i, 0)),
                  pl.BlockSpec((nrecv, tr, n), lambda i, c_ref: (0, i, 0)), blk, blk, blk],
        out_specs=[blk] * 4)
    return _call(body, name=name, grid_spec=spec, out_shape=[jax.ShapeDtypeStruct(w.shape, F32)] * 4,
                 compiler_params=_params("parallel"))(idx, p, r2, w, m, v)


def _sum_slabs(p, r2, idx, name):
    _, r, n = p.shape

    def body(c_ref, p_ref, r_ref, o_ref):
        acc = p_ref[...]
        for k in range(N_PEERS):
            acc = acc + r_ref[k]
        o_ref[...] = acc

    spec = pltpu.PrefetchScalarGridSpec(
        num_scalar_prefetch=1, grid=(1,),
        in_specs=[pl.BlockSpec((None, r, n), lambda i, c_ref: (c_ref[0], 0, 0)),
                  pl.BlockSpec((N_PEERS, r, n), lambda i, c_ref: (0, 0, 0))],
        out_specs=pl.BlockSpec((r, n), lambda i, c_ref: (0, 0)))
    return _call(body, name=name, grid_spec=spec, out_shape=jax.ShapeDtypeStruct((r, n), F32))(idx, p, r2)


def _adam_small(ws, gs, ms, vs, name):
    n = len(ws)

    def body(*refs):
        for i in range(n):
            d, m2, v2 = _adam_update(refs[i][...], refs[n + i][...], refs[2 * n + i][...], refs[3 * n + i][...])
            refs[4 * n + i][...] = d
            refs[5 * n + i][...] = m2
            refs[6 * n + i][...] = v2

    outs = _call(body, name=name, out_shape=[jax.ShapeDtypeStruct(w.shape, F32) for w in ws] * 3)(*ws, *gs, *ms, *vs)
    return outs[:n], outs[n:2 * n], outs[2 * n:]


def _pack_small_grads(full, halves, rcw, fcwg, fcwu, wrg, wig, lparts, name):
    nf, nh = len(full), len(halves)

    def body(*refs):
        o = refs[-1]
        o[...] = jnp.zeros_like(o)
        row = 0
        for r in refs[:nf]:
            for j in range(r.shape[1] // 1024):
                o[row:row + 1, :] = r[:, 1024 * j:1024 * (j + 1)]
                row += 1
        for k in range(0, nh, 2):
            o[row:row + 1, 0:512] = refs[nf + k][...]
            o[row:row + 1, 512:1024] = refs[nf + k + 1][...]
            row += 1
        rcw_ref, fg_ref, fu_ref, wrg_ref, wig_ref, l_ref = refs[nf + nh:nf + nh + 6]
        for k in range(2):
            o[row:row + 1, 0:512] = rcw_ref[2 * k:2 * k + 1, :]
            o[row:row + 1, 512:1024] = rcw_ref[2 * k + 1:2 * k + 2, :]
            row += 1
        for f_ref in (fg_ref, fu_ref):
            for k in range(FFN_CONV):
                for j in range(D_FF // 1024):
                    o[row:row + 1, :] = f_ref[k:k + 1, 1024 * j:1024 * (j + 1)]
                    row += 1
        assert row == 32
        for n in range(8):
            o[32:96, 64 * n:64 * n + 64] = wrg_ref[64 * n:64 * n + 64, 64 * n:64 * n + 64]
            o[32:96, 512 + 64 * n:512 + 64 * n + 64] = wig_ref[64 * n:64 * n + 64, 64 * n:64 * n + 64]
        o[96:97, :] = jnp.sum(l_ref[...], axis=0, keepdims=True)

    return _call(body, name=name, out_shape=jax.ShapeDtypeStruct((SMALL_ROWS, 1024), F32))(
        *full, *halves, rcw, fcwg, fcwu, wrg, wig, lparts)


def _block_diag(w):
    eye = jnp.eye(8, dtype=w.dtype)
    return (w[:, :, None, :] * eye[:, None, :, None]).reshape(512, 512)


def kernel(x, positions, g_mix, w_in, q_norm_g, k_norm_g, rec_conv_w, rec_conv_b, w_rg, b_rg, w_ig, b_ig, lru_lambda, g_attn_out, g_rec_out, w_out, g_ffn, w_up, ffn_conv_w, ffn_conv_b, w_down, loss_target, m_g_mix, m_w_in, m_q_norm_g, m_k_norm_g, m_rec_conv_w, m_rec_conv_b, m_w_rg, m_b_rg, m_w_ig, m_b_ig, m_lru_lambda, m_g_attn_out, m_g_rec_out, m_w_out, m_g_ffn, m_w_up, m_ffn_conv_w, m_ffn_conv_b, m_w_down, v_g_mix, v_w_in, v_q_norm_g, v_k_norm_g, v_rec_conv_w, v_rec_conv_b, v_w_rg, v_b_rg, v_w_ig, v_b_ig, v_lru_lambda, v_g_attn_out, v_g_rec_out, v_w_out, v_g_ffn, v_w_up, v_ffn_conv_w, v_ffn_conv_b, v_w_down):
    T = x.shape[1]
    ix, iy, ic = lax.axis_index("x"), lax.axis_index("y"), lax.axis_index("c")
    dev = 4 * ix + 2 * iy + ic
    xs = x.reshape(T, D_MODEL)
    tgt = loss_target.reshape(T, D_MODEL)
    pos = positions.reshape(T, 1)

    shards = {"w_in": (w_in[0], m_w_in[0], v_w_in[0]), "w_out": (w_out[0], m_w_out[0], v_w_out[0]),
              "w_up": (w_up[0], m_w_up[0], v_w_up[0]), "w_down": (w_down[0], m_w_down[0], v_w_down[0])}
    taps = jnp.concatenate([rec_conv_w.reshape(-1), ffn_conv_w.reshape(-1), jnp.zeros((4096 - 2560,), F32)]).reshape(8, 512)
    W_inT, taps_all = _all_gather([w_in[0].T.astype(BF16), taps], "ag_w_in")
    gather_landing = lambda s: _landing((N_DEV * s.shape[0], 1024), BF16, s, dev * s.shape[0])
    late = [w_out[0].astype(BF16), w_up[0].T.astype(BF16)]
    ag_send, ag_recv, late_thru, land_thru, ag_token = _exchange_start(
        late, [gather_landing(s) for s in late], "gather", taps_all, "ag_late_start")
    w_down_b = w_down[0].astype(BF16)
    down_landing = gather_landing(w_down_b)
    taps_all = taps_all.reshape(N_DEV, 4096)
    rcw = taps_all[:, :256].reshape(8, 4, 64).transpose(1, 0, 2).reshape(4, REC_W)
    fcw = taps_all[:, 256:2560].reshape(8, 3, 768).transpose(1, 0, 2).reshape(3, 2 * D_FF)
    rcw8 = jnp.pad(rcw, ((0, 4), (0, 0)))
    fcw8 = jnp.pad(fcw, ((0, 5), (0, 0)))
    fcb = ffn_conv_b.reshape(1, 2 * D_FF)

    half = HEAD_DIM // 2
    inv_freq = ROPE_THETA ** (-jnp.arange(half, dtype=F32) / half)
    invf = jnp.tile(inv_freq, 2 * N_HEADS).reshape(1, ATTN_W)
    bd = jnp.asarray(np.kron(np.eye(2), np.full((HEAD_DIM, HEAD_DIM), 1.0 / HEAD_DIM)), BF16)
    qg = jnp.tile(q_norm_g.reshape(HEAD_DIM), N_HEADS).reshape(1, ATTN_W)
    kg = jnp.tile(k_norm_g.reshape(HEAD_DIM), N_HEADS).reshape(1, ATTN_W)
    wrg_bd = _block_diag(w_rg[0]).astype(BF16)
    wig_bd = _block_diag(w_ig[0]).astype(BF16)
    brg, big = b_rg.reshape(1, REC_W), b_ig.reshape(1, REC_W)

    proj, h1 = _norm_proj(xs, g_mix + ag_token[0, 0], W_inT, "in_proj", tn=IN_W)
    qf, kf = _qk_prep(proj, pos, invf, qg, kg, bd, "qk_prep")
    attn, lse = _attn_fwd(qf, kf, proj, "attn_fwd")
    dn_send, dn_recv, dn_thru, dn_land, dn_token = _exchange_start(
        [w_down_b], [down_landing], "gather", attn, "ag_down_start")
    mix = _attn_norm(attn, g_attn_out + dn_token[0, 0], "attn_norm")
    xc, hstate, mix = _rec_fwd(proj, mix, rcw8, rec_conv_b, wrg_bd, wig_bd, brg, big, lru_lambda, g_rec_out, "rec_fwd")
    _, (W_out, W_upT) = _exchange_wait(ag_send, ag_recv, late_thru, land_thru, "gather", hstate, "ag_late_wait")
    x2 = _mm(mix, W_out, "nn", F32, "out_proj", add=xs)

    act, da, db, pg, pu, h2 = _up_proj_act(x2, g_ffn, W_upT, fcw8, fcb, "up_proj_act")
    _, (W_down,) = _exchange_wait(dn_send, dn_recv, dn_thru, dn_land, "gather", h2, "ag_down_wait")
    dy, dyb, lparts = _mm(act, W_down, "nn", F32, "down_proj_loss", add=x2, loss_target=tgt, tm=512, tk=D_FF)

    g_down = _mm(act, dyb, "tn", BF16, "g_w_down", tk=4096)
    dpg, dpu, g_fcwg, g_fcwu, g_fcbg, g_fcbu = _ffn_bwd(dyb, W_down, da, db, pg, pu, fcw8, "ffn_bwd")
    g_upT = _mm(dpg, h2, "tn", BF16, "g_w_up_gate", tk=4096, o_rows=2 * D_FF)
    g_upT = _mm(dpu, h2, "tn", BF16, "g_w_up_up", tk=4096, into=g_upT, o_moff=D_FF // 1024)
    ffn_g = [g_upT.reshape(N_DEV, 2 * D_FF // N_DEV, 1024), g_down.reshape(N_DEV, D_FF // N_DEV, 1024)]
    rs_send, rs_recv, ffn_g, ffn_land, rs_token = _exchange_start(
        ffn_g, [_landing((N_PEERS,) + g.shape[1:], BF16) for g in ffn_g], "scatter", dpu, "rs_ffn_start")
    dx2, dx2b, g_gffn = _mm_norm_bwd([dpg, dpu], W_upT, x2, dy, g_ffn + rs_token[0, 0], "d_h2_norm_bwd", tm=1024, tk=1536)

    dmix = _mm(dx2b, W_out, "nt", F32, "d_mix")
    g_out = _mm(mix, dx2b, "tn", BF16, "g_w_out", tk=4096).reshape(N_DEV, D_MODEL // N_DEV, 1024)
    out_send, out_recv, (g_out,), out_land, out_token = _exchange_start(
        [g_out], [_landing((N_PEERS,) + g_out.shape[1:], BF16)], "scatter", dmix, "rs_out_start")
    do, delta, g_gattn = _attn_norm_bwd(dmix, attn, g_attn_out + out_token[0, 0], bd, "attn_norm_bwd")
    dqh, dkh, dv = _attn_bwd(qf, kf, proj, do, lse, delta, "attn_bwd")
    dqkv, g_qg, g_kg = _qk_prep_bwd(proj, dqh, dkh, dv, pos, invf, qg, kg, bd, "qk_prep_bwd")
    (drec, g_rcw, g_rcb, g_wrg, g_wig, g_brg, g_big, g_lam, g_grec) = _rec_bwd(
        dmix, proj, xc, hstate, rcw8, rec_conv_b, wrg_bd, wig_bd, brg, big, lru_lambda, g_rec_out, "rec_bwd")
    g_inT = _mm(dqkv, h1, "tn", BF16, "g_w_in_qkv", tm=512, tk=4096, o_rows=IN_W)
    g_inT = _mm(drec, h1, "tn", BF16, "g_w_in_rec", tm=512, tk=4096, into=g_inT, o_moff=3 * ATTN_W // 512)
    g_inT = g_inT.reshape(N_DEV, IN_W // N_DEV, 1024)
    in_send, in_recv, (g_inT,), in_land, in_token = _exchange_start(
        [g_inT], [_landing((N_PEERS,) + g_inT.shape[1:], BF16)], "scatter", drec, "rs_in_start")
    grad_x, _, g_gmix = _mm_norm_bwd([dqkv, drec], W_inT, xs, dx2, g_mix + in_token[0, 0], "d_h1_norm_bwd", tm=1024, tk=512)

    flat = _pack_small_grads([g_gmix, g_gffn, g_fcbg, g_fcbu], [g_rcb, g_brg, g_big, g_lam, g_gattn, g_grec, g_qg, g_kg],
                             g_rcw, g_fcwg, g_fcwu, g_wrg, g_wig, lparts.reshape(-1, D_MODEL), "pack_small_grads")
    srows = SMALL_ROWS // N_DEV
    flat = flat.reshape(N_DEV, srows, 1024)
    sm_send, sm_recv, (flat,), sm_land, sm_token = _exchange_start(
        [flat], [_landing((N_PEERS, srows, 1024), F32)], "scatter", grad_x, "ar_small_rs_start")

    devi = jnp.reshape(dev, (1,)).astype(jnp.int32)
    ffn_g, ffn_land = _exchange_wait(rs_send, rs_recv, ffn_g, ffn_land, "scatter", sm_token, "rs_ffn_wait")
    (g_out,), out_land = _exchange_wait(out_send, out_recv, [g_out], out_land, "scatter", sm_token, "rs_out_wait")
    big_out = {"grad": {}, "delta": {}, "new_m": {}, "new_v": {}}

    def adam_big(nm, p, r):
        w_, m_, v_ = shards[nm]
        res = _adam_sharded(p, r, devi, w_, m_, v_, "adam_" + nm, transposed=nm in ("w_in", "w_up"))
        for kind, a in zip(("grad", "delta", "new_m", "new_v"), res):
            big_out[kind][nm] = a[None]
        return res[0]

    last = adam_big("w_up", ffn_g[0], ffn_land[0])
    (flat,), sm_land = _exchange_wait(sm_send, sm_recv, [flat], sm_land, "scatter", last, "ar_small_rs_wait")
    mine = _sum_slabs(flat, sm_land[0], devi, "sum_small_grads")
    sm_send, sm_recv, (mine,), sm_land, sm_token = _exchange_start(
        [mine], [_landing((SMALL_ROWS, 1024), F32, mine, dev * srows)], "gather", last, "ar_small_ag_start")
    adam_big("w_down", ffn_g[1], ffn_land[1])
    last = adam_big("w_out", g_out, out_land[0])
    (g_inT,), in_land = _exchange_wait(in_send, in_recv, [g_inT], in_land, "scatter", last, "rs_in_wait")
    last = adam_big("w_in", g_inT, in_land[0])
    _, (tot,) = _exchange_wait(sm_send, sm_recv, [mine], sm_land, "gather", last, "ar_small_ag_wait")

    half = lambda r, h, shape: tot[r, 512 * h:512 * h + 512].reshape(shape)
    blocks = lambda h: tot[32:96, 512 * h:512 * h + 512].reshape(64, 8, 64).transpose(1, 0, 2)[None]
    fcw_full = jnp.concatenate([tot[14:23].reshape(1, 3, D_FF), tot[23:32].reshape(1, 3, D_FF)], axis=2)
    g_small = {
        "g_mix": tot[0:1], "g_ffn": tot[1:2], "ffn_conv_b": tot[2:8].reshape(1, 2 * D_FF),
        "rec_conv_b": half(8, 0, (1, 512)), "b_rg": half(8, 1, (1, 8, 64)), "b_ig": half(9, 0, (1, 8, 64)),
        "lru_lambda": half(9, 1, (1, 512)), "g_attn_out": half(10, 0, (1, 512)), "g_rec_out": half(10, 1, (1, 512)),
        "q_norm_g": half(11, 0, (N_HEADS, HEAD_DIM)).sum(0)[None], "k_norm_g": half(11, 1, (N_HEADS, HEAD_DIM)).sum(0)[None],
        "w_rg": blocks(0), "w_ig": blocks(1),
        "rec_conv_w": lax.dynamic_slice(tot[12:14].reshape(1, 4, REC_W), (0, 0, 64 * dev), (1, 4, 64)),
        "ffn_conv_w": lax.dynamic_slice(fcw_full, (0, 0, 768 * dev), (1, 3, 768))}
    loss = 0.5 / D_MODEL * jnp.sum(tot[96])
    given = dict(rec_conv_w=rec_conv_w, ffn_conv_w=ffn_conv_w,g_mix=g_mix, q_norm_g=q_norm_g, k_norm_g=k_norm_g, rec_conv_b=rec_conv_b, w_rg=w_rg, b_rg=b_rg, w_ig=w_ig,
                 b_ig=b_ig, lru_lambda=lru_lambda, g_attn_out=g_attn_out, g_rec_out=g_rec_out, g_ffn=g_ffn, ffn_conv_b=ffn_conv_b)
    given_m = dict(rec_conv_w=m_rec_conv_w, ffn_conv_w=m_ffn_conv_w, g_mix=m_g_mix, q_norm_g=m_q_norm_g, k_norm_g=m_k_norm_g, rec_conv_b=m_rec_conv_b, w_rg=m_w_rg, b_rg=m_b_rg,
                   w_ig=m_w_ig, b_ig=m_b_ig, lru_lambda=m_lru_lambda, g_attn_out=m_g_attn_out, g_rec_out=m_g_rec_out,
                   g_ffn=m_g_ffn, ffn_conv_b=m_ffn_conv_b)
    given_v = dict(rec_conv_w=v_rec_conv_w, ffn_conv_w=v_ffn_conv_w, g_mix=v_g_mix, q_norm_g=v_q_norm_g, k_norm_g=v_k_norm_g, rec_conv_b=v_rec_conv_b, w_rg=v_w_rg, b_rg=v_b_rg,
                   w_ig=v_w_ig, b_ig=v_b_ig, lru_lambda=v_lru_lambda, g_attn_out=v_g_attn_out, g_rec_out=v_g_rec_out,
                   g_ffn=v_g_ffn, ffn_conv_b=v_ffn_conv_b)
    small = sorted(given)
    ds, m2s, v2s = _adam_small([given[k] for k in small], [g_small[k] for k in small], [given_m[k] for k in small],
                               [given_v[k] for k in small], "adam_small")
    small_out = {"grad": g_small, "delta": dict(zip(small, ds)), "new_m": dict(zip(small, m2s)), "new_v": dict(zip(small, v2s))}

    order = ("g_mix", "w_in", "q_norm_g", "k_norm_g", "rec_conv_w", "rec_conv_b", "w_rg", "b_rg", "w_ig", "b_ig",
             "lru_lambda", "g_attn_out", "g_rec_out", "w_out", "g_ffn", "w_up", "ffn_conv_w", "ffn_conv_b", "w_down")
    outs = [loss, grad_x.reshape(1, T, D_MODEL)]
    for kind in ("grad", "delta", "new_m", "new_v"):
        for name in order:
            outs.append(big_out[kind][name] if name in big_out[kind] else small_out[kind][name])
    return tuple(outs)
```

```python
import math

import numpy as np
import jax
import jax.numpy as jnp
from jax import lax
from jax.experimental import pallas as pl
from jax.experimental.pallas import tpu as pltpu

F32 = jnp.float32
BF16 = jnp.bfloat16

D_MODEL = 1024
HEAD_DIM = 64
ATTN_W = 512
REC_W = 512
N_HEADS = 8
D_FF = 3072
IN_W = 2560
REC_CONV = 4
FFN_CONV = 3
LRU_C = 8.0
ROPE_THETA = 10000.0
EPS = 1e-6
NEG_INF = -1e30
QBLK = 128
DILATIONS = (1, 4, 16)
N_DEV = 8
SMALL_ROWS = 128
ADAM_LR, ADAM_B1, ADAM_B2, ADAM_EPS, ADAM_WD, ADAM_STEP = 0.001, 0.9, 0.999, 1e-08, 0.01, 10
MESH = pl.DeviceIdType.MESH
ANY = pl.BlockSpec(memory_space=pl.ANY)


def _call(body, *, name, **kw):
    return pl.pallas_call(body, name=name, **kw)


def _params(*sem):
    return pltpu.CompilerParams(dimension_semantics=sem, vmem_limit_bytes=56 * 1024 * 1024)


_GELU_C = math.sqrt(2.0 / math.pi)
_GELU_A = 0.044715


def _gelu(x):
    return (0.5 * x) * (1.0 + jnp.tanh(x * (_GELU_C + (_GELU_C * _GELU_A) * (x * x))))


def _gelu_and_grad(x):
    x2 = x * x
    u = 1.0 + jnp.tanh(x * (_GELU_C + (_GELU_C * _GELU_A) * x2))
    hx = 0.5 * x
    return hx * u, 0.5 * u + (hx * ((2.0 - u) * u)) * (_GELU_C + (3.0 * _GELU_C * _GELU_A) * x2)


def _sigmoid(x):
    return 1.0 / (1.0 + jnp.exp(-x))


def _softplus_neg(lam):
    y = jnp.exp(-jnp.abs(lam))
    u = 1.0 + y
    log1p = jnp.where(u == 1.0, y, jnp.log(u) * y / jnp.where(u == 1.0, 1.0, u - 1.0))
    return jnp.maximum(-lam, 0.0) + log1p


_NN = (((1,), (0,)), ((), ()))
_NT = (((1,), (1,)), ((), ()))
_TN = (((0,), (0,)), ((), ()))


def _dot(a, b, dims=_NN):
    return lax.dot_general(a, b, dims, preferred_element_type=F32)


def _group_mean(v, bd):
    hi = v.astype(BF16)
    lo = (v - hi.astype(F32)).astype(BF16)
    w = bd.shape[0]
    return jnp.concatenate([_dot(hi[:, c:c + w], bd) + _dot(lo[:, c:c + w], bd) for c in range(0, v.shape[1], w)], axis=1)


def _rope_tables(pos_ref, invf_ref):
    ang = pos_ref[...].astype(F32) * invf_ref[:, :2 * HEAD_DIM]
    reps = invf_ref.shape[1] // (2 * HEAD_DIM)
    return jnp.tile(jnp.cos(ang), (1, reps)), jnp.tile(jnp.sin(ang), (1, reps))


def _shift_down(x, halo, s):
    rolled = pltpu.roll(x, s, 0)
    hr = pltpu.roll(halo, s, 0)
    row = lax.broadcasted_iota(jnp.int32, hr.shape, 0)
    first = jnp.where(row < s, hr, rolled[:8])
    return jnp.concatenate([first, rolled[8:]], axis=0)


def _shift_up(x, halo, s):
    n = x.shape[0]
    rolled = pltpu.roll(x, n - s, 0)
    hr = pltpu.roll(halo, 8 - s, 0)
    row = lax.broadcasted_iota(jnp.int32, hr.shape, 0)
    last = jnp.where(row >= 8 - s, hr, rolled[n - 8:])
    return jnp.concatenate([rolled[:n - 8], last], axis=0)


def _scan_fwd(a, u):
    n, w = a.shape
    a3, u3 = a.reshape(n // 8, 8, w), u.reshape(n // 8, 8, w)
    row = lax.broadcasted_iota(jnp.int32, a3.shape, 1)
    for s in (1, 2, 4):
        a_s = jnp.where(row < s, 1.0, pltpu.roll(a3, s, 1))
        u_s = jnp.where(row < s, 0.0, pltpu.roll(u3, s, 1))
        u3 = u3 + a3 * u_s
        a3 = a3 * a_s
    ps, hs = [a3[0]], [u3[0]]
    for k in range(1, n // 8):
        ps.append(a3[k] * ps[-1][7:8, :])
        hs.append(u3[k] + a3[k] * hs[-1][7:8, :])
    return jnp.concatenate(ps, axis=0), jnp.concatenate(hs, axis=0)


def _scan_bwd(b, v):
    n, w = b.shape
    b3, v3 = b.reshape(n // 8, 8, w), v.reshape(n // 8, 8, w)
    row = lax.broadcasted_iota(jnp.int32, b3.shape, 1)
    for s in (1, 2, 4):
        b_s = jnp.where(row >= 8 - s, 1.0, pltpu.roll(b3, 8 - s, 1))
        v_s = jnp.where(row >= 8 - s, 0.0, pltpu.roll(v3, 8 - s, 1))
        v3 = v3 + b3 * v_s
        b3 = b3 * b_s
    last = n // 8 - 1
    ps, gs = [b3[last]], [v3[last]]
    for k in range(last - 1, -1, -1):
        ps.append(b3[k] * ps[-1][0:1, :])
        gs.append(v3[k] + b3[k] * gs[-1][0:1, :])
    return jnp.concatenate(ps[::-1], axis=0), jnp.concatenate(gs[::-1], axis=0)


def _rot_half(y):
    n = y.shape[1]
    lane = lax.broadcasted_iota(jnp.int32, y.shape, 1) & (HEAD_DIM - 1)
    return jnp.where(lane < HEAD_DIM // 2, -pltpu.roll(y, n - HEAD_DIM // 2, 1), pltpu.roll(y, HEAD_DIM // 2, 1))


def _row_tile(r, cap=256):
    return max(t for t in range(16, cap + 1, 16) if r % t == 0)


def _all_gather(shards, name):
    na = len(shards)
    ms = [s.shape[0] for s in shards]

    def body(*refs):
        x_refs, out_refs = refs[:na], refs[na:2 * na]
        send_sems, recv_sems, local_sems = refs[2 * na:]
        x, y, c = lax.axis_index("x"), lax.axis_index("y"), lax.axis_index("c")
        me, sibling = (x, y, c), (x, y, 1 - c)
        chips = [(1 - x, y), (x, 1 - y), (1 - x, 1 - y)]

        def rows(a, px, py, pc):
            return out_refs[a].at[pl.ds(pl.multiple_of((4 * px + 2 * py + pc) * ms[a], 8), ms[a]), :]

        def copy(a, k, block, to, src=None):
            return pltpu.make_async_remote_copy(
                src_ref=rows(a, *block) if src is None else src, dst_ref=rows(a, *block),
                send_sem=send_sems.at[7 * a + k], recv_sem=recv_sems.at[7 * a + k], device_id=to, device_id_type=MESH)

        mine = [pltpu.make_async_copy(x_refs[a], rows(a, *me), local_sems.at[a]) for a in range(na)]
        first = []
        for a in range(na):
            mine[a].start()
            first.append(copy(a, 0, me, sibling, src=x_refs[a]))
            first += [copy(a, 1 + j, me, (*chip, c), src=x_refs[a]) for j, chip in enumerate(chips)]
        for cp in first:
            cp.start()
        passed = []
        for a in range(na):
            for j, chip in enumerate(chips):
                copy(a, 1 + j, (*chip, c), me).wait_recv()
                fw = copy(a, 4 + j, (*chip, c), sibling)
                fw.start()
                passed.append(fw)
        for a in range(na):
            copy(a, 0, sibling, me).wait_recv()
            for j, chip in enumerate(chips):
                copy(a, 4 + j, (*chip, 1 - c), me).wait_recv()
        for cp in first + passed:
            cp.wait_send()
        for cp in mine:
            cp.wait()

    return _call(
        body, name=name, out_shape=[jax.ShapeDtypeStruct((N_DEV * s.shape[0], s.shape[1]), s.dtype) for s in shards],
        in_specs=[ANY] * na, out_specs=[ANY] * na,
        scratch_shapes=[pltpu.SemaphoreType.DMA((7 * na,)), pltpu.SemaphoreType.DMA((7 * na,)),
                        pltpu.SemaphoreType.DMA((na,))],
    )(*shards)


HBM = pl.BlockSpec(memory_space=pltpu.HBM)
SEM = pl.BlockSpec(memory_space=pltpu.SEMAPHORE)
EFFECT = pltpu.SideEffectType.DATAFLOW_SIDE_EFFECTING
N_PEERS = N_DEV - 1


def _peer(k):
    x, y, c = lax.axis_index("x"), lax.axis_index("y"), lax.axis_index("c")
    b = k + 1
    flip = lambda v, bit: 1 - v if bit else v
    return flip(x, b & 4), flip(y, b & 2), flip(c, b & 1)


def _in_hbm(a):
    return pltpu.with_memory_space_constraint(a, pltpu.HBM)


def _split_copy_descr(na, kind, src_refs, land_refs, send_sems, recv_sems):
    x, y, c = lax.axis_index("x"), lax.axis_index("y"), lax.axis_index("c")
    me = 4 * x + 2 * y + c
    copies = []
    for a in range(na):
        for k in range(N_PEERS):
            px, py, pc = _peer(k)
            if kind == "gather":
                m = src_refs[a].shape[0]
                src, dst = src_refs[a], land_refs[a].at[pl.ds(pl.multiple_of(me * m, 8), m), :]
            else:
                src, dst = src_refs[a].at[4 * px + 2 * py + pc], land_refs[a].at[k]
            copies.append(pltpu.make_async_remote_copy(
                src_ref=src, dst_ref=dst, send_sem=send_sems.at[N_PEERS * a + k], recv_sem=recv_sems.at[N_PEERS * a + k],
                device_id=(px, py, pc), device_id_type=MESH))
    return copies


def _landing(shape, dtype, own=None, at=None):
    buf = lax.empty(shape, dtype)
    return buf if own is None else lax.dynamic_update_slice(buf, own, (at, 0))


def _exchange_start(srcs, lands, kind, after, name):
    na = len(srcs)
    land_shapes = [l.shape for l in lands]

    def body(*refs):
        src_refs, land_refs = refs[:na], refs[na:2 * na]
        send_sems, recv_sems = refs[2 * na + 1], refs[2 * na + 2]
        token = refs[-1]
        for cp in _split_copy_descr(na, kind, src_refs, land_refs, send_sems, recv_sems):
            cp.start()
        token[...] = jnp.zeros_like(token)

    lands = [_in_hbm(l) for l in lands]
    sem = pltpu.SemaphoreType.DMA((N_PEERS * na,))
    outs = _call(
        body, name=name,
        out_shape=[sem, sem] + [pltpu.HBM(s.shape, s.dtype) for s in srcs] + [pltpu.HBM(s, srcs[0].dtype) for s in land_shapes]
        + [jax.ShapeDtypeStruct((8, 128), F32)],
        in_specs=[HBM] * (2 * na) + [ANY], out_specs=[SEM, SEM] + [HBM] * (2 * na) + [pl.BlockSpec(memory_space=pltpu.VMEM)],
        input_output_aliases={i: 2 + i for i in range(2 * na)},
        compiler_params=pltpu.CompilerParams(has_side_effects=EFFECT),
    )(*[_in_hbm(s) for s in srcs], *lands, after)
    return outs[0], outs[1], outs[2:2 + na], outs[2 + na:2 + 2 * na], outs[-1]


def _exchange_wait(send_sems, recv_sems, srcs, lands, kind, after, name):
    na = len(srcs)

    def body(*refs):
        src_refs, land_refs = refs[:na], refs[na:2 * na]
        s_sems, r_sems = refs[2 * na], refs[2 * na + 1]
        for cp in _split_copy_descr(na, kind, src_refs, land_refs, s_sems, r_sems):
            cp.wait_send()
            cp.wait_recv()

    outs = _call(
        body, name=name, out_shape=[pltpu.HBM(s.shape, s.dtype) for s in srcs] + [pltpu.HBM(l.shape, l.dtype) for l in lands],
        in_specs=[HBM] * (2 * na) + [SEM, SEM, ANY], out_specs=[HBM] * (2 * na),
        input_output_aliases={i: i for i in range(2 * na)},
        compiler_params=pltpu.CompilerParams(has_side_effects=EFFECT),
    )(*srcs, *lands, send_sems, recv_sems, after)
    return outs[:na], outs[na:]


def _mm(a, b, mode, out_dtype, name, add=None, tm=1024, tn=1024, tk=1024, into=None, o_rows=None, o_moff=0,
        loss_target=None):
    if mode == "tn":
        K, M = a.shape
    else:
        M, K = a.shape
    N = b.shape[0] if mode == "nt" else b.shape[1]
    tm, tn, tk = min(tm, M), min(tn, N), min(tk, K)
    assert M % tm == 0 and N % tn == 0 and K % tk == 0, (name, M, N, K)
    nk = K // tk
    if mode == "nn":
        a_spec = pl.BlockSpec((tm, tk), lambda i, j, kk: (i, kk))
        b_spec, dims = pl.BlockSpec((tk, tn), lambda i, j, kk: (kk, j)), _NN
    elif mode == "nt":
        a_spec = pl.BlockSpec((tm, tk), lambda i, j, kk: (i, kk))
        b_spec, dims = pl.BlockSpec((tn, tk), lambda i, j, kk: (j, kk)), _NT
    else:
        a_spec = pl.BlockSpec((tk, tm), lambda i, j, kk: (kk, i))
        b_spec, dims = pl.BlockSpec((tk, tn), lambda i, j, kk: (kk, j)), _TN
    o_spec = pl.BlockSpec((tm, tn), lambda i, j, kk: (i + o_moff, j))
    has_add, has_into, has_loss = add is not None, into is not None, loss_target is not None
    assert not has_loss or (has_add and tn == N and not has_into)
    n_in = 2 + has_add + has_loss + has_into

    def body(*refs):
        a_ref, b_ref = refs[0], refs[1]
        add_ref = refs[2] if has_add else None
        outs = refs[n_in:]

        def finish(r):
            if has_add:
                r = r + add_ref[...]
            if has_loss:
                e = r - refs[3][...]
                dy = e * (1.0 / N)
                outs[0][...] = dy
                outs[1][...] = dy.astype(BF16)
                outs[2][...] = jnp.sum(e * e, axis=0, keepdims=True)[None]
            else:
                outs[0][...] = r.astype(out_dtype)

        if nk == 1:
            finish(_dot(a_ref[...], b_ref[...], dims))
        else:
            acc = refs[-1]
            kk = pl.program_id(2)

            @pl.when(kk == 0)
            def _():
                acc[...] = _dot(a_ref[...], b_ref[...], dims)

            @pl.when((kk > 0) & (kk < nk - 1))
            def _():
                acc[...] += _dot(a_ref[...], b_ref[...], dims)

            @pl.when(kk == nk - 1)
            def _():
                finish(acc[...] + _dot(a_ref[...], b_ref[...], dims))

    tile = pl.BlockSpec((tm, tn), lambda i, j, kk: (i, j))
    ins = [a, b] + ([add] if has_add else []) + ([loss_target] if has_loss else []) + ([into] if has_into else [])
    specs = [a_spec, b_spec] + [tile] * (has_add + has_loss) + ([ANY] if has_into else [])
    rows = into.shape[0] if has_into else (o_rows if o_rows is not None else M)
    if has_loss:
        out_specs = [tile, tile, pl.BlockSpec((1, 1, N), lambda i, j, kk: (i, 0, 0))]
        out_shape = [jax.ShapeDtypeStruct((M, N), F32), jax.ShapeDtypeStruct((M, N), BF16), jax.ShapeDtypeStruct((M // tm, 1, N), F32)]
    else:
        out_specs, out_shape = o_spec, jax.ShapeDtypeStruct((rows, N), out_dtype)
    return _call(
        body, name=name, grid=(M // tm, N // tn, nk), in_specs=specs, out_specs=out_specs, out_shape=out_shape,
        scratch_shapes=[pltpu.VMEM((tm, tn), F32)] if nk > 1 else [],
        input_output_aliases={len(ins) - 1: 0} if has_into else {},
        compiler_params=_params("parallel", "parallel", "arbitrary"),
    )(*ins)


def _mm_norm_bwd(parts, b, x, resid, g, name, tm=512, tk=512):
    T, N = x.shape
    counts = [p.shape[1] // tk for p in parts]
    starts = [sum(counts[:i]) for i in range(len(parts))]
    nsteps = sum(counts)
    assert all(p.shape[1] % tk == 0 for p in parts) and b.shape == (nsteps * tk, N)
    npart = len(parts)

    def body(*refs):
        a_refs, b_ref, x_ref, res_ref, g_ref = refs[:npart], refs[npart], refs[npart + 1], refs[npart + 2], refs[npart + 3]
        dx_ref, dxb_ref, dg_ref, acc = refs[npart + 4:]
        i, s = pl.program_id(0), pl.program_id(1)

        @pl.when((i == 0) & (s == 0))
        def _():
            dg_ref[...] = jnp.zeros_like(dg_ref)

        for p in range(npart):
            @pl.when((s >= starts[p]) & (s < starts[p] + counts[p]))
            def _(p=p):
                d = _dot(a_refs[p][...], b_ref[...])

                @pl.when(s == 0)
                def _():
                    acc[...] = d

                @pl.when(s > 0)
                def _():
                    acc[...] += d

        @pl.when(s == nsteps - 1)
        def _():
            xv, dhv = x_ref[...], acc[...]
            r = lax.rsqrt(jnp.mean(xv * xv, axis=-1, keepdims=True) + EPS)
            gd = dhv * g_ref[...]
            m = jnp.mean(gd * xv, axis=-1, keepdims=True)
            dx = res_ref[...] + r * gd - xv * (r * r * r) * m
            dx_ref[...] = dx
            dxb_ref[...] = dx.astype(BF16)
            dg_ref[...] += jnp.sum(dhv * xv * r, axis=0, keepdims=True)

    a_specs = [pl.BlockSpec((tm, tk), lambda i, s, st=st, c=c: (i, jnp.clip(s - st, 0, c - 1))) for st, c in zip(starts, counts)]
    row = pl.BlockSpec((tm, N), lambda i, s: (i, 0))
    vec = pl.BlockSpec((1, N), lambda i, s: (0, 0))
    return _call(
        body, name=name, grid=(T // tm, nsteps),
        in_specs=a_specs + [pl.BlockSpec((tk, N), lambda i, s: (s, 0)), row, row, vec], out_specs=[row, row, vec],
        out_shape=[jax.ShapeDtypeStruct((T, N), F32), jax.ShapeDtypeStruct((T, N), BF16), jax.ShapeDtypeStruct((1, N), F32)],
        scratch_shapes=[pltpu.VMEM((tm, N), F32)], compiler_params=_params("arbitrary", "arbitrary"),
    )(*parts, b, x, resid, g)


def _norm_proj(x, g, wT, name, tm=1024, tn=1280):
    T, K = x.shape
    N = wT.shape[0]

    def body(x_ref, g_ref, w_ref, o_ref, h_ref):
        xv = x_ref[...]
        r = lax.rsqrt(jnp.mean(xv * xv, axis=-1, keepdims=True) + EPS)
        hv = (xv * r * g_ref[...]).astype(BF16)

        @pl.when(pl.program_id(1) == 0)
        def _():
            h_ref[...] = hv

        o_ref[...] = _dot(hv, w_ref[...], _NT)

    return _call(
        body, name=name, grid=(T // tm, N // tn),
        in_specs=[pl.BlockSpec((tm, K), lambda i, j: (i, 0)), pl.BlockSpec((1, K), lambda i, j: (0, 0)),
                  pl.BlockSpec((tn, K), lambda i, j: (j, 0))],
        out_specs=[pl.BlockSpec((tm, tn), lambda i, j: (i, j)), pl.BlockSpec((tm, K), lambda i, j: (i, 0))],
        out_shape=[jax.ShapeDtypeStruct((T, N), F32), jax.ShapeDtypeStruct((T, K), BF16)],
        compiler_params=_params("parallel", "arbitrary"),
    )(x, g, wT)


def _qk_prep(proj, pos, invf, qg, kg, bd, name, tm=512):
    T = proj.shape[0]

    def body(q_ref, k_ref, pos_ref, invf_ref, qg_ref, kg_ref, bd_ref, qo_ref, ko_ref):
        cos, sin = _rope_tables(pos_ref, invf_ref)

        def prep(xv, gv, scale):
            r = lax.rsqrt(_group_mean(xv * xv, bd_ref[...]) + EPS)
            yv = xv * r * gv
            return ((yv * cos + _rot_half(yv) * sin) * scale).astype(BF16).astype(F32)

        qo_ref[...] = prep(q_ref[...], qg_ref[...], HEAD_DIM ** -0.5)
        ko_ref[...] = prep(k_ref[...], kg_ref[...], 1.0)

    col = lambda j: pl.BlockSpec((tm, ATTN_W), lambda i, j=j: (i, j))
    vec = pl.BlockSpec((1, ATTN_W), lambda i: (0, 0))
    out = pl.BlockSpec((tm, ATTN_W), lambda i: (i, 0))
    return _call(
        body, name=name, grid=(T // tm,),
        in_specs=[col(0), col(1), pl.BlockSpec((tm, 1), lambda i: (i, 0)), vec, vec, vec,
                  pl.BlockSpec((2 * HEAD_DIM, 2 * HEAD_DIM), lambda i: (0, 0))],
        out_specs=[out, out], out_shape=[jax.ShapeDtypeStruct((T, ATTN_W), F32)] * 2,
        compiler_params=_params("parallel"),
    )(proj, proj, pos, invf, qg, kg, bd)


def _qk_prep_bwd(proj, dqh, dkh, dv, pos, invf, qg, kg, bd, name, tm=512):
    T = proj.shape[0]

    def body(q_ref, k_ref, dq_ref, dk_ref, dv_ref, pos_ref, invf_ref, qg_ref, kg_ref, bd_ref, o_ref, gq_ref, gk_ref):
        @pl.when(pl.program_id(0) == 0)
        def _():
            gq_ref[...] = jnp.zeros_like(gq_ref)
            gk_ref[...] = jnp.zeros_like(gk_ref)

        cos, sin = _rope_tables(pos_ref, invf_ref)

        def back(xv, gv, dz, scale):
            dz = dz * scale
            dy = dz * cos - _rot_half(dz * sin)
            r = lax.rsqrt(_group_mean(xv * xv, bd_ref[...]) + EPS)
            gd = dy * gv
            m = _group_mean(gd * xv, bd_ref[...])
            dx = r * gd - xv * (r * r * r) * m
            return dx, jnp.sum(dy * xv * r, axis=0, keepdims=True)

        dxq, gs = back(q_ref[...], qg_ref[...], dq_ref[...], HEAD_DIM ** -0.5)
        gq_ref[...] += gs
        dxk, gs = back(k_ref[...], kg_ref[...], dk_ref[...], 1.0)
        gk_ref[...] += gs
        o_ref[...] = jnp.concatenate([dxq.astype(BF16), dxk.astype(BF16), dv_ref[...].astype(BF16)], axis=1)

    col = lambda j: pl.BlockSpec((tm, ATTN_W), lambda i, j=j: (i, j))
    row = pl.BlockSpec((tm, ATTN_W), lambda i: (i, 0))
    vec = pl.BlockSpec((1, ATTN_W), lambda i: (0, 0))
    return _call(
        body, name=name, grid=(T // tm,),
        in_specs=[col(0), col(1), row, row, row, pl.BlockSpec((tm, 1), lambda i: (i, 0)), vec, vec, vec,
                  pl.BlockSpec((2 * HEAD_DIM, 2 * HEAD_DIM), lambda i: (0, 0))],
        out_specs=[pl.BlockSpec((tm, 3 * ATTN_W), lambda i: (i, 0)), vec, vec],
        out_shape=[jax.ShapeDtypeStruct((T, 3 * ATTN_W), BF16)] + [jax.ShapeDtypeStruct((1, ATTN_W), F32)] * 2,
        compiler_params=_params("arbitrary"),
    )(proj, proj, dqh, dkh, dv, pos, invf, qg, kg, bd)


def _ld(ref, start, size, dil):
    return ref[pl.ds(start, size), :] if dil == 1 else ref[pl.ds(start, size, stride=dil), :]


def _st(ref, start, size, dil, val):
    if dil == 1:
        ref[pl.ds(start, size), :] = val
    else:
        ref[pl.ds(start, size, stride=dil), :] = val


def _attn_geometry(T, dil):
    nb = T // dil // QBLK
    if nb == 2:
        return 1, 2 * QBLK, 2 * QBLK
    return nb, QBLK, (2 * QBLK if nb >= 2 else QBLK)


ATTN_UNROLL = 4


def _attn_unit(j, u, dil, nit):
    return ATTN_UNROLL * j + u if dil >= ATTN_UNROLL else j + u * (nit // ATTN_UNROLL)


def _attn_block(it, dil, qb, kw):
    c, n = it & (dil - 1), lax.shift_right_logical(it, dil.bit_length() - 1)
    sq = n * (qb * dil) + c
    sk = jnp.maximum(n - (kw // qb - 1), 0) * (qb * dil) + c
    qi = lax.broadcasted_iota(jnp.int32, (2 * qb, kw), 0) & (qb - 1)
    kj = lax.broadcasted_iota(jnp.int32, (2 * qb, kw), 1)
    rel = jnp.where(n > 0, kw - qb, 0) + qi - kj
    return sq, sk, (rel >= 0) & (rel <= QBLK)


def _stack_heads(xv, head0):
    z = jnp.zeros_like(xv)
    return jnp.concatenate([jnp.where(head0, xv, z), jnp.where(head0, z, xv)], axis=0)


def _unstack_heads(x2, head0):
    qb = x2.shape[0] // 2
    return jnp.where(head0, x2[:qb], x2[qb:])


def _attn_fwd(qf, kf, proj, name):
    T = qf.shape[0]

    def body(q_ref, k_ref, v_ref, o_ref, lse_ref):
        for bi, dil in enumerate(DILATIONS):
            nb, qb, kw = _attn_geometry(T, dil)
            nit = nb * dil
            head0 = lax.broadcasted_iota(jnp.int32, (qb, 2 * HEAD_DIM), 1) < HEAD_DIM

            def step(j, carry, bi=bi, dil=dil, qb=qb, kw=kw, nit=nit, head0=head0):
                units = []
                for u in range(ATTN_UNROLL):
                    sq, sk, ok = _attn_block(_attn_unit(j, u, dil, nit), dil, qb, kw)
                    old = (_ld(o_ref, sq, qb, dil), _ld(lse_ref, sq, qb, dil)) if bi > 0 else None
                    units.append((sq, ok, _ld(q_ref, sq, qb, dil).astype(BF16), _ld(k_ref, sk, kw, dil).astype(BF16),
                                  _ld(v_ref, sk, kw, dil).astype(BF16), old))
                results = []
                for sq, ok, qv, kv, vv, old in units:
                    s = jnp.where(ok, _dot(_stack_heads(qv, head0), kv, _NT), NEG_INF)
                    m = jnp.max(s, axis=-1, keepdims=True)
                    p = jnp.exp(s - m).astype(BF16)
                    acc = _dot(p, jnp.concatenate([vv, jnp.ones_like(vv)], axis=1))
                    l = acc[:, 2 * HEAD_DIM:]
                    o_new = _unstack_heads(acc[:, :2 * HEAD_DIM] / l, head0)
                    l_new = _unstack_heads(m + jnp.log(l), head0)
                    if bi > 0:
                        o_old, l_old = old
                        mx = jnp.maximum(l_old, l_new)
                        e0, e1 = jnp.exp(l_old - mx), jnp.exp(l_new - mx)
                        z = e0 + e1
                        o_new = (e0 * o_old + e1 * o_new) / z
                        l_new = mx + jnp.log(z)
                    results.append((sq, o_new, l_new))
                for sq, o_new, l_new in results:
                    _st(o_ref, sq, qb, dil, o_new)
                    _st(lse_ref, sq, qb, dil, l_new)
                return carry

            lax.fori_loop(0, nit // ATTN_UNROLL, step, 0)

    blk = lambda off: pl.BlockSpec((T, 2 * HEAD_DIM), lambda hp, off=off: (0, off + hp))
    return _call(
        body, name=name, grid=(4,), in_specs=[blk(0), blk(0), blk(8)], out_specs=[blk(0), blk(0)],
        out_shape=[jax.ShapeDtypeStruct((T, ATTN_W), F32)] * 2, compiler_params=_params("parallel"),
    )(qf, kf, proj)


def _attn_bwd(qf, kf, proj, do, lse, delta, name):
    T = qf.shape[0]

    def body(q_ref, k_ref, v_ref, do_ref, lse_ref, dl_ref, dq_ref, dk_ref, dv_ref):
        for ref in (dq_ref, dk_ref, dv_ref):
            ref[...] = jnp.zeros_like(ref)
        for dil in DILATIONS:
            nb, qb, kw = _attn_geometry(T, dil)
            nit = nb * dil
            head0 = lax.broadcasted_iota(jnp.int32, (qb, 2 * HEAD_DIM), 1) < HEAD_DIM

            def step(j, carry, dil=dil, qb=qb, kw=kw, nit=nit, head0=head0):
                units = []
                for u in range(ATTN_UNROLL):
                    sq, sk, ok = _attn_block(_attn_unit(j, u, dil, nit), dil, qb, kw)
                    lsev, dlv = _ld(lse_ref, sq, qb, dil), _ld(dl_ref, sq, qb, dil)
                    units.append((sq, sk, ok, _ld(q_ref, sq, qb, dil).astype(BF16), _ld(do_ref, sq, qb, dil).astype(BF16),
                                  jnp.concatenate([lsev[:, 0:1], lsev[:, HEAD_DIM:HEAD_DIM + 1]], axis=0),
                                  jnp.concatenate([dlv[:, 0:1], dlv[:, HEAD_DIM:HEAD_DIM + 1]], axis=0),
                                  _ld(k_ref, sk, kw, dil).astype(BF16), _ld(v_ref, sk, kw, dil).astype(BF16),
                                  _ld(dq_ref, sq, qb, dil), _ld(dk_ref, sk, kw, dil), _ld(dv_ref, sk, kw, dil)))
                results = []
                for sq, sk, ok, qv, dov, lse2, dl2, kv, vv, dq0, dk0, dv0 in units:
                    q2, do2 = _stack_heads(qv, head0), _stack_heads(dov, head0)
                    p = jnp.where(ok, jnp.exp(_dot(q2, kv, _NT) - lse2), 0.0)
                    ds = (p * (_dot(do2, vv, _NT) - dl2)).astype(BF16)
                    results.append((sq, sk, dq0 + _unstack_heads(_dot(ds, kv), head0),
                                    dk0 + _dot(ds, q2, _TN), dv0 + _dot(p.astype(BF16), do2, _TN)))
                for sq, sk, dq, dk, dv in results:
                    _st(dq_ref, sq, qb, dil, dq)
                    _st(dk_ref, sk, kw, dil, dk)
                    _st(dv_ref, sk, kw, dil, dv)
                return carry

            lax.fori_loop(0, nit // ATTN_UNROLL, step, 0)

    blk = lambda off: pl.BlockSpec((T, 2 * HEAD_DIM), lambda hp, off=off: (0, off + hp))
    return _call(
        body, name=name, grid=(4,), in_specs=[blk(0), blk(0), blk(8), blk(0), blk(0), blk(0)], out_specs=[blk(0)] * 3,
        out_shape=[jax.ShapeDtypeStruct((T, ATTN_W), F32)] * 3, compiler_params=_params("parallel"),
    )(qf, kf, proj, do, lse, delta)


def _attn_norm(attn, g, name, tm=512):
    T = attn.shape[0]

    def body(a_ref, g_ref, o_ref):
        av = a_ref[...]
        r = lax.rsqrt(jnp.mean(av * av, axis=-1, keepdims=True) + EPS)
        o_ref[...] = (av * r * g_ref[...]).astype(BF16)

    row = pl.BlockSpec((tm, ATTN_W), lambda i: (i, 0))
    return _call(
        body, name=name, grid=(T // tm,), in_specs=[row, pl.BlockSpec((1, ATTN_W), lambda i: (0, 0))], out_specs=row,
        out_shape=jax.ShapeDtypeStruct((T, 2 * ATTN_W), BF16), compiler_params=_params("parallel"),
    )(attn, g)


def _attn_norm_bwd(dmix, attn, g, bd, name, tm=512):
    T = attn.shape[0]

    def body(d_ref, a_ref, g_ref, bd_ref, do_ref, dl_ref, dg_ref):
        @pl.when(pl.program_id(0) == 0)
        def _():
            dg_ref[...] = jnp.zeros_like(dg_ref)

        dy, av = d_ref[...], a_ref[...]
        r = lax.rsqrt(jnp.mean(av * av, axis=-1, keepdims=True) + EPS)
        gd = dy * g_ref[...]
        m = jnp.mean(gd * av, axis=-1, keepdims=True)
        da = r * gd - av * (r * r * r) * m
        do_ref[...] = da
        dl_ref[...] = _group_mean(da * av, bd_ref[...]) * float(HEAD_DIM)
        dg_ref[...] += jnp.sum(dy * av * r, axis=0, keepdims=True)

    row = pl.BlockSpec((tm, ATTN_W), lambda i: (i, 0))
    vec = pl.BlockSpec((1, ATTN_W), lambda i: (0, 0))
    return _call(
        body, name=name, grid=(T // tm,),
        in_specs=[row, row, vec, pl.BlockSpec((2 * HEAD_DIM, 2 * HEAD_DIM), lambda i: (0, 0))], out_specs=[row, row, vec],
        out_shape=[jax.ShapeDtypeStruct((T, ATTN_W), F32)] * 2 + [jax.ShapeDtypeStruct((1, ATTN_W), F32)],
        compiler_params=_params("arbitrary"),
    )(dmix, attn, g, bd)


def _rec_gates(xc, wrg_ref, wig_ref, brg_ref, big_ref, lam_ref):
    xb = xc.astype(BF16)
    r = _sigmoid(_dot(xb, wrg_ref[...]) + brg_ref[...])
    ig = _sigmoid(_dot(xb, wig_ref[...]) + big_ref[...])
    sp = _softplus_neg(lam_ref[...])
    log_a = -LRU_C * r * sp
    a = jnp.exp(log_a)
    th = jnp.tanh(log_a)
    mult = jnp.sqrt(-2.0 * th / (1.0 - th))
    return xb, r, ig, sp, a, mult


def _rec_fwd(proj, mix, cw, cb, wrg, wig, brg, big, lam, g, name, tm=512):
    T = proj.shape[0]
    hb = tm // 8

    def body(xr_ref, halo_ref, gr_ref, cw_ref, cb_ref, wrg_ref, wig_ref, brg_ref, big_ref, lam_ref, g_ref, mix_ref,
             xc_ref, h_ref, out_ref, carry):
        i = pl.program_id(0)

        @pl.when(i == 0)
        def _():
            carry[...] = jnp.zeros_like(carry)

        xr = xr_ref[...]
        halo = jnp.where(i > 0, halo_ref[...], 0.0)
        xc = cb_ref[...] + cw_ref[3:4, :] * xr
        for s in range(1, REC_CONV):
            xc = xc + cw_ref[3 - s:4 - s, :] * _shift_down(xr, halo, s)
        xc_ref[...] = xc
        _, _, ig, _, a, mult = _rec_gates(xc, wrg_ref, wig_ref, brg_ref, big_ref, lam_ref)
        pa, hl = _scan_fwd(a, mult * (ig * xc))
        h = hl + pa * carry[0:1, :]
        h_ref[...] = h
        carry[0:1, :] = h_ref[pl.ds(tm - 1, 1), :]
        hg = h * _gelu(gr_ref[...])
        r = lax.rsqrt(jnp.mean(hg * hg, axis=-1, keepdims=True) + EPS)
        out_ref[...] = (hg * r * g_ref[...]).astype(BF16)

    vec = pl.BlockSpec((1, REC_W), lambda i: (0, 0))
    row = pl.BlockSpec((tm, REC_W), lambda i: (i, 0))
    mat = pl.BlockSpec((REC_W, REC_W), lambda i: (0, 0))
    return _call(
        body, name=name, grid=(T // tm,),
        in_specs=[pl.BlockSpec((tm, REC_W), lambda i: (i, 3)),
                  pl.BlockSpec((8, REC_W), lambda i: (jnp.maximum(i * hb - 1, 0), 3)),
                  pl.BlockSpec((tm, REC_W), lambda i: (i, 4)),
                  pl.BlockSpec((8, REC_W), lambda i: (0, 0)), vec, mat, mat, vec, vec, vec, vec, ANY],
        out_specs=[row, row, pl.BlockSpec((tm, REC_W), lambda i: (i, 1))],
        out_shape=[jax.ShapeDtypeStruct((T, REC_W), F32)] * 2 + [jax.ShapeDtypeStruct(mix.shape, BF16)],
        scratch_shapes=[pltpu.VMEM((8, REC_W), F32)], input_output_aliases={11: 2},
        compiler_params=_params("arbitrary"),
    )(proj, proj, proj, cw, cb, wrg, wig, brg, big, lam, g, mix)


def _rec_bwd(dmix, proj, xc, h, cw, cb, wrg, wig, brg, big, lam, g, name, tm=512):
    T = proj.shape[0]
    nt = T // tm
    hb = tm // 8

    def body(d_ref, xr_ref, xhalo_ref, gr_ref, xc_ref, h_ref, hhalo_ref, cw_ref, cb_ref, wrg_ref, wig_ref, brg_ref,
             big_ref, lam_ref, g_ref,
             drec_ref, gcw_ref, gcb_ref, gwrg_ref, gwig_ref, gbrg_ref, gbig_ref, glam_ref, gg_ref,
             g_carry, a_first, dxc_next, gsp):
        i = pl.program_id(0)
        first_tile = i == nt - 1

        @pl.when(i == 0)
        def _():
            for ref in (gcw_ref, gcb_ref, gwrg_ref, gwig_ref, gbrg_ref, gbig_ref, glam_ref, gg_ref,
                        g_carry, a_first, dxc_next, gsp):
                ref[...] = jnp.zeros_like(ref)

        xr, xc, hv = xr_ref[...], xc_ref[...], h_ref[...]
        xhalo = jnp.where(first_tile, 0.0, xhalo_ref[...])
        hhalo = jnp.where(first_tile, 0.0, hhalo_ref[...])
        xb, r, ig, sp, a, mult = _rec_gates(xc, wrg_ref, wig_ref, brg_ref, big_ref, lam_ref)
        h_prev = _shift_down(hv, hhalo, 1)
        ge, dge = _gelu_and_grad(gr_ref[...])
        hg = hv * ge
        rr = lax.rsqrt(jnp.mean(hg * hg, axis=-1, keepdims=True) + EPS)
        dy = d_ref[...]
        gd = dy * g_ref[...]
        dhg = rr * gd - hg * (rr * rr * rr) * jnp.mean(gd * hg, axis=-1, keepdims=True)
        gg_ref[...] += jnp.sum(dy * hg * rr, axis=0, keepdims=True)
        dgr = (dhg * hv * dge).astype(BF16)
        dh = dhg * ge
        b = _shift_up(a, jnp.broadcast_to(a_first[0:1, :], (8, REC_W)), 1)
        pb, gl = _scan_bwd(b, dh)
        gs = gl + pb * g_carry[0:1, :]
        g_carry[0:1, :] = gs[0:1, :]
        a_first[0:1, :] = a[0:1, :]
        da = gs * h_prev
        dmult = gs * (ig * xc)
        di = gs * (mult * xc)
        dxc = gs * (mult * ig)
        dlog_a = da * a - dmult * (a * a) / mult
        gsp[...] += jnp.sum(dlog_a * (-LRU_C * r), axis=0, keepdims=True)
        dzr = (dlog_a * (-LRU_C * sp)) * (r * (1.0 - r))
        dzi = di * (ig * (1.0 - ig))
        dzr_b, dzi_b = dzr.astype(BF16), dzi.astype(BF16)
        dxc = dxc + _dot(dzr_b, wrg_ref[...], _NT) + _dot(dzi_b, wig_ref[...], _NT)
        gwrg_ref[...] += _dot(xb, dzr_b, _TN)
        gwig_ref[...] += _dot(xb, dzi_b, _TN)
        gbrg_ref[...] += jnp.sum(dzr, axis=0, keepdims=True)
        gbig_ref[...] += jnp.sum(dzi, axis=0, keepdims=True)
        nxt = dxc_next[...]
        dxr = cw_ref[3:4, :] * dxc
        gcw_ref[3:4, :] += jnp.sum(dxc * xr, axis=0, keepdims=True)
        for s in range(1, REC_CONV):
            dxr = dxr + cw_ref[3 - s:4 - s, :] * _shift_up(dxc, nxt, s)
            gcw_ref[3 - s:4 - s, :] += jnp.sum(dxc * _shift_down(xr, xhalo, s), axis=0, keepdims=True)
        gcb_ref[...] += jnp.sum(dxc, axis=0, keepdims=True)
        dxc_next[...] = dxc[:8]
        drec_ref[...] = jnp.concatenate([dxr.astype(BF16), dgr], axis=1)

        @pl.when(first_tile)
        def _():
            glam_ref[...] = gsp[...] * (-_sigmoid(-lam_ref[...]))

    rev = lambda i: nt - 1 - i
    vec = pl.BlockSpec((1, REC_W), lambda i: (0, 0))
    row = pl.BlockSpec((tm, REC_W), lambda i: (rev(i), 0))
    mat = pl.BlockSpec((REC_W, REC_W), lambda i: (0, 0))
    cwb = pl.BlockSpec((8, REC_W), lambda i: (0, 0))
    halo = lambda c: pl.BlockSpec((8, REC_W), lambda i, c=c: (jnp.maximum(rev(i) * hb - 1, 0), c))
    return _call(
        body, name=name, grid=(nt,),
        in_specs=[pl.BlockSpec((tm, REC_W), lambda i: (rev(i), 1)),
                  pl.BlockSpec((tm, REC_W), lambda i: (rev(i), 3)), halo(3),
                  pl.BlockSpec((tm, REC_W), lambda i: (rev(i), 4)),
                  row, row, halo(0), cwb, vec, mat, mat, vec, vec, vec, vec],
        out_specs=[pl.BlockSpec((tm, 2 * REC_W), lambda i: (rev(i), 0)), cwb, vec, mat, mat, vec, vec, vec, vec],
        out_shape=[jax.ShapeDtypeStruct((T, 2 * REC_W), BF16)]
        + [jax.ShapeDtypeStruct((8, REC_W), F32), jax.ShapeDtypeStruct((1, REC_W), F32)]
        + [jax.ShapeDtypeStruct((REC_W, REC_W), F32)] * 2 + [jax.ShapeDtypeStruct((1, REC_W), F32)] * 4,
        scratch_shapes=[pltpu.VMEM((8, REC_W), F32)] * 3 + [pltpu.VMEM((1, REC_W), F32)],
        compiler_params=_params("arbitrary"),
    )(dmix, proj, proj, proj, xc, h, h, cw, cb, wrg, wig, brg, big, lam, g)


def _ffn_conv(x_ext, cw_ref, cb_ref):
    return (cb_ref[...] + cw_ref[2:3, :] * x_ext + cw_ref[1:2, :] * pltpu.roll(x_ext, 1, 0)
            + cw_ref[0:1, :] * pltpu.roll(x_ext, 2, 0))


def _up_proj_act(x2, g, w_upT, cw, cb, name, tm=1024, tc=768):
    T = x2.shape[0]
    nc = D_FF // tc

    def body(x_ref, g_ref, wg_ref, wu_ref, cwg_ref, cwu_ref, cbg_ref, cbu_ref, act_ref, da_ref, db_ref, pg_ref, pu_ref,
             h_ref, hist_g, hist_u, hs):
        i, j = pl.program_id(0), pl.program_id(1)

        @pl.when(j == 0)
        def _():
            xv = x_ref[...]
            r = lax.rsqrt(jnp.mean(xv * xv, axis=-1, keepdims=True) + EPS)
            hs[...] = (xv * r * g_ref[...]).astype(BF16)
            h_ref[...] = hs[...]

        hv = hs[...]
        pg, pu = _dot(hv, wg_ref[...], _NT), _dot(hv, wu_ref[...], _NT)
        ge = jnp.concatenate([jnp.where(i > 0, hist_g[j], 0.0), pg], axis=0)
        ue = jnp.concatenate([jnp.where(i > 0, hist_u[j], 0.0), pu], axis=0)
        gel, dgel = _gelu_and_grad(_ffn_conv(ge, cwg_ref, cbg_ref)[8:])
        uu = _ffn_conv(ue, cwu_ref, cbu_ref)[8:]
        act_ref[...] = (gel * uu).astype(BF16)
        da_ref[...] = (uu * dgel).astype(BF16)
        db_ref[...] = gel.astype(BF16)
        pg_ref[...] = pg.astype(BF16)
        pu_ref[...] = pu.astype(BF16)
        hist_g[j] = pg[tm - 8:]
        hist_u[j] = pu[tm - 8:]

    tile = pl.BlockSpec((tm, tc), lambda i, j: (i, j))
    wsp = lambda off: pl.BlockSpec((tc, D_MODEL), lambda i, j, off=off: (j + off, 0))
    cws = lambda off: pl.BlockSpec((8, tc), lambda i, j, off=off: (0, j + off))
    cbs = lambda off: pl.BlockSpec((1, tc), lambda i, j, off=off: (0, j + off))
    return _call(
        body, name=name, grid=(T // tm, nc),
        in_specs=[pl.BlockSpec((tm, D_MODEL), lambda i, j: (i, 0)), pl.BlockSpec((1, D_MODEL), lambda i, j: (0, 0)),
                  wsp(0), wsp(nc), cws(0), cws(nc), cbs(0), cbs(nc)],
        out_specs=[tile] * 5 + [pl.BlockSpec((tm, D_MODEL), lambda i, j: (i, 0))],
        out_shape=[jax.ShapeDtypeStruct((T, D_FF), BF16)] * 5 + [jax.ShapeDtypeStruct((T, D_MODEL), BF16)],
        scratch_shapes=[pltpu.VMEM((nc, 8, tc), F32)] * 2 + [pltpu.VMEM((tm, D_MODEL), BF16)],
        compiler_params=_params("arbitrary", "arbitrary"),
    )(x2, g, w_upT, w_upT, cw, cw, cb, cb)


def _ffn_bwd(dyb, w_down, da, db, pg, pu, cw, name, tm=1024, tc=768):
    T, F = pg.shape
    nt = T // tm
    hb16 = tm // 16
    nc = F // tc
    n = tm + 8

    def body(dy_ref, dyn_ref, wd_ref, a_ref, an_ref, b_ref, bn_ref, g_ref, u_ref, cwg_ref, cwu_ref,
             dg_ref, du_ref, gcwg_ref, gcwu_ref, gcbg_ref, gcbu_ref):
        i = pl.program_id(1)
        last = i == nt - 1

        @pl.when(i == 0)
        def _():
            for ref in (gcwg_ref, gcwu_ref, gcbg_ref, gcbu_ref):
                ref[...] = jnp.zeros_like(ref)

        wd = wd_ref[...]
        dact_next = jnp.where(last, 0.0, _dot(dyn_ref[...], wd, _NT)[:8])
        de = jnp.concatenate([_dot(dy_ref[...], wd, _NT), dact_next], axis=0)
        ext = lambda t, nx: jnp.concatenate([t[...].astype(F32), nx[...].astype(F32)[:8]], axis=0)
        for dcv, x_ref, cw_ref, dx_ref, gcw_ref, gcb_ref in ((de * ext(a_ref, an_ref), g_ref, cwg_ref, dg_ref, gcwg_ref, gcbg_ref),
                                                               (de * ext(b_ref, bn_ref), u_ref, cwu_ref, du_ref, gcwu_ref, gcbu_ref)):
            s1, s2 = pltpu.roll(dcv, n - 1, 0), pltpu.roll(dcv, n - 2, 0)
            dx_ref[...] = (cw_ref[2:3, :] * dcv + cw_ref[1:2, :] * s1 + cw_ref[0:1, :] * s2)[:tm].astype(BF16)
            xv = x_ref[...].astype(F32)
            gcw_ref[2:3, :] += jnp.sum(xv * dcv[:tm], axis=0, keepdims=True)
            gcw_ref[1:2, :] += jnp.sum(xv * s1[:tm], axis=0, keepdims=True)
            gcw_ref[0:1, :] += jnp.sum(xv * s2[:tm], axis=0, keepdims=True)
            gcb_ref[...] += jnp.sum(dcv[:tm], axis=0, keepdims=True)

    tile = pl.BlockSpec((tm, tc), lambda j, i: (i, j))
    nxt = pl.BlockSpec((16, tc), lambda j, i: (jnp.minimum((i + 1) * hb16, nt * hb16 - 1), j))
    cws = lambda off: pl.BlockSpec((8, tc), lambda j, i, off=off: (0, j + off))
    cbs = pl.BlockSpec((1, tc), lambda j, i: (0, j))
    return _call(
        body, name=name, grid=(nc, nt),
        in_specs=[pl.BlockSpec((tm, D_MODEL), lambda j, i: (i, 0)),
                  pl.BlockSpec((16, D_MODEL), lambda j, i: (jnp.minimum((i + 1) * hb16, nt * hb16 - 1), 0)),
                  pl.BlockSpec((tc, D_MODEL), lambda j, i: (j, 0)), tile, nxt, tile, nxt, tile, tile, cws(0), cws(nc)],
        out_specs=[tile, tile, cws(0), cws(0), cbs, cbs],
        out_shape=[jax.ShapeDtypeStruct((T, F), BF16)] * 2 + [jax.ShapeDtypeStruct((8, F), F32)] * 2
        + [jax.ShapeDtypeStruct((1, F), F32)] * 2,
        compiler_params=_params("parallel", "arbitrary"),
    )(dyb, dyb, w_down, da, da, db, db, pg, pu, cw, cw)


def _adam_update(w, g, m, v):
    m2 = ADAM_B1 * m + (1.0 - ADAM_B1) * g
    v2 = ADAM_B2 * v + (1.0 - ADAM_B2) * (g * g)
    m_hat = m2 / (1.0 - ADAM_B1 ** ADAM_STEP)
    v_hat = v2 / (1.0 - ADAM_B2 ** ADAM_STEP)
    delta = -ADAM_LR * (m_hat / (jnp.sqrt(v_hat) + ADAM_EPS) + ADAM_WD * w)
    return delta, m2, v2


def _adam_sharded(p, r2, idx, w, m, v, name, transposed=False):
    r, n = p.shape[1:]
    nrecv = r2.shape[0]
    tr = (256 if r % 256 == 0 else r) if transposed else _row_tile(r)

    def body(c_ref, p_ref, r_ref, w_ref, m_ref, v_ref, g_ref, d_ref, m2_ref, v2_ref):
        g = p_ref[...].astype(F32)
        for k in range(nrecv):
            g = g + r_ref[k].astype(F32)
        if transposed:
            g = g.T
        g_ref[...] = g
        d_ref[...], m2_ref[...], v2_ref[...] = _adam_update(w_ref[...], g, m_ref[...], v_ref[...])

    blk = pl.BlockSpec((n, tr), lambda i, c_ref: (0, i)) if transposed else pl.BlockSpec((tr, n), lambda i, c_ref: (i, 0))
    spec = pltpu.PrefetchScalarGridSpec(
        num_scalar_prefetch=1, grid=(r // tr,),
        in_specs=[pl.BlockSpec((None, tr, n), lambda i, c_ref: (c_ref[0], i, 0)),
                  pl.BlockSpec((nrecv, tr, n), lambda i, c_ref: (0, i, 0)), blk, blk, blk],
        out_specs=[blk] * 4)
    return _call(body, name=name, grid_spec=spec, out_shape=[jax.ShapeDtypeStruct(w.shape, F32)] * 4,
                 compiler_params=_params("parallel"))(idx, p, r2, w, m, v)


def _sum_slabs(p, r2, idx, name):
    _, r, n = p.shape

    def body(c_ref, p_ref, r_ref, o_ref):
        acc = p_ref[...]
        for k in range(N_PEERS):
            acc = acc + r_ref[k]
        o_ref[...] = acc

    spec = pltpu.PrefetchScalarGridSpec(
        num_scalar_prefetch=1, grid=(1,),
        in_specs=[pl.BlockSpec((None, r, n), lambda i, c_ref: (c_ref[0], 0, 0)),
                  pl.BlockSpec((N_PEERS, r, n), lambda i, c_ref: (0, 0, 0))],
        out_specs=pl.BlockSpec((r, n), lambda i, c_ref: (0, 0)))
    return _call(body, name=name, grid_spec=spec, out_shape=jax.ShapeDtypeStruct((r, n), F32))(idx, p, r2)


def _adam_small(ws, gs, ms, vs, name):
    n = len(ws)

    def body(*refs):
        for i in range(n):
            d, m2, v2 = _adam_update(refs[i][...], refs[n + i][...], refs[2 * n + i][...], refs[3 * n + i][...])
            refs[4 * n + i][...] = d
            refs[5 * n + i][...] = m2
            refs[6 * n + i][...] = v2

    outs = _call(body, name=name, out_shape=[jax.ShapeDtypeStruct(w.shape, F32) for w in ws] * 3)(*ws, *gs, *ms, *vs)
    return outs[:n], outs[n:2 * n], outs[2 * n:]


def _pack_small_grads(full, halves, rcw, fcwg, fcwu, wrg, wig, lparts, name):
    nf, nh = len(full), len(halves)

    def body(*refs):
        o = refs[-1]
        o[...] = jnp.zeros_like(o)
        row = 0
        for r in refs[:nf]:
            for j in range(r.shape[1] // 1024):
                o[row:row + 1, :] = r[:, 1024 * j:1024 * (j + 1)]
                row += 1
        for k in range(0, nh, 2):
            o[row:row + 1, 0:512] = refs[nf + k][...]
            o[row:row + 1, 512:1024] = refs[nf + k + 1][...]
            row += 1
        rcw_ref, fg_ref, fu_ref, wrg_ref, wig_ref, l_ref = refs[nf + nh:nf + nh + 6]
        for k in range(2):
            o[row:row + 1, 0:512] = rcw_ref[2 * k:2 * k + 1, :]
            o[row:row + 1, 512:1024] = rcw_ref[2 * k + 1:2 * k + 2, :]
            row += 1
        for f_ref in (fg_ref, fu_ref):
            for k in range(FFN_CONV):
                for j in range(D_FF // 1024):
                    o[row:row + 1, :] = f_ref[k:k + 1, 1024 * j:1024 * (j + 1)]
                    row += 1
        assert row == 32
        for n in range(8):
            o[32:96, 64 * n:64 * n + 64] = wrg_ref[64 * n:64 * n + 64, 64 * n:64 * n + 64]
            o[32:96, 512 + 64 * n:512 + 64 * n + 64] = wig_ref[64 * n:64 * n + 64, 64 * n:64 * n + 64]
        o[96:97, :] = jnp.sum(l_ref[...], axis=0, keepdims=True)

    return _call(body, name=name, out_shape=jax.ShapeDtypeStruct((SMALL_ROWS, 1024), F32))(
        *full, *halves, rcw, fcwg, fcwu, wrg, wig, lparts)


def _block_diag(w):
    eye = jnp.eye(8, dtype=w.dtype)
    return (w[:, :, None, :] * eye[:, None, :, None]).reshape(512, 512)


def kernel(x, positions, g_mix, w_in, q_norm_g, k_norm_g, rec_conv_w, rec_conv_b, w_rg, b_rg, w_ig, b_ig, lru_lambda, g_attn_out, g_rec_out, w_out, g_ffn, w_up, ffn_conv_w, ffn_conv_b, w_down, loss_target, m_g_mix, m_w_in, m_q_norm_g, m_k_norm_g, m_rec_conv_w, m_rec_conv_b, m_w_rg, m_b_rg, m_w_ig, m_b_ig, m_lru_lambda, m_g_attn_out, m_g_rec_out, m_w_out, m_g_ffn, m_w_up, m_ffn_conv_w, m_ffn_conv_b, m_w_down, v_g_mix, v_w_in, v_q_norm_g, v_k_norm_g, v_rec_conv_w, v_rec_conv_b, v_w_rg, v_b_rg, v_w_ig, v_b_ig, v_lru_lambda, v_g_attn_out, v_g_rec_out, v_w_out, v_g_ffn, v_w_up, v_ffn_conv_w, v_ffn_conv_b, v_w_down):
    T = x.shape[1]
    ix, iy, ic = lax.axis_index("x"), lax.axis_index("y"), lax.axis_index("c")
    dev = 4 * ix + 2 * iy + ic
    xs = x.reshape(T, D_MODEL)
    tgt = loss_target.reshape(T, D_MODEL)
    pos = positions.reshape(T, 1)

    shards = {"w_in": (w_in[0], m_w_in[0], v_w_in[0]), "w_out": (w_out[0], m_w_out[0], v_w_out[0]),
              "w_up": (w_up[0], m_w_up[0], v_w_up[0]), "w_down": (w_down[0], m_w_down[0], v_w_down[0])}
    taps = jnp.concatenate([rec_conv_w.reshape(-1), ffn_conv_w.reshape(-1), jnp.zeros((4096 - 2560,), F32)]).reshape(8, 512)
    W_inT, taps_all = _all_gather([w_in[0].T.astype(BF16), taps], "ag_w_in")
    gather_landing = lambda s: _landing((N_DEV * s.shape[0], 1024), BF16, s, dev * s.shape[0])
    late = [w_out[0].astype(BF16), w_up[0].T.astype(BF16)]
    ag_send, ag_recv, late_thru, land_thru, ag_token = _exchange_start(
        late, [gather_landing(s) for s in late], "gather", taps_all, "ag_late_start")
    w_down_b = w_down[0].astype(BF16)
    down_landing = gather_landing(w_down_b)
    taps_all = taps_all.reshape(N_DEV, 4096)
    rcw = taps_all[:, :256].reshape(8, 4, 64).transpose(1, 0, 2).reshape(4, REC_W)
    fcw = taps_all[:, 256:2560].reshape(8, 3, 768).transpose(1, 0, 2).reshape(3, 2 * D_FF)
    rcw8 = jnp.pad(rcw, ((0, 4), (0, 0)))
    fcw8 = jnp.pad(fcw, ((0, 5), (0, 0)))
    fcb = ffn_conv_b.reshape(1, 2 * D_FF)

    half = HEAD_DIM // 2
    inv_freq = ROPE_THETA ** (-jnp.arange(half, dtype=F32) / half)
    invf = jnp.tile(inv_freq, 2 * N_HEADS).reshape(1, ATTN_W)
    bd = jnp.asarray(np.kron(np.eye(2), np.full((HEAD_DIM, HEAD_DIM), 1.0 / HEAD_DIM)), BF16)
    qg = jnp.tile(q_norm_g.reshape(HEAD_DIM), N_HEADS).reshape(1, ATTN_W)
    kg = jnp.tile(k_norm_g.reshape(HEAD_DIM), N_HEADS).reshape(1, ATTN_W)
    wrg_bd = _block_diag(w_rg[0]).astype(BF16)
    wig_bd = _block_diag(w_ig[0]).astype(BF16)
    brg, big = b_rg.reshape(1, REC_W), b_ig.reshape(1, REC_W)

    proj, h1 = _norm_proj(xs, g_mix + ag_token[0, 0], W_inT, "in_proj", tn=IN_W)
    qf, kf = _qk_prep(proj, pos, invf, qg, kg, bd, "qk_prep")
    attn, lse = _attn_fwd(qf, kf, proj, "attn_fwd")
    dn_send, dn_recv, dn_thru, dn_land, dn_token = _exchange_start(
        [w_down_b], [down_landing], "gather", attn, "ag_down_start")
    mix = _attn_norm(attn, g_attn_out + dn_token[0, 0], "attn_norm")
    xc, hstate, mix = _rec_fwd(proj, mix, rcw8, rec_conv_b, wrg_bd, wig_bd, brg, big, lru_lambda, g_rec_out, "rec_fwd")
    _, (W_out, W_upT) = _exchange_wait(ag_send, ag_recv, late_thru, land_thru, "gather", hstate, "ag_late_wait")
    x2 = _mm(mix, W_out, "nn", F32, "out_proj", add=xs)

    act, da, db, pg, pu, h2 = _up_proj_act(x2, g_ffn, W_upT, fcw8, fcb, "up_proj_act")
    _, (W_down,) = _exchange_wait(dn_send, dn_recv, dn_thru, dn_land, "gather", h2, "ag_down_wait")
    dy, dyb, lparts = _mm(act, W_down, "nn", F32, "down_proj_loss", add=x2, loss_target=tgt, tm=512, tk=D_FF)

    g_down = _mm(act, dyb, "tn", BF16, "g_w_down", tk=4096)
    dpg, dpu, g_fcwg, g_fcwu, g_fcbg, g_fcbu = _ffn_bwd(dyb, W_down, da, db, pg, pu, fcw8, "ffn_bwd")
    g_upT = _mm(dpg, h2, "tn", BF16, "g_w_up_gate", tk=4096, o_rows=2 * D_FF)
    g_upT = _mm(dpu, h2, "tn", BF16, "g_w_up_up", tk=4096, into=g_upT, o_moff=D_FF // 1024)
    ffn_g = [g_upT.reshape(N_DEV, 2 * D_FF // N_DEV, 1024), g_down.reshape(N_DEV, D_FF // N_DEV, 1024)]
    rs_send, rs_recv, ffn_g, ffn_land, rs_token = _exchange_start(
        ffn_g, [_landing((N_PEERS,) + g.shape[1:], BF16) for g in ffn_g], "scatter", dpu, "rs_ffn_start")
    dx2, dx2b, g_gffn = _mm_norm_bwd([dpg, dpu], W_upT, x2, dy, g_ffn + rs_token[0, 0], "d_h2_norm_bwd", tm=1024, tk=1536)

    dmix = _mm(dx2b, W_out, "nt", F32, "d_mix")
    g_out = _mm(mix, dx2b, "tn", BF16, "g_w_out", tk=4096).reshape(N_DEV, D_MODEL // N_DEV, 1024)
    out_send, out_recv, (g_out,), out_land, out_token = _exchange_start(
        [g_out], [_landing((N_PEERS,) + g_out.shape[1:], BF16)], "scatter", dmix, "rs_out_start")
    do, delta, g_gattn = _attn_norm_bwd(dmix, attn, g_attn_out + out_token[0, 0], bd, "attn_norm_bwd")
    dqh, dkh, dv = _attn_bwd(qf, kf, proj, do, lse, delta, "attn_bwd")
    dqkv, g_qg, g_kg = _qk_prep_bwd(proj, dqh, dkh, dv, pos, invf, qg, kg, bd, "qk_prep_bwd")
    (drec, g_rcw, g_rcb, g_wrg, g_wig, g_brg, g_big, g_lam, g_grec) = _rec_bwd(
        dmix, proj, xc, hstate, rcw8, rec_conv_b, wrg_bd, wig_bd, brg, big, lru_lambda, g_rec_out, "rec_bwd")
    g_inT = _mm(dqkv, h1, "tn", BF16, "g_w_in_qkv", tm=512, tk=4096, o_rows=IN_W)
    g_inT = _mm(drec, h1, "tn", BF16, "g_w_in_rec", tm=512, tk=4096, into=g_inT, o_moff=3 * ATTN_W // 512)
    g_inT = g_inT.reshape(N_DEV, IN_W // N_DEV, 1024)
    in_send, in_recv, (g_inT,), in_land, in_token = _exchange_start(
        [g_inT], [_landing((N_PEERS,) + g_inT.shape[1:], BF16)], "scatter", drec, "rs_in_start")
    grad_x, _, g_gmix = _mm_norm_bwd([dqkv, drec], W_inT, xs, dx2, g_mix + in_token[0, 0], "d_h1_norm_bwd", tm=1024, tk=512)

    flat = _pack_small_grads([g_gmix, g_gffn, g_fcbg, g_fcbu], [g_rcb, g_brg, g_big, g_lam, g_gattn, g_grec, g_qg, g_kg],
                             g_rcw, g_fcwg, g_fcwu, g_wrg, g_wig, lparts.reshape(-1, D_MODEL), "pack_small_grads")
    srows = SMALL_ROWS // N_DEV
    flat = flat.reshape(N_DEV, srows, 1024)
    sm_send, sm_recv, (flat,), sm_land, sm_token = _exchange_start(
        [flat], [_landing((N_PEERS, srows, 1024), F32)], "scatter", grad_x, "ar_small_rs_start")

    devi = jnp.reshape(dev, (1,)).astype(jnp.int32)
    ffn_g, ffn_land = _exchange_wait(rs_send, rs_recv, ffn_g, ffn_land, "scatter", sm_token, "rs_ffn_wait")
    (g_out,), out_land = _exchange_wait(out_send, out_recv, [g_out], out_land, "scatter", sm_token, "rs_out_wait")
    big_out = {"grad": {}, "delta": {}, "new_m": {}, "new_v": {}}

    def adam_big(nm, p, r):
        w_, m_, v_ = shards[nm]
        res = _adam_sharded(p, r, devi, w_, m_, v_, "adam_" + nm, transposed=nm in ("w_in", "w_up"))
        for kind, a in zip(("grad", "delta", "new_m", "new_v"), res):
            big_out[kind][nm] = a[None]
        return res[0]

    last = adam_big("w_up", ffn_g[0], ffn_land[0])
    (flat,), sm_land = _exchange_wait(sm_send, sm_recv, [flat], sm_land, "scatter", last, "ar_small_rs_wait")
    mine = _sum_slabs(flat, sm_land[0], devi, "sum_small_grads")
    sm_send, sm_recv, (mine,), sm_land, sm_token = _exchange_start(
        [mine], [_landing((SMALL_ROWS, 1024), F32, mine, dev * srows)], "gather", last, "ar_small_ag_start")
    adam_big("w_down", ffn_g[1], ffn_land[1])
    last = adam_big("w_out", g_out, out_land[0])
    _, (tot,) = _exchange_wait(sm_send, sm_recv, [mine], sm_land, "gather", last, "ar_small_ag_wait")
    (g_inT,), in_land = _exchange_wait(in_send, in_recv, [g_inT], in_land, "scatter", tot, "rs_in_wait")
    adam_big("w_in", g_inT, in_land[0])

    half = lambda r, h, shape: tot[r, 512 * h:512 * h + 512].reshape(shape)
    blocks = lambda h: tot[32:96, 512 * h:512 * h + 512].reshape(64, 8, 64).transpose(1, 0, 2)[None]
    fcw_full = jnp.concatenate([tot[14:23].reshape(1, 3, D_FF), tot[23:32].reshape(1, 3, D_FF)], axis=2)
    g_small = {
        "g_mix": tot[0:1], "g_ffn": tot[1:2], "ffn_conv_b": tot[2:8].reshape(1, 2 * D_FF),
        "rec_conv_b": half(8, 0, (1, 512)), "b_rg": half(8, 1, (1, 8, 64)), "b_ig": half(9, 0, (1, 8, 64)),
        "lru_lambda": half(9, 1, (1, 512)), "g_attn_out": half(10, 0, (1, 512)), "g_rec_out": half(10, 1, (1, 512)),
        "q_norm_g": half(11, 0, (N_HEADS, HEAD_DIM)).sum(0)[None], "k_norm_g": half(11, 1, (N_HEADS, HEAD_DIM)).sum(0)[None],
        "w_rg": blocks(0), "w_ig": blocks(1),
        "rec_conv_w": lax.dynamic_slice(tot[12:14].reshape(1, 4, REC_W), (0, 0, 64 * dev), (1, 4, 64)),
        "ffn_conv_w": lax.dynamic_slice(fcw_full, (0, 0, 768 * dev), (1, 3, 768))}
    loss = 0.5 / D_MODEL * jnp.sum(tot[96])
    given = dict(rec_conv_w=rec_conv_w, ffn_conv_w=ffn_conv_w,g_mix=g_mix, q_norm_g=q_norm_g, k_norm_g=k_norm_g, rec_conv_b=rec_conv_b, w_rg=w_rg, b_rg=b_rg, w_ig=w_ig,
                 b_ig=b_ig, lru_lambda=lru_lambda, g_attn_out=g_attn_out, g_rec_out=g_rec_out, g_ffn=g_ffn, ffn_conv_b=ffn_conv_b)
    given_m = dict(rec_conv_w=m_rec_conv_w, ffn_conv_w=m_ffn_conv_w, g_mix=m_g_mix, q_norm_g=m_q_norm_g, k_norm_g=m_k_norm_g, rec_conv_b=m_rec_conv_b, w_rg=m_w_rg, b_rg=m_b_rg,
                   w_ig=m_w_ig, b_ig=m_b_ig, lru_lambda=m_lru_lambda, g_attn_out=m_g_attn_out, g_rec_out=m_g_rec_out,
                   g_ffn=m_g_ffn, ffn_conv_b=m_ffn_conv_b)
    given_v = dict(rec_conv_w=v_rec_conv_w, ffn_conv_w=v_ffn_conv_w, g_mix=v_g_mix, q_norm_g=v_q_norm_g, k_norm_g=v_k_norm_g, rec_conv_b=v_rec_conv_b, w_rg=v_w_rg, b_rg=v_b_rg,
                   w_ig=v_w_ig, b_ig=v_b_ig, lru_lambda=v_lru_lambda, g_attn_out=v_g_attn_out, g_rec_out=v_g_rec_out,
                   g_ffn=v_g_ffn, ffn_conv_b=v_ffn_conv_b)
    small = sorted(given)
    ds, m2s, v2s = _adam_small([given[k] for k in small], [g_small[k] for k in small], [given_m[k] for k in small],
                               [given_v[k] for k in small], "adam_small")
    small_out = {"grad": g_small, "delta": dict(zip(small, ds)), "new_m": dict(zip(small, m2s)), "new_v": dict(zip(small, v2s))}

    order = ("g_mix", "w_in", "q_norm_g", "k_norm_g", "rec_conv_w", "rec_conv_b", "w_rg", "b_rg", "w_ig", "b_ig",
             "lru_lambda", "g_attn_out", "g_rec_out", "w_out", "g_ffn", "w_up", "ffn_conv_w", "ffn_conv_b", "w_down")
    outs = [loss, grad_x.reshape(1, T, D_MODEL)]
    for kind in ("grad", "delta", "new_m", "new_v"):
        for name in order:
            outs.append(big_out[kind][name] if name in big_out[kind] else small_out[kind][name])
    return tuple(outs)
```

```python
import math

import numpy as np
import jax
import jax.numpy as jnp
from jax import lax
from jax.experimental import pallas as pl
from jax.experimental.pallas import tpu as pltpu

F32 = jnp.float32
BF16 = jnp.bfloat16

D_MODEL = 1024
HEAD_DIM = 64
ATTN_W = 512
REC_W = 512
N_HEADS = 8
D_FF = 3072
IN_W = 2560
REC_CONV = 4
FFN_CONV = 3
LRU_C = 8.0
ROPE_THETA = 10000.0
EPS = 1e-6
NEG_INF = -1e30
QBLK = 128
DILATIONS = (1, 4, 16)
N_DEV = 8
SMALL_ROWS = 128
ADAM_LR, ADAM_B1, ADAM_B2, ADAM_EPS, ADAM_WD, ADAM_STEP = 0.001, 0.9, 0.999, 1e-08, 0.01, 10
MESH = pl.DeviceIdType.MESH
ANY = pl.BlockSpec(memory_space=pl.ANY)


def _call(body, *, name, **kw):
    return pl.pallas_call(body, name=name, **kw)


def _params(*sem):
    return pltpu.CompilerParams(dimension_semantics=sem, vmem_limit_bytes=56 * 1024 * 1024)


_GELU_C = math.sqrt(2.0 / math.pi)
_GELU_A = 0.044715


def _gelu(x):
    return (0.5 * x) * (1.0 + jnp.tanh(x * (_GELU_C + (_GELU_C * _GELU_A) * (x * x))))


def _gelu_and_grad(x):
    x2 = x * x
    u = 1.0 + jnp.tanh(x * (_GELU_C + (_GELU_C * _GELU_A) * x2))
    hx = 0.5 * x
    return hx * u, 0.5 * u + (hx * ((2.0 - u) * u)) * (_GELU_C + (3.0 * _GELU_C * _GELU_A) * x2)


def _sigmoid(x):
    return 1.0 / (1.0 + jnp.exp(-x))


def _softplus_neg(lam):
    y = jnp.exp(-jnp.abs(lam))
    u = 1.0 + y
    log1p = jnp.where(u == 1.0, y, jnp.log(u) * y / jnp.where(u == 1.0, 1.0, u - 1.0))
    return jnp.maximum(-lam, 0.0) + log1p


_NN = (((1,), (0,)), ((), ()))
_NT = (((1,), (1,)), ((), ()))
_TN = (((0,), (0,)), ((), ()))


def _dot(a, b, dims=_NN):
    return lax.dot_general(a, b, dims, preferred_element_type=F32)


def _group_mean(v, bd):
    hi = v.astype(BF16)
    lo = (v - hi.astype(F32)).astype(BF16)
    w = bd.shape[0]
    return jnp.concatenate([_dot(hi[:, c:c + w], bd) + _dot(lo[:, c:c + w], bd) for c in range(0, v.shape[1], w)], axis=1)


def _rope_tables(pos_ref, invf_ref):
    ang = pos_ref[...].astype(F32) * invf_ref[:, :2 * HEAD_DIM]
    reps = invf_ref.shape[1] // (2 * HEAD_DIM)
    return jnp.tile(jnp.cos(ang), (1, reps)), jnp.tile(jnp.sin(ang), (1, reps))


def _shift_down(x, halo, s):
    rolled = pltpu.roll(x, s, 0)
    hr = pltpu.roll(halo, s, 0)
    row = lax.broadcasted_iota(jnp.int32, hr.shape, 0)
    first = jnp.where(row < s, hr, rolled[:8])
    return jnp.concatenate([first, rolled[8:]], axis=0)


def _shift_up(x, halo, s):
    n = x.shape[0]
    rolled = pltpu.roll(x, n - s, 0)
    hr = pltpu.roll(halo, 8 - s, 0)
    row = lax.broadcasted_iota(jnp.int32, hr.shape, 0)
    last = jnp.where(row >= 8 - s, hr, rolled[n - 8:])
    return jnp.concatenate([rolled[:n - 8], last], axis=0)


def _scan_fwd(a, u):
    n, w = a.shape
    a3, u3 = a.reshape(n // 8, 8, w), u.reshape(n // 8, 8, w)
    row = lax.broadcasted_iota(jnp.int32, a3.shape, 1)
    for s in (1, 2, 4):
        a_s = jnp.where(row < s, 1.0, pltpu.roll(a3, s, 1))
        u_s = jnp.where(row < s, 0.0, pltpu.roll(u3, s, 1))
        u3 = u3 + a3 * u_s
        a3 = a3 * a_s
    ps, hs = [a3[0]], [u3[0]]
    for k in range(1, n // 8):
        ps.append(a3[k] * ps[-1][7:8, :])
        hs.append(u3[k] + a3[k] * hs[-1][7:8, :])
    return jnp.concatenate(ps, axis=0), jnp.concatenate(hs, axis=0)


def _scan_bwd(b, v):
    n, w = b.shape
    b3, v3 = b.reshape(n // 8, 8, w), v.reshape(n // 8, 8, w)
    row = lax.broadcasted_iota(jnp.int32, b3.shape, 1)
    for s in (1, 2, 4):
        b_s = jnp.where(row >= 8 - s, 1.0, pltpu.roll(b3, 8 - s, 1))
        v_s = jnp.where(row >= 8 - s, 0.0, pltpu.roll(v3, 8 - s, 1))
        v3 = v3 + b3 * v_s
        b3 = b3 * b_s
    last = n // 8 - 1
    ps, gs = [b3[last]], [v3[last]]
    for k in range(last - 1, -1, -1):
        ps.append(b3[k] * ps[-1][0:1, :])
        gs.append(v3[k] + b3[k] * gs[-1][0:1, :])
    return jnp.concatenate(ps[::-1], axis=0), jnp.concatenate(gs[::-1], axis=0)


def _rot_half(y):
    n = y.shape[1]
    lane = lax.broadcasted_iota(jnp.int32, y.shape, 1) & (HEAD_DIM - 1)
    return jnp.where(lane < HEAD_DIM // 2, -pltpu.roll(y, n - HEAD_DIM // 2, 1), pltpu.roll(y, HEAD_DIM // 2, 1))


def _row_tile(r, cap=256):
    return max(t for t in range(16, cap + 1, 16) if r % t == 0)


def _all_gather(shards, name):
    na = len(shards)
    ms = [s.shape[0] for s in shards]

    def body(*refs):
        x_refs, out_refs = refs[:na], refs[na:2 * na]
        send_sems, recv_sems, local_sems = refs[2 * na:]
        x, y, c = lax.axis_index("x"), lax.axis_index("y"), lax.axis_index("c")
        me, sibling = (x, y, c), (x, y, 1 - c)
        chips = [(1 - x, y), (x, 1 - y), (1 - x, 1 - y)]

        def rows(a, px, py, pc):
            return out_refs[a].at[pl.ds(pl.multiple_of((4 * px + 2 * py + pc) * ms[a], 8), ms[a]), :]

        def copy(a, k, block, to, src=None):
            return pltpu.make_async_remote_copy(
                src_ref=rows(a, *block) if src is None else src, dst_ref=rows(a, *block),
                send_sem=send_sems.at[7 * a + k], recv_sem=recv_sems.at[7 * a + k], device_id=to, device_id_type=MESH)

        mine = [pltpu.make_async_copy(x_refs[a], rows(a, *me), local_sems.at[a]) for a in range(na)]
        first = []
        for a in range(na):
            mine[a].start()
            first.append(copy(a, 0, me, sibling, src=x_refs[a]))
            first += [copy(a, 1 + j, me, (*chip, c), src=x_refs[a]) for j, chip in enumerate(chips)]
        for cp in first:
            cp.start()
        passed = []
        for a in range(na):
            for j, chip in enumerate(chips):
                copy(a, 1 + j, (*chip, c), me).wait_recv()
                fw = copy(a, 4 + j, (*chip, c), sibling)
                fw.start()
                passed.append(fw)
        for a in range(na):
            copy(a, 0, sibling, me).wait_recv()
            for j, chip in enumerate(chips):
                copy(a, 4 + j, (*chip, 1 - c), me).wait_recv()
        for cp in first + passed:
            cp.wait_send()
        for cp in mine:
            cp.wait()

    return _call(
        body, name=name, out_shape=[jax.ShapeDtypeStruct((N_DEV * s.shape[0], s.shape[1]), s.dtype) for s in shards],
        in_specs=[ANY] * na, out_specs=[ANY] * na,
        scratch_shapes=[pltpu.SemaphoreType.DMA((7 * na,)), pltpu.SemaphoreType.DMA((7 * na,)),
                        pltpu.SemaphoreType.DMA((na,))],
    )(*shards)


HBM = pl.BlockSpec(memory_space=pltpu.HBM)
SEM = pl.BlockSpec(memory_space=pltpu.SEMAPHORE)
EFFECT = pltpu.SideEffectType.DATAFLOW_SIDE_EFFECTING
N_PEERS = N_DEV - 1


def _peer(k):
    x, y, c = lax.axis_index("x"), lax.axis_index("y"), lax.axis_index("c")
    b = k + 1
    flip = lambda v, bit: 1 - v if bit else v
    return flip(x, b & 4), flip(y, b & 2), flip(c, b & 1)


def _in_hbm(a):
    return pltpu.with_memory_space_constraint(a, pltpu.HBM)


def _split_copy_descr(na, kind, src_refs, land_refs, send_sems, recv_sems):
    x, y, c = lax.axis_index("x"), lax.axis_index("y"), lax.axis_index("c")
    me = 4 * x + 2 * y + c
    copies = []
    for a in range(na):
        for k in range(N_PEERS):
            px, py, pc = _peer(k)
            if kind == "gather":
                m = src_refs[a].shape[0]
                src, dst = src_refs[a], land_refs[a].at[pl.ds(pl.multiple_of(me * m, 8), m), :]
            else:
                src, dst = src_refs[a].at[4 * px + 2 * py + pc], land_refs[a].at[k]
            copies.append(pltpu.make_async_remote_copy(
                src_ref=src, dst_ref=dst, send_sem=send_sems.at[N_PEERS * a + k], recv_sem=recv_sems.at[N_PEERS * a + k],
                device_id=(px, py, pc), device_id_type=MESH))
    return copies


def _landing(shape, dtype, own=None, at=None):
    buf = lax.empty(shape, dtype)
    return buf if own is None else lax.dynamic_update_slice(buf, own, (at, 0))


def _exchange_start(srcs, lands, kind, after, name):
    na = len(srcs)
    land_shapes = [l.shape for l in lands]

    def body(*refs):
        src_refs, land_refs = refs[:na], refs[na:2 * na]
        send_sems, recv_sems = refs[2 * na + 1], refs[2 * na + 2]
        token = refs[-1]
        for cp in _split_copy_descr(na, kind, src_refs, land_refs, send_sems, recv_sems):
            cp.start()
        token[...] = jnp.zeros_like(token)

    lands = [_in_hbm(l) for l in lands]
    sem = pltpu.SemaphoreType.DMA((N_PEERS * na,))
    outs = _call(
        body, name=name,
        out_shape=[sem, sem] + [pltpu.HBM(s.shape, s.dtype) for s in srcs] + [pltpu.HBM(s, srcs[0].dtype) for s in land_shapes]
        + [jax.ShapeDtypeStruct((8, 128), F32)],
        in_specs=[HBM] * (2 * na) + [ANY], out_specs=[SEM, SEM] + [HBM] * (2 * na) + [pl.BlockSpec(memory_space=pltpu.VMEM)],
        input_output_aliases={i: 2 + i for i in range(2 * na)},
        compiler_params=pltpu.CompilerParams(has_side_effects=EFFECT),
    )(*[_in_hbm(s) for s in srcs], *lands, after)
    return outs[0], outs[1], outs[2:2 + na], outs[2 + na:2 + 2 * na], outs[-1]


def _exchange_wait(send_sems, recv_sems, srcs, lands, kind, after, name):
    na = len(srcs)

    def body(*refs):
        src_refs, land_refs = refs[:na], refs[na:2 * na]
        s_sems, r_sems = refs[2 * na], refs[2 * na + 1]
        for cp in _split_copy_descr(na, kind, src_refs, land_refs, s_sems, r_sems):
            cp.wait_send()
            cp.wait_recv()

    outs = _call(
        body, name=name, out_shape=[pltpu.HBM(s.shape, s.dtype) for s in srcs] + [pltpu.HBM(l.shape, l.dtype) for l in lands],
        in_specs=[HBM] * (2 * na) + [SEM, SEM, ANY], out_specs=[HBM] * (2 * na),
        input_output_aliases={i: i for i in range(2 * na)},
        compiler_params=pltpu.CompilerParams(has_side_effects=EFFECT),
    )(*srcs, *lands, send_sems, recv_sems, after)
    return outs[:na], outs[na:]


def _mm(a, b, mode, out_dtype, name, add=None, tm=1024, tn=1024, tk=1024, into=None, o_rows=None, o_moff=0,
        loss_target=None):
    if mode == "tn":
        K, M = a.shape
    else:
        M, K = a.shape
    N = b.shape[0] if mode == "nt" else b.shape[1]
    tm, tn, tk = min(tm, M), min(tn, N), min(tk, K)
    assert M % tm == 0 and N % tn == 0 and K % tk == 0, (name, M, N, K)
    nk = K // tk
    if mode == "nn":
        a_spec = pl.BlockSpec((tm, tk), lambda i, j, kk: (i, kk))
        b_spec, dims = pl.BlockSpec((tk, tn), lambda i, j, kk: (kk, j)), _NN
    elif mode == "nt":
        a_spec = pl.BlockSpec((tm, tk), lambda i, j, kk: (i, kk))
        b_spec, dims = pl.BlockSpec((tn, tk), lambda i, j, kk: (j, kk)), _NT
    else:
        a_spec = pl.BlockSpec((tk, tm), lambda i, j, kk: (kk, i))
        b_spec, dims = pl.BlockSpec((tk, tn), lambda i, j, kk: (kk, j)), _TN
    o_spec = pl.BlockSpec((tm, tn), lambda i, j, kk: (i + o_moff, j))
    has_add, has_into, has_loss = add is not None, into is not None, loss_target is not None
    assert not has_loss or (has_add and tn == N and not has_into)
    n_in = 2 + has_add + has_loss + has_into

    def body(*refs):
        a_ref, b_ref = refs[0], refs[1]
        add_ref = refs[2] if has_add else None
        outs = refs[n_in:]

        def finish(r):
            if has_add:
                r = r + add_ref[...]
            if has_loss:
                e = r - refs[3][...]
                dy = e * (1.0 / N)
                outs[0][...] = dy
                outs[1][...] = dy.astype(BF16)
                outs[2][...] = jnp.sum(e * e, axis=0, keepdims=True)[None]
            else:
                outs[0][...] = r.astype(out_dtype)

        if nk == 1:
            finish(_dot(a_ref[...], b_ref[...], dims))
        else:
            acc = refs[-1]
            kk = pl.program_id(2)

            @pl.when(kk == 0)
            def _():
                acc[...] = _dot(a_ref[...], b_ref[...], dims)

            @pl.when((kk > 0) & (kk < nk - 1))
            def _():
                acc[...] += _dot(a_ref[...], b_ref[...], dims)

            @pl.when(kk == nk - 1)
            def _():
                finish(acc[...] + _dot(a_ref[...], b_ref[...], dims))

    tile = pl.BlockSpec((tm, tn), lambda i, j, kk: (i, j))
    ins = [a, b] + ([add] if has_add else []) + ([loss_target] if has_loss else []) + ([into] if has_into else [])
    specs = [a_spec, b_spec] + [tile] * (has_add + has_loss) + ([ANY] if has_into else [])
    rows = into.shape[0] if has_into else (o_rows if o_rows is not None else M)
    if has_loss:
        out_specs = [tile, tile, pl.BlockSpec((1, 1, N), lambda i, j, kk: (i, 0, 0))]
        out_shape = [jax.ShapeDtypeStruct((M, N), F32), jax.ShapeDtypeStruct((M, N), BF16), jax.ShapeDtypeStruct((M // tm, 1, N), F32)]
    else:
        out_specs, out_shape = o_spec, jax.ShapeDtypeStruct((rows, N), out_dtype)
    return _call(
        body, name=name, grid=(M // tm, N // tn, nk), in_specs=specs, out_specs=out_specs, out_shape=out_shape,
        scratch_shapes=[pltpu.VMEM((tm, tn), F32)] if nk > 1 else [],
        input_output_aliases={len(ins) - 1: 0} if has_into else {},
        compiler_params=_params("parallel", "parallel", "arbitrary"),
    )(*ins)


def _mm_norm_bwd(parts, b, x, resid, g, name, tm=512, tk=512):
    T, N = x.shape
    counts = [p.shape[1] // tk for p in parts]
    starts = [sum(counts[:i]) for i in range(len(parts))]
    nsteps = sum(counts)
    assert all(p.shape[1] % tk == 0 for p in parts) and b.shape == (nsteps * tk, N)
    npart = len(parts)

    def body(*refs):
        a_refs, b_ref, x_ref, res_ref, g_ref = refs[:npart], refs[npart], refs[npart + 1], refs[npart + 2], refs[npart + 3]
        dx_ref, dxb_ref, dg_ref, acc = refs[npart + 4:]
        i, s = pl.program_id(0), pl.program_id(1)

        @pl.when((i == 0) & (s == 0))
        def _():
            dg_ref[...] = jnp.zeros_like(dg_ref)

        for p in range(npart):
            @pl.when((s >= starts[p]) & (s < starts[p] + counts[p]))
            def _(p=p):
                d = _dot(a_refs[p][...], b_ref[...])

                @pl.when(s == 0)
                def _():
                    acc[...] = d

                @pl.when(s > 0)
                def _():
                    acc[...] += d

        @pl.when(s == nsteps - 1)
        def _():
            xv, dhv = x_ref[...], acc[...]
            r = lax.rsqrt(jnp.mean(xv * xv, axis=-1, keepdims=True) + EPS)
            gd = dhv * g_ref[...]
            m = jnp.mean(gd * xv, axis=-1, keepdims=True)
            dx = res_ref[...] + r * gd - xv * (r * r * r) * m
            dx_ref[...] = dx
            dxb_ref[...] = dx.astype(BF16)
            dg_ref[...] += jnp.sum(dhv * xv * r, axis=0, keepdims=True)

    a_specs = [pl.BlockSpec((tm, tk), lambda i, s, st=st, c=c: (i, jnp.clip(s - st, 0, c - 1))) for st, c in zip(starts, counts)]
    row = pl.BlockSpec((tm, N), lambda i, s: (i, 0))
    vec = pl.BlockSpec((1, N), lambda i, s: (0, 0))
    return _call(
        body, name=name, grid=(T // tm, nsteps),
        in_specs=a_specs + [pl.BlockSpec((tk, N), lambda i, s: (s, 0)), row, row, vec], out_specs=[row, row, vec],
        out_shape=[jax.ShapeDtypeStruct((T, N), F32), jax.ShapeDtypeStruct((T, N), BF16), jax.ShapeDtypeStruct((1, N), F32)],
        scratch_shapes=[pltpu.VMEM((tm, N), F32)], compiler_params=_params("arbitrary", "arbitrary"),
    )(*parts, b, x, resid, g)


def _norm_proj(x, g, wT, name, tm=1024, tn=1280):
    T, K = x.shape
    N = wT.shape[0]

    def body(x_ref, g_ref, w_ref, o_ref, h_ref):
        xv = x_ref[...]
        r = lax.rsqrt(jnp.mean(xv * xv, axis=-1, keepdims=True) + EPS)
        hv = (xv * r * g_ref[...]).astype(BF16)

        @pl.when(pl.program_id(1) == 0)
        def _():
            h_ref[...] = hv

        o_ref[...] = _dot(hv, w_ref[...], _NT)

    return _call(
        body, name=name, grid=(T // tm, N // tn),
        in_specs=[pl.BlockSpec((tm, K), lambda i, j: (i, 0)), pl.BlockSpec((1, K), lambda i, j: (0, 0)),
                  pl.BlockSpec((tn, K), lambda i, j: (j, 0))],
        out_specs=[pl.BlockSpec((tm, tn), lambda i, j: (i, j)), pl.BlockSpec((tm, K), lambda i, j: (i, 0))],
        out_shape=[jax.ShapeDtypeStruct((T, N), F32), jax.ShapeDtypeStruct((T, K), BF16)],
        compiler_params=_params("parallel", "arbitrary"),
    )(x, g, wT)


def _qk_prep(proj, pos, invf, qg, kg, bd, name, tm=1024):
    T = proj.shape[0]

    def body(q_ref, k_ref, pos_ref, invf_ref, qg_ref, kg_ref, bd_ref, qo_ref, ko_ref):
        cos, sin = _rope_tables(pos_ref, invf_ref)

        def prep(xv, gv, scale):
            r = lax.rsqrt(_group_mean(xv * xv, bd_ref[...]) + EPS)
            yv = xv * r * gv
            return ((yv * cos + _rot_half(yv) * sin) * scale).astype(BF16).astype(F32)

        qo_ref[...] = prep(q_ref[...], qg_ref[...], HEAD_DIM ** -0.5)
        ko_ref[...] = prep(k_ref[...], kg_ref[...], 1.0)

    col = lambda j: pl.BlockSpec((tm, ATTN_W), lambda i, j=j: (i, j))
    vec = pl.BlockSpec((1, ATTN_W), lambda i: (0, 0))
    out = pl.BlockSpec((tm, ATTN_W), lambda i: (i, 0))
    return _call(
        body, name=name, grid=(T // tm,),
        in_specs=[col(0), col(1), pl.BlockSpec((tm, 1), lambda i: (i, 0)), vec, vec, vec,
                  pl.BlockSpec((2 * HEAD_DIM, 2 * HEAD_DIM), lambda i: (0, 0))],
        out_specs=[out, out], out_shape=[jax.ShapeDtypeStruct((T, ATTN_W), F32)] * 2,
        compiler_params=_params("parallel"),
    )(proj, proj, pos, invf, qg, kg, bd)


def _qk_prep_bwd(proj, dqh, dkh, dv, pos, invf, qg, kg, bd, name, tm=1024):
    T = proj.shape[0]

    def body(q_ref, k_ref, dq_ref, dk_ref, dv_ref, pos_ref, invf_ref, qg_ref, kg_ref, bd_ref, o_ref, gq_ref, gk_ref):
        @pl.when(pl.program_id(0) == 0)
        def _():
            gq_ref[...] = jnp.zeros_like(gq_ref)
            gk_ref[...] = jnp.zeros_like(gk_ref)

        cos, sin = _rope_tables(pos_ref, invf_ref)

        def back(xv, gv, dz, scale):
            dz = dz * scale
            dy = dz * cos - _rot_half(dz * sin)
            r = lax.rsqrt(_group_mean(xv * xv, bd_ref[...]) + EPS)
            gd = dy * gv
            m = _group_mean(gd * xv, bd_ref[...])
            dx = r * gd - xv * (r * r * r) * m
            return dx, jnp.sum(dy * xv * r, axis=0, keepdims=True)

        dxq, gs = back(q_ref[...], qg_ref[...], dq_ref[...], HEAD_DIM ** -0.5)
        gq_ref[...] += gs
        dxk, gs = back(k_ref[...], kg_ref[...], dk_ref[...], 1.0)
        gk_ref[...] += gs
        o_ref[...] = jnp.concatenate([dxq.astype(BF16), dxk.astype(BF16), dv_ref[...].astype(BF16)], axis=1)

    col = lambda j: pl.BlockSpec((tm, ATTN_W), lambda i, j=j: (i, j))
    row = pl.BlockSpec((tm, ATTN_W), lambda i: (i, 0))
    vec = pl.BlockSpec((1, ATTN_W), lambda i: (0, 0))
    return _call(
        body, name=name, grid=(T // tm,),
        in_specs=[col(0), col(1), row, row, row, pl.BlockSpec((tm, 1), lambda i: (i, 0)), vec, vec, vec,
                  pl.BlockSpec((2 * HEAD_DIM, 2 * HEAD_DIM), lambda i: (0, 0))],
        out_specs=[pl.BlockSpec((tm, 3 * ATTN_W), lambda i: (i, 0)), vec, vec],
        out_shape=[jax.ShapeDtypeStruct((T, 3 * ATTN_W), BF16)] + [jax.ShapeDtypeStruct((1, ATTN_W), F32)] * 2,
        compiler_params=_params("arbitrary"),
    )(proj, proj, dqh, dkh, dv, pos, invf, qg, kg, bd)


def _ld(ref, start, size, dil):
    return ref[pl.ds(start, size), :] if dil == 1 else ref[pl.ds(start, size, stride=dil), :]


def _st(ref, start, size, dil, val):
    if dil == 1:
        ref[pl.ds(start, size), :] = val
    else:
        ref[pl.ds(start, size, stride=dil), :] = val


def _attn_geometry(T, dil):
    nb = T // dil // QBLK
    if nb == 2:
        return 1, 2 * QBLK, 2 * QBLK
    return nb, QBLK, (2 * QBLK if nb >= 2 else QBLK)


ATTN_UNROLL = 4


def _attn_unit(j, u, dil, nit):
    return ATTN_UNROLL * j + u if dil >= ATTN_UNROLL else j + u * (nit // ATTN_UNROLL)


def _attn_block(it, dil, qb, kw):
    c, n = it & (dil - 1), lax.shift_right_logical(it, dil.bit_length() - 1)
    sq = n * (qb * dil) + c
    sk = jnp.maximum(n - (kw // qb - 1), 0) * (qb * dil) + c
    qi = lax.broadcasted_iota(jnp.int32, (2 * qb, kw), 0) & (qb - 1)
    kj = lax.broadcasted_iota(jnp.int32, (2 * qb, kw), 1)
    rel = jnp.where(n > 0, kw - qb, 0) + qi - kj
    return sq, sk, (rel >= 0) & (rel <= QBLK)


def _stack_heads(xv, head0):
    z = jnp.zeros_like(xv)
    return jnp.concatenate([jnp.where(head0, xv, z), jnp.where(head0, z, xv)], axis=0)


def _unstack_heads(x2, head0):
    qb = x2.shape[0] // 2
    return jnp.where(head0, x2[:qb], x2[qb:])


def _attn_fwd(qf, kf, proj, name):
    T = qf.shape[0]

    def body(q_ref, k_ref, v_ref, o_ref, lse_ref):
        for bi, dil in enumerate(DILATIONS):
            nb, qb, kw = _attn_geometry(T, dil)
            nit = nb * dil
            head0 = lax.broadcasted_iota(jnp.int32, (qb, 2 * HEAD_DIM), 1) < HEAD_DIM

            def step(j, carry, bi=bi, dil=dil, qb=qb, kw=kw, nit=nit, head0=head0):
                units = []
                for u in range(ATTN_UNROLL):
                    sq, sk, ok = _attn_block(_attn_unit(j, u, dil, nit), dil, qb, kw)
                    old = (_ld(o_ref, sq, qb, dil), _ld(lse_ref, sq, qb, dil)) if bi > 0 else None
                    units.append((sq, ok, _ld(q_ref, sq, qb, dil).astype(BF16), _ld(k_ref, sk, kw, dil).astype(BF16),
                                  _ld(v_ref, sk, kw, dil).astype(BF16), old))
                results = []
                for sq, ok, qv, kv, vv, old in units:
                    s = jnp.where(ok, _dot(_stack_heads(qv, head0), kv, _NT), NEG_INF)
                    m = jnp.max(s, axis=-1, keepdims=True)
                    p = jnp.exp(s - m).astype(BF16)
                    acc = _dot(p, jnp.concatenate([vv, jnp.ones_like(vv)], axis=1))
                    l = acc[:, 2 * HEAD_DIM:]
                    o_new = _unstack_heads(acc[:, :2 * HEAD_DIM] / l, head0)
                    l_new = _unstack_heads(m + jnp.log(l), head0)
                    if bi > 0:
                        o_old, l_old = old
                        mx = jnp.maximum(l_old, l_new)
                        e0, e1 = jnp.exp(l_old - mx), jnp.exp(l_new - mx)
                        z = e0 + e1
                        o_new = (e0 * o_old + e1 * o_new) / z
                        l_new = mx + jnp.log(z)
                    results.append((sq, o_new, l_new))
                for sq, o_new, l_new in results:
                    _st(o_ref, sq, qb, dil, o_new)
                    _st(lse_ref, sq, qb, dil, l_new)
                return carry

            lax.fori_loop(0, nit // ATTN_UNROLL, step, 0)

    blk = lambda off: pl.BlockSpec((T, 2 * HEAD_DIM), lambda hp, off=off: (0, off + hp))
    return _call(
        body, name=name, grid=(4,), in_specs=[blk(0), blk(0), blk(8)], out_specs=[blk(0), blk(0)],
        out_shape=[jax.ShapeDtypeStruct((T, ATTN_W), F32)] * 2, compiler_params=_params("parallel"),
    )(qf, kf, proj)


def _attn_bwd(qf, kf, proj, do, lse, delta, name):
    T = qf.shape[0]

    def body(q_ref, k_ref, v_ref, do_ref, lse_ref, dl_ref, dq_ref, dk_ref, dv_ref):
        for ref in (dq_ref, dk_ref, dv_ref):
            ref[...] = jnp.zeros_like(ref)
        for dil in DILATIONS:
            nb, qb, kw = _attn_geometry(T, dil)
            nit = nb * dil
            head0 = lax.broadcasted_iota(jnp.int32, (qb, 2 * HEAD_DIM), 1) < HEAD_DIM

            def step(j, carry, dil=dil, qb=qb, kw=kw, nit=nit, head0=head0):
                units = []
                for u in range(ATTN_UNROLL):
                    sq, sk, ok = _attn_block(_attn_unit(j, u, dil, nit), dil, qb, kw)
                    lsev, dlv = _ld(lse_ref, sq, qb, dil), _ld(dl_ref, sq, qb, dil)
                    units.append((sq, sk, ok, _ld(q_ref, sq, qb, dil).astype(BF16), _ld(do_ref, sq, qb, dil).astype(BF16),
                                  jnp.concatenate([lsev[:, 0:1], lsev[:, HEAD_DIM:HEAD_DIM + 1]], axis=0),
                                  jnp.concatenate([dlv[:, 0:1], dlv[:, HEAD_DIM:HEAD_DIM + 1]], axis=0),
                                  _ld(k_ref, sk, kw, dil).astype(BF16), _ld(v_ref, sk, kw, dil).astype(BF16),
                                  _ld(dq_ref, sq, qb, dil), _ld(dk_ref, sk, kw, dil), _ld(dv_ref, sk, kw, dil)))
                results = []
                for sq, sk, ok, qv, dov, lse2, dl2, kv, vv, dq0, dk0, dv0 in units:
                    q2, do2 = _stack_heads(qv, head0), _stack_heads(dov, head0)
                    p = jnp.where(ok, jnp.exp(_dot(q2, kv, _NT) - lse2), 0.0)
                    ds = (p * (_dot(do2, vv, _NT) - dl2)).astype(BF16)
                    results.append((sq, sk, dq0 + _unstack_heads(_dot(ds, kv), head0),
                                    dk0 + _dot(ds, q2, _TN), dv0 + _dot(p.astype(BF16), do2, _TN)))
                for sq, sk, dq, dk, dv in results:
                    _st(dq_ref, sq, qb, dil, dq)
                    _st(dk_ref, sk, kw, dil, dk)
                    _st(dv_ref, sk, kw, dil, dv)
                return carry

            lax.fori_loop(0, nit // ATTN_UNROLL, step, 0)

    blk = lambda off: pl.BlockSpec((T, 2 * HEAD_DIM), lambda hp, off=off: (0, off + hp))
    return _call(
        body, name=name, grid=(4,), in_specs=[blk(0), blk(0), blk(8), blk(0), blk(0), blk(0)], out_specs=[blk(0)] * 3,
        out_shape=[jax.ShapeDtypeStruct((T, ATTN_W), F32)] * 3, compiler_params=_params("parallel"),
    )(qf, kf, proj, do, lse, delta)


def _attn_norm(attn, g, name, tm=1024):
    T = attn.shape[0]

    def body(a_ref, g_ref, o_ref):
        av = a_ref[...]
        r = lax.rsqrt(jnp.mean(av * av, axis=-1, keepdims=True) + EPS)
        o_ref[...] = (av * r * g_ref[...]).astype(BF16)

    row = pl.BlockSpec((tm, ATTN_W), lambda i: (i, 0))
    return _call(
        body, name=name, grid=(T // tm,), in_specs=[row, pl.BlockSpec((1, ATTN_W), lambda i: (0, 0))], out_specs=row,
        out_shape=jax.ShapeDtypeStruct((T, 2 * ATTN_W), BF16), compiler_params=_params("parallel"),
    )(attn, g)


def _attn_norm_bwd(dmix, attn, g, bd, name, tm=1024):
    T = attn.shape[0]

    def body(d_ref, a_ref, g_ref, bd_ref, do_ref, dl_ref, dg_ref):
        @pl.when(pl.program_id(0) == 0)
        def _():
            dg_ref[...] = jnp.zeros_like(dg_ref)

        dy, av = d_ref[...], a_ref[...]
        r = lax.rsqrt(jnp.mean(av * av, axis=-1, keepdims=True) + EPS)
        gd = dy * g_ref[...]
        m = jnp.mean(gd * av, axis=-1, keepdims=True)
        da = r * gd - av * (r * r * r) * m
        do_ref[...] = da
        dl_ref[...] = _group_mean(da * av, bd_ref[...]) * float(HEAD_DIM)
        dg_ref[...] += jnp.sum(dy * av * r, axis=0, keepdims=True)

    row = pl.BlockSpec((tm, ATTN_W), lambda i: (i, 0))
    vec = pl.BlockSpec((1, ATTN_W), lambda i: (0, 0))
    return _call(
        body, name=name, grid=(T // tm,),
        in_specs=[row, row, vec, pl.BlockSpec((2 * HEAD_DIM, 2 * HEAD_DIM), lambda i: (0, 0))], out_specs=[row, row, vec],
        out_shape=[jax.ShapeDtypeStruct((T, ATTN_W), F32)] * 2 + [jax.ShapeDtypeStruct((1, ATTN_W), F32)],
        compiler_params=_params("arbitrary"),
    )(dmix, attn, g, bd)


def _rec_gates(xc, wrg_ref, wig_ref, brg_ref, big_ref, lam_ref):
    xb = xc.astype(BF16)
    r = _sigmoid(_dot(xb, wrg_ref[...]) + brg_ref[...])
    ig = _sigmoid(_dot(xb, wig_ref[...]) + big_ref[...])
    sp = _softplus_neg(lam_ref[...])
    log_a = -LRU_C * r * sp
    a = jnp.exp(log_a)
    th = jnp.tanh(log_a)
    mult = jnp.sqrt(-2.0 * th / (1.0 - th))
    return xb, r, ig, sp, a, mult


def _rec_fwd(proj, mix, cw, cb, wrg, wig, brg, big, lam, g, name, tm=512):
    T = proj.shape[0]
    hb = tm // 8

    def body(xr_ref, halo_ref, gr_ref, cw_ref, cb_ref, wrg_ref, wig_ref, brg_ref, big_ref, lam_ref, g_ref, mix_ref,
             xc_ref, h_ref, out_ref, carry):
        i = pl.program_id(0)

        @pl.when(i == 0)
        def _():
            carry[...] = jnp.zeros_like(carry)

        xr = xr_ref[...]
        halo = jnp.where(i > 0, halo_ref[...], 0.0)
        xc = cb_ref[...] + cw_ref[3:4, :] * xr
        for s in range(1, REC_CONV):
            xc = xc + cw_ref[3 - s:4 - s, :] * _shift_down(xr, halo, s)
        xc_ref[...] = xc
        _, _, ig, _, a, mult = _rec_gates(xc, wrg_ref, wig_ref, brg_ref, big_ref, lam_ref)
        pa, hl = _scan_fwd(a, mult * (ig * xc))
        h = hl + pa * carry[0:1, :]
        h_ref[...] = h
        carry[0:1, :] = h_ref[pl.ds(tm - 1, 1), :]
        hg = h * _gelu(gr_ref[...])
        r = lax.rsqrt(jnp.mean(hg * hg, axis=-1, keepdims=True) + EPS)
        out_ref[...] = (hg * r * g_ref[...]).astype(BF16)

    vec = pl.BlockSpec((1, REC_W), lambda i: (0, 0))
    row = pl.BlockSpec((tm, REC_W), lambda i: (i, 0))
    mat = pl.BlockSpec((REC_W, REC_W), lambda i: (0, 0))
    return _call(
        body, name=name, grid=(T // tm,),
        in_specs=[pl.BlockSpec((tm, REC_W), lambda i: (i, 3)),
                  pl.BlockSpec((8, REC_W), lambda i: (jnp.maximum(i * hb - 1, 0), 3)),
                  pl.BlockSpec((tm, REC_W), lambda i: (i, 4)),
                  pl.BlockSpec((8, REC_W), lambda i: (0, 0)), vec, mat, mat, vec, vec, vec, vec, ANY],
        out_specs=[row, row, pl.BlockSpec((tm, REC_W), lambda i: (i, 1))],
        out_shape=[jax.ShapeDtypeStruct((T, REC_W), F32)] * 2 + [jax.ShapeDtypeStruct(mix.shape, BF16)],
        scratch_shapes=[pltpu.VMEM((8, REC_W), F32)], input_output_aliases={11: 2},
        compiler_params=_params("arbitrary"),
    )(proj, proj, proj, cw, cb, wrg, wig, brg, big, lam, g, mix)


def _rec_bwd(dmix, proj, xc, h, cw, cb, wrg, wig, brg, big, lam, g, name, tm=512):
    T = proj.shape[0]
    nt = T // tm
    hb = tm // 8

    def body(d_ref, xr_ref, xhalo_ref, gr_ref, xc_ref, h_ref, hhalo_ref, cw_ref, cb_ref, wrg_ref, wig_ref, brg_ref,
             big_ref, lam_ref, g_ref,
             drec_ref, gcw_ref, gcb_ref, gwrg_ref, gwig_ref, gbrg_ref, gbig_ref, glam_ref, gg_ref,
             g_carry, a_first, dxc_next, gsp):
        i = pl.program_id(0)
        first_tile = i == nt - 1

        @pl.when(i == 0)
        def _():
            for ref in (gcw_ref, gcb_ref, gwrg_ref, gwig_ref, gbrg_ref, gbig_ref, glam_ref, gg_ref,
                        g_carry, a_first, dxc_next, gsp):
                ref[...] = jnp.zeros_like(ref)

        xr, xc, hv = xr_ref[...], xc_ref[...], h_ref[...]
        xhalo = jnp.where(first_tile, 0.0, xhalo_ref[...])
        hhalo = jnp.where(first_tile, 0.0, hhalo_ref[...])
        xb, r, ig, sp, a, mult = _rec_gates(xc, wrg_ref, wig_ref, brg_ref, big_ref, lam_ref)
        h_prev = _shift_down(hv, hhalo, 1)
        ge, dge = _gelu_and_grad(gr_ref[...])
        hg = hv * ge
        rr = lax.rsqrt(jnp.mean(hg * hg, axis=-1, keepdims=True) + EPS)
        dy = d_ref[...]
        gd = dy * g_ref[...]
        dhg = rr * gd - hg * (rr * rr * rr) * jnp.mean(gd * hg, axis=-1, keepdims=True)
        gg_ref[...] += jnp.sum(dy * hg * rr, axis=0, keepdims=True)
        dgr = (dhg * hv * dge).astype(BF16)
        dh = dhg * ge
        b = _shift_up(a, jnp.broadcast_to(a_first[0:1, :], (8, REC_W)), 1)
        pb, gl = _scan_bwd(b, dh)
        gs = gl + pb * g_carry[0:1, :]
        g_carry[0:1, :] = gs[0:1, :]
        a_first[0:1, :] = a[0:1, :]
        da = gs * h_prev
        dmult = gs * (ig * xc)
        di = gs * (mult * xc)
        dxc = gs * (mult * ig)
        dlog_a = da * a - dmult * (a * a) / mult
        gsp[...] += jnp.sum(dlog_a * (-LRU_C * r), axis=0, keepdims=True)
        dzr = (dlog_a * (-LRU_C * sp)) * (r * (1.0 - r))
        dzi = di * (ig * (1.0 - ig))
        dzr_b, dzi_b = dzr.astype(BF16), dzi.astype(BF16)
        dxc = dxc + _dot(dzr_b, wrg_ref[...], _NT) + _dot(dzi_b, wig_ref[...], _NT)
        gwrg_ref[...] += _dot(xb, dzr_b, _TN)
        gwig_ref[...] += _dot(xb, dzi_b, _TN)
        gbrg_ref[...] += jnp.sum(dzr, axis=0, keepdims=True)
        gbig_ref[...] += jnp.sum(dzi, axis=0, keepdims=True)
        nxt = dxc_next[...]
        dxr = cw_ref[3:4, :] * dxc
        gcw_ref[3:4, :] += jnp.sum(dxc * xr, axis=0, keepdims=True)
        for s in range(1, REC_CONV):
            dxr = dxr + cw_ref[3 - s:4 - s, :] * _shift_up(dxc, nxt, s)
            gcw_ref[3 - s:4 - s, :] += jnp.sum(dxc * _shift_down(xr, xhalo, s), axis=0, keepdims=True)
        gcb_ref[...] += jnp.sum(dxc, axis=0, keepdims=True)
        dxc_next[...] = dxc[:8]
        drec_ref[...] = jnp.concatenate([dxr.astype(BF16), dgr], axis=1)

        @pl.when(first_tile)
        def _():
            glam_ref[...] = gsp[...] * (-_sigmoid(-lam_ref[...]))

    rev = lambda i: nt - 1 - i
    vec = pl.BlockSpec((1, REC_W), lambda i: (0, 0))
    row = pl.BlockSpec((tm, REC_W), lambda i: (rev(i), 0))
    mat = pl.BlockSpec((REC_W, REC_W), lambda i: (0, 0))
    cwb = pl.BlockSpec((8, REC_W), lambda i: (0, 0))
    halo = lambda c: pl.BlockSpec((8, REC_W), lambda i, c=c: (jnp.maximum(rev(i) * hb - 1, 0), c))
    return _call(
        body, name=name, grid=(nt,),
        in_specs=[pl.BlockSpec((tm, REC_W), lambda i: (rev(i), 1)),
                  pl.BlockSpec((tm, REC_W), lambda i: (rev(i), 3)), halo(3),
                  pl.BlockSpec((tm, REC_W), lambda i: (rev(i), 4)),
                  row, row, halo(0), cwb, vec, mat, mat, vec, vec, vec, vec],
        out_specs=[pl.BlockSpec((tm, 2 * REC_W), lambda i: (rev(i), 0)), cwb, vec, mat, mat, vec, vec, vec, vec],
        out_shape=[jax.ShapeDtypeStruct((T, 2 * REC_W), BF16)]
        + [jax.ShapeDtypeStruct((8, REC_W), F32), jax.ShapeDtypeStruct((1, REC_W), F32)]
        + [jax.ShapeDtypeStruct((REC_W, REC_W), F32)] * 2 + [jax.ShapeDtypeStruct((1, REC_W), F32)] * 4,
        scratch_shapes=[pltpu.VMEM((8, REC_W), F32)] * 3 + [pltpu.VMEM((1, REC_W), F32)],
        compiler_params=_params("arbitrary"),
    )(dmix, proj, proj, proj, xc, h, h, cw, cb, wrg, wig, brg, big, lam, g)


def _ffn_conv(x_ext, cw_ref, cb_ref):
    return (cb_ref[...] + cw_ref[2:3, :] * x_ext + cw_ref[1:2, :] * pltpu.roll(x_ext, 1, 0)
            + cw_ref[0:1, :] * pltpu.roll(x_ext, 2, 0))


def _up_proj_act(x2, g, w_upT, cw, cb, name, tm=1024, tc=768):
    T = x2.shape[0]
    nc = D_FF // tc

    def body(x_ref, g_ref, wg_ref, wu_ref, cwg_ref, cwu_ref, cbg_ref, cbu_ref, act_ref, da_ref, db_ref, pg_ref, pu_ref,
             h_ref, hist_g, hist_u, hs):
        i, j = pl.program_id(0), pl.program_id(1)

        @pl.when(j == 0)
        def _():
            xv = x_ref[...]
            r = lax.rsqrt(jnp.mean(xv * xv, axis=-1, keepdims=True) + EPS)
            hs[...] = (xv * r * g_ref[...]).astype(BF16)
            h_ref[...] = hs[...]

        hv = hs[...]
        pg, pu = _dot(hv, wg_ref[...], _NT), _dot(hv, wu_ref[...], _NT)
        ge = jnp.concatenate([jnp.where(i > 0, hist_g[j], 0.0), pg], axis=0)
        ue = jnp.concatenate([jnp.where(i > 0, hist_u[j], 0.0), pu], axis=0)
        gel, dgel = _gelu_and_grad(_ffn_conv(ge, cwg_ref, cbg_ref)[8:])
        uu = _ffn_conv(ue, cwu_ref, cbu_ref)[8:]
        act_ref[...] = (gel * uu).astype(BF16)
        da_ref[...] = (uu * dgel).astype(BF16)
        db_ref[...] = gel.astype(BF16)
        pg_ref[...] = pg.astype(BF16)
        pu_ref[...] = pu.astype(BF16)
        hist_g[j] = pg[tm - 8:]
        hist_u[j] = pu[tm - 8:]

    tile = pl.BlockSpec((tm, tc), lambda i, j: (i, j))
    wsp = lambda off: pl.BlockSpec((tc, D_MODEL), lambda i, j, off=off: (j + off, 0))
    cws = lambda off: pl.BlockSpec((8, tc), lambda i, j, off=off: (0, j + off))
    cbs = lambda off: pl.BlockSpec((1, tc), lambda i, j, off=off: (0, j + off))
    return _call(
        body, name=name, grid=(T // tm, nc),
        in_specs=[pl.BlockSpec((tm, D_MODEL), lambda i, j: (i, 0)), pl.BlockSpec((1, D_MODEL), lambda i, j: (0, 0)),
                  wsp(0), wsp(nc), cws(0), cws(nc), cbs(0), cbs(nc)],
        out_specs=[tile] * 5 + [pl.BlockSpec((tm, D_MODEL), lambda i, j: (i, 0))],
        out_shape=[jax.ShapeDtypeStruct((T, D_FF), BF16)] * 5 + [jax.ShapeDtypeStruct((T, D_MODEL), BF16)],
        scratch_shapes=[pltpu.VMEM((nc, 8, tc), F32)] * 2 + [pltpu.VMEM((tm, D_MODEL), BF16)],
        compiler_params=_params("arbitrary", "arbitrary"),
    )(x2, g, w_upT, w_upT, cw, cw, cb, cb)


def _ffn_bwd(dyb, w_down, da, db, pg, pu, cw, name, tm=1024, tc=768):
    T, F = pg.shape
    nt = T // tm
    hb16 = tm // 16
    nc = F // tc
    n = tm + 8

    def body(dy_ref, dyn_ref, wd_ref, a_ref, an_ref, b_ref, bn_ref, g_ref, u_ref, cwg_ref, cwu_ref,
             dg_ref, du_ref, gcwg_ref, gcwu_ref, gcbg_ref, gcbu_ref):
        i = pl.program_id(1)
        last = i == nt - 1

        @pl.when(i == 0)
        def _():
            for ref in (gcwg_ref, gcwu_ref, gcbg_ref, gcbu_ref):
                ref[...] = jnp.zeros_like(ref)

        wd = wd_ref[...]
        dact_next = jnp.where(last, 0.0, _dot(dyn_ref[...], wd, _NT)[:8])
        de = jnp.concatenate([_dot(dy_ref[...], wd, _NT), dact_next], axis=0)
        ext = lambda t, nx: jnp.concatenate([t[...].astype(F32), nx[...].astype(F32)[:8]], axis=0)
        for dcv, x_ref, cw_ref, dx_ref, gcw_ref, gcb_ref in ((de * ext(a_ref, an_ref), g_ref, cwg_ref, dg_ref, gcwg_ref, gcbg_ref),
                                                               (de * ext(b_ref, bn_ref), u_ref, cwu_ref, du_ref, gcwu_ref, gcbu_ref)):
            s1, s2 = pltpu.roll(dcv, n - 1, 0), pltpu.roll(dcv, n - 2, 0)
            dx_ref[...] = (cw_ref[2:3, :] * dcv + cw_ref[1:2, :] * s1 + cw_ref[0:1, :] * s2)[:tm].astype(BF16)
            xv = x_ref[...].astype(F32)
            gcw_ref[2:3, :] += jnp.sum(xv * dcv[:tm], axis=0, keepdims=True)
            gcw_ref[1:2, :] += jnp.sum(xv * s1[:tm], axis=0, keepdims=True)
            gcw_ref[0:1, :] += jnp.sum(xv * s2[:tm], axis=0, keepdims=True)
            gcb_ref[...] += jnp.sum(dcv[:tm], axis=0, keepdims=True)

    tile = pl.BlockSpec((tm, tc), lambda j, i: (i, j))
    nxt = pl.BlockSpec((16, tc), lambda j, i: (jnp.minimum((i + 1) * hb16, nt * hb16 - 1), j))
    cws = lambda off: pl.BlockSpec((8, tc), lambda j, i, off=off: (0, j + off))
    cbs = pl.BlockSpec((1, tc), lambda j, i: (0, j))
    return _call(
        body, name=name, grid=(nc, nt),
        in_specs=[pl.BlockSpec((tm, D_MODEL), lambda j, i: (i, 0)),
                  pl.BlockSpec((16, D_MODEL), lambda j, i: (jnp.minimum((i + 1) * hb16, nt * hb16 - 1), 0)),
                  pl.BlockSpec((tc, D_MODEL), lambda j, i: (j, 0)), tile, nxt, tile, nxt, tile, tile, cws(0), cws(nc)],
        out_specs=[tile, tile, cws(0), cws(0), cbs, cbs],
        out_shape=[jax.ShapeDtypeStruct((T, F), BF16)] * 2 + [jax.ShapeDtypeStruct((8, F), F32)] * 2
        + [jax.ShapeDtypeStruct((1, F), F32)] * 2,
        compiler_params=_params("parallel", "arbitrary"),
    )(dyb, dyb, w_down, da, da, db, db, pg, pu, cw, cw)


def _adam_update(w, g, m, v):
    m2 = ADAM_B1 * m + (1.0 - ADAM_B1) * g
    v2 = ADAM_B2 * v + (1.0 - ADAM_B2) * (g * g)
    m_hat = m2 / (1.0 - ADAM_B1 ** ADAM_STEP)
    v_hat = v2 / (1.0 - ADAM_B2 ** ADAM_STEP)
    delta = -ADAM_LR * (m_hat / (jnp.sqrt(v_hat) + ADAM_EPS) + ADAM_WD * w)
    return delta, m2, v2


def _adam_sharded(p, r2, idx, w, m, v, name, transposed=False):
    r, n = p.shape[1:]
    nrecv = r2.shape[0]
    tr = (256 if r % 256 == 0 else r) if transposed else _row_tile(r)

    def body(c_ref, p_ref, r_ref, w_ref, m_ref, v_ref, g_ref, d_ref, m2_ref, v2_ref):
        g = p_ref[...].astype(F32)
        for k in range(nrecv):
            g = g + r_ref[k].astype(F32)
        if transposed:
            g = g.T
        g_ref[...] = g
        d_ref[...], m2_ref[...], v2_ref[...] = _adam_update(w_ref[...], g, m_ref[...], v_ref[...])

    blk = pl.BlockSpec((n, tr), lambda i, c_ref: (0, i)) if transposed else pl.BlockSpec((tr, n), lambda i, c_ref: (i, 0))
    spec = pltpu.PrefetchScalarGridSpec(
        num_scalar_prefetch=1, grid=(r // tr,),
        in_specs=[pl.BlockSpec((None, tr, n), lambda i, c_ref: (c_ref[0], i, 0)),
                  pl.BlockSpec((nrecv, tr, n), lambda i, c_ref: (0, i, 0)), blk, blk, blk],
        out_specs=[blk] * 4)
    return _call(body, name=name, grid_spec=spec, out_shape=[jax.ShapeDtypeStruct(w.shape, F32)] * 4,
                 compiler_params=_params("parallel"))(idx, p, r2, w, m, v)


def _sum_slabs(p, r2, idx, name):
    _, r, n = p.shape

    def body(c_ref, p_ref, r_ref, o_ref):
        acc = p_ref[...]
        for k in range(N_PEERS):
            acc = acc + r_ref[k]
        o_ref[...] = acc

    spec = pltpu.PrefetchScalarGridSpec(
        num_scalar_prefetch=1, grid=(1,),
        in_specs=[pl.BlockSpec((None, r, n), lambda i, c_ref: (c_ref[0], 0, 0)),
                  pl.BlockSpec((N_PEERS, r, n), lambda i, c_ref: (0, 0, 0))],
        out_specs=pl.BlockSpec((r, n), lambda i, c_ref: (0, 0)))
    return _call(body, name=name, grid_spec=spec, out_shape=jax.ShapeDtypeStruct((r, n), F32))(idx, p, r2)


def _adam_small(ws, gs, ms, vs, name):
    n = len(ws)

    def body(*refs):
        for i in range(n):
            d, m2, v2 = _adam_update(refs[i][...], refs[n + i][...], refs[2 * n + i][...], refs[3 * n + i][...])
            refs[4 * n + i][...] = d
            refs[5 * n + i][...] = m2
            refs[6 * n + i][...] = v2

    outs = _call(body, name=name, out_shape=[jax.ShapeDtypeStruct(w.shape, F32) for w in ws] * 3)(*ws, *gs, *ms, *vs)
    return outs[:n], outs[n:2 * n], outs[2 * n:]


def _pack_small_grads(full, halves, rcw, fcwg, fcwu, wrg, wig, lparts, name):
    nf, nh = len(full), len(halves)

    def body(*refs):
        o = refs[-1]
        o[...] = jnp.zeros_like(o)
        row = 0
        for r in refs[:nf]:
            for j in range(r.shape[1] // 1024):
                o[row:row + 1, :] = r[:, 1024 * j:1024 * (j + 1)]
                row += 1
        for k in range(0, nh, 2):
            o[row:row + 1, 0:512] = refs[nf + k][...]
            o[row:row + 1, 512:1024] = refs[nf + k + 1][...]
            row += 1
        rcw_ref, fg_ref, fu_ref, wrg_ref, wig_ref, l_ref = refs[nf + nh:nf + nh + 6]
        for k in range(2):
            o[row:row + 1, 0:512] = rcw_ref[2 * k:2 * k + 1, :]
            o[row:row + 1, 512:1024] = rcw_ref[2 * k + 1:2 * k + 2, :]
            row += 1
        for f_ref in (fg_ref, fu_ref):
            for k in range(FFN_CONV):
                for j in range(D_FF // 1024):
                    o[row:row + 1, :] = f_ref[k:k + 1, 1024 * j:1024 * (j + 1)]
                    row += 1
        assert row == 32
        for n in range(8):
            o[32:96, 64 * n:64 * n + 64] = wrg_ref[64 * n:64 * n + 64, 64 * n:64 * n + 64]
            o[32:96, 512 + 64 * n:512 + 64 * n + 64] = wig_ref[64 * n:64 * n + 64, 64 * n:64 * n + 64]
        o[96:97, :] = jnp.sum(l_ref[...], axis=0, keepdims=True)

    return _call(body, name=name, out_shape=jax.ShapeDtypeStruct((SMALL_ROWS, 1024), F32))(
        *full, *halves, rcw, fcwg, fcwu, wrg, wig, lparts)


def _block_diag(w):
    eye = jnp.eye(8, dtype=w.dtype)
    return (w[:, :, None, :] * eye[:, None, :, None]).reshape(512, 512)


def kernel(x, positions, g_mix, w_in, q_norm_g, k_norm_g, rec_conv_w, rec_conv_b, w_rg, b_rg, w_ig, b_ig, lru_lambda, g_attn_out, g_rec_out, w_out, g_ffn, w_up, ffn_conv_w, ffn_conv_b, w_down, loss_target, m_g_mix, m_w_in, m_q_norm_g, m_k_norm_g, m_rec_conv_w, m_rec_conv_b, m_w_rg, m_b_rg, m_w_ig, m_b_ig, m_lru_lambda, m_g_attn_out, m_g_rec_out, m_w_out, m_g_ffn, m_w_up, m_ffn_conv_w, m_ffn_conv_b, m_w_down, v_g_mix, v_w_in, v_q_norm_g, v_k_norm_g, v_rec_conv_w, v_rec_conv_b, v_w_rg, v_b_rg, v_w_ig, v_b_ig, v_lru_lambda, v_g_attn_out, v_g_rec_out, v_w_out, v_g_ffn, v_w_up, v_ffn_conv_w, v_ffn_conv_b, v_w_down):
    T = x.shape[1]
    ix, iy, ic = lax.axis_index("x"), lax.axis_index("y"), lax.axis_index("c")
    dev = 4 * ix + 2 * iy + ic
    xs = x.reshape(T, D_MODEL)
    tgt = loss_target.reshape(T, D_MODEL)
    pos = positions.reshape(T, 1)

    shards = {"w_in": (w_in[0], m_w_in[0], v_w_in[0]), "w_out": (w_out[0], m_w_out[0], v_w_out[0]),
              "w_up": (w_up[0], m_w_up[0], v_w_up[0]), "w_down": (w_down[0], m_w_down[0], v_w_down[0])}
    taps = jnp.concatenate([rec_conv_w.reshape(-1), ffn_conv_w.reshape(-1), jnp.zeros((4096 - 2560,), F32)]).reshape(8, 512)
    W_inT, taps_all = _all_gather([w_in[0].T.astype(BF16), taps], "ag_w_in")
    gather_landing = lambda s: _landing((N_DEV * s.shape[0], 1024), BF16, s, dev * s.shape[0])
    late = [w_out[0].astype(BF16), w_up[0].T.astype(BF16)]
    ag_send, ag_recv, late_thru, land_thru, ag_token = _exchange_start(
        late, [gather_landing(s) for s in late], "gather", taps_all, "ag_late_start")
    w_down_b = w_down[0].astype(BF16)
    down_landing = gather_landing(w_down_b)
    taps_all = taps_all.reshape(N_DEV, 4096)
    rcw = taps_all[:, :256].reshape(8, 4, 64).transpose(1, 0, 2).reshape(4, REC_W)
    fcw = taps_all[:, 256:2560].reshape(8, 3, 768).transpose(1, 0, 2).reshape(3, 2 * D_FF)
    rcw8 = jnp.pad(rcw, ((0, 4), (0, 0)))
    fcw8 = jnp.pad(fcw, ((0, 5), (0, 0)))
    fcb = ffn_conv_b.reshape(1, 2 * D_FF)

    half = HEAD_DIM // 2
    inv_freq = ROPE_THETA ** (-jnp.arange(half, dtype=F32) / half)
    invf = jnp.tile(inv_freq, 2 * N_HEADS).reshape(1, ATTN_W)
    bd = jnp.asarray(np.kron(np.eye(2), np.full((HEAD_DIM, HEAD_DIM), 1.0 / HEAD_DIM)), BF16)
    qg = jnp.tile(q_norm_g.reshape(HEAD_DIM), N_HEADS).reshape(1, ATTN_W)
    kg = jnp.tile(k_norm_g.reshape(HEAD_DIM), N_HEADS).reshape(1, ATTN_W)
    wrg_bd = _block_diag(w_rg[0]).astype(BF16)
    wig_bd = _block_diag(w_ig[0]).astype(BF16)
    brg, big = b_rg.reshape(1, REC_W), b_ig.reshape(1, REC_W)

    proj, h1 = _norm_proj(xs, g_mix + ag_token[0, 0], W_inT, "in_proj", tn=IN_W)
    qf, kf = _qk_prep(proj, pos, invf, qg, kg, bd, "qk_prep")
    attn, lse = _attn_fwd(qf, kf, proj, "attn_fwd")
    dn_send, dn_recv, dn_thru, dn_land, dn_token = _exchange_start(
        [w_down_b], [down_landing], "gather", attn, "ag_down_start")
    mix = _attn_norm(attn, g_attn_out + dn_token[0, 0], "attn_norm")
    xc, hstate, mix = _rec_fwd(proj, mix, rcw8, rec_conv_b, wrg_bd, wig_bd, brg, big, lru_lambda, g_rec_out, "rec_fwd")
    _, (W_out, W_upT) = _exchange_wait(ag_send, ag_recv, late_thru, land_thru, "gather", hstate, "ag_late_wait")
    x2 = _mm(mix, W_out, "nn", F32, "out_proj", add=xs)

    act, da, db, pg, pu, h2 = _up_proj_act(x2, g_ffn, W_upT, fcw8, fcb, "up_proj_act")
    _, (W_down,) = _exchange_wait(dn_send, dn_recv, dn_thru, dn_land, "gather", h2, "ag_down_wait")
    dy, dyb, lparts = _mm(act, W_down, "nn", F32, "down_proj_loss", add=x2, loss_target=tgt, tm=512, tk=D_FF)

    g_down = _mm(act, dyb, "tn", BF16, "g_w_down", tk=4096)
    dpg, dpu, g_fcwg, g_fcwu, g_fcbg, g_fcbu = _ffn_bwd(dyb, W_down, da, db, pg, pu, fcw8, "ffn_bwd")
    g_upT = _mm(dpg, h2, "tn", BF16, "g_w_up_gate", tk=4096, o_rows=2 * D_FF)
    g_upT = _mm(dpu, h2, "tn", BF16, "g_w_up_up", tk=4096, into=g_upT, o_moff=D_FF // 1024)
    ffn_g = [g_upT.reshape(N_DEV, 2 * D_FF // N_DEV, 1024), g_down.reshape(N_DEV, D_FF // N_DEV, 1024)]
    rs_send, rs_recv, ffn_g, ffn_land, rs_token = _exchange_start(
        ffn_g, [_landing((N_PEERS,) + g.shape[1:], BF16) for g in ffn_g], "scatter", dpu, "rs_ffn_start")
    dx2, dx2b, g_gffn = _mm_norm_bwd([dpg, dpu], W_upT, x2, dy, g_ffn + rs_token[0, 0], "d_h2_norm_bwd", tm=1024, tk=1536)

    dmix = _mm(dx2b, W_out, "nt", F32, "d_mix")
    g_out = _mm(mix, dx2b, "tn", BF16, "g_w_out", tk=4096).reshape(N_DEV, D_MODEL // N_DEV, 1024)
    out_send, out_recv, (g_out,), out_land, out_token = _exchange_start(
        [g_out], [_landing((N_PEERS,) + g_out.shape[1:], BF16)], "scatter", dmix, "rs_out_start")
    do, delta, g_gattn = _attn_norm_bwd(dmix, attn, g_attn_out + out_token[0, 0], bd, "attn_norm_bwd")
    dqh, dkh, dv = _attn_bwd(qf, kf, proj, do, lse, delta, "attn_bwd")
    dqkv, g_qg, g_kg = _qk_prep_bwd(proj, dqh, dkh, dv, pos, invf, qg, kg, bd, "qk_prep_bwd")
    (drec, g_rcw, g_rcb, g_wrg, g_wig, g_brg, g_big, g_lam, g_grec) = _rec_bwd(
        dmix, proj, xc, hstate, rcw8, rec_conv_b, wrg_bd, wig_bd, brg, big, lru_lambda, g_rec_out, "rec_bwd")
    g_inT = _mm(dqkv, h1, "tn", BF16, "g_w_in_qkv", tm=512, tk=4096, o_rows=IN_W)
    g_inT = _mm(drec, h1, "tn", BF16, "g_w_in_rec", tm=512, tk=4096, into=g_inT, o_moff=3 * ATTN_W // 512)
    g_inT = g_inT.reshape(N_DEV, IN_W // N_DEV, 1024)
    in_send, in_recv, (g_inT,), in_land, in_token = _exchange_start(
        [g_inT], [_landing((N_PEERS,) + g_inT.shape[1:], BF16)], "scatter", drec, "rs_in_start")
    grad_x, _, g_gmix = _mm_norm_bwd([dqkv, drec], W_inT, xs, dx2, g_mix + in_token[0, 0], "d_h1_norm_bwd", tm=1024, tk=512)

    flat = _pack_small_grads([g_gmix, g_gffn, g_fcbg, g_fcbu], [g_rcb, g_brg, g_big, g_lam, g_gattn, g_grec, g_qg, g_kg],
                             g_rcw, g_fcwg, g_fcwu, g_wrg, g_wig, lparts.reshape(-1, D_MODEL), "pack_small_grads")
    srows = SMALL_ROWS // N_DEV
    flat = flat.reshape(N_DEV, srows, 1024)
    sm_send, sm_recv, (flat,), sm_land, sm_token = _exchange_start(
        [flat], [_landing((N_PEERS, srows, 1024), F32)], "scatter", grad_x, "ar_small_rs_start")

    devi = jnp.reshape(dev, (1,)).astype(jnp.int32)
    ffn_g, ffn_land = _exchange_wait(rs_send, rs_recv, ffn_g, ffn_land, "scatter", sm_token, "rs_ffn_wait")
    (g_out,), out_land = _exchange_wait(out_send, out_recv, [g_out], out_land, "scatter", sm_token, "rs_out_wait")
    big_out = {"grad": {}, "delta": {}, "new_m": {}, "new_v": {}}

    def adam_big(nm, p, r):
        w_, m_, v_ = shards[nm]
        res = _adam_sharded(p, r, devi, w_, m_, v_, "adam_" + nm, transposed=nm in ("w_in", "w_up"))
        for kind, a in zip(("grad", "delta", "new_m", "new_v"), res):
            big_out[kind][nm] = a[None]
        return res[0]

    last = adam_big("w_up", ffn_g[0], ffn_land[0])
    (flat,), sm_land = _exchange_wait(sm_send, sm_recv, [flat], sm_land, "scatter", last, "ar_small_rs_wait")
    mine = _sum_slabs(flat, sm_land[0], devi, "sum_small_grads")
    sm_send, sm_recv, (mine,), sm_land, sm_token = _exchange_start(
        [mine], [_landing((SMALL_ROWS, 1024), F32, mine, dev * srows)], "gather", last, "ar_small_ag_start")
    adam_big("w_down", ffn_g[1], ffn_land[1])
    last = adam_big("w_out", g_out, out_land[0])
    _, (tot,) = _exchange_wait(sm_send, sm_recv, [mine], sm_land, "gather", last, "ar_small_ag_wait")
    (g_inT,), in_land = _exchange_wait(in_send, in_recv, [g_inT], in_land, "scatter", tot, "rs_in_wait")
    adam_big("w_in", g_inT, in_land[0])

    half = lambda r, h, shape: tot[r, 512 * h:512 * h + 512].reshape(shape)
    blocks = lambda h: tot[32:96, 512 * h:512 * h + 512].reshape(64, 8, 64).transpose(1, 0, 2)[None]
    fcw_full = jnp.concatenate([tot[14:23].reshape(1, 3, D_FF), tot[23:32].reshape(1, 3, D_FF)], axis=2)
    g_small = {
        "g_mix": tot[0:1], "g_ffn": tot[1:2], "ffn_conv_b": tot[2:8].reshape(1, 2 * D_FF),
        "rec_conv_b": half(8, 0, (1, 512)), "b_rg": half(8, 1, (1, 8, 64)), "b_ig": half(9, 0, (1, 8, 64)),
        "lru_lambda": half(9, 1, (1, 512)), "g_attn_out": half(10, 0, (1, 512)), "g_rec_out": half(10, 1, (1, 512)),
        "q_norm_g": half(11, 0, (N_HEADS, HEAD_DIM)).sum(0)[None], "k_norm_g": half(11, 1, (N_HEADS, HEAD_DIM)).sum(0)[None],
        "w_rg": blocks(0), "w_ig": blocks(1),
        "rec_conv_w": lax.dynamic_slice(tot[12:14].reshape(1, 4, REC_W), (0, 0, 64 * dev), (1, 4, 64)),
        "ffn_conv_w": lax.dynamic_slice(fcw_full, (0, 0, 768 * dev), (1, 3, 768))}
    loss = 0.5 / D_MODEL * jnp.sum(tot[96])
    given = dict(rec_conv_w=rec_conv_w, ffn_conv_w=ffn_conv_w,g_mix=g_mix, q_norm_g=q_norm_g, k_norm_g=k_norm_g, rec_conv_b=rec_conv_b, w_rg=w_rg, b_rg=b_rg, w_ig=w_ig,
                 b_ig=b_ig, lru_lambda=lru_lambda, g_attn_out=g_attn_out, g_rec_out=g_rec_out, g_ffn=g_ffn, ffn_conv_b=ffn_conv_b)
    given_m = dict(rec_conv_w=m_rec_conv_w, ffn_conv_w=m_ffn_conv_w, g_mix=m_g_mix, q_norm_g=m_q_norm_g, k_norm_g=m_k_norm_g, rec_conv_b=m_rec_conv_b, w_rg=m_w_rg, b_rg=m_b_rg,
                   w_ig=m_w_ig, b_ig=m_b_ig, lru_lambda=m_lru_lambda, g_attn_out=m_g_attn_out, g_rec_out=m_g_rec_out,
                   g_ffn=m_g_ffn, ffn_conv_b=m_ffn_conv_b)
    given_v = dict(rec_conv_w=v_rec_conv_w, ffn_conv_w=v_ffn_conv_w, g_mix=v_g_mix, q_norm_g=v_q_norm_g, k_norm_g=v_k_norm_g, rec_conv_b=v_rec_conv_b, w_rg=v_w_rg, b_rg=v_b_rg,
                   w_ig=v_w_ig, b_ig=v_b_ig, lru_lambda=v_lru_lambda, g_attn_out=v_g_attn_out, g_rec_out=v_g_rec_out,
                   g_ffn=v_g_ffn, ffn_conv_b=v_ffn_conv_b)
    small = sorted(given)
    ds, m2s, v2s = _adam_small([given[k] for k in small], [g_small[k] for k in small], [given_m[k] for k in small],
                               [given_v[k] for k in small], "adam_small")
    small_out = {"grad": g_small, "delta": dict(zip(small, ds)), "new_m": dict(zip(small, m2s)), "new_v": dict(zip(small, v2s))}

    order = ("g_mix", "w_in", "q_norm_g", "k_norm_g", "rec_conv_w", "rec_conv_b", "w_rg", "b_rg", "w_ig", "b_ig",
             "lru_lambda", "g_attn_out", "g_rec_out", "w_out", "g_ffn", "w_up", "ffn_conv_w", "ffn_conv_b", "w_down")
    outs = [loss, grad_x.reshape(1, T, D_MODEL)]
    for kind in ("grad", "delta", "new_m", "new_v"):
        for name in order:
            outs.append(big_out[kind][name] if name in big_out[kind] else small_out[kind][name])
    return tuple(outs)
```

```python
import math

import numpy as np
import jax
import jax.numpy as jnp
from jax import lax
from jax.experimental import pallas as pl
from jax.experimental.pallas import tpu as pltpu

F32 = jnp.float32
BF16 = jnp.bfloat16

D_MODEL = 1024
HEAD_DIM = 64
ATTN_W = 512
REC_W = 512
N_HEADS = 8
D_FF = 3072
IN_W = 2560
REC_CONV = 4
FFN_CONV = 3
LRU_C = 8.0
ROPE_THETA = 10000.0
EPS = 1e-6
NEG_INF = -1e30
QBLK = 128
DILATIONS = (1, 4, 16)
N_DEV = 8
SMALL_ROWS = 128
ADAM_LR, ADAM_B1, ADAM_B2, ADAM_EPS, ADAM_WD, ADAM_STEP = 0.001, 0.9, 0.999, 1e-08, 0.01, 10
MESH = pl.DeviceIdType.MESH
ANY = pl.BlockSpec(memory_space=pl.ANY)


def _call(body, *, name, **kw):
    return pl.pallas_call(body, name=name, **kw)


def _params(*sem):
    return pltpu.CompilerParams(dimension_semantics=sem, vmem_limit_bytes=56 * 1024 * 1024)


_GELU_C = math.sqrt(2.0 / math.pi)
_GELU_A = 0.044715


def _gelu(x):
    return (0.5 * x) * (1.0 + jnp.tanh(x * (_GELU_C + (_GELU_C * _GELU_A) * (x * x))))


def _gelu_and_grad(x):
    x2 = x * x
    u = 1.0 + jnp.tanh(x * (_GELU_C + (_GELU_C * _GELU_A) * x2))
    hx = 0.5 * x
    return hx * u, 0.5 * u + (hx * ((2.0 - u) * u)) * (_GELU_C + (3.0 * _GELU_C * _GELU_A) * x2)


def _sigmoid(x):
    return 1.0 / (1.0 + jnp.exp(-x))


def _softplus_neg(lam):
    y = jnp.exp(-jnp.abs(lam))
    u = 1.0 + y
    log1p = jnp.where(u == 1.0, y, jnp.log(u) * y / jnp.where(u == 1.0, 1.0, u - 1.0))
    return jnp.maximum(-lam, 0.0) + log1p


_NN = (((1,), (0,)), ((), ()))
_NT = (((1,), (1,)), ((), ()))
_TN = (((0,), (0,)), ((), ()))


def _dot(a, b, dims=_NN):
    return lax.dot_general(a, b, dims, preferred_element_type=F32)


def _group_mean(v, bd):
    hi = v.astype(BF16)
    lo = (v - hi.astype(F32)).astype(BF16)
    w = bd.shape[0]
    return jnp.concatenate([_dot(hi[:, c:c + w], bd) + _dot(lo[:, c:c + w], bd) for c in range(0, v.shape[1], w)], axis=1)


def _rope_tables(pos_ref, invf_ref):
    ang = pos_ref[...].astype(F32) * invf_ref[:, :2 * HEAD_DIM]
    reps = invf_ref.shape[1] // (2 * HEAD_DIM)
    return jnp.tile(jnp.cos(ang), (1, reps)), jnp.tile(jnp.sin(ang), (1, reps))


def _shift_down(x, halo, s):
    rolled = pltpu.roll(x, s, 0)
    hr = pltpu.roll(halo, s, 0)
    row = lax.broadcasted_iota(jnp.int32, hr.shape, 0)
    first = jnp.where(row < s, hr, rolled[:8])
    return jnp.concatenate([first, rolled[8:]], axis=0)


def _shift_up(x, halo, s):
    n = x.shape[0]
    rolled = pltpu.roll(x, n - s, 0)
    hr = pltpu.roll(halo, 8 - s, 0)
    row = lax.broadcasted_iota(jnp.int32, hr.shape, 0)
    last = jnp.where(row >= 8 - s, hr, rolled[n - 8:])
    return jnp.concatenate([rolled[:n - 8], last], axis=0)


def _scan_fwd(a, u):
    n, w = a.shape
    a3, u3 = a.reshape(n // 8, 8, w), u.reshape(n // 8, 8, w)
    row = lax.broadcasted_iota(jnp.int32, a3.shape, 1)
    for s in (1, 2, 4):
        a_s = jnp.where(row < s, 1.0, pltpu.roll(a3, s, 1))
        u_s = jnp.where(row < s, 0.0, pltpu.roll(u3, s, 1))
        u3 = u3 + a3 * u_s
        a3 = a3 * a_s
    ps, hs = [a3[0]], [u3[0]]
    for k in range(1, n // 8):
        ps.append(a3[k] * ps[-1][7:8, :])
        hs.append(u3[k] + a3[k] * hs[-1][7:8, :])
    return jnp.concatenate(ps, axis=0), jnp.concatenate(hs, axis=0)


def _scan_bwd(b, v):
    n, w = b.shape
    b3, v3 = b.reshape(n // 8, 8, w), v.reshape(n // 8, 8, w)
    row = lax.broadcasted_iota(jnp.int32, b3.shape, 1)
    for s in (1, 2, 4):
        b_s = jnp.where(row >= 8 - s, 1.0, pltpu.roll(b3, 8 - s, 1))
        v_s = jnp.where(row >= 8 - s, 0.0, pltpu.roll(v3, 8 - s, 1))
        v3 = v3 + b3 * v_s
        b3 = b3 * b_s
    last = n // 8 - 1
    ps, gs = [b3[last]], [v3[last]]
    for k in range(last - 1, -1, -1):
        ps.append(b3[k] * ps[-1][0:1, :])
        gs.append(v3[k] + b3[k] * gs[-1][0:1, :])
    return jnp.concatenate(ps[::-1], axis=0), jnp.concatenate(gs[::-1], axis=0)


def _rot_half(y):
    n = y.shape[1]
    lane = lax.broadcasted_iota(jnp.int32, y.shape, 1) & (HEAD_DIM - 1)
    return jnp.where(lane < HEAD_DIM // 2, -pltpu.roll(y, n - HEAD_DIM // 2, 1), pltpu.roll(y, HEAD_DIM // 2, 1))


def _row_tile(r, cap=256):
    return max(t for t in range(16, cap + 1, 16) if r % t == 0)


def _all_gather(shards, name):
    na = len(shards)
    ms = [s.shape[0] for s in shards]

    def body(*refs):
        x_refs, out_refs = refs[:na], refs[na:2 * na]
        send_sems, recv_sems, local_sems = refs[2 * na:]
        x, y, c = lax.axis_index("x"), lax.axis_index("y"), lax.axis_index("c")
        me, sibling = (x, y, c), (x, y, 1 - c)
        chips = [(1 - x, y), (x, 1 - y), (1 - x, 1 - y)]

        def rows(a, px, py, pc):
            return out_refs[a].at[pl.ds(pl.multiple_of((4 * px + 2 * py + pc) * ms[a], 8), ms[a]), :]

        def copy(a, k, block, to, src=None):
            return pltpu.make_async_remote_copy(
                src_ref=rows(a, *block) if src is None else src, dst_ref=rows(a, *block),
                send_sem=send_sems.at[7 * a + k], recv_sem=recv_sems.at[7 * a + k], device_id=to, device_id_type=MESH)

        mine = [pltpu.make_async_copy(x_refs[a], rows(a, *me), local_sems.at[a]) for a in range(na)]
        first = []
        for a in range(na):
            mine[a].start()
            first.append(copy(a, 0, me, sibling, src=x_refs[a]))
            first += [copy(a, 1 + j, me, (*chip, c), src=x_refs[a]) for j, chip in enumerate(chips)]
        for cp in first:
            cp.start()
        passed = []
        for a in range(na):
            for j, chip in enumerate(chips):
                copy(a, 1 + j, (*chip, c), me).wait_recv()
                fw = copy(a, 4 + j, (*chip, c), sibling)
                fw.start()
                passed.append(fw)
        for a in range(na):
            copy(a, 0, sibling, me).wait_recv()
            for j, chip in enumerate(chips):
                copy(a, 4 + j, (*chip, 1 - c), me).wait_recv()
        for cp in first + passed:
            cp.wait_send()
        for cp in mine:
            cp.wait()

    return _call(
        body, name=name, out_shape=[jax.ShapeDtypeStruct((N_DEV * s.shape[0], s.shape[1]), s.dtype) for s in shards],
        in_specs=[ANY] * na, out_specs=[ANY] * na,
        scratch_shapes=[pltpu.SemaphoreType.DMA((7 * na,)), pltpu.SemaphoreType.DMA((7 * na,)),
                        pltpu.SemaphoreType.DMA((na,))],
    )(*shards)


HBM = pl.BlockSpec(memory_space=pltpu.HBM)
SEM = pl.BlockSpec(memory_space=pltpu.SEMAPHORE)
EFFECT = pltpu.SideEffectType.DATAFLOW_SIDE_EFFECTING
N_PEERS = N_DEV - 1


def _peer(k):
    x, y, c = lax.axis_index("x"), lax.axis_index("y"), lax.axis_index("c")
    b = k + 1
    flip = lambda v, bit: 1 - v if bit else v
    return flip(x, b & 4), flip(y, b & 2), flip(c, b & 1)


def _in_hbm(a):
    return pltpu.with_memory_space_constraint(a, pltpu.HBM)


def _split_copy_descr(na, kind, src_refs, land_refs, send_sems, recv_sems):
    x, y, c = lax.axis_index("x"), lax.axis_index("y"), lax.axis_index("c")
    me = 4 * x + 2 * y + c
    copies = []
    for a in range(na):
        for k in range(N_PEERS):
            px, py, pc = _peer(k)
            if kind == "gather":
                m = src_refs[a].shape[0]
                src, dst = src_refs[a], land_refs[a].at[pl.ds(pl.multiple_of(me * m, 8), m), :]
            else:
                src, dst = src_refs[a].at[4 * px + 2 * py + pc], land_refs[a].at[k]
            copies.append(pltpu.make_async_remote_copy(
                src_ref=src, dst_ref=dst, send_sem=send_sems.at[N_PEERS * a + k], recv_sem=recv_sems.at[N_PEERS * a + k],
                device_id=(px, py, pc), device_id_type=MESH))
    return copies


def _landing(shape, dtype, own=None, at=None):
    buf = lax.empty(shape, dtype)
    return buf if own is None else lax.dynamic_update_slice(buf, own, (at, 0))


def _exchange_start(srcs, lands, kind, after, name):
    na = len(srcs)
    land_shapes = [l.shape for l in lands]

    def body(*refs):
        src_refs, land_refs = refs[:na], refs[na:2 * na]
        send_sems, recv_sems = refs[2 * na + 1], refs[2 * na + 2]
        token = refs[-1]
        for cp in _split_copy_descr(na, kind, src_refs, land_refs, send_sems, recv_sems):
            cp.start()
        token[...] = jnp.zeros_like(token)

    lands = [_in_hbm(l) for l in lands]
    sem = pltpu.SemaphoreType.DMA((N_PEERS * na,))
    outs = _call(
        body, name=name,
        out_shape=[sem, sem] + [pltpu.HBM(s.shape, s.dtype) for s in srcs] + [pltpu.HBM(s, srcs[0].dtype) for s in land_shapes]
        + [jax.ShapeDtypeStruct((8, 128), F32)],
        in_specs=[HBM] * (2 * na) + [ANY], out_specs=[SEM, SEM] + [HBM] * (2 * na) + [pl.BlockSpec(memory_space=pltpu.VMEM)],
        input_output_aliases={i: 2 + i for i in range(2 * na)},
        compiler_params=pltpu.CompilerParams(has_side_effects=EFFECT),
    )(*[_in_hbm(s) for s in srcs], *lands, after)
    return outs[0], outs[1], outs[2:2 + na], outs[2 + na:2 + 2 * na], outs[-1]


def _exchange_wait(send_sems, recv_sems, srcs, lands, kind, after, name):
    na = len(srcs)

    def body(*refs):
        src_refs, land_refs = refs[:na], refs[na:2 * na]
        s_sems, r_sems = refs[2 * na], refs[2 * na + 1]
        for cp in _split_copy_descr(na, kind, src_refs, land_refs, s_sems, r_sems):
            cp.wait_send()
            cp.wait_recv()

    outs = _call(
        body, name=name, out_shape=[pltpu.HBM(s.shape, s.dtype) for s in srcs] + [pltpu.HBM(l.shape, l.dtype) for l in lands],
        in_specs=[HBM] * (2 * na) + [SEM, SEM, ANY], out_specs=[HBM] * (2 * na),
        input_output_aliases={i: i for i in range(2 * na)},
        compiler_params=pltpu.CompilerParams(has_side_effects=EFFECT),
    )(*srcs, *lands, send_sems, recv_sems, after)
    return outs[:na], outs[na:]


def _mm(a, b, mode, out_dtype, name, add=None, tm=1024, tn=1024, tk=1024, into=None, o_rows=None, o_moff=0,
        loss_target=None):
    if mode == "tn":
        K, M = a.shape
    else:
        M, K = a.shape
    N = b.shape[0] if mode == "nt" else b.shape[1]
    tm, tn, tk = min(tm, M), min(tn, N), min(tk, K)
    assert M % tm == 0 and N % tn == 0 and K % tk == 0, (name, M, N, K)
    nk = K // tk
    if mode == "nn":
        a_spec = pl.BlockSpec((tm, tk), lambda i, j, kk: (i, kk))
        b_spec, dims = pl.BlockSpec((tk, tn), lambda i, j, kk: (kk, j)), _NN
    elif mode == "nt":
        a_spec = pl.BlockSpec((tm, tk), lambda i, j, kk: (i, kk))
        b_spec, dims = pl.BlockSpec((tn, tk), lambda i, j, kk: (j, kk)), _NT
    else:
        a_spec = pl.BlockSpec((tk, tm), lambda i, j, kk: (kk, i))
        b_spec, dims = pl.BlockSpec((tk, tn), lambda i, j, kk: (kk, j)), _TN
    o_spec = pl.BlockSpec((tm, tn), lambda i, j, kk: (i + o_moff, j))
    has_add, has_into, has_loss = add is not None, into is not None, loss_target is not None
    assert not has_loss or (has_add and tn == N and not has_into)
    n_in = 2 + has_add + has_loss + has_into

    def body(*refs):
        a_ref, b_ref = refs[0], refs[1]
        add_ref = refs[2] if has_add else None
        outs = refs[n_in:]

        def finish(r):
            if has_add:
                r = r + add_ref[...]
            if has_loss:
                e = r - refs[3][...]
                dy = e * (1.0 / N)
                outs[0][...] = dy
                outs[1][...] = dy.astype(BF16)
                outs[2][...] = jnp.sum(e * e, axis=0, keepdims=True)[None]
            else:
                outs[0][...] = r.astype(out_dtype)

        if nk == 1:
            finish(_dot(a_ref[...], b_ref[...], dims))
        else:
            acc = refs[-1]
            kk = pl.program_id(2)

            @pl.when(kk == 0)
            def _():
                acc[...] = _dot(a_ref[...], b_ref[...], dims)

            @pl.when((kk > 0) & (kk < nk - 1))
            def _():
                acc[...] += _dot(a_ref[...], b_ref[...], dims)

            @pl.when(kk == nk - 1)
            def _():
                finish(acc[...] + _dot(a_ref[...], b_ref[...], dims))

    tile = pl.BlockSpec((tm, tn), lambda i, j, kk: (i, j))
    ins = [a, b] + ([add] if has_add else []) + ([loss_target] if has_loss else []) + ([into] if has_into else [])
    specs = [a_spec, b_spec] + [tile] * (has_add + has_loss) + ([ANY] if has_into else [])
    rows = into.shape[0] if has_into else (o_rows if o_rows is not None else M)
    if has_loss:
        out_specs = [tile, tile, pl.BlockSpec((1, 1, N), lambda i, j, kk: (i, 0, 0))]
        out_shape = [jax.ShapeDtypeStruct((M, N), F32), jax.ShapeDtypeStruct((M, N), BF16), jax.ShapeDtypeStruct((M // tm, 1, N), F32)]
    else:
        out_specs, out_shape = o_spec, jax.ShapeDtypeStruct((rows, N), out_dtype)
    return _call(
        body, name=name, grid=(M // tm, N // tn, nk), in_specs=specs, out_specs=out_specs, out_shape=out_shape,
        scratch_shapes=[pltpu.VMEM((tm, tn), F32)] if nk > 1 else [],
        input_output_aliases={len(ins) - 1: 0} if has_into else {},
        compiler_params=_params("parallel", "parallel", "arbitrary"),
    )(*ins)


def _mm_norm_bwd(parts, b, x, resid, g, name, tm=512, tk=512):
    T, N = x.shape
    counts = [p.shape[1] // tk for p in parts]
    starts = [sum(counts[:i]) for i in range(len(parts))]
    nsteps = sum(counts)
    assert all(p.shape[1] % tk == 0 for p in parts) and b.shape == (nsteps * tk, N)
    npart = len(parts)

    def body(*refs):
        a_refs, b_ref, x_ref, res_ref, g_ref = refs[:npart], refs[npart], refs[npart + 1], refs[npart + 2], refs[npart + 3]
        dx_ref, dxb_ref, dg_ref, acc = refs[npart + 4:]
        i, s = pl.program_id(0), pl.program_id(1)

        @pl.when((i == 0) & (s == 0))
        def _():
            dg_ref[...] = jnp.zeros_like(dg_ref)

        for p in range(npart):
            @pl.when((s >= starts[p]) & (s < starts[p] + counts[p]))
            def _(p=p):
                d = _dot(a_refs[p][...], b_ref[...])

                @pl.when(s == 0)
                def _():
                    acc[...] = d

                @pl.when(s > 0)
                def _():
                    acc[...] += d

        @pl.when(s == nsteps - 1)
        def _():
            xv, dhv = x_ref[...], acc[...]
            r = lax.rsqrt(jnp.mean(xv * xv, axis=-1, keepdims=True) + EPS)
            gd = dhv * g_ref[...]
            m = jnp.mean(gd * xv, axis=-1, keepdims=True)
            dx = res_ref[...] + r * gd - xv * (r * r * r) * m
            dx_ref[...] = dx
            dxb_ref[...] = dx.astype(BF16)
            dg_ref[...] += jnp.sum(dhv * xv * r, axis=0, keepdims=True)

    a_specs = [pl.BlockSpec((tm, tk), lambda i, s, st=st, c=c: (i, jnp.clip(s - st, 0, c - 1))) for st, c in zip(starts, counts)]
    row = pl.BlockSpec((tm, N), lambda i, s: (i, 0))
    vec = pl.BlockSpec((1, N), lambda i, s: (0, 0))
    return _call(
        body, name=name, grid=(T // tm, nsteps),
        in_specs=a_specs + [pl.BlockSpec((tk, N), lambda i, s: (s, 0)), row, row, vec], out_specs=[row, row, vec],
        out_shape=[jax.ShapeDtypeStruct((T, N), F32), jax.ShapeDtypeStruct((T, N), BF16), jax.ShapeDtypeStruct((1, N), F32)],
        scratch_shapes=[pltpu.VMEM((tm, N), F32)], compiler_params=_params("arbitrary", "arbitrary"),
    )(*parts, b, x, resid, g)


def _norm_proj(x, g, wT, name, tm=1024, tn=1280):
    T, K = x.shape
    N = wT.shape[0]

    def body(x_ref, g_ref, w_ref, o_ref, h_ref):
        xv = x_ref[...]
        r = lax.rsqrt(jnp.mean(xv * xv, axis=-1, keepdims=True) + EPS)
        hv = (xv * r * g_ref[...]).astype(BF16)

        @pl.when(pl.program_id(1) == 0)
        def _():
            h_ref[...] = hv

        o_ref[...] = _dot(hv, w_ref[...], _NT)

    return _call(
        body, name=name, grid=(T // tm, N // tn),
        in_specs=[pl.BlockSpec((tm, K), lambda i, j: (i, 0)), pl.BlockSpec((1, K), lambda i, j: (0, 0)),
                  pl.BlockSpec((tn, K), lambda i, j: (j, 0))],
        out_specs=[pl.BlockSpec((tm, tn), lambda i, j: (i, j)), pl.BlockSpec((tm, K), lambda i, j: (i, 0))],
        out_shape=[jax.ShapeDtypeStruct((T, N), F32), jax.ShapeDtypeStruct((T, K), BF16)],
        compiler_params=_params("parallel", "arbitrary"),
    )(x, g, wT)


def _qk_prep(proj, pos, invf, qg, kg, bd, name, tm=1024):
    T = proj.shape[0]

    def body(q_ref, k_ref, pos_ref, invf_ref, qg_ref, kg_ref, bd_ref, qo_ref, ko_ref):
        cos, sin = _rope_tables(pos_ref, invf_ref)

        def prep(xv, gv, scale):
            r = lax.rsqrt(_group_mean(xv * xv, bd_ref[...]) + EPS)
            yv = xv * r * gv
            return ((yv * cos + _rot_half(yv) * sin) * scale).astype(BF16).astype(F32)

        qo_ref[...] = prep(q_ref[...], qg_ref[...], HEAD_DIM ** -0.5)
        ko_ref[...] = prep(k_ref[...], kg_ref[...], 1.0)

    col = lambda j: pl.BlockSpec((tm, ATTN_W), lambda i, j=j: (i, j))
    vec = pl.BlockSpec((1, ATTN_W), lambda i: (0, 0))
    out = pl.BlockSpec((tm, ATTN_W), lambda i: (i, 0))
    return _call(
        body, name=name, grid=(T // tm,),
        in_specs=[col(0), col(1), pl.BlockSpec((tm, 1), lambda i: (i, 0)), vec, vec, vec,
                  pl.BlockSpec((2 * HEAD_DIM, 2 * HEAD_DIM), lambda i: (0, 0))],
        out_specs=[out, out], out_shape=[jax.ShapeDtypeStruct((T, ATTN_W), F32)] * 2,
        compiler_params=_params("parallel"),
    )(proj, proj, pos, invf, qg, kg, bd)


def _qk_prep_bwd(proj, dqh, dkh, dv, pos, invf, qg, kg, bd, name, tm=512):
    T = proj.shape[0]

    def body(q_ref, k_ref, dq_ref, dk_ref, dv_ref, pos_ref, invf_ref, qg_ref, kg_ref, bd_ref, o_ref, gq_ref, gk_ref):
        @pl.when(pl.program_id(0) == 0)
        def _():
            gq_ref[...] = jnp.zeros_like(gq_ref)
            gk_ref[...] = jnp.zeros_like(gk_ref)

        cos, sin = _rope_tables(pos_ref, invf_ref)

        def back(xv, gv, dz, scale):
            dz = dz * scale
            dy = dz * cos - _rot_half(dz * sin)
            r = lax.rsqrt(_group_mean(xv * xv, bd_ref[...]) + EPS)
            gd = dy * gv
            m = _group_mean(gd * xv, bd_ref[...])
            dx = r * gd - xv * (r * r * r) * m
            return dx, jnp.sum(dy * xv * r, axis=0, keepdims=True)

        dxq, gs = back(q_ref[...], qg_ref[...], dq_ref[...], HEAD_DIM ** -0.5)
        gq_ref[...] += gs
        dxk, gs = back(k_ref[...], kg_ref[...], dk_ref[...], 1.0)
        gk_ref[...] += gs
        o_ref[...] = jnp.concatenate([dxq.astype(BF16), dxk.astype(BF16), dv_ref[...].astype(BF16)], axis=1)

    col = lambda j: pl.BlockSpec((tm, ATTN_W), lambda i, j=j: (i, j))
    row = pl.BlockSpec((tm, ATTN_W), lambda i: (i, 0))
    vec = pl.BlockSpec((1, ATTN_W), lambda i: (0, 0))
    return _call(
        body, name=name, grid=(T // tm,),
        in_specs=[col(0), col(1), row, row, row, pl.BlockSpec((tm, 1), lambda i: (i, 0)), vec, vec, vec,
                  pl.BlockSpec((2 * HEAD_DIM, 2 * HEAD_DIM), lambda i: (0, 0))],
        out_specs=[pl.BlockSpec((tm, 3 * ATTN_W), lambda i: (i, 0)), vec, vec],
        out_shape=[jax.ShapeDtypeStruct((T, 3 * ATTN_W), BF16)] + [jax.ShapeDtypeStruct((1, ATTN_W), F32)] * 2,
        compiler_params=_params("arbitrary"),
    )(proj, proj, dqh, dkh, dv, pos, invf, qg, kg, bd)


def _ld(ref, start, size, dil):
    return ref[pl.ds(start, size), :] if dil == 1 else ref[pl.ds(start, size, stride=dil), :]


def _st(ref, start, size, dil, val):
    if dil == 1:
        ref[pl.ds(start, size), :] = val
    else:
        ref[pl.ds(start, size, stride=dil), :] = val


def _attn_geometry(T, dil):
    nb = T // dil // QBLK
    if nb == 2:
        return 1, 2 * QBLK, 2 * QBLK
    return nb, QBLK, (2 * QBLK if nb >= 2 else QBLK)


ATTN_UNROLL = 4


def _attn_unit(j, u, dil, nit):
    return ATTN_UNROLL * j + u if dil >= ATTN_UNROLL else j + u * (nit // ATTN_UNROLL)


def _attn_block(it, dil, qb, kw):
    c, n = it & (dil - 1), lax.shift_right_logical(it, dil.bit_length() - 1)
    sq = n * (qb * dil) + c
    sk = jnp.maximum(n - (kw // qb - 1), 0) * (qb * dil) + c
    qi = lax.broadcasted_iota(jnp.int32, (2 * qb, kw), 0) & (qb - 1)
    kj = lax.broadcasted_iota(jnp.int32, (2 * qb, kw), 1)
    rel = jnp.where(n > 0, kw - qb, 0) + qi - kj
    return sq, sk, (rel >= 0) & (rel <= QBLK)


def _stack_heads(xv, head0):
    z = jnp.zeros_like(xv)
    return jnp.concatenate([jnp.where(head0, xv, z), jnp.where(head0, z, xv)], axis=0)


def _unstack_heads(x2, head0):
    qb = x2.shape[0] // 2
    return jnp.where(head0, x2[:qb], x2[qb:])


def _attn_fwd(qf, kf, proj, name):
    T = qf.shape[0]

    def body(q_ref, k_ref, v_ref, o_ref, lse_ref):
        for bi, dil in enumerate(DILATIONS):
            nb, qb, kw = _attn_geometry(T, dil)
            nit = nb * dil
            head0 = lax.broadcasted_iota(jnp.int32, (qb, 2 * HEAD_DIM), 1) < HEAD_DIM

            def step(j, carry, bi=bi, dil=dil, qb=qb, kw=kw, nit=nit, head0=head0):
                units = []
                for u in range(ATTN_UNROLL):
                    sq, sk, ok = _attn_block(_attn_unit(j, u, dil, nit), dil, qb, kw)
                    old = (_ld(o_ref, sq, qb, dil), _ld(lse_ref, sq, qb, dil)) if bi > 0 else None
                    units.append((sq, ok, _ld(q_ref, sq, qb, dil).astype(BF16), _ld(k_ref, sk, kw, dil).astype(BF16),
                                  _ld(v_ref, sk, kw, dil).astype(BF16), old))
                results = []
                for sq, ok, qv, kv, vv, old in units:
                    s = jnp.where(ok, _dot(_stack_heads(qv, head0), kv, _NT), NEG_INF)
                    m = jnp.max(s, axis=-1, keepdims=True)
                    p = jnp.exp(s - m).astype(BF16)
                    acc = _dot(p, jnp.concatenate([vv, jnp.ones_like(vv)], axis=1))
                    l = acc[:, 2 * HEAD_DIM:]
                    o_new = _unstack_heads(acc[:, :2 * HEAD_DIM] / l, head0)
                    l_new = _unstack_heads(m + jnp.log(l), head0)
                    if bi > 0:
                        o_old, l_old = old
                        mx = jnp.maximum(l_old, l_new)
                        e0, e1 = jnp.exp(l_old - mx), jnp.exp(l_new - mx)
                        z = e0 + e1
                        o_new = (e0 * o_old + e1 * o_new) / z
                        l_new = mx + jnp.log(z)
                    results.append((sq, o_new, l_new))
                for sq, o_new, l_new in results:
                    _st(o_ref, sq, qb, dil, o_new)
                    _st(lse_ref, sq, qb, dil, l_new)
                return carry

            lax.fori_loop(0, nit // ATTN_UNROLL, step, 0)

    blk = lambda off: pl.BlockSpec((T, 2 * HEAD_DIM), lambda hp, off=off: (0, off + hp))
    return _call(
        body, name=name, grid=(4,), in_specs=[blk(0), blk(0), blk(8)], out_specs=[blk(0), blk(0)],
        out_shape=[jax.ShapeDtypeStruct((T, ATTN_W), F32)] * 2, compiler_params=_params("parallel"),
    )(qf, kf, proj)


def _attn_bwd(qf, kf, proj, do, lse, delta, name):
    T = qf.shape[0]

    def body(q_ref, k_ref, v_ref, do_ref, lse_ref, dl_ref, dq_ref, dk_ref, dv_ref):
        for ref in (dq_ref, dk_ref, dv_ref):
            ref[...] = jnp.zeros_like(ref)
        for dil in DILATIONS:
            nb, qb, kw = _attn_geometry(T, dil)
            nit = nb * dil
            head0 = lax.broadcasted_iota(jnp.int32, (qb, 2 * HEAD_DIM), 1) < HEAD_DIM

            def step(j, carry, dil=dil, qb=qb, kw=kw, nit=nit, head0=head0):
                units = []
                for u in range(ATTN_UNROLL):
                    sq, sk, ok = _attn_block(_attn_unit(j, u, dil, nit), dil, qb, kw)
                    lsev, dlv = _ld(lse_ref, sq, qb, dil), _ld(dl_ref, sq, qb, dil)
                    units.append((sq, sk, ok, _ld(q_ref, sq, qb, dil).astype(BF16), _ld(do_ref, sq, qb, dil).astype(BF16),
                                  jnp.concatenate([lsev[:, 0:1], lsev[:, HEAD_DIM:HEAD_DIM + 1]], axis=0),
                                  jnp.concatenate([dlv[:, 0:1], dlv[:, HEAD_DIM:HEAD_DIM + 1]], axis=0),
                                  _ld(k_ref, sk, kw, dil).astype(BF16), _ld(v_ref, sk, kw, dil).astype(BF16),
                                  _ld(dq_ref, sq, qb, dil), _ld(dk_ref, sk, kw, dil), _ld(dv_ref, sk, kw, dil)))
                results = []
                for sq, sk, ok, qv, dov, lse2, dl2, kv, vv, dq0, dk0, dv0 in units:
                    q2, do2 = _stack_heads(qv, head0), _stack_heads(dov, head0)
                    p = jnp.where(ok, jnp.exp(_dot(q2, kv, _NT) - lse2), 0.0)
                    ds = (p * (_dot(do2, vv, _NT) - dl2)).astype(BF16)
                    results.append((sq, sk, dq0 + _unstack_heads(_dot(ds, kv), head0),
                                    dk0 + _dot(ds, q2, _TN), dv0 + _dot(p.astype(BF16), do2, _TN)))
                for sq, sk, dq, dk, dv in results:
                    _st(dq_ref, sq, qb, dil, dq)
                    _st(dk_ref, sk, kw, dil, dk)
                    _st(dv_ref, sk, kw, dil, dv)
                return carry

            lax.fori_loop(0, nit // ATTN_UNROLL, step, 0)

    blk = lambda off: pl.BlockSpec((T, 2 * HEAD_DIM), lambda hp, off=off: (0, off + hp))
    return _call(
        body, name=name, grid=(4,), in_specs=[blk(0), blk(0), blk(8), blk(0), blk(0), blk(0)], out_specs=[blk(0)] * 3,
        out_shape=[jax.ShapeDtypeStruct((T, ATTN_W), F32)] * 3, compiler_params=_params("parallel"),
    )(qf, kf, proj, do, lse, delta)


def _attn_norm(attn, g, name, tm=1024):
    T = attn.shape[0]

    def body(a_ref, g_ref, o_ref):
        av = a_ref[...]
        r = lax.rsqrt(jnp.mean(av * av, axis=-1, keepdims=True) + EPS)
        o_ref[...] = (av * r * g_ref[...]).astype(BF16)

    row = pl.BlockSpec((tm, ATTN_W), lambda i: (i, 0))
    return _call(
        body, name=name, grid=(T // tm,), in_specs=[row, pl.BlockSpec((1, ATTN_W), lambda i: (0, 0))], out_specs=row,
        out_shape=jax.ShapeDtypeStruct((T, 2 * ATTN_W), BF16), compiler_params=_params("parallel"),
    )(attn, g)


def _attn_norm_bwd(dmix, attn, g, bd, name, tm=1024):
    T = attn.shape[0]

    def body(d_ref, a_ref, g_ref, bd_ref, do_ref, dl_ref, dg_ref):
        @pl.when(pl.program_id(0) == 0)
        def _():
            dg_ref[...] = jnp.zeros_like(dg_ref)

        dy, av = d_ref[...], a_ref[...]
        r = lax.rsqrt(jnp.mean(av * av, axis=-1, keepdims=True) + EPS)
        gd = dy * g_ref[...]
        m = jnp.mean(gd * av, axis=-1, keepdims=True)
        da = r * gd - av * (r * r * r) * m
        do_ref[...] = da
        dl_ref[...] = _group_mean(da * av, bd_ref[...]) * float(HEAD_DIM)
        dg_ref[...] += jnp.sum(dy * av * r, axis=0, keepdims=True)

    row = pl.BlockSpec((tm, ATTN_W), lambda i: (i, 0))
    vec = pl.BlockSpec((1, ATTN_W), lambda i: (0, 0))
    return _call(
        body, name=name, grid=(T // tm,),
        in_specs=[row, row, vec, pl.BlockSpec((2 * HEAD_DIM, 2 * HEAD_DIM), lambda i: (0, 0))], out_specs=[row, row, vec],
        out_shape=[jax.ShapeDtypeStruct((T, ATTN_W), F32)] * 2 + [jax.ShapeDtypeStruct((1, ATTN_W), F32)],
        compiler_params=_params("arbitrary"),
    )(dmix, attn, g, bd)


def _rec_gates(xc, wrg_ref, wig_ref, brg_ref, big_ref, lam_ref):
    xb = xc.astype(BF16)
    r = _sigmoid(_dot(xb, wrg_ref[...]) + brg_ref[...])
    ig = _sigmoid(_dot(xb, wig_ref[...]) + big_ref[...])
    sp = _softplus_neg(lam_ref[...])
    log_a = -LRU_C * r * sp
    a = jnp.exp(log_a)
    th = jnp.tanh(log_a)
    mult = jnp.sqrt(-2.0 * th / (1.0 - th))
    return xb, r, ig, sp, a, mult


def _rec_fwd(proj, mix, cw, cb, wrg, wig, brg, big, lam, g, name, tm=512):
    T = proj.shape[0]
    hb = tm // 8

    def body(xr_ref, halo_ref, gr_ref, cw_ref, cb_ref, wrg_ref, wig_ref, brg_ref, big_ref, lam_ref, g_ref, mix_ref,
             xc_ref, h_ref, out_ref, carry):
        i = pl.program_id(0)

        @pl.when(i == 0)
        def _():
            carry[...] = jnp.zeros_like(carry)

        xr = xr_ref[...]
        halo = jnp.where(i > 0, halo_ref[...], 0.0)
        xc = cb_ref[...] + cw_ref[3:4, :] * xr
        for s in range(1, REC_CONV):
            xc = xc + cw_ref[3 - s:4 - s, :] * _shift_down(xr, halo, s)
        xc_ref[...] = xc
        _, _, ig, _, a, mult = _rec_gates(xc, wrg_ref, wig_ref, brg_ref, big_ref, lam_ref)
        pa, hl = _scan_fwd(a, mult * (ig * xc))
        h = hl + pa * carry[0:1, :]
        h_ref[...] = h
        carry[0:1, :] = h_ref[pl.ds(tm - 1, 1), :]
        hg = h * _gelu(gr_ref[...])
        r = lax.rsqrt(jnp.mean(hg * hg, axis=-1, keepdims=True) + EPS)
        out_ref[...] = (hg * r * g_ref[...]).astype(BF16)

    vec = pl.BlockSpec((1, REC_W), lambda i: (0, 0))
    row = pl.BlockSpec((tm, REC_W), lambda i: (i, 0))
    mat = pl.BlockSpec((REC_W, REC_W), lambda i: (0, 0))
    return _call(
        body, name=name, grid=(T // tm,),
        in_specs=[pl.BlockSpec((tm, REC_W), lambda i: (i, 3)),
                  pl.BlockSpec((8, REC_W), lambda i: (jnp.maximum(i * hb - 1, 0), 3)),
                  pl.BlockSpec((tm, REC_W), lambda i: (i, 4)),
                  pl.BlockSpec((8, REC_W), lambda i: (0, 0)), vec, mat, mat, vec, vec, vec, vec, ANY],
        out_specs=[row, row, pl.BlockSpec((tm, REC_W), lambda i: (i, 1))],
        out_shape=[jax.ShapeDtypeStruct((T, REC_W), F32)] * 2 + [jax.ShapeDtypeStruct(mix.shape, BF16)],
        scratch_shapes=[pltpu.VMEM((8, REC_W), F32)], input_output_aliases={11: 2},
        compiler_params=_params("arbitrary"),
    )(proj, proj, proj, cw, cb, wrg, wig, brg, big, lam, g, mix)


def _rec_bwd(dmix, proj, xc, h, cw, cb, wrg, wig, brg, big, lam, g, name, tm=512):
    T = proj.shape[0]
    nt = T // tm
    hb = tm // 8

    def body(d_ref, xr_ref, xhalo_ref, gr_ref, xc_ref, h_ref, hhalo_ref, cw_ref, cb_ref, wrg_ref, wig_ref, brg_ref,
             big_ref, lam_ref, g_ref,
             drec_ref, gcw_ref, gcb_ref, gwrg_ref, gwig_ref, gbrg_ref, gbig_ref, glam_ref, gg_ref,
             g_carry, a_first, dxc_next, gsp):
        i = pl.program_id(0)
        first_tile = i == nt - 1

        @pl.when(i == 0)
        def _():
            for ref in (gcw_ref, gcb_ref, gwrg_ref, gwig_ref, gbrg_ref, gbig_ref, glam_ref, gg_ref,
                        g_carry, a_first, dxc_next, gsp):
                ref[...] = jnp.zeros_like(ref)

        xr, xc, hv = xr_ref[...], xc_ref[...], h_ref[...]
        xhalo = jnp.where(first_tile, 0.0, xhalo_ref[...])
        hhalo = jnp.where(first_tile, 0.0, hhalo_ref[...])
        xb, r, ig, sp, a, mult = _rec_gates(xc, wrg_ref, wig_ref, brg_ref, big_ref, lam_ref)
        h_prev = _shift_down(hv, hhalo, 1)
        ge, dge = _gelu_and_grad(gr_ref[...])
        hg = hv * ge
        rr = lax.rsqrt(jnp.mean(hg * hg, axis=-1, keepdims=True) + EPS)
        dy = d_ref[...]
        gd = dy * g_ref[...]
        dhg = rr * gd - hg * (rr * rr * rr) * jnp.mean(gd * hg, axis=-1, keepdims=True)
        gg_ref[...] += jnp.sum(dy * hg * rr, axis=0, keepdims=True)
        dgr = (dhg * hv * dge).astype(BF16)
        dh = dhg * ge
        b = _shift_up(a, jnp.broadcast_to(a_first[0:1, :], (8, REC_W)), 1)
        pb, gl = _scan_bwd(b, dh)
        gs = gl + pb * g_carry[0:1, :]
        g_carry[0:1, :] = gs[0:1, :]
        a_first[0:1, :] = a[0:1, :]
        da = gs * h_prev
        dmult = gs * (ig * xc)
        di = gs * (mult * xc)
        dxc = gs * (mult * ig)
        dlog_a = da * a - dmult * (a * a) / mult
        gsp[...] += jnp.sum(dlog_a * (-LRU_C * r), axis=0, keepdims=True)
        dzr = (dlog_a * (-LRU_C * sp)) * (r * (1.0 - r))
        dzi = di * (ig * (1.0 - ig))
        dzr_b, dzi_b = dzr.astype(BF16), dzi.astype(BF16)
        dxc = dxc + _dot(dzr_b, wrg_ref[...], _NT) + _dot(dzi_b, wig_ref[...], _NT)
        gwrg_ref[...] += _dot(xb, dzr_b, _TN)
        gwig_ref[...] += _dot(xb, dzi_b, _TN)
        gbrg_ref[...] += jnp.sum(dzr, axis=0, keepdims=True)
        gbig_ref[...] += jnp.sum(dzi, axis=0, keepdims=True)
        nxt = dxc_next[...]
        dxr = cw_ref[3:4, :] * dxc
        gcw_ref[3:4, :] += jnp.sum(dxc * xr, axis=0, keepdims=True)
        for s in range(1, REC_CONV):
            dxr = dxr + cw_ref[3 - s:4 - s, :] * _shift_up(dxc, nxt, s)
            gcw_ref[3 - s:4 - s, :] += jnp.sum(dxc * _shift_down(xr, xhalo, s), axis=0, keepdims=True)
        gcb_ref[...] += jnp.sum(dxc, axis=0, keepdims=True)
        dxc_next[...] = dxc[:8]
        drec_ref[...] = jnp.concatenate([dxr.astype(BF16), dgr], axis=1)

        @pl.when(first_tile)
        def _():
            glam_ref[...] = gsp[...] * (-_sigmoid(-lam_ref[...]))

    rev = lambda i: nt - 1 - i
    vec = pl.BlockSpec((1, REC_W), lambda i: (0, 0))
    row = pl.BlockSpec((tm, REC_W), lambda i: (rev(i), 0))
    mat = pl.BlockSpec((REC_W, REC_W), lambda i: (0, 0))
    cwb = pl.BlockSpec((8, REC_W), lambda i: (0, 0))
    halo = lambda c: pl.BlockSpec((8, REC_W), lambda i, c=c: (jnp.maximum(rev(i) * hb - 1, 0), c))
    return _call(
        body, name=name, grid=(nt,),
        in_specs=[pl.BlockSpec((tm, REC_W), lambda i: (rev(i), 1)),
                  pl.BlockSpec((tm, REC_W), lambda i: (rev(i), 3)), halo(3),
                  pl.BlockSpec((tm, REC_W), lambda i: (rev(i), 4)),
                  row, row, halo(0), cwb, vec, mat, mat, vec, vec, vec, vec],
        out_specs=[pl.BlockSpec((tm, 2 * REC_W), lambda i: (rev(i), 0)), cwb, vec, mat, mat, vec, vec, vec, vec],
        out_shape=[jax.ShapeDtypeStruct((T, 2 * REC_W), BF16)]
        + [jax.ShapeDtypeStruct((8, REC_W), F32), jax.ShapeDtypeStruct((1, REC_W), F32)]
        + [jax.ShapeDtypeStruct((REC_W, REC_W), F32)] * 2 + [jax.ShapeDtypeStruct((1, REC_W), F32)] * 4,
        scratch_shapes=[pltpu.VMEM((8, REC_W), F32)] * 3 + [pltpu.VMEM((1, REC_W), F32)],
        compiler_params=_params("arbitrary"),
    )(dmix, proj, proj, proj, xc, h, h, cw, cb, wrg, wig, brg, big, lam, g)


def _ffn_conv(x_ext, cw_ref, cb_ref):
    return (cb_ref[...] + cw_ref[2:3, :] * x_ext + cw_ref[1:2, :] * pltpu.roll(x_ext, 1, 0)
            + cw_ref[0:1, :] * pltpu.roll(x_ext, 2, 0))


def _up_proj_act(x2, g, w_upT, cw, cb, name, tm=1024, tc=768):
    T = x2.shape[0]
    nc = D_FF // tc

    def body(x_ref, g_ref, wg_ref, wu_ref, cwg_ref, cwu_ref, cbg_ref, cbu_ref, act_ref, da_ref, db_ref, pg_ref, pu_ref,
             h_ref, hist_g, hist_u, hs):
        i, j = pl.program_id(0), pl.program_id(1)

        @pl.when(j == 0)
        def _():
            xv = x_ref[...]
            r = lax.rsqrt(jnp.mean(xv * xv, axis=-1, keepdims=True) + EPS)
            hs[...] = (xv * r * g_ref[...]).astype(BF16)
            h_ref[...] = hs[...]

        hv = hs[...]
        pg, pu = _dot(hv, wg_ref[...], _NT), _dot(hv, wu_ref[...], _NT)
        ge = jnp.concatenate([jnp.where(i > 0, hist_g[j], 0.0), pg], axis=0)
        ue = jnp.concatenate([jnp.where(i > 0, hist_u[j], 0.0), pu], axis=0)
        gel, dgel = _gelu_and_grad(_ffn_conv(ge, cwg_ref, cbg_ref)[8:])
        uu = _ffn_conv(ue, cwu_ref, cbu_ref)[8:]
        act_ref[...] = (gel * uu).astype(BF16)
        da_ref[...] = (uu * dgel).astype(BF16)
        db_ref[...] = gel.astype(BF16)
        pg_ref[...] = pg.astype(BF16)
        pu_ref[...] = pu.astype(BF16)
        hist_g[j] = pg[tm - 8:]
        hist_u[j] = pu[tm - 8:]

    tile = pl.BlockSpec((tm, tc), lambda i, j: (i, j))
    wsp = lambda off: pl.BlockSpec((tc, D_MODEL), lambda i, j, off=off: (j + off, 0))
    cws = lambda off: pl.BlockSpec((8, tc), lambda i, j, off=off: (0, j + off))
    cbs = lambda off: pl.BlockSpec((1, tc), lambda i, j, off=off: (0, j + off))
    return _call(
        body, name=name, grid=(T // tm, nc),
        in_specs=[pl.BlockSpec((tm, D_MODEL), lambda i, j: (i, 0)), pl.BlockSpec((1, D_MODEL), lambda i, j: (0, 0)),
                  wsp(0), wsp(nc), cws(0), cws(nc), cbs(0), cbs(nc)],
        out_specs=[tile] * 5 + [pl.BlockSpec((tm, D_MODEL), lambda i, j: (i, 0))],
        out_shape=[jax.ShapeDtypeStruct((T, D_FF), BF16)] * 5 + [jax.ShapeDtypeStruct((T, D_MODEL), BF16)],
        scratch_shapes=[pltpu.VMEM((nc, 8, tc), F32)] * 2 + [pltpu.VMEM((tm, D_MODEL), BF16)],
        compiler_params=_params("arbitrary", "arbitrary"),
    )(x2, g, w_upT, w_upT, cw, cw, cb, cb)


def _ffn_bwd(dyb, w_down, da, db, pg, pu, cw, name, tm=1024, tc=768):
    T, F = pg.shape
    nt = T // tm
    hb16 = tm // 16
    nc = F // tc
    n = tm + 8

    def body(dy_ref, dyn_ref, wd_ref, a_ref, an_ref, b_ref, bn_ref, g_ref, u_ref, cwg_ref, cwu_ref,
             dg_ref, du_ref, gcwg_ref, gcwu_ref, gcbg_ref, gcbu_ref):
        i = pl.program_id(1)
        last = i == nt - 1

        @pl.when(i == 0)
        def _():
            for ref in (gcwg_ref, gcwu_ref, gcbg_ref, gcbu_ref):
                ref[...] = jnp.zeros_like(ref)

        wd = wd_ref[...]
        dact_next = jnp.where(last, 0.0, _dot(dyn_ref[...], wd, _NT)[:8])
        de = jnp.concatenate([_dot(dy_ref[...], wd, _NT), dact_next], axis=0)
        ext = lambda t, nx: jnp.concatenate([t[...].astype(F32), nx[...].astype(F32)[:8]], axis=0)
        for dcv, x_ref, cw_ref, dx_ref, gcw_ref, gcb_ref in ((de * ext(a_ref, an_ref), g_ref, cwg_ref, dg_ref, gcwg_ref, gcbg_ref),
                                                               (de * ext(b_ref, bn_ref), u_ref, cwu_ref, du_ref, gcwu_ref, gcbu_ref)):
            s1, s2 = pltpu.roll(dcv, n - 1, 0), pltpu.roll(dcv, n - 2, 0)
            dx_ref[...] = (cw_ref[2:3, :] * dcv + cw_ref[1:2, :] * s1 + cw_ref[0:1, :] * s2)[:tm].astype(BF16)
            xv = x_ref[...].astype(F32)
            gcw_ref[2:3, :] += jnp.sum(xv * dcv[:tm], axis=0, keepdims=True)
            gcw_ref[1:2, :] += jnp.sum(xv * s1[:tm], axis=0, keepdims=True)
            gcw_ref[0:1, :] += jnp.sum(xv * s2[:tm], axis=0, keepdims=True)
            gcb_ref[...] += jnp.sum(dcv[:tm], axis=0, keepdims=True)

    tile = pl.BlockSpec((tm, tc), lambda j, i: (i, j))
    nxt = pl.BlockSpec((16, tc), lambda j, i: (jnp.minimum((i + 1) * hb16, nt * hb16 - 1), j))
    cws = lambda off: pl.BlockSpec((8, tc), lambda j, i, off=off: (0, j + off))
    cbs = pl.BlockSpec((1, tc), lambda j, i: (0, j))
    return _call(
        body, name=name, grid=(nc, nt),
        in_specs=[pl.BlockSpec((tm, D_MODEL), lambda j, i: (i, 0)),
                  pl.BlockSpec((16, D_MODEL), lambda j, i: (jnp.minimum((i + 1) * hb16, nt * hb16 - 1), 0)),
                  pl.BlockSpec((tc, D_MODEL), lambda j, i: (j, 0)), tile, nxt, tile, nxt, tile, tile, cws(0), cws(nc)],
        out_specs=[tile, tile, cws(0), cws(0), cbs, cbs],
        out_shape=[jax.ShapeDtypeStruct((T, F), BF16)] * 2 + [jax.ShapeDtypeStruct((8, F), F32)] * 2
        + [jax.ShapeDtypeStruct((1, F), F32)] * 2,
        compiler_params=_params("parallel", "arbitrary"),
    )(dyb, dyb, w_down, da, da, db, db, pg, pu, cw, cw)


def _adam_update(w, g, m, v):
    m2 = ADAM_B1 * m + (1.0 - ADAM_B1) * g
    v2 = ADAM_B2 * v + (1.0 - ADAM_B2) * (g * g)
    m_hat = m2 / (1.0 - ADAM_B1 ** ADAM_STEP)
    v_hat = v2 / (1.0 - ADAM_B2 ** ADAM_STEP)
    delta = -ADAM_LR * (m_hat / (jnp.sqrt(v_hat) + ADAM_EPS) + ADAM_WD * w)
    return delta, m2, v2


def _adam_sharded(p, r2, idx, w, m, v, name, transposed=False):
    r, n = p.shape[1:]
    nrecv = r2.shape[0]
    tr = (256 if r % 256 == 0 else r) if transposed else _row_tile(r)

    def body(c_ref, p_ref, r_ref, w_ref, m_ref, v_ref, g_ref, d_ref, m2_ref, v2_ref):
        g = p_ref[...].astype(F32)
        for k in range(nrecv):
            g = g + r_ref[k].astype(F32)
        if transposed:
            g = g.T
        g_ref[...] = g
        d_ref[...], m2_ref[...], v2_ref[...] = _adam_update(w_ref[...], g, m_ref[...], v_ref[...])

    blk = pl.BlockSpec((n, tr), lambda i, c_ref: (0, i)) if transposed else pl.BlockSpec((tr, n), lambda i, c_ref: (i, 0))
    spec = pltpu.PrefetchScalarGridSpec(
        num_scalar_prefetch=1, grid=(r // tr,),
        in_specs=[pl.BlockSpec((None, tr, n), lambda i, c_ref: (c_ref[0], i, 0)),
                  pl.BlockSpec((nrecv, tr, n), lambda i, c_ref: (0, i, 0)), blk, blk, blk],
        out_specs=[blk] * 4)
    return _call(body, name=name, grid_spec=spec, out_shape=[jax.ShapeDtypeStruct(w.shape, F32)] * 4,
                 compiler_params=_params("parallel"))(idx, p, r2, w, m, v)


def _sum_slabs(p, r2, idx, name):
    _, r, n = p.shape

    def body(c_ref, p_ref, r_ref, o_ref):
        acc = p_ref[...]
        for k in range(N_PEERS):
            acc = acc + r_ref[k]
        o_ref[...] = acc

    spec = pltpu.PrefetchScalarGridSpec(
        num_scalar_prefetch=1, grid=(1,),
        in_specs=[pl.BlockSpec((None, r, n), lambda i, c_ref: (c_ref[0], 0, 0)),
                  pl.BlockSpec((N_PEERS, r, n), lambda i, c_ref: (0, 0, 0))],
        out_specs=pl.BlockSpec((r, n), lambda i, c_ref: (0, 0)))
    return _call(body, name=name, grid_spec=spec, out_shape=jax.ShapeDtypeStruct((r, n), F32))(idx, p, r2)


def _adam_small(ws, gs, ms, vs, name):
    n = len(ws)

    def body(*refs):
        for i in range(n):
            d, m2, v2 = _adam_update(refs[i][...], refs[n + i][...], refs[2 * n + i][...], refs[3 * n + i][...])
            refs[4 * n + i][...] = d
            refs[5 * n + i][...] = m2
            refs[6 * n + i][...] = v2

    outs = _call(body, name=name, out_shape=[jax.ShapeDtypeStruct(w.shape, F32) for w in ws] * 3)(*ws, *gs, *ms, *vs)
    return outs[:n], outs[n:2 * n], outs[2 * n:]


def _pack_small_grads(full, halves, rcw, fcwg, fcwu, wrg, wig, lparts, name):
    nf, nh = len(full), len(halves)

    def body(*refs):
        o = refs[-1]
        o[...] = jnp.zeros_like(o)
        row = 0
        for r in refs[:nf]:
            for j in range(r.shape[1] // 1024):
                o[row:row + 1, :] = r[:, 1024 * j:1024 * (j + 1)]
                row += 1
        for k in range(0, nh, 2):
            o[row:row + 1, 0:512] = refs[nf + k][...]
            o[row:row + 1, 512:1024] = refs[nf + k + 1][...]
            row += 1
        rcw_ref, fg_ref, fu_ref, wrg_ref, wig_ref, l_ref = refs[nf + nh:nf + nh + 6]
        for k in range(2):
            o[row:row + 1, 0:512] = rcw_ref[2 * k:2 * k + 1, :]
            o[row:row + 1, 512:1024] = rcw_ref[2 * k + 1:2 * k + 2, :]
            row += 1
        for f_ref in (fg_ref, fu_ref):
            for k in range(FFN_CONV):
                for j in range(D_FF // 1024):
                    o[row:row + 1, :] = f_ref[k:k + 1, 1024 * j:1024 * (j + 1)]
                    row += 1
        assert row == 32
        for n in range(8):
            o[32:96, 64 * n:64 * n + 64] = wrg_ref[64 * n:64 * n + 64, 64 * n:64 * n + 64]
            o[32:96, 512 + 64 * n:512 + 64 * n + 64] = wig_ref[64 * n:64 * n + 64, 64 * n:64 * n + 64]
        o[96:97, :] = jnp.sum(l_ref[...], axis=0, keepdims=True)

    return _call(body, name=name, out_shape=jax.ShapeDtypeStruct((SMALL_ROWS, 1024), F32))(
        *full, *halves, rcw, fcwg, fcwu, wrg, wig, lparts)


def _block_diag(w):
    eye = jnp.eye(8, dtype=w.dtype)
    return (w[:, :, None, :] * eye[:, None, :, None]).reshape(512, 512)


def kernel(x, positions, g_mix, w_in, q_norm_g, k_norm_g, rec_conv_w, rec_conv_b, w_rg, b_rg, w_ig, b_ig, lru_lambda, g_attn_out, g_rec_out, w_out, g_ffn, w_up, ffn_conv_w, ffn_conv_b, w_down, loss_target, m_g_mix, m_w_in, m_q_norm_g, m_k_norm_g, m_rec_conv_w, m_rec_conv_b, m_w_rg, m_b_rg, m_w_ig, m_b_ig, m_lru_lambda, m_g_attn_out, m_g_rec_out, m_w_out, m_g_ffn, m_w_up, m_ffn_conv_w, m_ffn_conv_b, m_w_down, v_g_mix, v_w_in, v_q_norm_g, v_k_norm_g, v_rec_conv_w, v_rec_conv_b, v_w_rg, v_b_rg, v_w_ig, v_b_ig, v_lru_lambda, v_g_attn_out, v_g_rec_out, v_w_out, v_g_ffn, v_w_up, v_ffn_conv_w, v_ffn_conv_b, v_w_down):
    T = x.shape[1]
    ix, iy, ic = lax.axis_index("x"), lax.axis_index("y"), lax.axis_index("c")
    dev = 4 * ix + 2 * iy + ic
    xs = x.reshape(T, D_MODEL)
    tgt = loss_target.reshape(T, D_MODEL)
    pos = positions.reshape(T, 1)

    shards = {"w_in": (w_in[0], m_w_in[0], v_w_in[0]), "w_out": (w_out[0], m_w_out[0], v_w_out[0]),
              "w_up": (w_up[0], m_w_up[0], v_w_up[0]), "w_down": (w_down[0], m_w_down[0], v_w_down[0])}
    taps = jnp.concatenate([rec_conv_w.reshape(-1), ffn_conv_w.reshape(-1), jnp.zeros((4096 - 2560,), F32)]).reshape(8, 512)
    W_inT, taps_all = _all_gather([w_in[0].T.astype(BF16), taps], "ag_w_in")
    gather_landing = lambda s: _landing((N_DEV * s.shape[0], 1024), BF16, s, dev * s.shape[0])
    late = [w_out[0].astype(BF16), w_up[0].T.astype(BF16)]
    ag_send, ag_recv, late_thru, land_thru, ag_token = _exchange_start(
        late, [gather_landing(s) for s in late], "gather", taps_all, "ag_late_start")
    w_down_b = w_down[0].astype(BF16)
    down_landing = gather_landing(w_down_b)
    taps_all = taps_all.reshape(N_DEV, 4096)
    rcw = taps_all[:, :256].reshape(8, 4, 64).transpose(1, 0, 2).reshape(4, REC_W)
    fcw = taps_all[:, 256:2560].reshape(8, 3, 768).transpose(1, 0, 2).reshape(3, 2 * D_FF)
    rcw8 = jnp.pad(rcw, ((0, 4), (0, 0)))
    fcw8 = jnp.pad(fcw, ((0, 5), (0, 0)))
    fcb = ffn_conv_b.reshape(1, 2 * D_FF)

    half = HEAD_DIM // 2
    inv_freq = ROPE_THETA ** (-jnp.arange(half, dtype=F32) / half)
    invf = jnp.tile(inv_freq, 2 * N_HEADS).reshape(1, ATTN_W)
    bd = jnp.asarray(np.kron(np.eye(2), np.full((HEAD_DIM, HEAD_DIM), 1.0 / HEAD_DIM)), BF16)
    qg = jnp.tile(q_norm_g.reshape(HEAD_DIM), N_HEADS).reshape(1, ATTN_W)
    kg = jnp.tile(k_norm_g.reshape(HEAD_DIM), N_HEADS).reshape(1, ATTN_W)
    wrg_bd = _block_diag(w_rg[0]).astype(BF16)
    wig_bd = _block_diag(w_ig[0]).astype(BF16)
    brg, big = b_rg.reshape(1, REC_W), b_ig.reshape(1, REC_W)

    proj, h1 = _norm_proj(xs, g_mix + ag_token[0, 0], W_inT, "in_proj", tn=IN_W)
    qf, kf = _qk_prep(proj, pos, invf, qg, kg, bd, "qk_prep")
    attn, lse = _attn_fwd(qf, kf, proj, "attn_fwd")
    dn_send, dn_recv, dn_thru, dn_land, dn_token = _exchange_start(
        [w_down_b], [down_landing], "gather", attn, "ag_down_start")
    mix = _attn_norm(attn, g_attn_out + dn_token[0, 0], "attn_norm")
    xc, hstate, mix = _rec_fwd(proj, mix, rcw8, rec_conv_b, wrg_bd, wig_bd, brg, big, lru_lambda, g_rec_out, "rec_fwd")
    _, (W_out, W_upT) = _exchange_wait(ag_send, ag_recv, late_thru, land_thru, "gather", hstate, "ag_late_wait")
    x2 = _mm(mix, W_out, "nn", F32, "out_proj", add=xs)

    act, da, db, pg, pu, h2 = _up_proj_act(x2, g_ffn, W_upT, fcw8, fcb, "up_proj_act")
    _, (W_down,) = _exchange_wait(dn_send, dn_recv, dn_thru, dn_land, "gather", h2, "ag_down_wait")
    dy, dyb, lparts = _mm(act, W_down, "nn", F32, "down_proj_loss", add=x2, loss_target=tgt, tm=512, tk=D_FF)

    g_down = _mm(act, dyb, "tn", BF16, "g_w_down", tk=4096)
    dpg, dpu, g_fcwg, g_fcwu, g_fcbg, g_fcbu = _ffn_bwd(dyb, W_down, da, db, pg, pu, fcw8, "ffn_bwd")
    g_upT = _mm(dpg, h2, "tn", BF16, "g_w_up_gate", tk=4096, o_rows=2 * D_FF)
    g_upT = _mm(dpu, h2, "tn", BF16, "g_w_up_up", tk=4096, into=g_upT, o_moff=D_FF // 1024)
    ffn_g = [g_upT.reshape(N_DEV, 2 * D_FF // N_DEV, 1024), g_down.reshape(N_DEV, D_FF // N_DEV, 1024)]
    rs_send, rs_recv, ffn_g, ffn_land, rs_token = _exchange_start(
        ffn_g, [_landing((N_PEERS,) + g.shape[1:], BF16) for g in ffn_g], "scatter", dpu, "rs_ffn_start")
    dx2, dx2b, g_gffn = _mm_norm_bwd([dpg, dpu], W_upT, x2, dy, g_ffn + rs_token[0, 0], "d_h2_norm_bwd", tm=1024, tk=1536)

    dmix = _mm(dx2b, W_out, "nt", F32, "d_mix")
    g_out = _mm(mix, dx2b, "tn", BF16, "g_w_out", tk=4096).reshape(N_DEV, D_MODEL // N_DEV, 1024)
    out_send, out_recv, (g_out,), out_land, out_token = _exchange_start(
        [g_out], [_landing((N_PEERS,) + g_out.shape[1:], BF16)], "scatter", dmix, "rs_out_start")
    do, delta, g_gattn = _attn_norm_bwd(dmix, attn, g_attn_out + out_token[0, 0], bd, "attn_norm_bwd")
    dqh, dkh, dv = _attn_bwd(qf, kf, proj, do, lse, delta, "attn_bwd")
    dqkv, g_qg, g_kg = _qk_prep_bwd(proj, dqh, dkh, dv, pos, invf, qg, kg, bd, "qk_prep_bwd")
    (drec, g_rcw, g_rcb, g_wrg, g_wig, g_brg, g_big, g_lam, g_grec) = _rec_bwd(
        dmix, proj, xc, hstate, rcw8, rec_conv_b, wrg_bd, wig_bd, brg, big, lru_lambda, g_rec_out, "rec_bwd")
    g_inT = _mm(dqkv, h1, "tn", BF16, "g_w_in_qkv", tm=512, tk=4096, o_rows=IN_W)
    g_inT = _mm(drec, h1, "tn", BF16, "g_w_in_rec", tm=512, tk=4096, into=g_inT, o_moff=3 * ATTN_W // 512)
    g_inT = g_inT.reshape(N_DEV, IN_W // N_DEV, 1024)
    in_send, in_recv, (g_inT,), in_land, in_token = _exchange_start(
        [g_inT], [_landing((N_PEERS,) + g_inT.shape[1:], BF16)], "scatter", drec, "rs_in_start")
    grad_x, _, g_gmix = _mm_norm_bwd([dqkv, drec], W_inT, xs, dx2, g_mix + in_token[0, 0], "d_h1_norm_bwd", tm=1024, tk=512)

    flat = _pack_small_grads([g_gmix, g_gffn, g_fcbg, g_fcbu], [g_rcb, g_brg, g_big, g_lam, g_gattn, g_grec, g_qg, g_kg],
                             g_rcw, g_fcwg, g_fcwu, g_wrg, g_wig, lparts.reshape(-1, D_MODEL), "pack_small_grads")
    srows = SMALL_ROWS // N_DEV
    flat = flat.reshape(N_DEV, srows, 1024)
    sm_send, sm_recv, (flat,), sm_land, sm_token = _exchange_start(
        [flat], [_landing((N_PEERS, srows, 1024), F32)], "scatter", grad_x, "ar_small_rs_start")

    devi = jnp.reshape(dev, (1,)).astype(jnp.int32)
    ffn_g, ffn_land = _exchange_wait(rs_send, rs_recv, ffn_g, ffn_land, "scatter", sm_token, "rs_ffn_wait")
    (g_out,), out_land = _exchange_wait(out_send, out_recv, [g_out], out_land, "scatter", sm_token, "rs_out_wait")
    big_out = {"grad": {}, "delta": {}, "new_m": {}, "new_v": {}}

    def adam_big(nm, p, r):
        w_, m_, v_ = shards[nm]
        res = _adam_sharded(p, r, devi, w_, m_, v_, "adam_" + nm, transposed=nm in ("w_in", "w_up"))
        for kind, a in zip(("grad", "delta", "new_m", "new_v"), res):
            big_out[kind][nm] = a[None]
        return res[0]

    last = adam_big("w_up", ffn_g[0], ffn_land[0])
    (flat,), sm_land = _exchange_wait(sm_send, sm_recv, [flat], sm_land, "scatter", last, "ar_small_rs_wait")
    mine = _sum_slabs(flat, sm_land[0], devi, "sum_small_grads")
    sm_send, sm_recv, (mine,), sm_land, sm_token = _exchange_start(
        [mine], [_landing((SMALL_ROWS, 1024), F32, mine, dev * srows)], "gather", last, "ar_small_ag_start")
    adam_big("w_down", ffn_g[1], ffn_land[1])
    last = adam_big("w_out", g_out, out_land[0])
    _, (tot,) = _exchange_wait(sm_send, sm_recv, [mine], sm_land, "gather", last, "ar_small_ag_wait")
    (g_inT,), in_land = _exchange_wait(in_send, in_recv, [g_inT], in_land, "scatter", tot, "rs_in_wait")
    adam_big("w_in", g_inT, in_land[0])

    half = lambda r, h, shape: tot[r, 512 * h:512 * h + 512].reshape(shape)
    blocks = lambda h: tot[32:96, 512 * h:512 * h + 512].reshape(64, 8, 64).transpose(1, 0, 2)[None]
    fcw_full = jnp.concatenate([tot[14:23].reshape(1, 3, D_FF), tot[23:32].reshape(1, 3, D_FF)], axis=2)
    g_small = {
        "g_mix": tot[0:1], "g_ffn": tot[1:2], "ffn_conv_b": tot[2:8].reshape(1, 2 * D_FF),
        "rec_conv_b": half(8, 0, (1, 512)), "b_rg": half(8, 1, (1, 8, 64)), "b_ig": half(9, 0, (1, 8, 64)),
        "lru_lambda": half(9, 1, (1, 512)), "g_attn_out": half(10, 0, (1, 512)), "g_rec_out": half(10, 1, (1, 512)),
        "q_norm_g": half(11, 0, (N_HEADS, HEAD_DIM)).sum(0)[None], "k_norm_g": half(11, 1, (N_HEADS, HEAD_DIM)).sum(0)[None],
        "w_rg": blocks(0), "w_ig": blocks(1),
        "rec_conv_w": lax.dynamic_slice(tot[12:14].reshape(1, 4, REC_W), (0, 0, 64 * dev), (1, 4, 64)),
        "ffn_conv_w": lax.dynamic_slice(fcw_full, (0, 0, 768 * dev), (1, 3, 768))}
    loss = 0.5 / D_MODEL * jnp.sum(tot[96])
    given = dict(rec_conv_w=rec_conv_w, ffn_conv_w=ffn_conv_w,g_mix=g_mix, q_norm_g=q_norm_g, k_norm_g=k_norm_g, rec_conv_b=rec_conv_b, w_rg=w_rg, b_rg=b_rg, w_ig=w_ig,
                 b_ig=b_ig, lru_lambda=lru_lambda, g_attn_out=g_attn_out, g_rec_out=g_rec_out, g_ffn=g_ffn, ffn_conv_b=ffn_conv_b)
    given_m = dict(rec_conv_w=m_rec_conv_w, ffn_conv_w=m_ffn_conv_w, g_mix=m_g_mix, q_norm_g=m_q_norm_g, k_norm_g=m_k_norm_g, rec_conv_b=m_rec_conv_b, w_rg=m_w_rg, b_rg=m_b_rg,
                   w_ig=m_w_ig, b_ig=m_b_ig, lru_lambda=m_lru_lambda, g_attn_out=m_g_attn_out, g_rec_out=m_g_rec_out,
                   g_ffn=m_g_ffn, ffn_conv_b=m_ffn_conv_b)
    given_v = dict(rec_conv_w=v_rec_conv_w, ffn_conv_w=v_ffn_conv_w, g_mix=v_g_mix, q_norm_g=v_q_norm_g, k_norm_g=v_k_norm_g, rec_conv_b=v_rec_conv_b, w_rg=v_w_rg, b_rg=v_b_rg,
                   w_ig=v_w_ig, b_ig=v_b_ig, lru_lambda=v_lru_lambda, g_attn_out=v_g_attn_out, g_rec_out=v_g_rec_out,
                   g_ffn=v_g_ffn, ffn_conv_b=v_ffn_conv_b)
    small = sorted(given)
    ds, m2s, v2s = _adam_small([given[k] for k in small], [g_small[k] for k in small], [given_m[k] for k in small],
                               [given_v[k] for k in small], "adam_small")
    small_out = {"grad": g_small, "delta": dict(zip(small, ds)), "new_m": dict(zip(small, m2s)), "new_v": dict(zip(small, v2s))}

    order = ("g_mix", "w_in", "q_norm_g", "k_norm_g", "rec_conv_w", "rec_conv_b", "w_rg", "b_rg", "w_ig", "b_ig",
             "lru_lambda", "g_attn_out", "g_rec_out", "w_out", "g_ffn", "w_up", "ffn_conv_w", "ffn_conv_b", "w_down")
    outs = [loss, grad_x.reshape(1, T, D_MODEL)]
    for kind in ("grad", "delta", "new_m", "new_v"):
        for name in order:
            outs.append(big_out[kind][name] if name in big_out[kind] else small_out[kind][name])
    return tuple(outs)
```

```python
import math

import numpy as np
import jax
import jax.numpy as jnp
from jax import lax
from jax.experimental import pallas as pl
from jax.experimental.pallas import tpu as pltpu

F32 = jnp.float32
BF16 = jnp.bfloat16

D_MODEL = 1024
HEAD_DIM = 64
ATTN_W = 512
REC_W = 512
N_HEADS = 8
D_FF = 3072
IN_W = 2560
REC_CONV = 4
FFN_CONV = 3
LRU_C = 8.0
ROPE_THETA = 10000.0
EPS = 1e-6
NEG_INF = -1e30
QBLK = 128
DILATIONS = (1, 4, 16)
N_DEV = 8
SMALL_ROWS = 128
ADAM_LR, ADAM_B1, ADAM_B2, ADAM_EPS, ADAM_WD, ADAM_STEP = 0.001, 0.9, 0.999, 1e-08, 0.01, 10
MESH = pl.DeviceIdType.MESH
ANY = pl.BlockSpec(memory_space=pl.ANY)


def _call(body, *, name, **kw):
    return pl.pallas_call(body, name=name, **kw)


def _params(*sem):
    return pltpu.CompilerParams(dimension_semantics=sem, vmem_limit_bytes=56 * 1024 * 1024)


_GELU_C = math.sqrt(2.0 / math.pi)
_GELU_A = 0.044715


def _gelu(x):
    return (0.5 * x) * (1.0 + jnp.tanh(x * (_GELU_C + (_GELU_C * _GELU_A) * (x * x))))


def _gelu_and_grad(x):
    x2 = x * x
    u = 1.0 + jnp.tanh(x * (_GELU_C + (_GELU_C * _GELU_A) * x2))
    hx = 0.5 * x
    return hx * u, 0.5 * u + (hx * ((2.0 - u) * u)) * (_GELU_C + (3.0 * _GELU_C * _GELU_A) * x2)


def _sigmoid(x):
    return 1.0 / (1.0 + jnp.exp(-x))


def _softplus_neg(lam):
    y = jnp.exp(-jnp.abs(lam))
    u = 1.0 + y
    log1p = jnp.where(u == 1.0, y, jnp.log(u) * y / jnp.where(u == 1.0, 1.0, u - 1.0))
    return jnp.maximum(-lam, 0.0) + log1p


_NN = (((1,), (0,)), ((), ()))
_NT = (((1,), (1,)), ((), ()))
_TN = (((0,), (0,)), ((), ()))


def _dot(a, b, dims=_NN):
    return lax.dot_general(a, b, dims, preferred_element_type=F32)


def _group_mean(v, bd):
    hi = v.astype(BF16)
    lo = (v - hi.astype(F32)).astype(BF16)
    w = bd.shape[0]
    return jnp.concatenate([_dot(hi[:, c:c + w], bd) + _dot(lo[:, c:c + w], bd) for c in range(0, v.shape[1], w)], axis=1)


def _rope_tables(pos_ref, invf_ref):
    ang = pos_ref[...].astype(F32) * invf_ref[:, :2 * HEAD_DIM]
    reps = invf_ref.shape[1] // (2 * HEAD_DIM)
    return jnp.tile(jnp.cos(ang), (1, reps)), jnp.tile(jnp.sin(ang), (1, reps))


def _shift_down(x, halo, s):
    rolled = pltpu.roll(x, s, 0)
    hr = pltpu.roll(halo, s, 0)
    row = lax.broadcasted_iota(jnp.int32, hr.shape, 0)
    first = jnp.where(row < s, hr, rolled[:8])
    return jnp.concatenate([first, rolled[8:]], axis=0)


def _shift_up(x, halo, s):
    n = x.shape[0]
    rolled = pltpu.roll(x, n - s, 0)
    hr = pltpu.roll(halo, 8 - s, 0)
    row = lax.broadcasted_iota(jnp.int32, hr.shape, 0)
    last = jnp.where(row >= 8 - s, hr, rolled[n - 8:])
    return jnp.concatenate([rolled[:n - 8], last], axis=0)


def _scan_fwd(a, u):
    n, w = a.shape
    a3, u3 = a.reshape(n // 8, 8, w), u.reshape(n // 8, 8, w)
    row = lax.broadcasted_iota(jnp.int32, a3.shape, 1)
    for s in (1, 2, 4):
        a_s = jnp.where(row < s, 1.0, pltpu.roll(a3, s, 1))
        u_s = jnp.where(row < s, 0.0, pltpu.roll(u3, s, 1))
        u3 = u3 + a3 * u_s
        a3 = a3 * a_s
    ps, hs = [a3[0]], [u3[0]]
    for k in range(1, n // 8):
        ps.append(a3[k] * ps[-1][7:8, :])
        hs.append(u3[k] + a3[k] * hs[-1][7:8, :])
    return jnp.concatenate(ps, axis=0), jnp.concatenate(hs, axis=0)


def _scan_bwd(b, v):
    n, w = b.shape
    b3, v3 = b.reshape(n // 8, 8, w), v.reshape(n // 8, 8, w)
    row = lax.broadcasted_iota(jnp.int32, b3.shape, 1)
    for s in (1, 2, 4):
        b_s = jnp.where(row >= 8 - s, 1.0, pltpu.roll(b3, 8 - s, 1))
        v_s = jnp.where(row >= 8 - s, 0.0, pltpu.roll(v3, 8 - s, 1))
        v3 = v3 + b3 * v_s
        b3 = b3 * b_s
    last = n // 8 - 1
    ps, gs = [b3[last]], [v3[last]]
    for k in range(last - 1, -1, -1):
        ps.append(b3[k] * ps[-1][0:1, :])
        gs.append(v3[k] + b3[k] * gs[-1][0:1, :])
    return jnp.concatenate(ps[::-1], axis=0), jnp.concatenate(gs[::-1], axis=0)


def _rot_half(y):
    n = y.shape[1]
    lane = lax.broadcasted_iota(jnp.int32, y.shape, 1) & (HEAD_DIM - 1)
    return jnp.where(lane < HEAD_DIM // 2, -pltpu.roll(y, n - HEAD_DIM // 2, 1), pltpu.roll(y, HEAD_DIM // 2, 1))


def _row_tile(r, cap=256):
    return max(t for t in range(16, cap + 1, 16) if r % t == 0)


def _all_gather(shards, name):
    na = len(shards)
    ms = [s.shape[0] for s in shards]

    def body(*refs):
        x_refs, out_refs = refs[:na], refs[na:2 * na]
        send_sems, recv_sems, local_sems = refs[2 * na:]
        x, y, c = lax.axis_index("x"), lax.axis_index("y"), lax.axis_index("c")
        me, sibling = (x, y, c), (x, y, 1 - c)
        chips = [(1 - x, y), (x, 1 - y), (1 - x, 1 - y)]

        def rows(a, px, py, pc):
            return out_refs[a].at[pl.ds(pl.multiple_of((4 * px + 2 * py + pc) * ms[a], 8), ms[a]), :]

        def copy(a, k, block, to, src=None):
            return pltpu.make_async_remote_copy(
                src_ref=rows(a, *block) if src is None else src, dst_ref=rows(a, *block),
                send_sem=send_sems.at[7 * a + k], recv_sem=recv_sems.at[7 * a + k], device_id=to, device_id_type=MESH)

        mine = [pltpu.make_async_copy(x_refs[a], rows(a, *me), local_sems.at[a]) for a in range(na)]
        first = []
        for a in range(na):
            mine[a].start()
            first.append(copy(a, 0, me, sibling, src=x_refs[a]))
            first += [copy(a, 1 + j, me, (*chip, c), src=x_refs[a]) for j, chip in enumerate(chips)]
        for cp in first:
            cp.start()
        passed = []
        for a in range(na):
            for j, chip in enumerate(chips):
                copy(a, 1 + j, (*chip, c), me).wait_recv()
                fw = copy(a, 4 + j, (*chip, c), sibling)
                fw.start()
                passed.append(fw)
        for a in range(na):
            copy(a, 0, sibling, me).wait_recv()
            for j, chip in enumerate(chips):
                copy(a, 4 + j, (*chip, 1 - c), me).wait_recv()
        for cp in first + passed:
            cp.wait_send()
        for cp in mine:
            cp.wait()

    return _call(
        body, name=name, out_shape=[jax.ShapeDtypeStruct((N_DEV * s.shape[0], s.shape[1]), s.dtype) for s in shards],
        in_specs=[ANY] * na, out_specs=[ANY] * na,
        scratch_shapes=[pltpu.SemaphoreType.DMA((7 * na,)), pltpu.SemaphoreType.DMA((7 * na,)),
                        pltpu.SemaphoreType.DMA((na,))],
    )(*shards)


HBM = pl.BlockSpec(memory_space=pltpu.HBM)
SEM = pl.BlockSpec(memory_space=pltpu.SEMAPHORE)
EFFECT = pltpu.SideEffectType.DATAFLOW_SIDE_EFFECTING
N_PEERS = N_DEV - 1


def _peer(k):
    x, y, c = lax.axis_index("x"), lax.axis_index("y"), lax.axis_index("c")
    b = k + 1
    flip = lambda v, bit: 1 - v if bit else v
    return flip(x, b & 4), flip(y, b & 2), flip(c, b & 1)


def _in_hbm(a):
    return pltpu.with_memory_space_constraint(a, pltpu.HBM)


def _split_copy_descr(na, kind, src_refs, land_refs, send_sems, recv_sems):
    x, y, c = lax.axis_index("x"), lax.axis_index("y"), lax.axis_index("c")
    me = 4 * x + 2 * y + c
    copies = []
    for a in range(na):
        for k in range(N_PEERS):
            px, py, pc = _peer(k)
            if kind == "gather":
                m = src_refs[a].shape[0]
                src, dst = src_refs[a], land_refs[a].at[pl.ds(pl.multiple_of(me * m, 8), m), :]
            else:
                src, dst = src_refs[a].at[4 * px + 2 * py + pc], land_refs[a].at[k]
            copies.append(pltpu.make_async_remote_copy(
                src_ref=src, dst_ref=dst, send_sem=send_sems.at[N_PEERS * a + k], recv_sem=recv_sems.at[N_PEERS * a + k],
                device_id=(px, py, pc), device_id_type=MESH))
    return copies


def _landing(shape, dtype, own=None, at=None):
    buf = lax.empty(shape, dtype)
    return buf if own is None else lax.dynamic_update_slice(buf, own, (at, 0))


def _exchange_start(srcs, lands, kind, after, name):
    na = len(srcs)
    land_shapes = [l.shape for l in lands]

    def body(*refs):
        src_refs, land_refs = refs[:na], refs[na:2 * na]
        send_sems, recv_sems = refs[2 * na + 1], refs[2 * na + 2]
        token = refs[-1]
        for cp in _split_copy_descr(na, kind, src_refs, land_refs, send_sems, recv_sems):
            cp.start()
        token[...] = jnp.zeros_like(token)

    lands = [_in_hbm(l) for l in lands]
    sem = pltpu.SemaphoreType.DMA((N_PEERS * na,))
    outs = _call(
        body, name=name,
        out_shape=[sem, sem] + [pltpu.HBM(s.shape, s.dtype) for s in srcs] + [pltpu.HBM(s, srcs[0].dtype) for s in land_shapes]
        + [jax.ShapeDtypeStruct((8, 128), F32)],
        in_specs=[HBM] * (2 * na) + [ANY], out_specs=[SEM, SEM] + [HBM] * (2 * na) + [pl.BlockSpec(memory_space=pltpu.VMEM)],
        input_output_aliases={i: 2 + i for i in range(2 * na)},
        compiler_params=pltpu.CompilerParams(has_side_effects=EFFECT),
    )(*[_in_hbm(s) for s in srcs], *lands, after)
    return outs[0], outs[1], outs[2:2 + na], outs[2 + na:2 + 2 * na], outs[-1]


def _exchange_wait(send_sems, recv_sems, srcs, lands, kind, after, name):
    na = len(srcs)

    def body(*refs):
        src_refs, land_refs = refs[:na], refs[na:2 * na]
        s_sems, r_sems = refs[2 * na], refs[2 * na + 1]
        for cp in _split_copy_descr(na, kind, src_refs, land_refs, s_sems, r_sems):
            cp.wait_send()
            cp.wait_recv()

    outs = _call(
        body, name=name, out_shape=[pltpu.HBM(s.shape, s.dtype) for s in srcs] + [pltpu.HBM(l.shape, l.dtype) for l in lands],
        in_specs=[HBM] * (2 * na) + [SEM, SEM, ANY], out_specs=[HBM] * (2 * na),
        input_output_aliases={i: i for i in range(2 * na)},
        compiler_params=pltpu.CompilerParams(has_side_effects=EFFECT),
    )(*srcs, *lands, send_sems, recv_sems, after)
    return outs[:na], outs[na:]


def _mm(a, b, mode, out_dtype, name, add=None, tm=1024, tn=1024, tk=1024, into=None, o_rows=None, o_moff=0,
        loss_target=None):
    if mode == "tn":
        K, M = a.shape
    else:
        M, K = a.shape
    N = b.shape[0] if mode == "nt" else b.shape[1]
    tm, tn, tk = min(tm, M), min(tn, N), min(tk, K)
    assert M % tm == 0 and N % tn == 0 and K % tk == 0, (name, M, N, K)
    nk = K // tk
    if mode == "nn":
        a_spec = pl.BlockSpec((tm, tk), lambda i, j, kk: (i, kk))
        b_spec, dims = pl.BlockSpec((tk, tn), lambda i, j, kk: (kk, j)), _NN
    elif mode == "nt":
        a_spec = pl.BlockSpec((tm, tk), lambda i, j, kk: (i, kk))
        b_spec, dims = pl.BlockSpec((tn, tk), lambda i, j, kk: (j, kk)), _NT
    else:
        a_spec = pl.BlockSpec((tk, tm), lambda i, j, kk: (kk, i))
        b_spec, dims = pl.BlockSpec((tk, tn), lambda i, j, kk: (kk, j)), _TN
    o_spec = pl.BlockSpec((tm, tn), lambda i, j, kk: (i + o_moff, j))
    has_add, has_into, has_loss = add is not None, into is not None, loss_target is not None
    assert not has_loss or (has_add and tn == N and not has_into)
    n_in = 2 + has_add + has_loss + has_into

    def body(*refs):
        a_ref, b_ref = refs[0], refs[1]
        add_ref = refs[2] if has_add else None
        outs = refs[n_in:]

        def finish(r):
            if has_add:
                r = r + add_ref[...]
            if has_loss:
                e = r - refs[3][...]
                dy = e * (1.0 / N)
                outs[0][...] = dy
                outs[1][...] = dy.astype(BF16)
                outs[2][...] = jnp.sum(e * e, axis=0, keepdims=True)[None]
            else:
                outs[0][...] = r.astype(out_dtype)

        if nk == 1:
            finish(_dot(a_ref[...], b_ref[...], dims))
        else:
            acc = refs[-1]
            kk = pl.program_id(2)

            @pl.when(kk == 0)
            def _():
                acc[...] = _dot(a_ref[...], b_ref[...], dims)

            @pl.when((kk > 0) & (kk < nk - 1))
            def _():
                acc[...] += _dot(a_ref[...], b_ref[...], dims)

            @pl.when(kk == nk - 1)
            def _():
                finish(acc[...] + _dot(a_ref[...], b_ref[...], dims))

    tile = pl.BlockSpec((tm, tn), lambda i, j, kk: (i, j))
    ins = [a, b] + ([add] if has_add else []) + ([loss_target] if has_loss else []) + ([into] if has_into else [])
    specs = [a_spec, b_spec] + [tile] * (has_add + has_loss) + ([ANY] if has_into else [])
    rows = into.shape[0] if has_into else (o_rows if o_rows is not None else M)
    if has_loss:
        out_specs = [tile, tile, pl.BlockSpec((1, 1, N), lambda i, j, kk: (i, 0, 0))]
        out_shape = [jax.ShapeDtypeStruct((M, N), F32), jax.ShapeDtypeStruct((M, N), BF16), jax.ShapeDtypeStruct((M // tm, 1, N), F32)]
    else:
        out_specs, out_shape = o_spec, jax.ShapeDtypeStruct((rows, N), out_dtype)
    return _call(
        body, name=name, grid=(M // tm, N // tn, nk), in_specs=specs, out_specs=out_specs, out_shape=out_shape,
        scratch_shapes=[pltpu.VMEM((tm, tn), F32)] if nk > 1 else [],
        input_output_aliases={len(ins) - 1: 0} if has_into else {},
        compiler_params=_params("parallel", "parallel", "arbitrary"),
    )(*ins)


def _mm_norm_bwd(parts, b, x, resid, g, name, tm=512, tk=512):
    T, N = x.shape
    counts = [p.shape[1] // tk for p in parts]
    starts = [sum(counts[:i]) for i in range(len(parts))]
    nsteps = sum(counts)
    assert all(p.shape[1] % tk == 0 for p in parts) and b.shape == (nsteps * tk, N)
    npart = len(parts)

    def body(*refs):
        a_refs, b_ref, x_ref, res_ref, g_ref = refs[:npart], refs[npart], refs[npart + 1], refs[npart + 2], refs[npart + 3]
        dx_ref, dxb_ref, dg_ref, acc = refs[npart + 4:]
        i, s = pl.program_id(0), pl.program_id(1)

        @pl.when((i == 0) & (s == 0))
        def _():
            dg_ref[...] = jnp.zeros_like(dg_ref)

        for p in range(npart):
            @pl.when((s >= starts[p]) & (s < starts[p] + counts[p]))
            def _(p=p):
                d = _dot(a_refs[p][...], b_ref[...])

                @pl.when(s == 0)
                def _():
                    acc[...] = d

                @pl.when(s > 0)
                def _():
                    acc[...] += d

        @pl.when(s == nsteps - 1)
        def _():
            xv, dhv = x_ref[...], acc[...]
            r = lax.rsqrt(jnp.mean(xv * xv, axis=-1, keepdims=True) + EPS)
            gd = dhv * g_ref[...]
            m = jnp.mean(gd * xv, axis=-1, keepdims=True)
            dx = res_ref[...] + r * gd - xv * (r * r * r) * m
            dx_ref[...] = dx
            dxb_ref[...] = dx.astype(BF16)
            dg_ref[...] += jnp.sum(dhv * xv * r, axis=0, keepdims=True)

    a_specs = [pl.BlockSpec((tm, tk), lambda i, s, st=st, c=c: (i, jnp.clip(s - st, 0, c - 1))) for st, c in zip(starts, counts)]
    row = pl.BlockSpec((tm, N), lambda i, s: (i, 0))
    vec = pl.BlockSpec((1, N), lambda i, s: (0, 0))
    return _call(
        body, name=name, grid=(T // tm, nsteps),
        in_specs=a_specs + [pl.BlockSpec((tk, N), lambda i, s: (s, 0)), row, row, vec], out_specs=[row, row, vec],
        out_shape=[jax.ShapeDtypeStruct((T, N), F32), jax.ShapeDtypeStruct((T, N), BF16), jax.ShapeDtypeStruct((1, N), F32)],
        scratch_shapes=[pltpu.VMEM((tm, N), F32)], compiler_params=_params("arbitrary", "arbitrary"),
    )(*parts, b, x, resid, g)


def _norm_proj(x, g, wT, name, tm=1024, tn=1280):
    T, K = x.shape
    N = wT.shape[0]

    def body(x_ref, g_ref, w_ref, o_ref, h_ref):
        xv = x_ref[...]
        r = lax.rsqrt(jnp.mean(xv * xv, axis=-1, keepdims=True) + EPS)
        hv = (xv * r * g_ref[...]).astype(BF16)

        @pl.when(pl.program_id(1) == 0)
        def _():
            h_ref[...] = hv

        o_ref[...] = _dot(hv, w_ref[...], _NT)

    return _call(
        body, name=name, grid=(T // tm, N // tn),
        in_specs=[pl.BlockSpec((tm, K), lambda i, j: (i, 0)), pl.BlockSpec((1, K), lambda i, j: (0, 0)),
                  pl.BlockSpec((tn, K), lambda i, j: (j, 0))],
        out_specs=[pl.BlockSpec((tm, tn), lambda i, j: (i, j)), pl.BlockSpec((tm, K), lambda i, j: (i, 0))],
        out_shape=[jax.ShapeDtypeStruct((T, N), F32), jax.ShapeDtypeStruct((T, K), BF16)],
        compiler_params=_params("parallel", "arbitrary"),
    )(x, g, wT)


def _qk_prep(proj, pos, invf, qg, kg, bd, name, tm=1024):
    T = proj.shape[0]

    def body(q_ref, k_ref, pos_ref, invf_ref, qg_ref, kg_ref, bd_ref, qo_ref, ko_ref):
        cos, sin = _rope_tables(pos_ref, invf_ref)

        def prep(xv, gv, scale):
            r = lax.rsqrt(_group_mean(xv * xv, bd_ref[...]) + EPS)
            yv = xv * r * gv
            return ((yv * cos + _rot_half(yv) * sin) * scale).astype(BF16).astype(F32)

        qo_ref[...] = prep(q_ref[...], qg_ref[...], HEAD_DIM ** -0.5)
        ko_ref[...] = prep(k_ref[...], kg_ref[...], 1.0)

    col = lambda j: pl.BlockSpec((tm, ATTN_W), lambda i, j=j: (i, j))
    vec = pl.BlockSpec((1, ATTN_W), lambda i: (0, 0))
    out = pl.BlockSpec((tm, ATTN_W), lambda i: (i, 0))
    return _call(
        body, name=name, grid=(T // tm,),
        in_specs=[col(0), col(1), pl.BlockSpec((tm, 1), lambda i: (i, 0)), vec, vec, vec,
                  pl.BlockSpec((2 * HEAD_DIM, 2 * HEAD_DIM), lambda i: (0, 0))],
        out_specs=[out, out], out_shape=[jax.ShapeDtypeStruct((T, ATTN_W), F32)] * 2,
        compiler_params=_params("parallel"),
    )(proj, proj, pos, invf, qg, kg, bd)


def _qk_prep_bwd(proj, dqh, dkh, dv, pos, invf, qg, kg, bd, name, tm=512):
    T = proj.shape[0]

    def body(q_ref, k_ref, dq_ref, dk_ref, dv_ref, pos_ref, invf_ref, qg_ref, kg_ref, bd_ref, o_ref, gq_ref, gk_ref):
        @pl.when(pl.program_id(0) == 0)
        def _():
            gq_ref[...] = jnp.zeros_like(gq_ref)
            gk_ref[...] = jnp.zeros_like(gk_ref)

        cos, sin = _rope_tables(pos_ref, invf_ref)

        def back(xv, gv, dz, scale):
            dz = dz * scale
            dy = dz * cos - _rot_half(dz * sin)
            r = lax.rsqrt(_group_mean(xv * xv, bd_ref[...]) + EPS)
            gd = dy * gv
            m = _group_mean(gd * xv, bd_ref[...])
            dx = r * gd - xv * (r * r * r) * m
            return dx, jnp.sum(dy * xv * r, axis=0, keepdims=True)

        dxq, gs = back(q_ref[...], qg_ref[...], dq_ref[...], HEAD_DIM ** -0.5)
        gq_ref[...] += gs
        dxk, gs = back(k_ref[...], kg_ref[...], dk_ref[...], 1.0)
        gk_ref[...] += gs
        o_ref[...] = jnp.concatenate([dxq.astype(BF16), dxk.astype(BF16), dv_ref[...].astype(BF16)], axis=1)

    col = lambda j: pl.BlockSpec((tm, ATTN_W), lambda i, j=j: (i, j))
    row = pl.BlockSpec((tm, ATTN_W), lambda i: (i, 0))
    vec = pl.BlockSpec((1, ATTN_W), lambda i: (0, 0))
    return _call(
        body, name=name, grid=(T // tm,),
        in_specs=[col(0), col(1), row, row, row, pl.BlockSpec((tm, 1), lambda i: (i, 0)), vec, vec, vec,
                  pl.BlockSpec((2 * HEAD_DIM, 2 * HEAD_DIM), lambda i: (0, 0))],
        out_specs=[pl.BlockSpec((tm, 3 * ATTN_W), lambda i: (i, 0)), vec, vec],
        out_shape=[jax.ShapeDtypeStruct((T, 3 * ATTN_W), BF16)] + [jax.ShapeDtypeStruct((1, ATTN_W), F32)] * 2,
        compiler_params=_params("arbitrary"),
    )(proj, proj, dqh, dkh, dv, pos, invf, qg, kg, bd)


def _ld(ref, start, size, dil):
    return ref[pl.ds(start, size), :] if dil == 1 else ref[pl.ds(start, size, stride=dil), :]


def _st(ref, start, size, dil, val):
    if dil == 1:
        ref[pl.ds(start, size), :] = val
    else:
        ref[pl.ds(start, size, stride=dil), :] = val


def _attn_geometry(T, dil):
    nb = T // dil // QBLK
    if nb == 2:
        return 1, 2 * QBLK, 2 * QBLK
    return nb, QBLK, (2 * QBLK if nb >= 2 else QBLK)


ATTN_UNROLL = 4


def _attn_unit(j, u, dil, nit):
    return ATTN_UNROLL * j + u if dil >= ATTN_UNROLL else j + u * (nit // ATTN_UNROLL)


def _attn_block(it, dil, qb, kw):
    c, n = it & (dil - 1), lax.shift_right_logical(it, dil.bit_length() - 1)
    sq = n * (qb * dil) + c
    sk = jnp.maximum(n - (kw // qb - 1), 0) * (qb * dil) + c
    qi = lax.broadcasted_iota(jnp.int32, (2 * qb, kw), 0) & (qb - 1)
    kj = lax.broadcasted_iota(jnp.int32, (2 * qb, kw), 1)
    rel = jnp.where(n > 0, kw - qb, 0) + qi - kj
    return sq, sk, (rel >= 0) & (rel <= QBLK)


def _stack_heads(xv, head0):
    z = jnp.zeros_like(xv)
    return jnp.concatenate([jnp.where(head0, xv, z), jnp.where(head0, z, xv)], axis=0)


def _unstack_heads(x2, head0):
    qb = x2.shape[0] // 2
    return jnp.where(head0, x2[:qb], x2[qb:])


def _attn_fwd(qf, kf, proj, name):
    T = qf.shape[0]

    def body(q_ref, k_ref, v_ref, o_ref, lse_ref):
        for bi, dil in enumerate(DILATIONS):
            nb, qb, kw = _attn_geometry(T, dil)
            nit = nb * dil
            head0 = lax.broadcasted_iota(jnp.int32, (qb, 2 * HEAD_DIM), 1) < HEAD_DIM

            def step(j, carry, bi=bi, dil=dil, qb=qb, kw=kw, nit=nit, head0=head0):
                units = []
                for u in range(ATTN_UNROLL):
                    sq, sk, ok = _attn_block(_attn_unit(j, u, dil, nit), dil, qb, kw)
                    old = (_ld(o_ref, sq, qb, dil), _ld(lse_ref, sq, qb, dil)) if bi > 0 else None
                    units.append((sq, ok, _ld(q_ref, sq, qb, dil).astype(BF16), _ld(k_ref, sk, kw, dil).astype(BF16),
                                  _ld(v_ref, sk, kw, dil).astype(BF16), old))
                results = []
                for sq, ok, qv, kv, vv, old in units:
                    s = jnp.where(ok, _dot(_stack_heads(qv, head0), kv, _NT), NEG_INF)
                    m = jnp.max(s, axis=-1, keepdims=True)
                    p = jnp.exp(s - m).astype(BF16)
                    acc = _dot(p, jnp.concatenate([vv, jnp.ones_like(vv)], axis=1))
                    l = acc[:, 2 * HEAD_DIM:]
                    o_new = _unstack_heads(acc[:, :2 * HEAD_DIM] / l, head0)
                    l_new = _unstack_heads(m + jnp.log(l), head0)
                    if bi > 0:
                        o_old, l_old = old
                        mx = jnp.maximum(l_old, l_new)
                        e0, e1 = jnp.exp(l_old - mx), jnp.exp(l_new - mx)
                        z = e0 + e1
                        o_new = (e0 * o_old + e1 * o_new) / z
                        l_new = mx + jnp.log(z)
                    results.append((sq, o_new, l_new))
                for sq, o_new, l_new in results:
                    _st(o_ref, sq, qb, dil, o_new)
                    _st(lse_ref, sq, qb, dil, l_new)
                return carry

            lax.fori_loop(0, nit // ATTN_UNROLL, step, 0)

    blk = lambda off: pl.BlockSpec((T, 2 * HEAD_DIM), lambda hp, off=off: (0, off + hp))
    return _call(
        body, name=name, grid=(4,), in_specs=[blk(0), blk(0), blk(8)], out_specs=[blk(0), blk(0)],
        out_shape=[jax.ShapeDtypeStruct((T, ATTN_W), F32)] * 2, compiler_params=_params("parallel"),
    )(qf, kf, proj)


def _attn_bwd(qf, kf, proj, do, lse, delta, name):
    T = qf.shape[0]

    def body(q_ref, k_ref, v_ref, do_ref, lse_ref, dl_ref, dq_ref, dk_ref, dv_ref):
        for ref in (dq_ref, dk_ref, dv_ref):
            ref[...] = jnp.zeros_like(ref)
        for dil in DILATIONS:
            nb, qb, kw = _attn_geometry(T, dil)
            nit = nb * dil
            head0 = lax.broadcasted_iota(jnp.int32, (qb, 2 * HEAD_DIM), 1) < HEAD_DIM

            def step(j, carry, dil=dil, qb=qb, kw=kw, nit=nit, head0=head0):
                units = []
                for u in range(ATTN_UNROLL):
                    sq, sk, ok = _attn_block(_attn_unit(j, u, dil, nit), dil, qb, kw)
                    lsev, dlv = _ld(lse_ref, sq, qb, dil), _ld(dl_ref, sq, qb, dil)
                    units.append((sq, sk, ok, _ld(q_ref, sq, qb, dil).astype(BF16), _ld(do_ref, sq, qb, dil).astype(BF16),
                                  jnp.concatenate([lsev[:, 0:1], lsev[:, HEAD_DIM:HEAD_DIM + 1]], axis=0),
                                  jnp.concatenate([dlv[:, 0:1], dlv[:, HEAD_DIM:HEAD_DIM + 1]], axis=0),
                                  _ld(k_ref, sk, kw, dil).astype(BF16), _ld(v_ref, sk, kw, dil).astype(BF16),
                                  _ld(dq_ref, sq, qb, dil), _ld(dk_ref, sk, kw, dil), _ld(dv_ref, sk, kw, dil)))
                results = []
                for sq, sk, ok, qv, dov, lse2, dl2, kv, vv, dq0, dk0, dv0 in units:
                    q2, do2 = _stack_heads(qv, head0), _stack_heads(dov, head0)
                    p = jnp.where(ok, jnp.exp(_dot(q2, kv, _NT) - lse2), 0.0)
                    ds = (p * (_dot(do2, vv, _NT) - dl2)).astype(BF16)
                    results.append((sq, sk, dq0 + _unstack_heads(_dot(ds, kv), head0),
                                    dk0 + _dot(ds, q2, _TN), dv0 + _dot(p.astype(BF16), do2, _TN)))
                for sq, sk, dq, dk, dv in results:
                    _st(dq_ref, sq, qb, dil, dq)
                    _st(dk_ref, sk, kw, dil, dk)
                    _st(dv_ref, sk, kw, dil, dv)
                return carry

            lax.fori_loop(0, nit // ATTN_UNROLL, step, 0)

    blk = lambda off: pl.BlockSpec((T, 2 * HEAD_DIM), lambda hp, off=off: (0, off + hp))
    return _call(
        body, name=name, grid=(4,), in_specs=[blk(0), blk(0), blk(8), blk(0), blk(0), blk(0)], out_specs=[blk(0)] * 3,
        out_shape=[jax.ShapeDtypeStruct((T, ATTN_W), F32)] * 3, compiler_params=_params("parallel"),
    )(qf, kf, proj, do, lse, delta)


def _attn_norm(attn, g, name, tm=1024):
    T = attn.shape[0]

    def body(a_ref, g_ref, o_ref):
        av = a_ref[...]
        r = lax.rsqrt(jnp.mean(av * av, axis=-1, keepdims=True) + EPS)
        o_ref[...] = (av * r * g_ref[...]).astype(BF16)

    row = pl.BlockSpec((tm, ATTN_W), lambda i: (i, 0))
    return _call(
        body, name=name, grid=(T // tm,), in_specs=[row, pl.BlockSpec((1, ATTN_W), lambda i: (0, 0))], out_specs=row,
        out_shape=jax.ShapeDtypeStruct((T, 2 * ATTN_W), BF16), compiler_params=_params("parallel"),
    )(attn, g)


def _attn_norm_bwd(dmix, attn, g, bd, name, tm=1024):
    T = attn.shape[0]

    def body(d_ref, a_ref, g_ref, bd_ref, do_ref, dl_ref, dg_ref):
        @pl.when(pl.program_id(0) == 0)
        def _():
            dg_ref[...] = jnp.zeros_like(dg_ref)

        dy, av = d_ref[...], a_ref[...]
        r = lax.rsqrt(jnp.mean(av * av, axis=-1, keepdims=True) + EPS)
        gd = dy * g_ref[...]
        m = jnp.mean(gd * av, axis=-1, keepdims=True)
        da = r * gd - av * (r * r * r) * m
        do_ref[...] = da
        dl_ref[...] = _group_mean(da * av, bd_ref[...]) * float(HEAD_DIM)
        dg_ref[...] += jnp.sum(dy * av * r, axis=0, keepdims=True)

    row = pl.BlockSpec((tm, ATTN_W), lambda i: (i, 0))
    vec = pl.BlockSpec((1, ATTN_W), lambda i: (0, 0))
    return _call(
        body, name=name, grid=(T // tm,),
        in_specs=[row, row, vec, pl.BlockSpec((2 * HEAD_DIM, 2 * HEAD_DIM), lambda i: (0, 0))], out_specs=[row, row, vec],
        out_shape=[jax.ShapeDtypeStruct((T, ATTN_W), F32)] * 2 + [jax.ShapeDtypeStruct((1, ATTN_W), F32)],
        compiler_params=_params("arbitrary"),
    )(dmix, attn, g, bd)


def _rec_gates(xc, wrg_ref, wig_ref, brg_ref, big_ref, lam_ref):
    xb = xc.astype(BF16)
    r = _sigmoid(_dot(xb, wrg_ref[...]) + brg_ref[...])
    ig = _sigmoid(_dot(xb, wig_ref[...]) + big_ref[...])
    sp = _softplus_neg(lam_ref[...])
    log_a = -LRU_C * r * sp
    a = jnp.exp(log_a)
    th = jnp.tanh(log_a)
    mult = jnp.sqrt(-2.0 * th / (1.0 - th))
    return xb, r, ig, sp, a, mult


def _rec_fwd(proj, mix, cw, cb, wrg, wig, brg, big, lam, g, name, tm=512):
    T = proj.shape[0]
    hb = tm // 8

    def body(xr_ref, halo_ref, gr_ref, cw_ref, cb_ref, wrg_ref, wig_ref, brg_ref, big_ref, lam_ref, g_ref, mix_ref,
             xc_ref, h_ref, out_ref, carry):
        i = pl.program_id(0)

        @pl.when(i == 0)
        def _():
            carry[...] = jnp.zeros_like(carry)

        xr = xr_ref[...]
        halo = jnp.where(i > 0, halo_ref[...], 0.0)
        xc = cb_ref[...] + cw_ref[3:4, :] * xr
        for s in range(1, REC_CONV):
            xc = xc + cw_ref[3 - s:4 - s, :] * _shift_down(xr, halo, s)
        xc_ref[...] = xc
        _, _, ig, _, a, mult = _rec_gates(xc, wrg_ref, wig_ref, brg_ref, big_ref, lam_ref)
        pa, hl = _scan_fwd(a, mult * (ig * xc))
        h = hl + pa * carry[0:1, :]
        h_ref[...] = h
        carry[0:1, :] = h_ref[pl.ds(tm - 1, 1), :]
        hg = h * _gelu(gr_ref[...])
        r = lax.rsqrt(jnp.mean(hg * hg, axis=-1, keepdims=True) + EPS)
        out_ref[...] = (hg * r * g_ref[...]).astype(BF16)

    vec = pl.BlockSpec((1, REC_W), lambda i: (0, 0))
    row = pl.BlockSpec((tm, REC_W), lambda i: (i, 0))
    mat = pl.BlockSpec((REC_W, REC_W), lambda i: (0, 0))
    return _call(
        body, name=name, grid=(T // tm,),
        in_specs=[pl.BlockSpec((tm, REC_W), lambda i: (i, 3)),
                  pl.BlockSpec((8, REC_W), lambda i: (jnp.maximum(i * hb - 1, 0), 3)),
                  pl.BlockSpec((tm, REC_W), lambda i: (i, 4)),
                  pl.BlockSpec((8, REC_W), lambda i: (0, 0)), vec, mat, mat, vec, vec, vec, vec, ANY],
        out_specs=[row, row, pl.BlockSpec((tm, REC_W), lambda i: (i, 1))],
        out_shape=[jax.ShapeDtypeStruct((T, REC_W), F32)] * 2 + [jax.ShapeDtypeStruct(mix.shape, BF16)],
        scratch_shapes=[pltpu.VMEM((8, REC_W), F32)], input_output_aliases={11: 2},
        compiler_params=_params("arbitrary"),
    )(proj, proj, proj, cw, cb, wrg, wig, brg, big, lam, g, mix)


def _rec_bwd(dmix, proj, xc, h, cw, cb, wrg, wig, brg, big, lam, g, name, tm=512):
    T = proj.shape[0]
    nt = T // tm
    hb = tm // 8

    def body(d_ref, xr_ref, xhalo_ref, gr_ref, xc_ref, h_ref, hhalo_ref, cw_ref, cb_ref, wrg_ref, wig_ref, brg_ref,
             big_ref, lam_ref, g_ref,
             drec_ref, gcw_ref, gcb_ref, gwrg_ref, gwig_ref, gbrg_ref, gbig_ref, glam_ref, gg_ref,
             g_carry, a_first, dxc_next, gsp):
        i = pl.program_id(0)
        first_tile = i == nt - 1

        @pl.when(i == 0)
        def _():
            for ref in (gcw_ref, gcb_ref, gwrg_ref, gwig_ref, gbrg_ref, gbig_ref, glam_ref, gg_ref,
                        g_carry, a_first, dxc_next, gsp):
                ref[...] = jnp.zeros_like(ref)

        xr, xc, hv = xr_ref[...], xc_ref[...], h_ref[...]
        xhalo = jnp.where(first_tile, 0.0, xhalo_ref[...])
        hhalo = jnp.where(first_tile, 0.0, hhalo_ref[...])
        xb, r, ig, sp, a, mult = _rec_gates(xc, wrg_ref, wig_ref, brg_ref, big_ref, lam_ref)
        h_prev = _shift_down(hv, hhalo, 1)
        ge, dge = _gelu_and_grad(gr_ref[...])
        hg = hv * ge
        rr = lax.rsqrt(jnp.mean(hg * hg, axis=-1, keepdims=True) + EPS)
        dy = d_ref[...]
        gd = dy * g_ref[...]
        dhg = rr * gd - hg * (rr * rr * rr) * jnp.mean(gd * hg, axis=-1, keepdims=True)
        gg_ref[...] += jnp.sum(dy * hg * rr, axis=0, keepdims=True)
        dgr = (dhg * hv * dge).astype(BF16)
        dh = dhg * ge
        b = _shift_up(a, jnp.broadcast_to(a_first[0:1, :], (8, REC_W)), 1)
        pb, gl = _scan_bwd(b, dh)
        gs = gl + pb * g_carry[0:1, :]
        g_carry[0:1, :] = gs[0:1, :]
        a_first[0:1, :] = a[0:1, :]
        da = gs * h_prev
        dmult = gs * (ig * xc)
        di = gs * (mult * xc)
        dxc = gs * (mult * ig)
        dlog_a = da * a - dmult * (a * a) / mult
        gsp[...] += jnp.sum(dlog_a * (-LRU_C * r), axis=0, keepdims=True)
        dzr = (dlog_a * (-LRU_C * sp)) * (r * (1.0 - r))
        dzi = di * (ig * (1.0 - ig))
        dzr_b, dzi_b = dzr.astype(BF16), dzi.astype(BF16)
        dxc = dxc + _dot(dzr_b, wrg_ref[...], _NT) + _dot(dzi_b, wig_ref[...], _NT)
        gwrg_ref[...] += _dot(xb, dzr_b, _TN)
        gwig_ref[...] += _dot(xb, dzi_b, _TN)
        gbrg_ref[...] += jnp.sum(dzr, axis=0, keepdims=True)
        gbig_ref[...] += jnp.sum(dzi, axis=0, keepdims=True)
        nxt = dxc_next[...]
        dxr = cw_ref[3:4, :] * dxc
        gcw_ref[3:4, :] += jnp.sum(dxc * xr, axis=0, keepdims=True)
        for s in range(1, REC_CONV):
            dxr = dxr + cw_ref[3 - s:4 - s, :] * _shift_up(dxc, nxt, s)
            gcw_ref[3 - s:4 - s, :] += jnp.sum(dxc * _shift_down(xr, xhalo, s), axis=0, keepdims=True)
        gcb_ref[...] += jnp.sum(dxc, axis=0, keepdims=True)
        dxc_next[...] = dxc[:8]
        drec_ref[...] = jnp.concatenate([dxr.astype(BF16), dgr], axis=1)

        @pl.when(first_tile)
        def _():
            glam_ref[...] = gsp[...] * (-_sigmoid(-lam_ref[...]))

    rev = lambda i: nt - 1 - i
    vec = pl.BlockSpec((1, REC_W), lambda i: (0, 0))
    row = pl.BlockSpec((tm, REC_W), lambda i: (rev(i), 0))
    mat = pl.BlockSpec((REC_W, REC_W), lambda i: (0, 0))
    cwb = pl.BlockSpec((8, REC_W), lambda i: (0, 0))
    halo = lambda c: pl.BlockSpec((8, REC_W), lambda i, c=c: (jnp.maximum(rev(i) * hb - 1, 0), c))
    return _call(
        body, name=name, grid=(nt,),
        in_specs=[pl.BlockSpec((tm, REC_W), lambda i: (rev(i), 1)),
                  pl.BlockSpec((tm, REC_W), lambda i: (rev(i), 3)), halo(3),
                  pl.BlockSpec((tm, REC_W), lambda i: (rev(i), 4)),
                  row, row, halo(0), cwb, vec, mat, mat, vec, vec, vec, vec],
        out_specs=[pl.BlockSpec((tm, 2 * REC_W), lambda i: (rev(i), 0)), cwb, vec, mat, mat, vec, vec, vec, vec],
        out_shape=[jax.ShapeDtypeStruct((T, 2 * REC_W), BF16)]
        + [jax.ShapeDtypeStruct((8, REC_W), F32), jax.ShapeDtypeStruct((1, REC_W), F32)]
        + [jax.ShapeDtypeStruct((REC_W, REC_W), F32)] * 2 + [jax.ShapeDtypeStruct((1, REC_W), F32)] * 4,
        scratch_shapes=[pltpu.VMEM((8, REC_W), F32)] * 3 + [pltpu.VMEM((1, REC_W), F32)],
        compiler_params=_params("arbitrary"),
    )(dmix, proj, proj, proj, xc, h, h, cw, cb, wrg, wig, brg, big, lam, g)


def _ffn_conv(x_ext, cw_ref, cb_ref):
    return (cb_ref[...] + cw_ref[2:3, :] * x_ext + cw_ref[1:2, :] * pltpu.roll(x_ext, 1, 0)
            + cw_ref[0:1, :] * pltpu.roll(x_ext, 2, 0))


def _up_proj_act(x2, g, w_upT, cw, cb, name, tm=1024, tc=768):
    T = x2.shape[0]
    nc = D_FF // tc

    def body(x_ref, g_ref, wg_ref, wu_ref, cwg_ref, cwu_ref, cbg_ref, cbu_ref, act_ref, da_ref, db_ref, pg_ref, pu_ref,
             h_ref, hist_g, hist_u, hs):
        i, j = pl.program_id(0), pl.program_id(1)

        @pl.when(j == 0)
        def _():
            xv = x_ref[...]
            r = lax.rsqrt(jnp.mean(xv * xv, axis=-1, keepdims=True) + EPS)
            hs[...] = (xv * r * g_ref[...]).astype(BF16)
            h_ref[...] = hs[...]

        hv = hs[...]
        pg, pu = _dot(hv, wg_ref[...], _NT), _dot(hv, wu_ref[...], _NT)
        ge = jnp.concatenate([jnp.where(i > 0, hist_g[j], 0.0), pg], axis=0)
        ue = jnp.concatenate([jnp.where(i > 0, hist_u[j], 0.0), pu], axis=0)
        gel, dgel = _gelu_and_grad(_ffn_conv(ge, cwg_ref, cbg_ref)[8:])
        uu = _ffn_conv(ue, cwu_ref, cbu_ref)[8:]
        act_ref[...] = (gel * uu).astype(BF16)
        da_ref[...] = (uu * dgel).astype(BF16)
        db_ref[...] = gel.astype(BF16)
        pg_ref[...] = pg.astype(BF16)
        pu_ref[...] = pu.astype(BF16)
        hist_g[j] = pg[tm - 8:]
        hist_u[j] = pu[tm - 8:]

    tile = pl.BlockSpec((tm, tc), lambda i, j: (i, j))
    wsp = lambda off: pl.BlockSpec((tc, D_MODEL), lambda i, j, off=off: (j + off, 0))
    cws = lambda off: pl.BlockSpec((8, tc), lambda i, j, off=off: (0, j + off))
    cbs = lambda off: pl.BlockSpec((1, tc), lambda i, j, off=off: (0, j + off))
    return _call(
        body, name=name, grid=(T // tm, nc),
        in_specs=[pl.BlockSpec((tm, D_MODEL), lambda i, j: (i, 0)), pl.BlockSpec((1, D_MODEL), lambda i, j: (0, 0)),
                  wsp(0), wsp(nc), cws(0), cws(nc), cbs(0), cbs(nc)],
        out_specs=[tile] * 5 + [pl.BlockSpec((tm, D_MODEL), lambda i, j: (i, 0))],
        out_shape=[jax.ShapeDtypeStruct((T, D_FF), BF16)] * 5 + [jax.ShapeDtypeStruct((T, D_MODEL), BF16)],
        scratch_shapes=[pltpu.VMEM((nc, 8, tc), F32)] * 2 + [pltpu.VMEM((tm, D_MODEL), BF16)],
        compiler_params=_params("arbitrary", "arbitrary"),
    )(x2, g, w_upT, w_upT, cw, cw, cb, cb)


def _ffn_bwd(dyb, w_down, da, db, pg, pu, cw, name, tm=1024, tc=768):
    T, F = pg.shape
    nt = T // tm
    hb16 = tm // 16
    nc = F // tc
    n = tm + 8

    def body(dy_ref, dyn_ref, wd_ref, a_ref, an_ref, b_ref, bn_ref, g_ref, u_ref, cwg_ref, cwu_ref,
             dg_ref, du_ref, gcwg_ref, gcwu_ref, gcbg_ref, gcbu_ref):
        i = pl.program_id(1)
        last = i == nt - 1

        @pl.when(i == 0)
        def _():
            for ref in (gcwg_ref, gcwu_ref, gcbg_ref, gcbu_ref):
                ref[...] = jnp.zeros_like(ref)

        wd = wd_ref[...]
        dact_next = jnp.where(last, 0.0, _dot(dyn_ref[...], wd, _NT)[:8])
        de = jnp.concatenate([_dot(dy_ref[...], wd, _NT), dact_next], axis=0)
        ext = lambda t, nx: jnp.concatenate([t[...].astype(F32), nx[...].astype(F32)[:8]], axis=0)
        for dcv, x_ref, cw_ref, dx_ref, gcw_ref, gcb_ref in ((de * ext(a_ref, an_ref), g_ref, cwg_ref, dg_ref, gcwg_ref, gcbg_ref),
                                                               (de * ext(b_ref, bn_ref), u_ref, cwu_ref, du_ref, gcwu_ref, gcbu_ref)):
            s1, s2 = pltpu.roll(dcv, n - 1, 0), pltpu.roll(dcv, n - 2, 0)
            dx_ref[...] = (cw_ref[2:3, :] * dcv + cw_ref[1:2, :] * s1 + cw_ref[0:1, :] * s2)[:tm].astype(BF16)
            xv = x_ref[...].astype(F32)
            gcw_ref[2:3, :] += jnp.sum(xv * dcv[:tm], axis=0, keepdims=True)
            gcw_ref[1:2, :] += jnp.sum(xv * s1[:tm], axis=0, keepdims=True)
            gcw_ref[0:1, :] += jnp.sum(xv * s2[:tm], axis=0, keepdims=True)
            gcb_ref[...] += jnp.sum(dcv[:tm], axis=0, keepdims=True)

    tile = pl.BlockSpec((tm, tc), lambda j, i: (i, j))
    nxt = pl.BlockSpec((16, tc), lambda j, i: (jnp.minimum((i + 1) * hb16, nt * hb16 - 1), j))
    cws = lambda off: pl.BlockSpec((8, tc), lambda j, i, off=off: (0, j + off))
    cbs = pl.BlockSpec((1, tc), lambda j, i: (0, j))
    return _call(
        body, name=name, grid=(nc, nt),
        in_specs=[pl.BlockSpec((tm, D_MODEL), lambda j, i: (i, 0)),
                  pl.BlockSpec((16, D_MODEL), lambda j, i: (jnp.minimum((i + 1) * hb16, nt * hb16 - 1), 0)),
                  pl.BlockSpec((tc, D_MODEL), lambda j, i: (j, 0)), tile, nxt, tile, nxt, tile, tile, cws(0), cws(nc)],
        out_specs=[tile, tile, cws(0), cws(0), cbs, cbs],
        out_shape=[jax.ShapeDtypeStruct((T, F), BF16)] * 2 + [jax.ShapeDtypeStruct((8, F), F32)] * 2
        + [jax.ShapeDtypeStruct((1, F), F32)] * 2,
        compiler_params=_params("parallel", "arbitrary"),
    )(dyb, dyb, w_down, da, da, db, db, pg, pu, cw, cw)


def _adam_update(w, g, m, v):
    m2 = ADAM_B1 * m + (1.0 - ADAM_B1) * g
    v2 = ADAM_B2 * v + (1.0 - ADAM_B2) * (g * g)
    m_hat = m2 / (1.0 - ADAM_B1 ** ADAM_STEP)
    v_hat = v2 / (1.0 - ADAM_B2 ** ADAM_STEP)
    delta = -ADAM_LR * (m_hat / (jnp.sqrt(v_hat) + ADAM_EPS) + ADAM_WD * w)
    return delta, m2, v2


def _adam_sharded(p, r2, idx, w, m, v, name, transposed=False):
    r, n = p.shape[1:]
    nrecv = r2.shape[0]
    tr = (256 if r % 256 == 0 else r) if transposed else _row_tile(r)

    def body(c_ref, p_ref, r_ref, w_ref, m_ref, v_ref, g_ref, d_ref, m2_ref, v2_ref):
        g = p_ref[...].astype(F32)
        for k in range(nrecv):
            g = g + r_ref[k].astype(F32)
        if transposed:
            g = g.T
        g_ref[...] = g
        d_ref[...], m2_ref[...], v2_ref[...] = _adam_update(w_ref[...], g, m_ref[...], v_ref[...])

    blk = pl.BlockSpec((n, tr), lambda i, c_ref: (0, i)) if transposed else pl.BlockSpec((tr, n), lambda i, c_ref: (i, 0))
    spec = pltpu.PrefetchScalarGridSpec(
        num_scalar_prefetch=1, grid=(r // tr,),
        in_specs=[pl.BlockSpec((None, tr, n), lambda i, c_ref: (c_ref[0], i, 0)),
                  pl.BlockSpec((nrecv, tr, n), lambda i, c_ref: (0, i, 0)), blk, blk, blk],
        out_specs=[blk] * 4)
    return _call(body, name=name, grid_spec=spec, out_shape=[jax.ShapeDtypeStruct(w.shape, F32)] * 4,
                 compiler_params=_params("parallel"))(idx, p, r2, w, m, v)


def _sum_slabs(p, r2, idx, name):
    _, r, n = p.shape

    def body(c_ref, p_ref, r_ref, o_ref):
        acc = p_ref[...]
        for k in range(N_PEERS):
            acc = acc + r_ref[k]
        o_ref[...] = acc

    spec = pltpu.PrefetchScalarGridSpec(
        num_scalar_prefetch=1, grid=(1,),
        in_specs=[pl.BlockSpec((None, r, n), lambda i, c_ref: (c_ref[0], 0, 0)),
                  pl.BlockSpec((N_PEERS, r, n), lambda i, c_ref: (0, 0, 0))],
        out_specs=pl.BlockSpec((r, n), lambda i, c_ref: (0, 0)))
    return _call(body, name=name, grid_spec=spec, out_shape=jax.ShapeDtypeStruct((r, n), F32))(idx, p, r2)


def _adam_small(ws, gs, ms, vs, name):
    n = len(ws)

    def body(*refs):
        for i in range(n):
            d, m2, v2 = _adam_update(refs[i][...], refs[n + i][...], refs[2 * n + i][...], refs[3 * n + i][...])
            refs[4 * n + i][...] = d
            refs[5 * n + i][...] = m2
            refs[6 * n + i][...] = v2

    outs = _call(body, name=name, out_shape=[jax.ShapeDtypeStruct(w.shape, F32) for w in ws] * 3)(*ws, *gs, *ms, *vs)
    return outs[:n], outs[n:2 * n], outs[2 * n:]


def _pack_small_grads(full, halves, rcw, fcwg, fcwu, wrg, wig, lparts, name):
    nf, nh = len(full), len(halves)

    def body(*refs):
        o = refs[-1]
        o[...] = jnp.zeros_like(o)
        row = 0
        for r in refs[:nf]:
            for j in range(r.shape[1] // 1024):
                o[row:row + 1, :] = r[:, 1024 * j:1024 * (j + 1)]
                row += 1
        for k in range(0, nh, 2):
            o[row:row + 1, 0:512] = refs[nf + k][...]
            o[row:row + 1, 512:1024] = refs[nf + k + 1][...]
            row += 1
        rcw_ref, fg_ref, fu_ref, wrg_ref, wig_ref, l_ref = refs[nf + nh:nf + nh + 6]
        for k in range(2):
            o[row:row + 1, 0:512] = rcw_ref[2 * k:2 * k + 1, :]
            o[row:row + 1, 512:1024] = rcw_ref[2 * k + 1:2 * k + 2, :]
            row += 1
        for f_ref in (fg_ref, fu_ref):
            for k in range(FFN_CONV):
                for j in range(D_FF // 1024):
                    o[row:row + 1, :] = f_ref[k:k + 1, 1024 * j:1024 * (j + 1)]
                    row += 1
        assert row == 32
        for n in range(8):
            o[32:96, 64 * n:64 * n + 64] = wrg_ref[64 * n:64 * n + 64, 64 * n:64 * n + 64]
            o[32:96, 512 + 64 * n:512 + 64 * n + 64] = wig_ref[64 * n:64 * n + 64, 64 * n:64 * n + 64]
        o[96:97, :] = jnp.sum(l_ref[...], axis=0, keepdims=True)

    return _call(body, name=name, out_shape=jax.ShapeDtypeStruct((SMALL_ROWS, 1024), F32))(
        *full, *halves, rcw, fcwg, fcwu, wrg, wig, lparts)


def _block_diag(w):
    eye = jnp.eye(8, dtype=w.dtype)
    return (w[:, :, None, :] * eye[:, None, :, None]).reshape(512, 512)


def kernel(x, positions, g_mix, w_in, q_norm_g, k_norm_g, rec_conv_w, rec_conv_b, w_rg, b_rg, w_ig, b_ig, lru_lambda, g_attn_out, g_rec_out, w_out, g_ffn, w_up, ffn_conv_w, ffn_conv_b, w_down, loss_target, m_g_mix, m_w_in, m_q_norm_g, m_k_norm_g, m_rec_conv_w, m_rec_conv_b, m_w_rg, m_b_rg, m_w_ig, m_b_ig, m_lru_lambda, m_g_attn_out, m_g_rec_out, m_w_out, m_g_ffn, m_w_up, m_ffn_conv_w, m_ffn_conv_b, m_w_down, v_g_mix, v_w_in, v_q_norm_g, v_k_norm_g, v_rec_conv_w, v_rec_conv_b, v_w_rg, v_b_rg, v_w_ig, v_b_ig, v_lru_lambda, v_g_attn_out, v_g_rec_out, v_w_out, v_g_ffn, v_w_up, v_ffn_conv_w, v_ffn_conv_b, v_w_down):
    T = x.shape[1]
    ix, iy, ic = lax.axis_index("x"), lax.axis_index("y"), lax.axis_index("c")
    dev = 4 * ix + 2 * iy + ic
    xs = x.reshape(T, D_MODEL)
    tgt = loss_target.reshape(T, D_MODEL)
    pos = positions.reshape(T, 1)

    shards = {"w_in": (w_in[0], m_w_in[0], v_w_in[0]), "w_out": (w_out[0], m_w_out[0], v_w_out[0]),
              "w_up": (w_up[0], m_w_up[0], v_w_up[0]), "w_down": (w_down[0], m_w_down[0], v_w_down[0])}
    taps = jnp.concatenate([rec_conv_w.reshape(-1), ffn_conv_w.reshape(-1), jnp.zeros((4096 - 2560,), F32)]).reshape(8, 512)
    W_inT, taps_all = _all_gather([w_in[0].T.astype(BF16), taps], "ag_w_in")
    gather_landing = lambda s: _landing((N_DEV * s.shape[0], 1024), BF16, s, dev * s.shape[0])
    late = [w_out[0].astype(BF16), w_up[0].T.astype(BF16)]
    ag_send, ag_recv, late_thru, land_thru, ag_token = _exchange_start(
        late, [gather_landing(s) for s in late], "gather", taps_all, "ag_late_start")
    w_down_b = w_down[0].astype(BF16)
    down_landing = gather_landing(w_down_b)
    taps_all = taps_all.reshape(N_DEV, 4096)
    rcw = taps_all[:, :256].reshape(8, 4, 64).transpose(1, 0, 2).reshape(4, REC_W)
    fcw = taps_all[:, 256:2560].reshape(8, 3, 768).transpose(1, 0, 2).reshape(3, 2 * D_FF)
    rcw8 = jnp.pad(rcw, ((0, 4), (0, 0)))
    fcw8 = jnp.pad(fcw, ((0, 5), (0, 0)))
    fcb = ffn_conv_b.reshape(1, 2 * D_FF)

    half = HEAD_DIM // 2
    inv_freq = ROPE_THETA ** (-jnp.arange(half, dtype=F32) / half)
    invf = jnp.tile(inv_freq, 2 * N_HEADS).reshape(1, ATTN_W)
    bd = jnp.asarray(np.kron(np.eye(2), np.full((HEAD_DIM, HEAD_DIM), 1.0 / HEAD_DIM)), BF16)
    qg = jnp.tile(q_norm_g.reshape(HEAD_DIM), N_HEADS).reshape(1, ATTN_W)
    kg = jnp.tile(k_norm_g.reshape(HEAD_DIM), N_HEADS).reshape(1, ATTN_W)
    wrg_bd = _block_diag(w_rg[0]).astype(BF16)
    wig_bd = _block_diag(w_ig[0]).astype(BF16)
    brg, big = b_rg.reshape(1, REC_W), b_ig.reshape(1, REC_W)

    proj, h1 = _norm_proj(xs, g_mix + ag_token[0, 0], W_inT, "in_proj", tm=512, tn=IN_W)
    qf, kf = _qk_prep(proj, pos, invf, qg, kg, bd, "qk_prep")
    attn, lse = _attn_fwd(qf, kf, proj, "attn_fwd")
    dn_send, dn_recv, dn_thru, dn_land, dn_token = _exchange_start(
        [w_down_b], [down_landing], "gather", attn, "ag_down_start")
    mix = _attn_norm(attn, g_attn_out + dn_token[0, 0], "attn_norm")
    xc, hstate, mix = _rec_fwd(proj, mix, rcw8, rec_conv_b, wrg_bd, wig_bd, brg, big, lru_lambda, g_rec_out, "rec_fwd")
    _, (W_out, W_upT) = _exchange_wait(ag_send, ag_recv, late_thru, land_thru, "gather", hstate, "ag_late_wait")
    x2 = _mm(mix, W_out, "nn", F32, "out_proj", add=xs)

    act, da, db, pg, pu, h2 = _up_proj_act(x2, g_ffn, W_upT, fcw8, fcb, "up_proj_act")
    _, (W_down,) = _exchange_wait(dn_send, dn_recv, dn_thru, dn_land, "gather", h2, "ag_down_wait")
    dy, dyb, lparts = _mm(act, W_down, "nn", F32, "down_proj_loss", add=x2, loss_target=tgt, tm=512, tk=D_FF)

    g_down = _mm(act, dyb, "tn", BF16, "g_w_down", tk=4096)
    dpg, dpu, g_fcwg, g_fcwu, g_fcbg, g_fcbu = _ffn_bwd(dyb, W_down, da, db, pg, pu, fcw8, "ffn_bwd")
    g_upT = _mm(dpg, h2, "tn", BF16, "g_w_up_gate", tk=4096, o_rows=2 * D_FF)
    g_upT = _mm(dpu, h2, "tn", BF16, "g_w_up_up", tk=4096, into=g_upT, o_moff=D_FF // 1024)
    ffn_g = [g_upT.reshape(N_DEV, 2 * D_FF // N_DEV, 1024), g_down.reshape(N_DEV, D_FF // N_DEV, 1024)]
    rs_send, rs_recv, ffn_g, ffn_land, rs_token = _exchange_start(
        ffn_g, [_landing((N_PEERS,) + g.shape[1:], BF16) for g in ffn_g], "scatter", dpu, "rs_ffn_start")
    dx2, dx2b, g_gffn = _mm_norm_bwd([dpg, dpu], W_upT, x2, dy, g_ffn + rs_token[0, 0], "d_h2_norm_bwd", tm=1024, tk=1536)

    dmix = _mm(dx2b, W_out, "nt", F32, "d_mix")
    g_out = _mm(mix, dx2b, "tn", BF16, "g_w_out", tk=4096).reshape(N_DEV, D_MODEL // N_DEV, 1024)
    out_send, out_recv, (g_out,), out_land, out_token = _exchange_start(
        [g_out], [_landing((N_PEERS,) + g_out.shape[1:], BF16)], "scatter", dmix, "rs_out_start")
    do, delta, g_gattn = _attn_norm_bwd(dmix, attn, g_attn_out + out_token[0, 0], bd, "attn_norm_bwd")
    dqh, dkh, dv = _attn_bwd(qf, kf, proj, do, lse, delta, "attn_bwd")
    dqkv, g_qg, g_kg = _qk_prep_bwd(proj, dqh, dkh, dv, pos, invf, qg, kg, bd, "qk_prep_bwd")
    (drec, g_rcw, g_rcb, g_wrg, g_wig, g_brg, g_big, g_lam, g_grec) = _rec_bwd(
        dmix, proj, xc, hstate, rcw8, rec_conv_b, wrg_bd, wig_bd, brg, big, lru_lambda, g_rec_out, "rec_bwd")
    g_inT = _mm(dqkv, h1, "tn", BF16, "g_w_in_qkv", tm=512, tk=4096, o_rows=IN_W)
    g_inT = _mm(drec, h1, "tn", BF16, "g_w_in_rec", tm=512, tk=4096, into=g_inT, o_moff=3 * ATTN_W // 512)
    g_inT = g_inT.reshape(N_DEV, IN_W // N_DEV, 1024)
    in_send, in_recv, (g_inT,), in_land, in_token = _exchange_start(
        [g_inT], [_landing((N_PEERS,) + g_inT.shape[1:], BF16)], "scatter", drec, "rs_in_start")
    grad_x, _, g_gmix = _mm_norm_bwd([dqkv, drec], W_inT, xs, dx2, g_mix + in_token[0, 0], "d_h1_norm_bwd", tm=1024, tk=512)

    flat = _pack_small_grads([g_gmix, g_gffn, g_fcbg, g_fcbu], [g_rcb, g_brg, g_big, g_lam, g_gattn, g_grec, g_qg, g_kg],
                             g_rcw, g_fcwg, g_fcwu, g_wrg, g_wig, lparts.reshape(-1, D_MODEL), "pack_small_grads")
    srows = SMALL_ROWS // N_DEV
    flat = flat.reshape(N_DEV, srows, 1024)
    sm_send, sm_recv, (flat,), sm_land, sm_token = _exchange_start(
        [flat], [_landing((N_PEERS, srows, 1024), F32)], "scatter", grad_x, "ar_small_rs_start")

    devi = jnp.reshape(dev, (1,)).astype(jnp.int32)
    ffn_g, ffn_land = _exchange_wait(rs_send, rs_recv, ffn_g, ffn_land, "scatter", sm_token, "rs_ffn_wait")
    (g_out,), out_land = _exchange_wait(out_send, out_recv, [g_out], out_land, "scatter", sm_token, "rs_out_wait")
    big_out = {"grad": {}, "delta": {}, "new_m": {}, "new_v": {}}

    def adam_big(nm, p, r):
        w_, m_, v_ = shards[nm]
        res = _adam_sharded(p, r, devi, w_, m_, v_, "adam_" + nm, transposed=nm in ("w_in", "w_up"))
        for kind, a in zip(("grad", "delta", "new_m", "new_v"), res):
            big_out[kind][nm] = a[None]
        return res[0]

    last = adam_big("w_up", ffn_g[0], ffn_land[0])
    (flat,), sm_land = _exchange_wait(sm_send, sm_recv, [flat], sm_land, "scatter", last, "ar_small_rs_wait")
    mine = _sum_slabs(flat, sm_land[0], devi, "sum_small_grads")
    sm_send, sm_recv, (mine,), sm_land, sm_token = _exchange_start(
        [mine], [_landing((SMALL_ROWS, 1024), F32, mine, dev * srows)], "gather", last, "ar_small_ag_start")
    adam_big("w_down", ffn_g[1], ffn_land[1])
    last = adam_big("w_out", g_out, out_land[0])
    _, (tot,) = _exchange_wait(sm_send, sm_recv, [mine], sm_land, "gather", last, "ar_small_ag_wait")
    (g_inT,), in_land = _exchange_wait(in_send, in_recv, [g_inT], in_land, "scatter", tot, "rs_in_wait")
    adam_big("w_in", g_inT, in_land[0])

    half = lambda r, h, shape: tot[r, 512 * h:512 * h + 512].reshape(shape)
    blocks = lambda h: tot[32:96, 512 * h:512 * h + 512].reshape(64, 8, 64).transpose(1, 0, 2)[None]
    fcw_full = jnp.concatenate([tot[14:23].reshape(1, 3, D_FF), tot[23:32].reshape(1, 3, D_FF)], axis=2)
    g_small = {
        "g_mix": tot[0:1], "g_ffn": tot[1:2], "ffn_conv_b": tot[2:8].reshape(1, 2 * D_FF),
        "rec_conv_b": half(8, 0, (1, 512)), "b_rg": half(8, 1, (1, 8, 64)), "b_ig": half(9, 0, (1, 8, 64)),
        "lru_lambda": half(9, 1, (1, 512)), "g_attn_out": half(10, 0, (1, 512)), "g_rec_out": half(10, 1, (1, 512)),
        "q_norm_g": half(11, 0, (N_HEADS, HEAD_DIM)).sum(0)[None], "k_norm_g": half(11, 1, (N_HEADS, HEAD_DIM)).sum(0)[None],
        "w_rg": blocks(0), "w_ig": blocks(1),
        "rec_conv_w": lax.dynamic_slice(tot[12:14].reshape(1, 4, REC_W), (0, 0, 64 * dev), (1, 4, 64)),
        "ffn_conv_w": lax.dynamic_slice(fcw_full, (0, 0, 768 * dev), (1, 3, 768))}
    loss = 0.5 / D_MODEL * jnp.sum(tot[96])
    given = dict(rec_conv_w=rec_conv_w, ffn_conv_w=ffn_conv_w,g_mix=g_mix, q_norm_g=q_norm_g, k_norm_g=k_norm_g, rec_conv_b=rec_conv_b, w_rg=w_rg, b_rg=b_rg, w_ig=w_ig,
                 b_ig=b_ig, lru_lambda=lru_lambda, g_attn_out=g_attn_out, g_rec_out=g_rec_out, g_ffn=g_ffn, ffn_conv_b=ffn_conv_b)
    given_m = dict(rec_conv_w=m_rec_conv_w, ffn_conv_w=m_ffn_conv_w, g_mix=m_g_mix, q_norm_g=m_q_norm_g, k_norm_g=m_k_norm_g, rec_conv_b=m_rec_conv_b, w_rg=m_w_rg, b_rg=m_b_rg,
                   w_ig=m_w_ig, b_ig=m_b_ig, lru_lambda=m_lru_lambda, g_attn_out=m_g_attn_out, g_rec_out=m_g_rec_out,
                   g_ffn=m_g_ffn, ffn_conv_b=m_ffn_conv_b)
    given_v = dict(rec_conv_w=v_rec_conv_w, ffn_conv_w=v_ffn_conv_w, g_mix=v_g_mix, q_norm_g=v_q_norm_g, k_norm_g=v_k_norm_g, rec_conv_b=v_rec_conv_b, w_rg=v_w_rg, b_rg=v_b_rg,
                   w_ig=v_w_ig, b_ig=v_b_ig, lru_lambda=v_lru_lambda, g_attn_out=v_g_attn_out, g_rec_out=v_g_rec_out,
                   g_ffn=v_g_ffn, ffn_conv_b=v_ffn_conv_b)
    small = sorted(given)
    ds, m2s, v2s = _adam_small([given[k] for k in small], [g_small[k] for k in small], [given_m[k] for k in small],
                               [given_v[k] for k in small], "adam_small")
    small_out = {"grad": g_small, "delta": dict(zip(small, ds)), "new_m": dict(zip(small, m2s)), "new_v": dict(zip(small, v2s))}

    order = ("g_mix", "w_in", "q_norm_g", "k_norm_g", "rec_conv_w", "rec_conv_b", "w_rg", "b_rg", "w_ig", "b_ig",
             "lru_lambda", "g_attn_out", "g_rec_out", "w_out", "g_ffn", "w_up", "ffn_conv_w", "ffn_conv_b", "w_down")
    outs = [loss, grad_x.reshape(1, T, D_MODEL)]
    for kind in ("grad", "delta", "new_m", "new_v"):
        for name in order:
            outs.append(big_out[kind][name] if name in big_out[kind] else small_out[kind][name])
    return tuple(outs)
```
